```python
import math
import jax, jax.numpy as jnp
from jax import lax
import numpy as np

D_MODEL = 1024
BATCH = 8
SEQ = 8192
DEPTH = 1

D_MIX = D_MODEL
MLA_HEADS = 8
MLA_NOPE = 64
MLA_ROPE = 32
MLA_V = 64
MLA_WIDTH = MLA_HEADS * MLA_V
Q_LORA = 256
KV_LORA = 128
ROPE_THETA = 10000.0
Q_BLOCK = 128
CHUNK = 128
G_HEADS = 8
G_WIDTH = D_MIX - MLA_WIDTH
G_HEAD_DIM = G_WIDTH // G_HEADS
D_IN = Q_LORA + KV_LORA + MLA_ROPE + MLA_WIDTH + 3 * G_WIDTH
DN_ALPHA = (2.0 * DEPTH) ** 0.25
DN_BETA = (8.0 * DEPTH) ** -0.25
EPS = 1e-5

kernel_name = "hybrid_mla_gmlp_parallel_deepnorm"


def _rmsnorm(x, g):
    xf = x.astype(jnp.float32)
    y = xf * lax.rsqrt(jnp.mean(xf * xf, axis=-1, keepdims=True) + EPS)
    return (y * g.astype(jnp.float32)).astype(x.dtype)


def _layernorm(x, g, b):
    xf = x.astype(jnp.float32)
    mu = jnp.mean(xf, axis=-1, keepdims=True)
    var = jnp.mean(jnp.square(xf - mu), axis=-1, keepdims=True)
    y = (xf - mu) * lax.rsqrt(var + EPS)
    return (y * g.astype(jnp.float32) + b.astype(jnp.float32)).astype(x.dtype)


def _rope(t, positions):
    half = MLA_ROPE // 2
    inv_freq = 1.0 / (ROPE_THETA ** (jnp.arange(half, dtype=jnp.float32) / half))
    ang = positions.astype(jnp.float32)[..., None] * inv_freq
    cos = jnp.cos(ang)[:, :, None, :].astype(t.dtype)
    sin = jnp.sin(ang)[:, :, None, :].astype(t.dtype)
    t1, t2 = t[..., :half], t[..., half:]
    return jnp.concatenate([t1 * cos - t2 * sin, t1 * sin + t2 * cos], axis=-1)


def _causal_attention(q, k, v):
    b, s, h, dqk = q.shape
    dv = v.shape[-1]
    nb = s // Q_BLOCK
    scale = 1.0 / math.sqrt(dqk)
    qb = q.reshape(b, nb, Q_BLOCK, h, dqk).transpose(1, 0, 2, 3, 4)
    kpos = jnp.arange(s)

    def one_block(args):
        qi, i = args
        sc = jnp.einsum('bqhd,bkhd->bhqk', qi, k).astype(jnp.float32) * scale
        qpos = i * Q_BLOCK + jnp.arange(Q_BLOCK)
        mask = kpos[None, :] <= qpos[:, None]
        sc = jnp.where(mask[None, None], sc, -jnp.inf)
        p = jax.nn.softmax(sc, axis=-1).astype(v.dtype)
        return jnp.einsum('bhqk,bkhd->bqhd', p, v)

    o = lax.map(one_block, (qb, jnp.arange(nb)))
    return o.transpose(1, 0, 2, 3, 4).reshape(b, s, h, dv)


def _fwd_setup_inputs(seed: int = 0) -> dict:
    key = jax.random.key(seed)
    ks = jax.random.split(key, 16)
    f32 = jnp.float32
    x = jax.random.normal(ks[0], (BATCH, SEQ, D_MODEL), f32)
    positions = jnp.broadcast_to(jnp.arange(SEQ, dtype=jnp.int32)[None, :], (BATCH, SEQ))
    w_in = jax.random.normal(ks[1], (D_MODEL, D_IN), f32) * D_MODEL ** -0.5
    q_norm_g = 1.0 + 0.02 * jax.random.normal(ks[2], (Q_LORA,), f32)
    w_uq = jax.random.normal(ks[3], (Q_LORA, MLA_HEADS * (MLA_NOPE + MLA_ROPE)), f32) * Q_LORA ** -0.5
    kv_norm_g = 1.0 + 0.02 * jax.random.normal(ks[4], (KV_LORA,), f32)
    w_ukv = jax.random.normal(ks[5], (KV_LORA, MLA_HEADS * (MLA_NOPE + MLA_V)), f32) * KV_LORA ** -0.5
    sgu_norm_g = 1.0 + 0.02 * jax.random.normal(ks[6], (G_WIDTH,), f32)
    sgu_norm_b = 0.02 * jax.random.normal(ks[7], (G_WIDTH,), f32)
    w_spatial = jax.random.normal(ks[8], (G_HEADS, CHUNK, CHUNK), f32) * CHUNK ** -0.5
    b_spatial = 1.0 + 0.02 * jax.random.normal(ks[9], (G_HEADS, CHUNK), f32)
    w_out = jax.random.normal(ks[10], (D_MIX, D_MODEL), f32) * (D_MIX ** -0.5) * DN_BETA
    ln_g = 1.0 + 0.02 * jax.random.normal(ks[11], (D_MODEL,), f32)
    ln_b = 0.02 * jax.random.normal(ks[12], (D_MODEL,), f32)
    return {"x": x, "positions": positions, "w_in": w_in, "q_norm_g": q_norm_g,
            "w_uq": w_uq, "kv_norm_g": kv_norm_g, "w_ukv": w_ukv,
            "sgu_norm_g": sgu_norm_g, "sgu_norm_b": sgu_norm_b,
            "w_spatial": w_spatial, "b_spatial": b_spatial, "w_out": w_out,
            "ln_g": ln_g, "ln_b": ln_b}


def _hybrid_mixer(h, positions, w_in, q_norm_g, w_uq, kv_norm_g, w_ukv,
                  sgu_norm_g, sgu_norm_b, w_spatial, b_spatial, w_out):
    b, s, _ = h.shape
    proj = jnp.einsum('bsd,de->bse', h, w_in)
    splits = np.cumsum([Q_LORA, KV_LORA, MLA_ROPE, MLA_WIDTH, G_WIDTH, G_WIDTH]).tolist()
    c_q, c_kv, k_rope, z_a, u, v = jnp.split(proj, splits, axis=-1)[:6]
    z_b = proj[..., splits[-1]:]

    q = jnp.einsum('bsr,re->bse', _rmsnorm(c_q, q_norm_g), w_uq)
    q = q.reshape(b, s, MLA_HEADS, MLA_NOPE + MLA_ROPE)
    q_nope, q_rope = q[..., :MLA_NOPE], _rope(q[..., MLA_NOPE:], positions)
    kv = jnp.einsum('bsr,re->bse', _rmsnorm(c_kv, kv_norm_g), w_ukv)
    kv = kv.reshape(b, s, MLA_HEADS, MLA_NOPE + MLA_V)
    k_nope, val = kv[..., :MLA_NOPE], kv[..., MLA_NOPE:]
    k_r = jnp.broadcast_to(_rope(k_rope[:, :, None, :], positions), (b, s, MLA_HEADS, MLA_ROPE))
    qf = jnp.concatenate([q_nope, q_rope], axis=-1)
    kf = jnp.concatenate([k_nope, k_r], axis=-1)
    attn = _causal_attention(qf, kf, val).reshape(b, s, MLA_WIDTH)
    out_a = attn * jax.nn.silu(z_a)

    u = jax.nn.gelu(u, approximate=False)
    v = _layernorm(jax.nn.gelu(v, approximate=False), sgu_norm_g, sgu_norm_b)
    nc = s // CHUNK
    vc = v.reshape(b, nc, CHUNK, G_HEADS, G_HEAD_DIM)
    causal = jnp.tril(jnp.ones((CHUNK, CHUNK), dtype=bool))
    w_s = jnp.where(causal[None], w_spatial, 0.0).astype(v.dtype)
    sv = jnp.einsum('hts,bcshd->bcthd', w_s, vc) + b_spatial.T[None, None, :, :, None]
    sgu = u * sv.reshape(b, s, G_WIDTH)
    out_b = sgu * jax.nn.silu(z_b)

    merged = jnp.concatenate([out_a, out_b], axis=-1)
    return jnp.einsum('bse,ed->bsd', merged, w_out)


def _fwd_reference(x, positions, w_in, q_norm_g, w_uq, kv_norm_g, w_ukv,
              sgu_norm_g, sgu_norm_b, w_spatial, b_spatial, w_out, ln_g, ln_b):
    h = x
    for _ in range(DEPTH):
        y = _hybrid_mixer(h, positions, w_in, q_norm_g, w_uq, kv_norm_g, w_ukv,
                          sgu_norm_g, sgu_norm_b, w_spatial, b_spatial, w_out)
        h = _layernorm(DN_ALPHA * h + y, ln_g, ln_b)
    return h


import jax as _jax
import jax.numpy as _jnp

TWIN_FORMAT = 'train_step'
FWD_PARAMS = ['x', 'positions', 'w_in', 'q_norm_g', 'w_uq', 'kv_norm_g', 'w_ukv', 'sgu_norm_g', 'sgu_norm_b', 'w_spatial', 'b_spatial', 'w_out', 'ln_g', 'ln_b']
TWIN_WEIGHTS = ['w_in', 'q_norm_g', 'w_uq', 'kv_norm_g', 'w_ukv', 'sgu_norm_g', 'sgu_norm_b', 'w_spatial', 'b_spatial', 'w_out', 'ln_g', 'ln_b']
TWIN_DIFF_INPUT = 'x'
TWIN_INPUTS = ['x', 'positions', 'w_in', 'q_norm_g', 'w_uq', 'kv_norm_g', 'w_ukv', 'sgu_norm_g', 'sgu_norm_b', 'w_spatial', 'b_spatial', 'w_out', 'ln_g', 'ln_b', 'loss_target', 'm_w_in', 'm_q_norm_g', 'm_w_uq', 'm_kv_norm_g', 'm_w_ukv', 'm_sgu_norm_g', 'm_sgu_norm_b', 'm_w_spatial', 'm_b_spatial', 'm_w_out', 'm_ln_g', 'm_ln_b', 'v_w_in', 'v_q_norm_g', 'v_w_uq', 'v_kv_norm_g', 'v_w_ukv', 'v_sgu_norm_g', 'v_sgu_norm_b', 'v_w_spatial', 'v_b_spatial', 'v_w_out', 'v_ln_g', 'v_ln_b']
TWIN_OUTPUTS = ['loss', 'grad_x', 'grad_w_in', 'grad_q_norm_g', 'grad_w_uq', 'grad_kv_norm_g', 'grad_w_ukv', 'grad_sgu_norm_g', 'grad_sgu_norm_b', 'grad_w_spatial', 'grad_b_spatial', 'grad_w_out', 'grad_ln_g', 'grad_ln_b', 'delta_w_in', 'delta_q_norm_g', 'delta_w_uq', 'delta_kv_norm_g', 'delta_w_ukv', 'delta_sgu_norm_g', 'delta_sgu_norm_b', 'delta_w_spatial', 'delta_b_spatial', 'delta_w_out', 'delta_ln_g', 'delta_ln_b', 'new_m_w_in', 'new_m_q_norm_g', 'new_m_w_uq', 'new_m_kv_norm_g', 'new_m_w_ukv', 'new_m_sgu_norm_g', 'new_m_sgu_norm_b', 'new_m_w_spatial', 'new_m_b_spatial', 'new_m_w_out', 'new_m_ln_g', 'new_m_ln_b', 'new_v_w_in', 'new_v_q_norm_g', 'new_v_w_uq', 'new_v_kv_norm_g', 'new_v_w_ukv', 'new_v_sgu_norm_g', 'new_v_sgu_norm_b', 'new_v_w_spatial', 'new_v_b_spatial', 'new_v_w_out', 'new_v_ln_g', 'new_v_ln_b']
TWIN_LEAF_KINDS = {'loss': 'loss', 'grad_x': 'grad_x', 'grad_w_in': 'grad_w', 'grad_q_norm_g': 'grad_w', 'grad_w_uq': 'grad_w', 'grad_kv_norm_g': 'grad_w', 'grad_w_ukv': 'grad_w', 'grad_sgu_norm_g': 'grad_w', 'grad_sgu_norm_b': 'grad_w', 'grad_w_spatial': 'grad_w', 'grad_b_spatial': 'grad_w', 'grad_w_out': 'grad_w', 'grad_ln_g': 'grad_w', 'grad_ln_b': 'grad_w', 'delta_w_in': 'delta_w', 'delta_q_norm_g': 'delta_w', 'delta_w_uq': 'delta_w', 'delta_kv_norm_g': 'delta_w', 'delta_w_ukv': 'delta_w', 'delta_sgu_norm_g': 'delta_w', 'delta_sgu_norm_b': 'delta_w', 'delta_w_spatial': 'delta_w', 'delta_b_spatial': 'delta_w', 'delta_w_out': 'delta_w', 'delta_ln_g': 'delta_w', 'delta_ln_b': 'delta_w', 'new_m_w_in': 'new_m', 'new_m_q_norm_g': 'new_m', 'new_m_w_uq': 'new_m', 'new_m_kv_norm_g': 'new_m', 'new_m_w_ukv': 'new_m', 'new_m_sgu_norm_g': 'new_m', 'new_m_sgu_norm_b': 'new_m', 'new_m_w_spatial': 'new_m', 'new_m_b_spatial': 'new_m', 'new_m_w_out': 'new_m', 'new_m_ln_g': 'new_m', 'new_m_ln_b': 'new_m', 'new_v_w_in': 'new_v', 'new_v_q_norm_g': 'new_v', 'new_v_w_uq': 'new_v', 'new_v_kv_norm_g': 'new_v', 'new_v_w_ukv': 'new_v', 'new_v_sgu_norm_g': 'new_v', 'new_v_sgu_norm_b': 'new_v', 'new_v_w_spatial': 'new_v', 'new_v_b_spatial': 'new_v', 'new_v_w_out': 'new_v', 'new_v_ln_g': 'new_v', 'new_v_ln_b': 'new_v'}


def _forward(args):
    return _fwd_reference(*[args[k] for k in FWD_PARAMS])


def _output_shape():
    def fwd():
        inp = _fwd_setup_inputs(0)
        return _fwd_reference(*[inp[k] for k in FWD_PARAMS])
    out = _jax.eval_shape(fwd)
    return out.shape, out.dtype

N_MICROBATCH = 1
ADAM_LR = 0.001
ADAM_B1 = 0.9
ADAM_B2 = 0.999
ADAM_EPS = 1e-08
ADAM_WD = 0.01
ADAM_STEP = 10
PER_EXAMPLE_BATCH_AXIS = {'x': 0, 'positions': 0, 'loss_target': 0}
SHARED_INPUTS = []
_WEIGHT_DTYPES = {'w_in': _jnp.float32, 'q_norm_g': _jnp.float32, 'w_uq': _jnp.float32, 'kv_norm_g': _jnp.float32, 'w_ukv': _jnp.float32, 'sgu_norm_g': _jnp.float32, 'sgu_norm_b': _jnp.float32, 'w_spatial': _jnp.float32, 'b_spatial': _jnp.float32, 'w_out': _jnp.float32, 'ln_g': _jnp.float32, 'ln_b': _jnp.float32}
MOMENT_SCALE = {'w_in': 4.498701e-02, 'q_norm_g': 1.852776e-02, 'w_uq': 1.129073e-02, 'kv_norm_g': 4.771848e-02, 'w_ukv': 1.440992e-02, 'sgu_norm_g': 3.696424e-02, 'sgu_norm_b': 3.383369e-02, 'w_spatial': 2.406897e-02, 'b_spatial': 3.466188e-02, 'w_out': 7.291079e-02, 'ln_g': 6.402326e+01, 'ln_b': 1.433255e+00}


def _to_microbatches(a, axis):
    t = _jnp.moveaxis(a, axis, 0)
    t = t.reshape((N_MICROBATCH, t.shape[0] // N_MICROBATCH) + t.shape[1:])
    return _jnp.moveaxis(t, 1, axis + 1)


def setup_inputs(seed: int = 0) -> dict:
    inp = _fwd_setup_inputs(seed)
    key = _jax.random.fold_in(_jax.random.key(seed), 7919)
    shape, _ = _output_shape()
    out = dict(inp)
    out["loss_target"] = _jax.random.normal(_jax.random.fold_in(key, 0), shape, _jnp.float32)
    for i, name in enumerate(TWIN_WEIGHTS):
        w = inp[name].astype(_jnp.float32)
        if MOMENT_SCALE is None:
            s = _jnp.sqrt(_jnp.mean(_jnp.square(w)) + 1e-30)
        else:
            s = MOMENT_SCALE[name]
        km, kv = _jax.random.split(_jax.random.fold_in(key, i + 1))
        out[name] = w
        out["m_" + name] = s * _jax.random.normal(km, w.shape, _jnp.float32)
        out["v_" + name] = (s * s) * _jax.random.uniform(kv, w.shape, _jnp.float32, 0.5, 1.5)
    if N_MICROBATCH > 1:
        for name, axis in PER_EXAMPLE_BATCH_AXIS.items():
            out[name] = _to_microbatches(out[name], axis)
    return {'x': out['x'], 'positions': out['positions'], 'w_in': out['w_in'], 'q_norm_g': out['q_norm_g'], 'w_uq': out['w_uq'], 'kv_norm_g': out['kv_norm_g'], 'w_ukv': out['w_ukv'], 'sgu_norm_g': out['sgu_norm_g'], 'sgu_norm_b': out['sgu_norm_b'], 'w_spatial': out['w_spatial'], 'b_spatial': out['b_spatial'], 'w_out': out['w_out'], 'ln_g': out['ln_g'], 'ln_b': out['ln_b'], 'loss_target': out['loss_target'], 'm_w_in': out['m_w_in'], 'm_q_norm_g': out['m_q_norm_g'], 'm_w_uq': out['m_w_uq'], 'm_kv_norm_g': out['m_kv_norm_g'], 'm_w_ukv': out['m_w_ukv'], 'm_sgu_norm_g': out['m_sgu_norm_g'], 'm_sgu_norm_b': out['m_sgu_norm_b'], 'm_w_spatial': out['m_w_spatial'], 'm_b_spatial': out['m_b_spatial'], 'm_w_out': out['m_w_out'], 'm_ln_g': out['m_ln_g'], 'm_ln_b': out['m_ln_b'], 'v_w_in': out['v_w_in'], 'v_q_norm_g': out['v_q_norm_g'], 'v_w_uq': out['v_w_uq'], 'v_kv_norm_g': out['v_kv_norm_g'], 'v_w_ukv': out['v_w_ukv'], 'v_sgu_norm_g': out['v_sgu_norm_g'], 'v_sgu_norm_b': out['v_sgu_norm_b'], 'v_w_spatial': out['v_w_spatial'], 'v_b_spatial': out['v_b_spatial'], 'v_w_out': out['v_w_out'], 'v_ln_g': out['v_ln_g'], 'v_ln_b': out['v_ln_b']}


def _loss(weights, diff, rest, loss_target):
    with _jax.named_scope("forward"):
        args = {**rest, TWIN_DIFF_INPUT: diff, **{k: w.astype(_WEIGHT_DTYPES[k]) for k, w in weights.items()}}
        y = _forward(args)
    with _jax.named_scope("loss_head"):
        err = _jnp.square(y.astype(_jnp.float32) - loss_target)
        return 0.5 * _jnp.sum(_jnp.mean(err, axis=-1)) if err.ndim else 0.5 * err


def _adamw(w, g, m, v):
    m = ADAM_B1 * m + (1.0 - ADAM_B1) * g
    v = ADAM_B2 * v + (1.0 - ADAM_B2) * _jnp.square(g)
    m_hat = m / (1.0 - ADAM_B1 ** ADAM_STEP)
    v_hat = v / (1.0 - ADAM_B2 ** ADAM_STEP)
    delta = -ADAM_LR * (m_hat / (_jnp.sqrt(v_hat) + ADAM_EPS) + ADAM_WD * w)
    return delta, m, v


def reference(x, positions, w_in, q_norm_g, w_uq, kv_norm_g, w_ukv, sgu_norm_g, sgu_norm_b, w_spatial, b_spatial, w_out, ln_g, ln_b, loss_target, m_w_in, m_q_norm_g, m_w_uq, m_kv_norm_g, m_w_ukv, m_sgu_norm_g, m_sgu_norm_b, m_w_spatial, m_b_spatial, m_w_out, m_ln_g, m_ln_b, v_w_in, v_q_norm_g, v_w_uq, v_kv_norm_g, v_w_ukv, v_sgu_norm_g, v_sgu_norm_b, v_w_spatial, v_b_spatial, v_w_out, v_ln_g, v_ln_b):
    given = dict(x=x, positions=positions, w_in=w_in, q_norm_g=q_norm_g, w_uq=w_uq, kv_norm_g=kv_norm_g, w_ukv=w_ukv, sgu_norm_g=sgu_norm_g, sgu_norm_b=sgu_norm_b, w_spatial=w_spatial, b_spatial=b_spatial, w_out=w_out, ln_g=ln_g, ln_b=ln_b, loss_target=loss_target, m_w_in=m_w_in, m_q_norm_g=m_q_norm_g, m_w_uq=m_w_uq, m_kv_norm_g=m_kv_norm_g, m_w_ukv=m_w_ukv, m_sgu_norm_g=m_sgu_norm_g, m_sgu_norm_b=m_sgu_norm_b, m_w_spatial=m_w_spatial, m_b_spatial=m_b_spatial, m_w_out=m_w_out, m_ln_g=m_ln_g, m_ln_b=m_ln_b, v_w_in=v_w_in, v_q_norm_g=v_q_norm_g, v_w_uq=v_w_uq, v_kv_norm_g=v_kv_norm_g, v_w_ukv=v_w_ukv, v_sgu_norm_g=v_sgu_norm_g, v_sgu_norm_b=v_sgu_norm_b, v_w_spatial=v_w_spatial, v_b_spatial=v_b_spatial, v_w_out=v_w_out, v_ln_g=v_ln_g, v_ln_b=v_ln_b)
    weights = {n: given[n] for n in TWIN_WEIGHTS}
    shared = {n: given[n] for n in SHARED_INPUTS}
    per_example = {n: given[n] for n in ['x', 'positions']}
    grad_fn = _jax.value_and_grad(_loss, argnums=(0, 1))

    def one_microbatch(ex, loss_target):
        ex = dict(ex)
        diff = ex.pop(TWIN_DIFF_INPUT)
        return grad_fn(weights, diff, {**shared, **ex}, loss_target)

    if N_MICROBATCH == 1:
        loss, (grad_w, grad_x) = one_microbatch(per_example, given["loss_target"])
    else:
        def body(carry, xs):
            loss_sum, grad_sum = carry
            l_k, (gw_k, gx_k) = one_microbatch(xs[0], xs[1])
            with _jax.named_scope("update"):
                return (loss_sum + l_k, _jax.tree.map(_jnp.add, grad_sum, gw_k)), gx_k

        init = (_jnp.zeros((), _jnp.float32), _jax.tree.map(_jnp.zeros_like, weights))
        (loss, grad_w), grad_x = _jax.lax.scan(body, init, (per_example, given["loss_target"]))
    with _jax.named_scope("update"):
        delta_w, new_m, new_v = {}, {}, {}
        for n in TWIN_WEIGHTS:
            delta_w[n], new_m[n], new_v[n] = _adamw(weights[n], grad_w[n], given["m_" + n], given["v_" + n])
    return (loss, grad_x, *[grad_w[n] for n in TWIN_WEIGHTS], *[delta_w[n] for n in TWIN_WEIGHTS],
            *[new_m[n] for n in TWIN_WEIGHTS], *[new_v[n] for n in TWIN_WEIGHTS])
```

```python
import functools
import math

import jax
import jax.numpy as jnp
from jax import lax
from jax.experimental import pallas as pl
from jax.experimental.pallas import tpu as pltpu

F32 = jnp.float32
BF16 = jnp.bfloat16

N_DEV = 8
D_MODEL = 1024
HEADS = 8
NOPE = 64
ROPE = 32
HALF = ROPE // 2
VDIM = 64
Q_LORA = 256
KV_LORA = 128
G_WIDTH = 512
G_HEAD_DIM = 64
CHUNK = 128
HEAD_PAD = 128
D_IN = 2464
D_IN_PAD = 2560
KR_LO = NOPE
ROPE_THETA = 10000.0
DN_ALPHA = 2.0 ** 0.25
EPS = 1e-5
ATTN_SCALE = 1.0 / math.sqrt(NOPE + ROPE)
ADAM_LR, ADAM_B1, ADAM_B2, ADAM_EPS, ADAM_WD, ADAM_STEP = 0.001, 0.9, 0.999, 1e-08, 0.01, 10

LANES = 128
SLAB_ROWS = 4096
REP_ROWS = 136
SMALL_LEN = 8192
VMEM_LIMIT = 56 * 1024 * 1024

TOKEN_TILE = 256
ATTN_BLOCK = 512


def _cparams(sem=None):
    return pltpu.CompilerParams(dimension_semantics=sem, vmem_limit_bytes=VMEM_LIMIT)


def _dot(a, b):
    return jnp.dot(a, b, preferred_element_type=F32)


def _dot_nt(a, b):
    return lax.dot_general(a, b, (((1,), (1,)), ((), ())), preferred_element_type=F32)


def _dot_tn(a, b):
    return lax.dot_general(a, b, (((0,), (0,)), ((), ())), preferred_element_type=F32)


def _sigmoid(z):
    return 1.0 / (1.0 + jnp.exp(-z))


def _gelu(x):
    return 0.5 * x * (1.0 + lax.erf(x * 0.7071067811865476))


def _gelu_grad(x):
    cdf = 0.5 * (1.0 + lax.erf(x * 0.7071067811865476))
    return cdf + x * jnp.exp(-0.5 * x * x) * 0.3989422804014327


def _exchange(src, *, name, per_destination):
    rows = src.shape[-2]

    def body(src_ref, out_ref, send_sems, recv_sems, local_sem):
        x, y, c = lax.axis_index("x"), lax.axis_index("y"), lax.axis_index("c")
        me = 4 * x + 2 * y + c

        def slab_for(dest):
            return src_ref.at[dest] if per_destination else src_ref

        mine = pltpu.make_async_copy(slab_for(me), out_ref.at[me], local_sem)
        mine.start()
        copies = []
        for k in (6, 7, 4, 5, 2, 3, 1):
            px = 1 - x if k & 4 else x
            py = 1 - y if k & 2 else y
            pc = 1 - c if k & 1 else c
            peer = 4 * px + 2 * py + pc
            cp = pltpu.make_async_remote_copy(
                src_ref=slab_for(peer), dst_ref=out_ref.at[me],
                send_sem=send_sems.at[k - 1], recv_sem=recv_sems.at[k - 1],
                device_id=(px, py, pc), device_id_type=pl.DeviceIdType.MESH)
            cp.start()
            copies.append((k, peer, cp))
        for k, peer, cp in copies:
            pltpu.make_async_remote_copy(
                src_ref=slab_for(peer), dst_ref=out_ref.at[peer],
                send_sem=send_sems.at[k - 1], recv_sem=recv_sems.at[k - 1],
                device_id=(x, y, c), device_id_type=pl.DeviceIdType.MESH).wait_recv()
        for _, _, cp in copies:
            cp.wait_send()
        mine.wait()

    return pl.pallas_call(
        body, name=name,
        out_shape=jax.ShapeDtypeStruct((N_DEV, rows, LANES), src.dtype),
        in_specs=[pl.BlockSpec(memory_space=pl.ANY)],
        out_specs=pl.BlockSpec(memory_space=pl.ANY),
        scratch_shapes=[pltpu.SemaphoreType.DMA((N_DEV - 1,)), pltpu.SemaphoreType.DMA((N_DEV - 1,)),
                        pltpu.SemaphoreType.DMA(())],
    )(src)


def _rope_tables(pos_col, invf_row):
    ang = pos_col.astype(F32) * invf_row
    lane = lax.broadcasted_iota(jnp.int32, ang.shape, 1)
    cos, sin = jnp.cos(ang), jnp.sin(ang)
    first = (lane >= KR_LO) & (lane < KR_LO + HALF)
    second = (lane >= KR_LO + HALF) & (lane < KR_LO + ROPE)
    return cos, jnp.where(first, sin, 0.0), jnp.where(second, sin, 0.0)


def _rope(t, cos, sin_first, sin_second, sign):
    up = pltpu.roll(t, LANES - HALF, 1)
    down = pltpu.roll(t, HALF, 1)
    return t * cos - sign * (up * sin_first) + sign * (down * sin_second)


def _fwd_proj(x, pos_col, invf_row, wp_in, wp_uq, wp_ukv, q_g, kv_g):
    t = x.shape[0]
    tm = TOKEN_TILE

    def body(x_ref, pos_ref, invf_ref, win_ref, wuq_ref, wukv_ref, qg_ref, kvg_ref,
             proj_ref, q_ref, k_ref, v_ref):
        proj = _dot(x_ref[...].astype(BF16), win_ref[...])
        proj_ref[...] = proj
        c_q = proj[:, :Q_LORA]
        c_kv = proj[:, Q_LORA:Q_LORA + KV_LORA]
        kr_raw = proj[:, Q_LORA + KV_LORA:Q_LORA + KV_LORA + LANES]
        cqn = c_q * lax.rsqrt(jnp.mean(c_q * c_q, axis=-1, keepdims=True) + EPS) * qg_ref[...]
        ckvn = c_kv * lax.rsqrt(jnp.mean(c_kv * c_kv, axis=-1, keepdims=True) + EPS) * kvg_ref[...]
        q_full = _dot(cqn.astype(BF16), wuq_ref[...])
        kv_full = _dot(ckvn.astype(BF16), wukv_ref[...])
        cos, s1, s2 = _rope_tables(pos_ref[...], invf_ref[...])
        kr = _rope(kr_raw, cos, s1, s2, 1.0)
        for h in range(HEADS):
            lo = h * HEAD_PAD
            q_ref[h] = _rope(q_full[:, lo:lo + HEAD_PAD], cos, s1, s2, 1.0).astype(BF16)
            k_ref[h] = (kv_full[:, lo:lo + HEAD_PAD] + kr).astype(BF16)
            v_ref[h] = kv_full[:, HEADS * HEAD_PAD + lo:HEADS * HEAD_PAD + lo + HEAD_PAD].astype(BF16)

    full = lambda a: pl.BlockSpec(a.shape, lambda i: (0,) * a.ndim)
    head_spec = pl.BlockSpec((HEADS, tm, HEAD_PAD), lambda i: (0, i, 0))
    head_shape = jax.ShapeDtypeStruct((HEADS, t, HEAD_PAD), BF16)
    return pl.pallas_call(
        body, name="fwd_proj", grid=(t // tm,),
        in_specs=[pl.BlockSpec((tm, D_MODEL), lambda i: (i, 0)), pl.BlockSpec((tm, 1), lambda i: (i, 0)),
                  full(invf_row), full(wp_in), full(wp_uq), full(wp_ukv), full(q_g), full(kv_g)],
        out_specs=[pl.BlockSpec((tm, D_IN_PAD), lambda i: (i, 0)), head_spec, head_spec, head_spec],
        out_shape=[jax.ShapeDtypeStruct((t, D_IN_PAD), F32), head_shape, head_shape, head_shape],
        compiler_params=_cparams(("arbitrary",)),
    )(x, pos_col, invf_row, wp_in, wp_uq, wp_ukv, q_g, kv_g)


def _attn_fwd(q, k, v):
    t = q.shape[1]
    blk = ATTN_BLOCK

    def body(q_ref, k_ref, v_ref, ol_ref):
        i = pl.program_id(1)
        qb = q_ref[0]
        row = lax.broadcasted_iota(jnp.int32, (blk, blk), 0)
        col = lax.broadcasted_iota(jnp.int32, (blk, blk), 1)

        def step(j, carry, masked):
            m, l, acc = carry
            at = pl.ds(pl.multiple_of(j * blk, blk), blk)
            s = _dot_nt(qb, k_ref[0, at, :]) * ATTN_SCALE
            if masked:
                s = jnp.where(row >= col, s, -jnp.inf)
            m_new = jnp.maximum(m, jnp.max(s, axis=-1, keepdims=True))
            alpha = jnp.exp(m - m_new)
            p = jnp.exp(s - m_new)
            l = alpha * l + jnp.sum(p, axis=-1, keepdims=True)
            acc = alpha * acc + _dot(p.astype(BF16), v_ref[0, at, :])
            return m_new, l, acc

        init = (jnp.full((blk, 1), -jnp.inf, F32), jnp.zeros((blk, 1), F32), jnp.zeros((blk, HEAD_PAD), F32))
        carry = lax.fori_loop(0, i, lambda j, cr: step(j, cr, False), init)
        m, l, acc = step(i, carry, True)
        lane = lax.broadcasted_iota(jnp.int32, (blk, HEAD_PAD), 1)
        ol_ref[0] = jnp.where(lane < VDIM, acc / l, m + jnp.log(l))

    return pl.pallas_call(
        body, name="attn_fwd", grid=(HEADS, t // blk),
        in_specs=[pl.BlockSpec((1, blk, HEAD_PAD), lambda h, i: (h, i, 0)),
                  pl.BlockSpec((1, t, HEAD_PAD), lambda h, i: (h, 0, 0)),
                  pl.BlockSpec((1, t, HEAD_PAD), lambda h, i: (h, 0, 0))],
        out_specs=pl.BlockSpec((1, blk, HEAD_PAD), lambda h, i: (h, i, 0)),
        out_shape=jax.ShapeDtypeStruct((HEADS, t, HEAD_PAD), F32),
        compiler_params=_cparams(("arbitrary", "arbitrary")),
    )(q, k, v)


def _mid(x, target, proj, ol, w_out, ws_low, ws_low_t, bsp, sgu_g, sgu_b, ln_g, ln_b):
    t = x.shape[0]
    tm = TOKEN_TILE
    n_steps = t // tm

    def body(x_ref, tgt_ref, za_ref, u_ref, v_ref, zb_ref, ol_ref, wout_ref, ws_ref, wst_ref, bsp_ref,
             sg_ref, sb_ref, lg_ref, lb_ref,
             dr_ref, dod_ref, drest_ref, dwout_ref, dws_ref, dbs_ref, dlg_ref, dlb_ref, dsg_ref, dsb_ref,
             loss_ref, dbsp_acc):
        step = pl.program_id(0)

        @pl.when(step == 0)
        def _():
            dwout_ref[...] = jnp.zeros_like(dwout_ref)
            dws_ref[...] = jnp.zeros_like(dws_ref)
            dbs_ref[...] = jnp.zeros_like(dbs_ref)
            dlg_ref[...] = jnp.zeros_like(dlg_ref)
            dlb_ref[...] = jnp.zeros_like(dlb_ref)
            dsg_ref[...] = jnp.zeros_like(dsg_ref)
            dsb_ref[...] = jnp.zeros_like(dsb_ref)
            loss_ref[...] = jnp.zeros_like(loss_ref)
            dbsp_acc[...] = jnp.zeros_like(dbsp_acc)

        lane_head = lax.broadcasted_iota(jnp.int32, (CHUNK, G_WIDTH), 1) // G_HEAD_DIM

        attn = jnp.concatenate([ol_ref[h][:, :VDIM] for h in range(HEADS)], axis=-1)
        za = za_ref[...]
        sig_a = _sigmoid(za)
        silu_a = za * sig_a
        out_a = attn * silu_a
        u = u_ref[...]
        ug = _gelu(u)
        vpre = v_ref[...]
        gv = _gelu(vpre)
        mu_v = jnp.mean(gv, axis=-1, keepdims=True)
        cen_v = gv - mu_v
        rstd_v = lax.rsqrt(jnp.mean(cen_v * cen_v, axis=-1, keepdims=True) + EPS)
        vhat = cen_v * rstd_v
        vg = vhat * sg_ref[...] + sb_ref[...]
        vg_b = vg.astype(BF16)
        sv_parts = []
        for cix in range(tm // CHUNK):
            vc = vg_b[cix * CHUNK:(cix + 1) * CHUNK, :]
            acc = bsp_ref[...]
            for h in range(HEADS):
                acc = acc + jnp.where(lane_head == h, _dot(ws_ref[h], vc), 0.0)
            sv_parts.append(acc)
        sv = jnp.concatenate(sv_parts, axis=0)
        sgu = ug * sv
        zb = zb_ref[...]
        sig_b = _sigmoid(zb)
        silu_b = zb * sig_b
        out_b = sgu * silu_b
        merged = jnp.concatenate([out_a, out_b], axis=-1).astype(BF16)
        r = DN_ALPHA * x_ref[...] + _dot(merged, wout_ref[...])
        mu = jnp.mean(r, axis=-1, keepdims=True)
        cen = r - mu
        rstd = lax.rsqrt(jnp.mean(cen * cen, axis=-1, keepdims=True) + EPS)
        xhat = cen * rstd
        hout = xhat * lg_ref[...] + lb_ref[...]
        err = hout - tgt_ref[...]
        row_loss = jnp.mean(err * err, axis=-1, keepdims=True)
        loss_ref[...] += jnp.broadcast_to(0.5 * jnp.sum(row_loss, axis=0, keepdims=True), loss_ref.shape)

        dh = err * (1.0 / D_MODEL)
        dlg_ref[...] += jnp.sum(dh * xhat, axis=0, keepdims=True)
        dlb_ref[...] += jnp.sum(dh, axis=0, keepdims=True)
        dxhat = dh * lg_ref[...]
        dr = rstd * (dxhat - jnp.mean(dxhat, axis=-1, keepdims=True)
                     - xhat * jnp.mean(dxhat * xhat, axis=-1, keepdims=True))
        dr_ref[...] = dr
        dr_b = dr.astype(BF16)
        dwout_ref[...] += _dot_tn(merged, dr_b)
        dmerged = _dot_nt(dr_b, wout_ref[...])
        d_out_a = dmerged[:, :G_WIDTH]
        d_out_b = dmerged[:, G_WIDTH:]
        dattn = d_out_a * silu_a
        for h in range(HEADS):
            do_h = dattn[:, h * VDIM:(h + 1) * VDIM]
            dsum = jnp.sum(do_h * ol_ref[h][:, :VDIM], axis=-1, keepdims=True)
            dod_ref[h] = jnp.concatenate([do_h, jnp.broadcast_to(dsum, (tm, HEAD_PAD - VDIM))], axis=-1)
        dza = d_out_a * attn * (sig_a * (1.0 + za * (1.0 - sig_a)))
        dsgu = d_out_b * silu_b
        dzb = d_out_b * sgu * (sig_b * (1.0 + zb * (1.0 - sig_b)))
        du = dsgu * sv * _gelu_grad(u)
        dsv = dsgu * ug
        dsv_b = dsv.astype(BF16)
        dvg_parts = []
        for cix in range(tm // CHUNK):
            rows = slice(cix * CHUNK, (cix + 1) * CHUNK)
            dsv_c = dsv[rows, :]
            dsv_cb = dsv_b[rows, :]
            vc = vg_b[rows, :]
            dbsp_acc[...] += dsv_c
            acc = jnp.zeros((CHUNK, G_WIDTH), F32)
            for h in range(HEADS):
                on = lane_head == h
                acc = acc + jnp.where(on, _dot(wst_ref[h], dsv_cb), 0.0)
                dws_ref[h] += _dot_nt(jnp.where(on, dsv_cb, jnp.zeros_like(dsv_cb)), vc)
            dvg_parts.append(acc)
        dvg = jnp.concatenate(dvg_parts, axis=0)
        dsg_ref[...] += jnp.sum(dvg * vhat, axis=0, keepdims=True)
        dsb_ref[...] += jnp.sum(dvg, axis=0, keepdims=True)
        dvhat = dvg * sg_ref[...]
        dgv = rstd_v * (dvhat - jnp.mean(dvhat, axis=-1, keepdims=True)
                        - vhat * jnp.mean(dvhat * vhat, axis=-1, keepdims=True))
        dv = dgv * _gelu_grad(vpre)
        drest_ref[...] = jnp.concatenate([dza, du, dv, dzb], axis=-1).astype(BF16)

        @pl.when(step == n_steps - 1)
        def _():
            tri = (lax.broadcasted_iota(jnp.int32, (CHUNK, CHUNK), 0)
                   >= lax.broadcasted_iota(jnp.int32, (CHUNK, CHUNK), 1))
            for h in range(HEADS):
                dws_ref[h] = jnp.where(tri, dws_ref[h], 0.0)
            tot = dbsp_acc[...]
            lane = lax.broadcasted_iota(jnp.int32, (CHUNK, LANES), 1)
            dbs = jnp.zeros((CHUNK, LANES), F32)
            for h in range(HEADS):
                head_sum = jnp.sum(tot[:, h * G_HEAD_DIM:(h + 1) * G_HEAD_DIM], axis=-1, keepdims=True)
                dbs = jnp.where(lane == h, head_sum, dbs)
            dbs_ref[...] = dbs

    full = lambda a: pl.BlockSpec(a.shape, lambda i: (0,) * a.ndim)
    tile = lambda w, j=0: pl.BlockSpec((tm, w), lambda i, j=j: (i, j))
    heads = pl.BlockSpec((HEADS, tm, HEAD_PAD), lambda i: (0, i, 0))
    acc = lambda shape: (pl.BlockSpec(shape, lambda i: (0,) * len(shape)), jax.ShapeDtypeStruct(shape, F32))
    accs = [acc((D_MODEL, D_MODEL)), acc((HEADS, CHUNK, CHUNK)), acc((CHUNK, LANES)), acc((1, D_MODEL)),
            acc((1, D_MODEL)), acc((1, G_WIDTH)), acc((1, G_WIDTH)), acc((1, LANES))]
    return pl.pallas_call(
        body, name="mid", grid=(n_steps,),
        in_specs=[tile(D_MODEL), tile(D_MODEL), tile(G_WIDTH, 1), tile(G_WIDTH, 2), tile(G_WIDTH, 3), tile(G_WIDTH, 4),
                  heads, full(w_out), full(ws_low), full(ws_low_t), full(bsp), full(sgu_g), full(sgu_b),
                  full(ln_g), full(ln_b)],
        out_specs=[tile(D_MODEL), heads, tile(4 * G_WIDTH)] + [a[0] for a in accs],
        out_shape=[jax.ShapeDtypeStruct((t, D_MODEL), F32), jax.ShapeDtypeStruct((HEADS, t, HEAD_PAD), F32),
                   jax.ShapeDtypeStruct((t, 4 * G_WIDTH), BF16)] + [a[1] for a in accs],
        scratch_shapes=[pltpu.VMEM((CHUNK, G_WIDTH), F32)],
        compiler_params=_cparams(("arbitrary",)),
    )(x, target, proj, proj, proj, proj, ol, w_out, ws_low, ws_low_t, bsp, sgu_g, sgu_b, ln_g, ln_b)


def _attn_bwd(q, k, v, ol, dod):
    t = q.shape[1]
    blk = ATTN_BLOCK
    nblk = t // blk

    def body(q_ref, k_ref, v_ref, ol_ref, dod_ref, dq_ref, dk_ref, dv_ref):
        j = pl.program_id(1)

        @pl.when(j == 0)
        def _():
            dq_ref[...] = jnp.zeros_like(dq_ref)

        kb = k_ref[0]
        vb = v_ref[0]
        row = lax.broadcasted_iota(jnp.int32, (blk, blk), 0)
        col = lax.broadcasted_iota(jnp.int32, (blk, blk), 1)

        def step(i, carry, masked):
            dk, dv = carry
            at = pl.ds(pl.multiple_of(i * blk, blk), blk)
            qb = q_ref[0, at, :]
            dod_i = dod_ref[0, at, :]
            lse = ol_ref[0, at, VDIM:VDIM + 1]
            dsum = dod_i[:, VDIM:VDIM + 1]
            do_b = dod_i.astype(BF16)
            s = _dot_nt(qb, kb) * ATTN_SCALE
            p = jnp.exp(s - lse)
            if masked:
                p = jnp.where(row >= col, p, 0.0)
            dp = _dot_nt(do_b, vb)
            ds = (p * (dp - dsum) * ATTN_SCALE).astype(BF16)
            dv = dv + _dot_tn(p.astype(BF16), do_b)
            dk = dk + _dot_tn(ds, qb)
            dq_ref[0, at, :] += _dot(ds, kb)
            return dk, dv

        zero = jnp.zeros((blk, HEAD_PAD), F32)
        carry = step(j, (zero, zero), True)
        dk, dv = lax.fori_loop(j + 1, nblk, lambda i, cr: step(i, cr, False), carry)
        dk_ref[0] = dk
        dv_ref[0] = dv

    whole = pl.BlockSpec((1, t, HEAD_PAD), lambda h, j: (h, 0, 0))
    block = pl.BlockSpec((1, blk, HEAD_PAD), lambda h, j: (h, j, 0))
    shape = jax.ShapeDtypeStruct((HEADS, t, HEAD_PAD), F32)
    return pl.pallas_call(
        body, name="attn_bwd", grid=(HEADS, nblk),
        in_specs=[whole, block, block, whole, whole],
        out_specs=[whole, block, block],
        out_shape=[shape, shape, shape],
        compiler_params=_cparams(("arbitrary", "arbitrary")),
    )(q, k, v, ol, dod)


def _bwd_qkv(dq, dk, dv, proj, pos_col, invf_row, wp_uq, wp_ukv, q_g, kv_g):
    t = proj.shape[0]
    tm = TOKEN_TILE

    def body(dq_ref, dk_ref, dv_ref, ph_ref, pos_ref, invf_ref, wuq_ref, wukv_ref, qg_ref, kvg_ref,
             dhead_ref, dwuq_ref, dwukv_ref, dqg_ref, dkvg_ref):
        @pl.when(pl.program_id(0) == 0)
        def _():
            dwuq_ref[...] = jnp.zeros_like(dwuq_ref)
            dwukv_ref[...] = jnp.zeros_like(dwukv_ref)
            dqg_ref[...] = jnp.zeros_like(dqg_ref)
            dkvg_ref[...] = jnp.zeros_like(dkvg_ref)

        cos, s1, s2 = _rope_tables(pos_ref[...], invf_ref[...])
        dq_full = jnp.concatenate([_rope(dq_ref[h], cos, s1, s2, -1.0) for h in range(HEADS)], axis=-1)
        dkv_full = jnp.concatenate([dk_ref[h] for h in range(HEADS)] + [dv_ref[h] for h in range(HEADS)], axis=-1)
        dkr_rot = dk_ref[0]
        for h in range(1, HEADS):
            dkr_rot = dkr_rot + dk_ref[h]
        lane = lax.broadcasted_iota(jnp.int32, (tm, LANES), 1)
        rot_lanes = (lane >= KR_LO) & (lane < KR_LO + ROPE)
        dkr_raw = jnp.where(rot_lanes, _rope(dkr_rot, cos, s1, s2, -1.0), 0.0)

        c_q = ph_ref[:, :Q_LORA]
        c_kv = ph_ref[:, Q_LORA:Q_LORA + KV_LORA]
        rstd_q = lax.rsqrt(jnp.mean(c_q * c_q, axis=-1, keepdims=True) + EPS)
        rstd_kv = lax.rsqrt(jnp.mean(c_kv * c_kv, axis=-1, keepdims=True) + EPS)
        qhat = c_q * rstd_q
        kvhat = c_kv * rstd_kv
        dq_b = dq_full.astype(BF16)
        dkv_b = dkv_full.astype(BF16)
        dwuq_ref[...] += _dot_tn((qhat * qg_ref[...]).astype(BF16), dq_b)
        dwukv_ref[...] += _dot_tn((kvhat * kvg_ref[...]).astype(BF16), dkv_b)
        dcqn = _dot_nt(dq_b, wuq_ref[...])
        dckvn = _dot_nt(dkv_b, wukv_ref[...])
        dqg_ref[...] += jnp.sum(dcqn * qhat, axis=0, keepdims=True)
        dkvg_ref[...] += jnp.sum(dckvn * kvhat, axis=0, keepdims=True)
        dqh = dcqn * qg_ref[...]
        dkvh = dckvn * kvg_ref[...]
        dc_q = rstd_q * (dqh - qhat * jnp.mean(dqh * qhat, axis=-1, keepdims=True))
        dc_kv = rstd_kv * (dkvh - kvhat * jnp.mean(dkvh * kvhat, axis=-1, keepdims=True))
        dhead_ref[...] = jnp.concatenate([dc_q, dc_kv, dkr_raw], axis=-1).astype(BF16)

    full = lambda a: pl.BlockSpec(a.shape, lambda i: (0,) * a.ndim)
    heads = pl.BlockSpec((HEADS, tm, HEAD_PAD), lambda i: (0, i, 0))
    acc = lambda shape: (pl.BlockSpec(shape, lambda i: (0,) * len(shape)), jax.ShapeDtypeStruct(shape, F32))
    accs = [acc((Q_LORA, HEADS * HEAD_PAD)), acc((KV_LORA, 2 * HEADS * HEAD_PAD)), acc((1, Q_LORA)), acc((1, KV_LORA))]
    return pl.pallas_call(
        body, name="bwd_qkv", grid=(t // tm,),
        in_specs=[heads, heads, heads, pl.BlockSpec((tm, 4 * LANES), lambda i: (i, 0)),
                  pl.BlockSpec((tm, 1), lambda i: (i, 0)), full(invf_row), full(wp_uq), full(wp_ukv),
                  full(q_g), full(kv_g)],
        out_specs=[pl.BlockSpec((tm, 4 * LANES), lambda i: (i, 0))] + [a[0] for a in accs],
        out_shape=[jax.ShapeDtypeStruct((t, 4 * LANES), BF16)] + [a[1] for a in accs],
        compiler_params=_cparams(("arbitrary",)),
    )(dq, dk, dv, proj, pos_col, invf_row, wp_uq, wp_ukv, q_g, kv_g)


def _bwd_in(x, dr, dhead, drest, wp_in):
    t = x.shape[0]
    tm = TOKEN_TILE
    n_head = dhead.shape[1]

    def body(x_ref, dr_ref, dhead_ref, drest_ref, win_ref, gx_ref, dwin_ref):
        @pl.when(pl.program_id(0) == 0)
        def _():
            dwin_ref[...] = jnp.zeros_like(dwin_ref)

        xb = x_ref[...].astype(BF16)
        dh_b = dhead_ref[...]
        dr_b = drest_ref[...]
        gx_ref[...] = (DN_ALPHA * dr_ref[...] + _dot_nt(dh_b, win_ref[:, :n_head])
                       + _dot_nt(dr_b, win_ref[:, n_head:]))
        dwin_ref[:, :n_head] += _dot_tn(xb, dh_b)
        dwin_ref[:, n_head:] += _dot_tn(xb, dr_b)

    tile = lambda w: pl.BlockSpec((tm, w), lambda i: (i, 0))
    whole = pl.BlockSpec(wp_in.shape, lambda i: (0, 0))
    return pl.pallas_call(
        body, name="bwd_in", grid=(t // tm,),
        in_specs=[tile(D_MODEL), tile(D_MODEL), tile(n_head), tile(drest.shape[1]), whole],
        out_specs=[tile(D_MODEL), whole],
        out_shape=[jax.ShapeDtypeStruct((t, D_MODEL), F32), jax.ShapeDtypeStruct(wp_in.shape, F32)],
        compiler_params=_cparams(("arbitrary",)),
    )(x, dr, dhead, drest, wp_in)


def _adam(parts, w, m, v, *, name, tile_rows):
    n, rows, _ = parts.shape

    def body(p_ref, w_ref, m_ref, v_ref, g_ref, d_ref, nm_ref, nv_ref):
        g = p_ref[0]
        for s in range(1, n):
            g = g + p_ref[s]
        m_new = ADAM_B1 * m_ref[...] + (1.0 - ADAM_B1) * g
        v_new = ADAM_B2 * v_ref[...] + (1.0 - ADAM_B2) * (g * g)
        m_hat = m_new / (1.0 - ADAM_B1 ** ADAM_STEP)
        v_hat = v_new / (1.0 - ADAM_B2 ** ADAM_STEP)
        g_ref[...] = g
        d_ref[...] = -ADAM_LR * (m_hat / (jnp.sqrt(v_hat) + ADAM_EPS) + ADAM_WD * w_ref[...])
        nm_ref[...] = m_new
        nv_ref[...] = v_new

    flat = pl.BlockSpec((tile_rows, LANES), lambda i: (i, 0))
    shape = jax.ShapeDtypeStruct((rows, LANES), F32)
    return pl.pallas_call(
        body, name=name, grid=(rows // tile_rows,),
        in_specs=[pl.BlockSpec((n, tile_rows, LANES), lambda i: (0, i, 0)), flat, flat, flat],
        out_specs=[flat] * 4, out_shape=[shape] * 4,
        compiler_params=_cparams(("arbitrary",)),
    )(parts, w, m, v)


SMALL_NAMES = ("q_norm_g", "kv_norm_g", "sgu_norm_g", "sgu_norm_b", "b_spatial", "ln_g", "ln_b")
SMALL_SIZES = (Q_LORA, KV_LORA, G_WIDTH, G_WIDTH, HEADS * CHUNK, D_MODEL, D_MODEL)


def _pack_small(vals):
    flat = jnp.concatenate([v.reshape(-1) for v in vals])
    return jnp.pad(flat, (0, SMALL_LEN - flat.shape[0]))


def _unpack_small(flat):
    out, at = [], 0
    for n in SMALL_SIZES:
        out.append(flat[at:at + n])
        at += n
    out[4] = out[4].reshape(HEADS, CHUNK)
    return out


def _rows(a):
    return a.reshape(-1, LANES)


SHARD_ROWS = (D_MODEL * (D_IN // N_DEV) // LANES, Q_LORA * (HEADS * (NOPE + ROPE) // N_DEV) // LANES,
              KV_LORA * (HEADS * (NOPE + VDIM) // N_DEV) // LANES, (D_MODEL // N_DEV) * D_MODEL // LANES)
SHARD_SHAPES = ((D_MODEL, D_IN // N_DEV), (Q_LORA, HEADS * (NOPE + ROPE) // N_DEV),
                (KV_LORA, HEADS * (NOPE + VDIM) // N_DEV), (D_MODEL // N_DEV, D_MODEL))
W_ROWS = sum(SHARD_ROWS)


def _pack_owned(w_in, w_uq, w_ukv, w_out, w_sp_head, small_chunk):
    body = jnp.concatenate([_rows(w_in), _rows(w_uq), _rows(w_ukv), _rows(w_out), _rows(w_sp_head), _rows(small_chunk)])
    return jnp.pad(body, ((0, SLAB_ROWS - body.shape[0]), (0, 0)))


def _unpack_owned(slab):
    out, at = [], 0
    for n, shape in zip(SHARD_ROWS, SHARD_SHAPES):
        out.append(slab[at:at + n].reshape(shape))
        at += n
    return out


def _weights_for_kernels(w_in, w_uq, w_ukv, w_out):
    z = lambda r, c: jnp.zeros((r, c), BF16)
    split = Q_LORA + KV_LORA
    wp_in = jnp.concatenate([w_in[:, :split], z(D_MODEL, KR_LO), w_in[:, split:split + ROPE],
                             z(D_MODEL, LANES - KR_LO - ROPE), w_in[:, split + ROPE:]], axis=1)
    wp_uq = jnp.pad(w_uq.reshape(Q_LORA, HEADS, NOPE + ROPE), ((0, 0), (0, 0), (0, HEAD_PAD - NOPE - ROPE)))
    wp_uq = wp_uq.reshape(Q_LORA, HEADS * HEAD_PAD)
    kv = w_ukv.reshape(KV_LORA, HEADS, NOPE + VDIM)
    pad = ((0, 0), (0, 0), (0, HEAD_PAD - NOPE))
    wp_ukv = jnp.concatenate([jnp.pad(kv[:, :, :NOPE], pad).reshape(KV_LORA, -1),
                              jnp.pad(kv[:, :, NOPE:], pad).reshape(KV_LORA, -1)], axis=1)
    return wp_in, wp_uq, wp_ukv, w_out


def _grads_from_kernels(dwp_in, dwp_uq, dwp_ukv):
    split = Q_LORA + KV_LORA
    d_in = jnp.concatenate([dwp_in[:, :split], dwp_in[:, split + KR_LO:split + KR_LO + ROPE],
                            dwp_in[:, split + LANES:]], axis=1)
    d_uq = dwp_uq.reshape(Q_LORA, HEADS, HEAD_PAD)[:, :, :NOPE + ROPE].reshape(Q_LORA, -1)
    kv = dwp_ukv.reshape(KV_LORA, 2, HEADS, HEAD_PAD)
    d_ukv = jnp.concatenate([kv[:, 0, :, :NOPE], kv[:, 1, :, :VDIM]], axis=-1).reshape(KV_LORA, -1)
    return d_in, d_uq, d_ukv


def _column_shards(a):
    r = a.shape[0]
    return a.reshape(r, N_DEV, -1).transpose(1, 0, 2)


def kernel(x, positions, w_in, q_norm_g, w_uq, kv_norm_g, w_ukv, sgu_norm_g, sgu_norm_b, w_spatial, b_spatial, w_out, ln_g, ln_b, loss_target, m_w_in, m_q_norm_g, m_w_uq, m_kv_norm_g, m_w_ukv, m_sgu_norm_g, m_sgu_norm_b, m_w_spatial, m_b_spatial, m_w_out, m_ln_g, m_ln_b, v_w_in, v_q_norm_g, v_w_uq, v_kv_norm_g, v_w_ukv, v_sgu_norm_g, v_sgu_norm_b, v_w_spatial, v_b_spatial, v_w_out, v_ln_g, v_ln_b):
    me = 4 * lax.axis_index("x") + 2 * lax.axis_index("y") + lax.axis_index("c")
    seq = x.shape[1]
    x2 = x.reshape(seq, D_MODEL)
    tgt2 = loss_target.reshape(seq, D_MODEL)
    pos_col = positions.reshape(seq, 1)

    mine = jnp.concatenate([_rows(w_in), _rows(w_uq), _rows(w_ukv), _rows(w_out)]).astype(BF16)
    gathered = _exchange(mine, name="wgather", per_destination=False)
    parts, at = [], 0
    for n, shape in zip(SHARD_ROWS, SHARD_SHAPES):
        parts.append(gathered[:, at:at + n].reshape((N_DEV,) + shape))
        at += n
    full_in = parts[0].transpose(1, 0, 2).reshape(D_MODEL, D_IN)
    full_uq = parts[1].transpose(1, 0, 2).reshape(Q_LORA, -1)
    full_ukv = parts[2].transpose(1, 0, 2).reshape(KV_LORA, -1)
    full_out = parts[3].reshape(D_MODEL, D_MODEL)
    (loss_part, grad_x, d_in, d_uq, d_ukv, d_out, d_ws, d_bs_t, d_lng, d_lnb, d_sgug, d_sgub, d_qg, d_kvg) = _local_step(
        x2, tgt2, pos_col, full_in, full_uq, full_ukv, full_out, q_norm_g, kv_norm_g, sgu_norm_g, sgu_norm_b,
        w_spatial, b_spatial, ln_g, ln_b)

    small_part = _pack_small([d_qg, d_kvg, d_sgug, d_sgub, d_bs_t[:, :HEADS].T, d_lng, d_lnb]).reshape(N_DEV, -1, LANES)
    body = jnp.concatenate([_column_shards(d_in).reshape(N_DEV, -1, LANES),
                            _column_shards(d_uq).reshape(N_DEV, -1, LANES),
                            _column_shards(d_ukv).reshape(N_DEV, -1, LANES),
                            d_out.reshape(N_DEV, -1, LANES), d_ws.reshape(N_DEV, -1, LANES), small_part], axis=1)
    to_send = jnp.pad(body, ((0, 0), (0, SLAB_ROWS - body.shape[1]), (0, 0)))
    received = _exchange(to_send, name="gexch", per_destination=True)

    take = lambda a: lax.dynamic_index_in_dim(a, me, 0, keepdims=False)
    small_w = _pack_small([q_norm_g, kv_norm_g, sgu_norm_g, sgu_norm_b, b_spatial, ln_g, ln_b])
    small_m = _pack_small([m_q_norm_g, m_kv_norm_g, m_sgu_norm_g, m_sgu_norm_b, m_b_spatial, m_ln_g, m_ln_b])
    small_v = _pack_small([v_q_norm_g, v_kv_norm_g, v_sgu_norm_g, v_sgu_norm_b, v_b_spatial, v_ln_g, v_ln_b])
    chunk = lambda a: take(a.reshape(N_DEV, -1))
    own_w = _pack_owned(w_in, w_uq, w_ukv, w_out, take(w_spatial), chunk(small_w))
    own_m = _pack_owned(m_w_in, m_w_uq, m_w_ukv, m_w_out, take(m_w_spatial), chunk(small_m))
    own_v = _pack_owned(v_w_in, v_w_uq, v_w_ukv, v_w_out, take(v_w_spatial), chunk(small_v))
    g_own, delta_own, m_own, v_own = _adam(received, own_w, own_m, own_v, name="adam", tile_rows=512)

    rep_g = _exchange(g_own[W_ROWS:W_ROWS + REP_ROWS], name="sgather", per_destination=False)
    rep_pack = lambda sp, small: jnp.concatenate(
        [sp.reshape(N_DEV, CHUNK, LANES), small.reshape(N_DEV, -1, LANES)], axis=1).reshape(-1, LANES)
    _, delta_rep, m_rep, v_rep = _adam(rep_g.reshape(1, N_DEV * REP_ROWS, LANES), rep_pack(w_spatial, small_w),
                                       rep_pack(m_w_spatial, small_m), rep_pack(v_w_spatial, small_v),
                                       name="adam_rep", tile_rows=N_DEV * REP_ROWS)

    def rep_unpack(a):
        a = a.reshape(N_DEV, REP_ROWS, LANES)
        small = _unpack_small(a[:, CHUNK:].reshape(-1))
        return [small[0], small[1], small[2], small[3], a[:, :CHUNK], small[4], small[5], small[6]]

    def ordered(owned, rep):
        o_in, o_uq, o_ukv, o_out = _unpack_owned(owned)
        r_qg, r_kvg, r_sg, r_sb, r_ws, r_bs, r_lg, r_lb = rep_unpack(rep)
        return [o_in, r_qg, o_uq, r_kvg, o_ukv, r_sg, r_sb, r_ws, r_bs, o_out, r_lg, r_lb]

    loss = lax.psum(loss_part[0, 0], ("x", "y", "c"))
    outs = [loss, grad_x.reshape(x.shape)]
    outs += ordered(g_own, rep_g.reshape(-1, LANES))
    outs += ordered(delta_own, delta_rep)
    outs += ordered(m_own, m_rep)
    outs += ordered(v_own, v_rep)
    return tuple(outs)


def _local_step(x2, tgt2, pos_col, full_in, full_uq, full_ukv, full_out, q_norm_g, kv_norm_g, sgu_norm_g, sgu_norm_b,
                w_spatial, b_spatial, ln_g, ln_b):
    wp_in, wp_uq, wp_ukv, wb_out = _weights_for_kernels(full_in, full_uq, full_ukv, full_out)

    half = jnp.arange(HALF, dtype=F32)
    inv_freq = 1.0 / (ROPE_THETA ** (half / HALF))
    invf_row = jnp.concatenate([jnp.zeros((KR_LO,), F32), inv_freq, inv_freq,
                                jnp.zeros((LANES - KR_LO - ROPE,), F32)]).reshape(1, LANES)
    tri = jnp.tril(jnp.ones((CHUNK, CHUNK), dtype=bool))
    ws_low = jnp.where(tri[None], w_spatial, 0.0).astype(BF16)
    ws_low_t = ws_low.transpose(0, 2, 1)
    bsp = jnp.repeat(b_spatial.T, G_HEAD_DIM, axis=1)
    row = lambda a: a.reshape(1, -1)

    proj, q, k, v = _fwd_proj(x2, pos_col, invf_row, wp_in, wp_uq, wp_ukv, row(q_norm_g), row(kv_norm_g))
    ol = _attn_fwd(q, k, v)
    (dr, dod, drest, d_out, d_ws, d_bs_t, d_lng, d_lnb, d_sgug, d_sgub, loss_part) = _mid(
        x2, tgt2, proj, ol, wb_out, ws_low, ws_low_t, bsp, row(sgu_norm_g), row(sgu_norm_b), row(ln_g), row(ln_b))
    dq, dk, dv = _attn_bwd(q, k, v, ol, dod)
    dhead, dwp_uq, dwp_ukv, d_qg, d_kvg = _bwd_qkv(dq, dk, dv, proj, pos_col, invf_row, wp_uq, wp_ukv,
                                                   row(q_norm_g), row(kv_norm_g))
    grad_x, dwp_in = _bwd_in(x2, dr, dhead, drest, wp_in)
    d_in, d_uq, d_ukv = _grads_from_kernels(dwp_in, dwp_uq, dwp_ukv)
    return loss_part, grad_x, d_in, d_uq, d_ukv, d_out, d_ws, d_bs_t, d_lng, d_lnb, d_sgug, d_sgub, d_qg, d_kvg
```

```python
import functools
import math

import jax
import jax.numpy as jnp
from jax import lax
from jax.experimental import pallas as pl
from jax.experimental.pallas import tpu as pltpu

F32 = jnp.float32
BF16 = jnp.bfloat16

N_DEV = 8
D_MODEL = 1024
HEADS = 8
NOPE = 64
ROPE = 32
HALF = ROPE // 2
VDIM = 64
Q_LORA = 256
KV_LORA = 128
G_WIDTH = 512
G_HEAD_DIM = 64
CHUNK = 128
HEAD_PAD = 128
D_IN = 2464
D_IN_PAD = 2560
KR_LO = NOPE
ROPE_THETA = 10000.0
DN_ALPHA = 2.0 ** 0.25
EPS = 1e-5
ATTN_SCALE = 1.0 / math.sqrt(NOPE + ROPE)
ADAM_LR, ADAM_B1, ADAM_B2, ADAM_EPS, ADAM_WD, ADAM_STEP = 0.001, 0.9, 0.999, 1e-08, 0.01, 10

LANES = 128
SLAB_ROWS = 4096
REP_ROWS = 136
SMALL_LEN = 8192
VMEM_LIMIT = 56 * 1024 * 1024

TOKEN_TILE = 256
ATTN_BLOCK = 512
SOFTMAX_ROWS = 256
LOG2E = 1.4426950408889634
LN2 = 0.6931471805599453
Q_PRESCALE = ATTN_SCALE * LOG2E


def _cparams(sem=None):
    return pltpu.CompilerParams(dimension_semantics=sem, vmem_limit_bytes=VMEM_LIMIT)


def _dot(a, b):
    return jnp.dot(a, b, preferred_element_type=F32)


def _dot_nt(a, b):
    return lax.dot_general(a, b, (((1,), (1,)), ((), ())), preferred_element_type=F32)


def _dot_tn(a, b):
    return lax.dot_general(a, b, (((0,), (0,)), ((), ())), preferred_element_type=F32)


def _as_row(col):
    return jnp.transpose(jnp.broadcast_to(col, (col.shape[0], LANES)))[0:1, :]


def _sigmoid(z):
    return 1.0 / (1.0 + jnp.exp(-z))


def _gelu(x):
    return 0.5 * x * (1.0 + lax.erf(x * 0.7071067811865476))


def _gelu_grad(x):
    cdf = 0.5 * (1.0 + lax.erf(x * 0.7071067811865476))
    return cdf + x * jnp.exp(-0.5 * x * x) * 0.3989422804014327


def _exchange(src, *, name, per_destination):
    rows = src.shape[-2]

    def body(src_ref, out_ref, send_sems, recv_sems, local_sem):
        x, y, c = lax.axis_index("x"), lax.axis_index("y"), lax.axis_index("c")
        me = 4 * x + 2 * y + c

        def slab_for(dest):
            return src_ref.at[dest] if per_destination else src_ref

        mine = pltpu.make_async_copy(slab_for(me), out_ref.at[me], local_sem)
        mine.start()
        copies = []
        for k in (6, 7, 4, 5, 2, 3, 1):
            px = 1 - x if k & 4 else x
            py = 1 - y if k & 2 else y
            pc = 1 - c if k & 1 else c
            peer = 4 * px + 2 * py + pc
            cp = pltpu.make_async_remote_copy(
                src_ref=slab_for(peer), dst_ref=out_ref.at[me],
                send_sem=send_sems.at[k - 1], recv_sem=recv_sems.at[k - 1],
                device_id=(px, py, pc), device_id_type=pl.DeviceIdType.MESH)
            cp.start()
            copies.append((k, peer, cp))
        for k, peer, cp in copies:
            pltpu.make_async_remote_copy(
                src_ref=slab_for(peer), dst_ref=out_ref.at[peer],
                send_sem=send_sems.at[k - 1], recv_sem=recv_sems.at[k - 1],
                device_id=(x, y, c), device_id_type=pl.DeviceIdType.MESH).wait_recv()
        for _, _, cp in copies:
            cp.wait_send()
        mine.wait()

    return pl.pallas_call(
        body, name=name,
        out_shape=jax.ShapeDtypeStruct((N_DEV, rows, LANES), src.dtype),
        in_specs=[pl.BlockSpec(memory_space=pl.ANY)],
        out_specs=pl.BlockSpec(memory_space=pl.ANY),
        scratch_shapes=[pltpu.SemaphoreType.DMA((N_DEV - 1,)), pltpu.SemaphoreType.DMA((N_DEV - 1,)),
                        pltpu.SemaphoreType.DMA(())],
    )(src)


def _rope_tables(pos_col, invf_row):
    ang = pos_col.astype(F32) * invf_row
    lane = lax.broadcasted_iota(jnp.int32, ang.shape, 1)
    cos, sin = jnp.cos(ang), jnp.sin(ang)
    first = (lane >= KR_LO) & (lane < KR_LO + HALF)
    second = (lane >= KR_LO + HALF) & (lane < KR_LO + ROPE)
    return cos, jnp.where(first, sin, 0.0), jnp.where(second, sin, 0.0)


def _rope(t, cos, sin_first, sin_second, sign):
    up = pltpu.roll(t, LANES - HALF, 1)
    down = pltpu.roll(t, HALF, 1)
    return t * cos - sign * (up * sin_first) + sign * (down * sin_second)


def _fwd_proj(x, pos_col, invf_row, wp_in, wp_uq, wp_ukv, q_g, kv_g):
    t = x.shape[0]
    tm = TOKEN_TILE

    def body(x_ref, pos_ref, invf_ref, win_ref, wuq_ref, wukv_ref, qg_ref, kvg_ref,
             proj_ref, q_ref, k_ref, v_ref, vt_ref):
        proj = _dot(x_ref[...].astype(BF16), win_ref[...])
        proj_ref[...] = proj
        c_q = proj[:, :Q_LORA]
        c_kv = proj[:, Q_LORA:Q_LORA + KV_LORA]
        kr_raw = proj[:, Q_LORA + KV_LORA:Q_LORA + KV_LORA + LANES]
        cqn = c_q * lax.rsqrt(jnp.mean(c_q * c_q, axis=-1, keepdims=True) + EPS) * qg_ref[...]
        ckvn = c_kv * lax.rsqrt(jnp.mean(c_kv * c_kv, axis=-1, keepdims=True) + EPS) * kvg_ref[...]
        q_full = _dot(cqn.astype(BF16), wuq_ref[...])
        kv_full = _dot(ckvn.astype(BF16), wukv_ref[...])
        cos, s1, s2 = _rope_tables(pos_ref[...], invf_ref[...])
        kr = _rope(kr_raw, cos, s1, s2, 1.0)
        for h in range(HEADS):
            lo = h * HEAD_PAD
            q_ref[h] = (_rope(q_full[:, lo:lo + HEAD_PAD], cos, s1, s2, 1.0) * Q_PRESCALE).astype(BF16)
            k_ref[h] = (kv_full[:, lo:lo + HEAD_PAD] + kr).astype(BF16)
            v_h = kv_full[:, HEADS * HEAD_PAD + lo:HEADS * HEAD_PAD + lo + HEAD_PAD]
            v_ref[h] = v_h.astype(BF16)
            vt_ref[h] = jnp.transpose(v_h).astype(BF16)

    full = lambda a: pl.BlockSpec(a.shape, lambda i: (0,) * a.ndim)
    head_spec = pl.BlockSpec((HEADS, tm, HEAD_PAD), lambda i: (0, i, 0))
    head_shape = jax.ShapeDtypeStruct((HEADS, t, HEAD_PAD), BF16)
    return pl.pallas_call(
        body, name="fwd_proj", grid=(t // tm,),
        in_specs=[pl.BlockSpec((tm, D_MODEL), lambda i: (i, 0)), pl.BlockSpec((tm, 1), lambda i: (i, 0)),
                  full(invf_row), full(wp_in), full(wp_uq), full(wp_ukv), full(q_g), full(kv_g)],
        out_specs=[pl.BlockSpec((tm, D_IN_PAD), lambda i: (i, 0)), head_spec, head_spec, head_spec,
                   pl.BlockSpec((HEADS, HEAD_PAD, tm), lambda i: (0, 0, i))],
        out_shape=[jax.ShapeDtypeStruct((t, D_IN_PAD), F32), head_shape, head_shape, head_shape,
                   jax.ShapeDtypeStruct((HEADS, HEAD_PAD, t), BF16)],
        compiler_params=_cparams(("arbitrary",)),
    )(x, pos_col, invf_row, wp_in, wp_uq, wp_ukv, q_g, kv_g)


def _attn_fwd(q, k, vt):
    t = q.shape[1]
    blk = ATTN_BLOCK
    chunk = SOFTMAX_ROWS

    def body(q_ref, k_ref, vt_ref, o_ref, lse_ref, s0, s1, p0, p1, x0, x1, m_scr, l_scr, a_scr, acc_scr):
        i = pl.program_id(1)
        at = lambda j: pl.ds(pl.multiple_of(j * blk, blk), blk)

        def exp_pass(s_in, block_max, p_out, masked):
            def load(r):
                s = s_in[r:r + chunk, :]
                if masked:
                    key = lax.broadcasted_iota(jnp.int32, (chunk, blk), 0) + r
                    qry = lax.broadcasted_iota(jnp.int32, (chunk, blk), 1)
                    s = jnp.where(qry >= key, s, -jnp.inf)
                return s

            if masked:
                block_max = jnp.max(load(0), axis=0, keepdims=True)
                for r in range(chunk, blk, chunk):
                    block_max = jnp.maximum(block_max, jnp.max(load(r), axis=0, keepdims=True))
            m_new = jnp.maximum(m_scr[...], block_max)
            alpha = jnp.exp2(m_scr[...] - m_new)
            total = jnp.zeros((1, blk), F32)
            for r in range(0, blk, chunk):
                p = jnp.exp2(load(r) - m_new)
                p_out[r:r + chunk, :] = p.astype(BF16)
                total = total + jnp.sum(p, axis=0, keepdims=True)
            m_scr[...] = m_new
            l_scr[...] = alpha * l_scr[...] + total
            return alpha

        def scores(j, s_out, x_out):
            s = _dot_nt(k_ref[0, at(j), :], q_ref[0])
            s_out[...] = s
            x_out[...] = jnp.max(s, axis=0, keepdims=True)

        def value_product(j, p_in):
            return _dot(vt_ref[0, :, at(j)], p_in[...])

        def one_pass(j, s_in, x_in, s_out, x_out, p_prev, p_cur):
            scores(j + 1, s_out, x_out)
            acc_scr[...] = a_scr[...] * acc_scr[...] + value_product(jnp.maximum(j - 1, 0), p_prev)
            a_scr[...] = exp_pass(s_in, x_in[...], p_cur, False)

        scores(0, s0, x0)
        p1[...] = jnp.zeros_like(p1)
        a_scr[...] = jnp.ones_like(a_scr)
        m_scr[...] = jnp.full(m_scr.shape, -jnp.inf, F32)
        l_scr[...] = jnp.zeros_like(l_scr)
        acc_scr[...] = jnp.zeros_like(acc_scr)

        def two_passes(n, _):
            one_pass(2 * n, s0, x0, s1, x1, p1, p0)
            one_pass(2 * n + 1, s1, x1, s0, x0, p0, p1)
            return 0

        lax.fori_loop(0, i // 2, two_passes, 0)

        @pl.when(i % 2 == 1)
        def _():
            one_pass(i - 1, s0, x0, s1, x1, p1, p0)

        def diagonal(s_in, p_prev, p_cur):
            acc = a_scr[...] * acc_scr[...] + value_product(jnp.maximum(i - 1, 0), p_prev)
            alpha = exp_pass(s_in, None, p_cur, True)
            acc = alpha * acc + value_product(i, p_cur)
            o_ref[0] = jnp.transpose(acc / l_scr[...])
            lse_ref[0] = m_scr[...] + jnp.log2(l_scr[...])

        @pl.when(i % 2 == 0)
        def _():
            diagonal(s0, p1, p0)

        @pl.when(i % 2 == 1)
        def _():
            diagonal(s1, p0, p1)

    square = lambda dtype: pltpu.VMEM((blk, blk), dtype)
    stat = pltpu.VMEM((1, blk), F32)
    return pl.pallas_call(
        body, name="attn_fwd", grid=(HEADS, t // blk),
        in_specs=[pl.BlockSpec((1, blk, HEAD_PAD), lambda h, i: (h, i, 0)),
                  pl.BlockSpec((1, t, HEAD_PAD), lambda h, i: (h, 0, 0)),
                  pl.BlockSpec((1, HEAD_PAD, t), lambda h, i: (h, 0, 0))],
        out_specs=[pl.BlockSpec((1, blk, HEAD_PAD), lambda h, i: (h, i, 0)),
                   pl.BlockSpec((1, 1, blk), lambda h, i: (h, 0, i))],
        out_shape=[jax.ShapeDtypeStruct((HEADS, t, HEAD_PAD), F32), jax.ShapeDtypeStruct((HEADS, 1, t), F32)],
        scratch_shapes=[square(F32), square(F32), square(BF16), square(BF16), stat, stat, stat, stat, stat,
                        pltpu.VMEM((HEAD_PAD, blk), F32)],
        compiler_params=_cparams(("arbitrary", "arbitrary")),
    )(q, k, vt)


def _mid(x, target, proj, ol, w_out, ws_low, ws_low_t, bsp, sgu_g, sgu_b, ln_g, ln_b):
    t = x.shape[0]
    tm = TOKEN_TILE
    n_steps = t // tm

    def body(x_ref, tgt_ref, za_ref, u_ref, v_ref, zb_ref, ol_ref, wout_ref, ws_ref, wst_ref, bsp_ref,
             sg_ref, sb_ref, lg_ref, lb_ref,
             dr_ref, do_ref, drow_ref, drest_ref, dwout_ref, dws_ref, dbs_ref, dlg_ref, dlb_ref, dsg_ref, dsb_ref,
             loss_ref, dbsp_acc):
        step = pl.program_id(0)

        @pl.when(step == 0)
        def _():
            dwout_ref[...] = jnp.zeros_like(dwout_ref)
            dws_ref[...] = jnp.zeros_like(dws_ref)
            dbs_ref[...] = jnp.zeros_like(dbs_ref)
            dlg_ref[...] = jnp.zeros_like(dlg_ref)
            dlb_ref[...] = jnp.zeros_like(dlb_ref)
            dsg_ref[...] = jnp.zeros_like(dsg_ref)
            dsb_ref[...] = jnp.zeros_like(dsb_ref)
            loss_ref[...] = jnp.zeros_like(loss_ref)
            dbsp_acc[...] = jnp.zeros_like(dbsp_acc)

        lane_head = lax.broadcasted_iota(jnp.int32, (CHUNK, G_WIDTH), 1) // G_HEAD_DIM

        attn = jnp.concatenate([ol_ref[h][:, :VDIM] for h in range(HEADS)], axis=-1)
        za = za_ref[...]
        sig_a = _sigmoid(za)
        silu_a = za * sig_a
        out_a = attn * silu_a
        u = u_ref[...]
        ug = _gelu(u)
        vpre = v_ref[...]
        gv = _gelu(vpre)
        mu_v = jnp.mean(gv, axis=-1, keepdims=True)
        cen_v = gv - mu_v
        rstd_v = lax.rsqrt(jnp.mean(cen_v * cen_v, axis=-1, keepdims=True) + EPS)
        vhat = cen_v * rstd_v
        vg = vhat * sg_ref[...] + sb_ref[...]
        vg_b = vg.astype(BF16)
        sv_parts = []
        for cix in range(tm // CHUNK):
            vc = vg_b[cix * CHUNK:(cix + 1) * CHUNK, :]
            acc = bsp_ref[...]
            for h in range(HEADS):
                acc = acc + jnp.where(lane_head == h, _dot(ws_ref[h], vc), 0.0)
            sv_parts.append(acc)
        sv = jnp.concatenate(sv_parts, axis=0)
        sgu = ug * sv
        zb = zb_ref[...]
        sig_b = _sigmoid(zb)
        silu_b = zb * sig_b
        out_b = sgu * silu_b
        merged = jnp.concatenate([out_a, out_b], axis=-1).astype(BF16)
        r = DN_ALPHA * x_ref[...] + _dot(merged, wout_ref[...])
        mu = jnp.mean(r, axis=-1, keepdims=True)
        cen = r - mu
        rstd = lax.rsqrt(jnp.mean(cen * cen, axis=-1, keepdims=True) + EPS)
        xhat = cen * rstd
        hout = xhat * lg_ref[...] + lb_ref[...]
        err = hout - tgt_ref[...]
        row_loss = jnp.mean(err * err, axis=-1, keepdims=True)
        loss_ref[...] += jnp.broadcast_to(0.5 * jnp.sum(row_loss, axis=0, keepdims=True), loss_ref.shape)

        dh = err * (1.0 / D_MODEL)
        dlg_ref[...] += jnp.sum(dh * xhat, axis=0, keepdims=True)
        dlb_ref[...] += jnp.sum(dh, axis=0, keepdims=True)
        dxhat = dh * lg_ref[...]
        dr = rstd * (dxhat - jnp.mean(dxhat, axis=-1, keepdims=True)
                     - xhat * jnp.mean(dxhat * xhat, axis=-1, keepdims=True))
        dr_ref[...] = dr
        dr_b = dr.astype(BF16)
        dwout_ref[...] += _dot_tn(merged, dr_b)
        dmerged = _dot_nt(dr_b, wout_ref[...])
        d_out_a = dmerged[:, :G_WIDTH]
        d_out_b = dmerged[:, G_WIDTH:]
        dattn = d_out_a * silu_a
        for h in range(HEADS):
            do_h = dattn[:, h * VDIM:(h + 1) * VDIM]
            dsum = jnp.sum(do_h * ol_ref[h][:, :VDIM], axis=-1, keepdims=True)
            drow_ref[h] = _as_row(dsum)
            do_ref[h] = jnp.concatenate([do_h, jnp.zeros((tm, HEAD_PAD - VDIM), F32)], axis=-1).astype(BF16)
        dza = d_out_a * attn * (sig_a * (1.0 + za * (1.0 - sig_a)))
        dsgu = d_out_b * silu_b
        dzb = d_out_b * sgu * (sig_b * (1.0 + zb * (1.0 - sig_b)))
        du = dsgu * sv * _gelu_grad(u)
        dsv = dsgu * ug
        dsv_b = dsv.astype(BF16)
        dvg_parts = []
        for cix in range(tm // CHUNK):
            rows = slice(cix * CHUNK, (cix + 1) * CHUNK)
            dsv_c = dsv[rows, :]
            dsv_cb = dsv_b[rows, :]
            vc = vg_b[rows, :]
            dbsp_acc[...] += dsv_c
            acc = jnp.zeros((CHUNK, G_WIDTH), F32)
            for h in range(HEADS):
                on = lane_head == h
                acc = acc + jnp.where(on, _dot(wst_ref[h], dsv_cb), 0.0)
                dws_ref[h] += _dot_nt(jnp.where(on, dsv_cb, jnp.zeros_like(dsv_cb)), vc)
            dvg_parts.append(acc)
        dvg = jnp.concatenate(dvg_parts, axis=0)
        dsg_ref[...] += jnp.sum(dvg * vhat, axis=0, keepdims=True)
        dsb_ref[...] += jnp.sum(dvg, axis=0, keepdims=True)
        dvhat = dvg * sg_ref[...]
        dgv = rstd_v * (dvhat - jnp.mean(dvhat, axis=-1, keepdims=True)
                        - vhat * jnp.mean(dvhat * vhat, axis=-1, keepdims=True))
        dv = dgv * _gelu_grad(vpre)
        drest_ref[...] = jnp.concatenate([dza, du, dv, dzb], axis=-1).astype(BF16)

        @pl.when(step == n_steps - 1)
        def _():
            tri = (lax.broadcasted_iota(jnp.int32, (CHUNK, CHUNK), 0)
                   >= lax.broadcasted_iota(jnp.int32, (CHUNK, CHUNK), 1))
            for h in range(HEADS):
                dws_ref[h] = jnp.where(tri, dws_ref[h], 0.0)
            tot = dbsp_acc[...]
            lane = lax.broadcasted_iota(jnp.int32, (CHUNK, LANES), 1)
            dbs = jnp.zeros((CHUNK, LANES), F32)
            for h in range(HEADS):
                head_sum = jnp.sum(tot[:, h * G_HEAD_DIM:(h + 1) * G_HEAD_DIM], axis=-1, keepdims=True)
                dbs = jnp.where(lane == h, head_sum, dbs)
            dbs_ref[...] = dbs

    full = lambda a: pl.BlockSpec(a.shape, lambda i: (0,) * a.ndim)
    tile = lambda w, j=0: pl.BlockSpec((tm, w), lambda i, j=j: (i, j))
    heads = pl.BlockSpec((HEADS, tm, HEAD_PAD), lambda i: (0, i, 0))
    acc = lambda shape: (pl.BlockSpec(shape, lambda i: (0,) * len(shape)), jax.ShapeDtypeStruct(shape, F32))
    accs = [acc((D_MODEL, D_MODEL)), acc((HEADS, CHUNK, CHUNK)), acc((CHUNK, LANES)), acc((1, D_MODEL)),
            acc((1, D_MODEL)), acc((1, G_WIDTH)), acc((1, G_WIDTH)), acc((1, LANES))]
    return pl.pallas_call(
        body, name="mid", grid=(n_steps,),
        in_specs=[tile(D_MODEL), tile(D_MODEL), tile(G_WIDTH, 1), tile(G_WIDTH, 2), tile(G_WIDTH, 3), tile(G_WIDTH, 4),
                  heads, full(w_out), full(ws_low), full(ws_low_t), full(bsp), full(sgu_g), full(sgu_b),
                  full(ln_g), full(ln_b)],
        out_specs=[tile(D_MODEL), heads, pl.BlockSpec((HEADS, 1, tm), lambda i: (0, 0, i)), tile(4 * G_WIDTH)]
        + [a[0] for a in accs],
        out_shape=[jax.ShapeDtypeStruct((t, D_MODEL), F32), jax.ShapeDtypeStruct((HEADS, t, HEAD_PAD), BF16),
                   jax.ShapeDtypeStruct((HEADS, 1, t), F32), jax.ShapeDtypeStruct((t, 4 * G_WIDTH), BF16)]
        + [a[1] for a in accs],
        scratch_shapes=[pltpu.VMEM((CHUNK, G_WIDTH), F32)],
        compiler_params=_cparams(("arbitrary",)),
    )(x, target, proj, proj, proj, proj, ol, w_out, ws_low, ws_low_t, bsp, sgu_g, sgu_b, ln_g, ln_b)


def _attn_bwd(q, k, v, do, lse_row, d_row):
    t = q.shape[1]
    blk = ATTN_BLOCK
    nblk = t // blk
    last = nblk - 1
    chunk = SOFTMAX_ROWS

    def body(q_ref, k_ref, v_ref, do_ref, lse_ref, drow_ref, dqt_ref, dk_ref, dv_ref,
             s0, s1, e0, e1, p0, p1, g0, g1, kt_scr):
        j = pl.program_id(1)
        at = lambda i: pl.ds(pl.multiple_of(i * blk, blk), blk)

        @pl.when(j == 0)
        def _():
            dqt_ref[...] = jnp.zeros_like(dqt_ref)

        kt_scr[...] = jnp.transpose(k_ref[0].astype(F32)).astype(BF16)
        dk_ref[...] = jnp.zeros_like(dk_ref)
        dv_ref[...] = jnp.zeros_like(dv_ref)

        def products(i, s_out, e_out):
            i = jnp.minimum(i, last)
            s_out[...] = _dot_nt(k_ref[0], q_ref[0, at(i), :])
            e_out[...] = _dot_nt(v_ref[0], do_ref[0, at(i), :])

        def gradients(i, p_in, g_in):
            dv_ref[0] += _dot(p_in[...], do_ref[0, at(i), :])
            dk_ref[0] += _dot(g_in[...], q_ref[0, at(i), :])
            dqt_ref[0, :, at(i)] += _dot(kt_scr[...], g_in[...])

        def elementwise(i, s_in, e_in, p_out, g_out, masked):
            lse = lse_ref[0, :, at(i)]
            dsum = drow_ref[0, :, at(i)]
            for r in range(0, blk, chunk):
                p = jnp.exp2(s_in[r:r + chunk, :] - lse)
                if masked:
                    key = lax.broadcasted_iota(jnp.int32, (chunk, blk), 0) + r
                    qry = lax.broadcasted_iota(jnp.int32, (chunk, blk), 1)
                    p = jnp.where(qry >= key, p, 0.0)
                p_out[r:r + chunk, :] = p.astype(BF16)
                g_out[r:r + chunk, :] = (p * (e_in[r:r + chunk, :] - dsum)).astype(BF16)

        def one_pass(i, s_in, e_in, s_out, e_out, p_prev, g_prev, p_cur, g_cur):
            products(i + 1, s_out, e_out)
            gradients(i - 1, p_prev, g_prev)
            elementwise(i, s_in, e_in, p_cur, g_cur, False)

        products(j, s0, e0)
        products(j + 1, s1, e1)
        elementwise(j, s0, e0, p0, g0, True)
        rest = last - j

        def two_passes(n, _):
            i = j + 1 + 2 * n
            one_pass(i, s1, e1, s0, e0, p0, g0, p1, g1)
            one_pass(i + 1, s0, e0, s1, e1, p1, g1, p0, g0)
            return 0

        lax.fori_loop(0, rest // 2, two_passes, 0)

        @pl.when(rest % 2 == 1)
        def _():
            one_pass(last, s1, e1, s0, e0, p0, g0, p1, g1)
            gradients(last, p1, g1)

        @pl.when(rest % 2 == 0)
        def _():
            gradients(last, p0, g0)

        dk_ref[0] = dk_ref[0] * LN2

    whole = pl.BlockSpec((1, t, HEAD_PAD), lambda h, j: (h, 0, 0))
    block = pl.BlockSpec((1, blk, HEAD_PAD), lambda h, j: (h, j, 0))
    rows = pl.BlockSpec((1, 1, t), lambda h, j: (h, 0, 0))
    shape = jax.ShapeDtypeStruct((HEADS, t, HEAD_PAD), F32)
    square = lambda dtype: pltpu.VMEM((blk, blk), dtype)
    return pl.pallas_call(
        body, name="attn_bwd", grid=(HEADS, nblk),
        in_specs=[whole, block, block, whole, rows, rows],
        out_specs=[pl.BlockSpec((1, HEAD_PAD, t), lambda h, j: (h, 0, 0)), block, block],
        out_shape=[jax.ShapeDtypeStruct((HEADS, HEAD_PAD, t), F32), shape, shape],
        scratch_shapes=[square(F32), square(F32), square(F32), square(F32), square(BF16), square(BF16),
                        square(BF16), square(BF16), pltpu.VMEM((HEAD_PAD, blk), BF16)],
        compiler_params=_cparams(("arbitrary", "arbitrary")),
    )(q, k, v, do, lse_row, d_row)


def _bwd_qkv(dq, dk, dv, proj, pos_col, invf_row, wp_uq, wp_ukv, q_g, kv_g):
    t = proj.shape[0]
    tm = TOKEN_TILE

    def body(dq_ref, dk_ref, dv_ref, ph_ref, pos_ref, invf_ref, wuq_ref, wukv_ref, qg_ref, kvg_ref,
             dhead_ref, dwuq_ref, dwukv_ref, dqg_ref, dkvg_ref):
        @pl.when(pl.program_id(0) == 0)
        def _():
            dwuq_ref[...] = jnp.zeros_like(dwuq_ref)
            dwukv_ref[...] = jnp.zeros_like(dwukv_ref)
            dqg_ref[...] = jnp.zeros_like(dqg_ref)
            dkvg_ref[...] = jnp.zeros_like(dkvg_ref)

        cos, s1, s2 = _rope_tables(pos_ref[...], invf_ref[...])
        dq_full = jnp.concatenate(
            [_rope(jnp.transpose(dq_ref[h]) * ATTN_SCALE, cos, s1, s2, -1.0) for h in range(HEADS)], axis=-1)
        dkv_full = jnp.concatenate([dk_ref[h] for h in range(HEADS)] + [dv_ref[h] for h in range(HEADS)], axis=-1)
        dkr_rot = dk_ref[0]
        for h in range(1, HEADS):
            dkr_rot = dkr_rot + dk_ref[h]
        lane = lax.broadcasted_iota(jnp.int32, (tm, LANES), 1)
        rot_lanes = (lane >= KR_LO) & (lane < KR_LO + ROPE)
        dkr_raw = jnp.where(rot_lanes, _rope(dkr_rot, cos, s1, s2, -1.0), 0.0)

        c_q = ph_ref[:, :Q_LORA]
        c_kv = ph_ref[:, Q_LORA:Q_LORA + KV_LORA]
        rstd_q = lax.rsqrt(jnp.mean(c_q * c_q, axis=-1, keepdims=True) + EPS)
        rstd_kv = lax.rsqrt(jnp.mean(c_kv * c_kv, axis=-1, keepdims=True) + EPS)
        qhat = c_q * rstd_q
        kvhat = c_kv * rstd_kv
        dq_b = dq_full.astype(BF16)
        dkv_b = dkv_full.astype(BF16)
        dwuq_ref[...] += _dot_tn((qhat * qg_ref[...]).astype(BF16), dq_b)
        dwukv_ref[...] += _dot_tn((kvhat * kvg_ref[...]).astype(BF16), dkv_b)
        dcqn = _dot_nt(dq_b, wuq_ref[...])
        dckvn = _dot_nt(dkv_b, wukv_ref[...])
        dqg_ref[...] += jnp.sum(dcqn * qhat, axis=0, keepdims=True)
        dkvg_ref[...] += jnp.sum(dckvn * kvhat, axis=0, keepdims=True)
        dqh = dcqn * qg_ref[...]
        dkvh = dckvn * kvg_ref[...]
        dc_q = rstd_q * (dqh - qhat * jnp.mean(dqh * qhat, axis=-1, keepdims=True))
        dc_kv = rstd_kv * (dkvh - kvhat * jnp.mean(dkvh * kvhat, axis=-1, keepdims=True))
        dhead_ref[...] = jnp.concatenate([dc_q, dc_kv, dkr_raw], axis=-1).astype(BF16)

    full = lambda a: pl.BlockSpec(a.shape, lambda i: (0,) * a.ndim)
    heads = pl.BlockSpec((HEADS, tm, HEAD_PAD), lambda i: (0, i, 0))
    acc = lambda shape: (pl.BlockSpec(shape, lambda i: (0,) * len(shape)), jax.ShapeDtypeStruct(shape, F32))
    accs = [acc((Q_LORA, HEADS * HEAD_PAD)), acc((KV_LORA, 2 * HEADS * HEAD_PAD)), acc((1, Q_LORA)), acc((1, KV_LORA))]
    return pl.pallas_call(
        body, name="bwd_qkv", grid=(t // tm,),
        in_specs=[pl.BlockSpec((HEADS, HEAD_PAD, tm), lambda i: (0, 0, i)), heads, heads,
                  pl.BlockSpec((tm, 4 * LANES), lambda i: (i, 0)),
                  pl.BlockSpec((tm, 1), lambda i: (i, 0)), full(invf_row), full(wp_uq), full(wp_ukv),
                  full(q_g), full(kv_g)],
        out_specs=[pl.BlockSpec((tm, 4 * LANES), lambda i: (i, 0))] + [a[0] for a in accs],
        out_shape=[jax.ShapeDtypeStruct((t, 4 * LANES), BF16)] + [a[1] for a in accs],
        compiler_params=_cparams(("arbitrary",)),
    )(dq, dk, dv, proj, pos_col, invf_row, wp_uq, wp_ukv, q_g, kv_g)


def _bwd_in(x, dr, dhead, drest, wp_in):
    t = x.shape[0]
    tm = TOKEN_TILE
    n_head = dhead.shape[1]

    def body(x_ref, dr_ref, dhead_ref, drest_ref, win_ref, gx_ref, dwin_ref):
        @pl.when(pl.program_id(0) == 0)
        def _():
            dwin_ref[...] = jnp.zeros_like(dwin_ref)

        xb = x_ref[...].astype(BF16)
        dh_b = dhead_ref[...]
        dr_b = drest_ref[...]
        gx_ref[...] = (DN_ALPHA * dr_ref[...] + _dot_nt(dh_b, win_ref[:, :n_head])
                       + _dot_nt(dr_b, win_ref[:, n_head:]))
        dwin_ref[:, :n_head] += _dot_tn(xb, dh_b)
        dwin_ref[:, n_head:] += _dot_tn(xb, dr_b)

    tile = lambda w: pl.BlockSpec((tm, w), lambda i: (i, 0))
    whole = pl.BlockSpec(wp_in.shape, lambda i: (0, 0))
    return pl.pallas_call(
        body, name="bwd_in", grid=(t // tm,),
        in_specs=[tile(D_MODEL), tile(D_MODEL), tile(n_head), tile(drest.shape[1]), whole],
        out_specs=[tile(D_MODEL), whole],
        out_shape=[jax.ShapeDtypeStruct((t, D_MODEL), F32), jax.ShapeDtypeStruct(wp_in.shape, F32)],
        compiler_params=_cparams(("arbitrary",)),
    )(x, dr, dhead, drest, wp_in)


def _adam(parts, w, m, v, *, name, tile_rows):
    n, rows, _ = parts.shape

    def body(p_ref, w_ref, m_ref, v_ref, g_ref, d_ref, nm_ref, nv_ref):
        g = p_ref[0]
        for s in range(1, n):
            g = g + p_ref[s]
        m_new = ADAM_B1 * m_ref[...] + (1.0 - ADAM_B1) * g
        v_new = ADAM_B2 * v_ref[...] + (1.0 - ADAM_B2) * (g * g)
        m_hat = m_new / (1.0 - ADAM_B1 ** ADAM_STEP)
        v_hat = v_new / (1.0 - ADAM_B2 ** ADAM_STEP)
        g_ref[...] = g
        d_ref[...] = -ADAM_LR * (m_hat / (jnp.sqrt(v_hat) + ADAM_EPS) + ADAM_WD * w_ref[...])
        nm_ref[...] = m_new
        nv_ref[...] = v_new

    flat = pl.BlockSpec((tile_rows, LANES), lambda i: (i, 0))
    shape = jax.ShapeDtypeStruct((rows, LANES), F32)
    return pl.pallas_call(
        body, name=name, grid=(rows // tile_rows,),
        in_specs=[pl.BlockSpec((n, tile_rows, LANES), lambda i: (0, i, 0)), flat, flat, flat],
        out_specs=[flat] * 4, out_shape=[shape] * 4,
        compiler_params=_cparams(("arbitrary",)),
    )(parts, w, m, v)


SMALL_NAMES = ("q_norm_g", "kv_norm_g", "sgu_norm_g", "sgu_norm_b", "b_spatial", "ln_g", "ln_b")
SMALL_SIZES = (Q_LORA, KV_LORA, G_WIDTH, G_WIDTH, HEADS * CHUNK, D_MODEL, D_MODEL)


def _pack_small(vals):
    flat = jnp.concatenate([v.reshape(-1) for v in vals])
    return jnp.pad(flat, (0, SMALL_LEN - flat.shape[0]))


def _unpack_small(flat):
    out, at = [], 0
    for n in SMALL_SIZES:
        out.append(flat[at:at + n])
        at += n
    out[4] = out[4].reshape(HEADS, CHUNK)
    return out


def _rows(a):
    return a.reshape(-1, LANES)


SHARD_ROWS = (D_MODEL * (D_IN // N_DEV) // LANES, Q_LORA * (HEADS * (NOPE + ROPE) // N_DEV) // LANES,
              KV_LORA * (HEADS * (NOPE + VDIM) // N_DEV) // LANES, (D_MODEL // N_DEV) * D_MODEL // LANES)
SHARD_SHAPES = ((D_MODEL, D_IN // N_DEV), (Q_LORA, HEADS * (NOPE + ROPE) // N_DEV),
                (KV_LORA, HEADS * (NOPE + VDIM) // N_DEV), (D_MODEL // N_DEV, D_MODEL))
W_ROWS = sum(SHARD_ROWS)


def _pack_owned(w_in, w_uq, w_ukv, w_out, w_sp_head, small_chunk):
    body = jnp.concatenate([_rows(w_in), _rows(w_uq), _rows(w_ukv), _rows(w_out), _rows(w_sp_head), _rows(small_chunk)])
    return jnp.pad(body, ((0, SLAB_ROWS - body.shape[0]), (0, 0)))


def _unpack_owned(slab):
    out, at = [], 0
    for n, shape in zip(SHARD_ROWS, SHARD_SHAPES):
        out.append(slab[at:at + n].reshape(shape))
        at += n
    return out


def _weights_for_kernels(w_in, w_uq, w_ukv, w_out):
    z = lambda r, c: jnp.zeros((r, c), BF16)
    split = Q_LORA + KV_LORA
    wp_in = jnp.concatenate([w_in[:, :split], z(D_MODEL, KR_LO), w_in[:, split:split + ROPE],
                             z(D_MODEL, LANES - KR_LO - ROPE), w_in[:, split + ROPE:]], axis=1)
    wp_uq = jnp.pad(w_uq.reshape(Q_LORA, HEADS, NOPE + ROPE), ((0, 0), (0, 0), (0, HEAD_PAD - NOPE - ROPE)))
    wp_uq = wp_uq.reshape(Q_LORA, HEADS * HEAD_PAD)
    kv = w_ukv.reshape(KV_LORA, HEADS, NOPE + VDIM)
    pad = ((0, 0), (0, 0), (0, HEAD_PAD - NOPE))
    wp_ukv = jnp.concatenate([jnp.pad(kv[:, :, :NOPE], pad).reshape(KV_LORA, -1),
                              jnp.pad(kv[:, :, NOPE:], pad).reshape(KV_LORA, -1)], axis=1)
    return wp_in, wp_uq, wp_ukv, w_out


def _grads_from_kernels(dwp_in, dwp_uq, dwp_ukv):
    split = Q_LORA + KV_LORA
    d_in = jnp.concatenate([dwp_in[:, :split], dwp_in[:, split + KR_LO:split + KR_LO + ROPE],
                            dwp_in[:, split + LANES:]], axis=1)
    d_uq = dwp_uq.reshape(Q_LORA, HEADS, HEAD_PAD)[:, :, :NOPE + ROPE].reshape(Q_LORA, -1)
    kv = dwp_ukv.reshape(KV_LORA, 2, HEADS, HEAD_PAD)
    d_ukv = jnp.concatenate([kv[:, 0, :, :NOPE], kv[:, 1, :, :VDIM]], axis=-1).reshape(KV_LORA, -1)
    return d_in, d_uq, d_ukv


def _column_shards(a):
    r = a.shape[0]
    return a.reshape(r, N_DEV, -1).transpose(1, 0, 2)


def kernel(x, positions, w_in, q_norm_g, w_uq, kv_norm_g, w_ukv, sgu_norm_g, sgu_norm_b, w_spatial, b_spatial, w_out, ln_g, ln_b, loss_target, m_w_in, m_q_norm_g, m_w_uq, m_kv_norm_g, m_w_ukv, m_sgu_norm_g, m_sgu_norm_b, m_w_spatial, m_b_spatial, m_w_out, m_ln_g, m_ln_b, v_w_in, v_q_norm_g, v_w_uq, v_kv_norm_g, v_w_ukv, v_sgu_norm_g, v_sgu_norm_b, v_w_spatial, v_b_spatial, v_w_out, v_ln_g, v_ln_b):
    me = 4 * lax.axis_index("x") + 2 * lax.axis_index("y") + lax.axis_index("c")
    seq = x.shape[1]
    x2 = x.reshape(seq, D_MODEL)
    tgt2 = loss_target.reshape(seq, D_MODEL)
    pos_col = positions.reshape(seq, 1)

    mine = jnp.concatenate([_rows(w_in), _rows(w_uq), _rows(w_ukv), _rows(w_out)]).astype(BF16)
    gathered = _exchange(mine, name="wgather", per_destination=False)
    parts, at = [], 0
    for n, shape in zip(SHARD_ROWS, SHARD_SHAPES):
        parts.append(gathered[:, at:at + n].reshape((N_DEV,) + shape))
        at += n
    full_in = parts[0].transpose(1, 0, 2).reshape(D_MODEL, D_IN)
    full_uq = parts[1].transpose(1, 0, 2).reshape(Q_LORA, -1)
    full_ukv = parts[2].transpose(1, 0, 2).reshape(KV_LORA, -1)
    full_out = parts[3].reshape(D_MODEL, D_MODEL)
    (loss_part, grad_x, d_in, d_uq, d_ukv, d_out, d_ws, d_bs_t, d_lng, d_lnb, d_sgug, d_sgub, d_qg, d_kvg) = _local_step(
        x2, tgt2, pos_col, full_in, full_uq, full_ukv, full_out, q_norm_g, kv_norm_g, sgu_norm_g, sgu_norm_b,
        w_spatial, b_spatial, ln_g, ln_b)

    small_part = _pack_small([d_qg, d_kvg, d_sgug, d_sgub, d_bs_t[:, :HEADS].T, d_lng, d_lnb]).reshape(N_DEV, -1, LANES)
    body = jnp.concatenate([_column_shards(d_in).reshape(N_DEV, -1, LANES),
                            _column_shards(d_uq).reshape(N_DEV, -1, LANES),
                            _column_shards(d_ukv).reshape(N_DEV, -1, LANES),
                            d_out.reshape(N_DEV, -1, LANES), d_ws.reshape(N_DEV, -1, LANES), small_part], axis=1)
    to_send = jnp.pad(body, ((0, 0), (0, SLAB_ROWS - body.shape[1]), (0, 0)))
    received = _exchange(to_send, name="gexch", per_destination=True)

    take = lambda a: lax.dynamic_index_in_dim(a, me, 0, keepdims=False)
    small_w = _pack_small([q_norm_g, kv_norm_g, sgu_norm_g, sgu_norm_b, b_spatial, ln_g, ln_b])
    small_m = _pack_small([m_q_norm_g, m_kv_norm_g, m_sgu_norm_g, m_sgu_norm_b, m_b_spatial, m_ln_g, m_ln_b])
    small_v = _pack_small([v_q_norm_g, v_kv_norm_g, v_sgu_norm_g, v_sgu_norm_b, v_b_spatial, v_ln_g, v_ln_b])
    chunk = lambda a: take(a.reshape(N_DEV, -1))
    own_w = _pack_owned(w_in, w_uq, w_ukv, w_out, take(w_spatial), chunk(small_w))
    own_m = _pack_owned(m_w_in, m_w_uq, m_w_ukv, m_w_out, take(m_w_spatial), chunk(small_m))
    own_v = _pack_owned(v_w_in, v_w_uq, v_w_ukv, v_w_out, take(v_w_spatial), chunk(small_v))
    g_own, delta_own, m_own, v_own = _adam(received, own_w, own_m, own_v, name="adam", tile_rows=512)

    rep_g = _exchange(g_own[W_ROWS:W_ROWS + REP_ROWS], name="sgather", per_destination=False)
    rep_pack = lambda sp, small: jnp.concatenate(
        [sp.reshape(N_DEV, CHUNK, LANES), small.reshape(N_DEV, -1, LANES)], axis=1).reshape(-1, LANES)
    _, delta_rep, m_rep, v_rep = _adam(rep_g.reshape(1, N_DEV * REP_ROWS, LANES), rep_pack(w_spatial, small_w),
                                       rep_pack(m_w_spatial, small_m), rep_pack(v_w_spatial, small_v),
                                       name="adam_rep", tile_rows=N_DEV * REP_ROWS)

    def rep_unpack(a):
        a = a.reshape(N_DEV, REP_ROWS, LANES)
        small = _unpack_small(a[:, CHUNK:].reshape(-1))
        return [small[0], small[1], small[2], small[3], a[:, :CHUNK], small[4], small[5], small[6]]

    def ordered(owned, rep):
        o_in, o_uq, o_ukv, o_out = _unpack_owned(owned)
        r_qg, r_kvg, r_sg, r_sb, r_ws, r_bs, r_lg, r_lb = rep_unpack(rep)
        return [o_in, r_qg, o_uq, r_kvg, o_ukv, r_sg, r_sb, r_ws, r_bs, o_out, r_lg, r_lb]

    loss = lax.psum(loss_part[0, 0], ("x", "y", "c"))
    outs = [loss, grad_x.reshape(x.shape)]
    outs += ordered(g_own, rep_g.reshape(-1, LANES))
    outs += ordered(delta_own, delta_rep)
    outs += ordered(m_own, m_rep)
    outs += ordered(v_own, v_rep)
    return tuple(outs)


def _local_step(x2, tgt2, pos_col, full_in, full_uq, full_ukv, full_out, q_norm_g, kv_norm_g, sgu_norm_g, sgu_norm_b,
                w_spatial, b_spatial, ln_g, ln_b):
    wp_in, wp_uq, wp_ukv, wb_out = _weights_for_kernels(full_in, full_uq, full_ukv, full_out)

    half = jnp.arange(HALF, dtype=F32)
    inv_freq = 1.0 / (ROPE_THETA ** (half / HALF))
    invf_row = jnp.concatenate([jnp.zeros((KR_LO,), F32), inv_freq, inv_freq,
                                jnp.zeros((LANES - KR_LO - ROPE,), F32)]).reshape(1, LANES)
    tri = jnp.tril(jnp.ones((CHUNK, CHUNK), dtype=bool))
    ws_low = jnp.where(tri[None], w_spatial, 0.0).astype(BF16)
    ws_low_t = ws_low.transpose(0, 2, 1)
    bsp = jnp.repeat(b_spatial.T, G_HEAD_DIM, axis=1)
    row = lambda a: a.reshape(1, -1)

    proj, q, k, v, vt = _fwd_proj(x2, pos_col, invf_row, wp_in, wp_uq, wp_ukv, row(q_norm_g), row(kv_norm_g))
    o, lse_row = _attn_fwd(q, k, vt)
    (dr, do, d_row, drest, d_out, d_ws, d_bs_t, d_lng, d_lnb, d_sgug, d_sgub, loss_part) = _mid(
        x2, tgt2, proj, o, wb_out, ws_low, ws_low_t, bsp, row(sgu_norm_g), row(sgu_norm_b), row(ln_g), row(ln_b))
    dqt, dk, dv = _attn_bwd(q, k, v, do, lse_row, d_row)
    dhead, dwp_uq, dwp_ukv, d_qg, d_kvg = _bwd_qkv(dqt, dk, dv, proj, pos_col, invf_row, wp_uq, wp_ukv,
                                                   row(q_norm_g), row(kv_norm_g))
    grad_x, dwp_in = _bwd_in(x2, dr, dhead, drest, wp_in)
    d_in, d_uq, d_ukv = _grads_from_kernels(dwp_in, dwp_uq, dwp_ukv)
    return loss_part, grad_x, d_in, d_uq, d_ukv, d_out, d_ws, d_bs_t, d_lng, d_lnb, d_sgug, d_sgub, d_qg, d_kvg
```

```python
import functools
import math

import jax
import jax.numpy as jnp
from jax import lax
from jax.experimental import pallas as pl
from jax.experimental.pallas import tpu as pltpu

F32 = jnp.float32
BF16 = jnp.bfloat16

N_DEV = 8
D_MODEL = 1024
HEADS = 8
NOPE = 64
ROPE = 32
HALF = ROPE // 2
VDIM = 64
Q_LORA = 256
KV_LORA = 128
G_WIDTH = 512
G_HEAD_DIM = 64
CHUNK = 128
HEAD_PAD = 128
D_IN = 2464
D_IN_PAD = 2560
KR_LO = NOPE
ROPE_THETA = 10000.0
DN_ALPHA = 2.0 ** 0.25
EPS = 1e-5
ATTN_SCALE = 1.0 / math.sqrt(NOPE + ROPE)
ADAM_LR, ADAM_B1, ADAM_B2, ADAM_EPS, ADAM_WD, ADAM_STEP = 0.001, 0.9, 0.999, 1e-08, 0.01, 10

LANES = 128
SLAB_ROWS = 4096
REP_ROWS = 136
SMALL_LEN = 8192
VMEM_LIMIT = 56 * 1024 * 1024

TOKEN_TILE = 256
ATTN_WIDE = 1024
ATTN_NARROW = 512
SOFTMAX_ROWS = 256
LOG2E = 1.4426950408889634
LN2 = 0.6931471805599453
Q_PRESCALE = ATTN_SCALE * LOG2E


def _cparams(sem=None):
    return pltpu.CompilerParams(dimension_semantics=sem, vmem_limit_bytes=VMEM_LIMIT)


def _dot(a, b):
    return jnp.dot(a, b, preferred_element_type=F32)


def _dot_nt(a, b):
    return lax.dot_general(a, b, (((1,), (1,)), ((), ())), preferred_element_type=F32)


def _dot_tn(a, b):
    return lax.dot_general(a, b, (((0,), (0,)), ((), ())), preferred_element_type=F32)


def _as_row(col):
    return jnp.transpose(jnp.broadcast_to(col, (col.shape[0], LANES)))[0:1, :]


def _sigmoid(z):
    return 1.0 / (1.0 + jnp.exp(-z))


def _gelu(x):
    return 0.5 * x * (1.0 + lax.erf(x * 0.7071067811865476))


def _gelu_grad(x):
    cdf = 0.5 * (1.0 + lax.erf(x * 0.7071067811865476))
    return cdf + x * jnp.exp(-0.5 * x * x) * 0.3989422804014327


def _exchange(src, *, name, per_destination):
    rows = src.shape[-2]

    def body(src_ref, out_ref, send_sems, recv_sems, local_sem):
        x, y, c = lax.axis_index("x"), lax.axis_index("y"), lax.axis_index("c")
        me = 4 * x + 2 * y + c

        def slab_for(dest):
            return src_ref.at[dest] if per_destination else src_ref

        mine = pltpu.make_async_copy(slab_for(me), out_ref.at[me], local_sem)
        mine.start()
        copies = []
        for k in (6, 7, 4, 5, 2, 3, 1):
            px = 1 - x if k & 4 else x
            py = 1 - y if k & 2 else y
            pc = 1 - c if k & 1 else c
            peer = 4 * px + 2 * py + pc
            cp = pltpu.make_async_remote_copy(
                src_ref=slab_for(peer), dst_ref=out_ref.at[me],
                send_sem=send_sems.at[k - 1], recv_sem=recv_sems.at[k - 1],
                device_id=(px, py, pc), device_id_type=pl.DeviceIdType.MESH)
            cp.start()
            copies.append((k, peer, cp))
        for k, peer, cp in copies:
            pltpu.make_async_remote_copy(
                src_ref=slab_for(peer), dst_ref=out_ref.at[peer],
                send_sem=send_sems.at[k - 1], recv_sem=recv_sems.at[k - 1],
                device_id=(x, y, c), device_id_type=pl.DeviceIdType.MESH).wait_recv()
        for _, _, cp in copies:
            cp.wait_send()
        mine.wait()

    return pl.pallas_call(
        body, name=name,
        out_shape=jax.ShapeDtypeStruct((N_DEV, rows, LANES), src.dtype),
        in_specs=[pl.BlockSpec(memory_space=pl.ANY)],
        out_specs=pl.BlockSpec(memory_space=pl.ANY),
        scratch_shapes=[pltpu.SemaphoreType.DMA((N_DEV - 1,)), pltpu.SemaphoreType.DMA((N_DEV - 1,)),
                        pltpu.SemaphoreType.DMA(())],
    )(src)


def _rope_tables(pos_col, invf_row):
    ang = pos_col.astype(F32) * invf_row
    lane = lax.broadcasted_iota(jnp.int32, ang.shape, 1)
    cos, sin = jnp.cos(ang), jnp.sin(ang)
    first = (lane >= KR_LO) & (lane < KR_LO + HALF)
    second = (lane >= KR_LO + HALF) & (lane < KR_LO + ROPE)
    return cos, jnp.where(first, sin, 0.0), jnp.where(second, sin, 0.0)


def _rope(t, cos, sin_first, sin_second, sign):
    up = pltpu.roll(t, LANES - HALF, 1)
    down = pltpu.roll(t, HALF, 1)
    return t * cos - sign * (up * sin_first) + sign * (down * sin_second)


def _fwd_proj(x, pos_col, invf_row, wp_in, wp_uq, wp_ukv, q_g, kv_g):
    t = x.shape[0]
    tm = TOKEN_TILE

    def body(x_ref, pos_ref, invf_ref, win_ref, wuq_ref, wukv_ref, qg_ref, kvg_ref,
             proj_ref, q_ref, k_ref, v_ref, vt_ref):
        proj = _dot(x_ref[...].astype(BF16), win_ref[...])
        proj_ref[...] = proj
        c_q = proj[:, :Q_LORA]
        c_kv = proj[:, Q_LORA:Q_LORA + KV_LORA]
        kr_raw = proj[:, Q_LORA + KV_LORA:Q_LORA + KV_LORA + LANES]
        cqn = c_q * lax.rsqrt(jnp.mean(c_q * c_q, axis=-1, keepdims=True) + EPS) * qg_ref[...]
        ckvn = c_kv * lax.rsqrt(jnp.mean(c_kv * c_kv, axis=-1, keepdims=True) + EPS) * kvg_ref[...]
        q_full = _dot(cqn.astype(BF16), wuq_ref[...])
        kv_full = _dot(ckvn.astype(BF16), wukv_ref[...])
        cos, s1, s2 = _rope_tables(pos_ref[...], invf_ref[...])
        kr = _rope(kr_raw, cos, s1, s2, 1.0)
        for h in range(HEADS):
            lo = h * HEAD_PAD
            q_ref[h] = (_rope(q_full[:, lo:lo + HEAD_PAD], cos, s1, s2, 1.0) * Q_PRESCALE).astype(BF16)
            k_ref[h] = (kv_full[:, lo:lo + HEAD_PAD] + kr).astype(BF16)
            v_h = kv_full[:, HEADS * HEAD_PAD + lo:HEADS * HEAD_PAD + lo + HEAD_PAD]
            v_ref[h] = v_h.astype(BF16)
            vt_ref[h] = jnp.transpose(v_h).astype(BF16)

    full = lambda a: pl.BlockSpec(a.shape, lambda i: (0,) * a.ndim)
    head_spec = pl.BlockSpec((HEADS, tm, HEAD_PAD), lambda i: (0, i, 0))
    head_shape = jax.ShapeDtypeStruct((HEADS, t, HEAD_PAD), BF16)
    return pl.pallas_call(
        body, name="fwd_proj", grid=(t // tm,),
        in_specs=[pl.BlockSpec((tm, D_MODEL), lambda i: (i, 0)), pl.BlockSpec((tm, 1), lambda i: (i, 0)),
                  full(invf_row), full(wp_in), full(wp_uq), full(wp_ukv), full(q_g), full(kv_g)],
        out_specs=[pl.BlockSpec((tm, D_IN_PAD), lambda i: (i, 0)), head_spec, head_spec, head_spec,
                   pl.BlockSpec((HEADS, HEAD_PAD, tm), lambda i: (0, 0, i))],
        out_shape=[jax.ShapeDtypeStruct((t, D_IN_PAD), F32), head_shape, head_shape, head_shape,
                   jax.ShapeDtypeStruct((HEADS, HEAD_PAD, t), BF16)],
        compiler_params=_cparams(("arbitrary",)),
    )(x, pos_col, invf_row, wp_in, wp_uq, wp_ukv, q_g, kv_g)


def _attn_fwd(q, k, vt):
    t = q.shape[1]
    bq, bk = ATTN_WIDE, ATTN_NARROW
    chunk = SOFTMAX_ROWS

    def body(q_ref, k_ref, vt_ref, o_ref, lse_ref, s0, s1, p0, p1, x0, x1, m_scr, l_scr, a_scr, acc_scr):
        i = pl.program_id(1)
        at = lambda j: pl.ds(pl.multiple_of(j * bk, bk), bk)

        def exp_pass(s_in, block_max, p_out, key0=None):
            def load(r):
                s = s_in[r:r + chunk, :]
                if key0 is not None:
                    key = lax.broadcasted_iota(jnp.int32, (chunk, bq), 0) + (r + key0)
                    qry = lax.broadcasted_iota(jnp.int32, (chunk, bq), 1)
                    s = jnp.where(qry >= key, s, -jnp.inf)
                return s

            if key0 is not None:
                block_max = jnp.max(load(0), axis=0, keepdims=True)
                for r in range(chunk, bk, chunk):
                    block_max = jnp.maximum(block_max, jnp.max(load(r), axis=0, keepdims=True))
            m_new = jnp.maximum(m_scr[...], block_max)
            alpha = jnp.exp2(m_scr[...] - m_new)
            total = jnp.zeros((1, bq), F32)
            for r in range(0, bk, chunk):
                p = jnp.exp2(load(r) - m_new)
                p_out[r:r + chunk, :] = p.astype(BF16)
                total = total + jnp.sum(p, axis=0, keepdims=True)
            m_scr[...] = m_new
            l_scr[...] = alpha * l_scr[...] + total
            return alpha

        def scores(j, s_out, x_out):
            s = _dot_nt(k_ref[0, at(j), :], q_ref[0])
            s_out[...] = s
            x_out[...] = jnp.max(s, axis=0, keepdims=True)

        def value_product(j, p_in):
            return _dot(vt_ref[0, :, at(j)], p_in[...])

        def one_pass(j, s_in, x_in, s_out, x_out, p_prev, p_cur):
            scores(j + 1, s_out, x_out)
            acc_scr[...] = a_scr[...] * acc_scr[...] + value_product(jnp.maximum(j - 1, 0), p_prev)
            a_scr[...] = exp_pass(s_in, x_in[...], p_cur)

        scores(0, s0, x0)
        p1[...] = jnp.zeros_like(p1)
        a_scr[...] = jnp.ones_like(a_scr)
        m_scr[...] = jnp.full(m_scr.shape, -jnp.inf, F32)
        l_scr[...] = jnp.zeros_like(l_scr)
        acc_scr[...] = jnp.zeros_like(acc_scr)

        def two_passes(n, _):
            one_pass(2 * n, s0, x0, s1, x1, p1, p0)
            one_pass(2 * n + 1, s1, x1, s0, x0, p0, p1)
            return 0

        lax.fori_loop(0, i, two_passes, 0)
        d = 2 * i
        scores(d + 1, s1, x1)
        acc = a_scr[...] * acc_scr[...] + value_product(jnp.maximum(d - 1, 0), p1)
        alpha = exp_pass(s0, None, p0, key0=0)
        acc = alpha * acc + value_product(d, p0)
        alpha = exp_pass(s1, None, p1, key0=bk)
        acc = alpha * acc + value_product(d + 1, p1)
        o_ref[0] = jnp.transpose(acc / l_scr[...])
        lse_ref[0] = m_scr[...] + jnp.log2(l_scr[...])

    tile = lambda dtype: pltpu.VMEM((bk, bq), dtype)
    stat = pltpu.VMEM((1, bq), F32)
    return pl.pallas_call(
        body, name="attn_fwd", grid=(HEADS, t // bq),
        in_specs=[pl.BlockSpec((1, bq, HEAD_PAD), lambda h, i: (h, i, 0)),
                  pl.BlockSpec((1, t, HEAD_PAD), lambda h, i: (h, 0, 0)),
                  pl.BlockSpec((1, HEAD_PAD, t), lambda h, i: (h, 0, 0))],
        out_specs=[pl.BlockSpec((1, bq, HEAD_PAD), lambda h, i: (h, i, 0)),
                   pl.BlockSpec((1, 1, bq), lambda h, i: (h, 0, i))],
        out_shape=[jax.ShapeDtypeStruct((HEADS, t, HEAD_PAD), F32), jax.ShapeDtypeStruct((HEADS, 1, t), F32)],
        scratch_shapes=[tile(F32), tile(F32), tile(BF16), tile(BF16), stat, stat, stat, stat, stat,
                        pltpu.VMEM((HEAD_PAD, bq), F32)],
        compiler_params=_cparams(("arbitrary", "arbitrary")),
    )(q, k, vt)


def _mid(x, target, proj, ol, w_out, ws_low, ws_low_t, bsp, sgu_g, sgu_b, ln_g, ln_b):
    t = x.shape[0]
    tm = TOKEN_TILE
    n_steps = t // tm

    def body(x_ref, tgt_ref, za_ref, u_ref, v_ref, zb_ref, ol_ref, wout_ref, ws_ref, wst_ref, bsp_ref,
             sg_ref, sb_ref, lg_ref, lb_ref,
             dr_ref, do_ref, drow_ref, drest_ref, dwout_ref, dws_ref, dbs_ref, dlg_ref, dlb_ref, dsg_ref, dsb_ref,
             loss_ref, dbsp_acc):
        step = pl.program_id(0)

        @pl.when(step == 0)
        def _():
            dwout_ref[...] = jnp.zeros_like(dwout_ref)
            dws_ref[...] = jnp.zeros_like(dws_ref)
            dbs_ref[...] = jnp.zeros_like(dbs_ref)
            dlg_ref[...] = jnp.zeros_like(dlg_ref)
            dlb_ref[...] = jnp.zeros_like(dlb_ref)
            dsg_ref[...] = jnp.zeros_like(dsg_ref)
            dsb_ref[...] = jnp.zeros_like(dsb_ref)
            loss_ref[...] = jnp.zeros_like(loss_ref)
            dbsp_acc[...] = jnp.zeros_like(dbsp_acc)

        lane_head = lax.broadcasted_iota(jnp.int32, (CHUNK, G_WIDTH), 1) // G_HEAD_DIM

        attn = jnp.concatenate([ol_ref[h][:, :VDIM] for h in range(HEADS)], axis=-1)
        za = za_ref[...]
        sig_a = _sigmoid(za)
        silu_a = za * sig_a
        out_a = attn * silu_a
        u = u_ref[...]
        ug = _gelu(u)
        vpre = v_ref[...]
        gv = _gelu(vpre)
        mu_v = jnp.mean(gv, axis=-1, keepdims=True)
        cen_v = gv - mu_v
        rstd_v = lax.rsqrt(jnp.mean(cen_v * cen_v, axis=-1, keepdims=True) + EPS)
        vhat = cen_v * rstd_v
        vg = vhat * sg_ref[...] + sb_ref[...]
        vg_b = vg.astype(BF16)
        sv_parts = []
        for cix in range(tm // CHUNK):
            vc = vg_b[cix * CHUNK:(cix + 1) * CHUNK, :]
            acc = bsp_ref[...]
            for h in range(HEADS):
                acc = acc + jnp.where(lane_head == h, _dot(ws_ref[h], vc), 0.0)
            sv_parts.append(acc)
        sv = jnp.concatenate(sv_parts, axis=0)
        sgu = ug * sv
        zb = zb_ref[...]
        sig_b = _sigmoid(zb)
        silu_b = zb * sig_b
        out_b = sgu * silu_b
        merged = jnp.concatenate([out_a, out_b], axis=-1).astype(BF16)
        r = DN_ALPHA * x_ref[...] + _dot(merged, wout_ref[...])
        mu = jnp.mean(r, axis=-1, keepdims=True)
        cen = r - mu
        rstd = lax.rsqrt(jnp.mean(cen * cen, axis=-1, keepdims=True) + EPS)
        xhat = cen * rstd
        hout = xhat * lg_ref[...] + lb_ref[...]
        err = hout - tgt_ref[...]
        row_loss = jnp.mean(err * err, axis=-1, keepdims=True)
        loss_ref[...] += jnp.broadcast_to(0.5 * jnp.sum(row_loss, axis=0, keepdims=True), loss_ref.shape)

        dh = err * (1.0 / D_MODEL)
        dlg_ref[...] += jnp.sum(dh * xhat, axis=0, keepdims=True)
        dlb_ref[...] += jnp.sum(dh, axis=0, keepdims=True)
        dxhat = dh * lg_ref[...]
        dr = rstd * (dxhat - jnp.mean(dxhat, axis=-1, keepdims=True)
                     - xhat * jnp.mean(dxhat * xhat, axis=-1, keepdims=True))
        dr_ref[...] = dr
        dr_b = dr.astype(BF16)
        dwout_ref[...] += _dot_tn(merged, dr_b)
        dmerged = _dot_nt(dr_b, wout_ref[...])
        d_out_a = dmerged[:, :G_WIDTH]
        d_out_b = dmerged[:, G_WIDTH:]
        dattn = d_out_a * silu_a
        for h in range(HEADS):
            do_h = dattn[:, h * VDIM:(h + 1) * VDIM]
            dsum = jnp.sum(do_h * ol_ref[h][:, :VDIM], axis=-1, keepdims=True)
            drow_ref[h] = _as_row(dsum)
            do_ref[h] = jnp.concatenate([do_h, jnp.zeros((tm, HEAD_PAD - VDIM), F32)], axis=-1).astype(BF16)
        dza = d_out_a * attn * (sig_a * (1.0 + za * (1.0 - sig_a)))
        dsgu = d_out_b * silu_b
        dzb = d_out_b * sgu * (sig_b * (1.0 + zb * (1.0 - sig_b)))
        du = dsgu * sv * _gelu_grad(u)
        dsv = dsgu * ug
        dsv_b = dsv.astype(BF16)
        dvg_parts = []
        for cix in range(tm // CHUNK):
            rows = slice(cix * CHUNK, (cix + 1) * CHUNK)
            dsv_c = dsv[rows, :]
            dsv_cb = dsv_b[rows, :]
            vc = vg_b[rows, :]
            dbsp_acc[...] += dsv_c
            acc = jnp.zeros((CHUNK, G_WIDTH), F32)
            for h in range(HEADS):
                on = lane_head == h
                acc = acc + jnp.where(on, _dot(wst_ref[h], dsv_cb), 0.0)
                dws_ref[h] += _dot_nt(jnp.where(on, dsv_cb, jnp.zeros_like(dsv_cb)), vc)
            dvg_parts.append(acc)
        dvg = jnp.concatenate(dvg_parts, axis=0)
        dsg_ref[...] += jnp.sum(dvg * vhat, axis=0, keepdims=True)
        dsb_ref[...] += jnp.sum(dvg, axis=0, keepdims=True)
        dvhat = dvg * sg_ref[...]
        dgv = rstd_v * (dvhat - jnp.mean(dvhat, axis=-1, keepdims=True)
                        - vhat * jnp.mean(dvhat * vhat, axis=-1, keepdims=True))
        dv = dgv * _gelu_grad(vpre)
        drest_ref[...] = jnp.concatenate([dza, du, dv, dzb], axis=-1).astype(BF16)

        @pl.when(step == n_steps - 1)
        def _():
            tri = (lax.broadcasted_iota(jnp.int32, (CHUNK, CHUNK), 0)
                   >= lax.broadcasted_iota(jnp.int32, (CHUNK, CHUNK), 1))
            for h in range(HEADS):
                dws_ref[h] = jnp.where(tri, dws_ref[h], 0.0)
            tot = dbsp_acc[...]
            lane = lax.broadcasted_iota(jnp.int32, (CHUNK, LANES), 1)
            dbs = jnp.zeros((CHUNK, LANES), F32)
            for h in range(HEADS):
                head_sum = jnp.sum(tot[:, h * G_HEAD_DIM:(h + 1) * G_HEAD_DIM], axis=-1, keepdims=True)
                dbs = jnp.where(lane == h, head_sum, dbs)
            dbs_ref[...] = dbs

    full = lambda a: pl.BlockSpec(a.shape, lambda i: (0,) * a.ndim)
    tile = lambda w, j=0: pl.BlockSpec((tm, w), lambda i, j=j: (i, j))
    heads = pl.BlockSpec((HEADS, tm, HEAD_PAD), lambda i: (0, i, 0))
    acc = lambda shape: (pl.BlockSpec(shape, lambda i: (0,) * len(shape)), jax.ShapeDtypeStruct(shape, F32))
    accs = [acc((D_MODEL, D_MODEL)), acc((HEADS, CHUNK, CHUNK)), acc((CHUNK, LANES)), acc((1, D_MODEL)),
            acc((1, D_MODEL)), acc((1, G_WIDTH)), acc((1, G_WIDTH)), acc((1, LANES))]
    return pl.pallas_call(
        body, name="mid", grid=(n_steps,),
        in_specs=[tile(D_MODEL), tile(D_MODEL), tile(G_WIDTH, 1), tile(G_WIDTH, 2), tile(G_WIDTH, 3), tile(G_WIDTH, 4),
                  heads, full(w_out), full(ws_low), full(ws_low_t), full(bsp), full(sgu_g), full(sgu_b),
                  full(ln_g), full(ln_b)],
        out_specs=[tile(D_MODEL), heads, pl.BlockSpec((HEADS, 1, tm), lambda i: (0, 0, i)), tile(4 * G_WIDTH)]
        + [a[0] for a in accs],
        out_shape=[jax.ShapeDtypeStruct((t, D_MODEL), F32), jax.ShapeDtypeStruct((HEADS, t, HEAD_PAD), BF16),
                   jax.ShapeDtypeStruct((HEADS, 1, t), F32), jax.ShapeDtypeStruct((t, 4 * G_WIDTH), BF16)]
        + [a[1] for a in accs],
        scratch_shapes=[pltpu.VMEM((CHUNK, G_WIDTH), F32)],
        compiler_params=_cparams(("arbitrary",)),
    )(x, target, proj, proj, proj, proj, ol, w_out, ws_low, ws_low_t, bsp, sgu_g, sgu_b, ln_g, ln_b)


def _attn_bwd(q, k, v, do, lse_row, d_row):
    t = q.shape[1]
    bk, bq = ATTN_WIDE, ATTN_NARROW
    last = t // bq - 1
    chunk = SOFTMAX_ROWS

    def body(q_ref, k_ref, v_ref, do_ref, lse_ref, drow_ref, dqt_ref, dk_ref, dv_ref,
             s0, s1, e0, e1, p0, p1, g0, g1, kt_scr):
        j = pl.program_id(1)
        at = lambda i: pl.ds(pl.multiple_of(i * bq, bq), bq)

        @pl.when(j == 0)
        def _():
            dqt_ref[...] = jnp.zeros_like(dqt_ref)

        kt_scr[...] = jnp.transpose(k_ref[0].astype(F32)).astype(BF16)
        dk_ref[...] = jnp.zeros_like(dk_ref)
        dv_ref[...] = jnp.zeros_like(dv_ref)

        def products(i, s_out, e_out):
            i = jnp.minimum(i, last)
            s_out[...] = _dot_nt(k_ref[0], q_ref[0, at(i), :])
            e_out[...] = _dot_nt(v_ref[0], do_ref[0, at(i), :])

        def gradients(i, p_in, g_in):
            dv_ref[0] += _dot(p_in[...], do_ref[0, at(i), :])
            dk_ref[0] += _dot(g_in[...], q_ref[0, at(i), :])
            dqt_ref[0, :, at(i)] += _dot(kt_scr[...], g_in[...])

        def elementwise(i, s_in, e_in, p_out, g_out, qry0=None):
            lse = lse_ref[0, :, at(i)]
            dsum = drow_ref[0, :, at(i)]
            for r in range(0, bk, chunk):
                p = jnp.exp2(s_in[r:r + chunk, :] - lse)
                if qry0 is not None:
                    key = lax.broadcasted_iota(jnp.int32, (chunk, bq), 0) + r
                    qry = lax.broadcasted_iota(jnp.int32, (chunk, bq), 1) + qry0
                    p = jnp.where(qry >= key, p, 0.0)
                p_out[r:r + chunk, :] = p.astype(BF16)
                g_out[r:r + chunk, :] = (p * (e_in[r:r + chunk, :] - dsum)).astype(BF16)

        def one_pass(i, s_in, e_in, s_out, e_out, p_prev, g_prev, p_cur, g_cur, qry0=None):
            products(i + 1, s_out, e_out)
            gradients(i - 1, p_prev, g_prev)
            elementwise(i, s_in, e_in, p_cur, g_cur, qry0)

        first = 2 * j
        products(first, s0, e0)
        products(first + 1, s1, e1)
        elementwise(first, s0, e0, p0, g0, qry0=0)
        one_pass(first + 1, s1, e1, s0, e0, p0, g0, p1, g1, qry0=bq)

        def two_passes(n, _):
            i = first + 2 + 2 * n
            one_pass(i, s0, e0, s1, e1, p1, g1, p0, g0)
            one_pass(i + 1, s1, e1, s0, e0, p0, g0, p1, g1)
            return 0

        lax.fori_loop(0, (last - first - 1) // 2, two_passes, 0)
        gradients(last, p1, g1)
        dk_ref[0] = dk_ref[0] * LN2

    whole = pl.BlockSpec((1, t, HEAD_PAD), lambda h, j: (h, 0, 0))
    block = pl.BlockSpec((1, bk, HEAD_PAD), lambda h, j: (h, j, 0))
    rows = pl.BlockSpec((1, 1, t), lambda h, j: (h, 0, 0))
    shape = jax.ShapeDtypeStruct((HEADS, t, HEAD_PAD), F32)
    tile = lambda dtype: pltpu.VMEM((bk, bq), dtype)
    return pl.pallas_call(
        body, name="attn_bwd", grid=(HEADS, t // bk),
        in_specs=[whole, block, block, whole, rows, rows],
        out_specs=[pl.BlockSpec((1, HEAD_PAD, t), lambda h, j: (h, 0, 0)), block, block],
        out_shape=[jax.ShapeDtypeStruct((HEADS, HEAD_PAD, t), F32), shape, shape],
        scratch_shapes=[tile(F32), tile(F32), tile(F32), tile(F32), tile(BF16), tile(BF16),
                        tile(BF16), tile(BF16), pltpu.VMEM((HEAD_PAD, bk), BF16)],
        compiler_params=_cparams(("arbitrary", "arbitrary")),
    )(q, k, v, do, lse_row, d_row)


def _bwd_qkv(dq, dk, dv, proj, pos_col, invf_row, wp_uq, wp_ukv, q_g, kv_g):
    t = proj.shape[0]
    tm = TOKEN_TILE

    def body(dq_ref, dk_ref, dv_ref, ph_ref, pos_ref, invf_ref, wuq_ref, wukv_ref, qg_ref, kvg_ref,
             dhead_ref, dwuq_ref, dwukv_ref, dqg_ref, dkvg_ref):
        @pl.when(pl.program_id(0) == 0)
        def _():
            dwuq_ref[...] = jnp.zeros_like(dwuq_ref)
            dwukv_ref[...] = jnp.zeros_like(dwukv_ref)
            dqg_ref[...] = jnp.zeros_like(dqg_ref)
            dkvg_ref[...] = jnp.zeros_like(dkvg_ref)

        cos, s1, s2 = _rope_tables(pos_ref[...], invf_ref[...])
        dq_full = jnp.concatenate(
            [_rope(jnp.transpose(dq_ref[h]) * ATTN_SCALE, cos, s1, s2, -1.0) for h in range(HEADS)], axis=-1)
        dkv_full = jnp.concatenate([dk_ref[h] for h in range(HEADS)] + [dv_ref[h] for h in range(HEADS)], axis=-1)
        dkr_rot = dk_ref[0]
        for h in range(1, HEADS):
            dkr_rot = dkr_rot + dk_ref[h]
        lane = lax.broadcasted_iota(jnp.int32, (tm, LANES), 1)
        rot_lanes = (lane >= KR_LO) & (lane < KR_LO + ROPE)
        dkr_raw = jnp.where(rot_lanes, _rope(dkr_rot, cos, s1, s2, -1.0), 0.0)

        c_q = ph_ref[:, :Q_LORA]
        c_kv = ph_ref[:, Q_LORA:Q_LORA + KV_LORA]
        rstd_q = lax.rsqrt(jnp.mean(c_q * c_q, axis=-1, keepdims=True) + EPS)
        rstd_kv = lax.rsqrt(jnp.mean(c_kv * c_kv, axis=-1, keepdims=True) + EPS)
        qhat = c_q * rstd_q
        kvhat = c_kv * rstd_kv
        dq_b = dq_full.astype(BF16)
        dkv_b = dkv_full.astype(BF16)
        dwuq_ref[...] += _dot_tn((qhat * qg_ref[...]).astype(BF16), dq_b)
        dwukv_ref[...] += _dot_tn((kvhat * kvg_ref[...]).astype(BF16), dkv_b)
        dcqn = _dot_nt(dq_b, wuq_ref[...])
        dckvn = _dot_nt(dkv_b, wukv_ref[...])
        dqg_ref[...] += jnp.sum(dcqn * qhat, axis=0, keepdims=True)
        dkvg_ref[...] += jnp.sum(dckvn * kvhat, axis=0, keepdims=True)
        dqh = dcqn * qg_ref[...]
        dkvh = dckvn * kvg_ref[...]
        dc_q = rstd_q * (dqh - qhat * jnp.mean(dqh * qhat, axis=-1, keepdims=True))
        dc_kv = rstd_kv * (dkvh - kvhat * jnp.mean(dkvh * kvhat, axis=-1, keepdims=True))
        dhead_ref[...] = jnp.concatenate([dc_q, dc_kv, dkr_raw], axis=-1).astype(BF16)

    full = lambda a: pl.BlockSpec(a.shape, lambda i: (0,) * a.ndim)
    heads = pl.BlockSpec((HEADS, tm, HEAD_PAD), lambda i: (0, i, 0))
    acc = lambda shape: (pl.BlockSpec(shape, lambda i: (0,) * len(shape)), jax.ShapeDtypeStruct(shape, F32))
    accs = [acc((Q_LORA, HEADS * HEAD_PAD)), acc((KV_LORA, 2 * HEADS * HEAD_PAD)), acc((1, Q_LORA)), acc((1, KV_LORA))]
    return pl.pallas_call(
        body, name="bwd_qkv", grid=(t // tm,),
        in_specs=[pl.BlockSpec((HEADS, HEAD_PAD, tm), lambda i: (0, 0, i)), heads, heads,
                  pl.BlockSpec((tm, 4 * LANES), lambda i: (i, 0)),
                  pl.BlockSpec((tm, 1), lambda i: (i, 0)), full(invf_row), full(wp_uq), full(wp_ukv),
                  full(q_g), full(kv_g)],
        out_specs=[pl.BlockSpec((tm, 4 * LANES), lambda i: (i, 0))] + [a[0] for a in accs],
        out_shape=[jax.ShapeDtypeStruct((t, 4 * LANES), BF16)] + [a[1] for a in accs],
        compiler_params=_cparams(("arbitrary",)),
    )(dq, dk, dv, proj, pos_col, invf_row, wp_uq, wp_ukv, q_g, kv_g)


def _bwd_in(x, dr, dhead, drest, wp_in):
    t = x.shape[0]
    tm = TOKEN_TILE
    n_head = dhead.shape[1]

    def body(x_ref, dr_ref, dhead_ref, drest_ref, win_ref, gx_ref, dwin_ref):
        @pl.when(pl.program_id(0) == 0)
        def _():
            dwin_ref[...] = jnp.zeros_like(dwin_ref)

        xb = x_ref[...].astype(BF16)
        dh_b = dhead_ref[...]
        dr_b = drest_ref[...]
        gx_ref[...] = (DN_ALPHA * dr_ref[...] + _dot_nt(dh_b, win_ref[:, :n_head])
                       + _dot_nt(dr_b, win_ref[:, n_head:]))
        dwin_ref[:, :n_head] += _dot_tn(xb, dh_b)
        dwin_ref[:, n_head:] += _dot_tn(xb, dr_b)

    tile = lambda w: pl.BlockSpec((tm, w), lambda i: (i, 0))
    whole = pl.BlockSpec(wp_in.shape, lambda i: (0, 0))
    return pl.pallas_call(
        body, name="bwd_in", grid=(t // tm,),
        in_specs=[tile(D_MODEL), tile(D_MODEL), tile(n_head), tile(drest.shape[1]), whole],
        out_specs=[tile(D_MODEL), whole],
        out_shape=[jax.ShapeDtypeStruct((t, D_MODEL), F32), jax.ShapeDtypeStruct(wp_in.shape, F32)],
        compiler_params=_cparams(("arbitrary",)),
    )(x, dr, dhead, drest, wp_in)


def _adam(parts, w, m, v, *, name, tile_rows):
    n, rows, _ = parts.shape

    def body(p_ref, w_ref, m_ref, v_ref, g_ref, d_ref, nm_ref, nv_ref):
        g = p_ref[0]
        for s in range(1, n):
            g = g + p_ref[s]
        m_new = ADAM_B1 * m_ref[...] + (1.0 - ADAM_B1) * g
        v_new = ADAM_B2 * v_ref[...] + (1.0 - ADAM_B2) * (g * g)
        m_hat = m_new / (1.0 - ADAM_B1 ** ADAM_STEP)
        v_hat = v_new / (1.0 - ADAM_B2 ** ADAM_STEP)
        g_ref[...] = g
        d_ref[...] = -ADAM_LR * (m_hat / (jnp.sqrt(v_hat) + ADAM_EPS) + ADAM_WD * w_ref[...])
        nm_ref[...] = m_new
        nv_ref[...] = v_new

    flat = pl.BlockSpec((tile_rows, LANES), lambda i: (i, 0))
    shape = jax.ShapeDtypeStruct((rows, LANES), F32)
    return pl.pallas_call(
        body, name=name, grid=(rows // tile_rows,),
        in_specs=[pl.BlockSpec((n, tile_rows, LANES), lambda i: (0, i, 0)), flat, flat, flat],
        out_specs=[flat] * 4, out_shape=[shape] * 4,
        compiler_params=_cparams(("arbitrary",)),
    )(parts, w, m, v)


SMALL_NAMES = ("q_norm_g", "kv_norm_g", "sgu_norm_g", "sgu_norm_b", "b_spatial", "ln_g", "ln_b")
SMALL_SIZES = (Q_LORA, KV_LORA, G_WIDTH, G_WIDTH, HEADS * CHUNK, D_MODEL, D_MODEL)


def _pack_small(vals):
    flat = jnp.concatenate([v.reshape(-1) for v in vals])
    return jnp.pad(flat, (0, SMALL_LEN - flat.shape[0]))


def _unpack_small(flat):
    out, at = [], 0
    for n in SMALL_SIZES:
        out.append(flat[at:at + n])
        at += n
    out[4] = out[4].reshape(HEADS, CHUNK)
    return out


def _rows(a):
    return a.reshape(-1, LANES)


SHARD_ROWS = (D_MODEL * (D_IN // N_DEV) // LANES, Q_LORA * (HEADS * (NOPE + ROPE) // N_DEV) // LANES,
              KV_LORA * (HEADS * (NOPE + VDIM) // N_DEV) // LANES, (D_MODEL // N_DEV) * D_MODEL // LANES)
SHARD_SHAPES = ((D_MODEL, D_IN // N_DEV), (Q_LORA, HEADS * (NOPE + ROPE) // N_DEV),
                (KV_LORA, HEADS * (NOPE + VDIM) // N_DEV), (D_MODEL // N_DEV, D_MODEL))
W_ROWS = sum(SHARD_ROWS)


def _pack_owned(w_in, w_uq, w_ukv, w_out, w_sp_head, small_chunk):
    body = jnp.concatenate([_rows(w_in), _rows(w_uq), _rows(w_ukv), _rows(w_out), _rows(w_sp_head), _rows(small_chunk)])
    return jnp.pad(body, ((0, SLAB_ROWS - body.shape[0]), (0, 0)))


def _unpack_owned(slab):
    out, at = [], 0
    for n, shape in zip(SHARD_ROWS, SHARD_SHAPES):
        out.append(slab[at:at + n].reshape(shape))
        at += n
    return out


def _weights_for_kernels(w_in, w_uq, w_ukv, w_out):
    z = lambda r, c: jnp.zeros((r, c), BF16)
    split = Q_LORA + KV_LORA
    wp_in = jnp.concatenate([w_in[:, :split], z(D_MODEL, KR_LO), w_in[:, split:split + ROPE],
                             z(D_MODEL, LANES - KR_LO - ROPE), w_in[:, split + ROPE:]], axis=1)
    wp_uq = jnp.pad(w_uq.reshape(Q_LORA, HEADS, NOPE + ROPE), ((0, 0), (0, 0), (0, HEAD_PAD - NOPE - ROPE)))
    wp_uq = wp_uq.reshape(Q_LORA, HEADS * HEAD_PAD)
    kv = w_ukv.reshape(KV_LORA, HEADS, NOPE + VDIM)
    pad = ((0, 0), (0, 0), (0, HEAD_PAD - NOPE))
    wp_ukv = jnp.concatenate([jnp.pad(kv[:, :, :NOPE], pad).reshape(KV_LORA, -1),
                              jnp.pad(kv[:, :, NOPE:], pad).reshape(KV_LORA, -1)], axis=1)
    return wp_in, wp_uq, wp_ukv, w_out


def _grads_from_kernels(dwp_in, dwp_uq, dwp_ukv):
    split = Q_LORA + KV_LORA
    d_in = jnp.concatenate([dwp_in[:, :split], dwp_in[:, split + KR_LO:split + KR_LO + ROPE],
                            dwp_in[:, split + LANES:]], axis=1)
    d_uq = dwp_uq.reshape(Q_LORA, HEADS, HEAD_PAD)[:, :, :NOPE + ROPE].reshape(Q_LORA, -1)
    kv = dwp_ukv.reshape(KV_LORA, 2, HEADS, HEAD_PAD)
    d_ukv = jnp.concatenate([kv[:, 0, :, :NOPE], kv[:, 1, :, :VDIM]], axis=-1).reshape(KV_LORA, -1)
    return d_in, d_uq, d_ukv


def _column_shards(a):
    r = a.shape[0]
    return a.reshape(r, N_DEV, -1).transpose(1, 0, 2)


def kernel(x, positions, w_in, q_norm_g, w_uq, kv_norm_g, w_ukv, sgu_norm_g, sgu_norm_b, w_spatial, b_spatial, w_out, ln_g, ln_b, loss_target, m_w_in, m_q_norm_g, m_w_uq, m_kv_norm_g, m_w_ukv, m_sgu_norm_g, m_sgu_norm_b, m_w_spatial, m_b_spatial, m_w_out, m_ln_g, m_ln_b, v_w_in, v_q_norm_g, v_w_uq, v_kv_norm_g, v_w_ukv, v_sgu_norm_g, v_sgu_norm_b, v_w_spatial, v_b_spatial, v_w_out, v_ln_g, v_ln_b):
    me = 4 * lax.axis_index("x") + 2 * lax.axis_index("y") + lax.axis_index("c")
    seq = x.shape[1]
    x2 = x.reshape(seq, D_MODEL)
    tgt2 = loss_target.reshape(seq, D_MODEL)
    pos_col = positions.reshape(seq, 1)

    mine = jnp.concatenate([_rows(w_in), _rows(w_uq), _rows(w_ukv), _rows(w_out)]).astype(BF16)
    gathered = _exchange(mine, name="wgather", per_destination=False)
    parts, at = [], 0
    for n, shape in zip(SHARD_ROWS, SHARD_SHAPES):
        parts.append(gathered[:, at:at + n].reshape((N_DEV,) + shape))
        at += n
    full_in = parts[0].transpose(1, 0, 2).reshape(D_MODEL, D_IN)
    full_uq = parts[1].transpose(1, 0, 2).reshape(Q_LORA, -1)
    full_ukv = parts[2].transpose(1, 0, 2).reshape(KV_LORA, -1)
    full_out = parts[3].reshape(D_MODEL, D_MODEL)
    (loss_part, grad_x, d_in, d_uq, d_ukv, d_out, d_ws, d_bs_t, d_lng, d_lnb, d_sgug, d_sgub, d_qg, d_kvg) = _local_step(
        x2, tgt2, pos_col, full_in, full_uq, full_ukv, full_out, q_norm_g, kv_norm_g, sgu_norm_g, sgu_norm_b,
        w_spatial, b_spatial, ln_g, ln_b)

    small_part = _pack_small([d_qg, d_kvg, d_sgug, d_sgub, d_bs_t[:, :HEADS].T, d_lng, d_lnb]).reshape(N_DEV, -1, LANES)
    body = jnp.concatenate([_column_shards(d_in).reshape(N_DEV, -1, LANES),
                            _column_shards(d_uq).reshape(N_DEV, -1, LANES),
                            _column_shards(d_ukv).reshape(N_DEV, -1, LANES),
                            d_out.reshape(N_DEV, -1, LANES), d_ws.reshape(N_DEV, -1, LANES), small_part], axis=1)
    to_send = jnp.pad(body, ((0, 0), (0, SLAB_ROWS - body.shape[1]), (0, 0)))
    received = _exchange(to_send, name="gexch", per_destination=True)

    take = lambda a: lax.dynamic_index_in_dim(a, me, 0, keepdims=False)
    small_w = _pack_small([q_norm_g, kv_norm_g, sgu_norm_g, sgu_norm_b, b_spatial, ln_g, ln_b])
    small_m = _pack_small([m_q_norm_g, m_kv_norm_g, m_sgu_norm_g, m_sgu_norm_b, m_b_spatial, m_ln_g, m_ln_b])
    small_v = _pack_small([v_q_norm_g, v_kv_norm_g, v_sgu_norm_g, v_sgu_norm_b, v_b_spatial, v_ln_g, v_ln_b])
    chunk = lambda a: take(a.reshape(N_DEV, -1))
    own_w = _pack_owned(w_in, w_uq, w_ukv, w_out, take(w_spatial), chunk(small_w))
    own_m = _pack_owned(m_w_in, m_w_uq, m_w_ukv, m_w_out, take(m_w_spatial), chunk(small_m))
    own_v = _pack_owned(v_w_in, v_w_uq, v_w_ukv, v_w_out, take(v_w_spatial), chunk(small_v))
    g_own, delta_own, m_own, v_own = _adam(received, own_w, own_m, own_v, name="adam", tile_rows=512)

    rep_g = _exchange(g_own[W_ROWS:W_ROWS + REP_ROWS], name="sgather", per_destination=False)
    rep_pack = lambda sp, small: jnp.concatenate(
        [sp.reshape(N_DEV, CHUNK, LANES), small.reshape(N_DEV, -1, LANES)], axis=1).reshape(-1, LANES)
    _, delta_rep, m_rep, v_rep = _adam(rep_g.reshape(1, N_DEV * REP_ROWS, LANES), rep_pack(w_spatial, small_w),
                                       rep_pack(m_w_spatial, small_m), rep_pack(v_w_spatial, small_v),
                                       name="adam_rep", tile_rows=N_DEV * REP_ROWS)

    def rep_unpack(a):
        a = a.reshape(N_DEV, REP_ROWS, LANES)
        small = _unpack_small(a[:, CHUNK:].reshape(-1))
        return [small[0], small[1], small[2], small[3], a[:, :CHUNK], small[4], small[5], small[6]]

    def ordered(owned, rep):
        o_in, o_uq, o_ukv, o_out = _unpack_owned(owned)
        r_qg, r_kvg, r_sg, r_sb, r_ws, r_bs, r_lg, r_lb = rep_unpack(rep)
        return [o_in, r_qg, o_uq, r_kvg, o_ukv, r_sg, r_sb, r_ws, r_bs, o_out, r_lg, r_lb]

    loss = lax.psum(loss_part[0, 0], ("x", "y", "c"))
    outs = [loss, grad_x.reshape(x.shape)]
    outs += ordered(g_own, rep_g.reshape(-1, LANES))
    outs += ordered(delta_own, delta_rep)
    outs += ordered(m_own, m_rep)
    outs += ordered(v_own, v_rep)
    return tuple(outs)


def _local_step(x2, tgt2, pos_col, full_in, full_uq, full_ukv, full_out, q_norm_g, kv_norm_g, sgu_norm_g, sgu_norm_b,
                w_spatial, b_spatial, ln_g, ln_b):
    wp_in, wp_uq, wp_ukv, wb_out = _weights_for_kernels(full_in, full_uq, full_ukv, full_out)

    half = jnp.arange(HALF, dtype=F32)
    inv_freq = 1.0 / (ROPE_THETA ** (half / HALF))
    invf_row = jnp.concatenate([jnp.zeros((KR_LO,), F32), inv_freq, inv_freq,
                                jnp.zeros((LANES - KR_LO - ROPE,), F32)]).reshape(1, LANES)
    tri = jnp.tril(jnp.ones((CHUNK, CHUNK), dtype=bool))
    ws_low = jnp.where(tri[None], w_spatial, 0.0).astype(BF16)
    ws_low_t = ws_low.transpose(0, 2, 1)
    bsp = jnp.repeat(b_spatial.T, G_HEAD_DIM, axis=1)
    row = lambda a: a.reshape(1, -1)

    proj, q, k, v, vt = _fwd_proj(x2, pos_col, invf_row, wp_in, wp_uq, wp_ukv, row(q_norm_g), row(kv_norm_g))
    o, lse_row = _attn_fwd(q, k, vt)
    (dr, do, d_row, drest, d_out, d_ws, d_bs_t, d_lng, d_lnb, d_sgug, d_sgub, loss_part) = _mid(
        x2, tgt2, proj, o, wb_out, ws_low, ws_low_t, bsp, row(sgu_norm_g), row(sgu_norm_b), row(ln_g), row(ln_b))
    dqt, dk, dv = _attn_bwd(q, k, v, do, lse_row, d_row)
    dhead, dwp_uq, dwp_ukv, d_qg, d_kvg = _bwd_qkv(dqt, dk, dv, proj, pos_col, invf_row, wp_uq, wp_ukv,
                                                   row(q_norm_g), row(kv_norm_g))
    grad_x, dwp_in = _bwd_in(x2, dr, dhead, drest, wp_in)
    d_in, d_uq, d_ukv = _grads_from_kernels(dwp_in, dwp_uq, dwp_ukv)
    return loss_part, grad_x, d_in, d_uq, d_ukv, d_out, d_ws, d_bs_t, d_lng, d_lnb, d_sgug, d_sgub, d_qg, d_kvg
```

```python
import functools
import math

import jax
import jax.numpy as jnp
from jax import lax
from jax.experimental import pallas as pl
from jax.experimental.pallas import tpu as pltpu

F32 = jnp.float32
BF16 = jnp.bfloat16

N_DEV = 8
D_MODEL = 1024
HEADS = 8
NOPE = 64
ROPE = 32
HALF = ROPE // 2
VDIM = 64
Q_LORA = 256
KV_LORA = 128
G_WIDTH = 512
G_HEAD_DIM = 64
CHUNK = 128
HEAD_PAD = 128
D_IN = 2464
D_IN_PAD = 2560
KR_LO = NOPE
ROPE_THETA = 10000.0
DN_ALPHA = 2.0 ** 0.25
EPS = 1e-5
ATTN_SCALE = 1.0 / math.sqrt(NOPE + ROPE)
ADAM_LR, ADAM_B1, ADAM_B2, ADAM_EPS, ADAM_WD, ADAM_STEP = 0.001, 0.9, 0.999, 1e-08, 0.01, 10

LANES = 128
REP_ROWS = 136
SMALL_LEN = 8192
VMEM_LIMIT = 56 * 1024 * 1024

TOKEN_TILE = 256
ATTN_WIDE = 1024
ATTN_NARROW = 512
SOFTMAX_ROWS = 256
LOG2E = 1.4426950408889634
LN2 = 0.6931471805599453
Q_PRESCALE = ATTN_SCALE * LOG2E


def _cparams(sem=None):
    return pltpu.CompilerParams(dimension_semantics=sem, vmem_limit_bytes=VMEM_LIMIT)


def _dot(a, b):
    return jnp.dot(a, b, preferred_element_type=F32)


def _dot_nt(a, b):
    return lax.dot_general(a, b, (((1,), (1,)), ((), ())), preferred_element_type=F32)


def _dot_tn(a, b):
    return lax.dot_general(a, b, (((0,), (0,)), ((), ())), preferred_element_type=F32)


def _as_row(col):
    return jnp.transpose(jnp.broadcast_to(col, (col.shape[0], LANES)))[0:1, :]


def _sigmoid(z):
    return 1.0 / (1.0 + jnp.exp(-z))


def _gelu(x):
    return 0.5 * x * (1.0 + lax.erf(x * 0.7071067811865476))


def _gelu_grad(x):
    cdf = 0.5 * (1.0 + lax.erf(x * 0.7071067811865476))
    return cdf + x * jnp.exp(-0.5 * x * x) * 0.3989422804014327


def _exchange(srcs, *, name, per_destination):
    n = len(srcs)
    slab_shapes = [s.shape[1:] if per_destination else s.shape for s in srcs]

    def body(*refs):
        src_refs, out_refs = refs[:n], refs[n:2 * n]
        send_sems, recv_sems, local_sems = refs[2 * n:]
        x, y, c = lax.axis_index("x"), lax.axis_index("y"), lax.axis_index("c")
        me = 4 * x + 2 * y + c

        def slab_for(t, dest):
            return src_refs[t].at[dest] if per_destination else src_refs[t]

        mine = [pltpu.make_async_copy(slab_for(t, me), out_refs[t].at[me], local_sems.at[t]) for t in range(n)]
        for cp in mine:
            cp.start()
        sends, arrivals = [], []
        for k in (6, 7, 4, 5, 2, 3, 1):
            px = 1 - x if k & 4 else x
            py = 1 - y if k & 2 else y
            pc = 1 - c if k & 1 else c
            peer = 4 * px + 2 * py + pc
            for t in range(n):
                sem = (k - 1) * n + t
                cp = pltpu.make_async_remote_copy(
                    src_ref=slab_for(t, peer), dst_ref=out_refs[t].at[me],
                    send_sem=send_sems.at[sem], recv_sem=recv_sems.at[sem],
                    device_id=(px, py, pc), device_id_type=pl.DeviceIdType.MESH)
                cp.start()
                sends.append(cp)
                arrivals.append(pltpu.make_async_remote_copy(
                    src_ref=slab_for(t, peer), dst_ref=out_refs[t].at[peer],
                    send_sem=send_sems.at[sem], recv_sem=recv_sems.at[sem],
                    device_id=(x, y, c), device_id_type=pl.DeviceIdType.MESH))
        for cp in arrivals:
            cp.wait_recv()
        for cp in sends:
            cp.wait_send()
        for cp in mine:
            cp.wait()

    hbm = pl.BlockSpec(memory_space=pl.ANY)
    return pl.pallas_call(
        body, name=name,
        out_shape=[jax.ShapeDtypeStruct((N_DEV,) + tuple(shape), s.dtype) for shape, s in zip(slab_shapes, srcs)],
        in_specs=[hbm] * n, out_specs=[hbm] * n,
        scratch_shapes=[pltpu.SemaphoreType.DMA(((N_DEV - 1) * n,)), pltpu.SemaphoreType.DMA(((N_DEV - 1) * n,)),
                        pltpu.SemaphoreType.DMA((n,))],
    )(*srcs)


def _rope_tables(pos_col, invf_row):
    ang = pos_col.astype(F32) * invf_row
    lane = lax.broadcasted_iota(jnp.int32, ang.shape, 1)
    cos, sin = jnp.cos(ang), jnp.sin(ang)
    first = (lane >= KR_LO) & (lane < KR_LO + HALF)
    second = (lane >= KR_LO + HALF) & (lane < KR_LO + ROPE)
    return cos, jnp.where(first, sin, 0.0), jnp.where(second, sin, 0.0)


def _rope(t, cos, sin_first, sin_second, sign):
    up = pltpu.roll(t, LANES - HALF, 1)
    down = pltpu.roll(t, HALF, 1)
    return t * cos - sign * (up * sin_first) + sign * (down * sin_second)


def _fwd_proj(x, pos_col, invf_row, wp_in, w_heads, q_g, kv_g):
    t = x.shape[0]
    tm = TOKEN_TILE

    def body(x_ref, pos_ref, invf_ref, win_ref, wh_ref, qg_ref, kvg_ref,
             proj_ref, q_ref, k_ref, v_ref, vt_ref):
        proj = _dot(x_ref[...].astype(BF16), win_ref[...])
        proj_ref[...] = proj
        c_q = proj[:, :Q_LORA]
        c_kv = proj[:, Q_LORA:Q_LORA + KV_LORA]
        kr_raw = proj[:, Q_LORA + KV_LORA:Q_LORA + KV_LORA + LANES]
        cqn = (c_q * lax.rsqrt(jnp.mean(c_q * c_q, axis=-1, keepdims=True) + EPS) * qg_ref[...]).astype(BF16)
        ckvn = (c_kv * lax.rsqrt(jnp.mean(c_kv * c_kv, axis=-1, keepdims=True) + EPS) * kvg_ref[...]).astype(BF16)
        cos, s1, s2 = _rope_tables(pos_ref[...], invf_ref[...])
        kr = _rope(kr_raw, cos, s1, s2, 1.0)
        lane = lax.broadcasted_iota(jnp.int32, (tm, HEAD_PAD), 1)
        for h in range(HEADS):
            q_h = _dot(cqn, wh_ref[h, :Q_LORA, :])
            kv_h = _dot(ckvn, wh_ref[h, Q_LORA:, :])
            q_ref[h] = (_rope(q_h, cos, s1, s2, 1.0) * Q_PRESCALE).astype(BF16)
            k_ref[h] = jnp.where(lane < NOPE, kv_h, kr).astype(BF16)
            v_ref[h] = kv_h.astype(BF16)
            vt_ref[h] = jnp.transpose(kv_h).astype(BF16)

    full = lambda a: pl.BlockSpec(a.shape, lambda i: (0,) * a.ndim)
    head_spec = pl.BlockSpec((HEADS, tm, HEAD_PAD), lambda i: (0, i, 0))
    head_shape = jax.ShapeDtypeStruct((HEADS, t, HEAD_PAD), BF16)
    return pl.pallas_call(
        body, name="fwd_proj", grid=(t // tm,),
        in_specs=[pl.BlockSpec((tm, D_MODEL), lambda i: (i, 0)), pl.BlockSpec((tm, 1), lambda i: (i, 0)),
                  full(invf_row), full(wp_in), full(w_heads), full(q_g), full(kv_g)],
        out_specs=[pl.BlockSpec((tm, D_IN_PAD), lambda i: (i, 0)), head_spec, head_spec, head_spec,
                   pl.BlockSpec((HEADS, HEAD_PAD, tm), lambda i: (0, 0, i))],
        out_shape=[jax.ShapeDtypeStruct((t, D_IN_PAD), F32), head_shape, head_shape, head_shape,
                   jax.ShapeDtypeStruct((HEADS, HEAD_PAD, t), BF16)],
        compiler_params=_cparams(("arbitrary",)),
    )(x, pos_col, invf_row, wp_in, w_heads, q_g, kv_g)


def _attn_fwd(q, k, vt):
    t = q.shape[1]
    bq, bk = ATTN_WIDE, ATTN_NARROW
    chunk = SOFTMAX_ROWS

    def body(q_ref, k_ref, vt_ref, o_ref, lse_ref, s0, s1, p0, p1, x0, x1, m_scr, l_scr, a_scr, acc_scr):
        i = pl.program_id(1)
        at = lambda j: pl.ds(pl.multiple_of(j * bk, bk), bk)

        def exp_pass(s_in, block_max, p_out, key0=None):
            def load(r):
                s = s_in[r:r + chunk, :]
                if key0 is not None:
                    key = lax.broadcasted_iota(jnp.int32, (chunk, bq), 0) + (r + key0)
                    qry = lax.broadcasted_iota(jnp.int32, (chunk, bq), 1)
                    s = jnp.where(qry >= key, s, -jnp.inf)
                return s

            if key0 is not None:
                block_max = jnp.max(load(0), axis=0, keepdims=True)
                for r in range(chunk, bk, chunk):
                    block_max = jnp.maximum(block_max, jnp.max(load(r), axis=0, keepdims=True))
            m_new = jnp.maximum(m_scr[...], block_max)
            alpha = jnp.exp2(m_scr[...] - m_new)
            total = jnp.zeros((1, bq), F32)
            for r in range(0, bk, chunk):
                p = jnp.exp2(load(r) - m_new)
                p_out[r:r + chunk, :] = p.astype(BF16)
                total = total + jnp.sum(p, axis=0, keepdims=True)
            m_scr[...] = m_new
            l_scr[...] = alpha * l_scr[...] + total
            return alpha

        def scores(j, s_out, x_out):
            s = _dot_nt(k_ref[0, at(j), :], q_ref[0])
            s_out[...] = s
            x_out[...] = jnp.max(s, axis=0, keepdims=True)

        def value_product(j, p_in):
            return _dot(vt_ref[0, :, at(j)], p_in[...])

        def one_pass(j, s_in, x_in, s_out, x_out, p_prev, p_cur):
            scores(j + 1, s_out, x_out)
            acc_scr[...] = a_scr[...] * acc_scr[...] + value_product(jnp.maximum(j - 1, 0), p_prev)
            a_scr[...] = exp_pass(s_in, x_in[...], p_cur)

        scores(0, s0, x0)
        p1[...] = jnp.zeros_like(p1)
        a_scr[...] = jnp.ones_like(a_scr)
        m_scr[...] = jnp.full(m_scr.shape, -jnp.inf, F32)
        l_scr[...] = jnp.zeros_like(l_scr)
        acc_scr[...] = jnp.zeros_like(acc_scr)

        def two_passes(n, _):
            one_pass(2 * n, s0, x0, s1, x1, p1, p0)
            one_pass(2 * n + 1, s1, x1, s0, x0, p0, p1)
            return 0

        lax.fori_loop(0, i, two_passes, 0)
        d = 2 * i
        scores(d + 1, s1, x1)
        acc = a_scr[...] * acc_scr[...] + value_product(jnp.maximum(d - 1, 0), p1)
        alpha = exp_pass(s0, None, p0, key0=0)
        acc = alpha * acc + value_product(d, p0)
        alpha = exp_pass(s1, None, p1, key0=bk)
        acc = alpha * acc + value_product(d + 1, p1)
        o_ref[0] = jnp.transpose(acc / l_scr[...])
        lse_ref[0] = m_scr[...] + jnp.log2(l_scr[...])

    tile = lambda dtype: pltpu.VMEM((bk, bq), dtype)
    stat = pltpu.VMEM((1, bq), F32)
    return pl.pallas_call(
        body, name="attn_fwd", grid=(HEADS, t // bq),
        in_specs=[pl.BlockSpec((1, bq, HEAD_PAD), lambda h, i: (h, i, 0)),
                  pl.BlockSpec((1, t, HEAD_PAD), lambda h, i: (h, 0, 0)),
                  pl.BlockSpec((1, HEAD_PAD, t), lambda h, i: (h, 0, 0))],
        out_specs=[pl.BlockSpec((1, bq, HEAD_PAD), lambda h, i: (h, i, 0)),
                   pl.BlockSpec((1, 1, bq), lambda h, i: (h, 0, i))],
        out_shape=[jax.ShapeDtypeStruct((HEADS, t, HEAD_PAD), F32), jax.ShapeDtypeStruct((HEADS, 1, t), F32)],
        scratch_shapes=[tile(F32), tile(F32), tile(BF16), tile(BF16), stat, stat, stat, stat, stat,
                        pltpu.VMEM((HEAD_PAD, bq), F32)],
        compiler_params=_cparams(("arbitrary", "arbitrary")),
    )(q, k, vt)


def _mid(x, target, proj, ol, w_out, ws_low, ws_low_t, bsp, sgu_g, sgu_b, ln_g, ln_b):
    t = x.shape[0]
    tm = TOKEN_TILE
    n_steps = t // tm

    def body(x_ref, tgt_ref, za_ref, u_ref, v_ref, zb_ref, ol_ref, wout_ref, ws_ref, wst_ref, bsp_ref,
             sg_ref, sb_ref, lg_ref, lb_ref,
             dr_ref, do_ref, drow_ref, drest_ref, dwout_ref, dws_ref, dbs_ref, dlg_ref, dlb_ref, dsg_ref, dsb_ref,
             loss_ref, dbsp_acc):
        step = pl.program_id(0)

        @pl.when(step == 0)
        def _():
            dwout_ref[...] = jnp.zeros_like(dwout_ref)
            dws_ref[...] = jnp.zeros_like(dws_ref)
            dbs_ref[...] = jnp.zeros_like(dbs_ref)
            dlg_ref[...] = jnp.zeros_like(dlg_ref)
            dlb_ref[...] = jnp.zeros_like(dlb_ref)
            dsg_ref[...] = jnp.zeros_like(dsg_ref)
            dsb_ref[...] = jnp.zeros_like(dsb_ref)
            loss_ref[...] = jnp.zeros_like(loss_ref)
            dbsp_acc[...] = jnp.zeros_like(dbsp_acc)

        lane_head = lax.broadcasted_iota(jnp.int32, (CHUNK, G_WIDTH), 1) // G_HEAD_DIM

        attn = jnp.concatenate([ol_ref[h][:, NOPE:] for h in range(HEADS)], axis=-1)
        za = za_ref[...]
        sig_a = _sigmoid(za)
        silu_a = za * sig_a
        out_a = attn * silu_a
        u = u_ref[...]
        ug = _gelu(u)
        vpre = v_ref[...]
        gv = _gelu(vpre)
        mu_v = jnp.mean(gv, axis=-1, keepdims=True)
        cen_v = gv - mu_v
        rstd_v = lax.rsqrt(jnp.mean(cen_v * cen_v, axis=-1, keepdims=True) + EPS)
        vhat = cen_v * rstd_v
        vg = vhat * sg_ref[...] + sb_ref[...]
        vg_b = vg.astype(BF16)
        sv_parts = []
        for cix in range(tm // CHUNK):
            vc = vg_b[cix * CHUNK:(cix + 1) * CHUNK, :]
            acc = bsp_ref[...]
            for h in range(HEADS):
                acc = acc + jnp.where(lane_head == h, _dot(ws_ref[h], vc), 0.0)
            sv_parts.append(acc)
        sv = jnp.concatenate(sv_parts, axis=0)
        sgu = ug * sv
        zb = zb_ref[...]
        sig_b = _sigmoid(zb)
        silu_b = zb * sig_b
        out_b = sgu * silu_b
        merged = jnp.concatenate([out_a, out_b], axis=-1).astype(BF16)
        r = DN_ALPHA * x_ref[...] + _dot(merged, wout_ref[...])
        mu = jnp.mean(r, axis=-1, keepdims=True)
        cen = r - mu
        rstd = lax.rsqrt(jnp.mean(cen * cen, axis=-1, keepdims=True) + EPS)
        xhat = cen * rstd
        hout = xhat * lg_ref[...] + lb_ref[...]
        err = hout - tgt_ref[...]
        row_loss = jnp.mean(err * err, axis=-1, keepdims=True)
        loss_ref[...] += jnp.broadcast_to(0.5 * jnp.sum(row_loss, axis=0, keepdims=True), loss_ref.shape)

        dh = err * (1.0 / D_MODEL)
        dlg_ref[...] += jnp.sum(dh * xhat, axis=0, keepdims=True)
        dlb_ref[...] += jnp.sum(dh, axis=0, keepdims=True)
        dxhat = dh * lg_ref[...]
        dr = rstd * (dxhat - jnp.mean(dxhat, axis=-1, keepdims=True)
                     - xhat * jnp.mean(dxhat * xhat, axis=-1, keepdims=True))
        dr_ref[...] = dr
        dr_b = dr.astype(BF16)
        dwout_ref[...] += _dot_tn(merged, dr_b)
        dmerged = _dot_nt(dr_b, wout_ref[...])
        d_out_a = dmerged[:, :G_WIDTH]
        d_out_b = dmerged[:, G_WIDTH:]
        dattn = d_out_a * silu_a
        for h in range(HEADS):
            do_h = dattn[:, h * VDIM:(h + 1) * VDIM]
            dsum = jnp.sum(do_h * ol_ref[h][:, NOPE:], axis=-1, keepdims=True)
            drow_ref[h] = _as_row(dsum)
            do_ref[h] = jnp.concatenate([jnp.zeros((tm, NOPE), F32), do_h], axis=-1).astype(BF16)
        dza = d_out_a * attn * (sig_a * (1.0 + za * (1.0 - sig_a)))
        dsgu = d_out_b * silu_b
        dzb = d_out_b * sgu * (sig_b * (1.0 + zb * (1.0 - sig_b)))
        du = dsgu * sv * _gelu_grad(u)
        dsv = dsgu * ug
        dsv_b = dsv.astype(BF16)
        dvg_parts = []
        for cix in range(tm // CHUNK):
            rows = slice(cix * CHUNK, (cix + 1) * CHUNK)
            dsv_c = dsv[rows, :]
            dsv_cb = dsv_b[rows, :]
            vc = vg_b[rows, :]
            dbsp_acc[...] += dsv_c
            acc = jnp.zeros((CHUNK, G_WIDTH), F32)
            for h in range(HEADS):
                on = lane_head == h
                acc = acc + jnp.where(on, _dot(wst_ref[h], dsv_cb), 0.0)
                dws_ref[h] += _dot_nt(jnp.where(on, dsv_cb, jnp.zeros_like(dsv_cb)), vc)
            dvg_parts.append(acc)
        dvg = jnp.concatenate(dvg_parts, axis=0)
        dsg_ref[...] += jnp.sum(dvg * vhat, axis=0, keepdims=True)
        dsb_ref[...] += jnp.sum(dvg, axis=0, keepdims=True)
        dvhat = dvg * sg_ref[...]
        dgv = rstd_v * (dvhat - jnp.mean(dvhat, axis=-1, keepdims=True)
                        - vhat * jnp.mean(dvhat * vhat, axis=-1, keepdims=True))
        dv = dgv * _gelu_grad(vpre)
        drest_ref[...] = jnp.concatenate([dza, du, dv, dzb], axis=-1).astype(BF16)

        @pl.when(step == n_steps - 1)
        def _():
            tri = (lax.broadcasted_iota(jnp.int32, (CHUNK, CHUNK), 0)
                   >= lax.broadcasted_iota(jnp.int32, (CHUNK, CHUNK), 1))
            for h in range(HEADS):
                dws_ref[h] = jnp.where(tri, dws_ref[h], 0.0)
            tot = dbsp_acc[...]
            lane = lax.broadcasted_iota(jnp.int32, (CHUNK, LANES), 1)
            dbs = jnp.zeros((CHUNK, LANES), F32)
            for h in range(HEADS):
                head_sum = jnp.sum(tot[:, h * G_HEAD_DIM:(h + 1) * G_HEAD_DIM], axis=-1, keepdims=True)
                dbs = jnp.where(lane == h, head_sum, dbs)
            dbs_ref[...] = dbs

    full = lambda a: pl.BlockSpec(a.shape, lambda i: (0,) * a.ndim)
    tile = lambda w, j=0: pl.BlockSpec((tm, w), lambda i, j=j: (i, j))
    heads = pl.BlockSpec((HEADS, tm, HEAD_PAD), lambda i: (0, i, 0))
    acc = lambda shape: (pl.BlockSpec(shape, lambda i: (0,) * len(shape)), jax.ShapeDtypeStruct(shape, F32))
    accs = [acc((D_MODEL, D_MODEL)), acc((HEADS, CHUNK, CHUNK)), acc((CHUNK, LANES)), acc((1, D_MODEL)),
            acc((1, D_MODEL)), acc((1, G_WIDTH)), acc((1, G_WIDTH)), acc((1, LANES))]
    return pl.pallas_call(
        body, name="mid", grid=(n_steps,),
        in_specs=[tile(D_MODEL), tile(D_MODEL), tile(G_WIDTH, 1), tile(G_WIDTH, 2), tile(G_WIDTH, 3), tile(G_WIDTH, 4),
                  heads, full(w_out), full(ws_low), full(ws_low_t), full(bsp), full(sgu_g), full(sgu_b),
                  full(ln_g), full(ln_b)],
        out_specs=[tile(D_MODEL), heads, pl.BlockSpec((HEADS, 1, tm), lambda i: (0, 0, i)), tile(4 * G_WIDTH)]
        + [a[0] for a in accs],
        out_shape=[jax.ShapeDtypeStruct((t, D_MODEL), F32), jax.ShapeDtypeStruct((HEADS, t, HEAD_PAD), BF16),
                   jax.ShapeDtypeStruct((HEADS, 1, t), F32), jax.ShapeDtypeStruct((t, 4 * G_WIDTH), BF16)]
        + [a[1] for a in accs],
        scratch_shapes=[pltpu.VMEM((CHUNK, G_WIDTH), F32)],
        compiler_params=_cparams(("arbitrary",)),
    )(x, target, proj, proj, proj, proj, ol, w_out, ws_low, ws_low_t, bsp, sgu_g, sgu_b, ln_g, ln_b)


def _attn_bwd(q, k, v, do, lse_row, d_row):
    t = q.shape[1]
    bk, bq = ATTN_WIDE, ATTN_NARROW
    last = t // bq - 1
    chunk = SOFTMAX_ROWS

    def body(q_ref, k_ref, v_ref, do_ref, lse_ref, drow_ref, dqt_ref, dk_ref, dv_ref,
             s0, s1, e0, e1, p0, p1, g0, g1, kt_scr):
        j = pl.program_id(1)
        at = lambda i: pl.ds(pl.multiple_of(i * bq, bq), bq)

        @pl.when(j == 0)
        def _():
            dqt_ref[...] = jnp.zeros_like(dqt_ref)

        kt_scr[...] = jnp.transpose(k_ref[0].astype(F32)).astype(BF16)
        dk_ref[...] = jnp.zeros_like(dk_ref)
        dv_ref[...] = jnp.zeros_like(dv_ref)

        def products(i, s_out, e_out):
            i = jnp.minimum(i, last)
            s_out[...] = _dot_nt(k_ref[0], q_ref[0, at(i), :])
            e_out[...] = _dot_nt(v_ref[0], do_ref[0, at(i), :])

        def gradients(i, p_in, g_in):
            dv_ref[0] += _dot(p_in[...], do_ref[0, at(i), :])
            dk_ref[0] += _dot(g_in[...], q_ref[0, at(i), :])
            dqt_ref[0, :, at(i)] += _dot(kt_scr[...], g_in[...])

        def elementwise(i, s_in, e_in, p_out, g_out, qry0=None):
            lse = lse_ref[0, :, at(i)]
            dsum = drow_ref[0, :, at(i)]
            for r in range(0, bk, chunk):
                p = jnp.exp2(s_in[r:r + chunk, :] - lse)
                if qry0 is not None:
                    key = lax.broadcasted_iota(jnp.int32, (chunk, bq), 0) + r
                    qry = lax.broadcasted_iota(jnp.int32, (chunk, bq), 1) + qry0
                    p = jnp.where(qry >= key, p, 0.0)
                p_out[r:r + chunk, :] = p.astype(BF16)
                g_out[r:r + chunk, :] = (p * (e_in[r:r + chunk, :] - dsum)).astype(BF16)

        def one_pass(i, s_in, e_in, s_out, e_out, p_prev, g_prev, p_cur, g_cur, qry0=None):
            products(i + 1, s_out, e_out)
            gradients(i - 1, p_prev, g_prev)
            elementwise(i, s_in, e_in, p_cur, g_cur, qry0)

        first = 2 * j
        products(first, s0, e0)
        products(first + 1, s1, e1)
        elementwise(first, s0, e0, p0, g0, qry0=0)
        one_pass(first + 1, s1, e1, s0, e0, p0, g0, p1, g1, qry0=bq)

        def two_passes(n, _):
            i = first + 2 + 2 * n
            one_pass(i, s0, e0, s1, e1, p1, g1, p0, g0)
            one_pass(i + 1, s1, e1, s0, e0, p0, g0, p1, g1)
            return 0

        lax.fori_loop(0, (last - first - 1) // 2, two_passes, 0)
        gradients(last, p1, g1)
        dk_ref[0] = dk_ref[0] * LN2

    whole = pl.BlockSpec((1, t, HEAD_PAD), lambda h, j: (h, 0, 0))
    block = pl.BlockSpec((1, bk, HEAD_PAD), lambda h, j: (h, j, 0))
    rows = pl.BlockSpec((1, 1, t), lambda h, j: (h, 0, 0))
    shape = jax.ShapeDtypeStruct((HEADS, t, HEAD_PAD), F32)
    tile = lambda dtype: pltpu.VMEM((bk, bq), dtype)
    return pl.pallas_call(
        body, name="attn_bwd", grid=(HEADS, t // bk),
        in_specs=[whole, block, block, whole, rows, rows],
        out_specs=[pl.BlockSpec((1, HEAD_PAD, t), lambda h, j: (h, 0, 0)), block, block],
        out_shape=[jax.ShapeDtypeStruct((HEADS, HEAD_PAD, t), F32), shape, shape],
        scratch_shapes=[tile(F32), tile(F32), tile(F32), tile(F32), tile(BF16), tile(BF16),
                        tile(BF16), tile(BF16), pltpu.VMEM((HEAD_PAD, bk), BF16)],
        compiler_params=_cparams(("arbitrary", "arbitrary")),
    )(q, k, v, do, lse_row, d_row)


def _bwd_qkv(dq, dk, dv, proj, pos_col, invf_row, w_heads, q_g, kv_g):
    t = proj.shape[0]
    tm = TOKEN_TILE

    def body(dq_ref, dk_ref, dv_ref, ph_ref, pos_ref, invf_ref, wh_ref, qg_ref, kvg_ref,
             dhead_ref, dwh_ref, dqg_ref, dkvg_ref):
        @pl.when(pl.program_id(0) == 0)
        def _():
            dwh_ref[...] = jnp.zeros_like(dwh_ref)
            dqg_ref[...] = jnp.zeros_like(dqg_ref)
            dkvg_ref[...] = jnp.zeros_like(dkvg_ref)

        cos, s1, s2 = _rope_tables(pos_ref[...], invf_ref[...])
        lane = lax.broadcasted_iota(jnp.int32, (tm, LANES), 1)
        c_q = ph_ref[:, :Q_LORA]
        c_kv = ph_ref[:, Q_LORA:Q_LORA + KV_LORA]
        rstd_q = lax.rsqrt(jnp.mean(c_q * c_q, axis=-1, keepdims=True) + EPS)
        rstd_kv = lax.rsqrt(jnp.mean(c_kv * c_kv, axis=-1, keepdims=True) + EPS)
        qhat = c_q * rstd_q
        kvhat = c_kv * rstd_kv
        cqn = (qhat * qg_ref[...]).astype(BF16)
        ckvn = (kvhat * kvg_ref[...]).astype(BF16)
        dcqn = jnp.zeros((tm, Q_LORA), F32)
        dckvn = jnp.zeros((tm, KV_LORA), F32)
        dkr_rot = jnp.zeros((tm, LANES), F32)
        for h in range(HEADS):
            dq_b = _rope(jnp.transpose(dq_ref[h]) * ATTN_SCALE, cos, s1, s2, -1.0).astype(BF16)
            dk_h = dk_ref[h]
            dkv_b = jnp.where(lane < NOPE, dk_h, dv_ref[h]).astype(BF16)
            dkr_rot = dkr_rot + dk_h
            dwh_ref[h, :Q_LORA, :] += _dot_tn(cqn, dq_b)
            dwh_ref[h, Q_LORA:, :] += _dot_tn(ckvn, dkv_b)
            dcqn = dcqn + _dot_nt(dq_b, wh_ref[h, :Q_LORA, :])
            dckvn = dckvn + _dot_nt(dkv_b, wh_ref[h, Q_LORA:, :])
        rot_lanes = (lane >= KR_LO) & (lane < KR_LO + ROPE)
        dkr_raw = jnp.where(rot_lanes, _rope(dkr_rot, cos, s1, s2, -1.0), 0.0)
        dqg_ref[...] += jnp.sum(dcqn * qhat, axis=0, keepdims=True)
        dkvg_ref[...] += jnp.sum(dckvn * kvhat, axis=0, keepdims=True)
        dqh = dcqn * qg_ref[...]
        dkvh = dckvn * kvg_ref[...]
        dc_q = rstd_q * (dqh - qhat * jnp.mean(dqh * qhat, axis=-1, keepdims=True))
        dc_kv = rstd_kv * (dkvh - kvhat * jnp.mean(dkvh * kvhat, axis=-1, keepdims=True))
        dhead_ref[...] = jnp.concatenate([dc_q, dc_kv, dkr_raw], axis=-1).astype(BF16)

    full = lambda a: pl.BlockSpec(a.shape, lambda i: (0,) * a.ndim)
    heads = pl.BlockSpec((HEADS, tm, HEAD_PAD), lambda i: (0, i, 0))
    acc = lambda shape: (pl.BlockSpec(shape, lambda i: (0,) * len(shape)), jax.ShapeDtypeStruct(shape, F32))
    accs = [acc(w_heads.shape), acc((1, Q_LORA)), acc((1, KV_LORA))]
    return pl.pallas_call(
        body, name="bwd_qkv", grid=(t // tm,),
        in_specs=[pl.BlockSpec((HEADS, HEAD_PAD, tm), lambda i: (0, 0, i)), heads, heads,
                  pl.BlockSpec((tm, 4 * LANES), lambda i: (i, 0)),
                  pl.BlockSpec((tm, 1), lambda i: (i, 0)), full(invf_row), full(w_heads),
                  full(q_g), full(kv_g)],
        out_specs=[pl.BlockSpec((tm, 4 * LANES), lambda i: (i, 0))] + [a[0] for a in accs],
        out_shape=[jax.ShapeDtypeStruct((t, 4 * LANES), BF16)] + [a[1] for a in accs],
        compiler_params=_cparams(("arbitrary",)),
    )(dq, dk, dv, proj, pos_col, invf_row, w_heads, q_g, kv_g)


def _bwd_in(x, dr, dhead, drest, wp_in):
    t = x.shape[0]
    tm = TOKEN_TILE
    n_head = dhead.shape[1]

    def body(x_ref, dr_ref, dhead_ref, drest_ref, win_ref, gx_ref, dwin_ref):
        @pl.when(pl.program_id(0) == 0)
        def _():
            dwin_ref[...] = jnp.zeros_like(dwin_ref)

        xb = x_ref[...].astype(BF16)
        dh_b = dhead_ref[...]
        dr_b = drest_ref[...]
        gx_ref[...] = (DN_ALPHA * dr_ref[...] + _dot_nt(dh_b, win_ref[:, :n_head])
                       + _dot_nt(dr_b, win_ref[:, n_head:]))
        dwin_ref[:, :n_head] += _dot_tn(xb, dh_b)
        dwin_ref[:, n_head:] += _dot_tn(xb, dr_b)

    tile = lambda w: pl.BlockSpec((tm, w), lambda i: (i, 0))
    whole = pl.BlockSpec(wp_in.shape, lambda i: (0, 0))
    return pl.pallas_call(
        body, name="bwd_in", grid=(t // tm,),
        in_specs=[tile(D_MODEL), tile(D_MODEL), tile(n_head), tile(drest.shape[1]), whole],
        out_specs=[tile(D_MODEL), whole],
        out_shape=[jax.ShapeDtypeStruct((t, D_MODEL), F32), jax.ShapeDtypeStruct(wp_in.shape, F32)],
        compiler_params=_cparams(("arbitrary",)),
    )(x, dr, dhead, drest, wp_in)


def _adam(parts, w, m, v, *, name, tile_rows):
    n, rows, cols = parts.shape

    def body(p_ref, w_ref, m_ref, v_ref, g_ref, d_ref, nm_ref, nv_ref):
        g = p_ref[0]
        for s in range(1, n):
            g = g + p_ref[s]
        m_new = ADAM_B1 * m_ref[...] + (1.0 - ADAM_B1) * g
        v_new = ADAM_B2 * v_ref[...] + (1.0 - ADAM_B2) * (g * g)
        m_hat = m_new / (1.0 - ADAM_B1 ** ADAM_STEP)
        v_hat = v_new / (1.0 - ADAM_B2 ** ADAM_STEP)
        g_ref[...] = g
        d_ref[...] = -ADAM_LR * (m_hat / (jnp.sqrt(v_hat) + ADAM_EPS) + ADAM_WD * w_ref[...])
        nm_ref[...] = m_new
        nv_ref[...] = v_new

    flat = pl.BlockSpec((tile_rows, cols), lambda i: (i, 0))
    shape = jax.ShapeDtypeStruct((rows, cols), F32)
    return pl.pallas_call(
        body, name=name, grid=(rows // tile_rows,),
        in_specs=[pl.BlockSpec((n, tile_rows, cols), lambda i: (0, i, 0)), flat, flat, flat],
        out_specs=[flat] * 4, out_shape=[shape] * 4,
        compiler_params=_cparams(("arbitrary",)),
    )(parts, w, m, v)


SMALL_NAMES = ("q_norm_g", "kv_norm_g", "sgu_norm_g", "sgu_norm_b", "b_spatial", "ln_g", "ln_b")
SMALL_SIZES = (Q_LORA, KV_LORA, G_WIDTH, G_WIDTH, HEADS * CHUNK, D_MODEL, D_MODEL)


def _pack_small(vals, last=None):
    flat = jnp.concatenate([v.reshape(-1) for v in vals])
    pad = SMALL_LEN - flat.shape[0]
    if last is None:
        return jnp.pad(flat, (0, pad))
    return jnp.concatenate([flat, jnp.zeros((pad - 1,), F32), last.reshape(1)])


def _unpack_small(flat):
    out, at = [], 0
    for n in SMALL_SIZES:
        out.append(flat[at:at + n])
        at += n
    out[4] = out[4].reshape(HEADS, CHUNK)
    return out


UQ_SHARD = HEADS * (NOPE + ROPE) // N_DEV
HEAD_ROWS = Q_LORA + KV_LORA
MIXED_ROWS = HEAD_ROWS + CHUNK + SMALL_LEN // N_DEV // LANES


def _head_slab(w_uq_shard, w_ukv_shard):
    return jnp.concatenate([jnp.pad(w_uq_shard, ((0, 0), (0, LANES - UQ_SHARD))), w_ukv_shard])


def _padded_w_in(shards):
    full = shards.transpose(1, 0, 2).reshape(D_MODEL, D_IN)
    z = lambda c: jnp.zeros((D_MODEL, c), shards.dtype)
    split = Q_LORA + KV_LORA
    return jnp.concatenate([full[:, :split], z(KR_LO), full[:, split:split + ROPE], z(LANES - KR_LO - ROPE),
                            full[:, split + ROPE:]], axis=1)


def _w_in_shards(dwp_in):
    split = Q_LORA + KV_LORA
    full = jnp.concatenate([dwp_in[:, :split], dwp_in[:, split + KR_LO:split + KR_LO + ROPE],
                            dwp_in[:, split + LANES:]], axis=1)
    return full.reshape(D_MODEL, N_DEV, D_IN // N_DEV).transpose(1, 0, 2)


def kernel(x, positions, w_in, q_norm_g, w_uq, kv_norm_g, w_ukv, sgu_norm_g, sgu_norm_b, w_spatial, b_spatial, w_out, ln_g, ln_b, loss_target, m_w_in, m_q_norm_g, m_w_uq, m_kv_norm_g, m_w_ukv, m_sgu_norm_g, m_sgu_norm_b, m_w_spatial, m_b_spatial, m_w_out, m_ln_g, m_ln_b, v_w_in, v_q_norm_g, v_w_uq, v_kv_norm_g, v_w_ukv, v_sgu_norm_g, v_sgu_norm_b, v_w_spatial, v_b_spatial, v_w_out, v_ln_g, v_ln_b):
    me = 4 * lax.axis_index("x") + 2 * lax.axis_index("y") + lax.axis_index("c")
    seq = x.shape[1]
    x2 = x.reshape(seq, D_MODEL)
    tgt2 = loss_target.reshape(seq, D_MODEL)
    pos_col = positions.reshape(seq, 1)

    w_in_shards, w_out_shards, w_heads = _exchange(
        [w_in.astype(BF16), w_out.astype(BF16), _head_slab(w_uq, w_ukv).astype(BF16)],
        name="wgather", per_destination=False)
    (loss_part, grad_x, d_in, d_heads, d_out, d_ws, d_bs_t, d_lng, d_lnb, d_sgug, d_sgub, d_qg, d_kvg) = _local_step(
        x2, tgt2, pos_col, w_in_shards, w_heads, w_out_shards.reshape(D_MODEL, D_MODEL), q_norm_g, kv_norm_g,
        sgu_norm_g, sgu_norm_b, w_spatial, b_spatial, ln_g, ln_b)

    small_part = _pack_small([d_qg, d_kvg, d_sgug, d_sgub, d_bs_t[:, :HEADS].T, d_lng, d_lnb], last=loss_part[0, :1])
    mixed = jnp.concatenate([d_heads, d_ws, small_part.reshape(N_DEV, -1, LANES)], axis=1)
    recv_in, recv_out, recv_mixed = _exchange(
        [d_in, d_out.reshape(N_DEV, D_MODEL // N_DEV, D_MODEL), mixed], name="gexch", per_destination=True)

    take = lambda a: lax.dynamic_index_in_dim(a, me, 0, keepdims=False)
    small_w = _pack_small([q_norm_g, kv_norm_g, sgu_norm_g, sgu_norm_b, b_spatial, ln_g, ln_b])
    small_m = _pack_small([m_q_norm_g, m_kv_norm_g, m_sgu_norm_g, m_sgu_norm_b, m_b_spatial, m_ln_g, m_ln_b])
    small_v = _pack_small([v_q_norm_g, v_kv_norm_g, v_sgu_norm_g, v_sgu_norm_b, v_b_spatial, v_ln_g, v_ln_b])
    own_mixed = lambda uq, ukv, sp, small: jnp.concatenate(
        [_head_slab(uq, ukv), take(sp), take(small.reshape(N_DEV, -1, LANES))])
    res_in = _adam(recv_in, w_in, m_w_in, v_w_in, name="adam_in", tile_rows=TOKEN_TILE)
    res_out = _adam(recv_out, w_out, m_w_out, v_w_out, name="adam_out", tile_rows=D_MODEL // N_DEV)
    res_mixed = _adam(recv_mixed, own_mixed(w_uq, w_ukv, w_spatial, small_w), own_mixed(m_w_uq, m_w_ukv, m_w_spatial, small_m),
                      own_mixed(v_w_uq, v_w_ukv, v_w_spatial, small_v), name="adam_mixed", tile_rows=MIXED_ROWS)

    rep_g, = _exchange([res_mixed[0][HEAD_ROWS:]], name="sgather", per_destination=False)
    rep_pack = lambda sp, small: jnp.concatenate(
        [sp.reshape(N_DEV, CHUNK, LANES), small.reshape(N_DEV, -1, LANES)], axis=1).reshape(-1, LANES)
    _, delta_rep, m_rep, v_rep = _adam(rep_g.reshape(1, N_DEV * REP_ROWS, LANES), rep_pack(w_spatial, small_w),
                                       rep_pack(m_w_spatial, small_m), rep_pack(v_w_spatial, small_v),
                                       name="adam_rep", tile_rows=N_DEV * REP_ROWS)

    def rep_unpack(a):
        a = a.reshape(N_DEV, REP_ROWS, LANES)
        small = _unpack_small(a[:, CHUNK:].reshape(-1))
        return [small[0], small[1], small[2], small[3], a[:, :CHUNK], small[4], small[5], small[6]]

    def ordered(which, rep):
        r_qg, r_kvg, r_sg, r_sb, r_ws, r_bs, r_lg, r_lb = rep_unpack(rep)
        heads = res_mixed[which]
        return [res_in[which], r_qg, heads[:Q_LORA, :UQ_SHARD], r_kvg, heads[Q_LORA:HEAD_ROWS], r_sg, r_sb, r_ws, r_bs,
                res_out[which], r_lg, r_lb]

    loss = rep_g[N_DEV - 1, REP_ROWS - 1, LANES - 1]
    outs = [loss, grad_x.reshape(x.shape)]
    outs += ordered(0, rep_g.reshape(-1, LANES))
    outs += ordered(1, delta_rep)
    outs += ordered(2, m_rep)
    outs += ordered(3, v_rep)
    return tuple(outs)


def _local_step(x2, tgt2, pos_col, w_in_shards, w_heads, w_out_full, q_norm_g, kv_norm_g, sgu_norm_g, sgu_norm_b,
                w_spatial, b_spatial, ln_g, ln_b):
    wp_in = _padded_w_in(w_in_shards)

    half = jnp.arange(HALF, dtype=F32)
    inv_freq = 1.0 / (ROPE_THETA ** (half / HALF))
    invf_row = jnp.concatenate([jnp.zeros((KR_LO,), F32), inv_freq, inv_freq,
                                jnp.zeros((LANES - KR_LO - ROPE,), F32)]).reshape(1, LANES)
    tri = jnp.tril(jnp.ones((CHUNK, CHUNK), dtype=bool))
    ws_low = jnp.where(tri[None], w_spatial, 0.0).astype(BF16)
    ws_low_t = ws_low.transpose(0, 2, 1)
    bsp = jnp.repeat(b_spatial.T, G_HEAD_DIM, axis=1)
    row = lambda a: a.reshape(1, -1)

    proj, q, k, v, vt = _fwd_proj(x2, pos_col, invf_row, wp_in, w_heads, row(q_norm_g), row(kv_norm_g))
    o, lse_row = _attn_fwd(q, k, vt)
    (dr, do, d_row, drest, d_out, d_ws, d_bs_t, d_lng, d_lnb, d_sgug, d_sgub, loss_part) = _mid(
        x2, tgt2, proj, o, w_out_full, ws_low, ws_low_t, bsp, row(sgu_norm_g), row(sgu_norm_b), row(ln_g), row(ln_b))
    dqt, dk, dv = _attn_bwd(q, k, v, do, lse_row, d_row)
    dhead, d_heads, d_qg, d_kvg = _bwd_qkv(dqt, dk, dv, proj, pos_col, invf_row, w_heads, row(q_norm_g), row(kv_norm_g))
    grad_x, dwp_in = _bwd_in(x2, dr, dhead, drest, wp_in)
    return (loss_part, grad_x, _w_in_shards(dwp_in), d_heads, d_out, d_ws, d_bs_t, d_lng, d_lnb, d_sgug, d_sgub,
            d_qg, d_kvg)
```

```python
import functools
import math

import jax
import jax.numpy as jnp
from jax import lax
from jax.experimental import pallas as pl
from jax.experimental.pallas import tpu as pltpu

F32 = jnp.float32
BF16 = jnp.bfloat16

N_DEV = 8
D_MODEL = 1024
HEADS = 8
NOPE = 64
ROPE = 32
HALF = ROPE // 2
VDIM = 64
Q_LORA = 256
KV_LORA = 128
G_WIDTH = 512
G_HEAD_DIM = 64
CHUNK = 128
HEAD_PAD = 128
D_IN = 2464
D_IN_PAD = 2560
KR_LO = NOPE
ROPE_THETA = 10000.0
DN_ALPHA = 2.0 ** 0.25
EPS = 1e-5
ATTN_SCALE = 1.0 / math.sqrt(NOPE + ROPE)
ADAM_LR, ADAM_B1, ADAM_B2, ADAM_EPS, ADAM_WD, ADAM_STEP = 0.001, 0.9, 0.999, 1e-08, 0.01, 10

LANES = 128
REP_ROWS = 136
SMALL_LEN = 8192
VMEM_LIMIT = 56 * 1024 * 1024

TOKEN_TILE = 256
ATTN_WIDE = 1024
ATTN_NARROW = 512
SOFTMAX_ROWS = 256
LOG2E = 1.4426950408889634
LN2 = 0.6931471805599453
Q_PRESCALE = ATTN_SCALE * LOG2E


def _cparams(sem=None):
    return pltpu.CompilerParams(dimension_semantics=sem, vmem_limit_bytes=VMEM_LIMIT)


def _dot(a, b):
    return jnp.dot(a, b, preferred_element_type=F32)


def _dot_nt(a, b):
    return lax.dot_general(a, b, (((1,), (1,)), ((), ())), preferred_element_type=F32)


def _dot_tn(a, b):
    return lax.dot_general(a, b, (((0,), (0,)), ((), ())), preferred_element_type=F32)


def _as_row(col):
    return jnp.transpose(jnp.broadcast_to(col, (col.shape[0], LANES)))[0:1, :]


def _sigmoid(z):
    return 1.0 / (1.0 + jnp.exp(-z))


def _gelu(x):
    return 0.5 * x * (1.0 + lax.erf(x * 0.7071067811865476))


def _gelu_grad(x):
    cdf = 0.5 * (1.0 + lax.erf(x * 0.7071067811865476))
    return cdf + x * jnp.exp(-0.5 * x * x) * 0.3989422804014327


def _exchange(srcs, *, name, per_destination):
    n = len(srcs)
    slab_shapes = [s.shape[1:] if per_destination else s.shape for s in srcs]

    def body(*refs):
        src_refs, out_refs = refs[:n], refs[n:2 * n]
        send_sems, recv_sems, local_sems = refs[2 * n:]
        x, y, c = lax.axis_index("x"), lax.axis_index("y"), lax.axis_index("c")
        me = 4 * x + 2 * y + c

        def slab_for(t, dest):
            return src_refs[t].at[dest] if per_destination else src_refs[t]

        mine = [pltpu.make_async_copy(slab_for(t, me), out_refs[t].at[me], local_sems.at[t]) for t in range(n)]
        for cp in mine:
            cp.start()
        sends, arrivals = [], []
        for k in (6, 7, 4, 5, 2, 3, 1):
            px = 1 - x if k & 4 else x
            py = 1 - y if k & 2 else y
            pc = 1 - c if k & 1 else c
            peer = 4 * px + 2 * py + pc
            for t in range(n):
                sem = (k - 1) * n + t
                cp = pltpu.make_async_remote_copy(
                    src_ref=slab_for(t, peer), dst_ref=out_refs[t].at[me],
                    send_sem=send_sems.at[sem], recv_sem=recv_sems.at[sem],
                    device_id=(px, py, pc), device_id_type=pl.DeviceIdType.MESH)
                cp.start()
                sends.append(cp)
                arrivals.append(pltpu.make_async_remote_copy(
                    src_ref=slab_for(t, peer), dst_ref=out_refs[t].at[peer],
                    send_sem=send_sems.at[sem], recv_sem=recv_sems.at[sem],
                    device_id=(x, y, c), device_id_type=pl.DeviceIdType.MESH))
        for cp in arrivals:
            cp.wait_recv()
        for cp in sends:
            cp.wait_send()
        for cp in mine:
            cp.wait()

    hbm = pl.BlockSpec(memory_space=pl.ANY)
    return pl.pallas_call(
        body, name=name,
        out_shape=[jax.ShapeDtypeStruct((N_DEV,) + tuple(shape), s.dtype) for shape, s in zip(slab_shapes, srcs)],
        in_specs=[hbm] * n, out_specs=[hbm] * n,
        scratch_shapes=[pltpu.SemaphoreType.DMA(((N_DEV - 1) * n,)), pltpu.SemaphoreType.DMA(((N_DEV - 1) * n,)),
                        pltpu.SemaphoreType.DMA((n,))],
    )(*srcs)


def _gather_two_level(srcs, *, name):
    n = len(srcs)

    def body(*refs):
        src_refs, out_refs = refs[:n], refs[n:2 * n]
        send_sems, recv_sems, local_sems = refs[2 * n:]
        x, y, c = lax.axis_index("x"), lax.axis_index("y"), lax.axis_index("c")
        me, sibling = (x, y, c), (x, y, 1 - c)
        chips = [(1 - x, 1 - y), (1 - x, y), (x, 1 - y)]
        index = lambda px, py, pc: 4 * px + 2 * py + pc

        def copy(k, t, block, to, src=None):
            place = out_refs[t].at[index(*block)]
            return pltpu.make_async_remote_copy(
                src_ref=place if src is None else src, dst_ref=place,
                send_sem=send_sems.at[k * n + t], recv_sem=recv_sems.at[k * n + t],
                device_id=to, device_id_type=pl.DeviceIdType.MESH)

        mine = [pltpu.make_async_copy(src_refs[t], out_refs[t].at[index(*me)], local_sems.at[t]) for t in range(n)]
        for cp in mine:
            cp.start()
        first = [copy(1 + j, t, me, (*chip, c), src=src_refs[t]) for j, chip in enumerate(chips) for t in range(n)]
        first += [copy(0, t, me, sibling, src=src_refs[t]) for t in range(n)]
        for cp in first:
            cp.start()
        passed = []
        for j, chip in enumerate(chips):
            for t in range(n):
                copy(1 + j, t, (*chip, c), me).wait_recv()
                cp = copy(4 + j, t, (*chip, c), sibling)
                cp.start()
                passed.append(cp)
        for t in range(n):
            copy(0, t, sibling, me).wait_recv()
        for j, chip in enumerate(chips):
            for t in range(n):
                copy(4 + j, t, (*chip, 1 - c), me).wait_recv()
        for cp in first + passed:
            cp.wait_send()
        for cp in mine:
            cp.wait()

    hbm = pl.BlockSpec(memory_space=pl.ANY)
    return pl.pallas_call(
        body, name=name,
        out_shape=[jax.ShapeDtypeStruct((N_DEV,) + s.shape, s.dtype) for s in srcs],
        in_specs=[hbm] * n, out_specs=[hbm] * n,
        scratch_shapes=[pltpu.SemaphoreType.DMA((7 * n,)), pltpu.SemaphoreType.DMA((7 * n,)),
                        pltpu.SemaphoreType.DMA((n,))],
    )(*srcs)


N_CHIPS = N_DEV // 2


def _sibling_swap(srcs, *, name):
    n = len(srcs)

    def body(*refs):
        src_refs, out_refs = refs[:n], refs[n:2 * n]
        send_sems, recv_sems = refs[2 * n:]
        x, y, c = lax.axis_index("x"), lax.axis_index("y"), lax.axis_index("c")
        sends = []
        for chip in range(N_CHIPS):
            for t in range(n):
                cp = pltpu.make_async_remote_copy(
                    src_ref=src_refs[t].at[chip, 1 - c], dst_ref=out_refs[t].at[chip],
                    send_sem=send_sems.at[chip * n + t], recv_sem=recv_sems.at[chip * n + t],
                    device_id=(x, y, 1 - c), device_id_type=pl.DeviceIdType.MESH)
                cp.start()
                sends.append(cp)
        for cp in sends:
            cp.wait_recv()
        for cp in sends:
            cp.wait_send()

    hbm = pl.BlockSpec(memory_space=pl.ANY)
    return pl.pallas_call(
        body, name=name,
        out_shape=[jax.ShapeDtypeStruct((N_CHIPS,) + s.shape[2:], s.dtype) for s in srcs],
        in_specs=[hbm] * n, out_specs=[hbm] * n,
        scratch_shapes=[pltpu.SemaphoreType.DMA((N_CHIPS * n,)), pltpu.SemaphoreType.DMA((N_CHIPS * n,))],
    )(*srcs)


def _pair_sum(mine, theirs, core, *, name, tile_rows):
    _, _, rows, cols = mine.shape

    def body(core_ref, a_ref, b_ref, o_ref):
        o_ref[...] = a_ref[0] + b_ref[...]

    return pl.pallas_call(
        body, name=name,
        grid_spec=pltpu.PrefetchScalarGridSpec(
            num_scalar_prefetch=1, grid=(N_CHIPS, rows // tile_rows),
            in_specs=[pl.BlockSpec((1, 1, tile_rows, cols), lambda q, r, core_ref: (q, core_ref[0], r, 0)),
                      pl.BlockSpec((1, tile_rows, cols), lambda q, r, core_ref: (q, r, 0))],
            out_specs=pl.BlockSpec((1, tile_rows, cols), lambda q, r, core_ref: (q, r, 0))),
        out_shape=jax.ShapeDtypeStruct((N_CHIPS, rows, cols), mine.dtype),
        compiler_params=_cparams(("arbitrary", "arbitrary")),
    )(core, mine, theirs)


def _chip_exchange(srcs, *, name):
    n = len(srcs)

    def body(*refs):
        src_refs, out_refs = refs[:n], refs[n:2 * n]
        send_sems, recv_sems, local_sems = refs[2 * n:]
        x, y, c = lax.axis_index("x"), lax.axis_index("y"), lax.axis_index("c")
        my_chip = 2 * x + y
        mine = [pltpu.make_async_copy(src_refs[t].at[my_chip], out_refs[t].at[my_chip], local_sems.at[t])
                for t in range(n)]
        for cp in mine:
            cp.start()
        sends, arrivals = [], []
        for k in (3, 2, 1):
            px = 1 - x if k & 2 else x
            py = 1 - y if k & 1 else y
            peer_chip = 2 * px + py
            for t in range(n):
                sem = (k - 1) * n + t
                cp = pltpu.make_async_remote_copy(
                    src_ref=src_refs[t].at[peer_chip], dst_ref=out_refs[t].at[my_chip],
                    send_sem=send_sems.at[sem], recv_sem=recv_sems.at[sem],
                    device_id=(px, py, c), device_id_type=pl.DeviceIdType.MESH)
                cp.start()
                sends.append(cp)
                arrivals.append(pltpu.make_async_remote_copy(
                    src_ref=src_refs[t].at[peer_chip], dst_ref=out_refs[t].at[peer_chip],
                    send_sem=send_sems.at[sem], recv_sem=recv_sems.at[sem],
                    device_id=(x, y, c), device_id_type=pl.DeviceIdType.MESH))
        for cp in arrivals:
            cp.wait_recv()
        for cp in sends:
            cp.wait_send()
        for cp in mine:
            cp.wait()

    hbm = pl.BlockSpec(memory_space=pl.ANY)
    return pl.pallas_call(
        body, name=name,
        out_shape=[jax.ShapeDtypeStruct(s.shape, s.dtype) for s in srcs],
        in_specs=[hbm] * n, out_specs=[hbm] * n,
        scratch_shapes=[pltpu.SemaphoreType.DMA((3 * n,)), pltpu.SemaphoreType.DMA((3 * n,)),
                        pltpu.SemaphoreType.DMA((n,))],
    )(*srcs)


def _rope_tables(pos_col, invf_row):
    ang = pos_col.astype(F32) * invf_row
    lane = lax.broadcasted_iota(jnp.int32, ang.shape, 1)
    cos, sin = jnp.cos(ang), jnp.sin(ang)
    first = (lane >= KR_LO) & (lane < KR_LO + HALF)
    second = (lane >= KR_LO + HALF) & (lane < KR_LO + ROPE)
    return cos, jnp.where(first, sin, 0.0), jnp.where(second, sin, 0.0)


def _rope(t, cos, sin_first, sin_second, sign):
    up = pltpu.roll(t, LANES - HALF, 1)
    down = pltpu.roll(t, HALF, 1)
    return t * cos - sign * (up * sin_first) + sign * (down * sin_second)


def _fwd_proj(x, pos_col, invf_row, wp_in, w_heads, q_g, kv_g):
    t = x.shape[0]
    tm = TOKEN_TILE

    def body(x_ref, pos_ref, invf_ref, win_ref, wh_ref, qg_ref, kvg_ref,
             proj_ref, q_ref, k_ref, v_ref, vt_ref):
        proj = _dot(x_ref[...].astype(BF16), win_ref[...])
        proj_ref[...] = proj
        c_q = proj[:, :Q_LORA]
        c_kv = proj[:, Q_LORA:Q_LORA + KV_LORA]
        kr_raw = proj[:, Q_LORA + KV_LORA:Q_LORA + KV_LORA + LANES]
        cqn = (c_q * lax.rsqrt(jnp.mean(c_q * c_q, axis=-1, keepdims=True) + EPS) * qg_ref[...]).astype(BF16)
        ckvn = (c_kv * lax.rsqrt(jnp.mean(c_kv * c_kv, axis=-1, keepdims=True) + EPS) * kvg_ref[...]).astype(BF16)
        cos, s1, s2 = _rope_tables(pos_ref[...], invf_ref[...])
        kr = _rope(kr_raw, cos, s1, s2, 1.0)
        lane = lax.broadcasted_iota(jnp.int32, (tm, HEAD_PAD), 1)
        for h in range(HEADS):
            q_h = _dot(cqn, wh_ref[h, :Q_LORA, :])
            kv_h = _dot(ckvn, wh_ref[h, Q_LORA:, :])
            q_ref[h] = (_rope(q_h, cos, s1, s2, 1.0) * Q_PRESCALE).astype(BF16)
            k_ref[h] = jnp.where(lane < NOPE, kv_h, kr).astype(BF16)
            v_ref[h] = kv_h.astype(BF16)
            vt_ref[h] = jnp.transpose(kv_h).astype(BF16)

    full = lambda a: pl.BlockSpec(a.shape, lambda i: (0,) * a.ndim)
    head_spec = pl.BlockSpec((HEADS, tm, HEAD_PAD), lambda i: (0, i, 0))
    head_shape = jax.ShapeDtypeStruct((HEADS, t, HEAD_PAD), BF16)
    return pl.pallas_call(
        body, name="fwd_proj", grid=(t // tm,),
        in_specs=[pl.BlockSpec((tm, D_MODEL), lambda i: (i, 0)), pl.BlockSpec((tm, 1), lambda i: (i, 0)),
                  full(invf_row), full(wp_in), full(w_heads), full(q_g), full(kv_g)],
        out_specs=[pl.BlockSpec((tm, D_IN_PAD), lambda i: (i, 0)), head_spec, head_spec, head_spec,
                   pl.BlockSpec((HEADS, HEAD_PAD, tm), lambda i: (0, 0, i))],
        out_shape=[jax.ShapeDtypeStruct((t, D_IN_PAD), F32), head_shape, head_shape, head_shape,
                   jax.ShapeDtypeStruct((HEADS, HEAD_PAD, t), BF16)],
        compiler_params=_cparams(("arbitrary",)),
    )(x, pos_col, invf_row, wp_in, w_heads, q_g, kv_g)


def _attn_fwd(q, k, vt):
    t = q.shape[1]
    bq, bk = ATTN_WIDE, ATTN_NARROW
    chunk = SOFTMAX_ROWS

    def body(q_ref, k_ref, vt_ref, o_ref, lse_ref, s0, s1, p0, p1, x0, x1, m_scr, l_scr, a_scr, acc_scr):
        i = pl.program_id(1)
        at = lambda j: pl.ds(pl.multiple_of(j * bk, bk), bk)

        def exp_pass(s_in, block_max, p_out, key0=None):
            def load(r):
                s = s_in[r:r + chunk, :]
                if key0 is not None:
                    key = lax.broadcasted_iota(jnp.int32, (chunk, bq), 0) + (r + key0)
                    qry = lax.broadcasted_iota(jnp.int32, (chunk, bq), 1)
                    s = jnp.where(qry >= key, s, -jnp.inf)
                return s

            if key0 is not None:
                block_max = jnp.max(load(0), axis=0, keepdims=True)
                for r in range(chunk, bk, chunk):
                    block_max = jnp.maximum(block_max, jnp.max(load(r), axis=0, keepdims=True))
            m_new = jnp.maximum(m_scr[...], block_max)
            alpha = jnp.exp2(m_scr[...] - m_new)
            total = jnp.zeros((1, bq), F32)
            for r in range(0, bk, chunk):
                p = jnp.exp2(load(r) - m_new)
                p_out[r:r + chunk, :] = p.astype(BF16)
                total = total + jnp.sum(p, axis=0, keepdims=True)
            m_scr[...] = m_new
            l_scr[...] = alpha * l_scr[...] + total
            return alpha

        def scores(j, s_out, x_out):
            s = _dot_nt(k_ref[0, at(j), :], q_ref[0])
            s_out[...] = s
            x_out[...] = jnp.max(s, axis=0, keepdims=True)

        def value_product(j, p_in):
            return _dot(vt_ref[0, :, at(j)], p_in[...])

        def one_pass(j, s_in, x_in, s_out, x_out, p_prev, p_cur):
            scores(j + 1, s_out, x_out)
            acc_scr[...] = a_scr[...] * acc_scr[...] + value_product(jnp.maximum(j - 1, 0), p_prev)
            a_scr[...] = exp_pass(s_in, x_in[...], p_cur)

        scores(0, s0, x0)
        p1[...] = jnp.zeros_like(p1)
        a_scr[...] = jnp.ones_like(a_scr)
        m_scr[...] = jnp.full(m_scr.shape, -jnp.inf, F32)
        l_scr[...] = jnp.zeros_like(l_scr)
        acc_scr[...] = jnp.zeros_like(acc_scr)

        def two_passes(n, _):
            one_pass(2 * n, s0, x0, s1, x1, p1, p0)
            one_pass(2 * n + 1, s1, x1, s0, x0, p0, p1)
            return 0

        lax.fori_loop(0, i, two_passes, 0)
        d = 2 * i
        scores(d + 1, s1, x1)
        acc = a_scr[...] * acc_scr[...] + value_product(jnp.maximum(d - 1, 0), p1)
        alpha = exp_pass(s0, None, p0, key0=0)
        acc = alpha * acc + value_product(d, p0)
        alpha = exp_pass(s1, None, p1, key0=bk)
        acc = alpha * acc + value_product(d + 1, p1)
        o_ref[0] = jnp.transpose(acc / l_scr[...])
        lse_ref[0] = m_scr[...] + jnp.log2(l_scr[...])

    tile = lambda dtype: pltpu.VMEM((bk, bq), dtype)
    stat = pltpu.VMEM((1, bq), F32)
    return pl.pallas_call(
        body, name="attn_fwd", grid=(HEADS, t // bq),
        in_specs=[pl.BlockSpec((1, bq, HEAD_PAD), lambda h, i: (h, i, 0)),
                  pl.BlockSpec((1, t, HEAD_PAD), lambda h, i: (h, 0, 0)),
                  pl.BlockSpec((1, HEAD_PAD, t), lambda h, i: (h, 0, 0))],
        out_specs=[pl.BlockSpec((1, bq, HEAD_PAD), lambda h, i: (h, i, 0)),
                   pl.BlockSpec((1, 1, bq), lambda h, i: (h, 0, i))],
        out_shape=[jax.ShapeDtypeStruct((HEADS, t, HEAD_PAD), F32), jax.ShapeDtypeStruct((HEADS, 1, t), F32)],
        scratch_shapes=[tile(F32), tile(F32), tile(BF16), tile(BF16), stat, stat, stat, stat, stat,
                        pltpu.VMEM((HEAD_PAD, bq), F32)],
        compiler_params=_cparams(("arbitrary", "arbitrary")),
    )(q, k, vt)


def _mid(x, target, proj, ol, w_out, ws_low, ws_low_t, bsp, sgu_g, sgu_b, ln_g, ln_b):
    t = x.shape[0]
    tm = TOKEN_TILE
    n_steps = t // tm

    def body(x_ref, tgt_ref, za_ref, u_ref, v_ref, zb_ref, ol_ref, wout_ref, ws_ref, wst_ref, bsp_ref,
             sg_ref, sb_ref, lg_ref, lb_ref,
             dr_ref, do_ref, drow_ref, drest_ref, dwout_ref, dws_ref, dbs_ref, dlg_ref, dlb_ref, dsg_ref, dsb_ref,
             loss_ref, dbsp_acc):
        step = pl.program_id(0)

        @pl.when(step == 0)
        def _():
            dwout_ref[...] = jnp.zeros_like(dwout_ref)
            dws_ref[...] = jnp.zeros_like(dws_ref)
            dbs_ref[...] = jnp.zeros_like(dbs_ref)
            dlg_ref[...] = jnp.zeros_like(dlg_ref)
            dlb_ref[...] = jnp.zeros_like(dlb_ref)
            dsg_ref[...] = jnp.zeros_like(dsg_ref)
            dsb_ref[...] = jnp.zeros_like(dsb_ref)
            loss_ref[...] = jnp.zeros_like(loss_ref)
            dbsp_acc[...] = jnp.zeros_like(dbsp_acc)

        lane_head = lax.broadcasted_iota(jnp.int32, (CHUNK, G_WIDTH), 1) // G_HEAD_DIM

        attn = jnp.concatenate([ol_ref[h][:, NOPE:] for h in range(HEADS)], axis=-1)
        za = za_ref[...]
        sig_a = _sigmoid(za)
        silu_a = za * sig_a
        out_a = attn * silu_a
        u = u_ref[...]
        ug = _gelu(u)
        vpre = v_ref[...]
        gv = _gelu(vpre)
        mu_v = jnp.mean(gv, axis=-1, keepdims=True)
        cen_v = gv - mu_v
        rstd_v = lax.rsqrt(jnp.mean(cen_v * cen_v, axis=-1, keepdims=True) + EPS)
        vhat = cen_v * rstd_v
        vg = vhat * sg_ref[...] + sb_ref[...]
        vg_b = vg.astype(BF16)
        sv_parts = []
        for cix in range(tm // CHUNK):
            vc = vg_b[cix * CHUNK:(cix + 1) * CHUNK, :]
            acc = bsp_ref[...]
            for h in range(HEADS):
                acc = acc + jnp.where(lane_head == h, _dot(ws_ref[h], vc), 0.0)
            sv_parts.append(acc)
        sv = jnp.concatenate(sv_parts, axis=0)
        sgu = ug * sv
        zb = zb_ref[...]
        sig_b = _sigmoid(zb)
        silu_b = zb * sig_b
        out_b = sgu * silu_b
        merged = jnp.concatenate([out_a, out_b], axis=-1).astype(BF16)
        r = DN_ALPHA * x_ref[...] + _dot(merged, wout_ref[...])
        mu = jnp.mean(r, axis=-1, keepdims=True)
        cen = r - mu
        rstd = lax.rsqrt(jnp.mean(cen * cen, axis=-1, keepdims=True) + EPS)
        xhat = cen * rstd
        hout = xhat * lg_ref[...] + lb_ref[...]
        err = hout - tgt_ref[...]
        row_loss = jnp.mean(err * err, axis=-1, keepdims=True)
        loss_ref[...] += jnp.broadcast_to(0.5 * jnp.sum(row_loss, axis=0, keepdims=True), loss_ref.shape)

        dh = err * (1.0 / D_MODEL)
        dlg_ref[...] += jnp.sum(dh * xhat, axis=0, keepdims=True)
        dlb_ref[...] += jnp.sum(dh, axis=0, keepdims=True)
        dxhat = dh * lg_ref[...]
        dr = rstd * (dxhat - jnp.mean(dxhat, axis=-1, keepdims=True)
                     - xhat * jnp.mean(dxhat * xhat, axis=-1, keepdims=True))
        dr_ref[...] = dr
        dr_b = dr.astype(BF16)
        dwout_ref[...] += _dot_tn(merged, dr_b)
        dmerged = _dot_nt(dr_b, wout_ref[...])
        d_out_a = dmerged[:, :G_WIDTH]
        d_out_b = dmerged[:, G_WIDTH:]
        dattn = d_out_a * silu_a
        for h in range(HEADS):
            do_h = dattn[:, h * VDIM:(h + 1) * VDIM]
            dsum = jnp.sum(do_h * ol_ref[h][:, NOPE:], axis=-1, keepdims=True)
            drow_ref[h] = _as_row(dsum)
            do_ref[h] = jnp.concatenate([jnp.zeros((tm, NOPE), F32), do_h], axis=-1).astype(BF16)
        dza = d_out_a * attn * (sig_a * (1.0 + za * (1.0 - sig_a)))
        dsgu = d_out_b * silu_b
        dzb = d_out_b * sgu * (sig_b * (1.0 + zb * (1.0 - sig_b)))
        du = dsgu * sv * _gelu_grad(u)
        dsv = dsgu * ug
        dsv_b = dsv.astype(BF16)
        dvg_parts = []
        for cix in range(tm // CHUNK):
            rows = slice(cix * CHUNK, (cix + 1) * CHUNK)
            dsv_c = dsv[rows, :]
            dsv_cb = dsv_b[rows, :]
            vc = vg_b[rows, :]
            dbsp_acc[...] += dsv_c
            acc = jnp.zeros((CHUNK, G_WIDTH), F32)
            for h in range(HEADS):
                on = lane_head == h
                acc = acc + jnp.where(on, _dot(wst_ref[h], dsv_cb), 0.0)
                dws_ref[h] += _dot_nt(jnp.where(on, dsv_cb, jnp.zeros_like(dsv_cb)), vc)
            dvg_parts.append(acc)
        dvg = jnp.concatenate(dvg_parts, axis=0)
        dsg_ref[...] += jnp.sum(dvg * vhat, axis=0, keepdims=True)
        dsb_ref[...] += jnp.sum(dvg, axis=0, keepdims=True)
        dvhat = dvg * sg_ref[...]
        dgv = rstd_v * (dvhat - jnp.mean(dvhat, axis=-1, keepdims=True)
                        - vhat * jnp.mean(dvhat * vhat, axis=-1, keepdims=True))
        dv = dgv * _gelu_grad(vpre)
        drest_ref[...] = jnp.concatenate([dza, du, dv, dzb], axis=-1).astype(BF16)

        @pl.when(step == n_steps - 1)
        def _():
            tri = (lax.broadcasted_iota(jnp.int32, (CHUNK, CHUNK), 0)
                   >= lax.broadcasted_iota(jnp.int32, (CHUNK, CHUNK), 1))
            for h in range(HEADS):
                dws_ref[h] = jnp.where(tri, dws_ref[h], 0.0)
            tot = dbsp_acc[...]
            lane = lax.broadcasted_iota(jnp.int32, (CHUNK, LANES), 1)
            dbs = jnp.zeros((CHUNK, LANES), F32)
            for h in range(HEADS):
                head_sum = jnp.sum(tot[:, h * G_HEAD_DIM:(h + 1) * G_HEAD_DIM], axis=-1, keepdims=True)
                dbs = jnp.where(lane == h, head_sum, dbs)
            dbs_ref[...] = dbs

    full = lambda a: pl.BlockSpec(a.shape, lambda i: (0,) * a.ndim)
    tile = lambda w, j=0: pl.BlockSpec((tm, w), lambda i, j=j: (i, j))
    heads = pl.BlockSpec((HEADS, tm, HEAD_PAD), lambda i: (0, i, 0))
    acc = lambda shape: (pl.BlockSpec(shape, lambda i: (0,) * len(shape)), jax.ShapeDtypeStruct(shape, F32))
    accs = [acc((D_MODEL, D_MODEL)), acc((HEADS, CHUNK, CHUNK)), acc((CHUNK, LANES)), acc((1, D_MODEL)),
            acc((1, D_MODEL)), acc((1, G_WIDTH)), acc((1, G_WIDTH)), acc((1, LANES))]
    return pl.pallas_call(
        body, name="mid", grid=(n_steps,),
        in_specs=[tile(D_MODEL), tile(D_MODEL), tile(G_WIDTH, 1), tile(G_WIDTH, 2), tile(G_WIDTH, 3), tile(G_WIDTH, 4),
                  heads, full(w_out), full(ws_low), full(ws_low_t), full(bsp), full(sgu_g), full(sgu_b),
                  full(ln_g), full(ln_b)],
        out_specs=[tile(D_MODEL), heads, pl.BlockSpec((HEADS, 1, tm), lambda i: (0, 0, i)), tile(4 * G_WIDTH)]
        + [a[0] for a in accs],
        out_shape=[jax.ShapeDtypeStruct((t, D_MODEL), F32), jax.ShapeDtypeStruct((HEADS, t, HEAD_PAD), BF16),
                   jax.ShapeDtypeStruct((HEADS, 1, t), F32), jax.ShapeDtypeStruct((t, 4 * G_WIDTH), BF16)]
        + [a[1] for a in accs],
        scratch_shapes=[pltpu.VMEM((CHUNK, G_WIDTH), F32)],
        compiler_params=_cparams(("arbitrary",)),
    )(x, target, proj, proj, proj, proj, ol, w_out, ws_low, ws_low_t, bsp, sgu_g, sgu_b, ln_g, ln_b)


def _attn_bwd(q, k, v, do, lse_row, d_row):
    t = q.shape[1]
    bk, bq = ATTN_WIDE, ATTN_NARROW
    last = t // bq - 1
    chunk = SOFTMAX_ROWS

    def body(q_ref, k_ref, v_ref, do_ref, lse_ref, drow_ref, dqt_ref, dk_ref, dv_ref,
             s0, s1, e0, e1, p0, p1, g0, g1, kt_scr):
        j = pl.program_id(1)
        at = lambda i: pl.ds(pl.multiple_of(i * bq, bq), bq)

        @pl.when(j == 0)
        def _():
            dqt_ref[...] = jnp.zeros_like(dqt_ref)

        kt_scr[...] = jnp.transpose(k_ref[0].astype(F32)).astype(BF16)
        dk_ref[...] = jnp.zeros_like(dk_ref)
        dv_ref[...] = jnp.zeros_like(dv_ref)

        def products(i, s_out, e_out):
            i = jnp.minimum(i, last)
            s_out[...] = _dot_nt(k_ref[0], q_ref[0, at(i), :])
            e_out[...] = _dot_nt(v_ref[0], do_ref[0, at(i), :])

        def gradients(i, p_in, g_in):
            dv_ref[0] += _dot(p_in[...], do_ref[0, at(i), :])
            dk_ref[0] += _dot(g_in[...], q_ref[0, at(i), :])
            dqt_ref[0, :, at(i)] += _dot(kt_scr[...], g_in[...])

        def elementwise(i, s_in, e_in, p_out, g_out, qry0=None):
            lse = lse_ref[0, :, at(i)]
            dsum = drow_ref[0, :, at(i)]
            for r in range(0, bk, chunk):
                p = jnp.exp2(s_in[r:r + chunk, :] - lse)
                if qry0 is not None:
                    key = lax.broadcasted_iota(jnp.int32, (chunk, bq), 0) + r
                    qry = lax.broadcasted_iota(jnp.int32, (chunk, bq), 1) + qry0
                    p = jnp.where(qry >= key, p, 0.0)
                p_out[r:r + chunk, :] = p.astype(BF16)
                g_out[r:r + chunk, :] = (p * (e_in[r:r + chunk, :] - dsum)).astype(BF16)

        def one_pass(i, s_in, e_in, s_out, e_out, p_prev, g_prev, p_cur, g_cur, qry0=None):
            products(i + 1, s_out, e_out)
            gradients(i - 1, p_prev, g_prev)
            elementwise(i, s_in, e_in, p_cur, g_cur, qry0)

        first = 2 * j
        products(first, s0, e0)
        products(first + 1, s1, e1)
        elementwise(first, s0, e0, p0, g0, qry0=0)
        one_pass(first + 1, s1, e1, s0, e0, p0, g0, p1, g1, qry0=bq)

        def two_passes(n, _):
            i = first + 2 + 2 * n
            one_pass(i, s0, e0, s1, e1, p1, g1, p0, g0)
            one_pass(i + 1, s1, e1, s0, e0, p0, g0, p1, g1)
            return 0

        lax.fori_loop(0, (last - first - 1) // 2, two_passes, 0)
        gradients(last, p1, g1)
        dk_ref[0] = dk_ref[0] * LN2

    whole = pl.BlockSpec((1, t, HEAD_PAD), lambda h, j: (h, 0, 0))
    block = pl.BlockSpec((1, bk, HEAD_PAD), lambda h, j: (h, j, 0))
    rows = pl.BlockSpec((1, 1, t), lambda h, j: (h, 0, 0))
    shape = jax.ShapeDtypeStruct((HEADS, t, HEAD_PAD), F32)
    tile = lambda dtype: pltpu.VMEM((bk, bq), dtype)
    return pl.pallas_call(
        body, name="attn_bwd", grid=(HEADS, t // bk),
        in_specs=[whole, block, block, whole, rows, rows],
        out_specs=[pl.BlockSpec((1, HEAD_PAD, t), lambda h, j: (h, 0, 0)), block, block],
        out_shape=[jax.ShapeDtypeStruct((HEADS, HEAD_PAD, t), F32), shape, shape],
        scratch_shapes=[tile(F32), tile(F32), tile(F32), tile(F32), tile(BF16), tile(BF16),
                        tile(BF16), tile(BF16), pltpu.VMEM((HEAD_PAD, bk), BF16)],
        compiler_params=_cparams(("arbitrary", "arbitrary")),
    )(q, k, v, do, lse_row, d_row)


def _bwd_qkv(dq, dk, dv, proj, pos_col, invf_row, w_heads, q_g, kv_g):
    t = proj.shape[0]
    tm = TOKEN_TILE

    def body(dq_ref, dk_ref, dv_ref, ph_ref, pos_ref, invf_ref, wh_ref, qg_ref, kvg_ref,
             dhead_ref, dwh_ref, dqg_ref, dkvg_ref):
        @pl.when(pl.program_id(0) == 0)
        def _():
            dwh_ref[...] = jnp.zeros_like(dwh_ref)
            dqg_ref[...] = jnp.zeros_like(dqg_ref)
            dkvg_ref[...] = jnp.zeros_like(dkvg_ref)

        cos, s1, s2 = _rope_tables(pos_ref[...], invf_ref[...])
        lane = lax.broadcasted_iota(jnp.int32, (tm, LANES), 1)
        c_q = ph_ref[:, :Q_LORA]
        c_kv = ph_ref[:, Q_LORA:Q_LORA + KV_LORA]
        rstd_q = lax.rsqrt(jnp.mean(c_q * c_q, axis=-1, keepdims=True) + EPS)
        rstd_kv = lax.rsqrt(jnp.mean(c_kv * c_kv, axis=-1, keepdims=True) + EPS)
        qhat = c_q * rstd_q
        kvhat = c_kv * rstd_kv
        cqn = (qhat * qg_ref[...]).astype(BF16)
        ckvn = (kvhat * kvg_ref[...]).astype(BF16)
        dcqn = jnp.zeros((tm, Q_LORA), F32)
        dckvn = jnp.zeros((tm, KV_LORA), F32)
        dkr_rot = jnp.zeros((tm, LANES), F32)
        for h in range(HEADS):
            dq_b = _rope(jnp.transpose(dq_ref[h]) * ATTN_SCALE, cos, s1, s2, -1.0).astype(BF16)
            dk_h = dk_ref[h]
            dkv_b = jnp.where(lane < NOPE, dk_h, dv_ref[h]).astype(BF16)
            dkr_rot = dkr_rot + dk_h
            dwh_ref[h, :Q_LORA, :] += _dot_tn(cqn, dq_b)
            dwh_ref[h, Q_LORA:, :] += _dot_tn(ckvn, dkv_b)
            dcqn = dcqn + _dot_nt(dq_b, wh_ref[h, :Q_LORA, :])
            dckvn = dckvn + _dot_nt(dkv_b, wh_ref[h, Q_LORA:, :])
        rot_lanes = (lane >= KR_LO) & (lane < KR_LO + ROPE)
        dkr_raw = jnp.where(rot_lanes, _rope(dkr_rot, cos, s1, s2, -1.0), 0.0)
        dqg_ref[...] += jnp.sum(dcqn * qhat, axis=0, keepdims=True)
        dkvg_ref[...] += jnp.sum(dckvn * kvhat, axis=0, keepdims=True)
        dqh = dcqn * qg_ref[...]
        dkvh = dckvn * kvg_ref[...]
        dc_q = rstd_q * (dqh - qhat * jnp.mean(dqh * qhat, axis=-1, keepdims=True))
        dc_kv = rstd_kv * (dkvh - kvhat * jnp.mean(dkvh * kvhat, axis=-1, keepdims=True))
        dhead_ref[...] = jnp.concatenate([dc_q, dc_kv, dkr_raw], axis=-1).astype(BF16)

    full = lambda a: pl.BlockSpec(a.shape, lambda i: (0,) * a.ndim)
    heads = pl.BlockSpec((HEADS, tm, HEAD_PAD), lambda i: (0, i, 0))
    acc = lambda shape: (pl.BlockSpec(shape, lambda i: (0,) * len(shape)), jax.ShapeDtypeStruct(shape, F32))
    accs = [acc(w_heads.shape), acc((1, Q_LORA)), acc((1, KV_LORA))]
    return pl.pallas_call(
        body, name="bwd_qkv", grid=(t // tm,),
        in_specs=[pl.BlockSpec((HEADS, HEAD_PAD, tm), lambda i: (0, 0, i)), heads, heads,
                  pl.BlockSpec((tm, 4 * LANES), lambda i: (i, 0)),
                  pl.BlockSpec((tm, 1), lambda i: (i, 0)), full(invf_row), full(w_heads),
                  full(q_g), full(kv_g)],
        out_specs=[pl.BlockSpec((tm, 4 * LANES), lambda i: (i, 0))] + [a[0] for a in accs],
        out_shape=[jax.ShapeDtypeStruct((t, 4 * LANES), BF16)] + [a[1] for a in accs],
        compiler_params=_cparams(("arbitrary",)),
    )(dq, dk, dv, proj, pos_col, invf_row, w_heads, q_g, kv_g)


def _bwd_in(x, dr, dhead, drest, wp_in):
    t = x.shape[0]
    tm = TOKEN_TILE
    n_head = dhead.shape[1]

    def body(x_ref, dr_ref, dhead_ref, drest_ref, win_ref, gx_ref, dwin_ref):
        @pl.when(pl.program_id(0) == 0)
        def _():
            dwin_ref[...] = jnp.zeros_like(dwin_ref)

        xb = x_ref[...].astype(BF16)
        dh_b = dhead_ref[...]
        dr_b = drest_ref[...]
        gx_ref[...] = (DN_ALPHA * dr_ref[...] + _dot_nt(dh_b, win_ref[:, :n_head])
                       + _dot_nt(dr_b, win_ref[:, n_head:]))
        dwin_ref[:, :n_head] += _dot_tn(xb, dh_b)
        dwin_ref[:, n_head:] += _dot_tn(xb, dr_b)

    tile = lambda w: pl.BlockSpec((tm, w), lambda i: (i, 0))
    whole = pl.BlockSpec(wp_in.shape, lambda i: (0, 0))
    return pl.pallas_call(
        body, name="bwd_in", grid=(t // tm,),
        in_specs=[tile(D_MODEL), tile(D_MODEL), tile(n_head), tile(drest.shape[1]), whole],
        out_specs=[tile(D_MODEL), whole],
        out_shape=[jax.ShapeDtypeStruct((t, D_MODEL), F32), jax.ShapeDtypeStruct(wp_in.shape, F32)],
        compiler_params=_cparams(("arbitrary",)),
    )(x, dr, dhead, drest, wp_in)


def _adam(parts, w, m, v, *, name, tile_rows):
    n, rows, cols = parts.shape

    def body(p_ref, w_ref, m_ref, v_ref, g_ref, d_ref, nm_ref, nv_ref):
        g = p_ref[0]
        for s in range(1, n):
            g = g + p_ref[s]
        m_new = ADAM_B1 * m_ref[...] + (1.0 - ADAM_B1) * g
        v_new = ADAM_B2 * v_ref[...] + (1.0 - ADAM_B2) * (g * g)
        m_hat = m_new / (1.0 - ADAM_B1 ** ADAM_STEP)
        v_hat = v_new / (1.0 - ADAM_B2 ** ADAM_STEP)
        g_ref[...] = g
        d_ref[...] = -ADAM_LR * (m_hat / (jnp.sqrt(v_hat) + ADAM_EPS) + ADAM_WD * w_ref[...])
        nm_ref[...] = m_new
        nv_ref[...] = v_new

    flat = pl.BlockSpec((tile_rows, cols), lambda i: (i, 0))
    shape = jax.ShapeDtypeStruct((rows, cols), F32)
    return pl.pallas_call(
        body, name=name, grid=(rows // tile_rows,),
        in_specs=[pl.BlockSpec((n, tile_rows, cols), lambda i: (0, i, 0)), flat, flat, flat],
        out_specs=[flat] * 4, out_shape=[shape] * 4,
        compiler_params=_cparams(("arbitrary",)),
    )(parts, w, m, v)


SMALL_NAMES = ("q_norm_g", "kv_norm_g", "sgu_norm_g", "sgu_norm_b", "b_spatial", "ln_g", "ln_b")
SMALL_SIZES = (Q_LORA, KV_LORA, G_WIDTH, G_WIDTH, HEADS * CHUNK, D_MODEL, D_MODEL)


def _pack_small(vals, last=None):
    flat = jnp.concatenate([v.reshape(-1) for v in vals])
    pad = SMALL_LEN - flat.shape[0]
    if last is None:
        return jnp.pad(flat, (0, pad))
    return jnp.concatenate([flat, jnp.zeros((pad - 1,), F32), last.reshape(1)])


def _unpack_small(flat):
    out, at = [], 0
    for n in SMALL_SIZES:
        out.append(flat[at:at + n])
        at += n
    out[4] = out[4].reshape(HEADS, CHUNK)
    return out


UQ_SHARD = HEADS * (NOPE + ROPE) // N_DEV
HEAD_ROWS = Q_LORA + KV_LORA
MIXED_ROWS = HEAD_ROWS + CHUNK + SMALL_LEN // N_DEV // LANES


def _head_slab(w_uq_shard, w_ukv_shard):
    return jnp.concatenate([jnp.pad(w_uq_shard, ((0, 0), (0, LANES - UQ_SHARD))), w_ukv_shard])


def _padded_w_in(shards):
    full = shards.transpose(1, 0, 2).reshape(D_MODEL, D_IN)
    z = lambda c: jnp.zeros((D_MODEL, c), shards.dtype)
    split = Q_LORA + KV_LORA
    return jnp.concatenate([full[:, :split], z(KR_LO), full[:, split:split + ROPE], z(LANES - KR_LO - ROPE),
                            full[:, split + ROPE:]], axis=1)


def _w_in_shards(dwp_in):
    split = Q_LORA + KV_LORA
    full = jnp.concatenate([dwp_in[:, :split], dwp_in[:, split + KR_LO:split + KR_LO + ROPE],
                            dwp_in[:, split + LANES:]], axis=1)
    return full.reshape(D_MODEL, N_DEV, D_IN // N_DEV).transpose(1, 0, 2)


def kernel(x, positions, w_in, q_norm_g, w_uq, kv_norm_g, w_ukv, sgu_norm_g, sgu_norm_b, w_spatial, b_spatial, w_out, ln_g, ln_b, loss_target, m_w_in, m_q_norm_g, m_w_uq, m_kv_norm_g, m_w_ukv, m_sgu_norm_g, m_sgu_norm_b, m_w_spatial, m_b_spatial, m_w_out, m_ln_g, m_ln_b, v_w_in, v_q_norm_g, v_w_uq, v_kv_norm_g, v_w_ukv, v_sgu_norm_g, v_sgu_norm_b, v_w_spatial, v_b_spatial, v_w_out, v_ln_g, v_ln_b):
    me = 4 * lax.axis_index("x") + 2 * lax.axis_index("y") + lax.axis_index("c")
    seq = x.shape[1]
    x2 = x.reshape(seq, D_MODEL)
    tgt2 = loss_target.reshape(seq, D_MODEL)
    pos_col = positions.reshape(seq, 1)

    w_in_shards, w_out_shards, w_heads = _gather_two_level(
        [w_in.astype(BF16), w_out.astype(BF16), _head_slab(w_uq, w_ukv).astype(BF16)],
        name="wgather")
    (loss_part, grad_x, d_in, d_heads, d_out, d_ws, d_bs_t, d_lng, d_lnb, d_sgug, d_sgub, d_qg, d_kvg) = _local_step(
        x2, tgt2, pos_col, w_in_shards, w_heads, w_out_shards.reshape(D_MODEL, D_MODEL), q_norm_g, kv_norm_g,
        sgu_norm_g, sgu_norm_b, w_spatial, b_spatial, ln_g, ln_b)

    small_part = _pack_small([d_qg, d_kvg, d_sgug, d_sgub, d_bs_t[:, :HEADS].T, d_lng, d_lnb], last=loss_part[0, :1])
    mixed = jnp.concatenate([d_heads, d_ws, small_part.reshape(N_DEV, -1, LANES)], axis=1)
    by_chip = [g.reshape((N_CHIPS, 2) + g.shape[1:])
               for g in (d_in, d_out.reshape(N_DEV, D_MODEL // N_DEV, D_MODEL), mixed)]
    from_sibling = _sibling_swap(by_chip, name="gswap")
    core = lax.axis_index("c").astype(jnp.int32).reshape(1)
    pair_sums = [_pair_sum(a, b, core, name=nm, tile_rows=tr) for a, b, nm, tr in zip(
        by_chip, from_sibling, ("gsum_in", "gsum_out", "gsum_mixed"), (TOKEN_TILE, D_MODEL // N_DEV, MIXED_ROWS))]
    recv_in, recv_out, recv_mixed = _chip_exchange(pair_sums, name="gexch")

    take = lambda a: lax.dynamic_index_in_dim(a, me, 0, keepdims=False)
    small_w = _pack_small([q_norm_g, kv_norm_g, sgu_norm_g, sgu_norm_b, b_spatial, ln_g, ln_b])
    small_m = _pack_small([m_q_norm_g, m_kv_norm_g, m_sgu_norm_g, m_sgu_norm_b, m_b_spatial, m_ln_g, m_ln_b])
    small_v = _pack_small([v_q_norm_g, v_kv_norm_g, v_sgu_norm_g, v_sgu_norm_b, v_b_spatial, v_ln_g, v_ln_b])
    own_mixed = lambda uq, ukv, sp, small: jnp.concatenate(
        [_head_slab(uq, ukv), take(sp), take(small.reshape(N_DEV, -1, LANES))])
    res_in = _adam(recv_in, w_in, m_w_in, v_w_in, name="adam_in", tile_rows=TOKEN_TILE)
    res_out = _adam(recv_out, w_out, m_w_out, v_w_out, name="adam_out", tile_rows=D_MODEL // N_DEV)
    res_mixed = _adam(recv_mixed, own_mixed(w_uq, w_ukv, w_spatial, small_w), own_mixed(m_w_uq, m_w_ukv, m_w_spatial, small_m),
                      own_mixed(v_w_uq, v_w_ukv, v_w_spatial, small_v), name="adam_mixed", tile_rows=MIXED_ROWS)

    rep_g, = _exchange([res_mixed[0][HEAD_ROWS:]], name="sgather", per_destination=False)
    rep_pack = lambda sp, small: jnp.concatenate(
        [sp.reshape(N_DEV, CHUNK, LANES), small.reshape(N_DEV, -1, LANES)], axis=1).reshape(-1, LANES)
    _, delta_rep, m_rep, v_rep = _adam(rep_g.reshape(1, N_DEV * REP_ROWS, LANES), rep_pack(w_spatial, small_w),
                                       rep_pack(m_w_spatial, small_m), rep_pack(v_w_spatial, small_v),
                                       name="adam_rep", tile_rows=N_DEV * REP_ROWS)

    def rep_unpack(a):
        a = a.reshape(N_DEV, REP_ROWS, LANES)
        small = _unpack_small(a[:, CHUNK:].reshape(-1))
        return [small[0], small[1], small[2], small[3], a[:, :CHUNK], small[4], small[5], small[6]]

    def ordered(which, rep):
        r_qg, r_kvg, r_sg, r_sb, r_ws, r_bs, r_lg, r_lb = rep_unpack(rep)
        heads = res_mixed[which]
        return [res_in[which], r_qg, heads[:Q_LORA, :UQ_SHARD], r_kvg, heads[Q_LORA:HEAD_ROWS], r_sg, r_sb, r_ws, r_bs,
                res_out[which], r_lg, r_lb]

    loss = rep_g[N_DEV - 1, REP_ROWS - 1, LANES - 1]
    outs = [loss, grad_x.reshape(x.shape)]
    outs += ordered(0, rep_g.reshape(-1, LANES))
    outs += ordered(1, delta_rep)
    outs += ordered(2, m_rep)
    outs += ordered(3, v_rep)
    return tuple(outs)


def _local_step(x2, tgt2, pos_col, w_in_shards, w_heads, w_out_full, q_norm_g, kv_norm_g, sgu_norm_g, sgu_norm_b,
                w_spatial, b_spatial, ln_g, ln_b):
    wp_in = _padded_w_in(w_in_shards)

    half = jnp.arange(HALF, dtype=F32)
    inv_freq = 1.0 / (ROPE_THETA ** (half / HALF))
    invf_row = jnp.concatenate([jnp.zeros((KR_LO,), F32), inv_freq, inv_freq,
                                jnp.zeros((LANES - KR_LO - ROPE,), F32)]).reshape(1, LANES)
    tri = jnp.tril(jnp.ones((CHUNK, CHUNK), dtype=bool))
    ws_low = jnp.where(tri[None], w_spatial, 0.0).astype(BF16)
    ws_low_t = ws_low.transpose(0, 2, 1)
    bsp = jnp.repeat(b_spatial.T, G_HEAD_DIM, axis=1)
    row = lambda a: a.reshape(1, -1)

    proj, q, k, v, vt = _fwd_proj(x2, pos_col, invf_row, wp_in, w_heads, row(q_norm_g), row(kv_norm_g))
    o, lse_row = _attn_fwd(q, k, vt)
    (dr, do, d_row, drest, d_out, d_ws, d_bs_t, d_lng, d_lnb, d_sgug, d_sgub, loss_part) = _mid(
        x2, tgt2, proj, o, w_out_full, ws_low, ws_low_t, bsp, row(sgu_norm_g), row(sgu_norm_b), row(ln_g), row(ln_b))
    dqt, dk, dv = _attn_bwd(q, k, v, do, lse_row, d_row)
    dhead, d_heads, d_qg, d_kvg = _bwd_qkv(dqt, dk, dv, proj, pos_col, invf_row, w_heads, row(q_norm_g), row(kv_norm_g))
    grad_x, dwp_in = _bwd_in(x2, dr, dhead, drest, wp_in)
    return (loss_part, grad_x, _w_in_shards(dwp_in), d_heads, d_out, d_ws, d_bs_t, d_lng, d_lnb, d_sgug, d_sgub,
            d_qg, d_kvg)
```

```python
import functools
import math

import jax
import jax.numpy as jnp
from jax import lax
from jax.experimental import pallas as pl
from jax.experimental.pallas import tpu as pltpu

F32 = jnp.float32
BF16 = jnp.bfloat16

N_DEV = 8
D_MODEL = 1024
HEADS = 8
NOPE = 64
ROPE = 32
HALF = ROPE // 2
VDIM = 64
Q_LORA = 256
KV_LORA = 128
G_WIDTH = 512
G_HEAD_DIM = 64
CHUNK = 128
HEAD_PAD = 128
D_IN = 2464
D_IN_PAD = 2560
KR_LO = NOPE
ROPE_THETA = 10000.0
DN_ALPHA = 2.0 ** 0.25
EPS = 1e-5
ATTN_SCALE = 1.0 / math.sqrt(NOPE + ROPE)
ADAM_LR, ADAM_B1, ADAM_B2, ADAM_EPS, ADAM_WD, ADAM_STEP = 0.001, 0.9, 0.999, 1e-08, 0.01, 10

LANES = 128
REP_ROWS = 136
SMALL_LEN = 8192
VMEM_LIMIT = 56 * 1024 * 1024

TOKEN_TILE = 256
ATTN_WIDE = 1024
ATTN_NARROW = 512
SOFTMAX_ROWS = 256
LOG2E = 1.4426950408889634
LN2 = 0.6931471805599453
Q_PRESCALE = ATTN_SCALE * LOG2E


def _cparams(sem=None):
    return pltpu.CompilerParams(dimension_semantics=sem, vmem_limit_bytes=VMEM_LIMIT)


def _dot(a, b):
    return jnp.dot(a, b, preferred_element_type=F32)


def _dot_nt(a, b):
    return lax.dot_general(a, b, (((1,), (1,)), ((), ())), preferred_element_type=F32)


def _dot_tn(a, b):
    return lax.dot_general(a, b, (((0,), (0,)), ((), ())), preferred_element_type=F32)


def _as_row(col):
    return jnp.transpose(jnp.broadcast_to(col, (col.shape[0], LANES)))[0:1, :]


def _sigmoid(z):
    return 1.0 / (1.0 + jnp.exp(-z))


def _gelu(x):
    return 0.5 * x * (1.0 + lax.erf(x * 0.7071067811865476))


def _gelu_grad(x):
    cdf = 0.5 * (1.0 + lax.erf(x * 0.7071067811865476))
    return cdf + x * jnp.exp(-0.5 * x * x) * 0.3989422804014327


def _exchange(srcs, *, name, per_destination):
    n = len(srcs)
    slab_shapes = [s.shape[1:] if per_destination else s.shape for s in srcs]

    def body(*refs):
        src_refs, out_refs = refs[:n], refs[n:2 * n]
        send_sems, recv_sems, local_sems = refs[2 * n:]
        x, y, c = lax.axis_index("x"), lax.axis_index("y"), lax.axis_index("c")
        me = 4 * x + 2 * y + c

        def slab_for(t, dest):
            return src_refs[t].at[dest] if per_destination else src_refs[t]

        mine = [pltpu.make_async_copy(slab_for(t, me), out_refs[t].at[me], local_sems.at[t]) for t in range(n)]
        for cp in mine:
            cp.start()
        sends, arrivals = [], []
        for k in (6, 7, 4, 5, 2, 3, 1):
            px = 1 - x if k & 4 else x
            py = 1 - y if k & 2 else y
            pc = 1 - c if k & 1 else c
            peer = 4 * px + 2 * py + pc
            for t in range(n):
                sem = (k - 1) * n + t
                cp = pltpu.make_async_remote_copy(
                    src_ref=slab_for(t, peer), dst_ref=out_refs[t].at[me],
                    send_sem=send_sems.at[sem], recv_sem=recv_sems.at[sem],
                    device_id=(px, py, pc), device_id_type=pl.DeviceIdType.MESH)
                cp.start()
                sends.append(cp)
                arrivals.append(pltpu.make_async_remote_copy(
                    src_ref=slab_for(t, peer), dst_ref=out_refs[t].at[peer],
                    send_sem=send_sems.at[sem], recv_sem=recv_sems.at[sem],
                    device_id=(x, y, c), device_id_type=pl.DeviceIdType.MESH))
        for cp in arrivals:
            cp.wait_recv()
        for cp in sends:
            cp.wait_send()
        for cp in mine:
            cp.wait()

    hbm = pl.BlockSpec(memory_space=pl.ANY)
    return pl.pallas_call(
        body, name=name,
        out_shape=[jax.ShapeDtypeStruct((N_DEV,) + tuple(shape), s.dtype) for shape, s in zip(slab_shapes, srcs)],
        in_specs=[hbm] * n, out_specs=[hbm] * n,
        scratch_shapes=[pltpu.SemaphoreType.DMA(((N_DEV - 1) * n,)), pltpu.SemaphoreType.DMA(((N_DEV - 1) * n,)),
                        pltpu.SemaphoreType.DMA((n,))],
    )(*srcs)


def _gather_two_level(srcs, *, name):
    n = len(srcs)

    def body(*refs):
        src_refs, out_refs = refs[:n], refs[n:2 * n]
        send_sems, recv_sems, local_sems = refs[2 * n:]
        x, y, c = lax.axis_index("x"), lax.axis_index("y"), lax.axis_index("c")
        me, sibling = (x, y, c), (x, y, 1 - c)
        chips = [(1 - x, 1 - y), (1 - x, y), (x, 1 - y)]
        index = lambda px, py, pc: 4 * px + 2 * py + pc

        def copy(k, t, block, to, src=None):
            place = out_refs[t].at[index(*block)]
            return pltpu.make_async_remote_copy(
                src_ref=place if src is None else src, dst_ref=place,
                send_sem=send_sems.at[k * n + t], recv_sem=recv_sems.at[k * n + t],
                device_id=to, device_id_type=pl.DeviceIdType.MESH)

        mine = [pltpu.make_async_copy(src_refs[t], out_refs[t].at[index(*me)], local_sems.at[t]) for t in range(n)]
        for cp in mine:
            cp.start()
        first = [copy(1 + j, t, me, (*chip, c), src=src_refs[t]) for j, chip in enumerate(chips) for t in range(n)]
        first += [copy(0, t, me, sibling, src=src_refs[t]) for t in range(n)]
        for cp in first:
            cp.start()
        passed = []
        for j, chip in enumerate(chips):
            for t in range(n):
                copy(1 + j, t, (*chip, c), me).wait_recv()
                cp = copy(4 + j, t, (*chip, c), sibling)
                cp.start()
                passed.append(cp)
        for t in range(n):
            copy(0, t, sibling, me).wait_recv()
        for j, chip in enumerate(chips):
            for t in range(n):
                copy(4 + j, t, (*chip, 1 - c), me).wait_recv()
        for cp in first + passed:
            cp.wait_send()
        for cp in mine:
            cp.wait()

    hbm = pl.BlockSpec(memory_space=pl.ANY)
    return pl.pallas_call(
        body, name=name,
        out_shape=[jax.ShapeDtypeStruct((N_DEV,) + s.shape, s.dtype) for s in srcs],
        in_specs=[hbm] * n, out_specs=[hbm] * n,
        scratch_shapes=[pltpu.SemaphoreType.DMA((7 * n,)), pltpu.SemaphoreType.DMA((7 * n,)),
                        pltpu.SemaphoreType.DMA((n,))],
    )(*srcs)


N_CHIPS = N_DEV // 2


def _sibling_swap(srcs, *, name):
    n = len(srcs)

    def body(*refs):
        src_refs, out_refs = refs[:n], refs[n:2 * n]
        send_sems, recv_sems = refs[2 * n:]
        x, y, c = lax.axis_index("x"), lax.axis_index("y"), lax.axis_index("c")
        sends = []
        for chip in range(N_CHIPS):
            for t in range(n):
                cp = pltpu.make_async_remote_copy(
                    src_ref=src_refs[t].at[chip, 1 - c], dst_ref=out_refs[t].at[chip],
                    send_sem=send_sems.at[chip * n + t], recv_sem=recv_sems.at[chip * n + t],
                    device_id=(x, y, 1 - c), device_id_type=pl.DeviceIdType.MESH)
                cp.start()
                sends.append(cp)
        for cp in sends:
            cp.wait_recv()
        for cp in sends:
            cp.wait_send()

    hbm = pl.BlockSpec(memory_space=pl.ANY)
    return pl.pallas_call(
        body, name=name,
        out_shape=[jax.ShapeDtypeStruct((N_CHIPS,) + s.shape[2:], s.dtype) for s in srcs],
        in_specs=[hbm] * n, out_specs=[hbm] * n,
        scratch_shapes=[pltpu.SemaphoreType.DMA((N_CHIPS * n,)), pltpu.SemaphoreType.DMA((N_CHIPS * n,))],
    )(*srcs)


def _pair_sum(mine, theirs, core, *, name, tile_rows, out_dtype):
    _, _, rows, cols = mine.shape

    def body(core_ref, a_ref, b_ref, o_ref):
        o_ref[...] = (a_ref[0] + b_ref[...]).astype(out_dtype)

    return pl.pallas_call(
        body, name=name,
        grid_spec=pltpu.PrefetchScalarGridSpec(
            num_scalar_prefetch=1, grid=(N_CHIPS, rows // tile_rows),
            in_specs=[pl.BlockSpec((1, 1, tile_rows, cols), lambda q, r, core_ref: (q, core_ref[0], r, 0)),
                      pl.BlockSpec((1, tile_rows, cols), lambda q, r, core_ref: (q, r, 0))],
            out_specs=pl.BlockSpec((1, tile_rows, cols), lambda q, r, core_ref: (q, r, 0))),
        out_shape=jax.ShapeDtypeStruct((N_CHIPS, rows, cols), out_dtype),
        compiler_params=_cparams(("arbitrary", "arbitrary")),
    )(core, mine, theirs)


def _chip_exchange(srcs, *, name):
    n = len(srcs)

    def body(*refs):
        src_refs, out_refs = refs[:n], refs[n:2 * n]
        send_sems, recv_sems, local_sems = refs[2 * n:]
        x, y, c = lax.axis_index("x"), lax.axis_index("y"), lax.axis_index("c")
        my_chip = 2 * x + y
        mine = [pltpu.make_async_copy(src_refs[t].at[my_chip], out_refs[t].at[my_chip], local_sems.at[t])
                for t in range(n)]
        for cp in mine:
            cp.start()
        sends, arrivals = [], []
        for k in (3, 2, 1):
            px = 1 - x if k & 2 else x
            py = 1 - y if k & 1 else y
            peer_chip = 2 * px + py
            for t in range(n):
                sem = (k - 1) * n + t
                cp = pltpu.make_async_remote_copy(
                    src_ref=src_refs[t].at[peer_chip], dst_ref=out_refs[t].at[my_chip],
                    send_sem=send_sems.at[sem], recv_sem=recv_sems.at[sem],
                    device_id=(px, py, c), device_id_type=pl.DeviceIdType.MESH)
                cp.start()
                sends.append(cp)
                arrivals.append(pltpu.make_async_remote_copy(
                    src_ref=src_refs[t].at[peer_chip], dst_ref=out_refs[t].at[peer_chip],
                    send_sem=send_sems.at[sem], recv_sem=recv_sems.at[sem],
                    device_id=(x, y, c), device_id_type=pl.DeviceIdType.MESH))
        for cp in arrivals:
            cp.wait_recv()
        for cp in sends:
            cp.wait_send()
        for cp in mine:
            cp.wait()

    hbm = pl.BlockSpec(memory_space=pl.ANY)
    return pl.pallas_call(
        body, name=name,
        out_shape=[jax.ShapeDtypeStruct(s.shape, s.dtype) for s in srcs],
        in_specs=[hbm] * n, out_specs=[hbm] * n,
        scratch_shapes=[pltpu.SemaphoreType.DMA((3 * n,)), pltpu.SemaphoreType.DMA((3 * n,)),
                        pltpu.SemaphoreType.DMA((n,))],
    )(*srcs)


def _rope_tables(pos_col, invf_row):
    ang = pos_col.astype(F32) * invf_row
    lane = lax.broadcasted_iota(jnp.int32, ang.shape, 1)
    cos, sin = jnp.cos(ang), jnp.sin(ang)
    first = (lane >= KR_LO) & (lane < KR_LO + HALF)
    second = (lane >= KR_LO + HALF) & (lane < KR_LO + ROPE)
    return cos, jnp.where(first, sin, 0.0), jnp.where(second, sin, 0.0)


def _rope(t, cos, sin_first, sin_second, sign):
    up = pltpu.roll(t, LANES - HALF, 1)
    down = pltpu.roll(t, HALF, 1)
    return t * cos - sign * (up * sin_first) + sign * (down * sin_second)


def _fwd_proj(x, pos_col, invf_row, wp_in, w_heads, q_g, kv_g):
    t = x.shape[0]
    tm = TOKEN_TILE

    def body(x_ref, pos_ref, invf_ref, win_ref, wh_ref, qg_ref, kvg_ref,
             proj_ref, q_ref, k_ref, v_ref, vt_ref):
        proj = _dot(x_ref[...].astype(BF16), win_ref[...])
        proj_ref[...] = proj
        c_q = proj[:, :Q_LORA]
        c_kv = proj[:, Q_LORA:Q_LORA + KV_LORA]
        kr_raw = proj[:, Q_LORA + KV_LORA:Q_LORA + KV_LORA + LANES]
        cqn = (c_q * lax.rsqrt(jnp.mean(c_q * c_q, axis=-1, keepdims=True) + EPS) * qg_ref[...]).astype(BF16)
        ckvn = (c_kv * lax.rsqrt(jnp.mean(c_kv * c_kv, axis=-1, keepdims=True) + EPS) * kvg_ref[...]).astype(BF16)
        cos, s1, s2 = _rope_tables(pos_ref[...], invf_ref[...])
        kr = _rope(kr_raw, cos, s1, s2, 1.0)
        lane = lax.broadcasted_iota(jnp.int32, (tm, HEAD_PAD), 1)
        for h in range(HEADS):
            q_h = _dot(cqn, wh_ref[h, :Q_LORA, :])
            kv_h = _dot(ckvn, wh_ref[h, Q_LORA:, :])
            q_ref[h] = (_rope(q_h, cos, s1, s2, 1.0) * Q_PRESCALE).astype(BF16)
            k_ref[h] = jnp.where(lane < NOPE, kv_h, kr).astype(BF16)
            v_ref[h] = kv_h.astype(BF16)
            vt_ref[h] = jnp.transpose(kv_h).astype(BF16)

    full = lambda a: pl.BlockSpec(a.shape, lambda i: (0,) * a.ndim)
    head_spec = pl.BlockSpec((HEADS, tm, HEAD_PAD), lambda i: (0, i, 0))
    head_shape = jax.ShapeDtypeStruct((HEADS, t, HEAD_PAD), BF16)
    return pl.pallas_call(
        body, name="fwd_proj", grid=(t // tm,),
        in_specs=[pl.BlockSpec((tm, D_MODEL), lambda i: (i, 0)), pl.BlockSpec((tm, 1), lambda i: (i, 0)),
                  full(invf_row), full(wp_in), full(w_heads), full(q_g), full(kv_g)],
        out_specs=[pl.BlockSpec((tm, D_IN_PAD), lambda i: (i, 0)), head_spec, head_spec, head_spec,
                   pl.BlockSpec((HEADS, HEAD_PAD, tm), lambda i: (0, 0, i))],
        out_shape=[jax.ShapeDtypeStruct((t, D_IN_PAD), F32), head_shape, head_shape, head_shape,
                   jax.ShapeDtypeStruct((HEADS, HEAD_PAD, t), BF16)],
        compiler_params=_cparams(("arbitrary",)),
    )(x, pos_col, invf_row, wp_in, w_heads, q_g, kv_g)


def _attn_fwd(q, k, vt):
    t = q.shape[1]
    bq, bk = ATTN_WIDE, ATTN_NARROW
    chunk = SOFTMAX_ROWS

    def body(q_ref, k_ref, vt_ref, o_ref, lse_ref, s0, s1, p0, p1, x0, x1, m_scr, l_scr, a_scr, acc_scr):
        i = pl.program_id(1)
        at = lambda j: pl.ds(pl.multiple_of(j * bk, bk), bk)

        def exp_pass(s_in, block_max, p_out, key0=None):
            def load(r):
                s = s_in[r:r + chunk, :]
                if key0 is not None:
                    key = lax.broadcasted_iota(jnp.int32, (chunk, bq), 0) + (r + key0)
                    qry = lax.broadcasted_iota(jnp.int32, (chunk, bq), 1)
                    s = jnp.where(qry >= key, s, -jnp.inf)
                return s

            if key0 is not None:
                block_max = jnp.max(load(0), axis=0, keepdims=True)
                for r in range(chunk, bk, chunk):
                    block_max = jnp.maximum(block_max, jnp.max(load(r), axis=0, keepdims=True))
            m_new = jnp.maximum(m_scr[...], block_max)
            alpha = jnp.exp2(m_scr[...] - m_new)
            total = jnp.zeros((1, bq), F32)
            for r in range(0, bk, chunk):
                p = jnp.exp2(load(r) - m_new)
                p_out[r:r + chunk, :] = p.astype(BF16)
                total = total + jnp.sum(p, axis=0, keepdims=True)
            m_scr[...] = m_new
            l_scr[...] = alpha * l_scr[...] + total
            return alpha

        def scores(j, s_out, x_out):
            s = _dot_nt(k_ref[0, at(j), :], q_ref[0])
            s_out[...] = s
            x_out[...] = jnp.max(s, axis=0, keepdims=True)

        def value_product(j, p_in):
            return _dot(vt_ref[0, :, at(j)], p_in[...])

        def one_pass(j, s_in, x_in, s_out, x_out, p_prev, p_cur):
            scores(j + 1, s_out, x_out)
            acc_scr[...] = a_scr[...] * acc_scr[...] + value_product(jnp.maximum(j - 1, 0), p_prev)
            a_scr[...] = exp_pass(s_in, x_in[...], p_cur)

        scores(0, s0, x0)
        p1[...] = jnp.zeros_like(p1)
        a_scr[...] = jnp.ones_like(a_scr)
        m_scr[...] = jnp.full(m_scr.shape, -jnp.inf, F32)
        l_scr[...] = jnp.zeros_like(l_scr)
        acc_scr[...] = jnp.zeros_like(acc_scr)

        def two_passes(n, _):
            one_pass(2 * n, s0, x0, s1, x1, p1, p0)
            one_pass(2 * n + 1, s1, x1, s0, x0, p0, p1)
            return 0

        lax.fori_loop(0, i, two_passes, 0)
        d = 2 * i
        scores(d + 1, s1, x1)
        acc = a_scr[...] * acc_scr[...] + value_product(jnp.maximum(d - 1, 0), p1)
        alpha = exp_pass(s0, None, p0, key0=0)
        acc = alpha * acc + value_product(d, p0)
        alpha = exp_pass(s1, None, p1, key0=bk)
        acc = alpha * acc + value_product(d + 1, p1)
        o_ref[0] = jnp.transpose(acc / l_scr[...])
        lse_ref[0] = m_scr[...] + jnp.log2(l_scr[...])

    tile = lambda dtype: pltpu.VMEM((bk, bq), dtype)
    stat = pltpu.VMEM((1, bq), F32)
    return pl.pallas_call(
        body, name="attn_fwd", grid=(HEADS, t // bq),
        in_specs=[pl.BlockSpec((1, bq, HEAD_PAD), lambda h, i: (h, i, 0)),
                  pl.BlockSpec((1, t, HEAD_PAD), lambda h, i: (h, 0, 0)),
                  pl.BlockSpec((1, HEAD_PAD, t), lambda h, i: (h, 0, 0))],
        out_specs=[pl.BlockSpec((1, bq, HEAD_PAD), lambda h, i: (h, i, 0)),
                   pl.BlockSpec((1, 1, bq), lambda h, i: (h, 0, i))],
        out_shape=[jax.ShapeDtypeStruct((HEADS, t, HEAD_PAD), F32), jax.ShapeDtypeStruct((HEADS, 1, t), F32)],
        scratch_shapes=[tile(F32), tile(F32), tile(BF16), tile(BF16), stat, stat, stat, stat, stat,
                        pltpu.VMEM((HEAD_PAD, bq), F32)],
        compiler_params=_cparams(("arbitrary", "arbitrary")),
    )(q, k, vt)


def _mid(x, target, proj, ol, w_out, ws_low, ws_low_t, bsp, sgu_g, sgu_b, ln_g, ln_b):
    t = x.shape[0]
    tm = TOKEN_TILE
    n_steps = t // tm

    def body(x_ref, tgt_ref, za_ref, u_ref, v_ref, zb_ref, ol_ref, wout_ref, ws_ref, wst_ref, bsp_ref,
             sg_ref, sb_ref, lg_ref, lb_ref,
             dr_ref, do_ref, drow_ref, drest_ref, dwout_ref, dws_ref, dbs_ref, dlg_ref, dlb_ref, dsg_ref, dsb_ref,
             loss_ref, dbsp_acc):
        step = pl.program_id(0)

        @pl.when(step == 0)
        def _():
            dwout_ref[...] = jnp.zeros_like(dwout_ref)
            dws_ref[...] = jnp.zeros_like(dws_ref)
            dbs_ref[...] = jnp.zeros_like(dbs_ref)
            dlg_ref[...] = jnp.zeros_like(dlg_ref)
            dlb_ref[...] = jnp.zeros_like(dlb_ref)
            dsg_ref[...] = jnp.zeros_like(dsg_ref)
            dsb_ref[...] = jnp.zeros_like(dsb_ref)
            loss_ref[...] = jnp.zeros_like(loss_ref)
            dbsp_acc[...] = jnp.zeros_like(dbsp_acc)

        lane_head = lax.broadcasted_iota(jnp.int32, (CHUNK, G_WIDTH), 1) // G_HEAD_DIM

        attn = jnp.concatenate([ol_ref[h][:, NOPE:] for h in range(HEADS)], axis=-1)
        za = za_ref[...]
        sig_a = _sigmoid(za)
        silu_a = za * sig_a
        out_a = attn * silu_a
        u = u_ref[...]
        ug = _gelu(u)
        vpre = v_ref[...]
        gv = _gelu(vpre)
        mu_v = jnp.mean(gv, axis=-1, keepdims=True)
        cen_v = gv - mu_v
        rstd_v = lax.rsqrt(jnp.mean(cen_v * cen_v, axis=-1, keepdims=True) + EPS)
        vhat = cen_v * rstd_v
        vg = vhat * sg_ref[...] + sb_ref[...]
        vg_b = vg.astype(BF16)
        sv_parts = []
        for cix in range(tm // CHUNK):
            vc = vg_b[cix * CHUNK:(cix + 1) * CHUNK, :]
            acc = bsp_ref[...]
            for h in range(HEADS):
                acc = acc + jnp.where(lane_head == h, _dot(ws_ref[h], vc), 0.0)
            sv_parts.append(acc)
        sv = jnp.concatenate(sv_parts, axis=0)
        sgu = ug * sv
        zb = zb_ref[...]
        sig_b = _sigmoid(zb)
        silu_b = zb * sig_b
        out_b = sgu * silu_b
        merged = jnp.concatenate([out_a, out_b], axis=-1).astype(BF16)
        r = DN_ALPHA * x_ref[...] + _dot(merged, wout_ref[...])
        mu = jnp.mean(r, axis=-1, keepdims=True)
        cen = r - mu
        rstd = lax.rsqrt(jnp.mean(cen * cen, axis=-1, keepdims=True) + EPS)
        xhat = cen * rstd
        hout = xhat * lg_ref[...] + lb_ref[...]
        err = hout - tgt_ref[...]
        row_loss = jnp.mean(err * err, axis=-1, keepdims=True)
        loss_ref[...] += jnp.broadcast_to(0.5 * jnp.sum(row_loss, axis=0, keepdims=True), loss_ref.shape)

        dh = err * (1.0 / D_MODEL)
        dlg_ref[...] += jnp.sum(dh * xhat, axis=0, keepdims=True)
        dlb_ref[...] += jnp.sum(dh, axis=0, keepdims=True)
        dxhat = dh * lg_ref[...]
        dr = rstd * (dxhat - jnp.mean(dxhat, axis=-1, keepdims=True)
                     - xhat * jnp.mean(dxhat * xhat, axis=-1, keepdims=True))
        dr_ref[...] = dr
        dr_b = dr.astype(BF16)
        dwout_ref[...] += _dot_tn(merged, dr_b)
        dmerged = _dot_nt(dr_b, wout_ref[...])
        d_out_a = dmerged[:, :G_WIDTH]
        d_out_b = dmerged[:, G_WIDTH:]
        dattn = d_out_a * silu_a
        for h in range(HEADS):
            do_h = dattn[:, h * VDIM:(h + 1) * VDIM]
            dsum = jnp.sum(do_h * ol_ref[h][:, NOPE:], axis=-1, keepdims=True)
            drow_ref[h] = _as_row(dsum)
            do_ref[h] = jnp.concatenate([jnp.zeros((tm, NOPE), F32), do_h], axis=-1).astype(BF16)
        dza = d_out_a * attn * (sig_a * (1.0 + za * (1.0 - sig_a)))
        dsgu = d_out_b * silu_b
        dzb = d_out_b * sgu * (sig_b * (1.0 + zb * (1.0 - sig_b)))
        du = dsgu * sv * _gelu_grad(u)
        dsv = dsgu * ug
        dsv_b = dsv.astype(BF16)
        dvg_parts = []
        for cix in range(tm // CHUNK):
            rows = slice(cix * CHUNK, (cix + 1) * CHUNK)
            dsv_c = dsv[rows, :]
            dsv_cb = dsv_b[rows, :]
            vc = vg_b[rows, :]
            dbsp_acc[...] += dsv_c
            acc = jnp.zeros((CHUNK, G_WIDTH), F32)
            for h in range(HEADS):
                on = lane_head == h
                acc = acc + jnp.where(on, _dot(wst_ref[h], dsv_cb), 0.0)
                dws_ref[h] += _dot_nt(jnp.where(on, dsv_cb, jnp.zeros_like(dsv_cb)), vc)
            dvg_parts.append(acc)
        dvg = jnp.concatenate(dvg_parts, axis=0)
        dsg_ref[...] += jnp.sum(dvg * vhat, axis=0, keepdims=True)
        dsb_ref[...] += jnp.sum(dvg, axis=0, keepdims=True)
        dvhat = dvg * sg_ref[...]
        dgv = rstd_v * (dvhat - jnp.mean(dvhat, axis=-1, keepdims=True)
                        - vhat * jnp.mean(dvhat * vhat, axis=-1, keepdims=True))
        dv = dgv * _gelu_grad(vpre)
        drest_ref[...] = jnp.concatenate([dza, du, dv, dzb], axis=-1).astype(BF16)

        @pl.when(step == n_steps - 1)
        def _():
            tri = (lax.broadcasted_iota(jnp.int32, (CHUNK, CHUNK), 0)
                   >= lax.broadcasted_iota(jnp.int32, (CHUNK, CHUNK), 1))
            for h in range(HEADS):
                dws_ref[h] = jnp.where(tri, dws_ref[h], 0.0)
            tot = dbsp_acc[...]
            lane = lax.broadcasted_iota(jnp.int32, (CHUNK, LANES), 1)
            dbs = jnp.zeros((CHUNK, LANES), F32)
            for h in range(HEADS):
                head_sum = jnp.sum(tot[:, h * G_HEAD_DIM:(h + 1) * G_HEAD_DIM], axis=-1, keepdims=True)
                dbs = jnp.where(lane == h, head_sum, dbs)
            dbs_ref[...] = dbs

    full = lambda a: pl.BlockSpec(a.shape, lambda i: (0,) * a.ndim)
    tile = lambda w, j=0: pl.BlockSpec((tm, w), lambda i, j=j: (i, j))
    heads = pl.BlockSpec((HEADS, tm, HEAD_PAD), lambda i: (0, i, 0))
    acc = lambda shape: (pl.BlockSpec(shape, lambda i: (0,) * len(shape)), jax.ShapeDtypeStruct(shape, F32))
    accs = [acc((D_MODEL, D_MODEL)), acc((HEADS, CHUNK, CHUNK)), acc((CHUNK, LANES)), acc((1, D_MODEL)),
            acc((1, D_MODEL)), acc((1, G_WIDTH)), acc((1, G_WIDTH)), acc((1, LANES))]
    return pl.pallas_call(
        body, name="mid", grid=(n_steps,),
        in_specs=[tile(D_MODEL), tile(D_MODEL), tile(G_WIDTH, 1), tile(G_WIDTH, 2), tile(G_WIDTH, 3), tile(G_WIDTH, 4),
                  heads, full(w_out), full(ws_low), full(ws_low_t), full(bsp), full(sgu_g), full(sgu_b),
                  full(ln_g), full(ln_b)],
        out_specs=[tile(D_MODEL), heads, pl.BlockSpec((HEADS, 1, tm), lambda i: (0, 0, i)), tile(4 * G_WIDTH)]
        + [a[0] for a in accs],
        out_shape=[jax.ShapeDtypeStruct((t, D_MODEL), F32), jax.ShapeDtypeStruct((HEADS, t, HEAD_PAD), BF16),
                   jax.ShapeDtypeStruct((HEADS, 1, t), F32), jax.ShapeDtypeStruct((t, 4 * G_WIDTH), BF16)]
        + [a[1] for a in accs],
        scratch_shapes=[pltpu.VMEM((CHUNK, G_WIDTH), F32)],
        compiler_params=_cparams(("arbitrary",)),
    )(x, target, proj, proj, proj, proj, ol, w_out, ws_low, ws_low_t, bsp, sgu_g, sgu_b, ln_g, ln_b)


def _attn_bwd(q, k, v, do, lse_row, d_row):
    t = q.shape[1]
    bk, bq = ATTN_WIDE, ATTN_NARROW
    last = t // bq - 1
    chunk = SOFTMAX_ROWS

    def body(q_ref, k_ref, v_ref, do_ref, lse_ref, drow_ref, dqt_ref, dk_ref, dv_ref,
             s0, s1, e0, e1, p0, p1, g0, g1, kt_scr):
        j = pl.program_id(1)
        at = lambda i: pl.ds(pl.multiple_of(i * bq, bq), bq)

        @pl.when(j == 0)
        def _():
            dqt_ref[...] = jnp.zeros_like(dqt_ref)

        kt_scr[...] = jnp.transpose(k_ref[0].astype(F32)).astype(BF16)
        dk_ref[...] = jnp.zeros_like(dk_ref)
        dv_ref[...] = jnp.zeros_like(dv_ref)

        def products(i, s_out, e_out):
            i = jnp.minimum(i, last)
            s_out[...] = _dot_nt(k_ref[0], q_ref[0, at(i), :])
            e_out[...] = _dot_nt(v_ref[0], do_ref[0, at(i), :])

        def gradients(i, p_in, g_in):
            dv_ref[0] += _dot(p_in[...], do_ref[0, at(i), :])
            dk_ref[0] += _dot(g_in[...], q_ref[0, at(i), :])
            dqt_ref[0, :, at(i)] += _dot(kt_scr[...], g_in[...])

        def elementwise(i, s_in, e_in, p_out, g_out, qry0=None):
            lse = lse_ref[0, :, at(i)]
            dsum = drow_ref[0, :, at(i)]
            for r in range(0, bk, chunk):
                p = jnp.exp2(s_in[r:r + chunk, :] - lse)
                if qry0 is not None:
                    key = lax.broadcasted_iota(jnp.int32, (chunk, bq), 0) + r
                    qry = lax.broadcasted_iota(jnp.int32, (chunk, bq), 1) + qry0
                    p = jnp.where(qry >= key, p, 0.0)
                p_out[r:r + chunk, :] = p.astype(BF16)
                g_out[r:r + chunk, :] = (p * (e_in[r:r + chunk, :] - dsum)).astype(BF16)

        def one_pass(i, s_in, e_in, s_out, e_out, p_prev, g_prev, p_cur, g_cur, qry0=None):
            products(i + 1, s_out, e_out)
            gradients(i - 1, p_prev, g_prev)
            elementwise(i, s_in, e_in, p_cur, g_cur, qry0)

        first = 2 * j
        products(first, s0, e0)
        products(first + 1, s1, e1)
        elementwise(first, s0, e0, p0, g0, qry0=0)
        one_pass(first + 1, s1, e1, s0, e0, p0, g0, p1, g1, qry0=bq)

        def two_passes(n, _):
            i = first + 2 + 2 * n
            one_pass(i, s0, e0, s1, e1, p1, g1, p0, g0)
            one_pass(i + 1, s1, e1, s0, e0, p0, g0, p1, g1)
            return 0

        lax.fori_loop(0, (last - first - 1) // 2, two_passes, 0)
        gradients(last, p1, g1)
        dk_ref[0] = dk_ref[0] * LN2

    whole = pl.BlockSpec((1, t, HEAD_PAD), lambda h, j: (h, 0, 0))
    block = pl.BlockSpec((1, bk, HEAD_PAD), lambda h, j: (h, j, 0))
    rows = pl.BlockSpec((1, 1, t), lambda h, j: (h, 0, 0))
    shape = jax.ShapeDtypeStruct((HEADS, t, HEAD_PAD), F32)
    tile = lambda dtype: pltpu.VMEM((bk, bq), dtype)
    return pl.pallas_call(
        body, name="attn_bwd", grid=(HEADS, t // bk),
        in_specs=[whole, block, block, whole, rows, rows],
        out_specs=[pl.BlockSpec((1, HEAD_PAD, t), lambda h, j: (h, 0, 0)), block, block],
        out_shape=[jax.ShapeDtypeStruct((HEADS, HEAD_PAD, t), F32), shape, shape],
        scratch_shapes=[tile(F32), tile(F32), tile(F32), tile(F32), tile(BF16), tile(BF16),
                        tile(BF16), tile(BF16), pltpu.VMEM((HEAD_PAD, bk), BF16)],
        compiler_params=_cparams(("arbitrary", "arbitrary")),
    )(q, k, v, do, lse_row, d_row)


def _bwd_qkv(dq, dk, dv, proj, pos_col, invf_row, w_heads, q_g, kv_g):
    t = proj.shape[0]
    tm = TOKEN_TILE

    def body(dq_ref, dk_ref, dv_ref, ph_ref, pos_ref, invf_ref, wh_ref, qg_ref, kvg_ref,
             dhead_ref, dwh_ref, dqg_ref, dkvg_ref):
        @pl.when(pl.program_id(0) == 0)
        def _():
            dwh_ref[...] = jnp.zeros_like(dwh_ref)
            dqg_ref[...] = jnp.zeros_like(dqg_ref)
            dkvg_ref[...] = jnp.zeros_like(dkvg_ref)

        cos, s1, s2 = _rope_tables(pos_ref[...], invf_ref[...])
        lane = lax.broadcasted_iota(jnp.int32, (tm, LANES), 1)
        c_q = ph_ref[:, :Q_LORA]
        c_kv = ph_ref[:, Q_LORA:Q_LORA + KV_LORA]
        rstd_q = lax.rsqrt(jnp.mean(c_q * c_q, axis=-1, keepdims=True) + EPS)
        rstd_kv = lax.rsqrt(jnp.mean(c_kv * c_kv, axis=-1, keepdims=True) + EPS)
        qhat = c_q * rstd_q
        kvhat = c_kv * rstd_kv
        cqn = (qhat * qg_ref[...]).astype(BF16)
        ckvn = (kvhat * kvg_ref[...]).astype(BF16)
        dcqn = jnp.zeros((tm, Q_LORA), F32)
        dckvn = jnp.zeros((tm, KV_LORA), F32)
        dkr_rot = jnp.zeros((tm, LANES), F32)
        for h in range(HEADS):
            dq_b = _rope(jnp.transpose(dq_ref[h]) * ATTN_SCALE, cos, s1, s2, -1.0).astype(BF16)
            dk_h = dk_ref[h]
            dkv_b = jnp.where(lane < NOPE, dk_h, dv_ref[h]).astype(BF16)
            dkr_rot = dkr_rot + dk_h
            dwh_ref[h, :Q_LORA, :] += _dot_tn(cqn, dq_b)
            dwh_ref[h, Q_LORA:, :] += _dot_tn(ckvn, dkv_b)
            dcqn = dcqn + _dot_nt(dq_b, wh_ref[h, :Q_LORA, :])
            dckvn = dckvn + _dot_nt(dkv_b, wh_ref[h, Q_LORA:, :])
        rot_lanes = (lane >= KR_LO) & (lane < KR_LO + ROPE)
        dkr_raw = jnp.where(rot_lanes, _rope(dkr_rot, cos, s1, s2, -1.0), 0.0)
        dqg_ref[...] += jnp.sum(dcqn * qhat, axis=0, keepdims=True)
        dkvg_ref[...] += jnp.sum(dckvn * kvhat, axis=0, keepdims=True)
        dqh = dcqn * qg_ref[...]
        dkvh = dckvn * kvg_ref[...]
        dc_q = rstd_q * (dqh - qhat * jnp.mean(dqh * qhat, axis=-1, keepdims=True))
        dc_kv = rstd_kv * (dkvh - kvhat * jnp.mean(dkvh * kvhat, axis=-1, keepdims=True))
        dhead_ref[...] = jnp.concatenate([dc_q, dc_kv, dkr_raw], axis=-1).astype(BF16)

    full = lambda a: pl.BlockSpec(a.shape, lambda i: (0,) * a.ndim)
    heads = pl.BlockSpec((HEADS, tm, HEAD_PAD), lambda i: (0, i, 0))
    acc = lambda shape: (pl.BlockSpec(shape, lambda i: (0,) * len(shape)), jax.ShapeDtypeStruct(shape, F32))
    accs = [acc(w_heads.shape), acc((1, Q_LORA)), acc((1, KV_LORA))]
    return pl.pallas_call(
        body, name="bwd_qkv", grid=(t // tm,),
        in_specs=[pl.BlockSpec((HEADS, HEAD_PAD, tm), lambda i: (0, 0, i)), heads, heads,
                  pl.BlockSpec((tm, 4 * LANES), lambda i: (i, 0)),
                  pl.BlockSpec((tm, 1), lambda i: (i, 0)), full(invf_row), full(w_heads),
                  full(q_g), full(kv_g)],
        out_specs=[pl.BlockSpec((tm, 4 * LANES), lambda i: (i, 0))] + [a[0] for a in accs],
        out_shape=[jax.ShapeDtypeStruct((t, 4 * LANES), BF16)] + [a[1] for a in accs],
        compiler_params=_cparams(("arbitrary",)),
    )(dq, dk, dv, proj, pos_col, invf_row, w_heads, q_g, kv_g)


def _bwd_in(x, dr, dhead, drest, wp_in):
    t = x.shape[0]
    tm = TOKEN_TILE
    n_head = dhead.shape[1]

    def body(x_ref, dr_ref, dhead_ref, drest_ref, win_ref, gx_ref, dwin_ref):
        @pl.when(pl.program_id(0) == 0)
        def _():
            dwin_ref[...] = jnp.zeros_like(dwin_ref)

        xb = x_ref[...].astype(BF16)
        dh_b = dhead_ref[...]
        dr_b = drest_ref[...]
        gx_ref[...] = (DN_ALPHA * dr_ref[...] + _dot_nt(dh_b, win_ref[:, :n_head])
                       + _dot_nt(dr_b, win_ref[:, n_head:]))
        dwin_ref[:, :n_head] += _dot_tn(xb, dh_b)
        dwin_ref[:, n_head:] += _dot_tn(xb, dr_b)

    tile = lambda w: pl.BlockSpec((tm, w), lambda i: (i, 0))
    whole = pl.BlockSpec(wp_in.shape, lambda i: (0, 0))
    return pl.pallas_call(
        body, name="bwd_in", grid=(t // tm,),
        in_specs=[tile(D_MODEL), tile(D_MODEL), tile(n_head), tile(drest.shape[1]), whole],
        out_specs=[tile(D_MODEL), whole],
        out_shape=[jax.ShapeDtypeStruct((t, D_MODEL), F32), jax.ShapeDtypeStruct(wp_in.shape, F32)],
        compiler_params=_cparams(("arbitrary",)),
    )(x, dr, dhead, drest, wp_in)


def _adam(parts, w, m, v, *, name, tile_rows):
    n, rows, cols = parts.shape

    def body(p_ref, w_ref, m_ref, v_ref, g_ref, d_ref, nm_ref, nv_ref):
        g = p_ref[0].astype(F32)
        for s in range(1, n):
            g = g + p_ref[s].astype(F32)
        m_new = ADAM_B1 * m_ref[...] + (1.0 - ADAM_B1) * g
        v_new = ADAM_B2 * v_ref[...] + (1.0 - ADAM_B2) * (g * g)
        m_hat = m_new / (1.0 - ADAM_B1 ** ADAM_STEP)
        v_hat = v_new / (1.0 - ADAM_B2 ** ADAM_STEP)
        g_ref[...] = g
        d_ref[...] = -ADAM_LR * (m_hat / (jnp.sqrt(v_hat) + ADAM_EPS) + ADAM_WD * w_ref[...])
        nm_ref[...] = m_new
        nv_ref[...] = v_new

    flat = pl.BlockSpec((tile_rows, cols), lambda i: (i, 0))
    shape = jax.ShapeDtypeStruct((rows, cols), F32)
    return pl.pallas_call(
        body, name=name, grid=(rows // tile_rows,),
        in_specs=[pl.BlockSpec((n, tile_rows, cols), lambda i: (0, i, 0)), flat, flat, flat],
        out_specs=[flat] * 4, out_shape=[shape] * 4,
        compiler_params=_cparams(("arbitrary",)),
    )(parts, w, m, v)


SMALL_NAMES = ("q_norm_g", "kv_norm_g", "sgu_norm_g", "sgu_norm_b", "b_spatial", "ln_g", "ln_b")
SMALL_SIZES = (Q_LORA, KV_LORA, G_WIDTH, G_WIDTH, HEADS * CHUNK, D_MODEL, D_MODEL)


def _pack_small(vals, last=None):
    flat = jnp.concatenate([v.reshape(-1) for v in vals])
    pad = SMALL_LEN - flat.shape[0]
    if last is None:
        return jnp.pad(flat, (0, pad))
    return jnp.concatenate([flat, jnp.zeros((pad - 1,), F32), last.reshape(1)])


def _unpack_small(flat):
    out, at = [], 0
    for n in SMALL_SIZES:
        out.append(flat[at:at + n])
        at += n
    out[4] = out[4].reshape(HEADS, CHUNK)
    return out


UQ_SHARD = HEADS * (NOPE + ROPE) // N_DEV
HEAD_ROWS = Q_LORA + KV_LORA
MIXED_ROWS = HEAD_ROWS + CHUNK + SMALL_LEN // N_DEV // LANES


def _head_slab(w_uq_shard, w_ukv_shard):
    return jnp.concatenate([jnp.pad(w_uq_shard, ((0, 0), (0, LANES - UQ_SHARD))), w_ukv_shard])


def _padded_w_in(shards):
    full = shards.transpose(1, 0, 2).reshape(D_MODEL, D_IN)
    z = lambda c: jnp.zeros((D_MODEL, c), shards.dtype)
    split = Q_LORA + KV_LORA
    return jnp.concatenate([full[:, :split], z(KR_LO), full[:, split:split + ROPE], z(LANES - KR_LO - ROPE),
                            full[:, split + ROPE:]], axis=1)


def _w_in_shards(dwp_in):
    split = Q_LORA + KV_LORA
    full = jnp.concatenate([dwp_in[:, :split], dwp_in[:, split + KR_LO:split + KR_LO + ROPE],
                            dwp_in[:, split + LANES:]], axis=1)
    return full.reshape(D_MODEL, N_DEV, D_IN // N_DEV).transpose(1, 0, 2)


def kernel(x, positions, w_in, q_norm_g, w_uq, kv_norm_g, w_ukv, sgu_norm_g, sgu_norm_b, w_spatial, b_spatial, w_out, ln_g, ln_b, loss_target, m_w_in, m_q_norm_g, m_w_uq, m_kv_norm_g, m_w_ukv, m_sgu_norm_g, m_sgu_norm_b, m_w_spatial, m_b_spatial, m_w_out, m_ln_g, m_ln_b, v_w_in, v_q_norm_g, v_w_uq, v_kv_norm_g, v_w_ukv, v_sgu_norm_g, v_sgu_norm_b, v_w_spatial, v_b_spatial, v_w_out, v_ln_g, v_ln_b):
    me = 4 * lax.axis_index("x") + 2 * lax.axis_index("y") + lax.axis_index("c")
    seq = x.shape[1]
    x2 = x.reshape(seq, D_MODEL)
    tgt2 = loss_target.reshape(seq, D_MODEL)
    pos_col = positions.reshape(seq, 1)

    w_in_shards, w_out_shards, w_heads = _gather_two_level(
        [w_in.astype(BF16), w_out.astype(BF16), _head_slab(w_uq, w_ukv).astype(BF16)],
        name="wgather")
    (loss_part, grad_x, d_in, d_heads, d_out, d_ws, d_bs_t, d_lng, d_lnb, d_sgug, d_sgub, d_qg, d_kvg) = _local_step(
        x2, tgt2, pos_col, w_in_shards, w_heads, w_out_shards.reshape(D_MODEL, D_MODEL), q_norm_g, kv_norm_g,
        sgu_norm_g, sgu_norm_b, w_spatial, b_spatial, ln_g, ln_b)

    small_part = _pack_small([d_qg, d_kvg, d_sgug, d_sgub, d_bs_t[:, :HEADS].T, d_lng, d_lnb], last=loss_part[0, :1])
    mixed = jnp.concatenate([d_heads, d_ws, small_part.reshape(N_DEV, -1, LANES)], axis=1)
    by_chip = [g.reshape((N_CHIPS, 2) + g.shape[1:])
               for g in (d_in, d_out.reshape(N_DEV, D_MODEL // N_DEV, D_MODEL), mixed)]
    from_sibling = _sibling_swap(by_chip, name="gswap")
    core = lax.axis_index("c").astype(jnp.int32).reshape(1)
    pair_sums = [_pair_sum(a, b, core, name=nm, tile_rows=tr, out_dtype=dt) for a, b, nm, tr, dt in zip(
        by_chip, from_sibling, ("gsum_in", "gsum_out", "gsum_mixed"), (TOKEN_TILE, D_MODEL // N_DEV, MIXED_ROWS),
        (BF16, BF16, F32))]
    recv_in, recv_out, recv_mixed = _chip_exchange(pair_sums, name="gexch")

    take = lambda a: lax.dynamic_index_in_dim(a, me, 0, keepdims=False)
    small_w = _pack_small([q_norm_g, kv_norm_g, sgu_norm_g, sgu_norm_b, b_spatial, ln_g, ln_b])
    small_m = _pack_small([m_q_norm_g, m_kv_norm_g, m_sgu_norm_g, m_sgu_norm_b, m_b_spatial, m_ln_g, m_ln_b])
    small_v = _pack_small([v_q_norm_g, v_kv_norm_g, v_sgu_norm_g, v_sgu_norm_b, v_b_spatial, v_ln_g, v_ln_b])
    own_mixed = lambda uq, ukv, sp, small: jnp.concatenate(
        [_head_slab(uq, ukv), take(sp), take(small.reshape(N_DEV, -1, LANES))])
    res_in = _adam(recv_in, w_in, m_w_in, v_w_in, name="adam_in", tile_rows=TOKEN_TILE)
    res_out = _adam(recv_out, w_out, m_w_out, v_w_out, name="adam_out", tile_rows=D_MODEL // N_DEV)
    res_mixed = _adam(recv_mixed, own_mixed(w_uq, w_ukv, w_spatial, small_w), own_mixed(m_w_uq, m_w_ukv, m_w_spatial, small_m),
                      own_mixed(v_w_uq, v_w_ukv, v_w_spatial, small_v), name="adam_mixed", tile_rows=MIXED_ROWS)

    rep_g, = _exchange([res_mixed[0][HEAD_ROWS:]], name="sgather", per_destination=False)
    rep_pack = lambda sp, small: jnp.concatenate(
        [sp.reshape(N_DEV, CHUNK, LANES), small.reshape(N_DEV, -1, LANES)], axis=1).reshape(-1, LANES)
    _, delta_rep, m_rep, v_rep = _adam(rep_g.reshape(1, N_DEV * REP_ROWS, LANES), rep_pack(w_spatial, small_w),
                                       rep_pack(m_w_spatial, small_m), rep_pack(v_w_spatial, small_v),
                                       name="adam_rep", tile_rows=N_DEV * REP_ROWS)

    def rep_unpack(a):
        a = a.reshape(N_DEV, REP_ROWS, LANES)
        small = _unpack_small(a[:, CHUNK:].reshape(-1))
        return [small[0], small[1], small[2], small[3], a[:, :CHUNK], small[4], small[5], small[6]]

    def ordered(which, rep):
        r_qg, r_kvg, r_sg, r_sb, r_ws, r_bs, r_lg, r_lb = rep_unpack(rep)
        heads = res_mixed[which]
        return [res_in[which], r_qg, heads[:Q_LORA, :UQ_SHARD], r_kvg, heads[Q_LORA:HEAD_ROWS], r_sg, r_sb, r_ws, r_bs,
                res_out[which], r_lg, r_lb]

    loss = rep_g[N_DEV - 1, REP_ROWS - 1, LANES - 1]
    outs = [loss, grad_x.reshape(x.shape)]
    outs += ordered(0, rep_g.reshape(-1, LANES))
    outs += ordered(1, delta_rep)
    outs += ordered(2, m_rep)
    outs += ordered(3, v_rep)
    return tuple(outs)


def _local_step(x2, tgt2, pos_col, w_in_shards, w_heads, w_out_full, q_norm_g, kv_norm_g, sgu_norm_g, sgu_norm_b,
                w_spatial, b_spatial, ln_g, ln_b):
    wp_in = _padded_w_in(w_in_shards)

    half = jnp.arange(HALF, dtype=F32)
    inv_freq = 1.0 / (ROPE_THETA ** (half / HALF))
    invf_row = jnp.concatenate([jnp.zeros((KR_LO,), F32), inv_freq, inv_freq,
                                jnp.zeros((LANES - KR_LO - ROPE,), F32)]).reshape(1, LANES)
    tri = jnp.tril(jnp.ones((CHUNK, CHUNK), dtype=bool))
    ws_low = jnp.where(tri[None], w_spatial, 0.0).astype(BF16)
    ws_low_t = ws_low.transpose(0, 2, 1)
    bsp = jnp.repeat(b_spatial.T, G_HEAD_DIM, axis=1)
    row = lambda a: a.reshape(1, -1)

    proj, q, k, v, vt = _fwd_proj(x2, pos_col, invf_row, wp_in, w_heads, row(q_norm_g), row(kv_norm_g))
    o, lse_row = _attn_fwd(q, k, vt)
    (dr, do, d_row, drest, d_out, d_ws, d_bs_t, d_lng, d_lnb, d_sgug, d_sgub, loss_part) = _mid(
        x2, tgt2, proj, o, w_out_full, ws_low, ws_low_t, bsp, row(sgu_norm_g), row(sgu_norm_b), row(ln_g), row(ln_b))
    dqt, dk, dv = _attn_bwd(q, k, v, do, lse_row, d_row)
    dhead, d_heads, d_qg, d_kvg = _bwd_qkv(dqt, dk, dv, proj, pos_col, invf_row, w_heads, row(q_norm_g), row(kv_norm_g))
    grad_x, dwp_in = _bwd_in(x2, dr, dhead, drest, wp_in)
    return (loss_part, grad_x, _w_in_shards(dwp_in), d_heads, d_out, d_ws, d_bs_t, d_lng, d_lnb, d_sgug, d_sgub,
            d_qg, d_kvg)
```

```python
import functools
import math

import jax
import jax.numpy as jnp
from jax import lax
from jax.experimental import pallas as pl
from jax.experimental.pallas import tpu as pltpu

F32 = jnp.float32
BF16 = jnp.bfloat16

N_DEV = 8
D_MODEL = 1024
HEADS = 8
NOPE = 64
ROPE = 32
HALF = ROPE // 2
VDIM = 64
Q_LORA = 256
KV_LORA = 128
G_WIDTH = 512
G_HEAD_DIM = 64
CHUNK = 128
HEAD_PAD = 128
D_IN = 2464
D_IN_PAD = 2560
KR_LO = NOPE
ROPE_THETA = 10000.0
DN_ALPHA = 2.0 ** 0.25
EPS = 1e-5
ATTN_SCALE = 1.0 / math.sqrt(NOPE + ROPE)
ADAM_LR, ADAM_B1, ADAM_B2, ADAM_EPS, ADAM_WD, ADAM_STEP = 0.001, 0.9, 0.999, 1e-08, 0.01, 10

LANES = 128
REP_ROWS = 136
SMALL_LEN = 8192
VMEM_LIMIT = 56 * 1024 * 1024

TOKEN_TILE = 256
ATTN_WIDE = 1024
ATTN_NARROW = 512
SOFTMAX_ROWS = 256
LOG2E = 1.4426950408889634
LN2 = 0.6931471805599453
Q_PRESCALE = ATTN_SCALE * LOG2E


def _cparams(sem=None):
    return pltpu.CompilerParams(dimension_semantics=sem, vmem_limit_bytes=VMEM_LIMIT)


def _dot(a, b):
    return jnp.dot(a, b, preferred_element_type=F32)


def _dot_nt(a, b):
    return lax.dot_general(a, b, (((1,), (1,)), ((), ())), preferred_element_type=F32)


def _dot_tn(a, b):
    return lax.dot_general(a, b, (((0,), (0,)), ((), ())), preferred_element_type=F32)


def _as_row(col):
    return jnp.transpose(jnp.broadcast_to(col, (col.shape[0], LANES)))[0:1, :]


def _sigmoid(z):
    return 1.0 / (1.0 + jnp.exp(-z))


def _gelu(x):
    return 0.5 * x * (1.0 + lax.erf(x * 0.7071067811865476))


def _gelu_grad(x):
    cdf = 0.5 * (1.0 + lax.erf(x * 0.7071067811865476))
    return cdf + x * jnp.exp(-0.5 * x * x) * 0.3989422804014327


def _exchange(srcs, *, name, per_destination):
    n = len(srcs)
    slab_shapes = [s.shape[1:] if per_destination else s.shape for s in srcs]

    def body(*refs):
        src_refs, out_refs = refs[:n], refs[n:2 * n]
        send_sems, recv_sems, local_sems = refs[2 * n:]
        x, y, c = lax.axis_index("x"), lax.axis_index("y"), lax.axis_index("c")
        me = 4 * x + 2 * y + c

        def slab_for(t, dest):
            return src_refs[t].at[dest] if per_destination else src_refs[t]

        mine = [pltpu.make_async_copy(slab_for(t, me), out_refs[t].at[me], local_sems.at[t]) for t in range(n)]
        for cp in mine:
            cp.start()
        sends, arrivals = [], []
        for k in (6, 7, 4, 5, 2, 3, 1):
            px = 1 - x if k & 4 else x
            py = 1 - y if k & 2 else y
            pc = 1 - c if k & 1 else c
            peer = 4 * px + 2 * py + pc
            for t in range(n):
                sem = (k - 1) * n + t
                cp = pltpu.make_async_remote_copy(
                    src_ref=slab_for(t, peer), dst_ref=out_refs[t].at[me],
                    send_sem=send_sems.at[sem], recv_sem=recv_sems.at[sem],
                    device_id=(px, py, pc), device_id_type=pl.DeviceIdType.MESH)
                cp.start()
                sends.append(cp)
                arrivals.append(pltpu.make_async_remote_copy(
                    src_ref=slab_for(t, peer), dst_ref=out_refs[t].at[peer],
                    send_sem=send_sems.at[sem], recv_sem=recv_sems.at[sem],
                    device_id=(x, y, c), device_id_type=pl.DeviceIdType.MESH))
        for cp in arrivals:
            cp.wait_recv()
        for cp in sends:
            cp.wait_send()
        for cp in mine:
            cp.wait()

    hbm = pl.BlockSpec(memory_space=pl.ANY)
    return pl.pallas_call(
        body, name=name,
        out_shape=[jax.ShapeDtypeStruct((N_DEV,) + tuple(shape), s.dtype) for shape, s in zip(slab_shapes, srcs)],
        in_specs=[hbm] * n, out_specs=[hbm] * n,
        scratch_shapes=[pltpu.SemaphoreType.DMA(((N_DEV - 1) * n,)), pltpu.SemaphoreType.DMA(((N_DEV - 1) * n,)),
                        pltpu.SemaphoreType.DMA((n,))],
    )(*srcs)


def _gather_two_level(srcs, *, name):
    n = len(srcs)

    def body(*refs):
        src_refs, out_refs = refs[:n], refs[n:2 * n]
        send_sems, recv_sems, local_sems = refs[2 * n:]
        x, y, c = lax.axis_index("x"), lax.axis_index("y"), lax.axis_index("c")
        me, sibling = (x, y, c), (x, y, 1 - c)
        chips = [(1 - x, 1 - y), (1 - x, y), (x, 1 - y)]
        index = lambda px, py, pc: 4 * px + 2 * py + pc

        def copy(k, t, block, to, src=None):
            place = out_refs[t].at[index(*block)]
            return pltpu.make_async_remote_copy(
                src_ref=place if src is None else src, dst_ref=place,
                send_sem=send_sems.at[k * n + t], recv_sem=recv_sems.at[k * n + t],
                device_id=to, device_id_type=pl.DeviceIdType.MESH)

        mine = [pltpu.make_async_copy(src_refs[t], out_refs[t].at[index(*me)], local_sems.at[t]) for t in range(n)]
        for cp in mine:
            cp.start()
        first = [copy(1 + j, t, me, (*chip, c), src=src_refs[t]) for j, chip in enumerate(chips) for t in range(n)]
        first += [copy(0, t, me, sibling, src=src_refs[t]) for t in range(n)]
        for cp in first:
            cp.start()
        passed = []
        for j, chip in enumerate(chips):
            for t in range(n):
                copy(1 + j, t, (*chip, c), me).wait_recv()
                cp = copy(4 + j, t, (*chip, c), sibling)
                cp.start()
                passed.append(cp)
        for t in range(n):
            copy(0, t, sibling, me).wait_recv()
        for j, chip in enumerate(chips):
            for t in range(n):
                copy(4 + j, t, (*chip, 1 - c), me).wait_recv()
        for cp in first + passed:
            cp.wait_send()
        for cp in mine:
            cp.wait()

    hbm = pl.BlockSpec(memory_space=pl.ANY)
    return pl.pallas_call(
        body, name=name,
        out_shape=[jax.ShapeDtypeStruct((N_DEV,) + s.shape, s.dtype) for s in srcs],
        in_specs=[hbm] * n, out_specs=[hbm] * n,
        scratch_shapes=[pltpu.SemaphoreType.DMA((7 * n,)), pltpu.SemaphoreType.DMA((7 * n,)),
                        pltpu.SemaphoreType.DMA((n,))],
    )(*srcs)


N_CHIPS = N_DEV // 2


def _sibling_swap(srcs, *, name):
    n = len(srcs)

    def body(*refs):
        src_refs, out_refs = refs[:n], refs[n:2 * n]
        send_sems, recv_sems = refs[2 * n:]
        x, y, c = lax.axis_index("x"), lax.axis_index("y"), lax.axis_index("c")
        sends = []
        for chip in range(N_CHIPS):
            for t in range(n):
                cp = pltpu.make_async_remote_copy(
                    src_ref=src_refs[t].at[chip, 1 - c], dst_ref=out_refs[t].at[chip],
                    send_sem=send_sems.at[chip * n + t], recv_sem=recv_sems.at[chip * n + t],
                    device_id=(x, y, 1 - c), device_id_type=pl.DeviceIdType.MESH)
                cp.start()
                sends.append(cp)
        for cp in sends:
            cp.wait_recv()
        for cp in sends:
            cp.wait_send()

    hbm = pl.BlockSpec(memory_space=pl.ANY)
    return pl.pallas_call(
        body, name=name,
        out_shape=[jax.ShapeDtypeStruct((N_CHIPS,) + s.shape[2:], s.dtype) for s in srcs],
        in_specs=[hbm] * n, out_specs=[hbm] * n,
        scratch_shapes=[pltpu.SemaphoreType.DMA((N_CHIPS * n,)), pltpu.SemaphoreType.DMA((N_CHIPS * n,))],
    )(*srcs)


def _pair_sum(mine, theirs, core, *, name, tile_rows, out_dtype):
    _, _, rows, cols = mine.shape

    def body(core_ref, a_ref, b_ref, o_ref):
        o_ref[...] = (a_ref[0] + b_ref[...]).astype(out_dtype)

    return pl.pallas_call(
        body, name=name,
        grid_spec=pltpu.PrefetchScalarGridSpec(
            num_scalar_prefetch=1, grid=(N_CHIPS, rows // tile_rows),
            in_specs=[pl.BlockSpec((1, 1, tile_rows, cols), lambda q, r, core_ref: (q, core_ref[0], r, 0)),
                      pl.BlockSpec((1, tile_rows, cols), lambda q, r, core_ref: (q, r, 0))],
            out_specs=pl.BlockSpec((1, tile_rows, cols), lambda q, r, core_ref: (q, r, 0))),
        out_shape=jax.ShapeDtypeStruct((N_CHIPS, rows, cols), out_dtype),
        compiler_params=_cparams(("arbitrary", "arbitrary")),
    )(core, mine, theirs)


def _chip_exchange(srcs, *, name):
    n = len(srcs)

    def body(*refs):
        src_refs, out_refs = refs[:n], refs[n:2 * n]
        send_sems, recv_sems, local_sems = refs[2 * n:]
        x, y, c = lax.axis_index("x"), lax.axis_index("y"), lax.axis_index("c")
        my_chip = 2 * x + y
        mine = [pltpu.make_async_copy(src_refs[t].at[my_chip], out_refs[t].at[my_chip], local_sems.at[t])
                for t in range(n)]
        for cp in mine:
            cp.start()
        sends, arrivals = [], []
        for k in (3, 2, 1):
            px = 1 - x if k & 2 else x
            py = 1 - y if k & 1 else y
            peer_chip = 2 * px + py
            for t in range(n):
                sem = (k - 1) * n + t
                cp = pltpu.make_async_remote_copy(
                    src_ref=src_refs[t].at[peer_chip], dst_ref=out_refs[t].at[my_chip],
                    send_sem=send_sems.at[sem], recv_sem=recv_sems.at[sem],
                    device_id=(px, py, c), device_id_type=pl.DeviceIdType.MESH)
                cp.start()
                sends.append(cp)
                arrivals.append(pltpu.make_async_remote_copy(
                    src_ref=src_refs[t].at[peer_chip], dst_ref=out_refs[t].at[peer_chip],
                    send_sem=send_sems.at[sem], recv_sem=recv_sems.at[sem],
                    device_id=(x, y, c), device_id_type=pl.DeviceIdType.MESH))
        for cp in arrivals:
            cp.wait_recv()
        for cp in sends:
            cp.wait_send()
        for cp in mine:
            cp.wait()

    hbm = pl.BlockSpec(memory_space=pl.ANY)
    return pl.pallas_call(
        body, name=name,
        out_shape=[jax.ShapeDtypeStruct(s.shape, s.dtype) for s in srcs],
        in_specs=[hbm] * n, out_specs=[hbm] * n,
        scratch_shapes=[pltpu.SemaphoreType.DMA((3 * n,)), pltpu.SemaphoreType.DMA((3 * n,)),
                        pltpu.SemaphoreType.DMA((n,))],
    )(*srcs)


def _rope_tables(pos_col, invf_row):
    ang = pos_col.astype(F32) * invf_row
    lane = lax.broadcasted_iota(jnp.int32, ang.shape, 1)
    cos, sin = jnp.cos(ang), jnp.sin(ang)
    first = (lane >= KR_LO) & (lane < KR_LO + HALF)
    second = (lane >= KR_LO + HALF) & (lane < KR_LO + ROPE)
    return cos, jnp.where(first, sin, 0.0), jnp.where(second, sin, 0.0)


def _rope(t, cos, sin_first, sin_second, sign):
    up = pltpu.roll(t, LANES - HALF, 1)
    down = pltpu.roll(t, HALF, 1)
    return t * cos - sign * (up * sin_first) + sign * (down * sin_second)


def _fwd_proj(x, pos_col, invf_row, wp_in, w_heads, q_g, kv_g):
    t = x.shape[0]
    tm = TOKEN_TILE

    def body(x_ref, pos_ref, invf_ref, win_ref, wh_ref, qg_ref, kvg_ref,
             proj_ref, q_ref, k_ref, v_ref, vt_ref):
        proj = _dot(x_ref[...].astype(BF16), win_ref[...])
        proj_ref[...] = proj
        c_q = proj[:, :Q_LORA]
        c_kv = proj[:, Q_LORA:Q_LORA + KV_LORA]
        kr_raw = proj[:, Q_LORA + KV_LORA:Q_LORA + KV_LORA + LANES]
        cqn = (c_q * lax.rsqrt(jnp.mean(c_q * c_q, axis=-1, keepdims=True) + EPS) * qg_ref[...]).astype(BF16)
        ckvn = (c_kv * lax.rsqrt(jnp.mean(c_kv * c_kv, axis=-1, keepdims=True) + EPS) * kvg_ref[...]).astype(BF16)
        cos, s1, s2 = _rope_tables(pos_ref[...], invf_ref[...])
        kr = _rope(kr_raw, cos, s1, s2, 1.0)
        lane = lax.broadcasted_iota(jnp.int32, (tm, HEAD_PAD), 1)
        for h in range(HEADS):
            q_h = _dot(cqn, wh_ref[h, :Q_LORA, :])
            kv_h = _dot(ckvn, wh_ref[h, Q_LORA:, :])
            q_ref[h] = (_rope(q_h, cos, s1, s2, 1.0) * Q_PRESCALE).astype(BF16)
            k_ref[h] = jnp.where(lane < NOPE, kv_h, kr).astype(BF16)
            v_ref[h] = kv_h.astype(BF16)
            vt_ref[h] = jnp.transpose(kv_h).astype(BF16)

    full = lambda a: pl.BlockSpec(a.shape, lambda i: (0,) * a.ndim)
    head_spec = pl.BlockSpec((HEADS, tm, HEAD_PAD), lambda i: (0, i, 0))
    head_shape = jax.ShapeDtypeStruct((HEADS, t, HEAD_PAD), BF16)
    return pl.pallas_call(
        body, name="fwd_proj", grid=(t // tm,),
        in_specs=[pl.BlockSpec((tm, D_MODEL), lambda i: (i, 0)), pl.BlockSpec((tm, 1), lambda i: (i, 0)),
                  full(invf_row), full(wp_in), full(w_heads), full(q_g), full(kv_g)],
        out_specs=[pl.BlockSpec((tm, D_IN_PAD), lambda i: (i, 0)), head_spec, head_spec, head_spec,
                   pl.BlockSpec((HEADS, HEAD_PAD, tm), lambda i: (0, 0, i))],
        out_shape=[jax.ShapeDtypeStruct((t, D_IN_PAD), F32), head_shape, head_shape, head_shape,
                   jax.ShapeDtypeStruct((HEADS, HEAD_PAD, t), BF16)],
        compiler_params=_cparams(("arbitrary",)),
    )(x, pos_col, invf_row, wp_in, w_heads, q_g, kv_g)


def _attn_fwd(q, k, vt):
    t = q.shape[1]
    bq, bk = ATTN_WIDE, ATTN_NARROW
    chunk = SOFTMAX_ROWS

    def body(q_ref, k_ref, vt_ref, o_ref, lse_ref, s0, s1, p0, p1, x0, x1, m_scr, l_scr, a_scr, acc_scr):
        i = pl.program_id(1)
        at = lambda j: pl.ds(pl.multiple_of(j * bk, bk), bk)

        def exp_pass(s_in, block_max, p_out, diagonal=False, cols=slice(None)):
            width = bq if cols == slice(None) else cols.stop - cols.start

            def load(r):
                s = s_in[r:r + chunk, cols]
                if diagonal:
                    key = lax.broadcasted_iota(jnp.int32, (chunk, width), 0) + r
                    qry = lax.broadcasted_iota(jnp.int32, (chunk, width), 1)
                    s = jnp.where(qry >= key, s, -jnp.inf)
                return s

            if diagonal:
                block_max = jnp.max(load(0), axis=0, keepdims=True)
                for r in range(chunk, bk, chunk):
                    block_max = jnp.maximum(block_max, jnp.max(load(r), axis=0, keepdims=True))
            m_old = m_scr[:, cols]
            m_new = jnp.maximum(m_old, block_max)
            alpha = jnp.exp2(m_old - m_new)
            total = jnp.zeros((1, width), F32)
            for r in range(0, bk, chunk):
                p = jnp.exp2(load(r) - m_new)
                p_out[r:r + chunk, cols] = p.astype(BF16)
                total = total + jnp.sum(p, axis=0, keepdims=True)
            m_scr[:, cols] = m_new
            l_scr[:, cols] = alpha * l_scr[:, cols] + total
            return alpha

        def scores(j, s_out, x_out):
            s = _dot_nt(k_ref[0, at(j), :], q_ref[0])
            s_out[...] = s
            x_out[...] = jnp.max(s, axis=0, keepdims=True)

        def value_product(j, p_in):
            return _dot(vt_ref[0, :, at(j)], p_in[...])

        def one_pass(j, s_in, x_in, s_out, x_out, p_prev, p_cur):
            scores(j + 1, s_out, x_out)
            acc_scr[...] = a_scr[...] * acc_scr[...] + value_product(jnp.maximum(j - 1, 0), p_prev)
            a_scr[...] = exp_pass(s_in, x_in[...], p_cur)

        scores(0, s0, x0)
        p1[...] = jnp.zeros_like(p1)
        a_scr[...] = jnp.ones_like(a_scr)
        m_scr[...] = jnp.full(m_scr.shape, -jnp.inf, F32)
        l_scr[...] = jnp.zeros_like(l_scr)
        acc_scr[...] = jnp.zeros_like(acc_scr)

        def two_passes(n, _):
            one_pass(2 * n, s0, x0, s1, x1, p1, p0)
            one_pass(2 * n + 1, s1, x1, s0, x0, p0, p1)
            return 0

        lax.fori_loop(0, i, two_passes, 0)
        d = 2 * i
        late = slice(bk, bq)
        s1[:, late] = _dot_nt(k_ref[0, at(d + 1), :], q_ref[0, late, :])
        acc = a_scr[...] * acc_scr[...] + value_product(jnp.maximum(d - 1, 0), p1)
        alpha = exp_pass(s0, None, p0, diagonal=True)
        acc = alpha * acc + value_product(d, p0)
        alpha = exp_pass(s1, None, p1, diagonal=True, cols=late)
        acc_late = alpha * acc[:, late] + _dot(vt_ref[0, :, at(d + 1)], p1[:, late])
        acc = jnp.concatenate([acc[:, :bk], acc_late], axis=1)
        o_ref[0] = jnp.transpose(acc / l_scr[...])
        lse_ref[0] = m_scr[...] + jnp.log2(l_scr[...])

    tile = lambda dtype: pltpu.VMEM((bk, bq), dtype)
    stat = pltpu.VMEM((1, bq), F32)
    return pl.pallas_call(
        body, name="attn_fwd", grid=(HEADS, t // bq),
        in_specs=[pl.BlockSpec((1, bq, HEAD_PAD), lambda h, i: (h, i, 0)),
                  pl.BlockSpec((1, t, HEAD_PAD), lambda h, i: (h, 0, 0)),
                  pl.BlockSpec((1, HEAD_PAD, t), lambda h, i: (h, 0, 0))],
        out_specs=[pl.BlockSpec((1, bq, HEAD_PAD), lambda h, i: (h, i, 0)),
                   pl.BlockSpec((1, 1, bq), lambda h, i: (h, 0, i))],
        out_shape=[jax.ShapeDtypeStruct((HEADS, t, HEAD_PAD), F32), jax.ShapeDtypeStruct((HEADS, 1, t), F32)],
        scratch_shapes=[tile(F32), tile(F32), tile(BF16), tile(BF16), stat, stat, stat, stat, stat,
                        pltpu.VMEM((HEAD_PAD, bq), F32)],
        compiler_params=_cparams(("arbitrary", "arbitrary")),
    )(q, k, vt)


def _mid(x, target, proj, ol, w_out, ws_low, ws_low_t, bsp, sgu_g, sgu_b, ln_g, ln_b):
    t = x.shape[0]
    tm = TOKEN_TILE
    n_steps = t // tm

    def body(x_ref, tgt_ref, za_ref, u_ref, v_ref, zb_ref, ol_ref, wout_ref, ws_ref, wst_ref, bsp_ref,
             sg_ref, sb_ref, lg_ref, lb_ref,
             dr_ref, do_ref, drow_ref, drest_ref, dwout_ref, dws_ref, dbs_ref, dlg_ref, dlb_ref, dsg_ref, dsb_ref,
             loss_ref, dbsp_acc):
        step = pl.program_id(0)

        @pl.when(step == 0)
        def _():
            dwout_ref[...] = jnp.zeros_like(dwout_ref)
            dws_ref[...] = jnp.zeros_like(dws_ref)
            dbs_ref[...] = jnp.zeros_like(dbs_ref)
            dlg_ref[...] = jnp.zeros_like(dlg_ref)
            dlb_ref[...] = jnp.zeros_like(dlb_ref)
            dsg_ref[...] = jnp.zeros_like(dsg_ref)
            dsb_ref[...] = jnp.zeros_like(dsb_ref)
            loss_ref[...] = jnp.zeros_like(loss_ref)
            dbsp_acc[...] = jnp.zeros_like(dbsp_acc)

        lane_head = lax.broadcasted_iota(jnp.int32, (CHUNK, G_WIDTH), 1) // G_HEAD_DIM

        attn = jnp.concatenate([ol_ref[h][:, NOPE:] for h in range(HEADS)], axis=-1)
        za = za_ref[...]
        sig_a = _sigmoid(za)
        silu_a = za * sig_a
        out_a = attn * silu_a
        u = u_ref[...]
        ug = _gelu(u)
        vpre = v_ref[...]
        gv = _gelu(vpre)
        mu_v = jnp.mean(gv, axis=-1, keepdims=True)
        cen_v = gv - mu_v
        rstd_v = lax.rsqrt(jnp.mean(cen_v * cen_v, axis=-1, keepdims=True) + EPS)
        vhat = cen_v * rstd_v
        vg = vhat * sg_ref[...] + sb_ref[...]
        vg_b = vg.astype(BF16)
        sv_parts = []
        for cix in range(tm // CHUNK):
            vc = vg_b[cix * CHUNK:(cix + 1) * CHUNK, :]
            acc = bsp_ref[...]
            for h in range(HEADS):
                acc = acc + jnp.where(lane_head == h, _dot(ws_ref[h], vc), 0.0)
            sv_parts.append(acc)
        sv = jnp.concatenate(sv_parts, axis=0)
        sgu = ug * sv
        zb = zb_ref[...]
        sig_b = _sigmoid(zb)
        silu_b = zb * sig_b
        out_b = sgu * silu_b
        merged = jnp.concatenate([out_a, out_b], axis=-1).astype(BF16)
        r = DN_ALPHA * x_ref[...] + _dot(merged, wout_ref[...])
        mu = jnp.mean(r, axis=-1, keepdims=True)
        cen = r - mu
        rstd = lax.rsqrt(jnp.mean(cen * cen, axis=-1, keepdims=True) + EPS)
        xhat = cen * rstd
        hout = xhat * lg_ref[...] + lb_ref[...]
        err = hout - tgt_ref[...]
        row_loss = jnp.mean(err * err, axis=-1, keepdims=True)
        loss_ref[...] += jnp.broadcast_to(0.5 * jnp.sum(row_loss, axis=0, keepdims=True), loss_ref.shape)

        dh = err * (1.0 / D_MODEL)
        dlg_ref[...] += jnp.sum(dh * xhat, axis=0, keepdims=True)
        dlb_ref[...] += jnp.sum(dh, axis=0, keepdims=True)
        dxhat = dh * lg_ref[...]
        dr = rstd * (dxhat - jnp.mean(dxhat, axis=-1, keepdims=True)
                     - xhat * jnp.mean(dxhat * xhat, axis=-1, keepdims=True))
        dr_ref[...] = dr
        dr_b = dr.astype(BF16)
        dwout_ref[...] += _dot_tn(merged, dr_b)
        dmerged = _dot_nt(dr_b, wout_ref[...])
        d_out_a = dmerged[:, :G_WIDTH]
        d_out_b = dmerged[:, G_WIDTH:]
        dattn = d_out_a * silu_a
        for h in range(HEADS):
            do_h = dattn[:, h * VDIM:(h + 1) * VDIM]
            dsum = jnp.sum(do_h * ol_ref[h][:, NOPE:], axis=-1, keepdims=True)
            drow_ref[h] = _as_row(dsum)
            do_ref[h] = jnp.concatenate([jnp.zeros((tm, NOPE), F32), do_h], axis=-1).astype(BF16)
        dza = d_out_a * attn * (sig_a * (1.0 + za * (1.0 - sig_a)))
        dsgu = d_out_b * silu_b
        dzb = d_out_b * sgu * (sig_b * (1.0 + zb * (1.0 - sig_b)))
        du = dsgu * sv * _gelu_grad(u)
        dsv = dsgu * ug
        dsv_b = dsv.astype(BF16)
        dvg_parts = []
        for cix in range(tm // CHUNK):
            rows = slice(cix * CHUNK, (cix + 1) * CHUNK)
            dsv_c = dsv[rows, :]
            dsv_cb = dsv_b[rows, :]
            vc = vg_b[rows, :]
            dbsp_acc[...] += dsv_c
            acc = jnp.zeros((CHUNK, G_WIDTH), F32)
            for h in range(HEADS):
                on = lane_head == h
                acc = acc + jnp.where(on, _dot(wst_ref[h], dsv_cb), 0.0)
                dws_ref[h] += _dot_nt(jnp.where(on, dsv_cb, jnp.zeros_like(dsv_cb)), vc)
            dvg_parts.append(acc)
        dvg = jnp.concatenate(dvg_parts, axis=0)
        dsg_ref[...] += jnp.sum(dvg * vhat, axis=0, keepdims=True)
        dsb_ref[...] += jnp.sum(dvg, axis=0, keepdims=True)
        dvhat = dvg * sg_ref[...]
        dgv = rstd_v * (dvhat - jnp.mean(dvhat, axis=-1, keepdims=True)
                        - vhat * jnp.mean(dvhat * vhat, axis=-1, keepdims=True))
        dv = dgv * _gelu_grad(vpre)
        drest_ref[...] = jnp.concatenate([dza, du, dv, dzb], axis=-1).astype(BF16)

        @pl.when(step == n_steps - 1)
        def _():
            tri = (lax.broadcasted_iota(jnp.int32, (CHUNK, CHUNK), 0)
                   >= lax.broadcasted_iota(jnp.int32, (CHUNK, CHUNK), 1))
            for h in range(HEADS):
                dws_ref[h] = jnp.where(tri, dws_ref[h], 0.0)
            tot = dbsp_acc[...]
            lane = lax.broadcasted_iota(jnp.int32, (CHUNK, LANES), 1)
            dbs = jnp.zeros((CHUNK, LANES), F32)
            for h in range(HEADS):
                head_sum = jnp.sum(tot[:, h * G_HEAD_DIM:(h + 1) * G_HEAD_DIM], axis=-1, keepdims=True)
                dbs = jnp.where(lane == h, head_sum, dbs)
            dbs_ref[...] = dbs

    full = lambda a: pl.BlockSpec(a.shape, lambda i: (0,) * a.ndim)
    tile = lambda w, j=0: pl.BlockSpec((tm, w), lambda i, j=j: (i, j))
    heads = pl.BlockSpec((HEADS, tm, HEAD_PAD), lambda i: (0, i, 0))
    acc = lambda shape: (pl.BlockSpec(shape, lambda i: (0,) * len(shape)), jax.ShapeDtypeStruct(shape, F32))
    accs = [acc((D_MODEL, D_MODEL)), acc((HEADS, CHUNK, CHUNK)), acc((CHUNK, LANES)), acc((1, D_MODEL)),
            acc((1, D_MODEL)), acc((1, G_WIDTH)), acc((1, G_WIDTH)), acc((1, LANES))]
    return pl.pallas_call(
        body, name="mid", grid=(n_steps,),
        in_specs=[tile(D_MODEL), tile(D_MODEL), tile(G_WIDTH, 1), tile(G_WIDTH, 2), tile(G_WIDTH, 3), tile(G_WIDTH, 4),
                  heads, full(w_out), full(ws_low), full(ws_low_t), full(bsp), full(sgu_g), full(sgu_b),
                  full(ln_g), full(ln_b)],
        out_specs=[tile(D_MODEL), heads, pl.BlockSpec((HEADS, 1, tm), lambda i: (0, 0, i)), tile(4 * G_WIDTH)]
        + [a[0] for a in accs],
        out_shape=[jax.ShapeDtypeStruct((t, D_MODEL), F32), jax.ShapeDtypeStruct((HEADS, t, HEAD_PAD), BF16),
                   jax.ShapeDtypeStruct((HEADS, 1, t), F32), jax.ShapeDtypeStruct((t, 4 * G_WIDTH), BF16)]
        + [a[1] for a in accs],
        scratch_shapes=[pltpu.VMEM((CHUNK, G_WIDTH), F32)],
        compiler_params=_cparams(("arbitrary",)),
    )(x, target, proj, proj, proj, proj, ol, w_out, ws_low, ws_low_t, bsp, sgu_g, sgu_b, ln_g, ln_b)


def _attn_bwd(q, k, v, do, lse_row, d_row):
    t = q.shape[1]
    bk, bq = ATTN_WIDE, ATTN_NARROW
    last = t // bq - 1
    chunk = SOFTMAX_ROWS

    def body(q_ref, k_ref, v_ref, do_ref, lse_ref, drow_ref, dqt_ref, dk_ref, dv_ref,
             s0, s1, e0, e1, p0, p1, g0, g1, kt_scr):
        j = pl.program_id(1)
        at = lambda i: pl.ds(pl.multiple_of(i * bq, bq), bq)

        @pl.when(j == 0)
        def _():
            dqt_ref[...] = jnp.zeros_like(dqt_ref)

        kt_scr[...] = jnp.transpose(k_ref[0].astype(F32)).astype(BF16)
        dk_ref[...] = jnp.zeros_like(dk_ref)
        dv_ref[...] = jnp.zeros_like(dv_ref)

        def products(i, s_out, e_out, keys=slice(0, bk)):
            i = jnp.minimum(i, last)
            s_out[keys, :] = _dot_nt(k_ref[0, keys, :], q_ref[0, at(i), :])
            e_out[keys, :] = _dot_nt(v_ref[0, keys, :], do_ref[0, at(i), :])

        def gradients(i, p_in, g_in, keys=slice(0, bk)):
            dv_ref[0, keys, :] += _dot(p_in[keys, :], do_ref[0, at(i), :])
            dk_ref[0, keys, :] += _dot(g_in[keys, :], q_ref[0, at(i), :])
            dqt_ref[0, :, at(i)] += _dot(kt_scr[:, keys], g_in[keys, :])

        def elementwise(i, s_in, e_in, p_out, g_out, qry0=None, keys=slice(0, bk)):
            lse = lse_ref[0, :, at(i)]
            dsum = drow_ref[0, :, at(i)]
            for r in range(keys.start, keys.stop, chunk):
                p = jnp.exp2(s_in[r:r + chunk, :] - lse)
                if qry0 is not None:
                    key = lax.broadcasted_iota(jnp.int32, (chunk, bq), 0) + r
                    qry = lax.broadcasted_iota(jnp.int32, (chunk, bq), 1) + qry0
                    p = jnp.where(qry >= key, p, 0.0)
                p_out[r:r + chunk, :] = p.astype(BF16)
                g_out[r:r + chunk, :] = (p * (e_in[r:r + chunk, :] - dsum)).astype(BF16)

        def one_pass(i, s_in, e_in, s_out, e_out, p_prev, g_prev, p_cur, g_cur, qry0=None, prev_keys=slice(0, bk)):
            products(i + 1, s_out, e_out)
            gradients(i - 1, p_prev, g_prev, prev_keys)
            elementwise(i, s_in, e_in, p_cur, g_cur, qry0)

        first = 2 * j
        early = slice(0, bq)
        products(first, s0, e0, early)
        products(first + 1, s1, e1)
        elementwise(first, s0, e0, p0, g0, qry0=0, keys=early)
        one_pass(first + 1, s1, e1, s0, e0, p0, g0, p1, g1, qry0=bq, prev_keys=early)

        def two_passes(n, _):
            i = first + 2 + 2 * n
            one_pass(i, s0, e0, s1, e1, p1, g1, p0, g0)
            one_pass(i + 1, s1, e1, s0, e0, p0, g0, p1, g1)
            return 0

        lax.fori_loop(0, (last - first - 1) // 2, two_passes, 0)
        gradients(last, p1, g1)
        dk_ref[0] = dk_ref[0] * LN2

    whole = pl.BlockSpec((1, t, HEAD_PAD), lambda h, j: (h, 0, 0))
    block = pl.BlockSpec((1, bk, HEAD_PAD), lambda h, j: (h, j, 0))
    rows = pl.BlockSpec((1, 1, t), lambda h, j: (h, 0, 0))
    shape = jax.ShapeDtypeStruct((HEADS, t, HEAD_PAD), F32)
    tile = lambda dtype: pltpu.VMEM((bk, bq), dtype)
    return pl.pallas_call(
        body, name="attn_bwd", grid=(HEADS, t // bk),
        in_specs=[whole, block, block, whole, rows, rows],
        out_specs=[pl.BlockSpec((1, HEAD_PAD, t), lambda h, j: (h, 0, 0)), block, block],
        out_shape=[jax.ShapeDtypeStruct((HEADS, HEAD_PAD, t), F32), shape, shape],
        scratch_shapes=[tile(F32), tile(F32), tile(F32), tile(F32), tile(BF16), tile(BF16),
                        tile(BF16), tile(BF16), pltpu.VMEM((HEAD_PAD, bk), BF16)],
        compiler_params=_cparams(("arbitrary", "arbitrary")),
    )(q, k, v, do, lse_row, d_row)


def _bwd_qkv(dq, dk, dv, proj, pos_col, invf_row, w_heads, q_g, kv_g):
    t = proj.shape[0]
    tm = TOKEN_TILE

    def body(dq_ref, dk_ref, dv_ref, ph_ref, pos_ref, invf_ref, wh_ref, qg_ref, kvg_ref,
             dhead_ref, dwh_ref, dqg_ref, dkvg_ref):
        @pl.when(pl.program_id(0) == 0)
        def _():
            dwh_ref[...] = jnp.zeros_like(dwh_ref)
            dqg_ref[...] = jnp.zeros_like(dqg_ref)
            dkvg_ref[...] = jnp.zeros_like(dkvg_ref)

        cos, s1, s2 = _rope_tables(pos_ref[...], invf_ref[...])
        lane = lax.broadcasted_iota(jnp.int32, (tm, LANES), 1)
        c_q = ph_ref[:, :Q_LORA]
        c_kv = ph_ref[:, Q_LORA:Q_LORA + KV_LORA]
        rstd_q = lax.rsqrt(jnp.mean(c_q * c_q, axis=-1, keepdims=True) + EPS)
        rstd_kv = lax.rsqrt(jnp.mean(c_kv * c_kv, axis=-1, keepdims=True) + EPS)
        qhat = c_q * rstd_q
        kvhat = c_kv * rstd_kv
        cqn = (qhat * qg_ref[...]).astype(BF16)
        ckvn = (kvhat * kvg_ref[...]).astype(BF16)
        dcqn = jnp.zeros((tm, Q_LORA), F32)
        dckvn = jnp.zeros((tm, KV_LORA), F32)
        dkr_rot = jnp.zeros((tm, LANES), F32)
        for h in range(HEADS):
            dq_b = _rope(jnp.transpose(dq_ref[h]) * ATTN_SCALE, cos, s1, s2, -1.0).astype(BF16)
            dk_h = dk_ref[h]
            dkv_b = jnp.where(lane < NOPE, dk_h, dv_ref[h]).astype(BF16)
            dkr_rot = dkr_rot + dk_h
            dwh_ref[h, :Q_LORA, :] += _dot_tn(cqn, dq_b)
            dwh_ref[h, Q_LORA:, :] += _dot_tn(ckvn, dkv_b)
            dcqn = dcqn + _dot_nt(dq_b, wh_ref[h, :Q_LORA, :])
            dckvn = dckvn + _dot_nt(dkv_b, wh_ref[h, Q_LORA:, :])
        rot_lanes = (lane >= KR_LO) & (lane < KR_LO + ROPE)
        dkr_raw = jnp.where(rot_lanes, _rope(dkr_rot, cos, s1, s2, -1.0), 0.0)
        dqg_ref[...] += jnp.sum(dcqn * qhat, axis=0, keepdims=True)
        dkvg_ref[...] += jnp.sum(dckvn * kvhat, axis=0, keepdims=True)
        dqh = dcqn * qg_ref[...]
        dkvh = dckvn * kvg_ref[...]
        dc_q = rstd_q * (dqh - qhat * jnp.mean(dqh * qhat, axis=-1, keepdims=True))
        dc_kv = rstd_kv * (dkvh - kvhat * jnp.mean(dkvh * kvhat, axis=-1, keepdims=True))
        dhead_ref[...] = jnp.concatenate([dc_q, dc_kv, dkr_raw], axis=-1).astype(BF16)

    full = lambda a: pl.BlockSpec(a.shape, lambda i: (0,) * a.ndim)
    heads = pl.BlockSpec((HEADS, tm, HEAD_PAD), lambda i: (0, i, 0))
    acc = lambda shape: (pl.BlockSpec(shape, lambda i: (0,) * len(shape)), jax.ShapeDtypeStruct(shape, F32))
    accs = [acc(w_heads.shape), acc((1, Q_LORA)), acc((1, KV_LORA))]
    return pl.pallas_call(
        body, name="bwd_qkv", grid=(t // tm,),
        in_specs=[pl.BlockSpec((HEADS, HEAD_PAD, tm), lambda i: (0, 0, i)), heads, heads,
                  pl.BlockSpec((tm, 4 * LANES), lambda i: (i, 0)),
                  pl.BlockSpec((tm, 1), lambda i: (i, 0)), full(invf_row), full(w_heads),
                  full(q_g), full(kv_g)],
        out_specs=[pl.BlockSpec((tm, 4 * LANES), lambda i: (i, 0))] + [a[0] for a in accs],
        out_shape=[jax.ShapeDtypeStruct((t, 4 * LANES), BF16)] + [a[1] for a in accs],
        compiler_params=_cparams(("arbitrary",)),
    )(dq, dk, dv, proj, pos_col, invf_row, w_heads, q_g, kv_g)


def _bwd_in(x, dr, dhead, drest, wp_in):
    t = x.shape[0]
    tm = TOKEN_TILE
    n_head = dhead.shape[1]

    def body(x_ref, dr_ref, dhead_ref, drest_ref, win_ref, gx_ref, dwin_ref):
        @pl.when(pl.program_id(0) == 0)
        def _():
            dwin_ref[...] = jnp.zeros_like(dwin_ref)

        xb = x_ref[...].astype(BF16)
        dh_b = dhead_ref[...]
        dr_b = drest_ref[...]
        gx_ref[...] = (DN_ALPHA * dr_ref[...] + _dot_nt(dh_b, win_ref[:, :n_head])
                       + _dot_nt(dr_b, win_ref[:, n_head:]))
        dwin_ref[:, :n_head] += _dot_tn(xb, dh_b)
        dwin_ref[:, n_head:] += _dot_tn(xb, dr_b)

    tile = lambda w: pl.BlockSpec((tm, w), lambda i: (i, 0))
    whole = pl.BlockSpec(wp_in.shape, lambda i: (0, 0))
    return pl.pallas_call(
        body, name="bwd_in", grid=(t // tm,),
        in_specs=[tile(D_MODEL), tile(D_MODEL), tile(n_head), tile(drest.shape[1]), whole],
        out_specs=[tile(D_MODEL), whole],
        out_shape=[jax.ShapeDtypeStruct((t, D_MODEL), F32), jax.ShapeDtypeStruct(wp_in.shape, F32)],
        compiler_params=_cparams(("arbitrary",)),
    )(x, dr, dhead, drest, wp_in)


def _adam(parts, w, m, v, *, name, tile_rows):
    n, rows, cols = parts.shape

    def body(p_ref, w_ref, m_ref, v_ref, g_ref, d_ref, nm_ref, nv_ref):
        g = p_ref[0].astype(F32)
        for s in range(1, n):
            g = g + p_ref[s].astype(F32)
        m_new = ADAM_B1 * m_ref[...] + (1.0 - ADAM_B1) * g
        v_new = ADAM_B2 * v_ref[...] + (1.0 - ADAM_B2) * (g * g)
        m_hat = m_new / (1.0 - ADAM_B1 ** ADAM_STEP)
        v_hat = v_new / (1.0 - ADAM_B2 ** ADAM_STEP)
        g_ref[...] = g
        d_ref[...] = -ADAM_LR * (m_hat / (jnp.sqrt(v_hat) + ADAM_EPS) + ADAM_WD * w_ref[...])
        nm_ref[...] = m_new
        nv_ref[...] = v_new

    flat = pl.BlockSpec((tile_rows, cols), lambda i: (i, 0))
    shape = jax.ShapeDtypeStruct((rows, cols), F32)
    return pl.pallas_call(
        body, name=name, grid=(rows // tile_rows,),
        in_specs=[pl.BlockSpec((n, tile_rows, cols), lambda i: (0, i, 0)), flat, flat, flat],
        out_specs=[flat] * 4, out_shape=[shape] * 4,
        compiler_params=_cparams(("arbitrary",)),
    )(parts, w, m, v)


SMALL_NAMES = ("q_norm_g", "kv_norm_g", "sgu_norm_g", "sgu_norm_b", "b_spatial", "ln_g", "ln_b")
SMALL_SIZES = (Q_LORA, KV_LORA, G_WIDTH, G_WIDTH, HEADS * CHUNK, D_MODEL, D_MODEL)


def _pack_small(vals, last=None):
    flat = jnp.concatenate([v.reshape(-1) for v in vals])
    pad = SMALL_LEN - flat.shape[0]
    if last is None:
        return jnp.pad(flat, (0, pad))
    return jnp.concatenate([flat, jnp.zeros((pad - 1,), F32), last.reshape(1)])


def _unpack_small(flat):
    out, at = [], 0
    for n in SMALL_SIZES:
        out.append(flat[at:at + n])
        at += n
    out[4] = out[4].reshape(HEADS, CHUNK)
    return out


UQ_SHARD = HEADS * (NOPE + ROPE) // N_DEV
HEAD_ROWS = Q_LORA + KV_LORA
MIXED_ROWS = HEAD_ROWS + CHUNK + SMALL_LEN // N_DEV // LANES


def _head_slab(w_uq_shard, w_ukv_shard):
    return jnp.concatenate([jnp.pad(w_uq_shard, ((0, 0), (0, LANES - UQ_SHARD))), w_ukv_shard])


def _padded_w_in(shards):
    full = shards.transpose(1, 0, 2).reshape(D_MODEL, D_IN)
    z = lambda c: jnp.zeros((D_MODEL, c), shards.dtype)
    split = Q_LORA + KV_LORA
    return jnp.concatenate([full[:, :split], z(KR_LO), full[:, split:split + ROPE], z(LANES - KR_LO - ROPE),
                            full[:, split + ROPE:]], axis=1)


def _w_in_shards(dwp_in):
    split = Q_LORA + KV_LORA
    full = jnp.concatenate([dwp_in[:, :split], dwp_in[:, split + KR_LO:split + KR_LO + ROPE],
                            dwp_in[:, split + LANES:]], axis=1)
    return full.reshape(D_MODEL, N_DEV, D_IN // N_DEV).transpose(1, 0, 2)


def kernel(x, positions, w_in, q_norm_g, w_uq, kv_norm_g, w_ukv, sgu_norm_g, sgu_norm_b, w_spatial, b_spatial, w_out, ln_g, ln_b, loss_target, m_w_in, m_q_norm_g, m_w_uq, m_kv_norm_g, m_w_ukv, m_sgu_norm_g, m_sgu_norm_b, m_w_spatial, m_b_spatial, m_w_out, m_ln_g, m_ln_b, v_w_in, v_q_norm_g, v_w_uq, v_kv_norm_g, v_w_ukv, v_sgu_norm_g, v_sgu_norm_b, v_w_spatial, v_b_spatial, v_w_out, v_ln_g, v_ln_b):
    me = 4 * lax.axis_index("x") + 2 * lax.axis_index("y") + lax.axis_index("c")
    seq = x.shape[1]
    x2 = x.reshape(seq, D_MODEL)
    tgt2 = loss_target.reshape(seq, D_MODEL)
    pos_col = positions.reshape(seq, 1)

    w_in_shards, w_out_shards, w_heads = _gather_two_level(
        [w_in.astype(BF16), w_out.astype(BF16), _head_slab(w_uq, w_ukv).astype(BF16)],
        name="wgather")
    (loss_part, grad_x, d_in, d_heads, d_out, d_ws, d_bs_t, d_lng, d_lnb, d_sgug, d_sgub, d_qg, d_kvg) = _local_step(
        x2, tgt2, pos_col, w_in_shards, w_heads, w_out_shards.reshape(D_MODEL, D_MODEL), q_norm_g, kv_norm_g,
        sgu_norm_g, sgu_norm_b, w_spatial, b_spatial, ln_g, ln_b)

    small_part = _pack_small([d_qg, d_kvg, d_sgug, d_sgub, d_bs_t[:, :HEADS].T, d_lng, d_lnb], last=loss_part[0, :1])
    mixed = jnp.concatenate([d_heads, d_ws, small_part.reshape(N_DEV, -1, LANES)], axis=1)
    by_chip = [g.reshape((N_CHIPS, 2) + g.shape[1:])
               for g in (d_in, d_out.reshape(N_DEV, D_MODEL // N_DEV, D_MODEL), mixed)]
    from_sibling = _sibling_swap(by_chip, name="gswap")
    core = lax.axis_index("c").astype(jnp.int32).reshape(1)
    pair_sums = [_pair_sum(a, b, core, name=nm, tile_rows=tr, out_dtype=dt) for a, b, nm, tr, dt in zip(
        by_chip, from_sibling, ("gsum_in", "gsum_out", "gsum_mixed"), (TOKEN_TILE, D_MODEL // N_DEV, MIXED_ROWS),
        (BF16, BF16, F32))]
    recv_in, recv_out, recv_mixed = _chip_exchange(pair_sums, name="gexch")

    take = lambda a: lax.dynamic_index_in_dim(a, me, 0, keepdims=False)
    small_w = _pack_small([q_norm_g, kv_norm_g, sgu_norm_g, sgu_norm_b, b_spatial, ln_g, ln_b])
    small_m = _pack_small([m_q_norm_g, m_kv_norm_g, m_sgu_norm_g, m_sgu_norm_b, m_b_spatial, m_ln_g, m_ln_b])
    small_v = _pack_small([v_q_norm_g, v_kv_norm_g, v_sgu_norm_g, v_sgu_norm_b, v_b_spatial, v_ln_g, v_ln_b])
    own_mixed = lambda uq, ukv, sp, small: jnp.concatenate(
        [_head_slab(uq, ukv), take(sp), take(small.reshape(N_DEV, -1, LANES))])
    res_in = _adam(recv_in, w_in, m_w_in, v_w_in, name="adam_in", tile_rows=TOKEN_TILE)
    res_out = _adam(recv_out, w_out, m_w_out, v_w_out, name="adam_out", tile_rows=D_MODEL // N_DEV)
    res_mixed = _adam(recv_mixed, own_mixed(w_uq, w_ukv, w_spatial, small_w), own_mixed(m_w_uq, m_w_ukv, m_w_spatial, small_m),
                      own_mixed(v_w_uq, v_w_ukv, v_w_spatial, small_v), name="adam_mixed", tile_rows=MIXED_ROWS)

    rep_g, = _exchange([res_mixed[0][HEAD_ROWS:]], name="sgather", per_destination=False)
    rep_pack = lambda sp, small: jnp.concatenate(
        [sp.reshape(N_DEV, CHUNK, LANES), small.reshape(N_DEV, -1, LANES)], axis=1).reshape(-1, LANES)
    _, delta_rep, m_rep, v_rep = _adam(rep_g.reshape(1, N_DEV * REP_ROWS, LANES), rep_pack(w_spatial, small_w),
                                       rep_pack(m_w_spatial, small_m), rep_pack(v_w_spatial, small_v),
                                       name="adam_rep", tile_rows=N_DEV * REP_ROWS)

    def rep_unpack(a):
        a = a.reshape(N_DEV, REP_ROWS, LANES)
        small = _unpack_small(a[:, CHUNK:].reshape(-1))
        return [small[0], small[1], small[2], small[3], a[:, :CHUNK], small[4], small[5], small[6]]

    def ordered(which, rep):
        r_qg, r_kvg, r_sg, r_sb, r_ws, r_bs, r_lg, r_lb = rep_unpack(rep)
        heads = res_mixed[which]
        return [res_in[which], r_qg, heads[:Q_LORA, :UQ_SHARD], r_kvg, heads[Q_LORA:HEAD_ROWS], r_sg, r_sb, r_ws, r_bs,
                res_out[which], r_lg, r_lb]

    loss = rep_g[N_DEV - 1, REP_ROWS - 1, LANES - 1]
    outs = [loss, grad_x.reshape(x.shape)]
    outs += ordered(0, rep_g.reshape(-1, LANES))
    outs += ordered(1, delta_rep)
    outs += ordered(2, m_rep)
    outs += ordered(3, v_rep)
    return tuple(outs)


def _local_step(x2, tgt2, pos_col, w_in_shards, w_heads, w_out_full, q_norm_g, kv_norm_g, sgu_norm_g, sgu_norm_b,
                w_spatial, b_spatial, ln_g, ln_b):
    wp_in = _padded_w_in(w_in_shards)

    half = jnp.arange(HALF, dtype=F32)
    inv_freq = 1.0 / (ROPE_THETA ** (half / HALF))
    invf_row = jnp.concatenate([jnp.zeros((KR_LO,), F32), inv_freq, inv_freq,
                                jnp.zeros((LANES - KR_LO - ROPE,), F32)]).reshape(1, LANES)
    tri = jnp.tril(jnp.ones((CHUNK, CHUNK), dtype=bool))
    ws_low = jnp.where(tri[None], w_spatial, 0.0).astype(BF16)
    ws_low_t = ws_low.transpose(0, 2, 1)
    bsp = jnp.repeat(b_spatial.T, G_HEAD_DIM, axis=1)
    row = lambda a: a.reshape(1, -1)

    proj, q, k, v, vt = _fwd_proj(x2, pos_col, invf_row, wp_in, w_heads, row(q_norm_g), row(kv_norm_g))
    o, lse_row = _attn_fwd(q, k, vt)
    (dr, do, d_row, drest, d_out, d_ws, d_bs_t, d_lng, d_lnb, d_sgug, d_sgub, loss_part) = _mid(
        x2, tgt2, proj, o, w_out_full, ws_low, ws_low_t, bsp, row(sgu_norm_g), row(sgu_norm_b), row(ln_g), row(ln_b))
    dqt, dk, dv = _attn_bwd(q, k, v, do, lse_row, d_row)
    dhead, d_heads, d_qg, d_kvg = _bwd_qkv(dqt, dk, dv, proj, pos_col, invf_row, w_heads, row(q_norm_g), row(kv_norm_g))
    grad_x, dwp_in = _bwd_in(x2, dr, dhead, drest, wp_in)
    return (loss_part, grad_x, _w_in_shards(dwp_in), d_heads, d_out, d_ws, d_bs_t, d_lng, d_lnb, d_sgug, d_sgub,
            d_qg, d_kvg)
```

```python
import functools
import math

import jax
import jax.numpy as jnp
from jax import lax
from jax.experimental import pallas as pl
from jax.experimental.pallas import tpu as pltpu

F32 = jnp.float32
BF16 = jnp.bfloat16

N_DEV = 8
D_MODEL = 1024
HEADS = 8
NOPE = 64
ROPE = 32
HALF = ROPE // 2
VDIM = 64
Q_LORA = 256
KV_LORA = 128
G_WIDTH = 512
G_HEAD_DIM = 64
CHUNK = 128
HEAD_PAD = 128
D_IN = 2464
D_IN_PAD = 2560
KR_LO = NOPE
ROPE_THETA = 10000.0
DN_ALPHA = 2.0 ** 0.25
EPS = 1e-5
ATTN_SCALE = 1.0 / math.sqrt(NOPE + ROPE)
ADAM_LR, ADAM_B1, ADAM_B2, ADAM_EPS, ADAM_WD, ADAM_STEP = 0.001, 0.9, 0.999, 1e-08, 0.01, 10

LANES = 128
REP_ROWS = 136
SMALL_LEN = 8192
VMEM_LIMIT = 56 * 1024 * 1024

TOKEN_TILE = 256
ATTN_WIDE = 1024
ATTN_NARROW = 512
SOFTMAX_ROWS = 256
LOG2E = 1.4426950408889634
LN2 = 0.6931471805599453
Q_PRESCALE = ATTN_SCALE * LOG2E


def _cparams(sem=None):
    return pltpu.CompilerParams(dimension_semantics=sem, vmem_limit_bytes=VMEM_LIMIT)


def _dot(a, b):
    return jnp.dot(a, b, preferred_element_type=F32)


def _dot_nt(a, b):
    return lax.dot_general(a, b, (((1,), (1,)), ((), ())), preferred_element_type=F32)


def _dot_tn(a, b):
    return lax.dot_general(a, b, (((0,), (0,)), ((), ())), preferred_element_type=F32)


def _as_row(col):
    return jnp.transpose(jnp.broadcast_to(col, (col.shape[0], LANES)))[0:1, :]


def _sigmoid(z):
    return 1.0 / (1.0 + jnp.exp(-z))


def _gelu(x):
    return 0.5 * x * (1.0 + lax.erf(x * 0.7071067811865476))


def _gelu_grad(x):
    cdf = 0.5 * (1.0 + lax.erf(x * 0.7071067811865476))
    return cdf + x * jnp.exp(-0.5 * x * x) * 0.3989422804014327


def _exchange(srcs, *, name, per_destination):
    n = len(srcs)
    slab_shapes = [s.shape[1:] if per_destination else s.shape for s in srcs]

    def body(*refs):
        src_refs, out_refs = refs[:n], refs[n:2 * n]
        send_sems, recv_sems, local_sems = refs[2 * n:]
        x, y, c = lax.axis_index("x"), lax.axis_index("y"), lax.axis_index("c")
        me = 4 * x + 2 * y + c

        def slab_for(t, dest):
            return src_refs[t].at[dest] if per_destination else src_refs[t]

        mine = [pltpu.make_async_copy(slab_for(t, me), out_refs[t].at[me], local_sems.at[t]) for t in range(n)]
        for cp in mine:
            cp.start()
        sends, arrivals = [], []
        for k in (6, 7, 4, 5, 2, 3, 1):
            px = 1 - x if k & 4 else x
            py = 1 - y if k & 2 else y
            pc = 1 - c if k & 1 else c
            peer = 4 * px + 2 * py + pc
            for t in range(n):
                sem = (k - 1) * n + t
                cp = pltpu.make_async_remote_copy(
                    src_ref=slab_for(t, peer), dst_ref=out_refs[t].at[me],
                    send_sem=send_sems.at[sem], recv_sem=recv_sems.at[sem],
                    device_id=(px, py, pc), device_id_type=pl.DeviceIdType.MESH)
                cp.start()
                sends.append(cp)
                arrivals.append(pltpu.make_async_remote_copy(
                    src_ref=slab_for(t, peer), dst_ref=out_refs[t].at[peer],
                    send_sem=send_sems.at[sem], recv_sem=recv_sems.at[sem],
                    device_id=(x, y, c), device_id_type=pl.DeviceIdType.MESH))
        for cp in arrivals:
            cp.wait_recv()
        for cp in sends:
            cp.wait_send()
        for cp in mine:
            cp.wait()

    hbm = pl.BlockSpec(memory_space=pl.ANY)
    return pl.pallas_call(
        body, name=name,
        out_shape=[jax.ShapeDtypeStruct((N_DEV,) + tuple(shape), s.dtype) for shape, s in zip(slab_shapes, srcs)],
        in_specs=[hbm] * n, out_specs=[hbm] * n,
        scratch_shapes=[pltpu.SemaphoreType.DMA(((N_DEV - 1) * n,)), pltpu.SemaphoreType.DMA(((N_DEV - 1) * n,)),
                        pltpu.SemaphoreType.DMA((n,))],
    )(*srcs)


def _gather_two_level(srcs, *, name):
    n = len(srcs)

    def body(*refs):
        src_refs, out_refs = refs[:n], refs[n:2 * n]
        send_sems, recv_sems, local_sems = refs[2 * n:]
        x, y, c = lax.axis_index("x"), lax.axis_index("y"), lax.axis_index("c")
        me, sibling = (x, y, c), (x, y, 1 - c)
        chips = [(1 - x, 1 - y), (1 - x, y), (x, 1 - y)]
        index = lambda px, py, pc: 4 * px + 2 * py + pc

        def copy(k, t, block, to, src=None):
            place = out_refs[t].at[index(*block)]
            return pltpu.make_async_remote_copy(
                src_ref=place if src is None else src, dst_ref=place,
                send_sem=send_sems.at[k * n + t], recv_sem=recv_sems.at[k * n + t],
                device_id=to, device_id_type=pl.DeviceIdType.MESH)

        mine = [pltpu.make_async_copy(src_refs[t], out_refs[t].at[index(*me)], local_sems.at[t]) for t in range(n)]
        for cp in mine:
            cp.start()
        first = [copy(1 + j, t, me, (*chip, c), src=src_refs[t]) for j, chip in enumerate(chips) for t in range(n)]
        first += [copy(0, t, me, sibling, src=src_refs[t]) for t in range(n)]
        for cp in first:
            cp.start()
        passed = []
        for j, chip in enumerate(chips):
            for t in range(n):
                copy(1 + j, t, (*chip, c), me).wait_recv()
                cp = copy(4 + j, t, (*chip, c), sibling)
                cp.start()
                passed.append(cp)
        for t in range(n):
            copy(0, t, sibling, me).wait_recv()
        for j, chip in enumerate(chips):
            for t in range(n):
                copy(4 + j, t, (*chip, 1 - c), me).wait_recv()
        for cp in first + passed:
            cp.wait_send()
        for cp in mine:
            cp.wait()

    hbm = pl.BlockSpec(memory_space=pl.ANY)
    return pl.pallas_call(
        body, name=name,
        out_shape=[jax.ShapeDtypeStruct((N_DEV,) + s.shape, s.dtype) for s in srcs],
        in_specs=[hbm] * n, out_specs=[hbm] * n,
        scratch_shapes=[pltpu.SemaphoreType.DMA((7 * n,)), pltpu.SemaphoreType.DMA((7 * n,)),
                        pltpu.SemaphoreType.DMA((n,))],
    )(*srcs)


N_CHIPS = N_DEV // 2


def _sibling_swap(srcs, *, name):
    n = len(srcs)

    def body(*refs):
        src_refs, out_refs = refs[:n], refs[n:2 * n]
        send_sems, recv_sems = refs[2 * n:]
        x, y, c = lax.axis_index("x"), lax.axis_index("y"), lax.axis_index("c")
        sends = []
        for chip in range(N_CHIPS):
            for t in range(n):
                cp = pltpu.make_async_remote_copy(
                    src_ref=src_refs[t].at[chip, 1 - c], dst_ref=out_refs[t].at[chip],
                    send_sem=send_sems.at[chip * n + t], recv_sem=recv_sems.at[chip * n + t],
                    device_id=(x, y, 1 - c), device_id_type=pl.DeviceIdType.MESH)
                cp.start()
                sends.append(cp)
        for cp in sends:
            cp.wait_recv()
        for cp in sends:
            cp.wait_send()

    hbm = pl.BlockSpec(memory_space=pl.ANY)
    return pl.pallas_call(
        body, name=name,
        out_shape=[jax.ShapeDtypeStruct((N_CHIPS,) + s.shape[2:], s.dtype) for s in srcs],
        in_specs=[hbm] * n, out_specs=[hbm] * n,
        scratch_shapes=[pltpu.SemaphoreType.DMA((N_CHIPS * n,)), pltpu.SemaphoreType.DMA((N_CHIPS * n,))],
    )(*srcs)


def _pair_sum(mine, theirs, core, *, name, tile_rows, out_dtype):
    _, _, rows, cols = mine.shape

    def body(core_ref, a_ref, b_ref, o_ref):
        o_ref[...] = (a_ref[0] + b_ref[...]).astype(out_dtype)

    return pl.pallas_call(
        body, name=name,
        grid_spec=pltpu.PrefetchScalarGridSpec(
            num_scalar_prefetch=1, grid=(N_CHIPS, rows // tile_rows),
            in_specs=[pl.BlockSpec((1, 1, tile_rows, cols), lambda q, r, core_ref: (q, core_ref[0], r, 0)),
                      pl.BlockSpec((1, tile_rows, cols), lambda q, r, core_ref: (q, r, 0))],
            out_specs=pl.BlockSpec((1, tile_rows, cols), lambda q, r, core_ref: (q, r, 0))),
        out_shape=jax.ShapeDtypeStruct((N_CHIPS, rows, cols), out_dtype),
        compiler_params=_cparams(("arbitrary", "arbitrary")),
    )(core, mine, theirs)


def _chip_exchange(srcs, *, name):
    n = len(srcs)

    def body(*refs):
        src_refs, out_refs = refs[:n], refs[n:2 * n]
        send_sems, recv_sems, local_sems = refs[2 * n:]
        x, y, c = lax.axis_index("x"), lax.axis_index("y"), lax.axis_index("c")
        my_chip = 2 * x + y
        mine = [pltpu.make_async_copy(src_refs[t].at[my_chip], out_refs[t].at[my_chip], local_sems.at[t])
                for t in range(n)]
        for cp in mine:
            cp.start()
        sends, arrivals = [], []
        for k in (3, 2, 1):
            px = 1 - x if k & 2 else x
            py = 1 - y if k & 1 else y
            peer_chip = 2 * px + py
            for t in range(n):
                sem = (k - 1) * n + t
                cp = pltpu.make_async_remote_copy(
                    src_ref=src_refs[t].at[peer_chip], dst_ref=out_refs[t].at[my_chip],
                    send_sem=send_sems.at[sem], recv_sem=recv_sems.at[sem],
                    device_id=(px, py, c), device_id_type=pl.DeviceIdType.MESH)
                cp.start()
                sends.append(cp)
                arrivals.append(pltpu.make_async_remote_copy(
                    src_ref=src_refs[t].at[peer_chip], dst_ref=out_refs[t].at[peer_chip],
                    send_sem=send_sems.at[sem], recv_sem=recv_sems.at[sem],
                    device_id=(x, y, c), device_id_type=pl.DeviceIdType.MESH))
        for cp in arrivals:
            cp.wait_recv()
        for cp in sends:
            cp.wait_send()
        for cp in mine:
            cp.wait()

    hbm = pl.BlockSpec(memory_space=pl.ANY)
    return pl.pallas_call(
        body, name=name,
        out_shape=[jax.ShapeDtypeStruct(s.shape, s.dtype) for s in srcs],
        in_specs=[hbm] * n, out_specs=[hbm] * n,
        scratch_shapes=[pltpu.SemaphoreType.DMA((3 * n,)), pltpu.SemaphoreType.DMA((3 * n,)),
                        pltpu.SemaphoreType.DMA((n,))],
    )(*srcs)


def _rope_tables(pos_col, invf_row):
    ang = pos_col.astype(F32) * invf_row
    lane = lax.broadcasted_iota(jnp.int32, ang.shape, 1)
    cos, sin = jnp.cos(ang), jnp.sin(ang)
    first = (lane >= KR_LO) & (lane < KR_LO + HALF)
    second = (lane >= KR_LO + HALF) & (lane < KR_LO + ROPE)
    return cos, jnp.where(first, sin, 0.0), jnp.where(second, sin, 0.0)


def _rope(t, cos, sin_first, sin_second, sign):
    up = pltpu.roll(t, LANES - HALF, 1)
    down = pltpu.roll(t, HALF, 1)
    return t * cos - sign * (up * sin_first) + sign * (down * sin_second)


def _fwd_proj(x, pos_col, invf_row, wp_in, w_heads, q_g, kv_g):
    t = x.shape[0]
    tm = TOKEN_TILE

    def body(x_ref, pos_ref, invf_ref, win_ref, wh_ref, qg_ref, kvg_ref,
             proj_ref, q_ref, k_ref, v_ref, vt_ref):
        proj = _dot(x_ref[...].astype(BF16), win_ref[...])
        proj_ref[...] = proj
        c_q = proj[:, :Q_LORA]
        c_kv = proj[:, Q_LORA:Q_LORA + KV_LORA]
        kr_raw = proj[:, Q_LORA + KV_LORA:Q_LORA + KV_LORA + LANES]
        cqn = (c_q * lax.rsqrt(jnp.mean(c_q * c_q, axis=-1, keepdims=True) + EPS) * qg_ref[...]).astype(BF16)
        ckvn = (c_kv * lax.rsqrt(jnp.mean(c_kv * c_kv, axis=-1, keepdims=True) + EPS) * kvg_ref[...]).astype(BF16)
        cos, s1, s2 = _rope_tables(pos_ref[...], invf_ref[...])
        kr = _rope(kr_raw, cos, s1, s2, 1.0)
        lane = lax.broadcasted_iota(jnp.int32, (tm, HEAD_PAD), 1)
        for h in range(HEADS):
            q_h = _dot(cqn, wh_ref[h, :Q_LORA, :])
            kv_h = _dot(ckvn, wh_ref[h, Q_LORA:, :])
            q_ref[h] = (_rope(q_h, cos, s1, s2, 1.0) * Q_PRESCALE).astype(BF16)
            k_ref[h] = jnp.where(lane < NOPE, kv_h, kr).astype(BF16)
            v_ref[h] = kv_h.astype(BF16)
            vt_ref[h] = jnp.transpose(kv_h).astype(BF16)

    full = lambda a: pl.BlockSpec(a.shape, lambda i: (0,) * a.ndim)
    head_spec = pl.BlockSpec((HEADS, tm, HEAD_PAD), lambda i: (0, i, 0))
    head_shape = jax.ShapeDtypeStruct((HEADS, t, HEAD_PAD), BF16)
    return pl.pallas_call(
        body, name="fwd_proj", grid=(t // tm,),
        in_specs=[pl.BlockSpec((tm, D_MODEL), lambda i: (i, 0)), pl.BlockSpec((tm, 1), lambda i: (i, 0)),
                  full(invf_row), full(wp_in), full(w_heads), full(q_g), full(kv_g)],
        out_specs=[pl.BlockSpec((tm, D_IN_PAD), lambda i: (i, 0)), head_spec, head_spec, head_spec,
                   pl.BlockSpec((HEADS, HEAD_PAD, tm), lambda i: (0, 0, i))],
        out_shape=[jax.ShapeDtypeStruct((t, D_IN_PAD), F32), head_shape, head_shape, head_shape,
                   jax.ShapeDtypeStruct((HEADS, HEAD_PAD, t), BF16)],
        compiler_params=_cparams(("arbitrary",)),
    )(x, pos_col, invf_row, wp_in, w_heads, q_g, kv_g)


def _attn_fwd(q, k, vt):
    t = q.shape[1]
    bq, bk = ATTN_WIDE, ATTN_NARROW
    chunk = SOFTMAX_ROWS

    def body(q_ref, k_ref, vt_ref, o_ref, lse_ref, s0, s1, p0, p1, x0, x1, m_scr, l_scr, a_scr, acc_scr):
        i = pl.program_id(1)
        at = lambda j: pl.ds(pl.multiple_of(j * bk, bk), bk)

        def exp_pass(s_in, block_max, p_out, diagonal=False, cols=slice(None)):
            width = bq if cols == slice(None) else cols.stop - cols.start

            def load(r):
                s = s_in[r:r + chunk, cols]
                if diagonal:
                    key = lax.broadcasted_iota(jnp.int32, (chunk, width), 0) + r
                    qry = lax.broadcasted_iota(jnp.int32, (chunk, width), 1)
                    s = jnp.where(qry >= key, s, -jnp.inf)
                return s

            if diagonal:
                block_max = jnp.max(load(0), axis=0, keepdims=True)
                for r in range(chunk, bk, chunk):
                    block_max = jnp.maximum(block_max, jnp.max(load(r), axis=0, keepdims=True))
            m_old = m_scr[:, cols]
            m_new = jnp.maximum(m_old, block_max)
            alpha = jnp.exp2(m_old - m_new)
            total = jnp.zeros((1, width), F32)
            for r in range(0, bk, chunk):
                p = jnp.exp2(load(r) - m_new)
                p_out[r:r + chunk, cols] = p.astype(BF16)
                total = total + jnp.sum(p, axis=0, keepdims=True)
            m_scr[:, cols] = m_new
            l_scr[:, cols] = alpha * l_scr[:, cols] + total
            return alpha

        def scores(j, s_out, x_out):
            s = _dot_nt(k_ref[0, at(j), :], q_ref[0])
            s_out[...] = s
            x_out[...] = jnp.max(s, axis=0, keepdims=True)

        def value_product(j, p_in):
            return _dot(vt_ref[0, :, at(j)], p_in[...])

        def one_pass(j, s_in, x_in, s_out, x_out, p_prev, p_cur):
            scores(j + 1, s_out, x_out)
            acc_scr[...] = a_scr[...] * acc_scr[...] + value_product(jnp.maximum(j - 1, 0), p_prev)
            a_scr[...] = exp_pass(s_in, x_in[...], p_cur)

        scores(0, s0, x0)
        p1[...] = jnp.zeros_like(p1)
        a_scr[...] = jnp.ones_like(a_scr)
        m_scr[...] = jnp.full(m_scr.shape, -jnp.inf, F32)
        l_scr[...] = jnp.zeros_like(l_scr)
        acc_scr[...] = jnp.zeros_like(acc_scr)

        def two_passes(n, _):
            one_pass(2 * n, s0, x0, s1, x1, p1, p0)
            one_pass(2 * n + 1, s1, x1, s0, x0, p0, p1)
            return 0

        lax.fori_loop(0, i, two_passes, 0)
        d = 2 * i
        late = slice(bk, bq)
        s1[:, late] = _dot_nt(k_ref[0, at(d + 1), :], q_ref[0, late, :])
        acc = a_scr[...] * acc_scr[...] + value_product(jnp.maximum(d - 1, 0), p1)
        alpha = exp_pass(s0, None, p0, diagonal=True)
        acc = alpha * acc + value_product(d, p0)
        alpha = exp_pass(s1, None, p1, diagonal=True, cols=late)
        acc_late = alpha * acc[:, late] + _dot(vt_ref[0, :, at(d + 1)], p1[:, late])
        acc = jnp.concatenate([acc[:, :bk], acc_late], axis=1)
        o_ref[0] = jnp.transpose(acc / l_scr[...])
        lse_ref[0] = m_scr[...] + jnp.log2(l_scr[...])

    tile = lambda dtype: pltpu.VMEM((bk, bq), dtype)
    stat = pltpu.VMEM((1, bq), F32)
    return pl.pallas_call(
        body, name="attn_fwd", grid=(HEADS, t // bq),
        in_specs=[pl.BlockSpec((1, bq, HEAD_PAD), lambda h, i: (h, i, 0)),
                  pl.BlockSpec((1, t, HEAD_PAD), lambda h, i: (h, 0, 0)),
                  pl.BlockSpec((1, HEAD_PAD, t), lambda h, i: (h, 0, 0))],
        out_specs=[pl.BlockSpec((1, bq, HEAD_PAD), lambda h, i: (h, i, 0)),
                   pl.BlockSpec((1, 1, bq), lambda h, i: (h, 0, i))],
        out_shape=[jax.ShapeDtypeStruct((HEADS, t, HEAD_PAD), F32), jax.ShapeDtypeStruct((HEADS, 1, t), F32)],
        scratch_shapes=[tile(F32), tile(F32), tile(BF16), tile(BF16), stat, stat, stat, stat, stat,
                        pltpu.VMEM((HEAD_PAD, bq), F32)],
        compiler_params=_cparams(("arbitrary", "arbitrary")),
    )(q, k, vt)


def _mid(x, target, proj, ol, w_out, ws_low, ws_low_t, bsp, sgu_g, sgu_b, ln_g, ln_b):
    t = x.shape[0]
    tm = TOKEN_TILE
    n_steps = t // tm

    def body(x_ref, tgt_ref, za_ref, u_ref, v_ref, zb_ref, ol_ref, wout_ref, ws_ref, wst_ref, bsp_ref,
             sg_ref, sb_ref, lg_ref, lb_ref,
             dr_ref, do_ref, drow_ref, drest_ref, dwout_ref, dws_ref, dbs_ref, dlg_ref, dlb_ref, dsg_ref, dsb_ref,
             loss_ref, dbsp_acc):
        step = pl.program_id(0)

        @pl.when(step == 0)
        def _():
            dwout_ref[...] = jnp.zeros_like(dwout_ref)
            dws_ref[...] = jnp.zeros_like(dws_ref)
            dbs_ref[...] = jnp.zeros_like(dbs_ref)
            dlg_ref[...] = jnp.zeros_like(dlg_ref)
            dlb_ref[...] = jnp.zeros_like(dlb_ref)
            dsg_ref[...] = jnp.zeros_like(dsg_ref)
            dsb_ref[...] = jnp.zeros_like(dsb_ref)
            loss_ref[...] = jnp.zeros_like(loss_ref)
            dbsp_acc[...] = jnp.zeros_like(dbsp_acc)

        lane_head = lax.broadcasted_iota(jnp.int32, (CHUNK, G_WIDTH), 1) // G_HEAD_DIM

        attn = jnp.concatenate([ol_ref[h][:, NOPE:] for h in range(HEADS)], axis=-1)
        za = za_ref[...]
        sig_a = _sigmoid(za)
        silu_a = za * sig_a
        out_a = attn * silu_a
        u = u_ref[...]
        ug = _gelu(u)
        vpre = v_ref[...]
        gv = _gelu(vpre)
        mu_v = jnp.mean(gv, axis=-1, keepdims=True)
        cen_v = gv - mu_v
        rstd_v = lax.rsqrt(jnp.mean(cen_v * cen_v, axis=-1, keepdims=True) + EPS)
        vhat = cen_v * rstd_v
        vg = vhat * sg_ref[...] + sb_ref[...]
        vg_b = vg.astype(BF16)
        sv_parts = []
        for cix in range(tm // CHUNK):
            vc = vg_b[cix * CHUNK:(cix + 1) * CHUNK, :]
            acc = bsp_ref[...]
            for h in range(HEADS):
                acc = acc + jnp.where(lane_head == h, _dot(ws_ref[h], vc), 0.0)
            sv_parts.append(acc)
        sv = jnp.concatenate(sv_parts, axis=0)
        sgu = ug * sv
        zb = zb_ref[...]
        sig_b = _sigmoid(zb)
        silu_b = zb * sig_b
        out_b = sgu * silu_b
        merged = jnp.concatenate([out_a, out_b], axis=-1).astype(BF16)
        r = DN_ALPHA * x_ref[...] + _dot(merged, wout_ref[...])
        mu = jnp.mean(r, axis=-1, keepdims=True)
        cen = r - mu
        rstd = lax.rsqrt(jnp.mean(cen * cen, axis=-1, keepdims=True) + EPS)
        xhat = cen * rstd
        hout = xhat * lg_ref[...] + lb_ref[...]
        err = hout - tgt_ref[...]
        row_loss = jnp.mean(err * err, axis=-1, keepdims=True)
        loss_ref[...] += jnp.broadcast_to(0.5 * jnp.sum(row_loss, axis=0, keepdims=True), loss_ref.shape)

        dh = err * (1.0 / D_MODEL)
        dlg_ref[...] += jnp.sum(dh * xhat, axis=0, keepdims=True)
        dlb_ref[...] += jnp.sum(dh, axis=0, keepdims=True)
        dxhat = dh * lg_ref[...]
        dr = rstd * (dxhat - jnp.mean(dxhat, axis=-1, keepdims=True)
                     - xhat * jnp.mean(dxhat * xhat, axis=-1, keepdims=True))
        dr_ref[...] = dr
        dr_b = dr.astype(BF16)
        dwout_ref[...] += _dot_tn(merged, dr_b)
        dmerged = _dot_nt(dr_b, wout_ref[...])
        d_out_a = dmerged[:, :G_WIDTH]
        d_out_b = dmerged[:, G_WIDTH:]
        dattn = d_out_a * silu_a
        for h in range(HEADS):
            do_h = dattn[:, h * VDIM:(h + 1) * VDIM]
            dsum = jnp.sum(do_h * ol_ref[h][:, NOPE:], axis=-1, keepdims=True)
            drow_ref[h] = _as_row(dsum)
            do_ref[h] = jnp.concatenate([jnp.zeros((tm, NOPE), F32), do_h], axis=-1).astype(BF16)
        dza = d_out_a * attn * (sig_a * (1.0 + za * (1.0 - sig_a)))
        dsgu = d_out_b * silu_b
        dzb = d_out_b * sgu * (sig_b * (1.0 + zb * (1.0 - sig_b)))
        du = dsgu * sv * _gelu_grad(u)
        dsv = dsgu * ug
        dsv_b = dsv.astype(BF16)
        dvg_parts = []
        for cix in range(tm // CHUNK):
            rows = slice(cix * CHUNK, (cix + 1) * CHUNK)
            dsv_c = dsv[rows, :]
            dsv_cb = dsv_b[rows, :]
            vc = vg_b[rows, :]
            dbsp_acc[...] += dsv_c
            acc = jnp.zeros((CHUNK, G_WIDTH), F32)
            for h in range(HEADS):
                on = lane_head == h
                acc = acc + jnp.where(on, _dot(wst_ref[h], dsv_cb), 0.0)
                dws_ref[h] += _dot_nt(jnp.where(on, dsv_cb, jnp.zeros_like(dsv_cb)), vc)
            dvg_parts.append(acc)
        dvg = jnp.concatenate(dvg_parts, axis=0)
        dsg_ref[...] += jnp.sum(dvg * vhat, axis=0, keepdims=True)
        dsb_ref[...] += jnp.sum(dvg, axis=0, keepdims=True)
        dvhat = dvg * sg_ref[...]
        dgv = rstd_v * (dvhat - jnp.mean(dvhat, axis=-1, keepdims=True)
                        - vhat * jnp.mean(dvhat * vhat, axis=-1, keepdims=True))
        dv = dgv * _gelu_grad(vpre)
        drest_ref[...] = jnp.concatenate([dza, du, dv, dzb], axis=-1).astype(BF16)

        @pl.when(step == n_steps - 1)
        def _():
            tri = (lax.broadcasted_iota(jnp.int32, (CHUNK, CHUNK), 0)
                   >= lax.broadcasted_iota(jnp.int32, (CHUNK, CHUNK), 1))
            for h in range(HEADS):
                dws_ref[h] = jnp.where(tri, dws_ref[h], 0.0)
            tot = dbsp_acc[...]
            lane = lax.broadcasted_iota(jnp.int32, (CHUNK, LANES), 1)
            dbs = jnp.zeros((CHUNK, LANES), F32)
            for h in range(HEADS):
                head_sum = jnp.sum(tot[:, h * G_HEAD_DIM:(h + 1) * G_HEAD_DIM], axis=-1, keepdims=True)
                dbs = jnp.where(lane == h, head_sum, dbs)
            dbs_ref[...] = dbs

    full = lambda a: pl.BlockSpec(a.shape, lambda i: (0,) * a.ndim)
    tile = lambda w, j=0: pl.BlockSpec((tm, w), lambda i, j=j: (i, j))
    heads = pl.BlockSpec((HEADS, tm, HEAD_PAD), lambda i: (0, i, 0))
    acc = lambda shape: (pl.BlockSpec(shape, lambda i: (0,) * len(shape)), jax.ShapeDtypeStruct(shape, F32))
    accs = [acc((D_MODEL, D_MODEL)), acc((HEADS, CHUNK, CHUNK)), acc((CHUNK, LANES)), acc((1, D_MODEL)),
            acc((1, D_MODEL)), acc((1, G_WIDTH)), acc((1, G_WIDTH)), acc((1, LANES))]
    return pl.pallas_call(
        body, name="mid", grid=(n_steps,),
        in_specs=[tile(D_MODEL), tile(D_MODEL), tile(G_WIDTH, 1), tile(G_WIDTH, 2), tile(G_WIDTH, 3), tile(G_WIDTH, 4),
                  heads, full(w_out), full(ws_low), full(ws_low_t), full(bsp), full(sgu_g), full(sgu_b),
                  full(ln_g), full(ln_b)],
        out_specs=[tile(D_MODEL), heads, pl.BlockSpec((HEADS, 1, tm), lambda i: (0, 0, i)), tile(4 * G_WIDTH)]
        + [a[0] for a in accs],
        out_shape=[jax.ShapeDtypeStruct((t, D_MODEL), F32), jax.ShapeDtypeStruct((HEADS, t, HEAD_PAD), BF16),
                   jax.ShapeDtypeStruct((HEADS, 1, t), F32), jax.ShapeDtypeStruct((t, 4 * G_WIDTH), BF16)]
        + [a[1] for a in accs],
        scratch_shapes=[pltpu.VMEM((CHUNK, G_WIDTH), F32)],
        compiler_params=_cparams(("arbitrary",)),
    )(x, target, proj, proj, proj, proj, ol, w_out, ws_low, ws_low_t, bsp, sgu_g, sgu_b, ln_g, ln_b)


def _attn_bwd(q, k, v, do, lse_row, d_row):
    t = q.shape[1]
    bk, bq = ATTN_WIDE, ATTN_NARROW
    last = t // bq - 1
    chunk = SOFTMAX_ROWS

    def body(q_ref, k_ref, v_ref, do_ref, lse_ref, drow_ref, dqt_ref, dk_ref, dv_ref,
             s0, s1, e0, e1, p0, p1, g0, g1, kt_scr):
        j = pl.program_id(1)
        at = lambda i: pl.ds(pl.multiple_of(i * bq, bq), bq)

        @pl.when(j == 0)
        def _():
            dqt_ref[...] = jnp.zeros_like(dqt_ref)

        kt_scr[...] = jnp.transpose(k_ref[0].astype(F32)).astype(BF16)
        dk_ref[...] = jnp.zeros_like(dk_ref)
        dv_ref[...] = jnp.zeros_like(dv_ref)

        def products(i, s_out, e_out, keys=slice(0, bk)):
            i = jnp.minimum(i, last)
            s_out[keys, :] = _dot_nt(k_ref[0, keys, :], q_ref[0, at(i), :])
            e_out[keys, :] = _dot_nt(v_ref[0, keys, :], do_ref[0, at(i), :])

        def gradients(i, p_in, g_in, keys=slice(0, bk)):
            dv_ref[0, keys, :] += _dot(p_in[keys, :], do_ref[0, at(i), :])
            dk_ref[0, keys, :] += _dot(g_in[keys, :], q_ref[0, at(i), :])
            dqt_ref[0, :, at(i)] += _dot(kt_scr[:, keys], g_in[keys, :])

        def elementwise(i, s_in, e_in, p_out, g_out, qry0=None, keys=slice(0, bk)):
            lse = lse_ref[0, :, at(i)]
            dsum = drow_ref[0, :, at(i)]
            for r in range(keys.start, keys.stop, chunk):
                p = jnp.exp2(s_in[r:r + chunk, :] - lse)
                if qry0 is not None:
                    key = lax.broadcasted_iota(jnp.int32, (chunk, bq), 0) + r
                    qry = lax.broadcasted_iota(jnp.int32, (chunk, bq), 1) + qry0
                    p = jnp.where(qry >= key, p, 0.0)
                p_out[r:r + chunk, :] = p.astype(BF16)
                g_out[r:r + chunk, :] = (p * (e_in[r:r + chunk, :] - dsum)).astype(BF16)

        def one_pass(i, s_in, e_in, s_out, e_out, p_prev, g_prev, p_cur, g_cur, qry0=None, prev_keys=slice(0, bk)):
            products(i + 1, s_out, e_out)
            gradients(i - 1, p_prev, g_prev, prev_keys)
            elementwise(i, s_in, e_in, p_cur, g_cur, qry0)

        first = 2 * j
        early = slice(0, bq)
        products(first, s0, e0, early)
        products(first + 1, s1, e1)
        elementwise(first, s0, e0, p0, g0, qry0=0, keys=early)
        one_pass(first + 1, s1, e1, s0, e0, p0, g0, p1, g1, qry0=bq, prev_keys=early)

        def two_passes(n, _):
            i = first + 2 + 2 * n
            one_pass(i, s0, e0, s1, e1, p1, g1, p0, g0)
            one_pass(i + 1, s1, e1, s0, e0, p0, g0, p1, g1)
            return 0

        lax.fori_loop(0, (last - first - 1) // 2, two_passes, 0)
        gradients(last, p1, g1)
        dk_ref[0] = dk_ref[0] * LN2

    whole = pl.BlockSpec((1, t, HEAD_PAD), lambda h, j: (h, 0, 0))
    block = pl.BlockSpec((1, bk, HEAD_PAD), lambda h, j: (h, j, 0))
    rows = pl.BlockSpec((1, 1, t), lambda h, j: (h, 0, 0))
    shape = jax.ShapeDtypeStruct((HEADS, t, HEAD_PAD), F32)
    tile = lambda dtype: pltpu.VMEM((bk, bq), dtype)
    return pl.pallas_call(
        body, name="attn_bwd", grid=(HEADS, t // bk),
        in_specs=[whole, block, block, whole, rows, rows],
        out_specs=[pl.BlockSpec((1, HEAD_PAD, t), lambda h, j: (h, 0, 0)), block, block],
        out_shape=[jax.ShapeDtypeStruct((HEADS, HEAD_PAD, t), F32), shape, shape],
        scratch_shapes=[tile(F32), tile(F32), tile(F32), tile(F32), tile(BF16), tile(BF16),
                        tile(BF16), tile(BF16), pltpu.VMEM((HEAD_PAD, bk), BF16)],
        compiler_params=_cparams(("arbitrary", "arbitrary")),
    )(q, k, v, do, lse_row, d_row)


def _bwd_qkv(dq, dk, dv, proj, pos_col, invf_row, w_heads, q_g, kv_g):
    t = proj.shape[0]
    tm = TOKEN_TILE

    def body(dq_ref, dk_ref, dv_ref, ph_ref, pos_ref, invf_ref, wh_ref, qg_ref, kvg_ref,
             dhead_ref, dwh_ref, dqg_ref, dkvg_ref):
        @pl.when(pl.program_id(0) == 0)
        def _():
            dwh_ref[...] = jnp.zeros_like(dwh_ref)
            dqg_ref[...] = jnp.zeros_like(dqg_ref)
            dkvg_ref[...] = jnp.zeros_like(dkvg_ref)

        cos, s1, s2 = _rope_tables(pos_ref[...], invf_ref[...])
        lane = lax.broadcasted_iota(jnp.int32, (tm, LANES), 1)
        c_q = ph_ref[:, :Q_LORA]
        c_kv = ph_ref[:, Q_LORA:Q_LORA + KV_LORA]
        rstd_q = lax.rsqrt(jnp.mean(c_q * c_q, axis=-1, keepdims=True) + EPS)
        rstd_kv = lax.rsqrt(jnp.mean(c_kv * c_kv, axis=-1, keepdims=True) + EPS)
        qhat = c_q * rstd_q
        kvhat = c_kv * rstd_kv
        cqn = (qhat * qg_ref[...]).astype(BF16)
        ckvn = (kvhat * kvg_ref[...]).astype(BF16)
        dcqn = jnp.zeros((tm, Q_LORA), F32)
        dckvn = jnp.zeros((tm, KV_LORA), F32)
        dkr_rot = jnp.zeros((tm, LANES), F32)
        for h in range(HEADS):
            dq_b = _rope(jnp.transpose(dq_ref[h]) * ATTN_SCALE, cos, s1, s2, -1.0).astype(BF16)
            dk_h = dk_ref[h]
            dkv_b = jnp.where(lane < NOPE, dk_h, dv_ref[h]).astype(BF16)
            dkr_rot = dkr_rot + dk_h
            dwh_ref[h, :Q_LORA, :] += _dot_tn(cqn, dq_b)
            dwh_ref[h, Q_LORA:, :] += _dot_tn(ckvn, dkv_b)
            dcqn = dcqn + _dot_nt(dq_b, wh_ref[h, :Q_LORA, :])
            dckvn = dckvn + _dot_nt(dkv_b, wh_ref[h, Q_LORA:, :])
        rot_lanes = (lane >= KR_LO) & (lane < KR_LO + ROPE)
        dkr_raw = jnp.where(rot_lanes, _rope(dkr_rot, cos, s1, s2, -1.0), 0.0)
        dqg_ref[...] += jnp.sum(dcqn * qhat, axis=0, keepdims=True)
        dkvg_ref[...] += jnp.sum(dckvn * kvhat, axis=0, keepdims=True)
        dqh = dcqn * qg_ref[...]
        dkvh = dckvn * kvg_ref[...]
        dc_q = rstd_q * (dqh - qhat * jnp.mean(dqh * qhat, axis=-1, keepdims=True))
        dc_kv = rstd_kv * (dkvh - kvhat * jnp.mean(dkvh * kvhat, axis=-1, keepdims=True))
        dhead_ref[...] = jnp.concatenate([dc_q, dc_kv, dkr_raw], axis=-1).astype(BF16)

    full = lambda a: pl.BlockSpec(a.shape, lambda i: (0,) * a.ndim)
    heads = pl.BlockSpec((HEADS, tm, HEAD_PAD), lambda i: (0, i, 0))
    acc = lambda shape: (pl.BlockSpec(shape, lambda i: (0,) * len(shape)), jax.ShapeDtypeStruct(shape, F32))
    accs = [acc(w_heads.shape), acc((1, Q_LORA)), acc((1, KV_LORA))]
    return pl.pallas_call(
        body, name="bwd_qkv", grid=(t // tm,),
        in_specs=[pl.BlockSpec((HEADS, HEAD_PAD, tm), lambda i: (0, 0, i)), heads, heads,
                  pl.BlockSpec((tm, 4 * LANES), lambda i: (i, 0)),
                  pl.BlockSpec((tm, 1), lambda i: (i, 0)), full(invf_row), full(w_heads),
                  full(q_g), full(kv_g)],
        out_specs=[pl.BlockSpec((tm, 4 * LANES), lambda i: (i, 0))] + [a[0] for a in accs],
        out_shape=[jax.ShapeDtypeStruct((t, 4 * LANES), BF16)] + [a[1] for a in accs],
        compiler_params=_cparams(("arbitrary",)),
    )(dq, dk, dv, proj, pos_col, invf_row, w_heads, q_g, kv_g)


def _bwd_in(x, dr, dhead, drest, wp_in):
    t = x.shape[0]
    tm = TOKEN_TILE
    n_head = dhead.shape[1]

    def body(x_ref, dr_ref, dhead_ref, drest_ref, win_ref, gx_ref, dwin_ref):
        @pl.when(pl.program_id(0) == 0)
        def _():
            dwin_ref[...] = jnp.zeros_like(dwin_ref)

        xb = x_ref[...].astype(BF16)
        dh_b = dhead_ref[...]
        dr_b = drest_ref[...]
        gx_ref[...] = (DN_ALPHA * dr_ref[...] + _dot_nt(dh_b, win_ref[:, :n_head])
                       + _dot_nt(dr_b, win_ref[:, n_head:]))
        dwin_ref[:, :n_head] += _dot_tn(xb, dh_b)
        dwin_ref[:, n_head:] += _dot_tn(xb, dr_b)

    tile = lambda w: pl.BlockSpec((tm, w), lambda i: (i, 0))
    whole = pl.BlockSpec(wp_in.shape, lambda i: (0, 0))
    return pl.pallas_call(
        body, name="bwd_in", grid=(t // tm,),
        in_specs=[tile(D_MODEL), tile(D_MODEL), tile(n_head), tile(drest.shape[1]), whole],
        out_specs=[tile(D_MODEL), whole],
        out_shape=[jax.ShapeDtypeStruct((t, D_MODEL), F32), jax.ShapeDtypeStruct(wp_in.shape, F32)],
        compiler_params=_cparams(("arbitrary",)),
    )(x, dr, dhead, drest, wp_in)


def _adam(parts, w, m, v, *, name, tile_rows):
    n, rows, cols = parts.shape

    def body(p_ref, w_ref, m_ref, v_ref, g_ref, d_ref, nm_ref, nv_ref):
        g = p_ref[0].astype(F32)
        for s in range(1, n):
            g = g + p_ref[s].astype(F32)
        m_new = ADAM_B1 * m_ref[...] + (1.0 - ADAM_B1) * g
        v_new = ADAM_B2 * v_ref[...] + (1.0 - ADAM_B2) * (g * g)
        m_hat = m_new / (1.0 - ADAM_B1 ** ADAM_STEP)
        v_hat = v_new / (1.0 - ADAM_B2 ** ADAM_STEP)
        g_ref[...] = g
        d_ref[...] = -ADAM_LR * (m_hat / (jnp.sqrt(v_hat) + ADAM_EPS) + ADAM_WD * w_ref[...])
        nm_ref[...] = m_new
        nv_ref[...] = v_new

    flat = pl.BlockSpec((tile_rows, cols), lambda i: (i, 0))
    shape = jax.ShapeDtypeStruct((rows, cols), F32)
    return pl.pallas_call(
        body, name=name, grid=(rows // tile_rows,),
        in_specs=[pl.BlockSpec((n, tile_rows, cols), lambda i: (0, i, 0)), flat, flat, flat],
        out_specs=[flat] * 4, out_shape=[shape] * 4,
        compiler_params=_cparams(("arbitrary",)),
    )(parts, w, m, v)


SMALL_NAMES = ("q_norm_g", "kv_norm_g", "sgu_norm_g", "sgu_norm_b", "b_spatial", "ln_g", "ln_b")
SMALL_SIZES = (Q_LORA, KV_LORA, G_WIDTH, G_WIDTH, HEADS * CHUNK, D_MODEL, D_MODEL)


def _pack_small(vals, last=None):
    flat = jnp.concatenate([v.reshape(-1) for v in vals])
    pad = SMALL_LEN - flat.shape[0]
    if last is None:
        return jnp.pad(flat, (0, pad))
    return jnp.concatenate([flat, jnp.zeros((pad - 1,), F32), last.reshape(1)])


def _unpack_small(flat):
    out, at = [], 0
    for n in SMALL_SIZES:
        out.append(flat[at:at + n])
        at += n
    out[4] = out[4].reshape(HEADS, CHUNK)
    return out


UQ_SHARD = HEADS * (NOPE + ROPE) // N_DEV
HEAD_ROWS = Q_LORA + KV_LORA
MIXED_ROWS = HEAD_ROWS + CHUNK + SMALL_LEN // N_DEV // LANES


def _head_slab(w_uq_shard, w_ukv_shard):
    return jnp.concatenate([jnp.pad(w_uq_shard, ((0, 0), (0, LANES - UQ_SHARD))), w_ukv_shard])


IN_SHARD = D_IN // N_DEV


def _w_in_pieces():
    split = Q_LORA + KV_LORA
    moves = ((0, split, 0), (split, split + ROPE, KR_LO), (split + ROPE, D_IN, LANES - ROPE))
    pieces = []
    for s in range(N_DEV):
        lo, hi = s * IN_SHARD, (s + 1) * IN_SHARD
        for a, b, shift in moves:
            a, b = max(a, lo), min(b, hi)
            if a < b:
                pieces.append((s, a - lo, a + shift, b - a))
    return pieces


def _padded_w_in(shards):
    tr = TOKEN_TILE

    def body(sh_ref, o_ref):
        o_ref[...] = jnp.zeros_like(o_ref)
        for s, src, dst, width in _w_in_pieces():
            o_ref[:, dst:dst + width] = sh_ref[s, :, src:src + width]

    return pl.pallas_call(
        body, name="w_in_pad", grid=(D_MODEL // tr,),
        in_specs=[pl.BlockSpec((N_DEV, tr, IN_SHARD), lambda i: (0, i, 0))],
        out_specs=pl.BlockSpec((tr, D_IN_PAD), lambda i: (i, 0)),
        out_shape=jax.ShapeDtypeStruct((D_MODEL, D_IN_PAD), shards.dtype),
        compiler_params=_cparams(("arbitrary",)),
    )(shards)


def _w_in_shards(dwp_in):
    tr = TOKEN_TILE
    by_shard = [[p for p in _w_in_pieces() if p[0] == s] for s in range(N_DEV)]

    def body(w_ref, o_ref):
        for s, pieces in enumerate(by_shard):
            parts = [w_ref[:, dst:dst + width] for _, _, dst, width in pieces]
            o_ref[s] = parts[0] if len(parts) == 1 else jnp.concatenate(parts, axis=1)

    return pl.pallas_call(
        body, name="w_in_split", grid=(D_MODEL // tr,),
        in_specs=[pl.BlockSpec((tr, D_IN_PAD), lambda i: (i, 0))],
        out_specs=pl.BlockSpec((N_DEV, tr, IN_SHARD), lambda i: (0, i, 0)),
        out_shape=jax.ShapeDtypeStruct((N_DEV, D_MODEL, IN_SHARD), dwp_in.dtype),
        compiler_params=_cparams(("arbitrary",)),
    )(dwp_in)


def kernel(x, positions, w_in, q_norm_g, w_uq, kv_norm_g, w_ukv, sgu_norm_g, sgu_norm_b, w_spatial, b_spatial, w_out, ln_g, ln_b, loss_target, m_w_in, m_q_norm_g, m_w_uq, m_kv_norm_g, m_w_ukv, m_sgu_norm_g, m_sgu_norm_b, m_w_spatial, m_b_spatial, m_w_out, m_ln_g, m_ln_b, v_w_in, v_q_norm_g, v_w_uq, v_kv_norm_g, v_w_ukv, v_sgu_norm_g, v_sgu_norm_b, v_w_spatial, v_b_spatial, v_w_out, v_ln_g, v_ln_b):
    me = 4 * lax.axis_index("x") + 2 * lax.axis_index("y") + lax.axis_index("c")
    seq = x.shape[1]
    x2 = x.reshape(seq, D_MODEL)
    tgt2 = loss_target.reshape(seq, D_MODEL)
    pos_col = positions.reshape(seq, 1)

    w_in_shards, w_out_shards, w_heads = _gather_two_level(
        [w_in.astype(BF16), w_out.astype(BF16), _head_slab(w_uq, w_ukv).astype(BF16)],
        name="wgather")
    (loss_part, grad_x, d_in, d_heads, d_out, d_ws, d_bs_t, d_lng, d_lnb, d_sgug, d_sgub, d_qg, d_kvg) = _local_step(
        x2, tgt2, pos_col, w_in_shards, w_heads, w_out_shards.reshape(D_MODEL, D_MODEL), q_norm_g, kv_norm_g,
        sgu_norm_g, sgu_norm_b, w_spatial, b_spatial, ln_g, ln_b)

    small_part = _pack_small([d_qg, d_kvg, d_sgug, d_sgub, d_bs_t[:, :HEADS].T, d_lng, d_lnb], last=loss_part[0, :1])
    mixed = jnp.concatenate([d_heads, d_ws, small_part.reshape(N_DEV, -1, LANES)], axis=1)
    by_chip = [g.reshape((N_CHIPS, 2) + g.shape[1:])
               for g in (d_in, d_out.reshape(N_DEV, D_MODEL // N_DEV, D_MODEL), mixed)]
    from_sibling = _sibling_swap(by_chip, name="gswap")
    core = lax.axis_index("c").astype(jnp.int32).reshape(1)
    pair_sums = [_pair_sum(a, b, core, name=nm, tile_rows=tr, out_dtype=dt) for a, b, nm, tr, dt in zip(
        by_chip, from_sibling, ("gsum_in", "gsum_out", "gsum_mixed"), (TOKEN_TILE, D_MODEL // N_DEV, MIXED_ROWS),
        (BF16, BF16, F32))]
    recv_in, recv_out, recv_mixed = _chip_exchange(pair_sums, name="gexch")

    take = lambda a: lax.dynamic_index_in_dim(a, me, 0, keepdims=False)
    small_w = _pack_small([q_norm_g, kv_norm_g, sgu_norm_g, sgu_norm_b, b_spatial, ln_g, ln_b])
    small_m = _pack_small([m_q_norm_g, m_kv_norm_g, m_sgu_norm_g, m_sgu_norm_b, m_b_spatial, m_ln_g, m_ln_b])
    small_v = _pack_small([v_q_norm_g, v_kv_norm_g, v_sgu_norm_g, v_sgu_norm_b, v_b_spatial, v_ln_g, v_ln_b])
    own_mixed = lambda uq, ukv, sp, small: jnp.concatenate(
        [_head_slab(uq, ukv), take(sp), take(small.reshape(N_DEV, -1, LANES))])
    res_in = _adam(recv_in, w_in, m_w_in, v_w_in, name="adam_in", tile_rows=TOKEN_TILE)
    res_out = _adam(recv_out, w_out, m_w_out, v_w_out, name="adam_out", tile_rows=D_MODEL // N_DEV)
    res_mixed = _adam(recv_mixed, own_mixed(w_uq, w_ukv, w_spatial, small_w), own_mixed(m_w_uq, m_w_ukv, m_w_spatial, small_m),
                      own_mixed(v_w_uq, v_w_ukv, v_w_spatial, small_v), name="adam_mixed", tile_rows=MIXED_ROWS)

    rep_g, = _exchange([res_mixed[0][HEAD_ROWS:]], name="sgather", per_destination=False)
    rep_pack = lambda sp, small: jnp.concatenate(
        [sp.reshape(N_DEV, CHUNK, LANES), small.reshape(N_DEV, -1, LANES)], axis=1).reshape(-1, LANES)
    _, delta_rep, m_rep, v_rep = _adam(rep_g.reshape(1, N_DEV * REP_ROWS, LANES), rep_pack(w_spatial, small_w),
                                       rep_pack(m_w_spatial, small_m), rep_pack(v_w_spatial, small_v),
                                       name="adam_rep", tile_rows=N_DEV * REP_ROWS)

    def rep_unpack(a):
        a = a.reshape(N_DEV, REP_ROWS, LANES)
        small = _unpack_small(a[:, CHUNK:].reshape(-1))
        return [small[0], small[1], small[2], small[3], a[:, :CHUNK], small[4], small[5], small[6]]

    def ordered(which, rep):
        r_qg, r_kvg, r_sg, r_sb, r_ws, r_bs, r_lg, r_lb = rep_unpack(rep)
        heads = res_mixed[which]
        return [res_in[which], r_qg, heads[:Q_LORA, :UQ_SHARD], r_kvg, heads[Q_LORA:HEAD_ROWS], r_sg, r_sb, r_ws, r_bs,
                res_out[which], r_lg, r_lb]

    loss = rep_g[N_DEV - 1, REP_ROWS - 1, LANES - 1]
    outs = [loss, grad_x.reshape(x.shape)]
    outs += ordered(0, rep_g.reshape(-1, LANES))
    outs += ordered(1, delta_rep)
    outs += ordered(2, m_rep)
    outs += ordered(3, v_rep)
    return tuple(outs)


def _local_step(x2, tgt2, pos_col, w_in_shards, w_heads, w_out_full, q_norm_g, kv_norm_g, sgu_norm_g, sgu_norm_b,
                w_spatial, b_spatial, ln_g, ln_b):
    wp_in = _padded_w_in(w_in_shards)

    half = jnp.arange(HALF, dtype=F32)
    inv_freq = 1.0 / (ROPE_THETA ** (half / HALF))
    invf_row = jnp.concatenate([jnp.zeros((KR_LO,), F32), inv_freq, inv_freq,
                                jnp.zeros((LANES - KR_LO - ROPE,), F32)]).reshape(1, LANES)
    tri = jnp.tril(jnp.ones((CHUNK, CHUNK), dtype=bool))
    ws_low = jnp.where(tri[None], w_spatial, 0.0).astype(BF16)
    ws_low_t = ws_low.transpose(0, 2, 1)
    bsp = jnp.repeat(b_spatial.T, G_HEAD_DIM, axis=1)
    row = lambda a: a.reshape(1, -1)

    proj, q, k, v, vt = _fwd_proj(x2, pos_col, invf_row, wp_in, w_heads, row(q_norm_g), row(kv_norm_g))
    o, lse_row = _attn_fwd(q, k, vt)
    (dr, do, d_row, drest, d_out, d_ws, d_bs_t, d_lng, d_lnb, d_sgug, d_sgub, loss_part) = _mid(
        x2, tgt2, proj, o, w_out_full, ws_low, ws_low_t, bsp, row(sgu_norm_g), row(sgu_norm_b), row(ln_g), row(ln_b))
    dqt, dk, dv = _attn_bwd(q, k, v, do, lse_row, d_row)
    dhead, d_heads, d_qg, d_kvg = _bwd_qkv(dqt, dk, dv, proj, pos_col, invf_row, w_heads, row(q_norm_g), row(kv_norm_g))
    grad_x, dwp_in = _bwd_in(x2, dr, dhead, drest, wp_in)
    return (loss_part, grad_x, _w_in_shards(dwp_in), d_heads, d_out, d_ws, d_bs_t, d_lng, d_lnb, d_sgug, d_sgub,
            d_qg, d_kvg)
```

```python
import functools
import math

import jax
import jax.numpy as jnp
from jax import lax
from jax.experimental import pallas as pl
from jax.experimental.pallas import tpu as pltpu

F32 = jnp.float32
BF16 = jnp.bfloat16

N_DEV = 8
D_MODEL = 1024
HEADS = 8
NOPE = 64
ROPE = 32
HALF = ROPE // 2
VDIM = 64
Q_LORA = 256
KV_LORA = 128
G_WIDTH = 512
G_HEAD_DIM = 64
CHUNK = 128
HEAD_PAD = 128
D_IN = 2464
D_IN_PAD = 2560
KR_LO = NOPE
ROPE_THETA = 10000.0
DN_ALPHA = 2.0 ** 0.25
EPS = 1e-5
ATTN_SCALE = 1.0 / math.sqrt(NOPE + ROPE)
ADAM_LR, ADAM_B1, ADAM_B2, ADAM_EPS, ADAM_WD, ADAM_STEP = 0.001, 0.9, 0.999, 1e-08, 0.01, 10

LANES = 128
REP_ROWS = 136
SMALL_LEN = 8192
VMEM_LIMIT = 56 * 1024 * 1024

TOKEN_TILE = 256
ATTN_FWD_WIDE = 2048
ATTN_BWD_WIDE = 2048
ATTN_NARROW = 512
SOFTMAX_ROWS = 256
LOG2E = 1.4426950408889634
LN2 = 0.6931471805599453
Q_PRESCALE = ATTN_SCALE * LOG2E


def _cparams(sem=None):
    return pltpu.CompilerParams(dimension_semantics=sem, vmem_limit_bytes=VMEM_LIMIT)


def _dot(a, b):
    return jnp.dot(a, b, preferred_element_type=F32)


def _dot_nt(a, b):
    return lax.dot_general(a, b, (((1,), (1,)), ((), ())), preferred_element_type=F32)


def _dot_tn(a, b):
    return lax.dot_general(a, b, (((0,), (0,)), ((), ())), preferred_element_type=F32)


def _as_row(col):
    return jnp.transpose(jnp.broadcast_to(col, (col.shape[0], LANES)))[0:1, :]


def _sigmoid(z):
    return 1.0 / (1.0 + jnp.exp(-z))


def _gelu(x):
    return 0.5 * x * (1.0 + lax.erf(x * 0.7071067811865476))


def _gelu_grad(x):
    cdf = 0.5 * (1.0 + lax.erf(x * 0.7071067811865476))
    return cdf + x * jnp.exp(-0.5 * x * x) * 0.3989422804014327


def _exchange(srcs, *, name, per_destination):
    n = len(srcs)
    slab_shapes = [s.shape[1:] if per_destination else s.shape for s in srcs]

    def body(*refs):
        src_refs, out_refs = refs[:n], refs[n:2 * n]
        send_sems, recv_sems, local_sems = refs[2 * n:]
        x, y, c = lax.axis_index("x"), lax.axis_index("y"), lax.axis_index("c")
        me = 4 * x + 2 * y + c

        def slab_for(t, dest):
            return src_refs[t].at[dest] if per_destination else src_refs[t]

        mine = [pltpu.make_async_copy(slab_for(t, me), out_refs[t].at[me], local_sems.at[t]) for t in range(n)]
        for cp in mine:
            cp.start()
        sends, arrivals = [], []
        for k in (6, 7, 4, 5, 2, 3, 1):
            px = 1 - x if k & 4 else x
            py = 1 - y if k & 2 else y
            pc = 1 - c if k & 1 else c
            peer = 4 * px + 2 * py + pc
            for t in range(n):
                sem = (k - 1) * n + t
                cp = pltpu.make_async_remote_copy(
                    src_ref=slab_for(t, peer), dst_ref=out_refs[t].at[me],
                    send_sem=send_sems.at[sem], recv_sem=recv_sems.at[sem],
                    device_id=(px, py, pc), device_id_type=pl.DeviceIdType.MESH)
                cp.start()
                sends.append(cp)
                arrivals.append(pltpu.make_async_remote_copy(
                    src_ref=slab_for(t, peer), dst_ref=out_refs[t].at[peer],
                    send_sem=send_sems.at[sem], recv_sem=recv_sems.at[sem],
                    device_id=(x, y, c), device_id_type=pl.DeviceIdType.MESH))
        for cp in arrivals:
            cp.wait_recv()
        for cp in sends:
            cp.wait_send()
        for cp in mine:
            cp.wait()

    hbm = pl.BlockSpec(memory_space=pl.ANY)
    return pl.pallas_call(
        body, name=name,
        out_shape=[jax.ShapeDtypeStruct((N_DEV,) + tuple(shape), s.dtype) for shape, s in zip(slab_shapes, srcs)],
        in_specs=[hbm] * n, out_specs=[hbm] * n,
        scratch_shapes=[pltpu.SemaphoreType.DMA(((N_DEV - 1) * n,)), pltpu.SemaphoreType.DMA(((N_DEV - 1) * n,)),
                        pltpu.SemaphoreType.DMA((n,))],
    )(*srcs)


def _gather_two_level(srcs, *, name):
    n = len(srcs)

    def body(*refs):
        src_refs, out_refs = refs[:n], refs[n:2 * n]
        send_sems, recv_sems, local_sems = refs[2 * n:]
        x, y, c = lax.axis_index("x"), lax.axis_index("y"), lax.axis_index("c")
        me, sibling = (x, y, c), (x, y, 1 - c)
        chips = [(1 - x, 1 - y), (1 - x, y), (x, 1 - y)]
        index = lambda px, py, pc: 4 * px + 2 * py + pc

        def copy(k, t, block, to, src=None):
            place = out_refs[t].at[index(*block)]
            return pltpu.make_async_remote_copy(
                src_ref=place if src is None else src, dst_ref=place,
                send_sem=send_sems.at[k * n + t], recv_sem=recv_sems.at[k * n + t],
                device_id=to, device_id_type=pl.DeviceIdType.MESH)

        mine = [pltpu.make_async_copy(src_refs[t], out_refs[t].at[index(*me)], local_sems.at[t]) for t in range(n)]
        for cp in mine:
            cp.start()
        first = [copy(1 + j, t, me, (*chip, c), src=src_refs[t]) for j, chip in enumerate(chips) for t in range(n)]
        first += [copy(0, t, me, sibling, src=src_refs[t]) for t in range(n)]
        for cp in first:
            cp.start()
        passed = []
        for j, chip in enumerate(chips):
            for t in range(n):
                copy(1 + j, t, (*chip, c), me).wait_recv()
                cp = copy(4 + j, t, (*chip, c), sibling)
                cp.start()
                passed.append(cp)
        for t in range(n):
            copy(0, t, sibling, me).wait_recv()
        for j, chip in enumerate(chips):
            for t in range(n):
                copy(4 + j, t, (*chip, 1 - c), me).wait_recv()
        for cp in first + passed:
            cp.wait_send()
        for cp in mine:
            cp.wait()

    hbm = pl.BlockSpec(memory_space=pl.ANY)
    return pl.pallas_call(
        body, name=name,
        out_shape=[jax.ShapeDtypeStruct((N_DEV,) + s.shape, s.dtype) for s in srcs],
        in_specs=[hbm] * n, out_specs=[hbm] * n,
        scratch_shapes=[pltpu.SemaphoreType.DMA((7 * n,)), pltpu.SemaphoreType.DMA((7 * n,)),
                        pltpu.SemaphoreType.DMA((n,))],
    )(*srcs)


N_CHIPS = N_DEV // 2


def _sibling_swap(srcs, *, name):
    n = len(srcs)

    def body(*refs):
        src_refs, out_refs = refs[:n], refs[n:2 * n]
        send_sems, recv_sems = refs[2 * n:]
        x, y, c = lax.axis_index("x"), lax.axis_index("y"), lax.axis_index("c")
        sends = []
        for chip in range(N_CHIPS):
            for t in range(n):
                cp = pltpu.make_async_remote_copy(
                    src_ref=src_refs[t].at[chip, 1 - c], dst_ref=out_refs[t].at[chip],
                    send_sem=send_sems.at[chip * n + t], recv_sem=recv_sems.at[chip * n + t],
                    device_id=(x, y, 1 - c), device_id_type=pl.DeviceIdType.MESH)
                cp.start()
                sends.append(cp)
        for cp in sends:
            cp.wait_recv()
        for cp in sends:
            cp.wait_send()

    hbm = pl.BlockSpec(memory_space=pl.ANY)
    return pl.pallas_call(
        body, name=name,
        out_shape=[jax.ShapeDtypeStruct((N_CHIPS,) + s.shape[2:], s.dtype) for s in srcs],
        in_specs=[hbm] * n, out_specs=[hbm] * n,
        scratch_shapes=[pltpu.SemaphoreType.DMA((N_CHIPS * n,)), pltpu.SemaphoreType.DMA((N_CHIPS * n,))],
    )(*srcs)


def _pair_sum(mine, theirs, core, *, name, tile_rows, out_dtype):
    _, _, rows, cols = mine.shape

    def body(core_ref, a_ref, b_ref, o_ref):
        o_ref[...] = (a_ref[0] + b_ref[...]).astype(out_dtype)

    return pl.pallas_call(
        body, name=name,
        grid_spec=pltpu.PrefetchScalarGridSpec(
            num_scalar_prefetch=1, grid=(N_CHIPS, rows // tile_rows),
            in_specs=[pl.BlockSpec((1, 1, tile_rows, cols), lambda q, r, core_ref: (q, core_ref[0], r, 0)),
                      pl.BlockSpec((1, tile_rows, cols), lambda q, r, core_ref: (q, r, 0))],
            out_specs=pl.BlockSpec((1, tile_rows, cols), lambda q, r, core_ref: (q, r, 0))),
        out_shape=jax.ShapeDtypeStruct((N_CHIPS, rows, cols), out_dtype),
        compiler_params=_cparams(("arbitrary", "arbitrary")),
    )(core, mine, theirs)


def _chip_exchange(srcs, *, name):
    n = len(srcs)

    def body(*refs):
        src_refs, out_refs = refs[:n], refs[n:2 * n]
        send_sems, recv_sems, local_sems = refs[2 * n:]
        x, y, c = lax.axis_index("x"), lax.axis_index("y"), lax.axis_index("c")
        my_chip = 2 * x + y
        mine = [pltpu.make_async_copy(src_refs[t].at[my_chip], out_refs[t].at[my_chip], local_sems.at[t])
                for t in range(n)]
        for cp in mine:
            cp.start()
        sends, arrivals = [], []
        for k in (3, 2, 1):
            px = 1 - x if k & 2 else x
            py = 1 - y if k & 1 else y
            peer_chip = 2 * px + py
            for t in range(n):
                sem = (k - 1) * n + t
                cp = pltpu.make_async_remote_copy(
                    src_ref=src_refs[t].at[peer_chip], dst_ref=out_refs[t].at[my_chip],
                    send_sem=send_sems.at[sem], recv_sem=recv_sems.at[sem],
                    device_id=(px, py, c), device_id_type=pl.DeviceIdType.MESH)
                cp.start()
                sends.append(cp)
                arrivals.append(pltpu.make_async_remote_copy(
                    src_ref=src_refs[t].at[peer_chip], dst_ref=out_refs[t].at[peer_chip],
                    send_sem=send_sems.at[sem], recv_sem=recv_sems.at[sem],
                    device_id=(x, y, c), device_id_type=pl.DeviceIdType.MESH))
        for cp in arrivals:
            cp.wait_recv()
        for cp in sends:
            cp.wait_send()
        for cp in mine:
            cp.wait()

    hbm = pl.BlockSpec(memory_space=pl.ANY)
    return pl.pallas_call(
        body, name=name,
        out_shape=[jax.ShapeDtypeStruct(s.shape, s.dtype) for s in srcs],
        in_specs=[hbm] * n, out_specs=[hbm] * n,
        scratch_shapes=[pltpu.SemaphoreType.DMA((3 * n,)), pltpu.SemaphoreType.DMA((3 * n,)),
                        pltpu.SemaphoreType.DMA((n,))],
    )(*srcs)


def _rope_tables(pos_col, invf_row):
    ang = pos_col.astype(F32) * invf_row
    lane = lax.broadcasted_iota(jnp.int32, ang.shape, 1)
    cos, sin = jnp.cos(ang), jnp.sin(ang)
    first = (lane >= KR_LO) & (lane < KR_LO + HALF)
    second = (lane >= KR_LO + HALF) & (lane < KR_LO + ROPE)
    return cos, jnp.where(first, sin, 0.0), jnp.where(second, sin, 0.0)


def _rope(t, cos, sin_first, sin_second, sign):
    up = pltpu.roll(t, LANES - HALF, 1)
    down = pltpu.roll(t, HALF, 1)
    return t * cos - sign * (up * sin_first) + sign * (down * sin_second)


def _fwd_proj(x, pos_col, invf_row, wp_in, w_heads, q_g, kv_g):
    t = x.shape[0]
    tm = TOKEN_TILE

    def body(x_ref, pos_ref, invf_ref, win_ref, wh_ref, qg_ref, kvg_ref,
             proj_ref, q_ref, k_ref, v_ref, vt_ref):
        proj = _dot(x_ref[...].astype(BF16), win_ref[...])
        proj_ref[...] = proj
        c_q = proj[:, :Q_LORA]
        c_kv = proj[:, Q_LORA:Q_LORA + KV_LORA]
        kr_raw = proj[:, Q_LORA + KV_LORA:Q_LORA + KV_LORA + LANES]
        cqn = (c_q * lax.rsqrt(jnp.mean(c_q * c_q, axis=-1, keepdims=True) + EPS) * qg_ref[...]).astype(BF16)
        ckvn = (c_kv * lax.rsqrt(jnp.mean(c_kv * c_kv, axis=-1, keepdims=True) + EPS) * kvg_ref[...]).astype(BF16)
        cos, s1, s2 = _rope_tables(pos_ref[...], invf_ref[...])
        kr = _rope(kr_raw, cos, s1, s2, 1.0)
        lane = lax.broadcasted_iota(jnp.int32, (tm, HEAD_PAD), 1)
        for h in range(HEADS):
            q_h = _dot(cqn, wh_ref[h, :Q_LORA, :])
            kv_h = _dot(ckvn, wh_ref[h, Q_LORA:, :])
            q_ref[h] = (_rope(q_h, cos, s1, s2, 1.0) * Q_PRESCALE).astype(BF16)
            k_ref[h] = jnp.where(lane < NOPE, kv_h, kr).astype(BF16)
            v_ref[h] = kv_h.astype(BF16)
            vt_ref[h] = jnp.transpose(kv_h).astype(BF16)

    full = lambda a: pl.BlockSpec(a.shape, lambda i: (0,) * a.ndim)
    head_spec = pl.BlockSpec((HEADS, tm, HEAD_PAD), lambda i: (0, i, 0))
    head_shape = jax.ShapeDtypeStruct((HEADS, t, HEAD_PAD), BF16)
    return pl.pallas_call(
        body, name="fwd_proj", grid=(t // tm,),
        in_specs=[pl.BlockSpec((tm, D_MODEL), lambda i: (i, 0)), pl.BlockSpec((tm, 1), lambda i: (i, 0)),
                  full(invf_row), full(wp_in), full(w_heads), full(q_g), full(kv_g)],
        out_specs=[pl.BlockSpec((tm, D_IN_PAD), lambda i: (i, 0)), head_spec, head_spec, head_spec,
                   pl.BlockSpec((HEADS, HEAD_PAD, tm), lambda i: (0, 0, i))],
        out_shape=[jax.ShapeDtypeStruct((t, D_IN_PAD), F32), head_shape, head_shape, head_shape,
                   jax.ShapeDtypeStruct((HEADS, HEAD_PAD, t), BF16)],
        compiler_params=_cparams(("arbitrary",)),
    )(x, pos_col, invf_row, wp_in, w_heads, q_g, kv_g)


def _attn_fwd(q, k, vt):
    t = q.shape[1]
    bq, bk = ATTN_FWD_WIDE, ATTN_NARROW
    n_diag = bq // bk
    chunk = SOFTMAX_ROWS

    def body(q_ref, k_ref, vt_ref, o_ref, lse_ref, s0, s1, p0, p1, x0, x1, m_scr, l_scr, a_scr, acc_scr):
        i = pl.program_id(1)
        at = lambda j: pl.ds(pl.multiple_of(j * bk, bk), bk)

        def exp_pass(s_in, block_max, p_out, diagonal=False, cols=slice(None)):
            width = bq if cols == slice(None) else cols.stop - cols.start

            def load(r):
                s = s_in[r:r + chunk, cols]
                if diagonal:
                    key = lax.broadcasted_iota(jnp.int32, (chunk, width), 0) + r
                    qry = lax.broadcasted_iota(jnp.int32, (chunk, width), 1)
                    s = jnp.where(qry >= key, s, -jnp.inf)
                return s

            if diagonal:
                block_max = jnp.max(load(0), axis=0, keepdims=True)
                for r in range(chunk, bk, chunk):
                    block_max = jnp.maximum(block_max, jnp.max(load(r), axis=0, keepdims=True))
            m_old = m_scr[:, cols]
            m_new = jnp.maximum(m_old, block_max)
            alpha = jnp.exp2(m_old - m_new)
            total = jnp.zeros((1, width), F32)
            for r in range(0, bk, chunk):
                p = jnp.exp2(load(r) - m_new)
                p_out[r:r + chunk, cols] = p.astype(BF16)
                total = total + jnp.sum(p, axis=0, keepdims=True)
            m_scr[:, cols] = m_new
            l_scr[:, cols] = alpha * l_scr[:, cols] + total
            return alpha

        def scores(j, s_out, x_out):
            s = _dot_nt(k_ref[0, at(j), :], q_ref[0])
            s_out[...] = s
            x_out[...] = jnp.max(s, axis=0, keepdims=True)

        def value_product(j, p_in):
            return _dot(vt_ref[0, :, at(j)], p_in[...])

        def one_pass(j, s_in, x_in, s_out, x_out, p_prev, p_cur):
            scores(j + 1, s_out, x_out)
            acc_scr[...] = a_scr[...] * acc_scr[...] + value_product(jnp.maximum(j - 1, 0), p_prev)
            a_scr[...] = exp_pass(s_in, x_in[...], p_cur)

        scores(0, s0, x0)
        p1[...] = jnp.zeros_like(p1)
        a_scr[...] = jnp.ones_like(a_scr)
        m_scr[...] = jnp.full(m_scr.shape, -jnp.inf, F32)
        l_scr[...] = jnp.zeros_like(l_scr)
        acc_scr[...] = jnp.zeros_like(acc_scr)

        def two_passes(n, _):
            one_pass(2 * n, s0, x0, s1, x1, p1, p0)
            one_pass(2 * n + 1, s1, x1, s0, x0, p0, p1)
            return 0

        lax.fori_loop(0, (n_diag // 2) * i, two_passes, 0)
        d = n_diag * i
        alpha, p_prev, cols = a_scr[...], p1, slice(0, bq)
        for u in range(n_diag + 1):
            s_in, s_next, p_cur = (s0, s1, p0) if u % 2 == 0 else (s1, s0, p1)
            if u + 1 < n_diag:
                ahead = slice((u + 1) * bk, bq)
                s_next[:, ahead] = _dot_nt(k_ref[0, at(d + u + 1), :], q_ref[0, ahead, :])
            acc_scr[:, cols] = alpha * acc_scr[:, cols] + _dot(vt_ref[0, :, at(jnp.maximum(d + u - 1, 0))], p_prev[:, cols])
            if u < n_diag:
                cols = slice(u * bk, bq)
                alpha = exp_pass(s_in, None, p_cur, diagonal=True, cols=cols)
                p_prev = p_cur
        o_ref[0] = jnp.transpose(acc_scr[...] / l_scr[...])
        lse_ref[0] = m_scr[...] + jnp.log2(l_scr[...])

    tile = lambda dtype: pltpu.VMEM((bk, bq), dtype)
    stat = pltpu.VMEM((1, bq), F32)
    return pl.pallas_call(
        body, name="attn_fwd", grid=(HEADS, t // bq),
        in_specs=[pl.BlockSpec((1, bq, HEAD_PAD), lambda h, i: (h, i, 0)),
                  pl.BlockSpec((1, t, HEAD_PAD), lambda h, i: (h, 0, 0)),
                  pl.BlockSpec((1, HEAD_PAD, t), lambda h, i: (h, 0, 0))],
        out_specs=[pl.BlockSpec((1, bq, HEAD_PAD), lambda h, i: (h, i, 0)),
                   pl.BlockSpec((1, 1, bq), lambda h, i: (h, 0, i))],
        out_shape=[jax.ShapeDtypeStruct((HEADS, t, HEAD_PAD), F32), jax.ShapeDtypeStruct((HEADS, 1, t), F32)],
        scratch_shapes=[tile(F32), tile(F32), tile(BF16), tile(BF16), stat, stat, stat, stat, stat,
                        pltpu.VMEM((HEAD_PAD, bq), F32)],
        compiler_params=_cparams(("arbitrary", "arbitrary")),
    )(q, k, vt)


def _mid(x, target, proj, ol, w_out, ws_low, ws_low_t, bsp, sgu_g, sgu_b, ln_g, ln_b):
    t = x.shape[0]
    tm = TOKEN_TILE
    n_steps = t // tm

    def body(x_ref, tgt_ref, za_ref, u_ref, v_ref, zb_ref, ol_ref, wout_ref, ws_ref, wst_ref, bsp_ref,
             sg_ref, sb_ref, lg_ref, lb_ref,
             dr_ref, do_ref, drow_ref, drest_ref, dwout_ref, dws_ref, dbs_ref, dlg_ref, dlb_ref, dsg_ref, dsb_ref,
             loss_ref, dbsp_acc):
        step = pl.program_id(0)

        @pl.when(step == 0)
        def _():
            dwout_ref[...] = jnp.zeros_like(dwout_ref)
            dws_ref[...] = jnp.zeros_like(dws_ref)
            dbs_ref[...] = jnp.zeros_like(dbs_ref)
            dlg_ref[...] = jnp.zeros_like(dlg_ref)
            dlb_ref[...] = jnp.zeros_like(dlb_ref)
            dsg_ref[...] = jnp.zeros_like(dsg_ref)
            dsb_ref[...] = jnp.zeros_like(dsb_ref)
            loss_ref[...] = jnp.zeros_like(loss_ref)
            dbsp_acc[...] = jnp.zeros_like(dbsp_acc)

        n_chunks = tm // CHUNK
        groups = G_WIDTH // LANES

        def side_by_side(a):
            return [jnp.concatenate([a[c * CHUNK:(c + 1) * CHUNK, g * LANES:(g + 1) * LANES] for c in range(n_chunks)],
                                    axis=1) for g in range(groups)]

        def by_chunk(wide):
            return jnp.concatenate([jnp.concatenate([wide[g][:, c * LANES:(c + 1) * LANES] for g in range(groups)], axis=1)
                                    for c in range(n_chunks)], axis=0)

        def own_lanes(h):
            lane = lax.broadcasted_iota(jnp.int32, (CHUNK, n_chunks * LANES), 1)
            return (lane % LANES) // G_HEAD_DIM == h % 2

        def spatial(w_ref, wide):
            return [sum(jnp.where(own_lanes(h), _dot(w_ref[h], wide[g]), 0.0) for h in (2 * g, 2 * g + 1))
                    for g in range(groups)]

        attn = jnp.concatenate([ol_ref[h][:, NOPE:] for h in range(HEADS)], axis=-1)
        za = za_ref[...]
        sig_a = _sigmoid(za)
        silu_a = za * sig_a
        out_a = attn * silu_a
        u = u_ref[...]
        ug = _gelu(u)
        vpre = v_ref[...]
        gv = _gelu(vpre)
        mu_v = jnp.mean(gv, axis=-1, keepdims=True)
        cen_v = gv - mu_v
        rstd_v = lax.rsqrt(jnp.mean(cen_v * cen_v, axis=-1, keepdims=True) + EPS)
        vhat = cen_v * rstd_v
        vg = vhat * sg_ref[...] + sb_ref[...]
        vg_b = vg.astype(BF16)
        sv = by_chunk(spatial(ws_ref, side_by_side(vg_b))) + jnp.tile(bsp_ref[...], (n_chunks, 1))
        sgu = ug * sv
        zb = zb_ref[...]
        sig_b = _sigmoid(zb)
        silu_b = zb * sig_b
        out_b = sgu * silu_b
        merged = jnp.concatenate([out_a, out_b], axis=-1).astype(BF16)
        r = DN_ALPHA * x_ref[...] + _dot(merged, wout_ref[...])
        mu = jnp.mean(r, axis=-1, keepdims=True)
        cen = r - mu
        rstd = lax.rsqrt(jnp.mean(cen * cen, axis=-1, keepdims=True) + EPS)
        xhat = cen * rstd
        hout = xhat * lg_ref[...] + lb_ref[...]
        err = hout - tgt_ref[...]
        row_loss = jnp.mean(err * err, axis=-1, keepdims=True)
        loss_ref[...] += jnp.broadcast_to(0.5 * jnp.sum(row_loss, axis=0, keepdims=True), loss_ref.shape)

        dh = err * (1.0 / D_MODEL)
        dlg_ref[...] += jnp.sum(dh * xhat, axis=0, keepdims=True)
        dlb_ref[...] += jnp.sum(dh, axis=0, keepdims=True)
        dxhat = dh * lg_ref[...]
        dr = rstd * (dxhat - jnp.mean(dxhat, axis=-1, keepdims=True)
                     - xhat * jnp.mean(dxhat * xhat, axis=-1, keepdims=True))
        dr_ref[...] = dr
        dr_b = dr.astype(BF16)
        dwout_ref[...] += _dot_tn(merged, dr_b)
        dmerged = _dot_nt(dr_b, wout_ref[...])
        d_out_a = dmerged[:, :G_WIDTH]
        d_out_b = dmerged[:, G_WIDTH:]
        dattn = d_out_a * silu_a
        for h in range(HEADS):
            do_h = dattn[:, h * VDIM:(h + 1) * VDIM]
            dsum = jnp.sum(do_h * ol_ref[h][:, NOPE:], axis=-1, keepdims=True)
            drow_ref[h] = _as_row(dsum)
            do_ref[h] = jnp.concatenate([jnp.zeros((tm, NOPE), F32), do_h], axis=-1).astype(BF16)
        dza = d_out_a * attn * (sig_a * (1.0 + za * (1.0 - sig_a)))
        dsgu = d_out_b * silu_b
        dzb = d_out_b * sgu * (sig_b * (1.0 + zb * (1.0 - sig_b)))
        du = dsgu * sv * _gelu_grad(u)
        dsv = dsgu * ug
        dsv_b = dsv.astype(BF16)
        for cix in range(n_chunks):
            dbsp_acc[...] += dsv[cix * CHUNK:(cix + 1) * CHUNK, :]
        dsv_wide, vg_wide = side_by_side(dsv_b), side_by_side(vg_b)
        dvg = by_chunk(spatial(wst_ref, dsv_wide))
        for h in range(HEADS):
            mine = jnp.where(own_lanes(h), dsv_wide[h // 2], jnp.zeros_like(dsv_wide[h // 2]))
            dws_ref[h] += _dot_nt(mine, vg_wide[h // 2])
        dsg_ref[...] += jnp.sum(dvg * vhat, axis=0, keepdims=True)
        dsb_ref[...] += jnp.sum(dvg, axis=0, keepdims=True)
        dvhat = dvg * sg_ref[...]
        dgv = rstd_v * (dvhat - jnp.mean(dvhat, axis=-1, keepdims=True)
                        - vhat * jnp.mean(dvhat * vhat, axis=-1, keepdims=True))
        dv = dgv * _gelu_grad(vpre)
        drest_ref[...] = jnp.concatenate([dza, du, dv, dzb], axis=-1).astype(BF16)

        @pl.when(step == n_steps - 1)
        def _():
            tri = (lax.broadcasted_iota(jnp.int32, (CHUNK, CHUNK), 0)
                   >= lax.broadcasted_iota(jnp.int32, (CHUNK, CHUNK), 1))
            for h in range(HEADS):
                dws_ref[h] = jnp.where(tri, dws_ref[h], 0.0)
            tot = dbsp_acc[...]
            lane = lax.broadcasted_iota(jnp.int32, (CHUNK, LANES), 1)
            dbs = jnp.zeros((CHUNK, LANES), F32)
            for h in range(HEADS):
                head_sum = jnp.sum(tot[:, h * G_HEAD_DIM:(h + 1) * G_HEAD_DIM], axis=-1, keepdims=True)
                dbs = jnp.where(lane == h, head_sum, dbs)
            dbs_ref[...] = dbs

    full = lambda a: pl.BlockSpec(a.shape, lambda i: (0,) * a.ndim)
    tile = lambda w, j=0: pl.BlockSpec((tm, w), lambda i, j=j: (i, j))
    heads = pl.BlockSpec((HEADS, tm, HEAD_PAD), lambda i: (0, i, 0))
    acc = lambda shape: (pl.BlockSpec(shape, lambda i: (0,) * len(shape)), jax.ShapeDtypeStruct(shape, F32))
    accs = [acc((D_MODEL, D_MODEL)), acc((HEADS, CHUNK, CHUNK)), acc((CHUNK, LANES)), acc((1, D_MODEL)),
            acc((1, D_MODEL)), acc((1, G_WIDTH)), acc((1, G_WIDTH)), acc((1, LANES))]
    return pl.pallas_call(
        body, name="mid", grid=(n_steps,),
        in_specs=[tile(D_MODEL), tile(D_MODEL), tile(G_WIDTH, 1), tile(G_WIDTH, 2), tile(G_WIDTH, 3), tile(G_WIDTH, 4),
                  heads, full(w_out), full(ws_low), full(ws_low_t), full(bsp), full(sgu_g), full(sgu_b),
                  full(ln_g), full(ln_b)],
        out_specs=[tile(D_MODEL), heads, pl.BlockSpec((HEADS, 1, tm), lambda i: (0, 0, i)), tile(4 * G_WIDTH)]
        + [a[0] for a in accs],
        out_shape=[jax.ShapeDtypeStruct((t, D_MODEL), F32), jax.ShapeDtypeStruct((HEADS, t, HEAD_PAD), BF16),
                   jax.ShapeDtypeStruct((HEADS, 1, t), F32), jax.ShapeDtypeStruct((t, 4 * G_WIDTH), BF16)]
        + [a[1] for a in accs],
        scratch_shapes=[pltpu.VMEM((CHUNK, G_WIDTH), F32)],
        compiler_params=_cparams(("arbitrary",)),
    )(x, target, proj, proj, proj, proj, ol, w_out, ws_low, ws_low_t, bsp, sgu_g, sgu_b, ln_g, ln_b)


def _attn_bwd(q, k, v, do, lse_row, d_row):
    t = q.shape[1]
    bk, bq = ATTN_BWD_WIDE, ATTN_NARROW
    n_diag = bk // bq
    last = t // bq - 1
    chunk = SOFTMAX_ROWS

    def body(q_ref, k_ref, v_ref, do_ref, lse_ref, drow_ref, dqt_ref, dk_ref, dv_ref,
             s0, s1, e0, e1, p0, p1, g0, g1, kt_scr):
        j = pl.program_id(1)
        at = lambda i: pl.ds(pl.multiple_of(i * bq, bq), bq)

        @pl.when(j == 0)
        def _():
            dqt_ref[...] = jnp.zeros_like(dqt_ref)

        kt_scr[...] = jnp.transpose(k_ref[0].astype(F32)).astype(BF16)
        dk_ref[...] = jnp.zeros_like(dk_ref)
        dv_ref[...] = jnp.zeros_like(dv_ref)

        def products(i, s_out, e_out, keys=slice(0, bk)):
            i = jnp.minimum(i, last)
            s_out[keys, :] = _dot_nt(k_ref[0, keys, :], q_ref[0, at(i), :])
            e_out[keys, :] = _dot_nt(v_ref[0, keys, :], do_ref[0, at(i), :])

        def gradients(i, p_in, g_in, keys=slice(0, bk)):
            dv_ref[0, keys, :] += _dot(p_in[keys, :], do_ref[0, at(i), :])
            dk_ref[0, keys, :] += _dot(g_in[keys, :], q_ref[0, at(i), :])
            dqt_ref[0, :, at(i)] += _dot(kt_scr[:, keys], g_in[keys, :])

        def elementwise(i, s_in, e_in, p_out, g_out, qry0=None, keys=slice(0, bk)):
            lse = lse_ref[0, :, at(i)]
            dsum = drow_ref[0, :, at(i)]
            for r in range(keys.start, keys.stop, chunk):
                p = jnp.exp2(s_in[r:r + chunk, :] - lse)
                if qry0 is not None:
                    key = lax.broadcasted_iota(jnp.int32, (chunk, bq), 0) + r
                    qry = lax.broadcasted_iota(jnp.int32, (chunk, bq), 1) + qry0
                    p = jnp.where(qry >= key, p, 0.0)
                p_out[r:r + chunk, :] = p.astype(BF16)
                g_out[r:r + chunk, :] = (p * (e_in[r:r + chunk, :] - dsum)).astype(BF16)

        def one_pass(i, s_in, e_in, s_out, e_out, p_prev, g_prev, p_cur, g_cur):
            products(i + 1, s_out, e_out)
            gradients(i - 1, p_prev, g_prev)
            elementwise(i, s_in, e_in, p_cur, g_cur)

        first = n_diag * j
        keys_of = lambda u: slice(0, min((u + 1) * bq, bk))
        even, odd = (s0, e0, p0, g0), (s1, e1, p1, g1)
        products(first, s0, e0, keys_of(0))
        products(first + 1, s1, e1, keys_of(1))
        elementwise(first, s0, e0, p0, g0, qry0=0, keys=keys_of(0))
        for u in range(1, n_diag):
            (s_in, e_in, p_cur, g_cur), (s_out, e_out, p_prev, g_prev) = (odd, even) if u % 2 else (even, odd)
            products(first + u + 1, s_out, e_out, keys_of(u + 1))
            gradients(first + u - 1, p_prev, g_prev, keys_of(u - 1))
            elementwise(first + u, s_in, e_in, p_cur, g_cur, qry0=u * bq, keys=keys_of(u))

        def two_passes(n, _):
            i = first + n_diag + 2 * n
            one_pass(i, s0, e0, s1, e1, p1, g1, p0, g0)
            one_pass(i + 1, s1, e1, s0, e0, p0, g0, p1, g1)
            return 0

        lax.fori_loop(0, (last - first - n_diag + 1) // 2, two_passes, 0)
        gradients(last, p1, g1)
        dk_ref[0] = dk_ref[0] * LN2

    whole = pl.BlockSpec((1, t, HEAD_PAD), lambda h, j: (h, 0, 0), pipeline_mode=pl.Buffered(1))
    block = pl.BlockSpec((1, bk, HEAD_PAD), lambda h, j: (h, j, 0))
    rows = pl.BlockSpec((1, 1, t), lambda h, j: (h, 0, 0), pipeline_mode=pl.Buffered(1))
    shape = jax.ShapeDtypeStruct((HEADS, t, HEAD_PAD), F32)
    tile = lambda dtype: pltpu.VMEM((bk, bq), dtype)
    return pl.pallas_call(
        body, name="attn_bwd", grid=(HEADS, t // bk),
        in_specs=[whole, block, block, whole, rows, rows],
        out_specs=[pl.BlockSpec((1, HEAD_PAD, t), lambda h, j: (h, 0, 0)), block, block],
        out_shape=[jax.ShapeDtypeStruct((HEADS, HEAD_PAD, t), F32), shape, shape],
        scratch_shapes=[tile(F32), tile(F32), tile(F32), tile(F32), tile(BF16), tile(BF16),
                        tile(BF16), tile(BF16), pltpu.VMEM((HEAD_PAD, bk), BF16)],
        compiler_params=_cparams(("arbitrary", "arbitrary")),
    )(q, k, v, do, lse_row, d_row)


def _bwd_qkv(dq, dk, dv, proj, pos_col, invf_row, w_heads, q_g, kv_g):
    t = proj.shape[0]
    tm = TOKEN_TILE

    def body(dq_ref, dk_ref, dv_ref, ph_ref, pos_ref, invf_ref, wh_ref, qg_ref, kvg_ref,
             dhead_ref, dwh_ref, dqg_ref, dkvg_ref):
        @pl.when(pl.program_id(0) == 0)
        def _():
            dwh_ref[...] = jnp.zeros_like(dwh_ref)
            dqg_ref[...] = jnp.zeros_like(dqg_ref)
            dkvg_ref[...] = jnp.zeros_like(dkvg_ref)

        cos, s1, s2 = _rope_tables(pos_ref[...], invf_ref[...])
        lane = lax.broadcasted_iota(jnp.int32, (tm, LANES), 1)
        c_q = ph_ref[:, :Q_LORA]
        c_kv = ph_ref[:, Q_LORA:Q_LORA + KV_LORA]
        rstd_q = lax.rsqrt(jnp.mean(c_q * c_q, axis=-1, keepdims=True) + EPS)
        rstd_kv = lax.rsqrt(jnp.mean(c_kv * c_kv, axis=-1, keepdims=True) + EPS)
        qhat = c_q * rstd_q
        kvhat = c_kv * rstd_kv
        cqn = (qhat * qg_ref[...]).astype(BF16)
        ckvn = (kvhat * kvg_ref[...]).astype(BF16)
        dcqn = jnp.zeros((tm, Q_LORA), F32)
        dckvn = jnp.zeros((tm, KV_LORA), F32)
        dkr_rot = jnp.zeros((tm, LANES), F32)
        for h in range(HEADS):
            dq_b = _rope(jnp.transpose(dq_ref[h]) * ATTN_SCALE, cos, s1, s2, -1.0).astype(BF16)
            dk_h = dk_ref[h]
            dkv_b = jnp.where(lane < NOPE, dk_h, dv_ref[h]).astype(BF16)
            dkr_rot = dkr_rot + dk_h
            dwh_ref[h, :Q_LORA, :] += _dot_tn(cqn, dq_b)
            dwh_ref[h, Q_LORA:, :] += _dot_tn(ckvn, dkv_b)
            dcqn = dcqn + _dot_nt(dq_b, wh_ref[h, :Q_LORA, :])
            dckvn = dckvn + _dot_nt(dkv_b, wh_ref[h, Q_LORA:, :])
        rot_lanes = (lane >= KR_LO) & (lane < KR_LO + ROPE)
        dkr_raw = jnp.where(rot_lanes, _rope(dkr_rot, cos, s1, s2, -1.0), 0.0)
        dqg_ref[...] += jnp.sum(dcqn * qhat, axis=0, keepdims=True)
        dkvg_ref[...] += jnp.sum(dckvn * kvhat, axis=0, keepdims=True)
        dqh = dcqn * qg_ref[...]
        dkvh = dckvn * kvg_ref[...]
        dc_q = rstd_q * (dqh - qhat * jnp.mean(dqh * qhat, axis=-1, keepdims=True))
        dc_kv = rstd_kv * (dkvh - kvhat * jnp.mean(dkvh * kvhat, axis=-1, keepdims=True))
        dhead_ref[...] = jnp.concatenate([dc_q, dc_kv, dkr_raw], axis=-1).astype(BF16)

    full = lambda a: pl.BlockSpec(a.shape, lambda i: (0,) * a.ndim)
    heads = pl.BlockSpec((HEADS, tm, HEAD_PAD), lambda i: (0, i, 0))
    acc = lambda shape: (pl.BlockSpec(shape, lambda i: (0,) * len(shape)), jax.ShapeDtypeStruct(shape, F32))
    accs = [acc(w_heads.shape), acc((1, Q_LORA)), acc((1, KV_LORA))]
    return pl.pallas_call(
        body, name="bwd_qkv", grid=(t // tm,),
        in_specs=[pl.BlockSpec((HEADS, HEAD_PAD, tm), lambda i: (0, 0, i)), heads, heads,
                  pl.BlockSpec((tm, 4 * LANES), lambda i: (i, 0)),
                  pl.BlockSpec((tm, 1), lambda i: (i, 0)), full(invf_row), full(w_heads),
                  full(q_g), full(kv_g)],
        out_specs=[pl.BlockSpec((tm, 4 * LANES), lambda i: (i, 0))] + [a[0] for a in accs],
        out_shape=[jax.ShapeDtypeStruct((t, 4 * LANES), BF16)] + [a[1] for a in accs],
        compiler_params=_cparams(("arbitrary",)),
    )(dq, dk, dv, proj, pos_col, invf_row, w_heads, q_g, kv_g)


def _bwd_in(x, dr, dhead, drest, wp_in):
    t = x.shape[0]
    tm = TOKEN_TILE
    n_head = dhead.shape[1]

    def body(x_ref, dr_ref, dhead_ref, drest_ref, win_ref, gx_ref, dwin_ref):
        @pl.when(pl.program_id(0) == 0)
        def _():
            dwin_ref[...] = jnp.zeros_like(dwin_ref)

        xb = x_ref[...].astype(BF16)
        dh_b = dhead_ref[...]
        dr_b = drest_ref[...]
        gx_ref[...] = (DN_ALPHA * dr_ref[...] + _dot_nt(dh_b, win_ref[:, :n_head])
                       + _dot_nt(dr_b, win_ref[:, n_head:]))
        dwin_ref[:, :n_head] += _dot_tn(xb, dh_b)
        dwin_ref[:, n_head:] += _dot_tn(xb, dr_b)

    tile = lambda w: pl.BlockSpec((tm, w), lambda i: (i, 0))
    whole = pl.BlockSpec(wp_in.shape, lambda i: (0, 0))
    return pl.pallas_call(
        body, name="bwd_in", grid=(t // tm,),
        in_specs=[tile(D_MODEL), tile(D_MODEL), tile(n_head), tile(drest.shape[1]), whole],
        out_specs=[tile(D_MODEL), whole],
        out_shape=[jax.ShapeDtypeStruct((t, D_MODEL), F32), jax.ShapeDtypeStruct(wp_in.shape, F32)],
        compiler_params=_cparams(("arbitrary",)),
    )(x, dr, dhead, drest, wp_in)


def _adam(parts, w, m, v, *, name, tile_rows):
    n, rows, cols = parts.shape

    def body(p_ref, w_ref, m_ref, v_ref, g_ref, d_ref, nm_ref, nv_ref):
        g = p_ref[0].astype(F32)
        for s in range(1, n):
            g = g + p_ref[s].astype(F32)
        m_new = ADAM_B1 * m_ref[...] + (1.0 - ADAM_B1) * g
        v_new = ADAM_B2 * v_ref[...] + (1.0 - ADAM_B2) * (g * g)
        m_hat = m_new / (1.0 - ADAM_B1 ** ADAM_STEP)
        v_hat = v_new / (1.0 - ADAM_B2 ** ADAM_STEP)
        g_ref[...] = g
        d_ref[...] = -ADAM_LR * (m_hat / (jnp.sqrt(v_hat) + ADAM_EPS) + ADAM_WD * w_ref[...])
        nm_ref[...] = m_new
        nv_ref[...] = v_new

    flat = pl.BlockSpec((tile_rows, cols), lambda i: (i, 0))
    shape = jax.ShapeDtypeStruct((rows, cols), F32)
    return pl.pallas_call(
        body, name=name, grid=(rows // tile_rows,),
        in_specs=[pl.BlockSpec((n, tile_rows, cols), lambda i: (0, i, 0)), flat, flat, flat],
        out_specs=[flat] * 4, out_shape=[shape] * 4,
        compiler_params=_cparams(("arbitrary",)),
    )(parts, w, m, v)


SMALL_NAMES = ("q_norm_g", "kv_norm_g", "sgu_norm_g", "sgu_norm_b", "b_spatial", "ln_g", "ln_b")
SMALL_SIZES = (Q_LORA, KV_LORA, G_WIDTH, G_WIDTH, HEADS * CHUNK, D_MODEL, D_MODEL)


def _pack_small(vals, last=None):
    flat = jnp.concatenate([v.reshape(-1) for v in vals])
    pad = SMALL_LEN - flat.shape[0]
    if last is None:
        return jnp.pad(flat, (0, pad))
    return jnp.concatenate([flat, jnp.zeros((pad - 1,), F32), last.reshape(1)])


def _unpack_small(flat):
    out, at = [], 0
    for n in SMALL_SIZES:
        out.append(flat[at:at + n])
        at += n
    out[4] = out[4].reshape(HEADS, CHUNK)
    return out


UQ_SHARD = HEADS * (NOPE + ROPE) // N_DEV
HEAD_ROWS = Q_LORA + KV_LORA
MIXED_ROWS = HEAD_ROWS + CHUNK + SMALL_LEN // N_DEV // LANES


def _head_slab(w_uq_shard, w_ukv_shard):
    return jnp.concatenate([jnp.pad(w_uq_shard, ((0, 0), (0, LANES - UQ_SHARD))), w_ukv_shard])


IN_SHARD = D_IN // N_DEV


def _w_in_pieces():
    split = Q_LORA + KV_LORA
    moves = ((0, split, 0), (split, split + ROPE, KR_LO), (split + ROPE, D_IN, LANES - ROPE))
    pieces = []
    for s in range(N_DEV):
        lo, hi = s * IN_SHARD, (s + 1) * IN_SHARD
        for a, b, shift in moves:
            a, b = max(a, lo), min(b, hi)
            if a < b:
                pieces.append((s, a - lo, a + shift, b - a))
    return pieces


def _padded_w_in(shards):
    tr = TOKEN_TILE

    def body(sh_ref, o_ref):
        o_ref[...] = jnp.zeros_like(o_ref)
        for s, src, dst, width in _w_in_pieces():
            o_ref[:, dst:dst + width] = sh_ref[s, :, src:src + width]

    return pl.pallas_call(
        body, name="w_in_pad", grid=(D_MODEL // tr,),
        in_specs=[pl.BlockSpec((N_DEV, tr, IN_SHARD), lambda i: (0, i, 0))],
        out_specs=pl.BlockSpec((tr, D_IN_PAD), lambda i: (i, 0)),
        out_shape=jax.ShapeDtypeStruct((D_MODEL, D_IN_PAD), shards.dtype),
        compiler_params=_cparams(("arbitrary",)),
    )(shards)


def _w_in_shards(dwp_in):
    tr = TOKEN_TILE
    by_shard = [[p for p in _w_in_pieces() if p[0] == s] for s in range(N_DEV)]

    def body(w_ref, o_ref):
        for s, pieces in enumerate(by_shard):
            parts = [w_ref[:, dst:dst + width] for _, _, dst, width in pieces]
            o_ref[s] = parts[0] if len(parts) == 1 else jnp.concatenate(parts, axis=1)

    return pl.pallas_call(
        body, name="w_in_split", grid=(D_MODEL // tr,),
        in_specs=[pl.BlockSpec((tr, D_IN_PAD), lambda i: (i, 0))],
        out_specs=pl.BlockSpec((N_DEV, tr, IN_SHARD), lambda i: (0, i, 0)),
        out_shape=jax.ShapeDtypeStruct((N_DEV, D_MODEL, IN_SHARD), dwp_in.dtype),
        compiler_params=_cparams(("arbitrary",)),
    )(dwp_in)


def kernel(x, positions, w_in, q_norm_g, w_uq, kv_norm_g, w_ukv, sgu_norm_g, sgu_norm_b, w_spatial, b_spatial, w_out, ln_g, ln_b, loss_target, m_w_in, m_q_norm_g, m_w_uq, m_kv_norm_g, m_w_ukv, m_sgu_norm_g, m_sgu_norm_b, m_w_spatial, m_b_spatial, m_w_out, m_ln_g, m_ln_b, v_w_in, v_q_norm_g, v_w_uq, v_kv_norm_g, v_w_ukv, v_sgu_norm_g, v_sgu_norm_b, v_w_spatial, v_b_spatial, v_w_out, v_ln_g, v_ln_b):
    me = 4 * lax.axis_index("x") + 2 * lax.axis_index("y") + lax.axis_index("c")
    seq = x.shape[1]
    x2 = x.reshape(seq, D_MODEL)
    tgt2 = loss_target.reshape(seq, D_MODEL)
    pos_col = positions.reshape(seq, 1)

    w_in_shards, w_out_shards, w_heads = _gather_two_level(
        [w_in.astype(BF16), w_out.astype(BF16), _head_slab(w_uq, w_ukv).astype(BF16)],
        name="wgather")
    (loss_part, grad_x, d_in, d_heads, d_out, d_ws, d_bs_t, d_lng, d_lnb, d_sgug, d_sgub, d_qg, d_kvg) = _local_step(
        x2, tgt2, pos_col, w_in_shards, w_heads, w_out_shards.reshape(D_MODEL, D_MODEL), q_norm_g, kv_norm_g,
        sgu_norm_g, sgu_norm_b, w_spatial, b_spatial, ln_g, ln_b)

    small_part = _pack_small([d_qg, d_kvg, d_sgug, d_sgub, d_bs_t[:, :HEADS].T, d_lng, d_lnb], last=loss_part[0, :1])
    mixed = jnp.concatenate([d_heads, d_ws, small_part.reshape(N_DEV, -1, LANES)], axis=1)
    by_chip = [g.reshape((N_CHIPS, 2) + g.shape[1:])
               for g in (d_in, d_out.reshape(N_DEV, D_MODEL // N_DEV, D_MODEL), mixed)]
    from_sibling = _sibling_swap(by_chip, name="gswap")
    core = lax.axis_index("c").astype(jnp.int32).reshape(1)
    pair_sums = [_pair_sum(a, b, core, name=nm, tile_rows=tr, out_dtype=dt) for a, b, nm, tr, dt in zip(
        by_chip, from_sibling, ("gsum_in", "gsum_out", "gsum_mixed"), (TOKEN_TILE, D_MODEL // N_DEV, MIXED_ROWS),
        (BF16, BF16, F32))]
    recv_in, recv_out, recv_mixed = _chip_exchange(pair_sums, name="gexch")

    take = lambda a: lax.dynamic_index_in_dim(a, me, 0, keepdims=False)
    small_w = _pack_small([q_norm_g, kv_norm_g, sgu_norm_g, sgu_norm_b, b_spatial, ln_g, ln_b])
    small_m = _pack_small([m_q_norm_g, m_kv_norm_g, m_sgu_norm_g, m_sgu_norm_b, m_b_spatial, m_ln_g, m_ln_b])
    small_v = _pack_small([v_q_norm_g, v_kv_norm_g, v_sgu_norm_g, v_sgu_norm_b, v_b_spatial, v_ln_g, v_ln_b])
    own_mixed = lambda uq, ukv, sp, small: jnp.concatenate(
        [_head_slab(uq, ukv), take(sp), take(small.reshape(N_DEV, -1, LANES))])
    res_in = _adam(recv_in, w_in, m_w_in, v_w_in, name="adam_in", tile_rows=TOKEN_TILE)
    res_out = _adam(recv_out, w_out, m_w_out, v_w_out, name="adam_out", tile_rows=D_MODEL // N_DEV)
    res_mixed = _adam(recv_mixed, own_mixed(w_uq, w_ukv, w_spatial, small_w), own_mixed(m_w_uq, m_w_ukv, m_w_spatial, small_m),
                      own_mixed(v_w_uq, v_w_ukv, v_w_spatial, small_v), name="adam_mixed", tile_rows=MIXED_ROWS)

    rep_g, = _exchange([res_mixed[0][HEAD_ROWS:]], name="sgather", per_destination=False)
    rep_pack = lambda sp, small: jnp.concatenate(
        [sp.reshape(N_DEV, CHUNK, LANES), small.reshape(N_DEV, -1, LANES)], axis=1).reshape(-1, LANES)
    _, delta_rep, m_rep, v_rep = _adam(rep_g.reshape(1, N_DEV * REP_ROWS, LANES), rep_pack(w_spatial, small_w),
                                       rep_pack(m_w_spatial, small_m), rep_pack(v_w_spatial, small_v),
                                       name="adam_rep", tile_rows=N_DEV * REP_ROWS)

    def rep_unpack(a):
        a = a.reshape(N_DEV, REP_ROWS, LANES)
        small = _unpack_small(a[:, CHUNK:].reshape(-1))
        return [small[0], small[1], small[2], small[3], a[:, :CHUNK], small[4], small[5], small[6]]

    def ordered(which, rep):
        r_qg, r_kvg, r_sg, r_sb, r_ws, r_bs, r_lg, r_lb = rep_unpack(rep)
        heads = res_mixed[which]
        return [res_in[which], r_qg, heads[:Q_LORA, :UQ_SHARD], r_kvg, heads[Q_LORA:HEAD_ROWS], r_sg, r_sb, r_ws, r_bs,
                res_out[which], r_lg, r_lb]

    loss = rep_g[N_DEV - 1, REP_ROWS - 1, LANES - 1]
    outs = [loss, grad_x.reshape(x.shape)]
    outs += ordered(0, rep_g.reshape(-1, LANES))
    outs += ordered(1, delta_rep)
    outs += ordered(2, m_rep)
    outs += ordered(3, v_rep)
    return tuple(outs)


def _local_step(x2, tgt2, pos_col, w_in_shards, w_heads, w_out_full, q_norm_g, kv_norm_g, sgu_norm_g, sgu_norm_b,
                w_spatial, b_spatial, ln_g, ln_b):
    wp_in = _padded_w_in(w_in_shards)

    half = jnp.arange(HALF, dtype=F32)
    inv_freq = 1.0 / (ROPE_THETA ** (half / HALF))
    invf_row = jnp.concatenate([jnp.zeros((KR_LO,), F32), inv_freq, inv_freq,
                                jnp.zeros((LANES - KR_LO - ROPE,), F32)]).reshape(1, LANES)
    tri = jnp.tril(jnp.ones((CHUNK, CHUNK), dtype=bool))
    ws_low = jnp.where(tri[None], w_spatial, 0.0).astype(BF16)
    ws_low_t = ws_low.transpose(0, 2, 1)
    bsp = jnp.repeat(b_spatial.T, G_HEAD_DIM, axis=1)
    row = lambda a: a.reshape(1, -1)

    proj, q, k, v, vt = _fwd_proj(x2, pos_col, invf_row, wp_in, w_heads, row(q_norm_g), row(kv_norm_g))
    o, lse_row = _attn_fwd(q, k, vt)
    (dr, do, d_row, drest, d_out, d_ws, d_bs_t, d_lng, d_lnb, d_sgug, d_sgub, loss_part) = _mid(
        x2, tgt2, proj, o, w_out_full, ws_low, ws_low_t, bsp, row(sgu_norm_g), row(sgu_norm_b), row(ln_g), row(ln_b))
    dqt, dk, dv = _attn_bwd(q, k, v, do, lse_row, d_row)
    dhead, d_heads, d_qg, d_kvg = _bwd_qkv(dqt, dk, dv, proj, pos_col, invf_row, w_heads, row(q_norm_g), row(kv_norm_g))
    grad_x, dwp_in = _bwd_in(x2, dr, dhead, drest, wp_in)
    return (loss_part, grad_x, _w_in_shards(dwp_in), d_heads, d_out, d_ws, d_bs_t, d_lng, d_lnb, d_sgug, d_sgub,
            d_qg, d_kvg)
```

```python
import functools
import math

import jax
import jax.numpy as jnp
from jax import lax
from jax.experimental import pallas as pl
from jax.experimental.pallas import tpu as pltpu

F32 = jnp.float32
BF16 = jnp.bfloat16

N_DEV = 8
D_MODEL = 1024
HEADS = 8
NOPE = 64
ROPE = 32
HALF = ROPE // 2
VDIM = 64
Q_LORA = 256
KV_LORA = 128
G_WIDTH = 512
G_HEAD_DIM = 64
CHUNK = 128
HEAD_PAD = 128
D_IN = 2464
D_IN_PAD = 2560
KR_LO = NOPE
ROPE_THETA = 10000.0
DN_ALPHA = 2.0 ** 0.25
EPS = 1e-5
ATTN_SCALE = 1.0 / math.sqrt(NOPE + ROPE)
ADAM_LR, ADAM_B1, ADAM_B2, ADAM_EPS, ADAM_WD, ADAM_STEP = 0.001, 0.9, 0.999, 1e-08, 0.01, 10

LANES = 128
REP_ROWS = 136
SMALL_LEN = 8192
VMEM_LIMIT = 56 * 1024 * 1024

TOKEN_TILE = 256
ATTN_FWD_WIDE = 2048
ATTN_BWD_WIDE = 2048
ATTN_NARROW = 512
SOFTMAX_ROWS = 256
LOG2E = 1.4426950408889634
LN2 = 0.6931471805599453
Q_PRESCALE = ATTN_SCALE * LOG2E


def _cparams(sem=None):
    return pltpu.CompilerParams(dimension_semantics=sem, vmem_limit_bytes=VMEM_LIMIT)


def _dot(a, b):
    return jnp.dot(a, b, preferred_element_type=F32)


def _dot_nt(a, b):
    return lax.dot_general(a, b, (((1,), (1,)), ((), ())), preferred_element_type=F32)


def _dot_tn(a, b):
    return lax.dot_general(a, b, (((0,), (0,)), ((), ())), preferred_element_type=F32)


def _as_row(col):
    return jnp.transpose(jnp.broadcast_to(col, (col.shape[0], LANES)))[0:1, :]


def _sigmoid(z):
    return 1.0 / (1.0 + jnp.exp(-z))


def _gelu(x):
    return 0.5 * x * (1.0 + lax.erf(x * 0.7071067811865476))


def _gelu_grad(x):
    cdf = 0.5 * (1.0 + lax.erf(x * 0.7071067811865476))
    return cdf + x * jnp.exp(-0.5 * x * x) * 0.3989422804014327


def _exchange(srcs, *, name, per_destination):
    n = len(srcs)
    slab_shapes = [s.shape[1:] if per_destination else s.shape for s in srcs]

    def body(*refs):
        src_refs, out_refs = refs[:n], refs[n:2 * n]
        send_sems, recv_sems, local_sems = refs[2 * n:]
        x, y, c = lax.axis_index("x"), lax.axis_index("y"), lax.axis_index("c")
        me = 4 * x + 2 * y + c

        def slab_for(t, dest):
            return src_refs[t].at[dest] if per_destination else src_refs[t]

        mine = [pltpu.make_async_copy(slab_for(t, me), out_refs[t].at[me], local_sems.at[t]) for t in range(n)]
        for cp in mine:
            cp.start()
        sends, arrivals = [], []
        for k in (6, 7, 4, 5, 2, 3, 1):
            px = 1 - x if k & 4 else x
            py = 1 - y if k & 2 else y
            pc = 1 - c if k & 1 else c
            peer = 4 * px + 2 * py + pc
            for t in range(n):
                sem = (k - 1) * n + t
                cp = pltpu.make_async_remote_copy(
                    src_ref=slab_for(t, peer), dst_ref=out_refs[t].at[me],
                    send_sem=send_sems.at[sem], recv_sem=recv_sems.at[sem],
                    device_id=(px, py, pc), device_id_type=pl.DeviceIdType.MESH)
                cp.start()
                sends.append(cp)
                arrivals.append(pltpu.make_async_remote_copy(
                    src_ref=slab_for(t, peer), dst_ref=out_refs[t].at[peer],
                    send_sem=send_sems.at[sem], recv_sem=recv_sems.at[sem],
                    device_id=(x, y, c), device_id_type=pl.DeviceIdType.MESH))
        for cp in arrivals:
            cp.wait_recv()
        for cp in sends:
            cp.wait_send()
        for cp in mine:
            cp.wait()

    hbm = pl.BlockSpec(memory_space=pl.ANY)
    return pl.pallas_call(
        body, name=name,
        out_shape=[jax.ShapeDtypeStruct((N_DEV,) + tuple(shape), s.dtype) for shape, s in zip(slab_shapes, srcs)],
        in_specs=[hbm] * n, out_specs=[hbm] * n,
        scratch_shapes=[pltpu.SemaphoreType.DMA(((N_DEV - 1) * n,)), pltpu.SemaphoreType.DMA(((N_DEV - 1) * n,)),
                        pltpu.SemaphoreType.DMA((n,))],
    )(*srcs)


def _gather_two_level(srcs, *, name):
    n = len(srcs)

    def body(*refs):
        src_refs, out_refs = refs[:n], refs[n:2 * n]
        send_sems, recv_sems, local_sems = refs[2 * n:]
        x, y, c = lax.axis_index("x"), lax.axis_index("y"), lax.axis_index("c")
        me, sibling = (x, y, c), (x, y, 1 - c)
        chips = [(1 - x, 1 - y), (1 - x, y), (x, 1 - y)]
        index = lambda px, py, pc: 4 * px + 2 * py + pc

        def copy(k, t, block, to, src=None):
            place = out_refs[t].at[index(*block)]
            return pltpu.make_async_remote_copy(
                src_ref=place if src is None else src, dst_ref=place,
                send_sem=send_sems.at[k * n + t], recv_sem=recv_sems.at[k * n + t],
                device_id=to, device_id_type=pl.DeviceIdType.MESH)

        mine = [pltpu.make_async_copy(src_refs[t], out_refs[t].at[index(*me)], local_sems.at[t]) for t in range(n)]
        for cp in mine:
            cp.start()
        first = [copy(1 + j, t, me, (*chip, c), src=src_refs[t]) for j, chip in enumerate(chips) for t in range(n)]
        first += [copy(0, t, me, sibling, src=src_refs[t]) for t in range(n)]
        for cp in first:
            cp.start()
        passed = []
        for j, chip in enumerate(chips):
            for t in range(n):
                copy(1 + j, t, (*chip, c), me).wait_recv()
                cp = copy(4 + j, t, (*chip, c), sibling)
                cp.start()
                passed.append(cp)
        for t in range(n):
            copy(0, t, sibling, me).wait_recv()
        for j, chip in enumerate(chips):
            for t in range(n):
                copy(4 + j, t, (*chip, 1 - c), me).wait_recv()
        for cp in first + passed:
            cp.wait_send()
        for cp in mine:
            cp.wait()

    hbm = pl.BlockSpec(memory_space=pl.ANY)
    return pl.pallas_call(
        body, name=name,
        out_shape=[jax.ShapeDtypeStruct((N_DEV,) + s.shape, s.dtype) for s in srcs],
        in_specs=[hbm] * n, out_specs=[hbm] * n,
        scratch_shapes=[pltpu.SemaphoreType.DMA((7 * n,)), pltpu.SemaphoreType.DMA((7 * n,)),
                        pltpu.SemaphoreType.DMA((n,))],
    )(*srcs)


N_CHIPS = N_DEV // 2


def _sibling_swap(srcs, *, name):
    n = len(srcs)

    def body(*refs):
        src_refs, out_refs = refs[:n], refs[n:2 * n]
        send_sems, recv_sems = refs[2 * n:]
        x, y, c = lax.axis_index("x"), lax.axis_index("y"), lax.axis_index("c")
        sends = []
        for chip in range(N_CHIPS):
            for t in range(n):
                cp = pltpu.make_async_remote_copy(
                    src_ref=src_refs[t].at[chip, 1 - c], dst_ref=out_refs[t].at[chip],
                    send_sem=send_sems.at[chip * n + t], recv_sem=recv_sems.at[chip * n + t],
                    device_id=(x, y, 1 - c), device_id_type=pl.DeviceIdType.MESH)
                cp.start()
                sends.append(cp)
        for cp in sends:
            cp.wait_recv()
        for cp in sends:
            cp.wait_send()

    hbm = pl.BlockSpec(memory_space=pl.ANY)
    return pl.pallas_call(
        body, name=name,
        out_shape=[jax.ShapeDtypeStruct((N_CHIPS,) + s.shape[2:], s.dtype) for s in srcs],
        in_specs=[hbm] * n, out_specs=[hbm] * n,
        scratch_shapes=[pltpu.SemaphoreType.DMA((N_CHIPS * n,)), pltpu.SemaphoreType.DMA((N_CHIPS * n,))],
    )(*srcs)


def _pair_sum(mine, theirs, core, *, name, tile_rows, out_dtype):
    _, _, rows, cols = mine.shape

    def body(core_ref, a_ref, b_ref, o_ref):
        o_ref[...] = (a_ref[0] + b_ref[...]).astype(out_dtype)

    return pl.pallas_call(
        body, name=name,
        grid_spec=pltpu.PrefetchScalarGridSpec(
            num_scalar_prefetch=1, grid=(N_CHIPS, rows // tile_rows),
            in_specs=[pl.BlockSpec((1, 1, tile_rows, cols), lambda q, r, core_ref: (q, core_ref[0], r, 0)),
                      pl.BlockSpec((1, tile_rows, cols), lambda q, r, core_ref: (q, r, 0))],
            out_specs=pl.BlockSpec((1, tile_rows, cols), lambda q, r, core_ref: (q, r, 0))),
        out_shape=jax.ShapeDtypeStruct((N_CHIPS, rows, cols), out_dtype),
        compiler_params=_cparams(("arbitrary", "arbitrary")),
    )(core, mine, theirs)


def _chip_exchange(srcs, *, name):
    n = len(srcs)

    def body(*refs):
        src_refs, out_refs = refs[:n], refs[n:2 * n]
        send_sems, recv_sems, local_sems = refs[2 * n:]
        x, y, c = lax.axis_index("x"), lax.axis_index("y"), lax.axis_index("c")
        my_chip = 2 * x + y
        mine = [pltpu.make_async_copy(src_refs[t].at[my_chip], out_refs[t].at[my_chip], local_sems.at[t])
                for t in range(n)]
        for cp in mine:
            cp.start()
        sends, arrivals = [], []
        for k in (3, 2, 1):
            px = 1 - x if k & 2 else x
            py = 1 - y if k & 1 else y
            peer_chip = 2 * px + py
            for t in range(n):
                sem = (k - 1) * n + t
                cp = pltpu.make_async_remote_copy(
                    src_ref=src_refs[t].at[peer_chip], dst_ref=out_refs[t].at[my_chip],
                    send_sem=send_sems.at[sem], recv_sem=recv_sems.at[sem],
                    device_id=(px, py, c), device_id_type=pl.DeviceIdType.MESH)
                cp.start()
                sends.append(cp)
                arrivals.append(pltpu.make_async_remote_copy(
                    src_ref=src_refs[t].at[peer_chip], dst_ref=out_refs[t].at[peer_chip],
                    send_sem=send_sems.at[sem], recv_sem=recv_sems.at[sem],
                    device_id=(x, y, c), device_id_type=pl.DeviceIdType.MESH))
        for cp in arrivals:
            cp.wait_recv()
        for cp in sends:
            cp.wait_send()
        for cp in mine:
            cp.wait()

    hbm = pl.BlockSpec(memory_space=pl.ANY)
    return pl.pallas_call(
        body, name=name,
        out_shape=[jax.ShapeDtypeStruct(s.shape, s.dtype) for s in srcs],
        in_specs=[hbm] * n, out_specs=[hbm] * n,
        scratch_shapes=[pltpu.SemaphoreType.DMA((3 * n,)), pltpu.SemaphoreType.DMA((3 * n,)),
                        pltpu.SemaphoreType.DMA((n,))],
    )(*srcs)


def _rope_tables(pos_col, invf_row):
    ang = pos_col.astype(F32) * invf_row
    lane = lax.broadcasted_iota(jnp.int32, ang.shape, 1)
    cos, sin = jnp.cos(ang), jnp.sin(ang)
    first = (lane >= KR_LO) & (lane < KR_LO + HALF)
    second = (lane >= KR_LO + HALF) & (lane < KR_LO + ROPE)
    return cos, jnp.where(first, sin, 0.0), jnp.where(second, sin, 0.0)


def _rope(t, cos, sin_first, sin_second, sign):
    up = pltpu.roll(t, LANES - HALF, 1)
    down = pltpu.roll(t, HALF, 1)
    return t * cos - sign * (up * sin_first) + sign * (down * sin_second)


def _fwd_proj(x, pos_col, invf_row, wp_in, w_heads, q_g, kv_g):
    t = x.shape[0]
    tm = TOKEN_TILE

    def body(x_ref, pos_ref, invf_ref, win_ref, wh_ref, qg_ref, kvg_ref,
             proj_ref, q_ref, k_ref, v_ref, vt_ref):
        proj = _dot(x_ref[...].astype(BF16), win_ref[...])
        proj_ref[...] = proj
        c_q = proj[:, :Q_LORA]
        c_kv = proj[:, Q_LORA:Q_LORA + KV_LORA]
        kr_raw = proj[:, Q_LORA + KV_LORA:Q_LORA + KV_LORA + LANES]
        cqn = (c_q * lax.rsqrt(jnp.mean(c_q * c_q, axis=-1, keepdims=True) + EPS) * qg_ref[...]).astype(BF16)
        ckvn = (c_kv * lax.rsqrt(jnp.mean(c_kv * c_kv, axis=-1, keepdims=True) + EPS) * kvg_ref[...]).astype(BF16)
        cos, s1, s2 = _rope_tables(pos_ref[...], invf_ref[...])
        kr = _rope(kr_raw, cos, s1, s2, 1.0)
        lane = lax.broadcasted_iota(jnp.int32, (tm, HEAD_PAD), 1)
        for h in range(HEADS):
            q_h = _dot(cqn, wh_ref[h, :Q_LORA, :])
            kv_h = _dot(ckvn, wh_ref[h, Q_LORA:, :])
            q_ref[h] = (_rope(q_h, cos, s1, s2, 1.0) * Q_PRESCALE).astype(BF16)
            k_ref[h] = jnp.where(lane < NOPE, kv_h, kr).astype(BF16)
            v_ref[h] = kv_h.astype(BF16)
            vt_ref[h] = jnp.transpose(kv_h).astype(BF16)

    full = lambda a: pl.BlockSpec(a.shape, lambda i: (0,) * a.ndim)
    head_spec = pl.BlockSpec((HEADS, tm, HEAD_PAD), lambda i: (0, i, 0))
    head_shape = jax.ShapeDtypeStruct((HEADS, t, HEAD_PAD), BF16)
    return pl.pallas_call(
        body, name="fwd_proj", grid=(t // tm,),
        in_specs=[pl.BlockSpec((tm, D_MODEL), lambda i: (i, 0)), pl.BlockSpec((tm, 1), lambda i: (i, 0)),
                  full(invf_row), full(wp_in), full(w_heads), full(q_g), full(kv_g)],
        out_specs=[pl.BlockSpec((tm, D_IN_PAD), lambda i: (i, 0)), head_spec, head_spec, head_spec,
                   pl.BlockSpec((HEADS, HEAD_PAD, tm), lambda i: (0, 0, i))],
        out_shape=[jax.ShapeDtypeStruct((t, D_IN_PAD), F32), head_shape, head_shape, head_shape,
                   jax.ShapeDtypeStruct((HEADS, HEAD_PAD, t), BF16)],
        compiler_params=_cparams(("arbitrary",)),
    )(x, pos_col, invf_row, wp_in, w_heads, q_g, kv_g)


def _attn_fwd(q, k, vt):
    t = q.shape[1]
    bq, bk = ATTN_FWD_WIDE, ATTN_NARROW
    n_diag = bq // bk
    chunk = SOFTMAX_ROWS

    def body(q_ref, k_ref, vt_ref, o_ref, lse_ref, s0, s1, p0, p1, x0, x1, m_scr, l_scr, a_scr, acc_scr):
        i = pl.program_id(1)
        at = lambda j: pl.ds(pl.multiple_of(j * bk, bk), bk)

        def exp_pass(s_in, block_max, p_out, diagonal=False, cols=slice(None)):
            width = bq if cols == slice(None) else cols.stop - cols.start

            def load(r):
                s = s_in[r:r + chunk, cols]
                if diagonal:
                    key = lax.broadcasted_iota(jnp.int32, (chunk, width), 0) + r
                    qry = lax.broadcasted_iota(jnp.int32, (chunk, width), 1)
                    s = jnp.where(qry >= key, s, -jnp.inf)
                return s

            if diagonal:
                block_max = jnp.max(load(0), axis=0, keepdims=True)
                for r in range(chunk, bk, chunk):
                    block_max = jnp.maximum(block_max, jnp.max(load(r), axis=0, keepdims=True))
            m_old = m_scr[:, cols]
            m_new = jnp.maximum(m_old, block_max)
            alpha = jnp.exp2(m_old - m_new)
            total = jnp.zeros((1, width), F32)
            for r in range(0, bk, chunk):
                p = jnp.exp2(load(r) - m_new)
                p_out[r:r + chunk, cols] = p.astype(BF16)
                total = total + jnp.sum(p, axis=0, keepdims=True)
            m_scr[:, cols] = m_new
            l_scr[:, cols] = alpha * l_scr[:, cols] + total
            return alpha

        def scores(j, s_out, x_out):
            s = _dot_nt(k_ref[0, at(j), :], q_ref[0])
            s_out[...] = s
            x_out[...] = jnp.max(s, axis=0, keepdims=True)

        def value_product(j, p_in):
            return _dot(vt_ref[0, :, at(j)], p_in[...])

        def one_pass(j, s_in, x_in, s_out, x_out, p_prev, p_cur):
            scores(j + 1, s_out, x_out)
            acc_scr[...] = a_scr[...] * acc_scr[...] + value_product(jnp.maximum(j - 1, 0), p_prev)
            a_scr[...] = exp_pass(s_in, x_in[...], p_cur)

        scores(0, s0, x0)
        p1[...] = jnp.zeros_like(p1)
        a_scr[...] = jnp.ones_like(a_scr)
        m_scr[...] = jnp.full(m_scr.shape, -jnp.inf, F32)
        l_scr[...] = jnp.zeros_like(l_scr)
        acc_scr[...] = jnp.zeros_like(acc_scr)

        def two_passes(n, _):
            one_pass(2 * n, s0, x0, s1, x1, p1, p0)
            one_pass(2 * n + 1, s1, x1, s0, x0, p0, p1)
            return 0

        lax.fori_loop(0, (n_diag // 2) * i, two_passes, 0)
        d = n_diag * i
        alpha, p_prev, cols = a_scr[...], p1, slice(0, bq)
        for u in range(n_diag + 1):
            s_in, s_next, p_cur = (s0, s1, p0) if u % 2 == 0 else (s1, s0, p1)
            if u + 1 < n_diag:
                ahead = slice((u + 1) * bk, bq)
                s_next[:, ahead] = _dot_nt(k_ref[0, at(d + u + 1), :], q_ref[0, ahead, :])
            acc_scr[:, cols] = alpha * acc_scr[:, cols] + _dot(vt_ref[0, :, at(jnp.maximum(d + u - 1, 0))], p_prev[:, cols])
            if u < n_diag:
                cols = slice(u * bk, bq)
                alpha = exp_pass(s_in, None, p_cur, diagonal=True, cols=cols)
                p_prev = p_cur
        o_ref[0] = jnp.transpose(acc_scr[...] / l_scr[...])
        lse_ref[0] = m_scr[...] + jnp.log2(l_scr[...])

    tile = lambda dtype: pltpu.VMEM((bk, bq), dtype)
    stat = pltpu.VMEM((1, bq), F32)
    return pl.pallas_call(
        body, name="attn_fwd", grid=(HEADS, t // bq),
        in_specs=[pl.BlockSpec((1, bq, HEAD_PAD), lambda h, i: (h, i, 0)),
                  pl.BlockSpec((1, t, HEAD_PAD), lambda h, i: (h, 0, 0)),
                  pl.BlockSpec((1, HEAD_PAD, t), lambda h, i: (h, 0, 0))],
        out_specs=[pl.BlockSpec((1, bq, HEAD_PAD), lambda h, i: (h, i, 0)),
                   pl.BlockSpec((1, 1, bq), lambda h, i: (h, 0, i))],
        out_shape=[jax.ShapeDtypeStruct((HEADS, t, HEAD_PAD), F32), jax.ShapeDtypeStruct((HEADS, 1, t), F32)],
        scratch_shapes=[tile(F32), tile(F32), tile(BF16), tile(BF16), stat, stat, stat, stat, stat,
                        pltpu.VMEM((HEAD_PAD, bq), F32)],
        compiler_params=_cparams(("arbitrary", "arbitrary")),
    )(q, k, vt)


def _mid(x, target, proj, ol, w_out, ws_low, ws_low_t, bsp, sgu_g, sgu_b, ln_g, ln_b):
    t = x.shape[0]
    tm = TOKEN_TILE
    n_steps = t // tm

    def body(x_ref, tgt_ref, za_ref, u_ref, v_ref, zb_ref, ol_ref, wout_ref, ws_ref, wst_ref, bsp_ref,
             sg_ref, sb_ref, lg_ref, lb_ref,
             dr_ref, do_ref, drow_ref, drest_ref, dwout_ref, dws_ref, dbs_ref, dlg_ref, dlb_ref, dsg_ref, dsb_ref,
             loss_ref, dbsp_acc):
        step = pl.program_id(0)

        @pl.when(step == 0)
        def _():
            dwout_ref[...] = jnp.zeros_like(dwout_ref)
            dws_ref[...] = jnp.zeros_like(dws_ref)
            dbs_ref[...] = jnp.zeros_like(dbs_ref)
            dlg_ref[...] = jnp.zeros_like(dlg_ref)
            dlb_ref[...] = jnp.zeros_like(dlb_ref)
            dsg_ref[...] = jnp.zeros_like(dsg_ref)
            dsb_ref[...] = jnp.zeros_like(dsb_ref)
            loss_ref[...] = jnp.zeros_like(loss_ref)
            dbsp_acc[...] = jnp.zeros_like(dbsp_acc)

        n_chunks = tm // CHUNK
        groups = G_WIDTH // LANES

        def side_by_side(a):
            return [jnp.concatenate([a[c * CHUNK:(c + 1) * CHUNK, g * LANES:(g + 1) * LANES] for c in range(n_chunks)],
                                    axis=1) for g in range(groups)]

        def by_chunk(wide):
            return jnp.concatenate([jnp.concatenate([wide[g][:, c * LANES:(c + 1) * LANES] for g in range(groups)], axis=1)
                                    for c in range(n_chunks)], axis=0)

        def own_lanes(h):
            lane = lax.broadcasted_iota(jnp.int32, (CHUNK, n_chunks * LANES), 1)
            return (lane % LANES) // G_HEAD_DIM == h % 2

        def spatial(w_ref, wide):
            return [sum(jnp.where(own_lanes(h), _dot(w_ref[h], wide[g]), 0.0) for h in (2 * g, 2 * g + 1))
                    for g in range(groups)]

        attn = jnp.concatenate([ol_ref[h][:, NOPE:] for h in range(HEADS)], axis=-1)
        za = za_ref[...]
        sig_a = _sigmoid(za)
        silu_a = za * sig_a
        out_a = attn * silu_a
        u = u_ref[...]
        ug = _gelu(u)
        vpre = v_ref[...]
        gv = _gelu(vpre)
        mu_v = jnp.mean(gv, axis=-1, keepdims=True)
        cen_v = gv - mu_v
        rstd_v = lax.rsqrt(jnp.mean(cen_v * cen_v, axis=-1, keepdims=True) + EPS)
        vhat = cen_v * rstd_v
        vg = vhat * sg_ref[...] + sb_ref[...]
        vg_b = vg.astype(BF16)
        sv = by_chunk(spatial(ws_ref, side_by_side(vg_b))) + jnp.tile(bsp_ref[...], (n_chunks, 1))
        sgu = ug * sv
        zb = zb_ref[...]
        sig_b = _sigmoid(zb)
        silu_b = zb * sig_b
        out_b = sgu * silu_b
        merged = jnp.concatenate([out_a, out_b], axis=-1).astype(BF16)
        r = DN_ALPHA * x_ref[...] + _dot(merged, wout_ref[...])
        mu = jnp.mean(r, axis=-1, keepdims=True)
        cen = r - mu
        rstd = lax.rsqrt(jnp.mean(cen * cen, axis=-1, keepdims=True) + EPS)
        xhat = cen * rstd
        hout = xhat * lg_ref[...] + lb_ref[...]
        err = hout - tgt_ref[...]
        row_loss = jnp.mean(err * err, axis=-1, keepdims=True)
        loss_ref[...] += jnp.broadcast_to(0.5 * jnp.sum(row_loss, axis=0, keepdims=True), loss_ref.shape)

        dh = err * (1.0 / D_MODEL)
        dlg_ref[...] += jnp.sum(dh * xhat, axis=0, keepdims=True)
        dlb_ref[...] += jnp.sum(dh, axis=0, keepdims=True)
        dxhat = dh * lg_ref[...]
        dr = rstd * (dxhat - jnp.mean(dxhat, axis=-1, keepdims=True)
                     - xhat * jnp.mean(dxhat * xhat, axis=-1, keepdims=True))
        dr_ref[...] = dr
        dr_b = dr.astype(BF16)
        dwout_ref[...] += _dot_tn(merged, dr_b)
        dmerged = _dot_nt(dr_b, wout_ref[...])
        d_out_a = dmerged[:, :G_WIDTH]
        d_out_b = dmerged[:, G_WIDTH:]
        dattn = d_out_a * silu_a
        for h in range(HEADS):
            do_h = dattn[:, h * VDIM:(h + 1) * VDIM]
            do_ref[h] = jnp.concatenate([jnp.zeros((tm, NOPE), F32), do_h], axis=-1).astype(BF16)
        feature = lax.broadcasted_iota(jnp.int32, (G_WIDTH, LANES), 0) // VDIM
        column = lax.broadcasted_iota(jnp.int32, (G_WIDTH, LANES), 1)
        head_sums = jnp.dot(dattn * attn, jnp.where(feature == column, 1.0, 0.0).astype(F32),
                            preferred_element_type=F32, precision=lax.Precision.HIGHEST)
        dsums_t = jnp.transpose(head_sums)
        for h in range(HEADS):
            drow_ref[h] = dsums_t[h:h + 1, :]
        dza = d_out_a * attn * (sig_a * (1.0 + za * (1.0 - sig_a)))
        dsgu = d_out_b * silu_b
        dzb = d_out_b * sgu * (sig_b * (1.0 + zb * (1.0 - sig_b)))
        du = dsgu * sv * _gelu_grad(u)
        dsv = dsgu * ug
        dsv_b = dsv.astype(BF16)
        for cix in range(n_chunks):
            dbsp_acc[...] += dsv[cix * CHUNK:(cix + 1) * CHUNK, :]
        dsv_wide, vg_wide = side_by_side(dsv_b), side_by_side(vg_b)
        dvg = by_chunk(spatial(wst_ref, dsv_wide))
        for h in range(HEADS):
            mine = jnp.where(own_lanes(h), dsv_wide[h // 2], jnp.zeros_like(dsv_wide[h // 2]))
            dws_ref[h] += _dot_nt(mine, vg_wide[h // 2])
        dsg_ref[...] += jnp.sum(dvg * vhat, axis=0, keepdims=True)
        dsb_ref[...] += jnp.sum(dvg, axis=0, keepdims=True)
        dvhat = dvg * sg_ref[...]
        dgv = rstd_v * (dvhat - jnp.mean(dvhat, axis=-1, keepdims=True)
                        - vhat * jnp.mean(dvhat * vhat, axis=-1, keepdims=True))
        dv = dgv * _gelu_grad(vpre)
        drest_ref[...] = jnp.concatenate([dza, du, dv, dzb], axis=-1).astype(BF16)

        @pl.when(step == n_steps - 1)
        def _():
            tri = (lax.broadcasted_iota(jnp.int32, (CHUNK, CHUNK), 0)
                   >= lax.broadcasted_iota(jnp.int32, (CHUNK, CHUNK), 1))
            for h in range(HEADS):
                dws_ref[h] = jnp.where(tri, dws_ref[h], 0.0)
            tot = dbsp_acc[...]
            lane = lax.broadcasted_iota(jnp.int32, (CHUNK, LANES), 1)
            dbs = jnp.zeros((CHUNK, LANES), F32)
            for h in range(HEADS):
                head_sum = jnp.sum(tot[:, h * G_HEAD_DIM:(h + 1) * G_HEAD_DIM], axis=-1, keepdims=True)
                dbs = jnp.where(lane == h, head_sum, dbs)
            dbs_ref[...] = dbs

    full = lambda a: pl.BlockSpec(a.shape, lambda i: (0,) * a.ndim)
    tile = lambda w, j=0: pl.BlockSpec((tm, w), lambda i, j=j: (i, j))
    heads = pl.BlockSpec((HEADS, tm, HEAD_PAD), lambda i: (0, i, 0))
    acc = lambda shape: (pl.BlockSpec(shape, lambda i: (0,) * len(shape)), jax.ShapeDtypeStruct(shape, F32))
    accs = [acc((D_MODEL, D_MODEL)), acc((HEADS, CHUNK, CHUNK)), acc((CHUNK, LANES)), acc((1, D_MODEL)),
            acc((1, D_MODEL)), acc((1, G_WIDTH)), acc((1, G_WIDTH)), acc((1, LANES))]
    return pl.pallas_call(
        body, name="mid", grid=(n_steps,),
        in_specs=[tile(D_MODEL), tile(D_MODEL), tile(G_WIDTH, 1), tile(G_WIDTH, 2), tile(G_WIDTH, 3), tile(G_WIDTH, 4),
                  heads, full(w_out), full(ws_low), full(ws_low_t), full(bsp), full(sgu_g), full(sgu_b),
                  full(ln_g), full(ln_b)],
        out_specs=[tile(D_MODEL), heads, pl.BlockSpec((HEADS, 1, tm), lambda i: (0, 0, i)), tile(4 * G_WIDTH)]
        + [a[0] for a in accs],
        out_shape=[jax.ShapeDtypeStruct((t, D_MODEL), F32), jax.ShapeDtypeStruct((HEADS, t, HEAD_PAD), BF16),
                   jax.ShapeDtypeStruct((HEADS, 1, t), F32), jax.ShapeDtypeStruct((t, 4 * G_WIDTH), BF16)]
        + [a[1] for a in accs],
        scratch_shapes=[pltpu.VMEM((CHUNK, G_WIDTH), F32)],
        compiler_params=_cparams(("arbitrary",)),
    )(x, target, proj, proj, proj, proj, ol, w_out, ws_low, ws_low_t, bsp, sgu_g, sgu_b, ln_g, ln_b)


def _attn_bwd(q, k, v, do, lse_row, d_row):
    t = q.shape[1]
    bk, bq = ATTN_BWD_WIDE, ATTN_NARROW
    n_diag = bk // bq
    last = t // bq - 1
    chunk = SOFTMAX_ROWS

    def body(q_ref, k_ref, v_ref, do_ref, lse_ref, drow_ref, dqt_ref, dk_ref, dv_ref,
             s0, s1, e0, e1, p0, p1, g0, g1, kt_scr):
        j = pl.program_id(1)
        at = lambda i: pl.ds(pl.multiple_of(i * bq, bq), bq)

        @pl.when(j == 0)
        def _():
            dqt_ref[...] = jnp.zeros_like(dqt_ref)

        kt_scr[...] = jnp.transpose(k_ref[0].astype(F32)).astype(BF16)
        dk_ref[...] = jnp.zeros_like(dk_ref)
        dv_ref[...] = jnp.zeros_like(dv_ref)

        def products(i, s_out, e_out, keys=slice(0, bk)):
            i = jnp.minimum(i, last)
            s_out[keys, :] = _dot_nt(k_ref[0, keys, :], q_ref[0, at(i), :])
            e_out[keys, :] = _dot_nt(v_ref[0, keys, :], do_ref[0, at(i), :])

        def gradients(i, p_in, g_in, keys=slice(0, bk)):
            dv_ref[0, keys, :] += _dot(p_in[keys, :], do_ref[0, at(i), :])
            dk_ref[0, keys, :] += _dot(g_in[keys, :], q_ref[0, at(i), :])
            dqt_ref[0, :, at(i)] += _dot(kt_scr[:, keys], g_in[keys, :])

        def elementwise(i, s_in, e_in, p_out, g_out, qry0=None, keys=slice(0, bk)):
            lse = lse_ref[0, :, at(i)]
            dsum = drow_ref[0, :, at(i)]
            for r in range(keys.start, keys.stop, chunk):
                p = jnp.exp2(s_in[r:r + chunk, :] - lse)
                if qry0 is not None:
                    key = lax.broadcasted_iota(jnp.int32, (chunk, bq), 0) + r
                    qry = lax.broadcasted_iota(jnp.int32, (chunk, bq), 1) + qry0
                    p = jnp.where(qry >= key, p, 0.0)
                p_out[r:r + chunk, :] = p.astype(BF16)
                g_out[r:r + chunk, :] = (p * (e_in[r:r + chunk, :] - dsum)).astype(BF16)

        def one_pass(i, s_in, e_in, s_out, e_out, p_prev, g_prev, p_cur, g_cur):
            products(i + 1, s_out, e_out)
            gradients(i - 1, p_prev, g_prev)
            elementwise(i, s_in, e_in, p_cur, g_cur)

        first = n_diag * j
        keys_of = lambda u: slice(0, min((u + 1) * bq, bk))
        even, odd = (s0, e0, p0, g0), (s1, e1, p1, g1)
        products(first, s0, e0, keys_of(0))
        products(first + 1, s1, e1, keys_of(1))
        elementwise(first, s0, e0, p0, g0, qry0=0, keys=keys_of(0))
        for u in range(1, n_diag):
            (s_in, e_in, p_cur, g_cur), (s_out, e_out, p_prev, g_prev) = (odd, even) if u % 2 else (even, odd)
            products(first + u + 1, s_out, e_out, keys_of(u + 1))
            gradients(first + u - 1, p_prev, g_prev, keys_of(u - 1))
            elementwise(first + u, s_in, e_in, p_cur, g_cur, qry0=u * bq, keys=keys_of(u))

        def two_passes(n, _):
            i = first + n_diag + 2 * n
            one_pass(i, s0, e0, s1, e1, p1, g1, p0, g0)
            one_pass(i + 1, s1, e1, s0, e0, p0, g0, p1, g1)
            return 0

        lax.fori_loop(0, (last - first - n_diag + 1) // 2, two_passes, 0)
        gradients(last, p1, g1)
        dk_ref[0] = dk_ref[0] * LN2

    whole = pl.BlockSpec((1, t, HEAD_PAD), lambda h, j: (h, 0, 0), pipeline_mode=pl.Buffered(1))
    block = pl.BlockSpec((1, bk, HEAD_PAD), lambda h, j: (h, j, 0))
    rows = pl.BlockSpec((1, 1, t), lambda h, j: (h, 0, 0), pipeline_mode=pl.Buffered(1))
    shape = jax.ShapeDtypeStruct((HEADS, t, HEAD_PAD), F32)
    tile = lambda dtype: pltpu.VMEM((bk, bq), dtype)
    return pl.pallas_call(
        body, name="attn_bwd", grid=(HEADS, t // bk),
        in_specs=[whole, block, block, whole, rows, rows],
        out_specs=[pl.BlockSpec((1, HEAD_PAD, t), lambda h, j: (h, 0, 0)), block, block],
        out_shape=[jax.ShapeDtypeStruct((HEADS, HEAD_PAD, t), F32), shape, shape],
        scratch_shapes=[tile(F32), tile(F32), tile(F32), tile(F32), tile(BF16), tile(BF16),
                        tile(BF16), tile(BF16), pltpu.VMEM((HEAD_PAD, bk), BF16)],
        compiler_params=_cparams(("arbitrary", "arbitrary")),
    )(q, k, v, do, lse_row, d_row)


def _bwd_qkv(dq, dk, dv, proj, pos_col, invf_row, w_heads, q_g, kv_g):
    t = proj.shape[0]
    tm = TOKEN_TILE

    def body(dq_ref, dk_ref, dv_ref, ph_ref, pos_ref, invf_ref, wh_ref, qg_ref, kvg_ref,
             dhead_ref, dwh_ref, dqg_ref, dkvg_ref):
        @pl.when(pl.program_id(0) == 0)
        def _():
            dwh_ref[...] = jnp.zeros_like(dwh_ref)
            dqg_ref[...] = jnp.zeros_like(dqg_ref)
            dkvg_ref[...] = jnp.zeros_like(dkvg_ref)

        cos, s1, s2 = _rope_tables(pos_ref[...], invf_ref[...])
        lane = lax.broadcasted_iota(jnp.int32, (tm, LANES), 1)
        c_q = ph_ref[:, :Q_LORA]
        c_kv = ph_ref[:, Q_LORA:Q_LORA + KV_LORA]
        rstd_q = lax.rsqrt(jnp.mean(c_q * c_q, axis=-1, keepdims=True) + EPS)
        rstd_kv = lax.rsqrt(jnp.mean(c_kv * c_kv, axis=-1, keepdims=True) + EPS)
        qhat = c_q * rstd_q
        kvhat = c_kv * rstd_kv
        cqn = (qhat * qg_ref[...]).astype(BF16)
        ckvn = (kvhat * kvg_ref[...]).astype(BF16)
        dcqn = jnp.zeros((tm, Q_LORA), F32)
        dckvn = jnp.zeros((tm, KV_LORA), F32)
        dkr_rot = jnp.zeros((tm, LANES), F32)
        for h in range(HEADS):
            dq_b = _rope(jnp.transpose(dq_ref[h]) * ATTN_SCALE, cos, s1, s2, -1.0).astype(BF16)
            dk_h = dk_ref[h]
            dkv_b = jnp.where(lane < NOPE, dk_h, dv_ref[h]).astype(BF16)
            dkr_rot = dkr_rot + dk_h
            dwh_ref[h, :Q_LORA, :] += _dot_tn(cqn, dq_b)
            dwh_ref[h, Q_LORA:, :] += _dot_tn(ckvn, dkv_b)
            dcqn = dcqn + _dot_nt(dq_b, wh_ref[h, :Q_LORA, :])
            dckvn = dckvn + _dot_nt(dkv_b, wh_ref[h, Q_LORA:, :])
        rot_lanes = (lane >= KR_LO) & (lane < KR_LO + ROPE)
        dkr_raw = jnp.where(rot_lanes, _rope(dkr_rot, cos, s1, s2, -1.0), 0.0)
        dqg_ref[...] += jnp.sum(dcqn * qhat, axis=0, keepdims=True)
        dkvg_ref[...] += jnp.sum(dckvn * kvhat, axis=0, keepdims=True)
        dqh = dcqn * qg_ref[...]
        dkvh = dckvn * kvg_ref[...]
        dc_q = rstd_q * (dqh - qhat * jnp.mean(dqh * qhat, axis=-1, keepdims=True))
        dc_kv = rstd_kv * (dkvh - kvhat * jnp.mean(dkvh * kvhat, axis=-1, keepdims=True))
        dhead_ref[...] = jnp.concatenate([dc_q, dc_kv, dkr_raw], axis=-1).astype(BF16)

    full = lambda a: pl.BlockSpec(a.shape, lambda i: (0,) * a.ndim)
    heads = pl.BlockSpec((HEADS, tm, HEAD_PAD), lambda i: (0, i, 0))
    acc = lambda shape: (pl.BlockSpec(shape, lambda i: (0,) * len(shape)), jax.ShapeDtypeStruct(shape, F32))
    accs = [acc(w_heads.shape), acc((1, Q_LORA)), acc((1, KV_LORA))]
    return pl.pallas_call(
        body, name="bwd_qkv", grid=(t // tm,),
        in_specs=[pl.BlockSpec((HEADS, HEAD_PAD, tm), lambda i: (0, 0, i)), heads, heads,
                  pl.BlockSpec((tm, 4 * LANES), lambda i: (i, 0)),
                  pl.BlockSpec((tm, 1), lambda i: (i, 0)), full(invf_row), full(w_heads),
                  full(q_g), full(kv_g)],
        out_specs=[pl.BlockSpec((tm, 4 * LANES), lambda i: (i, 0))] + [a[0] for a in accs],
        out_shape=[jax.ShapeDtypeStruct((t, 4 * LANES), BF16)] + [a[1] for a in accs],
        compiler_params=_cparams(("arbitrary",)),
    )(dq, dk, dv, proj, pos_col, invf_row, w_heads, q_g, kv_g)


def _bwd_in(x, dr, dhead, drest, wp_in):
    t = x.shape[0]
    tm = TOKEN_TILE
    n_head = dhead.shape[1]

    def body(x_ref, dr_ref, dhead_ref, drest_ref, win_ref, gx_ref, dwin_ref):
        @pl.when(pl.program_id(0) == 0)
        def _():
            dwin_ref[...] = jnp.zeros_like(dwin_ref)

        xb = x_ref[...].astype(BF16)
        dh_b = dhead_ref[...]
        dr_b = drest_ref[...]
        gx_ref[...] = (DN_ALPHA * dr_ref[...] + _dot_nt(dh_b, win_ref[:, :n_head])
                       + _dot_nt(dr_b, win_ref[:, n_head:]))
        dwin_ref[:, :n_head] += _dot_tn(xb, dh_b)
        dwin_ref[:, n_head:] += _dot_tn(xb, dr_b)

    tile = lambda w: pl.BlockSpec((tm, w), lambda i: (i, 0))
    whole = pl.BlockSpec(wp_in.shape, lambda i: (0, 0))
    return pl.pallas_call(
        body, name="bwd_in", grid=(t // tm,),
        in_specs=[tile(D_MODEL), tile(D_MODEL), tile(n_head), tile(drest.shape[1]), whole],
        out_specs=[tile(D_MODEL), whole],
        out_shape=[jax.ShapeDtypeStruct((t, D_MODEL), F32), jax.ShapeDtypeStruct(wp_in.shape, F32)],
        compiler_params=_cparams(("arbitrary",)),
    )(x, dr, dhead, drest, wp_in)


def _adam(parts, w, m, v, *, name, tile_rows):
    n, rows, cols = parts.shape

    def body(p_ref, w_ref, m_ref, v_ref, g_ref, d_ref, nm_ref, nv_ref):
        g = p_ref[0].astype(F32)
        for s in range(1, n):
            g = g + p_ref[s].astype(F32)
        m_new = ADAM_B1 * m_ref[...] + (1.0 - ADAM_B1) * g
        v_new = ADAM_B2 * v_ref[...] + (1.0 - ADAM_B2) * (g * g)
        m_hat = m_new / (1.0 - ADAM_B1 ** ADAM_STEP)
        v_hat = v_new / (1.0 - ADAM_B2 ** ADAM_STEP)
        g_ref[...] = g
        d_ref[...] = -ADAM_LR * (m_hat / (jnp.sqrt(v_hat) + ADAM_EPS) + ADAM_WD * w_ref[...])
        nm_ref[...] = m_new
        nv_ref[...] = v_new

    flat = pl.BlockSpec((tile_rows, cols), lambda i: (i, 0))
    shape = jax.ShapeDtypeStruct((rows, cols), F32)
    return pl.pallas_call(
        body, name=name, grid=(rows // tile_rows,),
        in_specs=[pl.BlockSpec((n, tile_rows, cols), lambda i: (0, i, 0)), flat, flat, flat],
        out_specs=[flat] * 4, out_shape=[shape] * 4,
        compiler_params=_cparams(("arbitrary",)),
    )(parts, w, m, v)


SMALL_NAMES = ("q_norm_g", "kv_norm_g", "sgu_norm_g", "sgu_norm_b", "b_spatial", "ln_g", "ln_b")
SMALL_SIZES = (Q_LORA, KV_LORA, G_WIDTH, G_WIDTH, HEADS * CHUNK, D_MODEL, D_MODEL)


def _pack_small(vals, last=None):
    flat = jnp.concatenate([v.reshape(-1) for v in vals])
    pad = SMALL_LEN - flat.shape[0]
    if last is None:
        return jnp.pad(flat, (0, pad))
    return jnp.concatenate([flat, jnp.zeros((pad - 1,), F32), last.reshape(1)])


def _unpack_small(flat):
    out, at = [], 0
    for n in SMALL_SIZES:
        out.append(flat[at:at + n])
        at += n
    out[4] = out[4].reshape(HEADS, CHUNK)
    return out


UQ_SHARD = HEADS * (NOPE + ROPE) // N_DEV
HEAD_ROWS = Q_LORA + KV_LORA
MIXED_ROWS = HEAD_ROWS + CHUNK + SMALL_LEN // N_DEV // LANES


def _head_slab(w_uq_shard, w_ukv_shard):
    return jnp.concatenate([jnp.pad(w_uq_shard, ((0, 0), (0, LANES - UQ_SHARD))), w_ukv_shard])


IN_SHARD = D_IN // N_DEV


def _w_in_pieces():
    split = Q_LORA + KV_LORA
    moves = ((0, split, 0), (split, split + ROPE, KR_LO), (split + ROPE, D_IN, LANES - ROPE))
    pieces = []
    for s in range(N_DEV):
        lo, hi = s * IN_SHARD, (s + 1) * IN_SHARD
        for a, b, shift in moves:
            a, b = max(a, lo), min(b, hi)
            if a < b:
                pieces.append((s, a - lo, a + shift, b - a))
    return pieces


def _padded_w_in(shards):
    tr = TOKEN_TILE

    def body(sh_ref, o_ref):
        o_ref[...] = jnp.zeros_like(o_ref)
        for s, src, dst, width in _w_in_pieces():
            o_ref[:, dst:dst + width] = sh_ref[s, :, src:src + width]

    return pl.pallas_call(
        body, name="w_in_pad", grid=(D_MODEL // tr,),
        in_specs=[pl.BlockSpec((N_DEV, tr, IN_SHARD), lambda i: (0, i, 0))],
        out_specs=pl.BlockSpec((tr, D_IN_PAD), lambda i: (i, 0)),
        out_shape=jax.ShapeDtypeStruct((D_MODEL, D_IN_PAD), shards.dtype),
        compiler_params=_cparams(("arbitrary",)),
    )(shards)


def _w_in_shards(dwp_in):
    tr = TOKEN_TILE
    by_shard = [[p for p in _w_in_pieces() if p[0] == s] for s in range(N_DEV)]

    def body(w_ref, o_ref):
        for s, pieces in enumerate(by_shard):
            parts = [w_ref[:, dst:dst + width] for _, _, dst, width in pieces]
            o_ref[s] = parts[0] if len(parts) == 1 else jnp.concatenate(parts, axis=1)

    return pl.pallas_call(
        body, name="w_in_split", grid=(D_MODEL // tr,),
        in_specs=[pl.BlockSpec((tr, D_IN_PAD), lambda i: (i, 0))],
        out_specs=pl.BlockSpec((N_DEV, tr, IN_SHARD), lambda i: (0, i, 0)),
        out_shape=jax.ShapeDtypeStruct((N_DEV, D_MODEL, IN_SHARD), dwp_in.dtype),
        compiler_params=_cparams(("arbitrary",)),
    )(dwp_in)


def kernel(x, positions, w_in, q_norm_g, w_uq, kv_norm_g, w_ukv, sgu_norm_g, sgu_norm_b, w_spatial, b_spatial, w_out, ln_g, ln_b, loss_target, m_w_in, m_q_norm_g, m_w_uq, m_kv_norm_g, m_w_ukv, m_sgu_norm_g, m_sgu_norm_b, m_w_spatial, m_b_spatial, m_w_out, m_ln_g, m_ln_b, v_w_in, v_q_norm_g, v_w_uq, v_kv_norm_g, v_w_ukv, v_sgu_norm_g, v_sgu_norm_b, v_w_spatial, v_b_spatial, v_w_out, v_ln_g, v_ln_b):
    me = 4 * lax.axis_index("x") + 2 * lax.axis_index("y") + lax.axis_index("c")
    seq = x.shape[1]
    x2 = x.reshape(seq, D_MODEL)
    tgt2 = loss_target.reshape(seq, D_MODEL)
    pos_col = positions.reshape(seq, 1)

    w_in_shards, w_out_shards, w_heads = _gather_two_level(
        [w_in.astype(BF16), w_out.astype(BF16), _head_slab(w_uq, w_ukv).astype(BF16)],
        name="wgather")
    (loss_part, grad_x, d_in, d_heads, d_out, d_ws, d_bs_t, d_lng, d_lnb, d_sgug, d_sgub, d_qg, d_kvg) = _local_step(
        x2, tgt2, pos_col, w_in_shards, w_heads, w_out_shards.reshape(D_MODEL, D_MODEL), q_norm_g, kv_norm_g,
        sgu_norm_g, sgu_norm_b, w_spatial, b_spatial, ln_g, ln_b)

    small_part = _pack_small([d_qg, d_kvg, d_sgug, d_sgub, d_bs_t[:, :HEADS].T, d_lng, d_lnb], last=loss_part[0, :1])
    mixed = jnp.concatenate([d_heads, d_ws, small_part.reshape(N_DEV, -1, LANES)], axis=1)
    by_chip = [g.reshape((N_CHIPS, 2) + g.shape[1:])
               for g in (d_in, d_out.reshape(N_DEV, D_MODEL // N_DEV, D_MODEL), mixed)]
    from_sibling = _sibling_swap(by_chip, name="gswap")
    core = lax.axis_index("c").astype(jnp.int32).reshape(1)
    pair_sums = [_pair_sum(a, b, core, name=nm, tile_rows=tr, out_dtype=dt) for a, b, nm, tr, dt in zip(
        by_chip, from_sibling, ("gsum_in", "gsum_out", "gsum_mixed"), (TOKEN_TILE, D_MODEL // N_DEV, MIXED_ROWS),
        (BF16, BF16, F32))]
    recv_in, recv_out, recv_mixed = _chip_exchange(pair_sums, name="gexch")

    take = lambda a: lax.dynamic_index_in_dim(a, me, 0, keepdims=False)
    small_w = _pack_small([q_norm_g, kv_norm_g, sgu_norm_g, sgu_norm_b, b_spatial, ln_g, ln_b])
    small_m = _pack_small([m_q_norm_g, m_kv_norm_g, m_sgu_norm_g, m_sgu_norm_b, m_b_spatial, m_ln_g, m_ln_b])
    small_v = _pack_small([v_q_norm_g, v_kv_norm_g, v_sgu_norm_g, v_sgu_norm_b, v_b_spatial, v_ln_g, v_ln_b])
    own_mixed = lambda uq, ukv, sp, small: jnp.concatenate(
        [_head_slab(uq, ukv), take(sp), take(small.reshape(N_DEV, -1, LANES))])
    res_in = _adam(recv_in, w_in, m_w_in, v_w_in, name="adam_in", tile_rows=TOKEN_TILE)
    res_out = _adam(recv_out, w_out, m_w_out, v_w_out, name="adam_out", tile_rows=D_MODEL // N_DEV)
    res_mixed = _adam(recv_mixed, own_mixed(w_uq, w_ukv, w_spatial, small_w), own_mixed(m_w_uq, m_w_ukv, m_w_spatial, small_m),
                      own_mixed(v_w_uq, v_w_ukv, v_w_spatial, small_v), name="adam_mixed", tile_rows=MIXED_ROWS)

    rep_g, = _exchange([res_mixed[0][HEAD_ROWS:]], name="sgather", per_destination=False)
    rep_pack = lambda sp, small: jnp.concatenate(
        [sp.reshape(N_DEV, CHUNK, LANES), small.reshape(N_DEV, -1, LANES)], axis=1).reshape(-1, LANES)
    _, delta_rep, m_rep, v_rep = _adam(rep_g.reshape(1, N_DEV * REP_ROWS, LANES), rep_pack(w_spatial, small_w),
                                       rep_pack(m_w_spatial, small_m), rep_pack(v_w_spatial, small_v),
                                       name="adam_rep", tile_rows=N_DEV * REP_ROWS)

    def rep_unpack(a):
        a = a.reshape(N_DEV, REP_ROWS, LANES)
        small = _unpack_small(a[:, CHUNK:].reshape(-1))
        return [small[0], small[1], small[2], small[3], a[:, :CHUNK], small[4], small[5], small[6]]

    def ordered(which, rep):
        r_qg, r_kvg, r_sg, r_sb, r_ws, r_bs, r_lg, r_lb = rep_unpack(rep)
        heads = res_mixed[which]
        return [res_in[which], r_qg, heads[:Q_LORA, :UQ_SHARD], r_kvg, heads[Q_LORA:HEAD_ROWS], r_sg, r_sb, r_ws, r_bs,
                res_out[which], r_lg, r_lb]

    loss = rep_g[N_DEV - 1, REP_ROWS - 1, LANES - 1]
    outs = [loss, grad_x.reshape(x.shape)]
    outs += ordered(0, rep_g.reshape(-1, LANES))
    outs += ordered(1, delta_rep)
    outs += ordered(2, m_rep)
    outs += ordered(3, v_rep)
    return tuple(outs)


def _local_step(x2, tgt2, pos_col, w_in_shards, w_heads, w_out_full, q_norm_g, kv_norm_g, sgu_norm_g, sgu_norm_b,
                w_spatial, b_spatial, ln_g, ln_b):
    wp_in = _padded_w_in(w_in_shards)

    half = jnp.arange(HALF, dtype=F32)
    inv_freq = 1.0 / (ROPE_THETA ** (half / HALF))
    invf_row = jnp.concatenate([jnp.zeros((KR_LO,), F32), inv_freq, inv_freq,
                                jnp.zeros((LANES - KR_LO - ROPE,), F32)]).reshape(1, LANES)
    tri = jnp.tril(jnp.ones((CHUNK, CHUNK), dtype=bool))
    ws_low = jnp.where(tri[None], w_spatial, 0.0).astype(BF16)
    ws_low_t = ws_low.transpose(0, 2, 1)
    bsp = jnp.repeat(b_spatial.T, G_HEAD_DIM, axis=1)
    row = lambda a: a.reshape(1, -1)

    proj, q, k, v, vt = _fwd_proj(x2, pos_col, invf_row, wp_in, w_heads, row(q_norm_g), row(kv_norm_g))
    o, lse_row = _attn_fwd(q, k, vt)
    (dr, do, d_row, drest, d_out, d_ws, d_bs_t, d_lng, d_lnb, d_sgug, d_sgub, loss_part) = _mid(
        x2, tgt2, proj, o, w_out_full, ws_low, ws_low_t, bsp, row(sgu_norm_g), row(sgu_norm_b), row(ln_g), row(ln_b))
    dqt, dk, dv = _attn_bwd(q, k, v, do, lse_row, d_row)
    dhead, d_heads, d_qg, d_kvg = _bwd_qkv(dqt, dk, dv, proj, pos_col, invf_row, w_heads, row(q_norm_g), row(kv_norm_g))
    grad_x, dwp_in = _bwd_in(x2, dr, dhead, drest, wp_in)
    return (loss_part, grad_x, _w_in_shards(dwp_in), d_heads, d_out, d_ws, d_bs_t, d_lng, d_lnb, d_sgug, d_sgub,
            d_qg, d_kvg)
```

```python
import functools
import math

import jax
import jax.numpy as jnp
from jax import lax
from jax.experimental import pallas as pl
from jax.experimental.pallas import tpu as pltpu

F32 = jnp.float32
BF16 = jnp.bfloat16

N_DEV = 8
D_MODEL = 1024
HEADS = 8
NOPE = 64
ROPE = 32
HALF = ROPE // 2
VDIM = 64
Q_LORA = 256
KV_LORA = 128
G_WIDTH = 512
G_HEAD_DIM = 64
CHUNK = 128
HEAD_PAD = 128
D_IN = 2464
D_IN_PAD = 2560
KR_LO = NOPE
ROPE_THETA = 10000.0
DN_ALPHA = 2.0 ** 0.25
EPS = 1e-5
ATTN_SCALE = 1.0 / math.sqrt(NOPE + ROPE)
ADAM_LR, ADAM_B1, ADAM_B2, ADAM_EPS, ADAM_WD, ADAM_STEP = 0.001, 0.9, 0.999, 1e-08, 0.01, 10

LANES = 128
REP_ROWS = 136
SMALL_LEN = 8192
VMEM_LIMIT = 56 * 1024 * 1024

TOKEN_TILE = 256
PROJ_TILE = 512
ATTN_FWD_WIDE = 2048
ATTN_BWD_WIDE = 2048
ATTN_NARROW = 512
SOFTMAX_ROWS = 256
LOG2E = 1.4426950408889634
LN2 = 0.6931471805599453
Q_PRESCALE = ATTN_SCALE * LOG2E


def _cparams(sem=None):
    return pltpu.CompilerParams(dimension_semantics=sem, vmem_limit_bytes=VMEM_LIMIT)


def _dot(a, b):
    return jnp.dot(a, b, preferred_element_type=F32)


def _dot_nt(a, b):
    return lax.dot_general(a, b, (((1,), (1,)), ((), ())), preferred_element_type=F32)


def _dot_tn(a, b):
    return lax.dot_general(a, b, (((0,), (0,)), ((), ())), preferred_element_type=F32)


def _as_row(col):
    return jnp.transpose(jnp.broadcast_to(col, (col.shape[0], LANES)))[0:1, :]


def _sigmoid(z):
    return 1.0 / (1.0 + jnp.exp(-z))


def _gelu(x):
    return 0.5 * x * (1.0 + lax.erf(x * 0.7071067811865476))


def _gelu_grad(x):
    cdf = 0.5 * (1.0 + lax.erf(x * 0.7071067811865476))
    return cdf + x * jnp.exp(-0.5 * x * x) * 0.3989422804014327


def _exchange(srcs, *, name, per_destination):
    n = len(srcs)
    slab_shapes = [s.shape[1:] if per_destination else s.shape for s in srcs]

    def body(*refs):
        src_refs, out_refs = refs[:n], refs[n:2 * n]
        send_sems, recv_sems, local_sems = refs[2 * n:]
        x, y, c = lax.axis_index("x"), lax.axis_index("y"), lax.axis_index("c")
        me = 4 * x + 2 * y + c

        def slab_for(t, dest):
            return src_refs[t].at[dest] if per_destination else src_refs[t]

        mine = [pltpu.make_async_copy(slab_for(t, me), out_refs[t].at[me], local_sems.at[t]) for t in range(n)]
        for cp in mine:
            cp.start()
        sends, arrivals = [], []
        for k in (6, 7, 4, 5, 2, 3, 1):
            px = 1 - x if k & 4 else x
            py = 1 - y if k & 2 else y
            pc = 1 - c if k & 1 else c
            peer = 4 * px + 2 * py + pc
            for t in range(n):
                sem = (k - 1) * n + t
                cp = pltpu.make_async_remote_copy(
                    src_ref=slab_for(t, peer), dst_ref=out_refs[t].at[me],
                    send_sem=send_sems.at[sem], recv_sem=recv_sems.at[sem],
                    device_id=(px, py, pc), device_id_type=pl.DeviceIdType.MESH)
                cp.start()
                sends.append(cp)
                arrivals.append(pltpu.make_async_remote_copy(
                    src_ref=slab_for(t, peer), dst_ref=out_refs[t].at[peer],
                    send_sem=send_sems.at[sem], recv_sem=recv_sems.at[sem],
                    device_id=(x, y, c), device_id_type=pl.DeviceIdType.MESH))
        for cp in arrivals:
            cp.wait_recv()
        for cp in sends:
            cp.wait_send()
        for cp in mine:
            cp.wait()

    hbm = pl.BlockSpec(memory_space=pl.ANY)
    return pl.pallas_call(
        body, name=name,
        out_shape=[jax.ShapeDtypeStruct((N_DEV,) + tuple(shape), s.dtype) for shape, s in zip(slab_shapes, srcs)],
        in_specs=[hbm] * n, out_specs=[hbm] * n,
        scratch_shapes=[pltpu.SemaphoreType.DMA(((N_DEV - 1) * n,)), pltpu.SemaphoreType.DMA(((N_DEV - 1) * n,)),
                        pltpu.SemaphoreType.DMA((n,))],
    )(*srcs)


def _gather_two_level(srcs, *, name):
    n = len(srcs)

    def body(*refs):
        src_refs, out_refs = refs[:n], refs[n:2 * n]
        send_sems, recv_sems, local_sems = refs[2 * n:]
        x, y, c = lax.axis_index("x"), lax.axis_index("y"), lax.axis_index("c")
        me, sibling = (x, y, c), (x, y, 1 - c)
        chips = [(1 - x, 1 - y), (1 - x, y), (x, 1 - y)]
        index = lambda px, py, pc: 4 * px + 2 * py + pc

        def copy(k, t, block, to, src=None):
            place = out_refs[t].at[index(*block)]
            return pltpu.make_async_remote_copy(
                src_ref=place if src is None else src, dst_ref=place,
                send_sem=send_sems.at[k * n + t], recv_sem=recv_sems.at[k * n + t],
                device_id=to, device_id_type=pl.DeviceIdType.MESH)

        mine = [pltpu.make_async_copy(src_refs[t], out_refs[t].at[index(*me)], local_sems.at[t]) for t in range(n)]
        for cp in mine:
            cp.start()
        first = [copy(1 + j, t, me, (*chip, c), src=src_refs[t]) for j, chip in enumerate(chips) for t in range(n)]
        first += [copy(0, t, me, sibling, src=src_refs[t]) for t in range(n)]
        for cp in first:
            cp.start()
        passed = []
        for j, chip in enumerate(chips):
            for t in range(n):
                copy(1 + j, t, (*chip, c), me).wait_recv()
                cp = copy(4 + j, t, (*chip, c), sibling)
                cp.start()
                passed.append(cp)
        for t in range(n):
            copy(0, t, sibling, me).wait_recv()
        for j, chip in enumerate(chips):
            for t in range(n):
                copy(4 + j, t, (*chip, 1 - c), me).wait_recv()
        for cp in first + passed:
            cp.wait_send()
        for cp in mine:
            cp.wait()

    hbm = pl.BlockSpec(memory_space=pl.ANY)
    return pl.pallas_call(
        body, name=name,
        out_shape=[jax.ShapeDtypeStruct((N_DEV,) + s.shape, s.dtype) for s in srcs],
        in_specs=[hbm] * n, out_specs=[hbm] * n,
        scratch_shapes=[pltpu.SemaphoreType.DMA((7 * n,)), pltpu.SemaphoreType.DMA((7 * n,)),
                        pltpu.SemaphoreType.DMA((n,))],
    )(*srcs)


N_CHIPS = N_DEV // 2


def _sibling_swap(srcs, *, name):
    n = len(srcs)

    def body(*refs):
        src_refs, out_refs = refs[:n], refs[n:2 * n]
        send_sems, recv_sems = refs[2 * n:]
        x, y, c = lax.axis_index("x"), lax.axis_index("y"), lax.axis_index("c")
        sends = []
        for chip in range(N_CHIPS):
            for t in range(n):
                cp = pltpu.make_async_remote_copy(
                    src_ref=src_refs[t].at[chip, 1 - c], dst_ref=out_refs[t].at[chip],
                    send_sem=send_sems.at[chip * n + t], recv_sem=recv_sems.at[chip * n + t],
                    device_id=(x, y, 1 - c), device_id_type=pl.DeviceIdType.MESH)
                cp.start()
                sends.append(cp)
        for cp in sends:
            cp.wait_recv()
        for cp in sends:
            cp.wait_send()

    hbm = pl.BlockSpec(memory_space=pl.ANY)
    return pl.pallas_call(
        body, name=name,
        out_shape=[jax.ShapeDtypeStruct((N_CHIPS,) + s.shape[2:], s.dtype) for s in srcs],
        in_specs=[hbm] * n, out_specs=[hbm] * n,
        scratch_shapes=[pltpu.SemaphoreType.DMA((N_CHIPS * n,)), pltpu.SemaphoreType.DMA((N_CHIPS * n,))],
    )(*srcs)


def _pair_sum(mine, theirs, core, *, name, tile_rows, out_dtype):
    _, _, rows, cols = mine.shape

    def body(core_ref, a_ref, b_ref, o_ref):
        o_ref[...] = (a_ref[0] + b_ref[...]).astype(out_dtype)

    return pl.pallas_call(
        body, name=name,
        grid_spec=pltpu.PrefetchScalarGridSpec(
            num_scalar_prefetch=1, grid=(N_CHIPS, rows // tile_rows),
            in_specs=[pl.BlockSpec((1, 1, tile_rows, cols), lambda q, r, core_ref: (q, core_ref[0], r, 0)),
                      pl.BlockSpec((1, tile_rows, cols), lambda q, r, core_ref: (q, r, 0))],
            out_specs=pl.BlockSpec((1, tile_rows, cols), lambda q, r, core_ref: (q, r, 0))),
        out_shape=jax.ShapeDtypeStruct((N_CHIPS, rows, cols), out_dtype),
        compiler_params=_cparams(("arbitrary", "arbitrary")),
    )(core, mine, theirs)


def _chip_exchange(srcs, *, name):
    n = len(srcs)

    def body(*refs):
        src_refs, out_refs = refs[:n], refs[n:2 * n]
        send_sems, recv_sems, local_sems = refs[2 * n:]
        x, y, c = lax.axis_index("x"), lax.axis_index("y"), lax.axis_index("c")
        my_chip = 2 * x + y
        mine = [pltpu.make_async_copy(src_refs[t].at[my_chip], out_refs[t].at[my_chip], local_sems.at[t])
                for t in range(n)]
        for cp in mine:
            cp.start()
        sends, arrivals = [], []
        for k in (3, 2, 1):
            px = 1 - x if k & 2 else x
            py = 1 - y if k & 1 else y
            peer_chip = 2 * px + py
            for t in range(n):
                sem = (k - 1) * n + t
                cp = pltpu.make_async_remote_copy(
                    src_ref=src_refs[t].at[peer_chip], dst_ref=out_refs[t].at[my_chip],
                    send_sem=send_sems.at[sem], recv_sem=recv_sems.at[sem],
                    device_id=(px, py, c), device_id_type=pl.DeviceIdType.MESH)
                cp.start()
                sends.append(cp)
                arrivals.append(pltpu.make_async_remote_copy(
                    src_ref=src_refs[t].at[peer_chip], dst_ref=out_refs[t].at[peer_chip],
                    send_sem=send_sems.at[sem], recv_sem=recv_sems.at[sem],
                    device_id=(x, y, c), device_id_type=pl.DeviceIdType.MESH))
        for cp in arrivals:
            cp.wait_recv()
        for cp in sends:
            cp.wait_send()
        for cp in mine:
            cp.wait()

    hbm = pl.BlockSpec(memory_space=pl.ANY)
    return pl.pallas_call(
        body, name=name,
        out_shape=[jax.ShapeDtypeStruct(s.shape, s.dtype) for s in srcs],
        in_specs=[hbm] * n, out_specs=[hbm] * n,
        scratch_shapes=[pltpu.SemaphoreType.DMA((3 * n,)), pltpu.SemaphoreType.DMA((3 * n,)),
                        pltpu.SemaphoreType.DMA((n,))],
    )(*srcs)


def _rope_tables(pos_col, invf_row):
    ang = pos_col.astype(F32) * invf_row
    lane = lax.broadcasted_iota(jnp.int32, ang.shape, 1)
    cos, sin = jnp.cos(ang), jnp.sin(ang)
    first = (lane >= KR_LO) & (lane < KR_LO + HALF)
    second = (lane >= KR_LO + HALF) & (lane < KR_LO + ROPE)
    return cos, jnp.where(first, sin, 0.0), jnp.where(second, sin, 0.0)


def _rope(t, cos, sin_first, sin_second, sign):
    up = pltpu.roll(t, LANES - HALF, 1)
    down = pltpu.roll(t, HALF, 1)
    return t * cos - sign * (up * sin_first) + sign * (down * sin_second)


def _fwd_proj(x, pos_col, invf_row, wp_in, w_heads, q_g, kv_g):
    t = x.shape[0]
    tm = PROJ_TILE

    def body(x_ref, pos_ref, invf_ref, win_ref, wh_ref, qg_ref, kvg_ref,
             proj_ref, q_ref, k_ref, v_ref, vt_ref):
        proj = _dot(x_ref[...].astype(BF16), win_ref[...])
        proj_ref[...] = proj
        c_q = proj[:, :Q_LORA]
        c_kv = proj[:, Q_LORA:Q_LORA + KV_LORA]
        kr_raw = proj[:, Q_LORA + KV_LORA:Q_LORA + KV_LORA + LANES]
        cqn = (c_q * lax.rsqrt(jnp.mean(c_q * c_q, axis=-1, keepdims=True) + EPS) * qg_ref[...]).astype(BF16)
        ckvn = (c_kv * lax.rsqrt(jnp.mean(c_kv * c_kv, axis=-1, keepdims=True) + EPS) * kvg_ref[...]).astype(BF16)
        cos, s1, s2 = _rope_tables(pos_ref[...], invf_ref[...])
        kr = _rope(kr_raw, cos, s1, s2, 1.0)
        lane = lax.broadcasted_iota(jnp.int32, (tm, HEAD_PAD), 1)
        for h in range(HEADS):
            q_h = _dot(cqn, wh_ref[h, :Q_LORA, :])
            kv_h = _dot(ckvn, wh_ref[h, Q_LORA:, :])
            q_ref[h] = (_rope(q_h, cos, s1, s2, 1.0) * Q_PRESCALE).astype(BF16)
            k_ref[h] = jnp.where(lane < NOPE, kv_h, kr).astype(BF16)
            v_ref[h] = kv_h.astype(BF16)
            vt_ref[h] = jnp.transpose(kv_h).astype(BF16)

    full = lambda a: pl.BlockSpec(a.shape, lambda i: (0,) * a.ndim)
    head_spec = pl.BlockSpec((HEADS, tm, HEAD_PAD), lambda i: (0, i, 0))
    head_shape = jax.ShapeDtypeStruct((HEADS, t, HEAD_PAD), BF16)
    return pl.pallas_call(
        body, name="fwd_proj", grid=(t // tm,),
        in_specs=[pl.BlockSpec((tm, D_MODEL), lambda i: (i, 0)), pl.BlockSpec((tm, 1), lambda i: (i, 0)),
                  full(invf_row), full(wp_in), full(w_heads), full(q_g), full(kv_g)],
        out_specs=[pl.BlockSpec((tm, D_IN_PAD), lambda i: (i, 0)), head_spec, head_spec, head_spec,
                   pl.BlockSpec((HEADS, HEAD_PAD, tm), lambda i: (0, 0, i))],
        out_shape=[jax.ShapeDtypeStruct((t, D_IN_PAD), F32), head_shape, head_shape, head_shape,
                   jax.ShapeDtypeStruct((HEADS, HEAD_PAD, t), BF16)],
        compiler_params=_cparams(("arbitrary",)),
    )(x, pos_col, invf_row, wp_in, w_heads, q_g, kv_g)


def _attn_fwd(q, k, vt):
    t = q.shape[1]
    bq, bk = ATTN_FWD_WIDE, ATTN_NARROW
    n_diag = bq // bk
    chunk = SOFTMAX_ROWS

    def body(q_ref, k_ref, vt_ref, o_ref, lse_ref, s0, s1, p0, p1, x0, x1, m_scr, l_scr, a_scr, acc_scr):
        i = pl.program_id(1)
        at = lambda j: pl.ds(pl.multiple_of(j * bk, bk), bk)

        def exp_pass(s_in, block_max, p_out, diagonal=False, cols=slice(None)):
            width = bq if cols == slice(None) else cols.stop - cols.start

            def load(r):
                s = s_in[r:r + chunk, cols]
                if diagonal:
                    key = lax.broadcasted_iota(jnp.int32, (chunk, width), 0) + r
                    qry = lax.broadcasted_iota(jnp.int32, (chunk, width), 1)
                    s = jnp.where(qry >= key, s, -jnp.inf)
                return s

            if diagonal:
                block_max = jnp.max(load(0), axis=0, keepdims=True)
                for r in range(chunk, bk, chunk):
                    block_max = jnp.maximum(block_max, jnp.max(load(r), axis=0, keepdims=True))
            m_old = m_scr[:, cols]
            m_new = jnp.maximum(m_old, block_max)
            alpha = jnp.exp2(m_old - m_new)
            total = jnp.zeros((1, width), F32)
            for r in range(0, bk, chunk):
                p = jnp.exp2(load(r) - m_new)
                p_out[r:r + chunk, cols] = p.astype(BF16)
                total = total + jnp.sum(p, axis=0, keepdims=True)
            m_scr[:, cols] = m_new
            l_scr[:, cols] = alpha * l_scr[:, cols] + total
            return alpha

        def scores(j, s_out, x_out):
            s = _dot_nt(k_ref[0, at(j), :], q_ref[0])
            s_out[...] = s
            x_out[...] = jnp.max(s, axis=0, keepdims=True)

        def value_product(j, p_in):
            return _dot(vt_ref[0, :, at(j)], p_in[...])

        def one_pass(j, s_in, x_in, s_out, x_out, p_prev, p_cur):
            scores(j + 1, s_out, x_out)
            acc_scr[...] = a_scr[...] * acc_scr[...] + value_product(jnp.maximum(j - 1, 0), p_prev)
            a_scr[...] = exp_pass(s_in, x_in[...], p_cur)

        scores(0, s0, x0)
        p1[...] = jnp.zeros_like(p1)
        a_scr[...] = jnp.ones_like(a_scr)
        m_scr[...] = jnp.full(m_scr.shape, -jnp.inf, F32)
        l_scr[...] = jnp.zeros_like(l_scr)
        acc_scr[...] = jnp.zeros_like(acc_scr)

        def two_passes(n, _):
            one_pass(2 * n, s0, x0, s1, x1, p1, p0)
            one_pass(2 * n + 1, s1, x1, s0, x0, p0, p1)
            return 0

        lax.fori_loop(0, (n_diag // 2) * i, two_passes, 0)
        d = n_diag * i
        alpha, p_prev, cols = a_scr[...], p1, slice(0, bq)
        for u in range(n_diag + 1):
            s_in, s_next, p_cur = (s0, s1, p0) if u % 2 == 0 else (s1, s0, p1)
            if u + 1 < n_diag:
                ahead = slice((u + 1) * bk, bq)
                s_next[:, ahead] = _dot_nt(k_ref[0, at(d + u + 1), :], q_ref[0, ahead, :])
            acc_scr[:, cols] = alpha * acc_scr[:, cols] + _dot(vt_ref[0, :, at(jnp.maximum(d + u - 1, 0))], p_prev[:, cols])
            if u < n_diag:
                cols = slice(u * bk, bq)
                alpha = exp_pass(s_in, None, p_cur, diagonal=True, cols=cols)
                p_prev = p_cur
        o_ref[0] = jnp.transpose(acc_scr[...] / l_scr[...])
        lse_ref[0] = m_scr[...] + jnp.log2(l_scr[...])

    tile = lambda dtype: pltpu.VMEM((bk, bq), dtype)
    stat = pltpu.VMEM((1, bq), F32)
    return pl.pallas_call(
        body, name="attn_fwd", grid=(HEADS, t // bq),
        in_specs=[pl.BlockSpec((1, bq, HEAD_PAD), lambda h, i: (h, i, 0)),
                  pl.BlockSpec((1, t, HEAD_PAD), lambda h, i: (h, 0, 0)),
                  pl.BlockSpec((1, HEAD_PAD, t), lambda h, i: (h, 0, 0))],
        out_specs=[pl.BlockSpec((1, bq, HEAD_PAD), lambda h, i: (h, i, 0)),
                   pl.BlockSpec((1, 1, bq), lambda h, i: (h, 0, i))],
        out_shape=[jax.ShapeDtypeStruct((HEADS, t, HEAD_PAD), F32), jax.ShapeDtypeStruct((HEADS, 1, t), F32)],
        scratch_shapes=[tile(F32), tile(F32), tile(BF16), tile(BF16), stat, stat, stat, stat, stat,
                        pltpu.VMEM((HEAD_PAD, bq), F32)],
        compiler_params=_cparams(("arbitrary", "arbitrary")),
    )(q, k, vt)


def _mid(x, target, proj, ol, w_out, ws_low, ws_low_t, bsp, sgu_g, sgu_b, ln_g, ln_b):
    t = x.shape[0]
    tm = TOKEN_TILE
    n_steps = t // tm

    def body(x_ref, tgt_ref, za_ref, u_ref, v_ref, zb_ref, ol_ref, wout_ref, ws_ref, wst_ref, bsp_ref,
             sg_ref, sb_ref, lg_ref, lb_ref,
             dr_ref, do_ref, drow_ref, drest_ref, dwout_ref, dws_ref, dbs_ref, dlg_ref, dlb_ref, dsg_ref, dsb_ref,
             loss_ref, dbsp_acc):
        step = pl.program_id(0)

        @pl.when(step == 0)
        def _():
            dwout_ref[...] = jnp.zeros_like(dwout_ref)
            dws_ref[...] = jnp.zeros_like(dws_ref)
            dbs_ref[...] = jnp.zeros_like(dbs_ref)
            dlg_ref[...] = jnp.zeros_like(dlg_ref)
            dlb_ref[...] = jnp.zeros_like(dlb_ref)
            dsg_ref[...] = jnp.zeros_like(dsg_ref)
            dsb_ref[...] = jnp.zeros_like(dsb_ref)
            loss_ref[...] = jnp.zeros_like(loss_ref)
            dbsp_acc[...] = jnp.zeros_like(dbsp_acc)

        n_chunks = tm // CHUNK
        groups = G_WIDTH // LANES

        def side_by_side(a):
            return [jnp.concatenate([a[c * CHUNK:(c + 1) * CHUNK, g * LANES:(g + 1) * LANES] for c in range(n_chunks)],
                                    axis=1) for g in range(groups)]

        def by_chunk(wide):
            return jnp.concatenate([jnp.concatenate([wide[g][:, c * LANES:(c + 1) * LANES] for g in range(groups)], axis=1)
                                    for c in range(n_chunks)], axis=0)

        def own_lanes(h):
            lane = lax.broadcasted_iota(jnp.int32, (CHUNK, n_chunks * LANES), 1)
            return (lane % LANES) // G_HEAD_DIM == h % 2

        def spatial(w_ref, wide):
            return [sum(jnp.where(own_lanes(h), _dot(w_ref[h], wide[g]), 0.0) for h in (2 * g, 2 * g + 1))
                    for g in range(groups)]

        attn = jnp.concatenate([ol_ref[h][:, NOPE:] for h in range(HEADS)], axis=-1)
        za = za_ref[...]
        sig_a = _sigmoid(za)
        silu_a = za * sig_a
        out_a = attn * silu_a
        u = u_ref[...]
        ug = _gelu(u)
        vpre = v_ref[...]
        gv = _gelu(vpre)
        mu_v = jnp.mean(gv, axis=-1, keepdims=True)
        cen_v = gv - mu_v
        rstd_v = lax.rsqrt(jnp.mean(cen_v * cen_v, axis=-1, keepdims=True) + EPS)
        vhat = cen_v * rstd_v
        vg = vhat * sg_ref[...] + sb_ref[...]
        vg_b = vg.astype(BF16)
        sv = by_chunk(spatial(ws_ref, side_by_side(vg_b))) + jnp.tile(bsp_ref[...], (n_chunks, 1))
        sgu = ug * sv
        zb = zb_ref[...]
        sig_b = _sigmoid(zb)
        silu_b = zb * sig_b
        out_b = sgu * silu_b
        merged = jnp.concatenate([out_a, out_b], axis=-1).astype(BF16)
        r = DN_ALPHA * x_ref[...] + _dot(merged, wout_ref[...])
        mu = jnp.mean(r, axis=-1, keepdims=True)
        cen = r - mu
        rstd = lax.rsqrt(jnp.mean(cen * cen, axis=-1, keepdims=True) + EPS)
        xhat = cen * rstd
        hout = xhat * lg_ref[...] + lb_ref[...]
        err = hout - tgt_ref[...]
        row_loss = jnp.mean(err * err, axis=-1, keepdims=True)
        loss_ref[...] += jnp.broadcast_to(0.5 * jnp.sum(row_loss, axis=0, keepdims=True), loss_ref.shape)

        dh = err * (1.0 / D_MODEL)
        dlg_ref[...] += jnp.sum(dh * xhat, axis=0, keepdims=True)
        dlb_ref[...] += jnp.sum(dh, axis=0, keepdims=True)
        dxhat = dh * lg_ref[...]
        dr = rstd * (dxhat - jnp.mean(dxhat, axis=-1, keepdims=True)
                     - xhat * jnp.mean(dxhat * xhat, axis=-1, keepdims=True))
        dr_ref[...] = dr
        dr_b = dr.astype(BF16)
        dwout_ref[...] += _dot_tn(merged, dr_b)
        dmerged = _dot_nt(dr_b, wout_ref[...])
        d_out_a = dmerged[:, :G_WIDTH]
        d_out_b = dmerged[:, G_WIDTH:]
        dattn = d_out_a * silu_a
        for h in range(HEADS):
            do_h = dattn[:, h * VDIM:(h + 1) * VDIM]
            do_ref[h] = jnp.concatenate([jnp.zeros((tm, NOPE), F32), do_h], axis=-1).astype(BF16)
        feature = lax.broadcasted_iota(jnp.int32, (G_WIDTH, LANES), 0) // VDIM
        column = lax.broadcasted_iota(jnp.int32, (G_WIDTH, LANES), 1)
        head_sums = jnp.dot(dattn * attn, jnp.where(feature == column, 1.0, 0.0).astype(F32),
                            preferred_element_type=F32, precision=lax.Precision.HIGHEST)
        dsums_t = jnp.transpose(head_sums)
        for h in range(HEADS):
            drow_ref[h] = dsums_t[h:h + 1, :]
        dza = d_out_a * attn * (sig_a * (1.0 + za * (1.0 - sig_a)))
        dsgu = d_out_b * silu_b
        dzb = d_out_b * sgu * (sig_b * (1.0 + zb * (1.0 - sig_b)))
        du = dsgu * sv * _gelu_grad(u)
        dsv = dsgu * ug
        dsv_b = dsv.astype(BF16)
        for cix in range(n_chunks):
            dbsp_acc[...] += dsv[cix * CHUNK:(cix + 1) * CHUNK, :]
        dsv_wide, vg_wide = side_by_side(dsv_b), side_by_side(vg_b)
        dvg = by_chunk(spatial(wst_ref, dsv_wide))
        for h in range(HEADS):
            mine = jnp.where(own_lanes(h), dsv_wide[h // 2], jnp.zeros_like(dsv_wide[h // 2]))
            dws_ref[h] += _dot_nt(mine, vg_wide[h // 2])
        dsg_ref[...] += jnp.sum(dvg * vhat, axis=0, keepdims=True)
        dsb_ref[...] += jnp.sum(dvg, axis=0, keepdims=True)
        dvhat = dvg * sg_ref[...]
        dgv = rstd_v * (dvhat - jnp.mean(dvhat, axis=-1, keepdims=True)
                        - vhat * jnp.mean(dvhat * vhat, axis=-1, keepdims=True))
        dv = dgv * _gelu_grad(vpre)
        drest_ref[...] = jnp.concatenate([dza, du, dv, dzb], axis=-1).astype(BF16)

        @pl.when(step == n_steps - 1)
        def _():
            tri = (lax.broadcasted_iota(jnp.int32, (CHUNK, CHUNK), 0)
                   >= lax.broadcasted_iota(jnp.int32, (CHUNK, CHUNK), 1))
            for h in range(HEADS):
                dws_ref[h] = jnp.where(tri, dws_ref[h], 0.0)
            tot = dbsp_acc[...]
            lane = lax.broadcasted_iota(jnp.int32, (CHUNK, LANES), 1)
            dbs = jnp.zeros((CHUNK, LANES), F32)
            for h in range(HEADS):
                head_sum = jnp.sum(tot[:, h * G_HEAD_DIM:(h + 1) * G_HEAD_DIM], axis=-1, keepdims=True)
                dbs = jnp.where(lane == h, head_sum, dbs)
            dbs_ref[...] = dbs

    full = lambda a: pl.BlockSpec(a.shape, lambda i: (0,) * a.ndim)
    tile = lambda w, j=0: pl.BlockSpec((tm, w), lambda i, j=j: (i, j))
    heads = pl.BlockSpec((HEADS, tm, HEAD_PAD), lambda i: (0, i, 0))
    acc = lambda shape: (pl.BlockSpec(shape, lambda i: (0,) * len(shape)), jax.ShapeDtypeStruct(shape, F32))
    accs = [acc((D_MODEL, D_MODEL)), acc((HEADS, CHUNK, CHUNK)), acc((CHUNK, LANES)), acc((1, D_MODEL)),
            acc((1, D_MODEL)), acc((1, G_WIDTH)), acc((1, G_WIDTH)), acc((1, LANES))]
    return pl.pallas_call(
        body, name="mid", grid=(n_steps,),
        in_specs=[tile(D_MODEL), tile(D_MODEL), tile(G_WIDTH, 1), tile(G_WIDTH, 2), tile(G_WIDTH, 3), tile(G_WIDTH, 4),
                  heads, full(w_out), full(ws_low), full(ws_low_t), full(bsp), full(sgu_g), full(sgu_b),
                  full(ln_g), full(ln_b)],
        out_specs=[tile(D_MODEL), heads, pl.BlockSpec((HEADS, 1, tm), lambda i: (0, 0, i)), tile(4 * G_WIDTH)]
        + [a[0] for a in accs],
        out_shape=[jax.ShapeDtypeStruct((t, D_MODEL), F32), jax.ShapeDtypeStruct((HEADS, t, HEAD_PAD), BF16),
                   jax.ShapeDtypeStruct((HEADS, 1, t), F32), jax.ShapeDtypeStruct((t, 4 * G_WIDTH), BF16)]
        + [a[1] for a in accs],
        scratch_shapes=[pltpu.VMEM((CHUNK, G_WIDTH), F32)],
        compiler_params=_cparams(("arbitrary",)),
    )(x, target, proj, proj, proj, proj, ol, w_out, ws_low, ws_low_t, bsp, sgu_g, sgu_b, ln_g, ln_b)


def _attn_bwd(q, k, v, do, lse_row, d_row):
    t = q.shape[1]
    bk, bq = ATTN_BWD_WIDE, ATTN_NARROW
    n_diag = bk // bq
    last = t // bq - 1
    chunk = SOFTMAX_ROWS

    def body(q_ref, k_ref, v_ref, do_ref, lse_ref, drow_ref, dqt_ref, dk_ref, dv_ref,
             s0, s1, e0, e1, p0, p1, g0, g1, kt_scr):
        j = pl.program_id(1)
        at = lambda i: pl.ds(pl.multiple_of(i * bq, bq), bq)

        @pl.when(j == 0)
        def _():
            dqt_ref[...] = jnp.zeros_like(dqt_ref)

        kt_scr[...] = jnp.transpose(k_ref[0].astype(F32)).astype(BF16)
        dk_ref[...] = jnp.zeros_like(dk_ref)
        dv_ref[...] = jnp.zeros_like(dv_ref)

        def products(i, s_out, e_out, keys=slice(0, bk)):
            i = jnp.minimum(i, last)
            s_out[keys, :] = _dot_nt(k_ref[0, keys, :], q_ref[0, at(i), :])
            e_out[keys, :] = _dot_nt(v_ref[0, keys, :], do_ref[0, at(i), :])

        def gradients(i, p_in, g_in, keys=slice(0, bk)):
            dv_ref[0, keys, :] += _dot(p_in[keys, :], do_ref[0, at(i), :])
            dk_ref[0, keys, :] += _dot(g_in[keys, :], q_ref[0, at(i), :])
            dqt_ref[0, :, at(i)] += _dot(kt_scr[:, keys], g_in[keys, :])

        def elementwise(i, s_in, e_in, p_out, g_out, qry0=None, keys=slice(0, bk)):
            lse = lse_ref[0, :, at(i)]
            dsum = drow_ref[0, :, at(i)]
            for r in range(keys.start, keys.stop, chunk):
                p = jnp.exp2(s_in[r:r + chunk, :] - lse)
                if qry0 is not None:
                    key = lax.broadcasted_iota(jnp.int32, (chunk, bq), 0) + r
                    qry = lax.broadcasted_iota(jnp.int32, (chunk, bq), 1) + qry0
                    p = jnp.where(qry >= key, p, 0.0)
                p_out[r:r + chunk, :] = p.astype(BF16)
                g_out[r:r + chunk, :] = (p * (e_in[r:r + chunk, :] - dsum)).astype(BF16)

        def one_pass(i, s_in, e_in, s_out, e_out, p_prev, g_prev, p_cur, g_cur):
            products(i + 1, s_out, e_out)
            gradients(i - 1, p_prev, g_prev)
            elementwise(i, s_in, e_in, p_cur, g_cur)

        first = n_diag * j
        keys_of = lambda u: slice(0, min((u + 1) * bq, bk))
        even, odd = (s0, e0, p0, g0), (s1, e1, p1, g1)
        products(first, s0, e0, keys_of(0))
        products(first + 1, s1, e1, keys_of(1))
        elementwise(first, s0, e0, p0, g0, qry0=0, keys=keys_of(0))
        for u in range(1, n_diag):
            (s_in, e_in, p_cur, g_cur), (s_out, e_out, p_prev, g_prev) = (odd, even) if u % 2 else (even, odd)
            products(first + u + 1, s_out, e_out, keys_of(u + 1))
            gradients(first + u - 1, p_prev, g_prev, keys_of(u - 1))
            elementwise(first + u, s_in, e_in, p_cur, g_cur, qry0=u * bq, keys=keys_of(u))

        def two_passes(n, _):
            i = first + n_diag + 2 * n
            one_pass(i, s0, e0, s1, e1, p1, g1, p0, g0)
            one_pass(i + 1, s1, e1, s0, e0, p0, g0, p1, g1)
            return 0

        lax.fori_loop(0, (last - first - n_diag + 1) // 2, two_passes, 0)
        gradients(last, p1, g1)
        dk_ref[0] = dk_ref[0] * LN2

    whole = pl.BlockSpec((1, t, HEAD_PAD), lambda h, j: (h, 0, 0), pipeline_mode=pl.Buffered(1))
    block = pl.BlockSpec((1, bk, HEAD_PAD), lambda h, j: (h, j, 0))
    rows = pl.BlockSpec((1, 1, t), lambda h, j: (h, 0, 0), pipeline_mode=pl.Buffered(1))
    shape = jax.ShapeDtypeStruct((HEADS, t, HEAD_PAD), F32)
    tile = lambda dtype: pltpu.VMEM((bk, bq), dtype)
    return pl.pallas_call(
        body, name="attn_bwd", grid=(HEADS, t // bk),
        in_specs=[whole, block, block, whole, rows, rows],
        out_specs=[pl.BlockSpec((1, HEAD_PAD, t), lambda h, j: (h, 0, 0)), block, block],
        out_shape=[jax.ShapeDtypeStruct((HEADS, HEAD_PAD, t), F32), shape, shape],
        scratch_shapes=[tile(F32), tile(F32), tile(F32), tile(F32), tile(BF16), tile(BF16),
                        tile(BF16), tile(BF16), pltpu.VMEM((HEAD_PAD, bk), BF16)],
        compiler_params=_cparams(("arbitrary", "arbitrary")),
    )(q, k, v, do, lse_row, d_row)


def _bwd_qkv(dq, dk, dv, proj, pos_col, invf_row, w_heads, q_g, kv_g):
    t = proj.shape[0]
    tm = PROJ_TILE

    def body(dq_ref, dk_ref, dv_ref, ph_ref, pos_ref, invf_ref, wh_ref, qg_ref, kvg_ref,
             dhead_ref, dwh_ref, dqg_ref, dkvg_ref):
        @pl.when(pl.program_id(0) == 0)
        def _():
            dwh_ref[...] = jnp.zeros_like(dwh_ref)
            dqg_ref[...] = jnp.zeros_like(dqg_ref)
            dkvg_ref[...] = jnp.zeros_like(dkvg_ref)

        cos, s1, s2 = _rope_tables(pos_ref[...], invf_ref[...])
        lane = lax.broadcasted_iota(jnp.int32, (tm, LANES), 1)
        c_q = ph_ref[:, :Q_LORA]
        c_kv = ph_ref[:, Q_LORA:Q_LORA + KV_LORA]
        rstd_q = lax.rsqrt(jnp.mean(c_q * c_q, axis=-1, keepdims=True) + EPS)
        rstd_kv = lax.rsqrt(jnp.mean(c_kv * c_kv, axis=-1, keepdims=True) + EPS)
        qhat = c_q * rstd_q
        kvhat = c_kv * rstd_kv
        cqn = (qhat * qg_ref[...]).astype(BF16)
        ckvn = (kvhat * kvg_ref[...]).astype(BF16)
        dcqn = jnp.zeros((tm, Q_LORA), F32)
        dckvn = jnp.zeros((tm, KV_LORA), F32)
        dkr_rot = jnp.zeros((tm, LANES), F32)
        for h in range(HEADS):
            dq_b = _rope(jnp.transpose(dq_ref[h]) * ATTN_SCALE, cos, s1, s2, -1.0).astype(BF16)
            dk_h = dk_ref[h]
            dkv_b = jnp.where(lane < NOPE, dk_h, dv_ref[h]).astype(BF16)
            dkr_rot = dkr_rot + dk_h
            dwh_ref[h, :Q_LORA, :] += _dot_tn(cqn, dq_b)
            dwh_ref[h, Q_LORA:, :] += _dot_tn(ckvn, dkv_b)
            dcqn = dcqn + _dot_nt(dq_b, wh_ref[h, :Q_LORA, :])
            dckvn = dckvn + _dot_nt(dkv_b, wh_ref[h, Q_LORA:, :])
        rot_lanes = (lane >= KR_LO) & (lane < KR_LO + ROPE)
        dkr_raw = jnp.where(rot_lanes, _rope(dkr_rot, cos, s1, s2, -1.0), 0.0)
        dqg_ref[...] += jnp.sum(dcqn * qhat, axis=0, keepdims=True)
        dkvg_ref[...] += jnp.sum(dckvn * kvhat, axis=0, keepdims=True)
        dqh = dcqn * qg_ref[...]
        dkvh = dckvn * kvg_ref[...]
        dc_q = rstd_q * (dqh - qhat * jnp.mean(dqh * qhat, axis=-1, keepdims=True))
        dc_kv = rstd_kv * (dkvh - kvhat * jnp.mean(dkvh * kvhat, axis=-1, keepdims=True))
        dhead_ref[...] = jnp.concatenate([dc_q, dc_kv, dkr_raw], axis=-1).astype(BF16)

    full = lambda a: pl.BlockSpec(a.shape, lambda i: (0,) * a.ndim)
    heads = pl.BlockSpec((HEADS, tm, HEAD_PAD), lambda i: (0, i, 0))
    acc = lambda shape: (pl.BlockSpec(shape, lambda i: (0,) * len(shape)), jax.ShapeDtypeStruct(shape, F32))
    accs = [acc(w_heads.shape), acc((1, Q_LORA)), acc((1, KV_LORA))]
    return pl.pallas_call(
        body, name="bwd_qkv", grid=(t // tm,),
        in_specs=[pl.BlockSpec((HEADS, HEAD_PAD, tm), lambda i: (0, 0, i)), heads, heads,
                  pl.BlockSpec((tm, 4 * LANES), lambda i: (i, 0)),
                  pl.BlockSpec((tm, 1), lambda i: (i, 0)), full(invf_row), full(w_heads),
                  full(q_g), full(kv_g)],
        out_specs=[pl.BlockSpec((tm, 4 * LANES), lambda i: (i, 0))] + [a[0] for a in accs],
        out_shape=[jax.ShapeDtypeStruct((t, 4 * LANES), BF16)] + [a[1] for a in accs],
        compiler_params=_cparams(("arbitrary",)),
    )(dq, dk, dv, proj, pos_col, invf_row, w_heads, q_g, kv_g)


def _bwd_in(x, dr, dhead, drest, wp_in):
    t = x.shape[0]
    tm = PROJ_TILE
    n_head = dhead.shape[1]

    def body(x_ref, dr_ref, dhead_ref, drest_ref, win_ref, gx_ref, dwin_ref):
        @pl.when(pl.program_id(0) == 0)
        def _():
            dwin_ref[...] = jnp.zeros_like(dwin_ref)

        xb = x_ref[...].astype(BF16)
        dh_b = dhead_ref[...]
        dr_b = drest_ref[...]
        gx_ref[...] = (DN_ALPHA * dr_ref[...] + _dot_nt(dh_b, win_ref[:, :n_head])
                       + _dot_nt(dr_b, win_ref[:, n_head:]))
        dwin_ref[:, :n_head] += _dot_tn(xb, dh_b)
        dwin_ref[:, n_head:] += _dot_tn(xb, dr_b)

    tile = lambda w: pl.BlockSpec((tm, w), lambda i: (i, 0))
    whole = pl.BlockSpec(wp_in.shape, lambda i: (0, 0))
    return pl.pallas_call(
        body, name="bwd_in", grid=(t // tm,),
        in_specs=[tile(D_MODEL), tile(D_MODEL), tile(n_head), tile(drest.shape[1]), whole],
        out_specs=[tile(D_MODEL), whole],
        out_shape=[jax.ShapeDtypeStruct((t, D_MODEL), F32), jax.ShapeDtypeStruct(wp_in.shape, F32)],
        compiler_params=_cparams(("arbitrary",)),
    )(x, dr, dhead, drest, wp_in)


def _adam(parts, w, m, v, *, name, tile_rows):
    n, rows, cols = parts.shape

    def body(p_ref, w_ref, m_ref, v_ref, g_ref, d_ref, nm_ref, nv_ref):
        g = p_ref[0].astype(F32)
        for s in range(1, n):
            g = g + p_ref[s].astype(F32)
        m_new = ADAM_B1 * m_ref[...] + (1.0 - ADAM_B1) * g
        v_new = ADAM_B2 * v_ref[...] + (1.0 - ADAM_B2) * (g * g)
        m_hat = m_new / (1.0 - ADAM_B1 ** ADAM_STEP)
        v_hat = v_new / (1.0 - ADAM_B2 ** ADAM_STEP)
        g_ref[...] = g
        d_ref[...] = -ADAM_LR * (m_hat / (jnp.sqrt(v_hat) + ADAM_EPS) + ADAM_WD * w_ref[...])
        nm_ref[...] = m_new
        nv_ref[...] = v_new

    flat = pl.BlockSpec((tile_rows, cols), lambda i: (i, 0))
    shape = jax.ShapeDtypeStruct((rows, cols), F32)
    return pl.pallas_call(
        body, name=name, grid=(rows // tile_rows,),
        in_specs=[pl.BlockSpec((n, tile_rows, cols), lambda i: (0, i, 0)), flat, flat, flat],
        out_specs=[flat] * 4, out_shape=[shape] * 4,
        compiler_params=_cparams(("arbitrary",)),
    )(parts, w, m, v)


SMALL_NAMES = ("q_norm_g", "kv_norm_g", "sgu_norm_g", "sgu_norm_b", "b_spatial", "ln_g", "ln_b")
SMALL_SIZES = (Q_LORA, KV_LORA, G_WIDTH, G_WIDTH, HEADS * CHUNK, D_MODEL, D_MODEL)


def _pack_small(vals, last=None):
    flat = jnp.concatenate([v.reshape(-1) for v in vals])
    pad = SMALL_LEN - flat.shape[0]
    if last is None:
        return jnp.pad(flat, (0, pad))
    return jnp.concatenate([flat, jnp.zeros((pad - 1,), F32), last.reshape(1)])


def _unpack_small(flat):
    out, at = [], 0
    for n in SMALL_SIZES:
        out.append(flat[at:at + n])
        at += n
    out[4] = out[4].reshape(HEADS, CHUNK)
    return out


UQ_SHARD = HEADS * (NOPE + ROPE) // N_DEV
HEAD_ROWS = Q_LORA + KV_LORA
MIXED_ROWS = HEAD_ROWS + CHUNK + SMALL_LEN // N_DEV // LANES


def _head_slab(w_uq_shard, w_ukv_shard):
    return jnp.concatenate([jnp.pad(w_uq_shard, ((0, 0), (0, LANES - UQ_SHARD))), w_ukv_shard])


IN_SHARD = D_IN // N_DEV


def _w_in_pieces():
    split = Q_LORA + KV_LORA
    moves = ((0, split, 0), (split, split + ROPE, KR_LO), (split + ROPE, D_IN, LANES - ROPE))
    pieces = []
    for s in range(N_DEV):
        lo, hi = s * IN_SHARD, (s + 1) * IN_SHARD
        for a, b, shift in moves:
            a, b = max(a, lo), min(b, hi)
            if a < b:
                pieces.append((s, a - lo, a + shift, b - a))
    return pieces


def _padded_w_in(shards):
    tr = TOKEN_TILE

    def body(sh_ref, o_ref):
        o_ref[...] = jnp.zeros_like(o_ref)
        for s, src, dst, width in _w_in_pieces():
            o_ref[:, dst:dst + width] = sh_ref[s, :, src:src + width]

    return pl.pallas_call(
        body, name="w_in_pad", grid=(D_MODEL // tr,),
        in_specs=[pl.BlockSpec((N_DEV, tr, IN_SHARD), lambda i: (0, i, 0))],
        out_specs=pl.BlockSpec((tr, D_IN_PAD), lambda i: (i, 0)),
        out_shape=jax.ShapeDtypeStruct((D_MODEL, D_IN_PAD), shards.dtype),
        compiler_params=_cparams(("arbitrary",)),
    )(shards)


def _w_in_shards(dwp_in):
    tr = TOKEN_TILE
    by_shard = [[p for p in _w_in_pieces() if p[0] == s] for s in range(N_DEV)]

    def body(w_ref, o_ref):
        for s, pieces in enumerate(by_shard):
            parts = [w_ref[:, dst:dst + width] for _, _, dst, width in pieces]
            o_ref[s] = parts[0] if len(parts) == 1 else jnp.concatenate(parts, axis=1)

    return pl.pallas_call(
        body, name="w_in_split", grid=(D_MODEL // tr,),
        in_specs=[pl.BlockSpec((tr, D_IN_PAD), lambda i: (i, 0))],
        out_specs=pl.BlockSpec((N_DEV, tr, IN_SHARD), lambda i: (0, i, 0)),
        out_shape=jax.ShapeDtypeStruct((N_DEV, D_MODEL, IN_SHARD), dwp_in.dtype),
        compiler_params=_cparams(("arbitrary",)),
    )(dwp_in)


def kernel(x, positions, w_in, q_norm_g, w_uq, kv_norm_g, w_ukv, sgu_norm_g, sgu_norm_b, w_spatial, b_spatial, w_out, ln_g, ln_b, loss_target, m_w_in, m_q_norm_g, m_w_uq, m_kv_norm_g, m_w_ukv, m_sgu_norm_g, m_sgu_norm_b, m_w_spatial, m_b_spatial, m_w_out, m_ln_g, m_ln_b, v_w_in, v_q_norm_g, v_w_uq, v_kv_norm_g, v_w_ukv, v_sgu_norm_g, v_sgu_norm_b, v_w_spatial, v_b_spatial, v_w_out, v_ln_g, v_ln_b):
    me = 4 * lax.axis_index("x") + 2 * lax.axis_index("y") + lax.axis_index("c")
    seq = x.shape[1]
    x2 = x.reshape(seq, D_MODEL)
    tgt2 = loss_target.reshape(seq, D_MODEL)
    pos_col = positions.reshape(seq, 1)

    w_in_shards, w_out_shards, w_heads = _gather_two_level(
        [w_in.astype(BF16), w_out.astype(BF16), _head_slab(w_uq, w_ukv).astype(BF16)],
        name="wgather")
    (loss_part, grad_x, d_in, d_heads, d_out, d_ws, d_bs_t, d_lng, d_lnb, d_sgug, d_sgub, d_qg, d_kvg) = _local_step(
        x2, tgt2, pos_col, w_in_shards, w_heads, w_out_shards.reshape(D_MODEL, D_MODEL), q_norm_g, kv_norm_g,
        sgu_norm_g, sgu_norm_b, w_spatial, b_spatial, ln_g, ln_b)

    small_part = _pack_small([d_qg, d_kvg, d_sgug, d_sgub, d_bs_t[:, :HEADS].T, d_lng, d_lnb], last=loss_part[0, :1])
    mixed = jnp.concatenate([d_heads, d_ws, small_part.reshape(N_DEV, -1, LANES)], axis=1)
    by_chip = [g.reshape((N_CHIPS, 2) + g.shape[1:])
               for g in (d_in, d_out.reshape(N_DEV, D_MODEL // N_DEV, D_MODEL), mixed)]
    from_sibling = _sibling_swap(by_chip, name="gswap")
    core = lax.axis_index("c").astype(jnp.int32).reshape(1)
    pair_sums = [_pair_sum(a, b, core, name=nm, tile_rows=tr, out_dtype=dt) for a, b, nm, tr, dt in zip(
        by_chip, from_sibling, ("gsum_in", "gsum_out", "gsum_mixed"), (TOKEN_TILE, D_MODEL // N_DEV, MIXED_ROWS),
        (BF16, BF16, F32))]
    recv_in, recv_out, recv_mixed = _chip_exchange(pair_sums, name="gexch")

    take = lambda a: lax.dynamic_index_in_dim(a, me, 0, keepdims=False)
    small_w = _pack_small([q_norm_g, kv_norm_g, sgu_norm_g, sgu_norm_b, b_spatial, ln_g, ln_b])
    small_m = _pack_small([m_q_norm_g, m_kv_norm_g, m_sgu_norm_g, m_sgu_norm_b, m_b_spatial, m_ln_g, m_ln_b])
    small_v = _pack_small([v_q_norm_g, v_kv_norm_g, v_sgu_norm_g, v_sgu_norm_b, v_b_spatial, v_ln_g, v_ln_b])
    own_mixed = lambda uq, ukv, sp, small: jnp.concatenate(
        [_head_slab(uq, ukv), take(sp), take(small.reshape(N_DEV, -1, LANES))])
    res_in = _adam(recv_in, w_in, m_w_in, v_w_in, name="adam_in", tile_rows=TOKEN_TILE)
    res_out = _adam(recv_out, w_out, m_w_out, v_w_out, name="adam_out", tile_rows=D_MODEL // N_DEV)
    res_mixed = _adam(recv_mixed, own_mixed(w_uq, w_ukv, w_spatial, small_w), own_mixed(m_w_uq, m_w_ukv, m_w_spatial, small_m),
                      own_mixed(v_w_uq, v_w_ukv, v_w_spatial, small_v), name="adam_mixed", tile_rows=MIXED_ROWS)

    rep_g, = _exchange([res_mixed[0][HEAD_ROWS:]], name="sgather", per_destination=False)
    rep_pack = lambda sp, small: jnp.concatenate(
        [sp.reshape(N_DEV, CHUNK, LANES), small.reshape(N_DEV, -1, LANES)], axis=1).reshape(-1, LANES)
    _, delta_rep, m_rep, v_rep = _adam(rep_g.reshape(1, N_DEV * REP_ROWS, LANES), rep_pack(w_spatial, small_w),
                                       rep_pack(m_w_spatial, small_m), rep_pack(v_w_spatial, small_v),
                                       name="adam_rep", tile_rows=N_DEV * REP_ROWS)

    def rep_unpack(a):
        a = a.reshape(N_DEV, REP_ROWS, LANES)
        small = _unpack_small(a[:, CHUNK:].reshape(-1))
        return [small[0], small[1], small[2], small[3], a[:, :CHUNK], small[4], small[5], small[6]]

    def ordered(which, rep):
        r_qg, r_kvg, r_sg, r_sb, r_ws, r_bs, r_lg, r_lb = rep_unpack(rep)
        heads = res_mixed[which]
        return [res_in[which], r_qg, heads[:Q_LORA, :UQ_SHARD], r_kvg, heads[Q_LORA:HEAD_ROWS], r_sg, r_sb, r_ws, r_bs,
                res_out[which], r_lg, r_lb]

    loss = rep_g[N_DEV - 1, REP_ROWS - 1, LANES - 1]
    outs = [loss, grad_x.reshape(x.shape)]
    outs += ordered(0, rep_g.reshape(-1, LANES))
    outs += ordered(1, delta_rep)
    outs += ordered(2, m_rep)
    outs += ordered(3, v_rep)
    return tuple(outs)


def _local_step(x2, tgt2, pos_col, w_in_shards, w_heads, w_out_full, q_norm_g, kv_norm_g, sgu_norm_g, sgu_norm_b,
                w_spatial, b_spatial, ln_g, ln_b):
    wp_in = _padded_w_in(w_in_shards)

    half = jnp.arange(HALF, dtype=F32)
    inv_freq = 1.0 / (ROPE_THETA ** (half / HALF))
    invf_row = jnp.concatenate([jnp.zeros((KR_LO,), F32), inv_freq, inv_freq,
                                jnp.zeros((LANES - KR_LO - ROPE,), F32)]).reshape(1, LANES)
    tri = jnp.tril(jnp.ones((CHUNK, CHUNK), dtype=bool))
    ws_low = jnp.where(tri[None], w_spatial, 0.0).astype(BF16)
    ws_low_t = ws_low.transpose(0, 2, 1)
    bsp = jnp.repeat(b_spatial.T, G_HEAD_DIM, axis=1)
    row = lambda a: a.reshape(1, -1)

    proj, q, k, v, vt = _fwd_proj(x2, pos_col, invf_row, wp_in, w_heads, row(q_norm_g), row(kv_norm_g))
    o, lse_row = _attn_fwd(q, k, vt)
    (dr, do, d_row, drest, d_out, d_ws, d_bs_t, d_lng, d_lnb, d_sgug, d_sgub, loss_part) = _mid(
        x2, tgt2, proj, o, w_out_full, ws_low, ws_low_t, bsp, row(sgu_norm_g), row(sgu_norm_b), row(ln_g), row(ln_b))
    dqt, dk, dv = _attn_bwd(q, k, v, do, lse_row, d_row)
    dhead, d_heads, d_qg, d_kvg = _bwd_qkv(dqt, dk, dv, proj, pos_col, invf_row, w_heads, row(q_norm_g), row(kv_norm_g))
    grad_x, dwp_in = _bwd_in(x2, dr, dhead, drest, wp_in)
    return (loss_part, grad_x, _w_in_shards(dwp_in), d_heads, d_out, d_ws, d_bs_t, d_lng, d_lnb, d_sgug, d_sgub,
            d_qg, d_kvg)
```

```python
import functools
import math

import jax
import jax.numpy as jnp
from jax import lax
from jax.experimental import pallas as pl
from jax.experimental.pallas import tpu as pltpu

F32 = jnp.float32
BF16 = jnp.bfloat16

N_DEV = 8
D_MODEL = 1024
HEADS = 8
NOPE = 64
ROPE = 32
HALF = ROPE // 2
VDIM = 64
Q_LORA = 256
KV_LORA = 128
G_WIDTH = 512
G_HEAD_DIM = 64
CHUNK = 128
HEAD_PAD = 128
D_IN = 2464
D_IN_PAD = 2560
KR_LO = NOPE
ROPE_THETA = 10000.0
DN_ALPHA = 2.0 ** 0.25
EPS = 1e-5
ATTN_SCALE = 1.0 / math.sqrt(NOPE + ROPE)
ADAM_LR, ADAM_B1, ADAM_B2, ADAM_EPS, ADAM_WD, ADAM_STEP = 0.001, 0.9, 0.999, 1e-08, 0.01, 10

LANES = 128
REP_ROWS = 136
SMALL_LEN = 8192
VMEM_LIMIT = 56 * 1024 * 1024
ATTN_BWD_VMEM_LIMIT = 61 * 1024 * 1024

TOKEN_TILE = 256
PROJ_TILE = 512
ATTN_FWD_WIDE = 2048
ATTN_BWD_WIDE = 2048
ATTN_NARROW = 512
SOFTMAX_ROWS = 512
LOG2E = 1.4426950408889634
LN2 = 0.6931471805599453
Q_PRESCALE = ATTN_SCALE * LOG2E


def _cparams(sem=None, vmem_limit=VMEM_LIMIT):
    return pltpu.CompilerParams(dimension_semantics=sem, vmem_limit_bytes=vmem_limit)


def _dot(a, b):
    return jnp.dot(a, b, preferred_element_type=F32)


def _dot_nt(a, b):
    return lax.dot_general(a, b, (((1,), (1,)), ((), ())), preferred_element_type=F32)


def _dot_tn(a, b):
    return lax.dot_general(a, b, (((0,), (0,)), ((), ())), preferred_element_type=F32)


def _as_row(col):
    return jnp.transpose(jnp.broadcast_to(col, (col.shape[0], LANES)))[0:1, :]


def _sigmoid(z):
    return 1.0 / (1.0 + jnp.exp(-z))


def _gelu(x):
    return 0.5 * x * (1.0 + lax.erf(x * 0.7071067811865476))


def _gelu_grad(x):
    cdf = 0.5 * (1.0 + lax.erf(x * 0.7071067811865476))
    return cdf + x * jnp.exp(-0.5 * x * x) * 0.3989422804014327


def _exchange(srcs, *, name, per_destination):
    n = len(srcs)
    slab_shapes = [s.shape[1:] if per_destination else s.shape for s in srcs]

    def body(*refs):
        src_refs, out_refs = refs[:n], refs[n:2 * n]
        send_sems, recv_sems, local_sems = refs[2 * n:]
        x, y, c = lax.axis_index("x"), lax.axis_index("y"), lax.axis_index("c")
        me = 4 * x + 2 * y + c

        def slab_for(t, dest):
            return src_refs[t].at[dest] if per_destination else src_refs[t]

        mine = [pltpu.make_async_copy(slab_for(t, me), out_refs[t].at[me], local_sems.at[t]) for t in range(n)]
        for cp in mine:
            cp.start()
        sends, arrivals = [], []
        for k in (6, 7, 4, 5, 2, 3, 1):
            px = 1 - x if k & 4 else x
            py = 1 - y if k & 2 else y
            pc = 1 - c if k & 1 else c
            peer = 4 * px + 2 * py + pc
            for t in range(n):
                sem = (k - 1) * n + t
                cp = pltpu.make_async_remote_copy(
                    src_ref=slab_for(t, peer), dst_ref=out_refs[t].at[me],
                    send_sem=send_sems.at[sem], recv_sem=recv_sems.at[sem],
                    device_id=(px, py, pc), device_id_type=pl.DeviceIdType.MESH)
                cp.start()
                sends.append(cp)
                arrivals.append(pltpu.make_async_remote_copy(
                    src_ref=slab_for(t, peer), dst_ref=out_refs[t].at[peer],
                    send_sem=send_sems.at[sem], recv_sem=recv_sems.at[sem],
                    device_id=(x, y, c), device_id_type=pl.DeviceIdType.MESH))
        for cp in arrivals:
            cp.wait_recv()
        for cp in sends:
            cp.wait_send()
        for cp in mine:
            cp.wait()

    hbm = pl.BlockSpec(memory_space=pl.ANY)
    return pl.pallas_call(
        body, name=name,
        out_shape=[jax.ShapeDtypeStruct((N_DEV,) + tuple(shape), s.dtype) for shape, s in zip(slab_shapes, srcs)],
        in_specs=[hbm] * n, out_specs=[hbm] * n,
        scratch_shapes=[pltpu.SemaphoreType.DMA(((N_DEV - 1) * n,)), pltpu.SemaphoreType.DMA(((N_DEV - 1) * n,)),
                        pltpu.SemaphoreType.DMA((n,))],
    )(*srcs)


def _gather_two_level(srcs, *, name):
    n = len(srcs)

    def body(*refs):
        src_refs, out_refs = refs[:n], refs[n:2 * n]
        send_sems, recv_sems, local_sems = refs[2 * n:]
        x, y, c = lax.axis_index("x"), lax.axis_index("y"), lax.axis_index("c")
        me, sibling = (x, y, c), (x, y, 1 - c)
        chips = [(1 - x, 1 - y), (1 - x, y), (x, 1 - y)]
        index = lambda px, py, pc: 4 * px + 2 * py + pc

        def copy(k, t, block, to, src=None):
            place = out_refs[t].at[index(*block)]
            return pltpu.make_async_remote_copy(
                src_ref=place if src is None else src, dst_ref=place,
                send_sem=send_sems.at[k * n + t], recv_sem=recv_sems.at[k * n + t],
                device_id=to, device_id_type=pl.DeviceIdType.MESH)

        mine = [pltpu.make_async_copy(src_refs[t], out_refs[t].at[index(*me)], local_sems.at[t]) for t in range(n)]
        for cp in mine:
            cp.start()
        first = [copy(1 + j, t, me, (*chip, c), src=src_refs[t]) for j, chip in enumerate(chips) for t in range(n)]
        first += [copy(0, t, me, sibling, src=src_refs[t]) for t in range(n)]
        for cp in first:
            cp.start()
        passed = []
        for j, chip in enumerate(chips):
            for t in range(n):
                copy(1 + j, t, (*chip, c), me).wait_recv()
                cp = copy(4 + j, t, (*chip, c), sibling)
                cp.start()
                passed.append(cp)
        for t in range(n):
            copy(0, t, sibling, me).wait_recv()
        for j, chip in enumerate(chips):
            for t in range(n):
                copy(4 + j, t, (*chip, 1 - c), me).wait_recv()
        for cp in first + passed:
            cp.wait_send()
        for cp in mine:
            cp.wait()

    hbm = pl.BlockSpec(memory_space=pl.ANY)
    return pl.pallas_call(
        body, name=name,
        out_shape=[jax.ShapeDtypeStruct((N_DEV,) + s.shape, s.dtype) for s in srcs],
        in_specs=[hbm] * n, out_specs=[hbm] * n,
        scratch_shapes=[pltpu.SemaphoreType.DMA((7 * n,)), pltpu.SemaphoreType.DMA((7 * n,)),
                        pltpu.SemaphoreType.DMA((n,))],
    )(*srcs)


N_CHIPS = N_DEV // 2


def _sibling_swap(srcs, *, name):
    n = len(srcs)

    def body(*refs):
        src_refs, out_refs = refs[:n], refs[n:2 * n]
        send_sems, recv_sems = refs[2 * n:]
        x, y, c = lax.axis_index("x"), lax.axis_index("y"), lax.axis_index("c")
        sends = []
        for chip in range(N_CHIPS):
            for t in range(n):
                cp = pltpu.make_async_remote_copy(
                    src_ref=src_refs[t].at[chip, 1 - c], dst_ref=out_refs[t].at[chip],
                    send_sem=send_sems.at[chip * n + t], recv_sem=recv_sems.at[chip * n + t],
                    device_id=(x, y, 1 - c), device_id_type=pl.DeviceIdType.MESH)
                cp.start()
                sends.append(cp)
        for cp in sends:
            cp.wait_recv()
        for cp in sends:
            cp.wait_send()

    hbm = pl.BlockSpec(memory_space=pl.ANY)
    return pl.pallas_call(
        body, name=name,
        out_shape=[jax.ShapeDtypeStruct((N_CHIPS,) + s.shape[2:], s.dtype) for s in srcs],
        in_specs=[hbm] * n, out_specs=[hbm] * n,
        scratch_shapes=[pltpu.SemaphoreType.DMA((N_CHIPS * n,)), pltpu.SemaphoreType.DMA((N_CHIPS * n,))],
    )(*srcs)


def _pair_sum(mine, theirs, core, *, name, tile_rows, out_dtype):
    _, _, rows, cols = mine.shape

    def body(core_ref, a_ref, b_ref, o_ref):
        o_ref[...] = (a_ref[0] + b_ref[...]).astype(out_dtype)

    return pl.pallas_call(
        body, name=name,
        grid_spec=pltpu.PrefetchScalarGridSpec(
            num_scalar_prefetch=1, grid=(N_CHIPS, rows // tile_rows),
            in_specs=[pl.BlockSpec((1, 1, tile_rows, cols), lambda q, r, core_ref: (q, core_ref[0], r, 0)),
                      pl.BlockSpec((1, tile_rows, cols), lambda q, r, core_ref: (q, r, 0))],
            out_specs=pl.BlockSpec((1, tile_rows, cols), lambda q, r, core_ref: (q, r, 0))),
        out_shape=jax.ShapeDtypeStruct((N_CHIPS, rows, cols), out_dtype),
        compiler_params=_cparams(("arbitrary", "arbitrary")),
    )(core, mine, theirs)


def _chip_exchange(srcs, *, name):
    n = len(srcs)

    def body(*refs):
        src_refs, out_refs = refs[:n], refs[n:2 * n]
        send_sems, recv_sems, local_sems = refs[2 * n:]
        x, y, c = lax.axis_index("x"), lax.axis_index("y"), lax.axis_index("c")
        my_chip = 2 * x + y
        mine = [pltpu.make_async_copy(src_refs[t].at[my_chip], out_refs[t].at[my_chip], local_sems.at[t])
                for t in range(n)]
        for cp in mine:
            cp.start()
        sends, arrivals = [], []
        for k in (3, 2, 1):
            px = 1 - x if k & 2 else x
            py = 1 - y if k & 1 else y
            peer_chip = 2 * px + py
            for t in range(n):
                sem = (k - 1) * n + t
                cp = pltpu.make_async_remote_copy(
                    src_ref=src_refs[t].at[peer_chip], dst_ref=out_refs[t].at[my_chip],
                    send_sem=send_sems.at[sem], recv_sem=recv_sems.at[sem],
                    device_id=(px, py, c), device_id_type=pl.DeviceIdType.MESH)
                cp.start()
                sends.append(cp)
                arrivals.append(pltpu.make_async_remote_copy(
                    src_ref=src_refs[t].at[peer_chip], dst_ref=out_refs[t].at[peer_chip],
                    send_sem=send_sems.at[sem], recv_sem=recv_sems.at[sem],
                    device_id=(x, y, c), device_id_type=pl.DeviceIdType.MESH))
        for cp in arrivals:
            cp.wait_recv()
        for cp in sends:
            cp.wait_send()
        for cp in mine:
            cp.wait()

    hbm = pl.BlockSpec(memory_space=pl.ANY)
    return pl.pallas_call(
        body, name=name,
        out_shape=[jax.ShapeDtypeStruct(s.shape, s.dtype) for s in srcs],
        in_specs=[hbm] * n, out_specs=[hbm] * n,
        scratch_shapes=[pltpu.SemaphoreType.DMA((3 * n,)), pltpu.SemaphoreType.DMA((3 * n,)),
                        pltpu.SemaphoreType.DMA((n,))],
    )(*srcs)


def _rope_tables(pos_col, invf_row):
    ang = pos_col.astype(F32) * invf_row
    lane = lax.broadcasted_iota(jnp.int32, ang.shape, 1)
    cos, sin = jnp.cos(ang), jnp.sin(ang)
    first = (lane >= KR_LO) & (lane < KR_LO + HALF)
    second = (lane >= KR_LO + HALF) & (lane < KR_LO + ROPE)
    return cos, jnp.where(first, sin, 0.0), jnp.where(second, sin, 0.0)


def _rope(t, cos, sin_first, sin_second, sign):
    up = pltpu.roll(t, LANES - HALF, 1)
    down = pltpu.roll(t, HALF, 1)
    return t * cos - sign * (up * sin_first) + sign * (down * sin_second)


def _fwd_proj(x, pos_col, invf_row, wp_in, w_heads, q_g, kv_g):
    t = x.shape[0]
    tm = PROJ_TILE

    def body(x_ref, pos_ref, invf_ref, win_ref, wh_ref, qg_ref, kvg_ref,
             proj_ref, q_ref, k_ref, v_ref, vt_ref):
        proj = _dot(x_ref[...].astype(BF16), win_ref[...])
        proj_ref[...] = proj
        c_q = proj[:, :Q_LORA]
        c_kv = proj[:, Q_LORA:Q_LORA + KV_LORA]
        kr_raw = proj[:, Q_LORA + KV_LORA:Q_LORA + KV_LORA + LANES]
        cqn = (c_q * lax.rsqrt(jnp.mean(c_q * c_q, axis=-1, keepdims=True) + EPS) * qg_ref[...]).astype(BF16)
        ckvn = (c_kv * lax.rsqrt(jnp.mean(c_kv * c_kv, axis=-1, keepdims=True) + EPS) * kvg_ref[...]).astype(BF16)
        cos, s1, s2 = _rope_tables(pos_ref[...], invf_ref[...])
        kr = _rope(kr_raw, cos, s1, s2, 1.0)
        lane = lax.broadcasted_iota(jnp.int32, (tm, HEAD_PAD), 1)
        for h in range(HEADS):
            q_h = _dot(cqn, wh_ref[h, :Q_LORA, :])
            kv_h = _dot(ckvn, wh_ref[h, Q_LORA:, :])
            q_ref[h] = (_rope(q_h, cos, s1, s2, 1.0) * Q_PRESCALE).astype(BF16)
            k_ref[h] = jnp.where(lane < NOPE, kv_h, kr).astype(BF16)
            v_ref[h] = kv_h.astype(BF16)
            vt_ref[h] = jnp.transpose(kv_h).astype(BF16)

    full = lambda a: pl.BlockSpec(a.shape, lambda i: (0,) * a.ndim)
    head_spec = pl.BlockSpec((HEADS, tm, HEAD_PAD), lambda i: (0, i, 0))
    head_shape = jax.ShapeDtypeStruct((HEADS, t, HEAD_PAD), BF16)
    return pl.pallas_call(
        body, name="fwd_proj", grid=(t // tm,),
        in_specs=[pl.BlockSpec((tm, D_MODEL), lambda i: (i, 0)), pl.BlockSpec((tm, 1), lambda i: (i, 0)),
                  full(invf_row), full(wp_in), full(w_heads), full(q_g), full(kv_g)],
        out_specs=[pl.BlockSpec((tm, D_IN_PAD), lambda i: (i, 0)), head_spec, head_spec, head_spec,
                   pl.BlockSpec((HEADS, HEAD_PAD, tm), lambda i: (0, 0, i))],
        out_shape=[jax.ShapeDtypeStruct((t, D_IN_PAD), F32), head_shape, head_shape, head_shape,
                   jax.ShapeDtypeStruct((HEADS, HEAD_PAD, t), BF16)],
        compiler_params=_cparams(("arbitrary",)),
    )(x, pos_col, invf_row, wp_in, w_heads, q_g, kv_g)


def _attn_fwd(q, k, vt):
    t = q.shape[1]
    bq, bk = ATTN_FWD_WIDE, ATTN_NARROW
    n_diag = bq // bk
    chunk = SOFTMAX_ROWS

    def body(q_ref, k_ref, vt_ref, o_ref, lse_ref, s0, s1, p0, p1, x0, x1, m_scr, l_scr, a_scr, acc_scr):
        i = pl.program_id(1)
        at = lambda j: pl.ds(pl.multiple_of(j * bk, bk), bk)

        def exp_pass(s_in, block_max, p_out, diagonal=False, cols=slice(None)):
            width = bq if cols == slice(None) else cols.stop - cols.start

            def load(r):
                s = s_in[r:r + chunk, cols]
                if diagonal:
                    key = lax.broadcasted_iota(jnp.int32, (chunk, width), 0) + r
                    qry = lax.broadcasted_iota(jnp.int32, (chunk, width), 1)
                    s = jnp.where(qry >= key, s, -jnp.inf)
                return s

            if diagonal:
                block_max = jnp.max(load(0), axis=0, keepdims=True)
                for r in range(chunk, bk, chunk):
                    block_max = jnp.maximum(block_max, jnp.max(load(r), axis=0, keepdims=True))
            m_old = m_scr[:, cols]
            m_new = jnp.maximum(m_old, block_max)
            alpha = jnp.exp2(m_old - m_new)
            total = jnp.zeros((1, width), F32)
            for r in range(0, bk, chunk):
                p = jnp.exp2(load(r) - m_new)
                p_out[r:r + chunk, cols] = p.astype(BF16)
                total = total + jnp.sum(p, axis=0, keepdims=True)
            m_scr[:, cols] = m_new
            l_scr[:, cols] = alpha * l_scr[:, cols] + total
            return alpha

        def scores(j, s_out, x_out):
            s = _dot_nt(k_ref[0, at(j), :], q_ref[0])
            s_out[...] = s
            x_out[...] = jnp.max(s, axis=0, keepdims=True)

        def value_product(j, p_in):
            return _dot(vt_ref[0, :, at(j)], p_in[...])

        def one_pass(j, s_in, x_in, s_out, x_out, p_prev, p_cur):
            scores(j + 1, s_out, x_out)
            acc_scr[...] = a_scr[...] * acc_scr[...] + value_product(jnp.maximum(j - 1, 0), p_prev)
            a_scr[...] = exp_pass(s_in, x_in[...], p_cur)

        scores(0, s0, x0)
        p1[...] = jnp.zeros_like(p1)
        a_scr[...] = jnp.ones_like(a_scr)
        m_scr[...] = jnp.full(m_scr.shape, -jnp.inf, F32)
        l_scr[...] = jnp.zeros_like(l_scr)
        acc_scr[...] = jnp.zeros_like(acc_scr)

        def two_passes(n, _):
            one_pass(2 * n, s0, x0, s1, x1, p1, p0)
            one_pass(2 * n + 1, s1, x1, s0, x0, p0, p1)
            return 0

        lax.fori_loop(0, (n_diag // 2) * i, two_passes, 0)
        d = n_diag * i
        alpha, p_prev, cols = a_scr[...], p1, slice(0, bq)
        for u in range(n_diag + 1):
            s_in, s_next, p_cur = (s0, s1, p0) if u % 2 == 0 else (s1, s0, p1)
            if u + 1 < n_diag:
                ahead = slice((u + 1) * bk, bq)
                s_next[:, ahead] = _dot_nt(k_ref[0, at(d + u + 1), :], q_ref[0, ahead, :])
            acc_scr[:, cols] = alpha * acc_scr[:, cols] + _dot(vt_ref[0, :, at(jnp.maximum(d + u - 1, 0))], p_prev[:, cols])
            if u < n_diag:
                cols = slice(u * bk, bq)
                alpha = exp_pass(s_in, None, p_cur, diagonal=True, cols=cols)
                p_prev = p_cur
        o_ref[0] = jnp.transpose(acc_scr[...] / l_scr[...])
        lse_ref[0] = m_scr[...] + jnp.log2(l_scr[...])

    tile = lambda dtype: pltpu.VMEM((bk, bq), dtype)
    stat = pltpu.VMEM((1, bq), F32)
    return pl.pallas_call(
        body, name="attn_fwd", grid=(HEADS, t // bq),
        in_specs=[pl.BlockSpec((1, bq, HEAD_PAD), lambda h, i: (h, i, 0)),
                  pl.BlockSpec((1, t, HEAD_PAD), lambda h, i: (h, 0, 0)),
                  pl.BlockSpec((1, HEAD_PAD, t), lambda h, i: (h, 0, 0))],
        out_specs=[pl.BlockSpec((1, bq, HEAD_PAD), lambda h, i: (h, i, 0)),
                   pl.BlockSpec((1, 1, bq), lambda h, i: (h, 0, i))],
        out_shape=[jax.ShapeDtypeStruct((HEADS, t, HEAD_PAD), F32), jax.ShapeDtypeStruct((HEADS, 1, t), F32)],
        scratch_shapes=[tile(F32), tile(F32), tile(BF16), tile(BF16), stat, stat, stat, stat, stat,
                        pltpu.VMEM((HEAD_PAD, bq), F32)],
        compiler_params=_cparams(("arbitrary", "arbitrary")),
    )(q, k, vt)


def _mid(x, target, proj, ol, w_out, ws_low, ws_low_t, bsp, sgu_g, sgu_b, ln_g, ln_b):
    t = x.shape[0]
    tm = TOKEN_TILE
    n_steps = t // tm

    def body(x_ref, tgt_ref, za_ref, u_ref, v_ref, zb_ref, ol_ref, wout_ref, ws_ref, wst_ref, bsp_ref,
             sg_ref, sb_ref, lg_ref, lb_ref,
             dr_ref, do_ref, drow_ref, drest_ref, dwout_ref, dws_ref, dbs_ref, dlg_ref, dlb_ref, dsg_ref, dsb_ref,
             loss_ref, dbsp_acc):
        step = pl.program_id(0)

        @pl.when(step == 0)
        def _():
            dwout_ref[...] = jnp.zeros_like(dwout_ref)
            dws_ref[...] = jnp.zeros_like(dws_ref)
            dbs_ref[...] = jnp.zeros_like(dbs_ref)
            dlg_ref[...] = jnp.zeros_like(dlg_ref)
            dlb_ref[...] = jnp.zeros_like(dlb_ref)
            dsg_ref[...] = jnp.zeros_like(dsg_ref)
            dsb_ref[...] = jnp.zeros_like(dsb_ref)
            loss_ref[...] = jnp.zeros_like(loss_ref)
            dbsp_acc[...] = jnp.zeros_like(dbsp_acc)

        n_chunks = tm // CHUNK
        groups = G_WIDTH // LANES

        def side_by_side(a):
            return [jnp.concatenate([a[c * CHUNK:(c + 1) * CHUNK, g * LANES:(g + 1) * LANES] for c in range(n_chunks)],
                                    axis=1) for g in range(groups)]

        def by_chunk(wide):
            return jnp.concatenate([jnp.concatenate([wide[g][:, c * LANES:(c + 1) * LANES] for g in range(groups)], axis=1)
                                    for c in range(n_chunks)], axis=0)

        def own_lanes(h):
            lane = lax.broadcasted_iota(jnp.int32, (CHUNK, n_chunks * LANES), 1)
            return (lane % LANES) // G_HEAD_DIM == h % 2

        def spatial(w_ref, wide):
            return [sum(jnp.where(own_lanes(h), _dot(w_ref[h], wide[g]), 0.0) for h in (2 * g, 2 * g + 1))
                    for g in range(groups)]

        attn = jnp.concatenate([ol_ref[h][:, NOPE:] for h in range(HEADS)], axis=-1)
        za = za_ref[...]
        sig_a = _sigmoid(za)
        silu_a = za * sig_a
        out_a = attn * silu_a
        u = u_ref[...]
        ug = _gelu(u)
        vpre = v_ref[...]
        gv = _gelu(vpre)
        mu_v = jnp.mean(gv, axis=-1, keepdims=True)
        cen_v = gv - mu_v
        rstd_v = lax.rsqrt(jnp.mean(cen_v * cen_v, axis=-1, keepdims=True) + EPS)
        vhat = cen_v * rstd_v
        vg = vhat * sg_ref[...] + sb_ref[...]
        vg_b = vg.astype(BF16)
        sv = by_chunk(spatial(ws_ref, side_by_side(vg_b))) + jnp.tile(bsp_ref[...], (n_chunks, 1))
        sgu = ug * sv
        zb = zb_ref[...]
        sig_b = _sigmoid(zb)
        silu_b = zb * sig_b
        out_b = sgu * silu_b
        merged = jnp.concatenate([out_a, out_b], axis=-1).astype(BF16)
        r = DN_ALPHA * x_ref[...] + _dot(merged, wout_ref[...])
        mu = jnp.mean(r, axis=-1, keepdims=True)
        cen = r - mu
        rstd = lax.rsqrt(jnp.mean(cen * cen, axis=-1, keepdims=True) + EPS)
        xhat = cen * rstd
        hout = xhat * lg_ref[...] + lb_ref[...]
        err = hout - tgt_ref[...]
        row_loss = jnp.mean(err * err, axis=-1, keepdims=True)
        loss_ref[...] += jnp.broadcast_to(0.5 * jnp.sum(row_loss, axis=0, keepdims=True), loss_ref.shape)

        dh = err * (1.0 / D_MODEL)
        dlg_ref[...] += jnp.sum(dh * xhat, axis=0, keepdims=True)
        dlb_ref[...] += jnp.sum(dh, axis=0, keepdims=True)
        dxhat = dh * lg_ref[...]
        dr = rstd * (dxhat - jnp.mean(dxhat, axis=-1, keepdims=True)
                     - xhat * jnp.mean(dxhat * xhat, axis=-1, keepdims=True))
        dr_ref[...] = dr
        dr_b = dr.astype(BF16)
        dwout_ref[...] += _dot_tn(merged, dr_b)
        dmerged = _dot_nt(dr_b, wout_ref[...])
        d_out_a = dmerged[:, :G_WIDTH]
        d_out_b = dmerged[:, G_WIDTH:]
        dattn = d_out_a * silu_a
        for h in range(HEADS):
            do_h = dattn[:, h * VDIM:(h + 1) * VDIM]
            do_ref[h] = jnp.concatenate([jnp.zeros((tm, NOPE), F32), do_h], axis=-1).astype(BF16)
        feature = lax.broadcasted_iota(jnp.int32, (G_WIDTH, LANES), 0) // VDIM
        column = lax.broadcasted_iota(jnp.int32, (G_WIDTH, LANES), 1)
        head_sums = jnp.dot(dattn * attn, jnp.where(feature == column, 1.0, 0.0).astype(F32),
                            preferred_element_type=F32, precision=lax.Precision.HIGHEST)
        dsums_t = jnp.transpose(head_sums)
        for h in range(HEADS):
            drow_ref[h] = dsums_t[h:h + 1, :]
        dza = d_out_a * attn * (sig_a * (1.0 + za * (1.0 - sig_a)))
        dsgu = d_out_b * silu_b
        dzb = d_out_b * sgu * (sig_b * (1.0 + zb * (1.0 - sig_b)))
        du = dsgu * sv * _gelu_grad(u)
        dsv = dsgu * ug
        dsv_b = dsv.astype(BF16)
        for cix in range(n_chunks):
            dbsp_acc[...] += dsv[cix * CHUNK:(cix + 1) * CHUNK, :]
        dsv_wide, vg_wide = side_by_side(dsv_b), side_by_side(vg_b)
        dvg = by_chunk(spatial(wst_ref, dsv_wide))
        for h in range(HEADS):
            mine = jnp.where(own_lanes(h), dsv_wide[h // 2], jnp.zeros_like(dsv_wide[h // 2]))
            dws_ref[h] += _dot_nt(mine, vg_wide[h // 2])
        dsg_ref[...] += jnp.sum(dvg * vhat, axis=0, keepdims=True)
        dsb_ref[...] += jnp.sum(dvg, axis=0, keepdims=True)
        dvhat = dvg * sg_ref[...]
        dgv = rstd_v * (dvhat - jnp.mean(dvhat, axis=-1, keepdims=True)
                        - vhat * jnp.mean(dvhat * vhat, axis=-1, keepdims=True))
        dv = dgv * _gelu_grad(vpre)
        drest_ref[...] = jnp.concatenate([dza, du, dv, dzb], axis=-1).astype(BF16)

        @pl.when(step == n_steps - 1)
        def _():
            tri = (lax.broadcasted_iota(jnp.int32, (CHUNK, CHUNK), 0)
                   >= lax.broadcasted_iota(jnp.int32, (CHUNK, CHUNK), 1))
            for h in range(HEADS):
                dws_ref[h] = jnp.where(tri, dws_ref[h], 0.0)
            tot = dbsp_acc[...]
            lane = lax.broadcasted_iota(jnp.int32, (CHUNK, LANES), 1)
            dbs = jnp.zeros((CHUNK, LANES), F32)
            for h in range(HEADS):
                head_sum = jnp.sum(tot[:, h * G_HEAD_DIM:(h + 1) * G_HEAD_DIM], axis=-1, keepdims=True)
                dbs = jnp.where(lane == h, head_sum, dbs)
            dbs_ref[...] = dbs

    full = lambda a: pl.BlockSpec(a.shape, lambda i: (0,) * a.ndim)
    tile = lambda w, j=0: pl.BlockSpec((tm, w), lambda i, j=j: (i, j))
    heads = pl.BlockSpec((HEADS, tm, HEAD_PAD), lambda i: (0, i, 0))
    acc = lambda shape: (pl.BlockSpec(shape, lambda i: (0,) * len(shape)), jax.ShapeDtypeStruct(shape, F32))
    accs = [acc((D_MODEL, D_MODEL)), acc((HEADS, CHUNK, CHUNK)), acc((CHUNK, LANES)), acc((1, D_MODEL)),
            acc((1, D_MODEL)), acc((1, G_WIDTH)), acc((1, G_WIDTH)), acc((1, LANES))]
    return pl.pallas_call(
        body, name="mid", grid=(n_steps,),
        in_specs=[tile(D_MODEL), tile(D_MODEL), tile(G_WIDTH, 1), tile(G_WIDTH, 2), tile(G_WIDTH, 3), tile(G_WIDTH, 4),
                  heads, full(w_out), full(ws_low), full(ws_low_t), full(bsp), full(sgu_g), full(sgu_b),
                  full(ln_g), full(ln_b)],
        out_specs=[tile(D_MODEL), heads, pl.BlockSpec((HEADS, 1, tm), lambda i: (0, 0, i)), tile(4 * G_WIDTH)]
        + [a[0] for a in accs],
        out_shape=[jax.ShapeDtypeStruct((t, D_MODEL), F32), jax.ShapeDtypeStruct((HEADS, t, HEAD_PAD), BF16),
                   jax.ShapeDtypeStruct((HEADS, 1, t), F32), jax.ShapeDtypeStruct((t, 4 * G_WIDTH), BF16)]
        + [a[1] for a in accs],
        scratch_shapes=[pltpu.VMEM((CHUNK, G_WIDTH), F32)],
        compiler_params=_cparams(("arbitrary",)),
    )(x, target, proj, proj, proj, proj, ol, w_out, ws_low, ws_low_t, bsp, sgu_g, sgu_b, ln_g, ln_b)


def _attn_bwd(q, k, v, do, lse_row, d_row):
    t = q.shape[1]
    bk, bq = ATTN_BWD_WIDE, ATTN_NARROW
    n_diag = bk // bq
    last = t // bq - 1
    chunk = SOFTMAX_ROWS

    def body(q_ref, k_ref, v_ref, do_ref, lse_ref, drow_ref, dqt_ref, dk_ref, dv_ref,
             s0, s1, e0, e1, p0, p1, g0, g1, kt_scr):
        j = pl.program_id(1)
        at = lambda i: pl.ds(pl.multiple_of(i * bq, bq), bq)

        @pl.when(j == 0)
        def _():
            dqt_ref[...] = jnp.zeros_like(dqt_ref)

        kt_scr[...] = jnp.transpose(k_ref[0].astype(F32)).astype(BF16)
        dk_ref[...] = jnp.zeros_like(dk_ref)
        dv_ref[...] = jnp.zeros_like(dv_ref)

        def products(i, s_out, e_out, keys=slice(0, bk)):
            i = jnp.minimum(i, last)
            s_out[keys, :] = _dot_nt(k_ref[0, keys, :], q_ref[0, at(i), :])
            e_out[keys, :] = _dot_nt(v_ref[0, keys, :], do_ref[0, at(i), :])

        def gradients(i, p_in, g_in, keys=slice(0, bk)):
            dv_ref[0, keys, :] += _dot(p_in[keys, :], do_ref[0, at(i), :])
            dk_ref[0, keys, :] += _dot(g_in[keys, :], q_ref[0, at(i), :])
            dqt_ref[0, :, at(i)] += _dot(kt_scr[:, keys], g_in[keys, :])

        def elementwise(i, s_in, e_in, p_out, g_out, qry0=None, keys=slice(0, bk)):
            lse = lse_ref[0, :, at(i)]
            dsum = drow_ref[0, :, at(i)]
            for r in range(keys.start, keys.stop, chunk):
                p = jnp.exp2(s_in[r:r + chunk, :] - lse)
                if qry0 is not None:
                    key = lax.broadcasted_iota(jnp.int32, (chunk, bq), 0) + r
                    qry = lax.broadcasted_iota(jnp.int32, (chunk, bq), 1) + qry0
                    p = jnp.where(qry >= key, p, 0.0)
                p_out[r:r + chunk, :] = p.astype(BF16)
                g_out[r:r + chunk, :] = (p * (e_in[r:r + chunk, :] - dsum)).astype(BF16)

        def one_pass(i, s_in, e_in, s_out, e_out, p_prev, g_prev, p_cur, g_cur):
            products(i + 1, s_out, e_out)
            gradients(i - 1, p_prev, g_prev)
            elementwise(i, s_in, e_in, p_cur, g_cur)

        first = n_diag * j
        keys_of = lambda u: slice(0, min((u + 1) * bq, bk))
        even, odd = (s0, e0, p0, g0), (s1, e1, p1, g1)
        products(first, s0, e0, keys_of(0))
        products(first + 1, s1, e1, keys_of(1))
        elementwise(first, s0, e0, p0, g0, qry0=0, keys=keys_of(0))
        for u in range(1, n_diag):
            (s_in, e_in, p_cur, g_cur), (s_out, e_out, p_prev, g_prev) = (odd, even) if u % 2 else (even, odd)
            products(first + u + 1, s_out, e_out, keys_of(u + 1))
            gradients(first + u - 1, p_prev, g_prev, keys_of(u - 1))
            elementwise(first + u, s_in, e_in, p_cur, g_cur, qry0=u * bq, keys=keys_of(u))

        def two_passes(n, _):
            i = first + n_diag + 2 * n
            one_pass(i, s0, e0, s1, e1, p1, g1, p0, g0)
            one_pass(i + 1, s1, e1, s0, e0, p0, g0, p1, g1)
            return 0

        lax.fori_loop(0, (last - first - n_diag + 1) // 2, two_passes, 0)
        gradients(last, p1, g1)
        dk_ref[0] = dk_ref[0] * LN2

    whole = pl.BlockSpec((1, t, HEAD_PAD), lambda h, j: (h, 0, 0))
    block = pl.BlockSpec((1, bk, HEAD_PAD), lambda h, j: (h, j, 0))
    rows = pl.BlockSpec((1, 1, t), lambda h, j: (h, 0, 0), pipeline_mode=pl.Buffered(1))
    shape = jax.ShapeDtypeStruct((HEADS, t, HEAD_PAD), F32)
    tile = lambda dtype: pltpu.VMEM((bk, bq), dtype)
    return pl.pallas_call(
        body, name="attn_bwd", grid=(HEADS, t // bk),
        in_specs=[whole, block, block, whole, rows, rows],
        out_specs=[pl.BlockSpec((1, HEAD_PAD, t), lambda h, j: (h, 0, 0)), block, block],
        out_shape=[jax.ShapeDtypeStruct((HEADS, HEAD_PAD, t), F32), shape, shape],
        scratch_shapes=[tile(F32), tile(F32), tile(F32), tile(F32), tile(BF16), tile(BF16),
                        tile(BF16), tile(BF16), pltpu.VMEM((HEAD_PAD, bk), BF16)],
        compiler_params=_cparams(("arbitrary", "arbitrary"), vmem_limit=ATTN_BWD_VMEM_LIMIT),
    )(q, k, v, do, lse_row, d_row)


def _bwd_qkv(dq, dk, dv, proj, pos_col, invf_row, w_heads, q_g, kv_g):
    t = proj.shape[0]
    tm = PROJ_TILE

    def body(dq_ref, dk_ref, dv_ref, ph_ref, pos_ref, invf_ref, wh_ref, qg_ref, kvg_ref,
             dhead_ref, dwh_ref, dqg_ref, dkvg_ref):
        @pl.when(pl.program_id(0) == 0)
        def _():
            dwh_ref[...] = jnp.zeros_like(dwh_ref)
            dqg_ref[...] = jnp.zeros_like(dqg_ref)
            dkvg_ref[...] = jnp.zeros_like(dkvg_ref)

        cos, s1, s2 = _rope_tables(pos_ref[...], invf_ref[...])
        lane = lax.broadcasted_iota(jnp.int32, (tm, LANES), 1)
        c_q = ph_ref[:, :Q_LORA]
        c_kv = ph_ref[:, Q_LORA:Q_LORA + KV_LORA]
        rstd_q = lax.rsqrt(jnp.mean(c_q * c_q, axis=-1, keepdims=True) + EPS)
        rstd_kv = lax.rsqrt(jnp.mean(c_kv * c_kv, axis=-1, keepdims=True) + EPS)
        qhat = c_q * rstd_q
        kvhat = c_kv * rstd_kv
        cqn = (qhat * qg_ref[...]).astype(BF16)
        ckvn = (kvhat * kvg_ref[...]).astype(BF16)
        dcqn = jnp.zeros((tm, Q_LORA), F32)
        dckvn = jnp.zeros((tm, KV_LORA), F32)
        dkr_rot = jnp.zeros((tm, LANES), F32)
        for h in range(HEADS):
            dq_b = _rope(jnp.transpose(dq_ref[h]) * ATTN_SCALE, cos, s1, s2, -1.0).astype(BF16)
            dk_h = dk_ref[h]
            dkv_b = jnp.where(lane < NOPE, dk_h, dv_ref[h]).astype(BF16)
            dkr_rot = dkr_rot + dk_h
            dwh_ref[h, :Q_LORA, :] += _dot_tn(cqn, dq_b)
            dwh_ref[h, Q_LORA:, :] += _dot_tn(ckvn, dkv_b)
            dcqn = dcqn + _dot_nt(dq_b, wh_ref[h, :Q_LORA, :])
            dckvn = dckvn + _dot_nt(dkv_b, wh_ref[h, Q_LORA:, :])
        rot_lanes = (lane >= KR_LO) & (lane < KR_LO + ROPE)
        dkr_raw = jnp.where(rot_lanes, _rope(dkr_rot, cos, s1, s2, -1.0), 0.0)
        dqg_ref[...] += jnp.sum(dcqn * qhat, axis=0, keepdims=True)
        dkvg_ref[...] += jnp.sum(dckvn * kvhat, axis=0, keepdims=True)
        dqh = dcqn * qg_ref[...]
        dkvh = dckvn * kvg_ref[...]
        dc_q = rstd_q * (dqh - qhat * jnp.mean(dqh * qhat, axis=-1, keepdims=True))
        dc_kv = rstd_kv * (dkvh - kvhat * jnp.mean(dkvh * kvhat, axis=-1, keepdims=True))
        dhead_ref[...] = jnp.concatenate([dc_q, dc_kv, dkr_raw], axis=-1).astype(BF16)

    full = lambda a: pl.BlockSpec(a.shape, lambda i: (0,) * a.ndim)
    heads = pl.BlockSpec((HEADS, tm, HEAD_PAD), lambda i: (0, i, 0))
    acc = lambda shape: (pl.BlockSpec(shape, lambda i: (0,) * len(shape)), jax.ShapeDtypeStruct(shape, F32))
    accs = [acc(w_heads.shape), acc((1, Q_LORA)), acc((1, KV_LORA))]
    return pl.pallas_call(
        body, name="bwd_qkv", grid=(t // tm,),
        in_specs=[pl.BlockSpec((HEADS, HEAD_PAD, tm), lambda i: (0, 0, i)), heads, heads,
                  pl.BlockSpec((tm, 4 * LANES), lambda i: (i, 0)),
                  pl.BlockSpec((tm, 1), lambda i: (i, 0)), full(invf_row), full(w_heads),
                  full(q_g), full(kv_g)],
        out_specs=[pl.BlockSpec((tm, 4 * LANES), lambda i: (i, 0))] + [a[0] for a in accs],
        out_shape=[jax.ShapeDtypeStruct((t, 4 * LANES), BF16)] + [a[1] for a in accs],
        compiler_params=_cparams(("arbitrary",)),
    )(dq, dk, dv, proj, pos_col, invf_row, w_heads, q_g, kv_g)


def _bwd_in(x, dr, dhead, drest, wp_in):
    t = x.shape[0]
    tm = PROJ_TILE
    n_head = dhead.shape[1]

    def body(x_ref, dr_ref, dhead_ref, drest_ref, win_ref, gx_ref, dwin_ref):
        @pl.when(pl.program_id(0) == 0)
        def _():
            dwin_ref[...] = jnp.zeros_like(dwin_ref)

        xb = x_ref[...].astype(BF16)
        dh_b = dhead_ref[...]
        dr_b = drest_ref[...]
        gx_ref[...] = (DN_ALPHA * dr_ref[...] + _dot_nt(dh_b, win_ref[:, :n_head])
                       + _dot_nt(dr_b, win_ref[:, n_head:]))
        dwin_ref[:, :n_head] += _dot_tn(xb, dh_b)
        dwin_ref[:, n_head:] += _dot_tn(xb, dr_b)

    tile = lambda w: pl.BlockSpec((tm, w), lambda i: (i, 0))
    whole = pl.BlockSpec(wp_in.shape, lambda i: (0, 0))
    return pl.pallas_call(
        body, name="bwd_in", grid=(t // tm,),
        in_specs=[tile(D_MODEL), tile(D_MODEL), tile(n_head), tile(drest.shape[1]), whole],
        out_specs=[tile(D_MODEL), whole],
        out_shape=[jax.ShapeDtypeStruct((t, D_MODEL), F32), jax.ShapeDtypeStruct(wp_in.shape, F32)],
        compiler_params=_cparams(("arbitrary",)),
    )(x, dr, dhead, drest, wp_in)


def _adam(parts, w, m, v, *, name, tile_rows):
    n, rows, cols = parts.shape

    def body(p_ref, w_ref, m_ref, v_ref, g_ref, d_ref, nm_ref, nv_ref):
        g = p_ref[0].astype(F32)
        for s in range(1, n):
            g = g + p_ref[s].astype(F32)
        m_new = ADAM_B1 * m_ref[...] + (1.0 - ADAM_B1) * g
        v_new = ADAM_B2 * v_ref[...] + (1.0 - ADAM_B2) * (g * g)
        m_hat = m_new / (1.0 - ADAM_B1 ** ADAM_STEP)
        v_hat = v_new / (1.0 - ADAM_B2 ** ADAM_STEP)
        g_ref[...] = g
        d_ref[...] = -ADAM_LR * (m_hat / (jnp.sqrt(v_hat) + ADAM_EPS) + ADAM_WD * w_ref[...])
        nm_ref[...] = m_new
        nv_ref[...] = v_new

    flat = pl.BlockSpec((tile_rows, cols), lambda i: (i, 0))
    shape = jax.ShapeDtypeStruct((rows, cols), F32)
    return pl.pallas_call(
        body, name=name, grid=(rows // tile_rows,),
        in_specs=[pl.BlockSpec((n, tile_rows, cols), lambda i: (0, i, 0)), flat, flat, flat],
        out_specs=[flat] * 4, out_shape=[shape] * 4,
        compiler_params=_cparams(("arbitrary",)),
    )(parts, w, m, v)


SMALL_NAMES = ("q_norm_g", "kv_norm_g", "sgu_norm_g", "sgu_norm_b", "b_spatial", "ln_g", "ln_b")
SMALL_SIZES = (Q_LORA, KV_LORA, G_WIDTH, G_WIDTH, HEADS * CHUNK, D_MODEL, D_MODEL)


def _pack_small(vals, last=None):
    flat = jnp.concatenate([v.reshape(-1) for v in vals])
    pad = SMALL_LEN - flat.shape[0]
    if last is None:
        return jnp.pad(flat, (0, pad))
    return jnp.concatenate([flat, jnp.zeros((pad - 1,), F32), last.reshape(1)])


def _unpack_small(flat):
    out, at = [], 0
    for n in SMALL_SIZES:
        out.append(flat[at:at + n])
        at += n
    out[4] = out[4].reshape(HEADS, CHUNK)
    return out


UQ_SHARD = HEADS * (NOPE + ROPE) // N_DEV
HEAD_ROWS = Q_LORA + KV_LORA
MIXED_ROWS = HEAD_ROWS + CHUNK + SMALL_LEN // N_DEV // LANES


def _head_slab(w_uq_shard, w_ukv_shard):
    return jnp.concatenate([jnp.pad(w_uq_shard, ((0, 0), (0, LANES - UQ_SHARD))), w_ukv_shard])


IN_SHARD = D_IN // N_DEV


def _w_in_pieces():
    split = Q_LORA + KV_LORA
    moves = ((0, split, 0), (split, split + ROPE, KR_LO), (split + ROPE, D_IN, LANES - ROPE))
    pieces = []
    for s in range(N_DEV):
        lo, hi = s * IN_SHARD, (s + 1) * IN_SHARD
        for a, b, shift in moves:
            a, b = max(a, lo), min(b, hi)
            if a < b:
                pieces.append((s, a - lo, a + shift, b - a))
    return pieces


def _padded_w_in(shards):
    tr = TOKEN_TILE

    def body(sh_ref, o_ref):
        o_ref[...] = jnp.zeros_like(o_ref)
        for s, src, dst, width in _w_in_pieces():
            o_ref[:, dst:dst + width] = sh_ref[s, :, src:src + width]

    return pl.pallas_call(
        body, name="w_in_pad", grid=(D_MODEL // tr,),
        in_specs=[pl.BlockSpec((N_DEV, tr, IN_SHARD), lambda i: (0, i, 0))],
        out_specs=pl.BlockSpec((tr, D_IN_PAD), lambda i: (i, 0)),
        out_shape=jax.ShapeDtypeStruct((D_MODEL, D_IN_PAD), shards.dtype),
        compiler_params=_cparams(("arbitrary",)),
    )(shards)


def _w_in_shards(dwp_in):
    tr = TOKEN_TILE
    by_shard = [[p for p in _w_in_pieces() if p[0] == s] for s in range(N_DEV)]

    def body(w_ref, o_ref):
        for s, pieces in enumerate(by_shard):
            parts = [w_ref[:, dst:dst + width] for _, _, dst, width in pieces]
            o_ref[s] = parts[0] if len(parts) == 1 else jnp.concatenate(parts, axis=1)

    return pl.pallas_call(
        body, name="w_in_split", grid=(D_MODEL // tr,),
        in_specs=[pl.BlockSpec((tr, D_IN_PAD), lambda i: (i, 0))],
        out_specs=pl.BlockSpec((N_DEV, tr, IN_SHARD), lambda i: (0, i, 0)),
        out_shape=jax.ShapeDtypeStruct((N_DEV, D_MODEL, IN_SHARD), dwp_in.dtype),
        compiler_params=_cparams(("arbitrary",)),
    )(dwp_in)


def kernel(x, positions, w_in, q_norm_g, w_uq, kv_norm_g, w_ukv, sgu_norm_g, sgu_norm_b, w_spatial, b_spatial, w_out, ln_g, ln_b, loss_target, m_w_in, m_q_norm_g, m_w_uq, m_kv_norm_g, m_w_ukv, m_sgu_norm_g, m_sgu_norm_b, m_w_spatial, m_b_spatial, m_w_out, m_ln_g, m_ln_b, v_w_in, v_q_norm_g, v_w_uq, v_kv_norm_g, v_w_ukv, v_sgu_norm_g, v_sgu_norm_b, v_w_spatial, v_b_spatial, v_w_out, v_ln_g, v_ln_b):
    me = 4 * lax.axis_index("x") + 2 * lax.axis_index("y") + lax.axis_index("c")
    seq = x.shape[1]
    x2 = x.reshape(seq, D_MODEL)
    tgt2 = loss_target.reshape(seq, D_MODEL)
    pos_col = positions.reshape(seq, 1)

    w_in_shards, w_out_shards, w_heads = _gather_two_level(
        [w_in.astype(BF16), w_out.astype(BF16), _head_slab(w_uq, w_ukv).astype(BF16)],
        name="wgather")
    (loss_part, grad_x, d_in, d_heads, d_out, d_ws, d_bs_t, d_lng, d_lnb, d_sgug, d_sgub, d_qg, d_kvg) = _local_step(
        x2, tgt2, pos_col, w_in_shards, w_heads, w_out_shards.reshape(D_MODEL, D_MODEL), q_norm_g, kv_norm_g,
        sgu_norm_g, sgu_norm_b, w_spatial, b_spatial, ln_g, ln_b)

    small_part = _pack_small([d_qg, d_kvg, d_sgug, d_sgub, d_bs_t[:, :HEADS].T, d_lng, d_lnb], last=loss_part[0, :1])
    mixed = jnp.concatenate([d_heads, d_ws, small_part.reshape(N_DEV, -1, LANES)], axis=1)
    by_chip = [g.reshape((N_CHIPS, 2) + g.shape[1:])
               for g in (d_in, d_out.reshape(N_DEV, D_MODEL // N_DEV, D_MODEL), mixed)]
    from_sibling = _sibling_swap(by_chip, name="gswap")
    core = lax.axis_index("c").astype(jnp.int32).reshape(1)
    pair_sums = [_pair_sum(a, b, core, name=nm, tile_rows=tr, out_dtype=dt) for a, b, nm, tr, dt in zip(
        by_chip, from_sibling, ("gsum_in", "gsum_out", "gsum_mixed"), (TOKEN_TILE, D_MODEL // N_DEV, MIXED_ROWS),
        (BF16, BF16, F32))]
    recv_in, recv_out, recv_mixed = _chip_exchange(pair_sums, name="gexch")

    take = lambda a: lax.dynamic_index_in_dim(a, me, 0, keepdims=False)
    small_w = _pack_small([q_norm_g, kv_norm_g, sgu_norm_g, sgu_norm_b, b_spatial, ln_g, ln_b])
    small_m = _pack_small([m_q_norm_g, m_kv_norm_g, m_sgu_norm_g, m_sgu_norm_b, m_b_spatial, m_ln_g, m_ln_b])
    small_v = _pack_small([v_q_norm_g, v_kv_norm_g, v_sgu_norm_g, v_sgu_norm_b, v_b_spatial, v_ln_g, v_ln_b])
    own_mixed = lambda uq, ukv, sp, small: jnp.concatenate(
        [_head_slab(uq, ukv), take(sp), take(small.reshape(N_DEV, -1, LANES))])
    res_in = _adam(recv_in, w_in, m_w_in, v_w_in, name="adam_in", tile_rows=TOKEN_TILE)
    res_out = _adam(recv_out, w_out, m_w_out, v_w_out, name="adam_out", tile_rows=D_MODEL // N_DEV)
    res_mixed = _adam(recv_mixed, own_mixed(w_uq, w_ukv, w_spatial, small_w), own_mixed(m_w_uq, m_w_ukv, m_w_spatial, small_m),
                      own_mixed(v_w_uq, v_w_ukv, v_w_spatial, small_v), name="adam_mixed", tile_rows=MIXED_ROWS)

    rep_g, = _exchange([res_mixed[0][HEAD_ROWS:]], name="sgather", per_destination=False)
    rep_pack = lambda sp, small: jnp.concatenate(
        [sp.reshape(N_DEV, CHUNK, LANES), small.reshape(N_DEV, -1, LANES)], axis=1).reshape(-1, LANES)
    _, delta_rep, m_rep, v_rep = _adam(rep_g.reshape(1, N_DEV * REP_ROWS, LANES), rep_pack(w_spatial, small_w),
                                       rep_pack(m_w_spatial, small_m), rep_pack(v_w_spatial, small_v),
                                       name="adam_rep", tile_rows=N_DEV * REP_ROWS)

    def rep_unpack(a):
        a = a.reshape(N_DEV, REP_ROWS, LANES)
        small = _unpack_small(a[:, CHUNK:].reshape(-1))
        return [small[0], small[1], small[2], small[3], a[:, :CHUNK], small[4], small[5], small[6]]

    def ordered(which, rep):
        r_qg, r_kvg, r_sg, r_sb, r_ws, r_bs, r_lg, r_lb = rep_unpack(rep)
        heads = res_mixed[which]
        return [res_in[which], r_qg, heads[:Q_LORA, :UQ_SHARD], r_kvg, heads[Q_LORA:HEAD_ROWS], r_sg, r_sb, r_ws, r_bs,
                res_out[which], r_lg, r_lb]

    loss = rep_g[N_DEV - 1, REP_ROWS - 1, LANES - 1]
    outs = [loss, grad_x.reshape(x.shape)]
    outs += ordered(0, rep_g.reshape(-1, LANES))
    outs += ordered(1, delta_rep)
    outs += ordered(2, m_rep)
    outs += ordered(3, v_rep)
    return tuple(outs)


def _local_step(x2, tgt2, pos_col, w_in_shards, w_heads, w_out_full, q_norm_g, kv_norm_g, sgu_norm_g, sgu_norm_b,
                w_spatial, b_spatial, ln_g, ln_b):
    wp_in = _padded_w_in(w_in_shards)

    half = jnp.arange(HALF, dtype=F32)
    inv_freq = 1.0 / (ROPE_THETA ** (half / HALF))
    invf_row = jnp.concatenate([jnp.zeros((KR_LO,), F32), inv_freq, inv_freq,
                                jnp.zeros((LANES - KR_LO - ROPE,), F32)]).reshape(1, LANES)
    tri = jnp.tril(jnp.ones((CHUNK, CHUNK), dtype=bool))
    ws_low = jnp.where(tri[None], w_spatial, 0.0).astype(BF16)
    ws_low_t = ws_low.transpose(0, 2, 1)
    bsp = jnp.repeat(b_spatial.T, G_HEAD_DIM, axis=1)
    row = lambda a: a.reshape(1, -1)

    proj, q, k, v, vt = _fwd_proj(x2, pos_col, invf_row, wp_in, w_heads, row(q_norm_g), row(kv_norm_g))
    o, lse_row = _attn_fwd(q, k, vt)
    (dr, do, d_row, drest, d_out, d_ws, d_bs_t, d_lng, d_lnb, d_sgug, d_sgub, loss_part) = _mid(
        x2, tgt2, proj, o, w_out_full, ws_low, ws_low_t, bsp, row(sgu_norm_g), row(sgu_norm_b), row(ln_g), row(ln_b))
    dqt, dk, dv = _attn_bwd(q, k, v, do, lse_row, d_row)
    dhead, d_heads, d_qg, d_kvg = _bwd_qkv(dqt, dk, dv, proj, pos_col, invf_row, w_heads, row(q_norm_g), row(kv_norm_g))
    grad_x, dwp_in = _bwd_in(x2, dr, dhead, drest, wp_in)
    return (loss_part, grad_x, _w_in_shards(dwp_in), d_heads, d_out, d_ws, d_bs_t, d_lng, d_lnb, d_sgug, d_sgub,
            d_qg, d_kvg)
```

```python
import functools
import math

import jax
import jax.numpy as jnp
from jax import lax
from jax.experimental import pallas as pl
from jax.experimental.pallas import tpu as pltpu

F32 = jnp.float32
BF16 = jnp.bfloat16

N_DEV = 8
D_MODEL = 1024
HEADS = 8
NOPE = 64
ROPE = 32
HALF = ROPE // 2
VDIM = 64
Q_LORA = 256
KV_LORA = 128
G_WIDTH = 512
G_HEAD_DIM = 64
CHUNK = 128
HEAD_PAD = 128
D_IN = 2464
D_IN_PAD = 2560
KR_LO = NOPE
ROPE_THETA = 10000.0
DN_ALPHA = 2.0 ** 0.25
EPS = 1e-5
ATTN_SCALE = 1.0 / math.sqrt(NOPE + ROPE)
ADAM_LR, ADAM_B1, ADAM_B2, ADAM_EPS, ADAM_WD, ADAM_STEP = 0.001, 0.9, 0.999, 1e-08, 0.01, 10

LANES = 128
REP_ROWS = 136
SMALL_LEN = 8192
VMEM_LIMIT = 56 * 1024 * 1024
ATTN_BWD_VMEM_LIMIT = 61 * 1024 * 1024
BWD_TAIL_VMEM_LIMIT = 61 * 1024 * 1024

TOKEN_TILE = 256
PROJ_TILE = 512
ATTN_FWD_WIDE = 2048
ATTN_BWD_WIDE = 2048
ATTN_NARROW = 512
SOFTMAX_ROWS = 512
LOG2E = 1.4426950408889634
LN2 = 0.6931471805599453
Q_PRESCALE = ATTN_SCALE * LOG2E


def _cparams(sem=None, vmem_limit=VMEM_LIMIT):
    return pltpu.CompilerParams(dimension_semantics=sem, vmem_limit_bytes=vmem_limit)


def _dot(a, b):
    return jnp.dot(a, b, preferred_element_type=F32)


def _dot_nt(a, b):
    return lax.dot_general(a, b, (((1,), (1,)), ((), ())), preferred_element_type=F32)


def _dot_tn(a, b):
    return lax.dot_general(a, b, (((0,), (0,)), ((), ())), preferred_element_type=F32)


def _as_row(col):
    return jnp.transpose(jnp.broadcast_to(col, (col.shape[0], LANES)))[0:1, :]


def _sigmoid(z):
    return 1.0 / (1.0 + jnp.exp(-z))


def _gelu(x):
    return 0.5 * x * (1.0 + lax.erf(x * 0.7071067811865476))


def _gelu_grad(x):
    cdf = 0.5 * (1.0 + lax.erf(x * 0.7071067811865476))
    return cdf + x * jnp.exp(-0.5 * x * x) * 0.3989422804014327


def _exchange(srcs, *, name, per_destination):
    n = len(srcs)
    slab_shapes = [s.shape[1:] if per_destination else s.shape for s in srcs]

    def body(*refs):
        src_refs, out_refs = refs[:n], refs[n:2 * n]
        send_sems, recv_sems, local_sems = refs[2 * n:]
        x, y, c = lax.axis_index("x"), lax.axis_index("y"), lax.axis_index("c")
        me = 4 * x + 2 * y + c

        def slab_for(t, dest):
            return src_refs[t].at[dest] if per_destination else src_refs[t]

        mine = [pltpu.make_async_copy(slab_for(t, me), out_refs[t].at[me], local_sems.at[t]) for t in range(n)]
        for cp in mine:
            cp.start()
        sends, arrivals = [], []
        for k in (6, 7, 4, 5, 2, 3, 1):
            px = 1 - x if k & 4 else x
            py = 1 - y if k & 2 else y
            pc = 1 - c if k & 1 else c
            peer = 4 * px + 2 * py + pc
            for t in range(n):
                sem = (k - 1) * n + t
                cp = pltpu.make_async_remote_copy(
                    src_ref=slab_for(t, peer), dst_ref=out_refs[t].at[me],
                    send_sem=send_sems.at[sem], recv_sem=recv_sems.at[sem],
                    device_id=(px, py, pc), device_id_type=pl.DeviceIdType.MESH)
                cp.start()
                sends.append(cp)
                arrivals.append(pltpu.make_async_remote_copy(
                    src_ref=slab_for(t, peer), dst_ref=out_refs[t].at[peer],
                    send_sem=send_sems.at[sem], recv_sem=recv_sems.at[sem],
                    device_id=(x, y, c), device_id_type=pl.DeviceIdType.MESH))
        for cp in arrivals:
            cp.wait_recv()
        for cp in sends:
            cp.wait_send()
        for cp in mine:
            cp.wait()

    hbm = pl.BlockSpec(memory_space=pl.ANY)
    return pl.pallas_call(
        body, name=name,
        out_shape=[jax.ShapeDtypeStruct((N_DEV,) + tuple(shape), s.dtype) for shape, s in zip(slab_shapes, srcs)],
        in_specs=[hbm] * n, out_specs=[hbm] * n,
        scratch_shapes=[pltpu.SemaphoreType.DMA(((N_DEV - 1) * n,)), pltpu.SemaphoreType.DMA(((N_DEV - 1) * n,)),
                        pltpu.SemaphoreType.DMA((n,))],
    )(*srcs)


def _gather_two_level(srcs, *, name):
    n = len(srcs)

    def body(*refs):
        src_refs, out_refs = refs[:n], refs[n:2 * n]
        send_sems, recv_sems, local_sems = refs[2 * n:]
        x, y, c = lax.axis_index("x"), lax.axis_index("y"), lax.axis_index("c")
        me, sibling = (x, y, c), (x, y, 1 - c)
        chips = [(1 - x, 1 - y), (1 - x, y), (x, 1 - y)]
        index = lambda px, py, pc: 4 * px + 2 * py + pc

        def copy(k, t, block, to, src=None):
            place = out_refs[t].at[index(*block)]
            return pltpu.make_async_remote_copy(
                src_ref=place if src is None else src, dst_ref=place,
                send_sem=send_sems.at[k * n + t], recv_sem=recv_sems.at[k * n + t],
                device_id=to, device_id_type=pl.DeviceIdType.MESH)

        mine = [pltpu.make_async_copy(src_refs[t], out_refs[t].at[index(*me)], local_sems.at[t]) for t in range(n)]
        for cp in mine:
            cp.start()
        first = [copy(1 + j, t, me, (*chip, c), src=src_refs[t]) for j, chip in enumerate(chips) for t in range(n)]
        first += [copy(0, t, me, sibling, src=src_refs[t]) for t in range(n)]
        for cp in first:
            cp.start()
        passed = []
        for j, chip in enumerate(chips):
            for t in range(n):
                copy(1 + j, t, (*chip, c), me).wait_recv()
                cp = copy(4 + j, t, (*chip, c), sibling)
                cp.start()
                passed.append(cp)
        for t in range(n):
            copy(0, t, sibling, me).wait_recv()
        for j, chip in enumerate(chips):
            for t in range(n):
                copy(4 + j, t, (*chip, 1 - c), me).wait_recv()
        for cp in first + passed:
            cp.wait_send()
        for cp in mine:
            cp.wait()

    hbm = pl.BlockSpec(memory_space=pl.ANY)
    return pl.pallas_call(
        body, name=name,
        out_shape=[jax.ShapeDtypeStruct((N_DEV,) + s.shape, s.dtype) for s in srcs],
        in_specs=[hbm] * n, out_specs=[hbm] * n,
        scratch_shapes=[pltpu.SemaphoreType.DMA((7 * n,)), pltpu.SemaphoreType.DMA((7 * n,)),
                        pltpu.SemaphoreType.DMA((n,))],
    )(*srcs)


N_CHIPS = N_DEV // 2


def _sibling_swap(srcs, *, name):
    n = len(srcs)

    def body(*refs):
        src_refs, out_refs = refs[:n], refs[n:2 * n]
        send_sems, recv_sems = refs[2 * n:]
        x, y, c = lax.axis_index("x"), lax.axis_index("y"), lax.axis_index("c")
        sends = []
        for chip in range(N_CHIPS):
            for t in range(n):
                cp = pltpu.make_async_remote_copy(
                    src_ref=src_refs[t].at[chip, 1 - c], dst_ref=out_refs[t].at[chip],
                    send_sem=send_sems.at[chip * n + t], recv_sem=recv_sems.at[chip * n + t],
                    device_id=(x, y, 1 - c), device_id_type=pl.DeviceIdType.MESH)
                cp.start()
                sends.append(cp)
        for cp in sends:
            cp.wait_recv()
        for cp in sends:
            cp.wait_send()

    hbm = pl.BlockSpec(memory_space=pl.ANY)
    return pl.pallas_call(
        body, name=name,
        out_shape=[jax.ShapeDtypeStruct((N_CHIPS,) + s.shape[2:], s.dtype) for s in srcs],
        in_specs=[hbm] * n, out_specs=[hbm] * n,
        scratch_shapes=[pltpu.SemaphoreType.DMA((N_CHIPS * n,)), pltpu.SemaphoreType.DMA((N_CHIPS * n,))],
    )(*srcs)


def _pair_sum(mine, theirs, core, *, name, tile_rows, out_dtype):
    _, _, rows, cols = mine.shape

    def body(core_ref, a_ref, b_ref, o_ref):
        o_ref[...] = (a_ref[0] + b_ref[...]).astype(out_dtype)

    return pl.pallas_call(
        body, name=name,
        grid_spec=pltpu.PrefetchScalarGridSpec(
            num_scalar_prefetch=1, grid=(N_CHIPS, rows // tile_rows),
            in_specs=[pl.BlockSpec((1, 1, tile_rows, cols), lambda q, r, core_ref: (q, core_ref[0], r, 0)),
                      pl.BlockSpec((1, tile_rows, cols), lambda q, r, core_ref: (q, r, 0))],
            out_specs=pl.BlockSpec((1, tile_rows, cols), lambda q, r, core_ref: (q, r, 0))),
        out_shape=jax.ShapeDtypeStruct((N_CHIPS, rows, cols), out_dtype),
        compiler_params=_cparams(("arbitrary", "arbitrary")),
    )(core, mine, theirs)


def _chip_exchange(srcs, *, name):
    n = len(srcs)

    def body(*refs):
        src_refs, out_refs = refs[:n], refs[n:2 * n]
        send_sems, recv_sems, local_sems = refs[2 * n:]
        x, y, c = lax.axis_index("x"), lax.axis_index("y"), lax.axis_index("c")
        my_chip = 2 * x + y
        mine = [pltpu.make_async_copy(src_refs[t].at[my_chip], out_refs[t].at[my_chip], local_sems.at[t])
                for t in range(n)]
        for cp in mine:
            cp.start()
        sends, arrivals = [], []
        for k in (3, 2, 1):
            px = 1 - x if k & 2 else x
            py = 1 - y if k & 1 else y
            peer_chip = 2 * px + py
            for t in range(n):
                sem = (k - 1) * n + t
                cp = pltpu.make_async_remote_copy(
                    src_ref=src_refs[t].at[peer_chip], dst_ref=out_refs[t].at[my_chip],
                    send_sem=send_sems.at[sem], recv_sem=recv_sems.at[sem],
                    device_id=(px, py, c), device_id_type=pl.DeviceIdType.MESH)
                cp.start()
                sends.append(cp)
                arrivals.append(pltpu.make_async_remote_copy(
                    src_ref=src_refs[t].at[peer_chip], dst_ref=out_refs[t].at[peer_chip],
                    send_sem=send_sems.at[sem], recv_sem=recv_sems.at[sem],
                    device_id=(x, y, c), device_id_type=pl.DeviceIdType.MESH))
        for cp in arrivals:
            cp.wait_recv()
        for cp in sends:
            cp.wait_send()
        for cp in mine:
            cp.wait()

    hbm = pl.BlockSpec(memory_space=pl.ANY)
    return pl.pallas_call(
        body, name=name,
        out_shape=[jax.ShapeDtypeStruct(s.shape, s.dtype) for s in srcs],
        in_specs=[hbm] * n, out_specs=[hbm] * n,
        scratch_shapes=[pltpu.SemaphoreType.DMA((3 * n,)), pltpu.SemaphoreType.DMA((3 * n,)),
                        pltpu.SemaphoreType.DMA((n,))],
    )(*srcs)


def _rope_tables(pos_col, invf_row):
    ang = pos_col.astype(F32) * invf_row
    lane = lax.broadcasted_iota(jnp.int32, ang.shape, 1)
    cos, sin = jnp.cos(ang), jnp.sin(ang)
    first = (lane >= KR_LO) & (lane < KR_LO + HALF)
    second = (lane >= KR_LO + HALF) & (lane < KR_LO + ROPE)
    return cos, jnp.where(first, sin, 0.0), jnp.where(second, sin, 0.0)


def _rope(t, cos, sin_first, sin_second, sign):
    up = pltpu.roll(t, LANES - HALF, 1)
    down = pltpu.roll(t, HALF, 1)
    return t * cos - sign * (up * sin_first) + sign * (down * sin_second)


def _fwd_proj(x, pos_col, invf_row, wp_in, w_heads, q_g, kv_g):
    t = x.shape[0]
    tm = PROJ_TILE

    def body(x_ref, pos_ref, invf_ref, win_ref, wh_ref, qg_ref, kvg_ref,
             proj_ref, q_ref, k_ref, v_ref, vt_ref):
        proj = _dot(x_ref[...].astype(BF16), win_ref[...])
        proj_ref[...] = proj
        c_q = proj[:, :Q_LORA]
        c_kv = proj[:, Q_LORA:Q_LORA + KV_LORA]
        kr_raw = proj[:, Q_LORA + KV_LORA:Q_LORA + KV_LORA + LANES]
        cqn = (c_q * lax.rsqrt(jnp.mean(c_q * c_q, axis=-1, keepdims=True) + EPS) * qg_ref[...]).astype(BF16)
        ckvn = (c_kv * lax.rsqrt(jnp.mean(c_kv * c_kv, axis=-1, keepdims=True) + EPS) * kvg_ref[...]).astype(BF16)
        cos, s1, s2 = _rope_tables(pos_ref[...], invf_ref[...])
        kr = _rope(kr_raw, cos, s1, s2, 1.0)
        lane = lax.broadcasted_iota(jnp.int32, (tm, HEAD_PAD), 1)
        for h in range(HEADS):
            q_h = _dot(cqn, wh_ref[h, :Q_LORA, :])
            kv_h = _dot(ckvn, wh_ref[h, Q_LORA:, :])
            q_ref[h] = (_rope(q_h, cos, s1, s2, 1.0) * Q_PRESCALE).astype(BF16)
            k_ref[h] = jnp.where(lane < NOPE, kv_h, kr).astype(BF16)
            v_ref[h] = kv_h.astype(BF16)
            vt_ref[h] = jnp.transpose(kv_h).astype(BF16)

    full = lambda a: pl.BlockSpec(a.shape, lambda i: (0,) * a.ndim)
    head_spec = pl.BlockSpec((HEADS, tm, HEAD_PAD), lambda i: (0, i, 0))
    head_shape = jax.ShapeDtypeStruct((HEADS, t, HEAD_PAD), BF16)
    return pl.pallas_call(
        body, name="fwd_proj", grid=(t // tm,),
        in_specs=[pl.BlockSpec((tm, D_MODEL), lambda i: (i, 0)), pl.BlockSpec((tm, 1), lambda i: (i, 0)),
                  full(invf_row), full(wp_in), full(w_heads), full(q_g), full(kv_g)],
        out_specs=[pl.BlockSpec((tm, D_IN_PAD), lambda i: (i, 0)), head_spec, head_spec, head_spec,
                   pl.BlockSpec((HEADS, HEAD_PAD, tm), lambda i: (0, 0, i))],
        out_shape=[jax.ShapeDtypeStruct((t, D_IN_PAD), F32), head_shape, head_shape, head_shape,
                   jax.ShapeDtypeStruct((HEADS, HEAD_PAD, t), BF16)],
        compiler_params=_cparams(("arbitrary",)),
    )(x, pos_col, invf_row, wp_in, w_heads, q_g, kv_g)


def _attn_fwd(q, k, vt):
    t = q.shape[1]
    bq, bk = ATTN_FWD_WIDE, ATTN_NARROW
    n_diag = bq // bk
    chunk = SOFTMAX_ROWS

    def body(q_ref, k_ref, vt_ref, o_ref, lse_ref, s0, s1, p0, p1, x0, x1, m_scr, l_scr, a_scr, acc_scr):
        i = pl.program_id(1)
        at = lambda j: pl.ds(pl.multiple_of(j * bk, bk), bk)

        def exp_pass(s_in, block_max, p_out, diagonal=False, cols=slice(None)):
            width = bq if cols == slice(None) else cols.stop - cols.start

            def load(r):
                s = s_in[r:r + chunk, cols]
                if diagonal:
                    key = lax.broadcasted_iota(jnp.int32, (chunk, width), 0) + r
                    qry = lax.broadcasted_iota(jnp.int32, (chunk, width), 1)
                    s = jnp.where(qry >= key, s, -jnp.inf)
                return s

            if diagonal:
                block_max = jnp.max(load(0), axis=0, keepdims=True)
                for r in range(chunk, bk, chunk):
                    block_max = jnp.maximum(block_max, jnp.max(load(r), axis=0, keepdims=True))
            m_old = m_scr[:, cols]
            m_new = jnp.maximum(m_old, block_max)
            alpha = jnp.exp2(m_old - m_new)
            total = jnp.zeros((1, width), F32)
            for r in range(0, bk, chunk):
                p = jnp.exp2(load(r) - m_new)
                p_out[r:r + chunk, cols] = p.astype(BF16)
                total = total + jnp.sum(p, axis=0, keepdims=True)
            m_scr[:, cols] = m_new
            l_scr[:, cols] = alpha * l_scr[:, cols] + total
            return alpha

        def scores(j, s_out, x_out):
            s = _dot_nt(k_ref[0, at(j), :], q_ref[0])
            s_out[...] = s
            x_out[...] = jnp.max(s, axis=0, keepdims=True)

        def value_product(j, p_in):
            return _dot(vt_ref[0, :, at(j)], p_in[...])

        def one_pass(j, s_in, x_in, s_out, x_out, p_prev, p_cur):
            scores(j + 1, s_out, x_out)
            acc_scr[...] = a_scr[...] * acc_scr[...] + value_product(jnp.maximum(j - 1, 0), p_prev)
            a_scr[...] = exp_pass(s_in, x_in[...], p_cur)

        scores(0, s0, x0)
        p1[...] = jnp.zeros_like(p1)
        a_scr[...] = jnp.ones_like(a_scr)
        m_scr[...] = jnp.full(m_scr.shape, -jnp.inf, F32)
        l_scr[...] = jnp.zeros_like(l_scr)
        acc_scr[...] = jnp.zeros_like(acc_scr)

        def two_passes(n, _):
            one_pass(2 * n, s0, x0, s1, x1, p1, p0)
            one_pass(2 * n + 1, s1, x1, s0, x0, p0, p1)
            return 0

        lax.fori_loop(0, (n_diag // 2) * i, two_passes, 0)
        d = n_diag * i
        alpha, p_prev, cols = a_scr[...], p1, slice(0, bq)
        for u in range(n_diag + 1):
            s_in, s_next, p_cur = (s0, s1, p0) if u % 2 == 0 else (s1, s0, p1)
            if u + 1 < n_diag:
                ahead = slice((u + 1) * bk, bq)
                s_next[:, ahead] = _dot_nt(k_ref[0, at(d + u + 1), :], q_ref[0, ahead, :])
            acc_scr[:, cols] = alpha * acc_scr[:, cols] + _dot(vt_ref[0, :, at(jnp.maximum(d + u - 1, 0))], p_prev[:, cols])
            if u < n_diag:
                cols = slice(u * bk, bq)
                alpha = exp_pass(s_in, None, p_cur, diagonal=True, cols=cols)
                p_prev = p_cur
        o_ref[0] = jnp.transpose(acc_scr[...] / l_scr[...])
        lse_ref[0] = m_scr[...] + jnp.log2(l_scr[...])

    tile = lambda dtype: pltpu.VMEM((bk, bq), dtype)
    stat = pltpu.VMEM((1, bq), F32)
    return pl.pallas_call(
        body, name="attn_fwd", grid=(HEADS, t // bq),
        in_specs=[pl.BlockSpec((1, bq, HEAD_PAD), lambda h, i: (h, i, 0)),
                  pl.BlockSpec((1, t, HEAD_PAD), lambda h, i: (h, 0, 0)),
                  pl.BlockSpec((1, HEAD_PAD, t), lambda h, i: (h, 0, 0))],
        out_specs=[pl.BlockSpec((1, bq, HEAD_PAD), lambda h, i: (h, i, 0)),
                   pl.BlockSpec((1, 1, bq), lambda h, i: (h, 0, i))],
        out_shape=[jax.ShapeDtypeStruct((HEADS, t, HEAD_PAD), F32), jax.ShapeDtypeStruct((HEADS, 1, t), F32)],
        scratch_shapes=[tile(F32), tile(F32), tile(BF16), tile(BF16), stat, stat, stat, stat, stat,
                        pltpu.VMEM((HEAD_PAD, bq), F32)],
        compiler_params=_cparams(("arbitrary", "arbitrary")),
    )(q, k, vt)


def _mid(x, target, proj, ol, w_out, ws_low, ws_low_t, bsp, sgu_g, sgu_b, ln_g, ln_b):
    t = x.shape[0]
    tm = TOKEN_TILE
    n_steps = t // tm

    def body(x_ref, tgt_ref, za_ref, u_ref, v_ref, zb_ref, ol_ref, wout_ref, ws_ref, wst_ref, bsp_ref,
             sg_ref, sb_ref, lg_ref, lb_ref,
             dr_ref, do_ref, drow_ref, drest_ref, dwout_ref, dws_ref, dbs_ref, dlg_ref, dlb_ref, dsg_ref, dsb_ref,
             loss_ref, dbsp_acc):
        step = pl.program_id(0)

        @pl.when(step == 0)
        def _():
            dwout_ref[...] = jnp.zeros_like(dwout_ref)
            dws_ref[...] = jnp.zeros_like(dws_ref)
            dbs_ref[...] = jnp.zeros_like(dbs_ref)
            dlg_ref[...] = jnp.zeros_like(dlg_ref)
            dlb_ref[...] = jnp.zeros_like(dlb_ref)
            dsg_ref[...] = jnp.zeros_like(dsg_ref)
            dsb_ref[...] = jnp.zeros_like(dsb_ref)
            loss_ref[...] = jnp.zeros_like(loss_ref)
            dbsp_acc[...] = jnp.zeros_like(dbsp_acc)

        n_chunks = tm // CHUNK
        groups = G_WIDTH // LANES

        def side_by_side(a):
            return [jnp.concatenate([a[c * CHUNK:(c + 1) * CHUNK, g * LANES:(g + 1) * LANES] for c in range(n_chunks)],
                                    axis=1) for g in range(groups)]

        def by_chunk(wide):
            return jnp.concatenate([jnp.concatenate([wide[g][:, c * LANES:(c + 1) * LANES] for g in range(groups)], axis=1)
                                    for c in range(n_chunks)], axis=0)

        def own_lanes(h):
            lane = lax.broadcasted_iota(jnp.int32, (CHUNK, n_chunks * LANES), 1)
            return (lane % LANES) // G_HEAD_DIM == h % 2

        def spatial(w_ref, wide):
            return [sum(jnp.where(own_lanes(h), _dot(w_ref[h], wide[g]), 0.0) for h in (2 * g, 2 * g + 1))
                    for g in range(groups)]

        attn = jnp.concatenate([ol_ref[h][:, NOPE:] for h in range(HEADS)], axis=-1)
        za = za_ref[...]
        sig_a = _sigmoid(za)
        silu_a = za * sig_a
        out_a = attn * silu_a
        u = u_ref[...]
        ug = _gelu(u)
        vpre = v_ref[...]
        gv = _gelu(vpre)
        mu_v = jnp.mean(gv, axis=-1, keepdims=True)
        cen_v = gv - mu_v
        rstd_v = lax.rsqrt(jnp.mean(cen_v * cen_v, axis=-1, keepdims=True) + EPS)
        vhat = cen_v * rstd_v
        vg = vhat * sg_ref[...] + sb_ref[...]
        vg_b = vg.astype(BF16)
        sv = by_chunk(spatial(ws_ref, side_by_side(vg_b))) + jnp.tile(bsp_ref[...], (n_chunks, 1))
        sgu = ug * sv
        zb = zb_ref[...]
        sig_b = _sigmoid(zb)
        silu_b = zb * sig_b
        out_b = sgu * silu_b
        merged = jnp.concatenate([out_a, out_b], axis=-1).astype(BF16)
        r = DN_ALPHA * x_ref[...] + _dot(merged, wout_ref[...])
        mu = jnp.mean(r, axis=-1, keepdims=True)
        cen = r - mu
        rstd = lax.rsqrt(jnp.mean(cen * cen, axis=-1, keepdims=True) + EPS)
        xhat = cen * rstd
        hout = xhat * lg_ref[...] + lb_ref[...]
        err = hout - tgt_ref[...]
        row_loss = jnp.mean(err * err, axis=-1, keepdims=True)
        loss_ref[...] += jnp.broadcast_to(0.5 * jnp.sum(row_loss, axis=0, keepdims=True), loss_ref.shape)

        dh = err * (1.0 / D_MODEL)
        dlg_ref[...] += jnp.sum(dh * xhat, axis=0, keepdims=True)
        dlb_ref[...] += jnp.sum(dh, axis=0, keepdims=True)
        dxhat = dh * lg_ref[...]
        dr = rstd * (dxhat - jnp.mean(dxhat, axis=-1, keepdims=True)
                     - xhat * jnp.mean(dxhat * xhat, axis=-1, keepdims=True))
        dr_ref[...] = dr
        dr_b = dr.astype(BF16)
        dwout_ref[...] += _dot_tn(merged, dr_b)
        dmerged = _dot_nt(dr_b, wout_ref[...])
        d_out_a = dmerged[:, :G_WIDTH]
        d_out_b = dmerged[:, G_WIDTH:]
        dattn = d_out_a * silu_a
        for h in range(HEADS):
            do_h = dattn[:, h * VDIM:(h + 1) * VDIM]
            do_ref[h] = jnp.concatenate([jnp.zeros((tm, NOPE), F32), do_h], axis=-1).astype(BF16)
        feature = lax.broadcasted_iota(jnp.int32, (G_WIDTH, LANES), 0) // VDIM
        column = lax.broadcasted_iota(jnp.int32, (G_WIDTH, LANES), 1)
        head_sums = jnp.dot(dattn * attn, jnp.where(feature == column, 1.0, 0.0).astype(F32),
                            preferred_element_type=F32, precision=lax.Precision.HIGHEST)
        dsums_t = jnp.transpose(head_sums)
        for h in range(HEADS):
            drow_ref[h] = dsums_t[h:h + 1, :]
        dza = d_out_a * attn * (sig_a * (1.0 + za * (1.0 - sig_a)))
        dsgu = d_out_b * silu_b
        dzb = d_out_b * sgu * (sig_b * (1.0 + zb * (1.0 - sig_b)))
        du = dsgu * sv * _gelu_grad(u)
        dsv = dsgu * ug
        dsv_b = dsv.astype(BF16)
        for cix in range(n_chunks):
            dbsp_acc[...] += dsv[cix * CHUNK:(cix + 1) * CHUNK, :]
        dsv_wide, vg_wide = side_by_side(dsv_b), side_by_side(vg_b)
        dvg = by_chunk(spatial(wst_ref, dsv_wide))
        for h in range(HEADS):
            mine = jnp.where(own_lanes(h), dsv_wide[h // 2], jnp.zeros_like(dsv_wide[h // 2]))
            dws_ref[h] += _dot_nt(mine, vg_wide[h // 2])
        dsg_ref[...] += jnp.sum(dvg * vhat, axis=0, keepdims=True)
        dsb_ref[...] += jnp.sum(dvg, axis=0, keepdims=True)
        dvhat = dvg * sg_ref[...]
        dgv = rstd_v * (dvhat - jnp.mean(dvhat, axis=-1, keepdims=True)
                        - vhat * jnp.mean(dvhat * vhat, axis=-1, keepdims=True))
        dv = dgv * _gelu_grad(vpre)
        drest_ref[...] = jnp.concatenate([dza, du, dv, dzb], axis=-1).astype(BF16)

        @pl.when(step == n_steps - 1)
        def _():
            tri = (lax.broadcasted_iota(jnp.int32, (CHUNK, CHUNK), 0)
                   >= lax.broadcasted_iota(jnp.int32, (CHUNK, CHUNK), 1))
            for h in range(HEADS):
                dws_ref[h] = jnp.where(tri, dws_ref[h], 0.0)
            tot = dbsp_acc[...]
            lane = lax.broadcasted_iota(jnp.int32, (CHUNK, LANES), 1)
            dbs = jnp.zeros((CHUNK, LANES), F32)
            for h in range(HEADS):
                head_sum = jnp.sum(tot[:, h * G_HEAD_DIM:(h + 1) * G_HEAD_DIM], axis=-1, keepdims=True)
                dbs = jnp.where(lane == h, head_sum, dbs)
            dbs_ref[...] = dbs

    full = lambda a: pl.BlockSpec(a.shape, lambda i: (0,) * a.ndim)
    tile = lambda w, j=0: pl.BlockSpec((tm, w), lambda i, j=j: (i, j))
    heads = pl.BlockSpec((HEADS, tm, HEAD_PAD), lambda i: (0, i, 0))
    acc = lambda shape: (pl.BlockSpec(shape, lambda i: (0,) * len(shape)), jax.ShapeDtypeStruct(shape, F32))
    accs = [acc((D_MODEL, D_MODEL)), acc((HEADS, CHUNK, CHUNK)), acc((CHUNK, LANES)), acc((1, D_MODEL)),
            acc((1, D_MODEL)), acc((1, G_WIDTH)), acc((1, G_WIDTH)), acc((1, LANES))]
    return pl.pallas_call(
        body, name="mid", grid=(n_steps,),
        in_specs=[tile(D_MODEL), tile(D_MODEL), tile(G_WIDTH, 1), tile(G_WIDTH, 2), tile(G_WIDTH, 3), tile(G_WIDTH, 4),
                  heads, full(w_out), full(ws_low), full(ws_low_t), full(bsp), full(sgu_g), full(sgu_b),
                  full(ln_g), full(ln_b)],
        out_specs=[tile(D_MODEL), heads, pl.BlockSpec((HEADS, 1, tm), lambda i: (0, 0, i)), tile(4 * G_WIDTH)]
        + [a[0] for a in accs],
        out_shape=[jax.ShapeDtypeStruct((t, D_MODEL), F32), jax.ShapeDtypeStruct((HEADS, t, HEAD_PAD), BF16),
                   jax.ShapeDtypeStruct((HEADS, 1, t), F32), jax.ShapeDtypeStruct((t, 4 * G_WIDTH), BF16)]
        + [a[1] for a in accs],
        scratch_shapes=[pltpu.VMEM((CHUNK, G_WIDTH), F32)],
        compiler_params=_cparams(("arbitrary",)),
    )(x, target, proj, proj, proj, proj, ol, w_out, ws_low, ws_low_t, bsp, sgu_g, sgu_b, ln_g, ln_b)


def _attn_bwd(q, k, v, do, lse_row, d_row):
    t = q.shape[1]
    bk, bq = ATTN_BWD_WIDE, ATTN_NARROW
    n_diag = bk // bq
    last = t // bq - 1
    chunk = SOFTMAX_ROWS

    def body(q_ref, k_ref, v_ref, do_ref, lse_ref, drow_ref, dqt_ref, dk_ref, dv_ref,
             s0, s1, e0, e1, p0, p1, g0, g1, kt_scr):
        j = pl.program_id(1)
        at = lambda i: pl.ds(pl.multiple_of(i * bq, bq), bq)

        @pl.when(j == 0)
        def _():
            dqt_ref[...] = jnp.zeros_like(dqt_ref)

        kt_scr[...] = jnp.transpose(k_ref[0].astype(F32)).astype(BF16)
        dk_ref[...] = jnp.zeros_like(dk_ref)
        dv_ref[...] = jnp.zeros_like(dv_ref)

        def products(i, s_out, e_out, keys=slice(0, bk)):
            i = jnp.minimum(i, last)
            s_out[keys, :] = _dot_nt(k_ref[0, keys, :], q_ref[0, at(i), :])
            e_out[keys, :] = _dot_nt(v_ref[0, keys, :], do_ref[0, at(i), :])

        def gradients(i, p_in, g_in, keys=slice(0, bk)):
            dv_ref[0, keys, :] += _dot(p_in[keys, :], do_ref[0, at(i), :])
            dk_ref[0, keys, :] += _dot(g_in[keys, :], q_ref[0, at(i), :])
            dqt_ref[0, :, at(i)] += _dot(kt_scr[:, keys], g_in[keys, :])

        def elementwise(i, s_in, e_in, p_out, g_out, qry0=None, keys=slice(0, bk)):
            lse = lse_ref[0, :, at(i)]
            dsum = drow_ref[0, :, at(i)]
            for r in range(keys.start, keys.stop, chunk):
                p = jnp.exp2(s_in[r:r + chunk, :] - lse)
                if qry0 is not None:
                    key = lax.broadcasted_iota(jnp.int32, (chunk, bq), 0) + r
                    qry = lax.broadcasted_iota(jnp.int32, (chunk, bq), 1) + qry0
                    p = jnp.where(qry >= key, p, 0.0)
                p_out[r:r + chunk, :] = p.astype(BF16)
                g_out[r:r + chunk, :] = (p * (e_in[r:r + chunk, :] - dsum)).astype(BF16)

        def one_pass(i, s_in, e_in, s_out, e_out, p_prev, g_prev, p_cur, g_cur):
            products(i + 1, s_out, e_out)
            gradients(i - 1, p_prev, g_prev)
            elementwise(i, s_in, e_in, p_cur, g_cur)

        first = n_diag * j
        keys_of = lambda u: slice(0, min((u + 1) * bq, bk))
        even, odd = (s0, e0, p0, g0), (s1, e1, p1, g1)
        products(first, s0, e0, keys_of(0))
        products(first + 1, s1, e1, keys_of(1))
        elementwise(first, s0, e0, p0, g0, qry0=0, keys=keys_of(0))
        for u in range(1, n_diag):
            (s_in, e_in, p_cur, g_cur), (s_out, e_out, p_prev, g_prev) = (odd, even) if u % 2 else (even, odd)
            products(first + u + 1, s_out, e_out, keys_of(u + 1))
            gradients(first + u - 1, p_prev, g_prev, keys_of(u - 1))
            elementwise(first + u, s_in, e_in, p_cur, g_cur, qry0=u * bq, keys=keys_of(u))

        def two_passes(n, _):
            i = first + n_diag + 2 * n
            one_pass(i, s0, e0, s1, e1, p1, g1, p0, g0)
            one_pass(i + 1, s1, e1, s0, e0, p0, g0, p1, g1)
            return 0

        lax.fori_loop(0, (last - first - n_diag + 1) // 2, two_passes, 0)
        gradients(last, p1, g1)
        dk_ref[0] = dk_ref[0] * LN2

    whole = pl.BlockSpec((1, t, HEAD_PAD), lambda h, j: (h, 0, 0))
    block = pl.BlockSpec((1, bk, HEAD_PAD), lambda h, j: (h, j, 0))
    rows = pl.BlockSpec((1, 1, t), lambda h, j: (h, 0, 0), pipeline_mode=pl.Buffered(1))
    shape = jax.ShapeDtypeStruct((HEADS, t, HEAD_PAD), F32)
    tile = lambda dtype: pltpu.VMEM((bk, bq), dtype)
    return pl.pallas_call(
        body, name="attn_bwd", grid=(HEADS, t // bk),
        in_specs=[whole, block, block, whole, rows, rows],
        out_specs=[pl.BlockSpec((1, HEAD_PAD, t), lambda h, j: (h, 0, 0)), block, block],
        out_shape=[jax.ShapeDtypeStruct((HEADS, HEAD_PAD, t), F32), shape, shape],
        scratch_shapes=[tile(F32), tile(F32), tile(F32), tile(F32), tile(BF16), tile(BF16),
                        tile(BF16), tile(BF16), pltpu.VMEM((HEAD_PAD, bk), BF16)],
        compiler_params=_cparams(("arbitrary", "arbitrary"), vmem_limit=ATTN_BWD_VMEM_LIMIT),
    )(q, k, v, do, lse_row, d_row)


def _bwd_tail(dq, dk, dv, proj, pos_col, invf_row, w_heads, q_g, kv_g, x, dr, drest, wp_in):
    t = proj.shape[0]
    tm = PROJ_TILE
    n_head = 4 * LANES

    def body(dq_ref, dk_ref, dv_ref, ph_ref, pos_ref, invf_ref, wh_ref, qg_ref, kvg_ref,
             x_ref, dr_ref, drest_ref, win_ref,
             gx_ref, dwin_ref, dwh_ref, dqg_ref, dkvg_ref):
        @pl.when(pl.program_id(0) == 0)
        def _():
            dwin_ref[...] = jnp.zeros_like(dwin_ref)
            dwh_ref[...] = jnp.zeros_like(dwh_ref)
            dqg_ref[...] = jnp.zeros_like(dqg_ref)
            dkvg_ref[...] = jnp.zeros_like(dkvg_ref)

        xb = x_ref[...].astype(BF16)
        dr_b = drest_ref[...]
        dwin_ref[:, n_head:] += _dot_tn(xb, dr_b)
        gx_rest = DN_ALPHA * dr_ref[...] + _dot_nt(dr_b, win_ref[:, n_head:])

        cos, s1, s2 = _rope_tables(pos_ref[...], invf_ref[...])
        lane = lax.broadcasted_iota(jnp.int32, (tm, LANES), 1)
        c_q = ph_ref[:, :Q_LORA]
        c_kv = ph_ref[:, Q_LORA:Q_LORA + KV_LORA]
        rstd_q = lax.rsqrt(jnp.mean(c_q * c_q, axis=-1, keepdims=True) + EPS)
        rstd_kv = lax.rsqrt(jnp.mean(c_kv * c_kv, axis=-1, keepdims=True) + EPS)
        qhat = c_q * rstd_q
        kvhat = c_kv * rstd_kv
        cqn = (qhat * qg_ref[...]).astype(BF16)
        ckvn = (kvhat * kvg_ref[...]).astype(BF16)
        dcqn = jnp.zeros((tm, Q_LORA), F32)
        dckvn = jnp.zeros((tm, KV_LORA), F32)
        dkr_rot = jnp.zeros((tm, LANES), F32)
        for h in range(HEADS):
            dq_b = _rope(jnp.transpose(dq_ref[h]) * ATTN_SCALE, cos, s1, s2, -1.0).astype(BF16)
            dk_h = dk_ref[h]
            dkv_b = jnp.where(lane < NOPE, dk_h, dv_ref[h]).astype(BF16)
            dkr_rot = dkr_rot + dk_h
            dwh_ref[h, :Q_LORA, :] += _dot_tn(cqn, dq_b)
            dwh_ref[h, Q_LORA:, :] += _dot_tn(ckvn, dkv_b)
            dcqn = dcqn + _dot_nt(dq_b, wh_ref[h, :Q_LORA, :])
            dckvn = dckvn + _dot_nt(dkv_b, wh_ref[h, Q_LORA:, :])
        rot_lanes = (lane >= KR_LO) & (lane < KR_LO + ROPE)
        dkr_raw = jnp.where(rot_lanes, _rope(dkr_rot, cos, s1, s2, -1.0), 0.0)
        dqg_ref[...] += jnp.sum(dcqn * qhat, axis=0, keepdims=True)
        dkvg_ref[...] += jnp.sum(dckvn * kvhat, axis=0, keepdims=True)
        dqh = dcqn * qg_ref[...]
        dkvh = dckvn * kvg_ref[...]
        dc_q = rstd_q * (dqh - qhat * jnp.mean(dqh * qhat, axis=-1, keepdims=True))
        dc_kv = rstd_kv * (dkvh - kvhat * jnp.mean(dkvh * kvhat, axis=-1, keepdims=True))
        dh_b = jnp.concatenate([dc_q, dc_kv, dkr_raw], axis=-1).astype(BF16)
        dwin_ref[:, :n_head] += _dot_tn(xb, dh_b)
        gx_ref[...] = gx_rest + _dot_nt(dh_b, win_ref[:, :n_head])

    full = lambda a: pl.BlockSpec(a.shape, lambda i: (0,) * a.ndim)
    tile = lambda w: pl.BlockSpec((tm, w), lambda i: (i, 0))
    heads = pl.BlockSpec((HEADS, tm, HEAD_PAD), lambda i: (0, i, 0))
    acc = lambda shape: (pl.BlockSpec(shape, lambda i: (0,) * len(shape)), jax.ShapeDtypeStruct(shape, F32))
    accs = [acc(wp_in.shape), acc(w_heads.shape), acc((1, Q_LORA)), acc((1, KV_LORA))]
    return pl.pallas_call(
        body, name="bwd_tail", grid=(t // tm,),
        in_specs=[pl.BlockSpec((HEADS, HEAD_PAD, tm), lambda i: (0, 0, i)), heads, heads, tile(n_head),
                  pl.BlockSpec((tm, 1), lambda i: (i, 0)), full(invf_row), full(w_heads), full(q_g), full(kv_g),
                  tile(D_MODEL), tile(D_MODEL), tile(drest.shape[1]), full(wp_in)],
        out_specs=[tile(D_MODEL)] + [a[0] for a in accs],
        out_shape=[jax.ShapeDtypeStruct((t, D_MODEL), F32)] + [a[1] for a in accs],
        compiler_params=_cparams(("arbitrary",), vmem_limit=BWD_TAIL_VMEM_LIMIT),
    )(dq, dk, dv, proj, pos_col, invf_row, w_heads, q_g, kv_g, x, dr, drest, wp_in)


def _adam(parts, w, m, v, *, name, tile_rows):
    n, rows, cols = parts.shape

    def body(p_ref, w_ref, m_ref, v_ref, g_ref, d_ref, nm_ref, nv_ref):
        g = p_ref[0].astype(F32)
        for s in range(1, n):
            g = g + p_ref[s].astype(F32)
        m_new = ADAM_B1 * m_ref[...] + (1.0 - ADAM_B1) * g
        v_new = ADAM_B2 * v_ref[...] + (1.0 - ADAM_B2) * (g * g)
        m_hat = m_new / (1.0 - ADAM_B1 ** ADAM_STEP)
        v_hat = v_new / (1.0 - ADAM_B2 ** ADAM_STEP)
        g_ref[...] = g
        d_ref[...] = -ADAM_LR * (m_hat / (jnp.sqrt(v_hat) + ADAM_EPS) + ADAM_WD * w_ref[...])
        nm_ref[...] = m_new
        nv_ref[...] = v_new

    flat = pl.BlockSpec((tile_rows, cols), lambda i: (i, 0))
    shape = jax.ShapeDtypeStruct((rows, cols), F32)
    return pl.pallas_call(
        body, name=name, grid=(rows // tile_rows,),
        in_specs=[pl.BlockSpec((n, tile_rows, cols), lambda i: (0, i, 0)), flat, flat, flat],
        out_specs=[flat] * 4, out_shape=[shape] * 4,
        compiler_params=_cparams(("arbitrary",)),
    )(parts, w, m, v)


SMALL_NAMES = ("q_norm_g", "kv_norm_g", "sgu_norm_g", "sgu_norm_b", "b_spatial", "ln_g", "ln_b")
SMALL_SIZES = (Q_LORA, KV_LORA, G_WIDTH, G_WIDTH, HEADS * CHUNK, D_MODEL, D_MODEL)


def _pack_small(vals, last=None):
    flat = jnp.concatenate([v.reshape(-1) for v in vals])
    pad = SMALL_LEN - flat.shape[0]
    if last is None:
        return jnp.pad(flat, (0, pad))
    return jnp.concatenate([flat, jnp.zeros((pad - 1,), F32), last.reshape(1)])


def _unpack_small(flat):
    out, at = [], 0
    for n in SMALL_SIZES:
        out.append(flat[at:at + n])
        at += n
    out[4] = out[4].reshape(HEADS, CHUNK)
    return out


UQ_SHARD = HEADS * (NOPE + ROPE) // N_DEV
HEAD_ROWS = Q_LORA + KV_LORA
MIXED_ROWS = HEAD_ROWS + CHUNK + SMALL_LEN // N_DEV // LANES


def _head_slab(w_uq_shard, w_ukv_shard):
    return jnp.concatenate([jnp.pad(w_uq_shard, ((0, 0), (0, LANES - UQ_SHARD))), w_ukv_shard])


IN_SHARD = D_IN // N_DEV


def _w_in_pieces():
    split = Q_LORA + KV_LORA
    moves = ((0, split, 0), (split, split + ROPE, KR_LO), (split + ROPE, D_IN, LANES - ROPE))
    pieces = []
    for s in range(N_DEV):
        lo, hi = s * IN_SHARD, (s + 1) * IN_SHARD
        for a, b, shift in moves:
            a, b = max(a, lo), min(b, hi)
            if a < b:
                pieces.append((s, a - lo, a + shift, b - a))
    return pieces


def _padded_w_in(shards):
    tr = TOKEN_TILE

    def body(sh_ref, o_ref):
        o_ref[...] = jnp.zeros_like(o_ref)
        for s, src, dst, width in _w_in_pieces():
            o_ref[:, dst:dst + width] = sh_ref[s, :, src:src + width]

    return pl.pallas_call(
        body, name="w_in_pad", grid=(D_MODEL // tr,),
        in_specs=[pl.BlockSpec((N_DEV, tr, IN_SHARD), lambda i: (0, i, 0))],
        out_specs=pl.BlockSpec((tr, D_IN_PAD), lambda i: (i, 0)),
        out_shape=jax.ShapeDtypeStruct((D_MODEL, D_IN_PAD), shards.dtype),
        compiler_params=_cparams(("arbitrary",)),
    )(shards)


def _w_in_shards(dwp_in):
    tr = TOKEN_TILE
    by_shard = [[p for p in _w_in_pieces() if p[0] == s] for s in range(N_DEV)]

    def body(w_ref, o_ref):
        for s, pieces in enumerate(by_shard):
            parts = [w_ref[:, dst:dst + width] for _, _, dst, width in pieces]
            o_ref[s] = parts[0] if len(parts) == 1 else jnp.concatenate(parts, axis=1)

    return pl.pallas_call(
        body, name="w_in_split", grid=(D_MODEL // tr,),
        in_specs=[pl.BlockSpec((tr, D_IN_PAD), lambda i: (i, 0))],
        out_specs=pl.BlockSpec((N_DEV, tr, IN_SHARD), lambda i: (0, i, 0)),
        out_shape=jax.ShapeDtypeStruct((N_DEV, D_MODEL, IN_SHARD), dwp_in.dtype),
        compiler_params=_cparams(("arbitrary",)),
    )(dwp_in)


def kernel(x, positions, w_in, q_norm_g, w_uq, kv_norm_g, w_ukv, sgu_norm_g, sgu_norm_b, w_spatial, b_spatial, w_out, ln_g, ln_b, loss_target, m_w_in, m_q_norm_g, m_w_uq, m_kv_norm_g, m_w_ukv, m_sgu_norm_g, m_sgu_norm_b, m_w_spatial, m_b_spatial, m_w_out, m_ln_g, m_ln_b, v_w_in, v_q_norm_g, v_w_uq, v_kv_norm_g, v_w_ukv, v_sgu_norm_g, v_sgu_norm_b, v_w_spatial, v_b_spatial, v_w_out, v_ln_g, v_ln_b):
    me = 4 * lax.axis_index("x") + 2 * lax.axis_index("y") + lax.axis_index("c")
    seq = x.shape[1]
    x2 = x.reshape(seq, D_MODEL)
    tgt2 = loss_target.reshape(seq, D_MODEL)
    pos_col = positions.reshape(seq, 1)

    w_in_shards, w_out_shards, w_heads = _gather_two_level(
        [w_in.astype(BF16), w_out.astype(BF16), _head_slab(w_uq, w_ukv).astype(BF16)],
        name="wgather")
    (loss_part, grad_x, d_in, d_heads, d_out, d_ws, d_bs_t, d_lng, d_lnb, d_sgug, d_sgub, d_qg, d_kvg) = _local_step(
        x2, tgt2, pos_col, w_in_shards, w_heads, w_out_shards.reshape(D_MODEL, D_MODEL), q_norm_g, kv_norm_g,
        sgu_norm_g, sgu_norm_b, w_spatial, b_spatial, ln_g, ln_b)

    small_part = _pack_small([d_qg, d_kvg, d_sgug, d_sgub, d_bs_t[:, :HEADS].T, d_lng, d_lnb], last=loss_part[0, :1])
    mixed = jnp.concatenate([d_heads, d_ws, small_part.reshape(N_DEV, -1, LANES)], axis=1)
    by_chip = [g.reshape((N_CHIPS, 2) + g.shape[1:])
               for g in (d_in, d_out.reshape(N_DEV, D_MODEL // N_DEV, D_MODEL), mixed)]
    from_sibling = _sibling_swap(by_chip, name="gswap")
    core = lax.axis_index("c").astype(jnp.int32).reshape(1)
    pair_sums = [_pair_sum(a, b, core, name=nm, tile_rows=tr, out_dtype=dt) for a, b, nm, tr, dt in zip(
        by_chip, from_sibling, ("gsum_in", "gsum_out", "gsum_mixed"), (TOKEN_TILE, D_MODEL // N_DEV, MIXED_ROWS),
        (BF16, BF16, F32))]
    recv_in, recv_out, recv_mixed = _chip_exchange(pair_sums, name="gexch")

    take = lambda a: lax.dynamic_index_in_dim(a, me, 0, keepdims=False)
    small_w = _pack_small([q_norm_g, kv_norm_g, sgu_norm_g, sgu_norm_b, b_spatial, ln_g, ln_b])
    small_m = _pack_small([m_q_norm_g, m_kv_norm_g, m_sgu_norm_g, m_sgu_norm_b, m_b_spatial, m_ln_g, m_ln_b])
    small_v = _pack_small([v_q_norm_g, v_kv_norm_g, v_sgu_norm_g, v_sgu_norm_b, v_b_spatial, v_ln_g, v_ln_b])
    own_mixed = lambda uq, ukv, sp, small: jnp.concatenate(
        [_head_slab(uq, ukv), take(sp), take(small.reshape(N_DEV, -1, LANES))])
    res_in = _adam(recv_in, w_in, m_w_in, v_w_in, name="adam_in", tile_rows=TOKEN_TILE)
    res_out = _adam(recv_out, w_out, m_w_out, v_w_out, name="adam_out", tile_rows=D_MODEL // N_DEV)
    res_mixed = _adam(recv_mixed, own_mixed(w_uq, w_ukv, w_spatial, small_w), own_mixed(m_w_uq, m_w_ukv, m_w_spatial, small_m),
                      own_mixed(v_w_uq, v_w_ukv, v_w_spatial, small_v), name="adam_mixed", tile_rows=MIXED_ROWS)

    rep_g, = _exchange([res_mixed[0][HEAD_ROWS:]], name="sgather", per_destination=False)
    rep_pack = lambda sp, small: jnp.concatenate(
        [sp.reshape(N_DEV, CHUNK, LANES), small.reshape(N_DEV, -1, LANES)], axis=1).reshape(-1, LANES)
    _, delta_rep, m_rep, v_rep = _adam(rep_g.reshape(1, N_DEV * REP_ROWS, LANES), rep_pack(w_spatial, small_w),
                                       rep_pack(m_w_spatial, small_m), rep_pack(v_w_spatial, small_v),
                                       name="adam_rep", tile_rows=N_DEV * REP_ROWS)

    def rep_unpack(a):
        a = a.reshape(N_DEV, REP_ROWS, LANES)
        small = _unpack_small(a[:, CHUNK:].reshape(-1))
        return [small[0], small[1], small[2], small[3], a[:, :CHUNK], small[4], small[5], small[6]]

    def ordered(which, rep):
        r_qg, r_kvg, r_sg, r_sb, r_ws, r_bs, r_lg, r_lb = rep_unpack(rep)
        heads = res_mixed[which]
        return [res_in[which], r_qg, heads[:Q_LORA, :UQ_SHARD], r_kvg, heads[Q_LORA:HEAD_ROWS], r_sg, r_sb, r_ws, r_bs,
                res_out[which], r_lg, r_lb]

    loss = rep_g[N_DEV - 1, REP_ROWS - 1, LANES - 1]
    outs = [loss, grad_x.reshape(x.shape)]
    outs += ordered(0, rep_g.reshape(-1, LANES))
    outs += ordered(1, delta_rep)
    outs += ordered(2, m_rep)
    outs += ordered(3, v_rep)
    return tuple(outs)


def _local_step(x2, tgt2, pos_col, w_in_shards, w_heads, w_out_full, q_norm_g, kv_norm_g, sgu_norm_g, sgu_norm_b,
                w_spatial, b_spatial, ln_g, ln_b):
    wp_in = _padded_w_in(w_in_shards)

    half = jnp.arange(HALF, dtype=F32)
    inv_freq = 1.0 / (ROPE_THETA ** (half / HALF))
    invf_row = jnp.concatenate([jnp.zeros((KR_LO,), F32), inv_freq, inv_freq,
                                jnp.zeros((LANES - KR_LO - ROPE,), F32)]).reshape(1, LANES)
    tri = jnp.tril(jnp.ones((CHUNK, CHUNK), dtype=bool))
    ws_low = jnp.where(tri[None], w_spatial, 0.0).astype(BF16)
    ws_low_t = ws_low.transpose(0, 2, 1)
    bsp = jnp.repeat(b_spatial.T, G_HEAD_DIM, axis=1)
    row = lambda a: a.reshape(1, -1)

    proj, q, k, v, vt = _fwd_proj(x2, pos_col, invf_row, wp_in, w_heads, row(q_norm_g), row(kv_norm_g))
    o, lse_row = _attn_fwd(q, k, vt)
    (dr, do, d_row, drest, d_out, d_ws, d_bs_t, d_lng, d_lnb, d_sgug, d_sgub, loss_part) = _mid(
        x2, tgt2, proj, o, w_out_full, ws_low, ws_low_t, bsp, row(sgu_norm_g), row(sgu_norm_b), row(ln_g), row(ln_b))
    dqt, dk, dv = _attn_bwd(q, k, v, do, lse_row, d_row)
    grad_x, dwp_in, d_heads, d_qg, d_kvg = _bwd_tail(dqt, dk, dv, proj, pos_col, invf_row, w_heads, row(q_norm_g),
                                                      row(kv_norm_g), x2, dr, drest, wp_in)
    return (loss_part, grad_x, _w_in_shards(dwp_in), d_heads, d_out, d_ws, d_bs_t, d_lng, d_lnb, d_sgug, d_sgub,
            d_qg, d_kvg)
```

```python
import functools
import math

import jax
import jax.numpy as jnp
from jax import lax
from jax.experimental import pallas as pl
from jax.experimental.pallas import tpu as pltpu

F32 = jnp.float32
BF16 = jnp.bfloat16

N_DEV = 8
D_MODEL = 1024
HEADS = 8
NOPE = 64
ROPE = 32
HALF = ROPE // 2
VDIM = 64
Q_LORA = 256
KV_LORA = 128
G_WIDTH = 512
G_HEAD_DIM = 64
CHUNK = 128
HEAD_PAD = 128
D_IN = 2464
D_IN_PAD = 2560
KR_LO = NOPE
ROPE_THETA = 10000.0
DN_ALPHA = 2.0 ** 0.25
EPS = 1e-5
ATTN_SCALE = 1.0 / math.sqrt(NOPE + ROPE)
ADAM_LR, ADAM_B1, ADAM_B2, ADAM_EPS, ADAM_WD, ADAM_STEP = 0.001, 0.9, 0.999, 1e-08, 0.01, 10

LANES = 128
REP_ROWS = 136
SMALL_LEN = 8192
VMEM_LIMIT = 56 * 1024 * 1024
ATTN_BWD_VMEM_LIMIT = 61 * 1024 * 1024
BWD_TAIL_VMEM_LIMIT = 61 * 1024 * 1024

TOKEN_TILE = 256
PROJ_TILE = 512
ATTN_FWD_WIDE = 2048
ATTN_BWD_WIDE = 2048
ATTN_NARROW = 512
SOFTMAX_ROWS = 512
LOG2E = 1.4426950408889634
LN2 = 0.6931471805599453
Q_PRESCALE = ATTN_SCALE * LOG2E


def _cparams(sem=None, vmem_limit=VMEM_LIMIT):
    return pltpu.CompilerParams(dimension_semantics=sem, vmem_limit_bytes=vmem_limit)


def _dot(a, b):
    return jnp.dot(a, b, preferred_element_type=F32)


def _dot_nt(a, b):
    return lax.dot_general(a, b, (((1,), (1,)), ((), ())), preferred_element_type=F32)


def _dot_tn(a, b):
    return lax.dot_general(a, b, (((0,), (0,)), ((), ())), preferred_element_type=F32)


def _as_row(col):
    return jnp.transpose(jnp.broadcast_to(col, (col.shape[0], LANES)))[0:1, :]


def _sigmoid(z):
    return 1.0 / (1.0 + jnp.exp(-z))


def _gelu(x):
    return 0.5 * x * (1.0 + lax.erf(x * 0.7071067811865476))


def _gelu_grad(x):
    cdf = 0.5 * (1.0 + lax.erf(x * 0.7071067811865476))
    return cdf + x * jnp.exp(-0.5 * x * x) * 0.3989422804014327


def _exchange(srcs, *, name, per_destination):
    n = len(srcs)
    slab_shapes = [s.shape[1:] if per_destination else s.shape for s in srcs]

    def body(*refs):
        src_refs, out_refs = refs[:n], refs[n:2 * n]
        send_sems, recv_sems, local_sems = refs[2 * n:]
        x, y, c = lax.axis_index("x"), lax.axis_index("y"), lax.axis_index("c")
        me = 4 * x + 2 * y + c

        def slab_for(t, dest):
            return src_refs[t].at[dest] if per_destination else src_refs[t]

        mine = [pltpu.make_async_copy(slab_for(t, me), out_refs[t].at[me], local_sems.at[t]) for t in range(n)]
        for cp in mine:
            cp.start()
        sends, arrivals = [], []
        for k in (6, 7, 4, 5, 2, 3, 1):
            px = 1 - x if k & 4 else x
            py = 1 - y if k & 2 else y
            pc = 1 - c if k & 1 else c
            peer = 4 * px + 2 * py + pc
            for t in range(n):
                sem = (k - 1) * n + t
                cp = pltpu.make_async_remote_copy(
                    src_ref=slab_for(t, peer), dst_ref=out_refs[t].at[me],
                    send_sem=send_sems.at[sem], recv_sem=recv_sems.at[sem],
                    device_id=(px, py, pc), device_id_type=pl.DeviceIdType.MESH)
                cp.start()
                sends.append(cp)
                arrivals.append(pltpu.make_async_remote_copy(
                    src_ref=slab_for(t, peer), dst_ref=out_refs[t].at[peer],
                    send_sem=send_sems.at[sem], recv_sem=recv_sems.at[sem],
                    device_id=(x, y, c), device_id_type=pl.DeviceIdType.MESH))
        for cp in arrivals:
            cp.wait_recv()
        for cp in sends:
            cp.wait_send()
        for cp in mine:
            cp.wait()

    hbm = pl.BlockSpec(memory_space=pl.ANY)
    return pl.pallas_call(
        body, name=name,
        out_shape=[jax.ShapeDtypeStruct((N_DEV,) + tuple(shape), s.dtype) for shape, s in zip(slab_shapes, srcs)],
        in_specs=[hbm] * n, out_specs=[hbm] * n,
        scratch_shapes=[pltpu.SemaphoreType.DMA(((N_DEV - 1) * n,)), pltpu.SemaphoreType.DMA(((N_DEV - 1) * n,)),
                        pltpu.SemaphoreType.DMA((n,))],
    )(*srcs)


def _gather_two_level(srcs, *, name):
    n = len(srcs)

    def body(*refs):
        src_refs, out_refs = refs[:n], refs[n:2 * n]
        send_sems, recv_sems, local_sems = refs[2 * n:]
        x, y, c = lax.axis_index("x"), lax.axis_index("y"), lax.axis_index("c")
        me, sibling = (x, y, c), (x, y, 1 - c)
        chips = [(1 - x, 1 - y), (1 - x, y), (x, 1 - y)]
        index = lambda px, py, pc: 4 * px + 2 * py + pc

        def copy(k, t, block, to, src=None):
            place = out_refs[t].at[index(*block)]
            return pltpu.make_async_remote_copy(
                src_ref=place if src is None else src, dst_ref=place,
                send_sem=send_sems.at[k * n + t], recv_sem=recv_sems.at[k * n + t],
                device_id=to, device_id_type=pl.DeviceIdType.MESH)

        mine = [pltpu.make_async_copy(src_refs[t], out_refs[t].at[index(*me)], local_sems.at[t]) for t in range(n)]
        for cp in mine:
            cp.start()
        first = [copy(1 + j, t, me, (*chip, c), src=src_refs[t]) for j, chip in enumerate(chips) for t in range(n)]
        first += [copy(0, t, me, sibling, src=src_refs[t]) for t in range(n)]
        for cp in first:
            cp.start()
        passed = []
        for j, chip in enumerate(chips):
            for t in range(n):
                copy(1 + j, t, (*chip, c), me).wait_recv()
                cp = copy(4 + j, t, (*chip, c), sibling)
                cp.start()
                passed.append(cp)
        for t in range(n):
            copy(0, t, sibling, me).wait_recv()
        for j, chip in enumerate(chips):
            for t in range(n):
                copy(4 + j, t, (*chip, 1 - c), me).wait_recv()
        for cp in first + passed:
            cp.wait_send()
        for cp in mine:
            cp.wait()

    hbm = pl.BlockSpec(memory_space=pl.ANY)
    return pl.pallas_call(
        body, name=name,
        out_shape=[jax.ShapeDtypeStruct((N_DEV,) + s.shape, s.dtype) for s in srcs],
        in_specs=[hbm] * n, out_specs=[hbm] * n,
        scratch_shapes=[pltpu.SemaphoreType.DMA((7 * n,)), pltpu.SemaphoreType.DMA((7 * n,)),
                        pltpu.SemaphoreType.DMA((n,))],
    )(*srcs)


N_CHIPS = N_DEV // 2


def _sibling_swap(srcs, *, name):
    n = len(srcs)

    def body(*refs):
        src_refs, out_refs = refs[:n], refs[n:2 * n]
        send_sems, recv_sems = refs[2 * n:]
        x, y, c = lax.axis_index("x"), lax.axis_index("y"), lax.axis_index("c")
        sends = []
        for chip in range(N_CHIPS):
            for t in range(n):
                cp = pltpu.make_async_remote_copy(
                    src_ref=src_refs[t].at[chip, 1 - c], dst_ref=out_refs[t].at[chip],
                    send_sem=send_sems.at[chip * n + t], recv_sem=recv_sems.at[chip * n + t],
                    device_id=(x, y, 1 - c), device_id_type=pl.DeviceIdType.MESH)
                cp.start()
                sends.append(cp)
        for cp in sends:
            cp.wait_recv()
        for cp in sends:
            cp.wait_send()

    hbm = pl.BlockSpec(memory_space=pl.ANY)
    return pl.pallas_call(
        body, name=name,
        out_shape=[jax.ShapeDtypeStruct((N_CHIPS,) + s.shape[2:], s.dtype) for s in srcs],
        in_specs=[hbm] * n, out_specs=[hbm] * n,
        scratch_shapes=[pltpu.SemaphoreType.DMA((N_CHIPS * n,)), pltpu.SemaphoreType.DMA((N_CHIPS * n,))],
    )(*srcs)


def _pair_sum(mine, theirs, core, *, name, tile_rows, out_dtype):
    _, _, rows, cols = mine.shape

    def body(core_ref, a_ref, b_ref, o_ref):
        o_ref[...] = (a_ref[0] + b_ref[...]).astype(out_dtype)

    return pl.pallas_call(
        body, name=name,
        grid_spec=pltpu.PrefetchScalarGridSpec(
            num_scalar_prefetch=1, grid=(N_CHIPS, rows // tile_rows),
            in_specs=[pl.BlockSpec((1, 1, tile_rows, cols), lambda q, r, core_ref: (q, core_ref[0], r, 0)),
                      pl.BlockSpec((1, tile_rows, cols), lambda q, r, core_ref: (q, r, 0))],
            out_specs=pl.BlockSpec((1, tile_rows, cols), lambda q, r, core_ref: (q, r, 0))),
        out_shape=jax.ShapeDtypeStruct((N_CHIPS, rows, cols), out_dtype),
        compiler_params=_cparams(("arbitrary", "arbitrary")),
    )(core, mine, theirs)


def _chip_exchange(srcs, *, name):
    n = len(srcs)

    def body(*refs):
        src_refs, out_refs = refs[:n], refs[n:2 * n]
        send_sems, recv_sems, local_sems = refs[2 * n:]
        x, y, c = lax.axis_index("x"), lax.axis_index("y"), lax.axis_index("c")
        my_chip = 2 * x + y
        mine = [pltpu.make_async_copy(src_refs[t].at[my_chip], out_refs[t].at[my_chip], local_sems.at[t])
                for t in range(n)]
        for cp in mine:
            cp.start()
        sends, arrivals = [], []
        for k in (3, 2, 1):
            px = 1 - x if k & 2 else x
            py = 1 - y if k & 1 else y
            peer_chip = 2 * px + py
            for t in range(n):
                sem = (k - 1) * n + t
                cp = pltpu.make_async_remote_copy(
                    src_ref=src_refs[t].at[peer_chip], dst_ref=out_refs[t].at[my_chip],
                    send_sem=send_sems.at[sem], recv_sem=recv_sems.at[sem],
                    device_id=(px, py, c), device_id_type=pl.DeviceIdType.MESH)
                cp.start()
                sends.append(cp)
                arrivals.append(pltpu.make_async_remote_copy(
                    src_ref=src_refs[t].at[peer_chip], dst_ref=out_refs[t].at[peer_chip],
                    send_sem=send_sems.at[sem], recv_sem=recv_sems.at[sem],
                    device_id=(x, y, c), device_id_type=pl.DeviceIdType.MESH))
        for cp in arrivals:
            cp.wait_recv()
        for cp in sends:
            cp.wait_send()
        for cp in mine:
            cp.wait()

    hbm = pl.BlockSpec(memory_space=pl.ANY)
    return pl.pallas_call(
        body, name=name,
        out_shape=[jax.ShapeDtypeStruct(s.shape, s.dtype) for s in srcs],
        in_specs=[hbm] * n, out_specs=[hbm] * n,
        scratch_shapes=[pltpu.SemaphoreType.DMA((3 * n,)), pltpu.SemaphoreType.DMA((3 * n,)),
                        pltpu.SemaphoreType.DMA((n,))],
    )(*srcs)


def _rope_tables(pos_col, invf_row):
    ang = pos_col.astype(F32) * invf_row
    lane = lax.broadcasted_iota(jnp.int32, ang.shape, 1)
    cos, sin = jnp.cos(ang), jnp.sin(ang)
    first = (lane >= KR_LO) & (lane < KR_LO + HALF)
    second = (lane >= KR_LO + HALF) & (lane < KR_LO + ROPE)
    return cos, jnp.where(first, sin, 0.0), jnp.where(second, sin, 0.0)


def _rope(t, cos, sin_first, sin_second, sign):
    up = pltpu.roll(t, LANES - HALF, 1)
    down = pltpu.roll(t, HALF, 1)
    return t * cos - sign * (up * sin_first) + sign * (down * sin_second)


def _fwd_proj(x, pos_col, invf_row, wp_in, w_heads, q_g, kv_g):
    t = x.shape[0]
    tm = PROJ_TILE

    def body(x_ref, pos_ref, invf_ref, win_ref, wh_ref, qg_ref, kvg_ref,
             proj_ref, q_ref, k_ref, v_ref, vt_ref):
        proj = _dot(x_ref[...].astype(BF16), win_ref[...])
        proj_ref[...] = proj
        c_q = proj[:, :Q_LORA]
        c_kv = proj[:, Q_LORA:Q_LORA + KV_LORA]
        kr_raw = proj[:, Q_LORA + KV_LORA:Q_LORA + KV_LORA + LANES]
        cqn = (c_q * lax.rsqrt(jnp.mean(c_q * c_q, axis=-1, keepdims=True) + EPS) * qg_ref[...]).astype(BF16)
        ckvn = (c_kv * lax.rsqrt(jnp.mean(c_kv * c_kv, axis=-1, keepdims=True) + EPS) * kvg_ref[...]).astype(BF16)
        cos, s1, s2 = _rope_tables(pos_ref[...], invf_ref[...])
        kr = _rope(kr_raw, cos, s1, s2, 1.0)
        lane = lax.broadcasted_iota(jnp.int32, (tm, HEAD_PAD), 1)
        for h in range(HEADS):
            q_h = _dot(cqn, wh_ref[h, :Q_LORA, :])
            kv_h = _dot(ckvn, wh_ref[h, Q_LORA:, :])
            q_ref[h] = (_rope(q_h, cos, s1, s2, 1.0) * Q_PRESCALE).astype(BF16)
            k_ref[h] = jnp.where(lane < NOPE, kv_h, kr).astype(BF16)
            v_ref[h] = kv_h.astype(BF16)
            vt_ref[h] = jnp.transpose(kv_h).astype(BF16)

    full = lambda a: pl.BlockSpec(a.shape, lambda i: (0,) * a.ndim)
    head_spec = pl.BlockSpec((HEADS, tm, HEAD_PAD), lambda i: (0, i, 0))
    head_shape = jax.ShapeDtypeStruct((HEADS, t, HEAD_PAD), BF16)
    return pl.pallas_call(
        body, name="fwd_proj", grid=(t // tm,),
        in_specs=[pl.BlockSpec((tm, D_MODEL), lambda i: (i, 0)), pl.BlockSpec((tm, 1), lambda i: (i, 0)),
                  full(invf_row), full(wp_in), full(w_heads), full(q_g), full(kv_g)],
        out_specs=[pl.BlockSpec((tm, D_IN_PAD), lambda i: (i, 0)), head_spec, head_spec, head_spec,
                   pl.BlockSpec((HEADS, HEAD_PAD, tm), lambda i: (0, 0, i))],
        out_shape=[jax.ShapeDtypeStruct((t, D_IN_PAD), F32), head_shape, head_shape, head_shape,
                   jax.ShapeDtypeStruct((HEADS, HEAD_PAD, t), BF16)],
        compiler_params=_cparams(("arbitrary",)),
    )(x, pos_col, invf_row, wp_in, w_heads, q_g, kv_g)


def _attn_fwd(q, k, vt):
    t = q.shape[1]
    bq, bk = ATTN_FWD_WIDE, ATTN_NARROW
    n_diag = bq // bk
    chunk = SOFTMAX_ROWS

    def body(q_ref, k_ref, vt_ref, o_ref, lse_ref, s0, s1, p0, p1, x0, x1, m_scr, l_scr, a_scr, acc_scr):
        i = pl.program_id(1)
        at = lambda j: pl.ds(pl.multiple_of(j * bk, bk), bk)

        def exp_pass(s_in, block_max, p_out, diagonal=False, cols=slice(None)):
            width = bq if cols == slice(None) else cols.stop - cols.start

            def load(r):
                s = s_in[r:r + chunk, cols]
                if diagonal:
                    key = lax.broadcasted_iota(jnp.int32, (chunk, width), 0) + r
                    qry = lax.broadcasted_iota(jnp.int32, (chunk, width), 1)
                    s = jnp.where(qry >= key, s, -jnp.inf)
                return s

            if diagonal:
                block_max = jnp.max(load(0), axis=0, keepdims=True)
                for r in range(chunk, bk, chunk):
                    block_max = jnp.maximum(block_max, jnp.max(load(r), axis=0, keepdims=True))
            m_old = m_scr[:, cols]
            m_new = jnp.maximum(m_old, block_max)
            alpha = jnp.exp2(m_old - m_new)
            total = jnp.zeros((1, width), F32)
            for r in range(0, bk, chunk):
                p = jnp.exp2(load(r) - m_new)
                p_out[r:r + chunk, cols] = p.astype(BF16)
                total = total + jnp.sum(p, axis=0, keepdims=True)
            m_scr[:, cols] = m_new
            l_scr[:, cols] = alpha * l_scr[:, cols] + total
            return alpha

        def scores(j, s_out, x_out):
            s = _dot_nt(k_ref[0, at(j), :], q_ref[0])
            s_out[...] = s
            x_out[...] = jnp.max(s, axis=0, keepdims=True)

        def value_product(j, p_in):
            return _dot(vt_ref[0, NOPE:, at(j)], p_in[...])

        def one_pass(j, s_in, x_in, s_out, x_out, p_prev, p_cur):
            scores(j + 1, s_out, x_out)
            acc_scr[...] = a_scr[...] * acc_scr[...] + value_product(jnp.maximum(j - 1, 0), p_prev)
            a_scr[...] = exp_pass(s_in, x_in[...], p_cur)

        scores(0, s0, x0)
        p1[...] = jnp.zeros_like(p1)
        a_scr[...] = jnp.ones_like(a_scr)
        m_scr[...] = jnp.full(m_scr.shape, -jnp.inf, F32)
        l_scr[...] = jnp.zeros_like(l_scr)
        acc_scr[...] = jnp.zeros_like(acc_scr)

        def two_passes(n, _):
            one_pass(2 * n, s0, x0, s1, x1, p1, p0)
            one_pass(2 * n + 1, s1, x1, s0, x0, p0, p1)
            return 0

        lax.fori_loop(0, (n_diag // 2) * i, two_passes, 0)
        d = n_diag * i
        alpha, p_prev, cols = a_scr[...], p1, slice(0, bq)
        for u in range(n_diag + 1):
            s_in, s_next, p_cur = (s0, s1, p0) if u % 2 == 0 else (s1, s0, p1)
            if u + 1 < n_diag:
                ahead = slice((u + 1) * bk, bq)
                s_next[:, ahead] = _dot_nt(k_ref[0, at(d + u + 1), :], q_ref[0, ahead, :])
            acc_scr[:, cols] = alpha * acc_scr[:, cols] + _dot(vt_ref[0, NOPE:, at(jnp.maximum(d + u - 1, 0))],
                                                               p_prev[:, cols])
            if u < n_diag:
                cols = slice(u * bk, bq)
                alpha = exp_pass(s_in, None, p_cur, diagonal=True, cols=cols)
                p_prev = p_cur
        o = jnp.transpose(acc_scr[...] / l_scr[...])
        o_ref[0] = jnp.concatenate([jnp.zeros_like(o), o], axis=1)
        lse_ref[0] = m_scr[...] + jnp.log2(l_scr[...])

    tile = lambda dtype: pltpu.VMEM((bk, bq), dtype)
    stat = pltpu.VMEM((1, bq), F32)
    return pl.pallas_call(
        body, name="attn_fwd", grid=(HEADS, t // bq),
        in_specs=[pl.BlockSpec((1, bq, HEAD_PAD), lambda h, i: (h, i, 0)),
                  pl.BlockSpec((1, t, HEAD_PAD), lambda h, i: (h, 0, 0)),
                  pl.BlockSpec((1, HEAD_PAD, t), lambda h, i: (h, 0, 0))],
        out_specs=[pl.BlockSpec((1, bq, HEAD_PAD), lambda h, i: (h, i, 0)),
                   pl.BlockSpec((1, 1, bq), lambda h, i: (h, 0, i))],
        out_shape=[jax.ShapeDtypeStruct((HEADS, t, HEAD_PAD), F32), jax.ShapeDtypeStruct((HEADS, 1, t), F32)],
        scratch_shapes=[tile(F32), tile(F32), tile(BF16), tile(BF16), stat, stat, stat, stat, stat,
                        pltpu.VMEM((VDIM, bq), F32)],
        compiler_params=_cparams(("arbitrary", "arbitrary")),
    )(q, k, vt)


def _mid(x, target, proj, ol, w_out, ws_low, ws_low_t, bsp, sgu_g, sgu_b, ln_g, ln_b):
    t = x.shape[0]
    tm = TOKEN_TILE
    n_steps = t // tm

    def body(x_ref, tgt_ref, za_ref, u_ref, v_ref, zb_ref, ol_ref, wout_ref, ws_ref, wst_ref, bsp_ref,
             sg_ref, sb_ref, lg_ref, lb_ref,
             dr_ref, do_ref, drow_ref, drest_ref, dwout_ref, dws_ref, dbs_ref, dlg_ref, dlb_ref, dsg_ref, dsb_ref,
             loss_ref, dbsp_acc):
        step = pl.program_id(0)

        @pl.when(step == 0)
        def _():
            dwout_ref[...] = jnp.zeros_like(dwout_ref)
            dws_ref[...] = jnp.zeros_like(dws_ref)
            dbs_ref[...] = jnp.zeros_like(dbs_ref)
            dlg_ref[...] = jnp.zeros_like(dlg_ref)
            dlb_ref[...] = jnp.zeros_like(dlb_ref)
            dsg_ref[...] = jnp.zeros_like(dsg_ref)
            dsb_ref[...] = jnp.zeros_like(dsb_ref)
            loss_ref[...] = jnp.zeros_like(loss_ref)
            dbsp_acc[...] = jnp.zeros_like(dbsp_acc)

        n_chunks = tm // CHUNK
        groups = G_WIDTH // LANES

        def side_by_side(a):
            return [jnp.concatenate([a[c * CHUNK:(c + 1) * CHUNK, g * LANES:(g + 1) * LANES] for c in range(n_chunks)],
                                    axis=1) for g in range(groups)]

        def by_chunk(wide):
            return jnp.concatenate([jnp.concatenate([wide[g][:, c * LANES:(c + 1) * LANES] for g in range(groups)], axis=1)
                                    for c in range(n_chunks)], axis=0)

        def own_lanes(h):
            lane = lax.broadcasted_iota(jnp.int32, (CHUNK, n_chunks * LANES), 1)
            return (lane % LANES) // G_HEAD_DIM == h % 2

        def spatial(w_ref, wide):
            return [sum(jnp.where(own_lanes(h), _dot(w_ref[h], wide[g]), 0.0) for h in (2 * g, 2 * g + 1))
                    for g in range(groups)]

        attn = jnp.concatenate([ol_ref[h][:, NOPE:] for h in range(HEADS)], axis=-1)
        za = za_ref[...]
        sig_a = _sigmoid(za)
        silu_a = za * sig_a
        out_a = attn * silu_a
        u = u_ref[...]
        ug = _gelu(u)
        vpre = v_ref[...]
        gv = _gelu(vpre)
        mu_v = jnp.mean(gv, axis=-1, keepdims=True)
        cen_v = gv - mu_v
        rstd_v = lax.rsqrt(jnp.mean(cen_v * cen_v, axis=-1, keepdims=True) + EPS)
        vhat = cen_v * rstd_v
        vg = vhat * sg_ref[...] + sb_ref[...]
        vg_b = vg.astype(BF16)
        sv = by_chunk(spatial(ws_ref, side_by_side(vg_b))) + jnp.tile(bsp_ref[...], (n_chunks, 1))
        sgu = ug * sv
        zb = zb_ref[...]
        sig_b = _sigmoid(zb)
        silu_b = zb * sig_b
        out_b = sgu * silu_b
        merged = jnp.concatenate([out_a, out_b], axis=-1).astype(BF16)
        r = DN_ALPHA * x_ref[...] + _dot(merged, wout_ref[...])
        mu = jnp.mean(r, axis=-1, keepdims=True)
        cen = r - mu
        rstd = lax.rsqrt(jnp.mean(cen * cen, axis=-1, keepdims=True) + EPS)
        xhat = cen * rstd
        hout = xhat * lg_ref[...] + lb_ref[...]
        err = hout - tgt_ref[...]
        row_loss = jnp.mean(err * err, axis=-1, keepdims=True)
        loss_ref[...] += jnp.broadcast_to(0.5 * jnp.sum(row_loss, axis=0, keepdims=True), loss_ref.shape)

        dh = err * (1.0 / D_MODEL)
        dlg_ref[...] += jnp.sum(dh * xhat, axis=0, keepdims=True)
        dlb_ref[...] += jnp.sum(dh, axis=0, keepdims=True)
        dxhat = dh * lg_ref[...]
        dr = rstd * (dxhat - jnp.mean(dxhat, axis=-1, keepdims=True)
                     - xhat * jnp.mean(dxhat * xhat, axis=-1, keepdims=True))
        dr_ref[...] = dr
        dr_b = dr.astype(BF16)
        dwout_ref[...] += _dot_tn(merged, dr_b)
        dmerged = _dot_nt(dr_b, wout_ref[...])
        d_out_a = dmerged[:, :G_WIDTH]
        d_out_b = dmerged[:, G_WIDTH:]
        dattn = d_out_a * silu_a
        for h in range(HEADS):
            do_h = dattn[:, h * VDIM:(h + 1) * VDIM]
            do_ref[h] = jnp.concatenate([jnp.zeros((tm, NOPE), F32), do_h], axis=-1).astype(BF16)
        feature = lax.broadcasted_iota(jnp.int32, (G_WIDTH, LANES), 0) // VDIM
        column = lax.broadcasted_iota(jnp.int32, (G_WIDTH, LANES), 1)
        head_sums = jnp.dot(dattn * attn, jnp.where(feature == column, 1.0, 0.0).astype(F32),
                            preferred_element_type=F32, precision=lax.Precision.HIGHEST)
        dsums_t = jnp.transpose(head_sums)
        for h in range(HEADS):
            drow_ref[h] = dsums_t[h:h + 1, :]
        dza = d_out_a * attn * (sig_a * (1.0 + za * (1.0 - sig_a)))
        dsgu = d_out_b * silu_b
        dzb = d_out_b * sgu * (sig_b * (1.0 + zb * (1.0 - sig_b)))
        du = dsgu * sv * _gelu_grad(u)
        dsv = dsgu * ug
        dsv_b = dsv.astype(BF16)
        for cix in range(n_chunks):
            dbsp_acc[...] += dsv[cix * CHUNK:(cix + 1) * CHUNK, :]
        dsv_wide, vg_wide = side_by_side(dsv_b), side_by_side(vg_b)
        dvg = by_chunk(spatial(wst_ref, dsv_wide))
        for h in range(HEADS):
            mine = jnp.where(own_lanes(h), dsv_wide[h // 2], jnp.zeros_like(dsv_wide[h // 2]))
            dws_ref[h] += _dot_nt(mine, vg_wide[h // 2])
        dsg_ref[...] += jnp.sum(dvg * vhat, axis=0, keepdims=True)
        dsb_ref[...] += jnp.sum(dvg, axis=0, keepdims=True)
        dvhat = dvg * sg_ref[...]
        dgv = rstd_v * (dvhat - jnp.mean(dvhat, axis=-1, keepdims=True)
                        - vhat * jnp.mean(dvhat * vhat, axis=-1, keepdims=True))
        dv = dgv * _gelu_grad(vpre)
        drest_ref[...] = jnp.concatenate([dza, du, dv, dzb], axis=-1).astype(BF16)

        @pl.when(step == n_steps - 1)
        def _():
            tri = (lax.broadcasted_iota(jnp.int32, (CHUNK, CHUNK), 0)
                   >= lax.broadcasted_iota(jnp.int32, (CHUNK, CHUNK), 1))
            for h in range(HEADS):
                dws_ref[h] = jnp.where(tri, dws_ref[h], 0.0)
            tot = dbsp_acc[...]
            lane = lax.broadcasted_iota(jnp.int32, (CHUNK, LANES), 1)
            dbs = jnp.zeros((CHUNK, LANES), F32)
            for h in range(HEADS):
                head_sum = jnp.sum(tot[:, h * G_HEAD_DIM:(h + 1) * G_HEAD_DIM], axis=-1, keepdims=True)
                dbs = jnp.where(lane == h, head_sum, dbs)
            dbs_ref[...] = dbs

    full = lambda a: pl.BlockSpec(a.shape, lambda i: (0,) * a.ndim)
    tile = lambda w, j=0: pl.BlockSpec((tm, w), lambda i, j=j: (i, j))
    heads = pl.BlockSpec((HEADS, tm, HEAD_PAD), lambda i: (0, i, 0))
    acc = lambda shape: (pl.BlockSpec(shape, lambda i: (0,) * len(shape)), jax.ShapeDtypeStruct(shape, F32))
    accs = [acc((D_MODEL, D_MODEL)), acc((HEADS, CHUNK, CHUNK)), acc((CHUNK, LANES)), acc((1, D_MODEL)),
            acc((1, D_MODEL)), acc((1, G_WIDTH)), acc((1, G_WIDTH)), acc((1, LANES))]
    return pl.pallas_call(
        body, name="mid", grid=(n_steps,),
        in_specs=[tile(D_MODEL), tile(D_MODEL), tile(G_WIDTH, 1), tile(G_WIDTH, 2), tile(G_WIDTH, 3), tile(G_WIDTH, 4),
                  heads, full(w_out), full(ws_low), full(ws_low_t), full(bsp), full(sgu_g), full(sgu_b),
                  full(ln_g), full(ln_b)],
        out_specs=[tile(D_MODEL), heads, pl.BlockSpec((HEADS, 1, tm), lambda i: (0, 0, i)), tile(4 * G_WIDTH)]
        + [a[0] for a in accs],
        out_shape=[jax.ShapeDtypeStruct((t, D_MODEL), F32), jax.ShapeDtypeStruct((HEADS, t, HEAD_PAD), BF16),
                   jax.ShapeDtypeStruct((HEADS, 1, t), F32), jax.ShapeDtypeStruct((t, 4 * G_WIDTH), BF16)]
        + [a[1] for a in accs],
        scratch_shapes=[pltpu.VMEM((CHUNK, G_WIDTH), F32)],
        compiler_params=_cparams(("arbitrary",)),
    )(x, target, proj, proj, proj, proj, ol, w_out, ws_low, ws_low_t, bsp, sgu_g, sgu_b, ln_g, ln_b)


def _attn_bwd(q, k, v, do, lse_row, d_row):
    t = q.shape[1]
    bk, bq = ATTN_BWD_WIDE, ATTN_NARROW
    n_diag = bk // bq
    last = t // bq - 1
    chunk = SOFTMAX_ROWS

    def body(q_ref, k_ref, v_ref, do_ref, lse_ref, drow_ref, dqt_ref, dk_ref, dv_ref,
             s0, s1, e0, e1, p0, p1, g0, g1, kt_scr):
        j = pl.program_id(1)
        at = lambda i: pl.ds(pl.multiple_of(i * bq, bq), bq)

        @pl.when(j == 0)
        def _():
            dqt_ref[...] = jnp.zeros_like(dqt_ref)

        kt_scr[...] = jnp.transpose(k_ref[0].astype(F32)).astype(BF16)
        dk_ref[...] = jnp.zeros_like(dk_ref)
        dv_ref[...] = jnp.zeros_like(dv_ref)

        def products(i, s_out, e_out, keys=slice(0, bk)):
            i = jnp.minimum(i, last)
            s_out[keys, :] = _dot_nt(k_ref[0, keys, :], q_ref[0, at(i), :])
            e_out[keys, :] = _dot_nt(v_ref[0, keys, :], do_ref[0, at(i), :])

        def gradients(i, p_in, g_in, keys=slice(0, bk)):
            dv_ref[0, keys, :] += _dot(p_in[keys, :], do_ref[0, at(i), :])
            dk_ref[0, keys, :] += _dot(g_in[keys, :], q_ref[0, at(i), :])
            dqt_ref[0, :NOPE + ROPE, at(i)] += _dot(kt_scr[:NOPE + ROPE, keys], g_in[keys, :])

        def elementwise(i, s_in, e_in, p_out, g_out, qry0=None, keys=slice(0, bk)):
            lse = lse_ref[0, :, at(i)]
            dsum = drow_ref[0, :, at(i)]
            for r in range(keys.start, keys.stop, chunk):
                p = jnp.exp2(s_in[r:r + chunk, :] - lse)
                if qry0 is not None:
                    key = lax.broadcasted_iota(jnp.int32, (chunk, bq), 0) + r
                    qry = lax.broadcasted_iota(jnp.int32, (chunk, bq), 1) + qry0
                    p = jnp.where(qry >= key, p, 0.0)
                p_out[r:r + chunk, :] = p.astype(BF16)
                g_out[r:r + chunk, :] = (p * (e_in[r:r + chunk, :] - dsum)).astype(BF16)

        def one_pass(i, s_in, e_in, s_out, e_out, p_prev, g_prev, p_cur, g_cur):
            products(i + 1, s_out, e_out)
            gradients(i - 1, p_prev, g_prev)
            elementwise(i, s_in, e_in, p_cur, g_cur)

        first = n_diag * j
        keys_of = lambda u: slice(0, min((u + 1) * bq, bk))
        even, odd = (s0, e0, p0, g0), (s1, e1, p1, g1)
        products(first, s0, e0, keys_of(0))
        products(first + 1, s1, e1, keys_of(1))
        elementwise(first, s0, e0, p0, g0, qry0=0, keys=keys_of(0))
        for u in range(1, n_diag):
            (s_in, e_in, p_cur, g_cur), (s_out, e_out, p_prev, g_prev) = (odd, even) if u % 2 else (even, odd)
            products(first + u + 1, s_out, e_out, keys_of(u + 1))
            gradients(first + u - 1, p_prev, g_prev, keys_of(u - 1))
            elementwise(first + u, s_in, e_in, p_cur, g_cur, qry0=u * bq, keys=keys_of(u))

        def two_passes(n, _):
            i = first + n_diag + 2 * n
            one_pass(i, s0, e0, s1, e1, p1, g1, p0, g0)
            one_pass(i + 1, s1, e1, s0, e0, p0, g0, p1, g1)
            return 0

        lax.fori_loop(0, (last - first - n_diag + 1) // 2, two_passes, 0)
        gradients(last, p1, g1)
        dk_ref[0] = dk_ref[0] * LN2

    whole = pl.BlockSpec((1, t, HEAD_PAD), lambda h, j: (h, 0, 0))
    block = pl.BlockSpec((1, bk, HEAD_PAD), lambda h, j: (h, j, 0))
    rows = pl.BlockSpec((1, 1, t), lambda h, j: (h, 0, 0), pipeline_mode=pl.Buffered(1))
    shape = jax.ShapeDtypeStruct((HEADS, t, HEAD_PAD), F32)
    tile = lambda dtype: pltpu.VMEM((bk, bq), dtype)
    return pl.pallas_call(
        body, name="attn_bwd", grid=(HEADS, t // bk),
        in_specs=[whole, block, block, whole, rows, rows],
        out_specs=[pl.BlockSpec((1, HEAD_PAD, t), lambda h, j: (h, 0, 0)), block, block],
        out_shape=[jax.ShapeDtypeStruct((HEADS, HEAD_PAD, t), F32), shape, shape],
        scratch_shapes=[tile(F32), tile(F32), tile(F32), tile(F32), tile(BF16), tile(BF16),
                        tile(BF16), tile(BF16), pltpu.VMEM((HEAD_PAD, bk), BF16)],
        compiler_params=_cparams(("arbitrary", "arbitrary"), vmem_limit=ATTN_BWD_VMEM_LIMIT),
    )(q, k, v, do, lse_row, d_row)


def _bwd_tail(dq, dk, dv, proj, pos_col, invf_row, w_heads, q_g, kv_g, x, dr, drest, wp_in):
    t = proj.shape[0]
    tm = PROJ_TILE
    n_head = 4 * LANES

    def body(dq_ref, dk_ref, dv_ref, ph_ref, pos_ref, invf_ref, wh_ref, qg_ref, kvg_ref,
             x_ref, dr_ref, drest_ref, win_ref,
             gx_ref, dwin_ref, dwh_ref, dqg_ref, dkvg_ref):
        @pl.when(pl.program_id(0) == 0)
        def _():
            dwin_ref[...] = jnp.zeros_like(dwin_ref)
            dwh_ref[...] = jnp.zeros_like(dwh_ref)
            dqg_ref[...] = jnp.zeros_like(dqg_ref)
            dkvg_ref[...] = jnp.zeros_like(dkvg_ref)

        xb = x_ref[...].astype(BF16)
        dr_b = drest_ref[...]
        dwin_ref[:, n_head:] += _dot_tn(xb, dr_b)
        gx_rest = DN_ALPHA * dr_ref[...] + _dot_nt(dr_b, win_ref[:, n_head:])

        cos, s1, s2 = _rope_tables(pos_ref[...], invf_ref[...])
        lane = lax.broadcasted_iota(jnp.int32, (tm, LANES), 1)
        c_q = ph_ref[:, :Q_LORA]
        c_kv = ph_ref[:, Q_LORA:Q_LORA + KV_LORA]
        rstd_q = lax.rsqrt(jnp.mean(c_q * c_q, axis=-1, keepdims=True) + EPS)
        rstd_kv = lax.rsqrt(jnp.mean(c_kv * c_kv, axis=-1, keepdims=True) + EPS)
        qhat = c_q * rstd_q
        kvhat = c_kv * rstd_kv
        cqn = (qhat * qg_ref[...]).astype(BF16)
        ckvn = (kvhat * kvg_ref[...]).astype(BF16)
        dcqn = jnp.zeros((tm, Q_LORA), F32)
        dckvn = jnp.zeros((tm, KV_LORA), F32)
        dkr_rot = jnp.zeros((tm, LANES), F32)
        for h in range(HEADS):
            dq_b = _rope(jnp.transpose(dq_ref[h]) * ATTN_SCALE, cos, s1, s2, -1.0).astype(BF16)
            dk_h = dk_ref[h]
            dkv_b = jnp.where(lane < NOPE, dk_h, dv_ref[h]).astype(BF16)
            dkr_rot = dkr_rot + dk_h
            dwh_ref[h, :Q_LORA, :] += _dot_tn(cqn, dq_b)
            dwh_ref[h, Q_LORA:, :] += _dot_tn(ckvn, dkv_b)
            dcqn = dcqn + _dot_nt(dq_b, wh_ref[h, :Q_LORA, :])
            dckvn = dckvn + _dot_nt(dkv_b, wh_ref[h, Q_LORA:, :])
        rot_lanes = (lane >= KR_LO) & (lane < KR_LO + ROPE)
        dkr_raw = jnp.where(rot_lanes, _rope(dkr_rot, cos, s1, s2, -1.0), 0.0)
        dqg_ref[...] += jnp.sum(dcqn * qhat, axis=0, keepdims=True)
        dkvg_ref[...] += jnp.sum(dckvn * kvhat, axis=0, keepdims=True)
        dqh = dcqn * qg_ref[...]
        dkvh = dckvn * kvg_ref[...]
        dc_q = rstd_q * (dqh - qhat * jnp.mean(dqh * qhat, axis=-1, keepdims=True))
        dc_kv = rstd_kv * (dkvh - kvhat * jnp.mean(dkvh * kvhat, axis=-1, keepdims=True))
        dh_b = jnp.concatenate([dc_q, dc_kv, dkr_raw], axis=-1).astype(BF16)
        dwin_ref[:, :n_head] += _dot_tn(xb, dh_b)
        gx_ref[...] = gx_rest + _dot_nt(dh_b, win_ref[:, :n_head])

    full = lambda a: pl.BlockSpec(a.shape, lambda i: (0,) * a.ndim)
    tile = lambda w: pl.BlockSpec((tm, w), lambda i: (i, 0))
    heads = pl.BlockSpec((HEADS, tm, HEAD_PAD), lambda i: (0, i, 0))
    acc = lambda shape: (pl.BlockSpec(shape, lambda i: (0,) * len(shape)), jax.ShapeDtypeStruct(shape, F32))
    accs = [acc(wp_in.shape), acc(w_heads.shape), acc((1, Q_LORA)), acc((1, KV_LORA))]
    return pl.pallas_call(
        body, name="bwd_tail", grid=(t // tm,),
        in_specs=[pl.BlockSpec((HEADS, HEAD_PAD, tm), lambda i: (0, 0, i)), heads, heads, tile(n_head),
                  pl.BlockSpec((tm, 1), lambda i: (i, 0)), full(invf_row), full(w_heads), full(q_g), full(kv_g),
                  tile(D_MODEL), tile(D_MODEL), tile(drest.shape[1]), full(wp_in)],
        out_specs=[tile(D_MODEL)] + [a[0] for a in accs],
        out_shape=[jax.ShapeDtypeStruct((t, D_MODEL), F32)] + [a[1] for a in accs],
        compiler_params=_cparams(("arbitrary",), vmem_limit=BWD_TAIL_VMEM_LIMIT),
    )(dq, dk, dv, proj, pos_col, invf_row, w_heads, q_g, kv_g, x, dr, drest, wp_in)


def _adam(parts, w, m, v, *, name, tile_rows):
    n, rows, cols = parts.shape

    def body(p_ref, w_ref, m_ref, v_ref, g_ref, d_ref, nm_ref, nv_ref):
        g = p_ref[0].astype(F32)
        for s in range(1, n):
            g = g + p_ref[s].astype(F32)
        m_new = ADAM_B1 * m_ref[...] + (1.0 - ADAM_B1) * g
        v_new = ADAM_B2 * v_ref[...] + (1.0 - ADAM_B2) * (g * g)
        m_hat = m_new / (1.0 - ADAM_B1 ** ADAM_STEP)
        v_hat = v_new / (1.0 - ADAM_B2 ** ADAM_STEP)
        g_ref[...] = g
        d_ref[...] = -ADAM_LR * (m_hat / (jnp.sqrt(v_hat) + ADAM_EPS) + ADAM_WD * w_ref[...])
        nm_ref[...] = m_new
        nv_ref[...] = v_new

    flat = pl.BlockSpec((tile_rows, cols), lambda i: (i, 0))
    shape = jax.ShapeDtypeStruct((rows, cols), F32)
    return pl.pallas_call(
        body, name=name, grid=(rows // tile_rows,),
        in_specs=[pl.BlockSpec((n, tile_rows, cols), lambda i: (0, i, 0)), flat, flat, flat],
        out_specs=[flat] * 4, out_shape=[shape] * 4,
        compiler_params=_cparams(("arbitrary",)),
    )(parts, w, m, v)


SMALL_NAMES = ("q_norm_g", "kv_norm_g", "sgu_norm_g", "sgu_norm_b", "b_spatial", "ln_g", "ln_b")
SMALL_SIZES = (Q_LORA, KV_LORA, G_WIDTH, G_WIDTH, HEADS * CHUNK, D_MODEL, D_MODEL)


def _pack_small(vals, last=None):
    flat = jnp.concatenate([v.reshape(-1) for v in vals])
    pad = SMALL_LEN - flat.shape[0]
    if last is None:
        return jnp.pad(flat, (0, pad))
    return jnp.concatenate([flat, jnp.zeros((pad - 1,), F32), last.reshape(1)])


def _unpack_small(flat):
    out, at = [], 0
    for n in SMALL_SIZES:
        out.append(flat[at:at + n])
        at += n
    out[4] = out[4].reshape(HEADS, CHUNK)
    return out


UQ_SHARD = HEADS * (NOPE + ROPE) // N_DEV
HEAD_ROWS = Q_LORA + KV_LORA
MIXED_ROWS = HEAD_ROWS + CHUNK + SMALL_LEN // N_DEV // LANES


def _head_slab(w_uq_shard, w_ukv_shard):
    return jnp.concatenate([jnp.pad(w_uq_shard, ((0, 0), (0, LANES - UQ_SHARD))), w_ukv_shard])


IN_SHARD = D_IN // N_DEV


def _w_in_pieces():
    split = Q_LORA + KV_LORA
    moves = ((0, split, 0), (split, split + ROPE, KR_LO), (split + ROPE, D_IN, LANES - ROPE))
    pieces = []
    for s in range(N_DEV):
        lo, hi = s * IN_SHARD, (s + 1) * IN_SHARD
        for a, b, shift in moves:
            a, b = max(a, lo), min(b, hi)
            if a < b:
                pieces.append((s, a - lo, a + shift, b - a))
    return pieces


def _padded_w_in(shards):
    tr = TOKEN_TILE

    def body(sh_ref, o_ref):
        o_ref[...] = jnp.zeros_like(o_ref)
        for s, src, dst, width in _w_in_pieces():
            o_ref[:, dst:dst + width] = sh_ref[s, :, src:src + width]

    return pl.pallas_call(
        body, name="w_in_pad", grid=(D_MODEL // tr,),
        in_specs=[pl.BlockSpec((N_DEV, tr, IN_SHARD), lambda i: (0, i, 0))],
        out_specs=pl.BlockSpec((tr, D_IN_PAD), lambda i: (i, 0)),
        out_shape=jax.ShapeDtypeStruct((D_MODEL, D_IN_PAD), shards.dtype),
        compiler_params=_cparams(("arbitrary",)),
    )(shards)


def _w_in_shards(dwp_in):
    tr = TOKEN_TILE
    by_shard = [[p for p in _w_in_pieces() if p[0] == s] for s in range(N_DEV)]

    def body(w_ref, o_ref):
        for s, pieces in enumerate(by_shard):
            parts = [w_ref[:, dst:dst + width] for _, _, dst, width in pieces]
            o_ref[s] = parts[0] if len(parts) == 1 else jnp.concatenate(parts, axis=1)

    return pl.pallas_call(
        body, name="w_in_split", grid=(D_MODEL // tr,),
        in_specs=[pl.BlockSpec((tr, D_IN_PAD), lambda i: (i, 0))],
        out_specs=pl.BlockSpec((N_DEV, tr, IN_SHARD), lambda i: (0, i, 0)),
        out_shape=jax.ShapeDtypeStruct((N_DEV, D_MODEL, IN_SHARD), dwp_in.dtype),
        compiler_params=_cparams(("arbitrary",)),
    )(dwp_in)


def kernel(x, positions, w_in, q_norm_g, w_uq, kv_norm_g, w_ukv, sgu_norm_g, sgu_norm_b, w_spatial, b_spatial, w_out, ln_g, ln_b, loss_target, m_w_in, m_q_norm_g, m_w_uq, m_kv_norm_g, m_w_ukv, m_sgu_norm_g, m_sgu_norm_b, m_w_spatial, m_b_spatial, m_w_out, m_ln_g, m_ln_b, v_w_in, v_q_norm_g, v_w_uq, v_kv_norm_g, v_w_ukv, v_sgu_norm_g, v_sgu_norm_b, v_w_spatial, v_b_spatial, v_w_out, v_ln_g, v_ln_b):
    me = 4 * lax.axis_index("x") + 2 * lax.axis_index("y") + lax.axis_index("c")
    seq = x.shape[1]
    x2 = x.reshape(seq, D_MODEL)
    tgt2 = loss_target.reshape(seq, D_MODEL)
    pos_col = positions.reshape(seq, 1)

    w_in_shards, w_out_shards, w_heads = _gather_two_level(
        [w_in.astype(BF16), w_out.astype(BF16), _head_slab(w_uq, w_ukv).astype(BF16)],
        name="wgather")
    (loss_part, grad_x, d_in, d_heads, d_out, d_ws, d_bs_t, d_lng, d_lnb, d_sgug, d_sgub, d_qg, d_kvg) = _local_step(
        x2, tgt2, pos_col, w_in_shards, w_heads, w_out_shards.reshape(D_MODEL, D_MODEL), q_norm_g, kv_norm_g,
        sgu_norm_g, sgu_norm_b, w_spatial, b_spatial, ln_g, ln_b)

    small_part = _pack_small([d_qg, d_kvg, d_sgug, d_sgub, d_bs_t[:, :HEADS].T, d_lng, d_lnb], last=loss_part[0, :1])
    mixed = jnp.concatenate([d_heads, d_ws, small_part.reshape(N_DEV, -1, LANES)], axis=1)
    by_chip = [g.reshape((N_CHIPS, 2) + g.shape[1:])
               for g in (d_in, d_out.reshape(N_DEV, D_MODEL // N_DEV, D_MODEL), mixed)]
    from_sibling = _sibling_swap(by_chip, name="gswap")
    core = lax.axis_index("c").astype(jnp.int32).reshape(1)
    pair_sums = [_pair_sum(a, b, core, name=nm, tile_rows=tr, out_dtype=dt) for a, b, nm, tr, dt in zip(
        by_chip, from_sibling, ("gsum_in", "gsum_out", "gsum_mixed"), (TOKEN_TILE, D_MODEL // N_DEV, MIXED_ROWS),
        (BF16, BF16, F32))]
    recv_in, recv_out, recv_mixed = _chip_exchange(pair_sums, name="gexch")

    take = lambda a: lax.dynamic_index_in_dim(a, me, 0, keepdims=False)
    small_w = _pack_small([q_norm_g, kv_norm_g, sgu_norm_g, sgu_norm_b, b_spatial, ln_g, ln_b])
    small_m = _pack_small([m_q_norm_g, m_kv_norm_g, m_sgu_norm_g, m_sgu_norm_b, m_b_spatial, m_ln_g, m_ln_b])
    small_v = _pack_small([v_q_norm_g, v_kv_norm_g, v_sgu_norm_g, v_sgu_norm_b, v_b_spatial, v_ln_g, v_ln_b])
    own_mixed = lambda uq, ukv, sp, small: jnp.concatenate(
        [_head_slab(uq, ukv), take(sp), take(small.reshape(N_DEV, -1, LANES))])
    res_in = _adam(recv_in, w_in, m_w_in, v_w_in, name="adam_in", tile_rows=TOKEN_TILE)
    res_out = _adam(recv_out, w_out, m_w_out, v_w_out, name="adam_out", tile_rows=D_MODEL // N_DEV)
    res_mixed = _adam(recv_mixed, own_mixed(w_uq, w_ukv, w_spatial, small_w), own_mixed(m_w_uq, m_w_ukv, m_w_spatial, small_m),
                      own_mixed(v_w_uq, v_w_ukv, v_w_spatial, small_v), name="adam_mixed", tile_rows=MIXED_ROWS)

    rep_g, = _exchange([res_mixed[0][HEAD_ROWS:]], name="sgather", per_destination=False)
    rep_pack = lambda sp, small: jnp.concatenate(
        [sp.reshape(N_DEV, CHUNK, LANES), small.reshape(N_DEV, -1, LANES)], axis=1).reshape(-1, LANES)
    _, delta_rep, m_rep, v_rep = _adam(rep_g.reshape(1, N_DEV * REP_ROWS, LANES), rep_pack(w_spatial, small_w),
                                       rep_pack(m_w_spatial, small_m), rep_pack(v_w_spatial, small_v),
                                       name="adam_rep", tile_rows=N_DEV * REP_ROWS)

    def rep_unpack(a):
        a = a.reshape(N_DEV, REP_ROWS, LANES)
        small = _unpack_small(a[:, CHUNK:].reshape(-1))
        return [small[0], small[1], small[2], small[3], a[:, :CHUNK], small[4], small[5], small[6]]

    def ordered(which, rep):
        r_qg, r_kvg, r_sg, r_sb, r_ws, r_bs, r_lg, r_lb = rep_unpack(rep)
        heads = res_mixed[which]
        return [res_in[which], r_qg, heads[:Q_LORA, :UQ_SHARD], r_kvg, heads[Q_LORA:HEAD_ROWS], r_sg, r_sb, r_ws, r_bs,
                res_out[which], r_lg, r_lb]

    loss = rep_g[N_DEV - 1, REP_ROWS - 1, LANES - 1]
    outs = [loss, grad_x.reshape(x.shape)]
    outs += ordered(0, rep_g.reshape(-1, LANES))
    outs += ordered(1, delta_rep)
    outs += ordered(2, m_rep)
    outs += ordered(3, v_rep)
    return tuple(outs)


def _local_step(x2, tgt2, pos_col, w_in_shards, w_heads, w_out_full, q_norm_g, kv_norm_g, sgu_norm_g, sgu_norm_b,
                w_spatial, b_spatial, ln_g, ln_b):
    wp_in = _padded_w_in(w_in_shards)

    half = jnp.arange(HALF, dtype=F32)
    inv_freq = 1.0 / (ROPE_THETA ** (half / HALF))
    invf_row = jnp.concatenate([jnp.zeros((KR_LO,), F32), inv_freq, inv_freq,
                                jnp.zeros((LANES - KR_LO - ROPE,), F32)]).reshape(1, LANES)
    tri = jnp.tril(jnp.ones((CHUNK, CHUNK), dtype=bool))
    ws_low = jnp.where(tri[None], w_spatial, 0.0).astype(BF16)
    ws_low_t = ws_low.transpose(0, 2, 1)
    bsp = jnp.repeat(b_spatial.T, G_HEAD_DIM, axis=1)
    row = lambda a: a.reshape(1, -1)

    proj, q, k, v, vt = _fwd_proj(x2, pos_col, invf_row, wp_in, w_heads, row(q_norm_g), row(kv_norm_g))
    o, lse_row = _attn_fwd(q, k, vt)
    (dr, do, d_row, drest, d_out, d_ws, d_bs_t, d_lng, d_lnb, d_sgug, d_sgub, loss_part) = _mid(
        x2, tgt2, proj, o, w_out_full, ws_low, ws_low_t, bsp, row(sgu_norm_g), row(sgu_norm_b), row(ln_g), row(ln_b))
    dqt, dk, dv = _attn_bwd(q, k, v, do, lse_row, d_row)
    grad_x, dwp_in, d_heads, d_qg, d_kvg = _bwd_tail(dqt, dk, dv, proj, pos_col, invf_row, w_heads, row(q_norm_g),
                                                      row(kv_norm_g), x2, dr, drest, wp_in)
    return (loss_part, grad_x, _w_in_shards(dwp_in), d_heads, d_out, d_ws, d_bs_t, d_lng, d_lnb, d_sgug, d_sgub,
            d_qg, d_kvg)
```

```python
import functools
import math

import jax
import jax.numpy as jnp
from jax import lax
from jax.experimental import pallas as pl
from jax.experimental.pallas import tpu as pltpu

F32 = jnp.float32
BF16 = jnp.bfloat16

N_DEV = 8
D_MODEL = 1024
HEADS = 8
NOPE = 64
ROPE = 32
HALF = ROPE // 2
VDIM = 64
Q_LORA = 256
KV_LORA = 128
G_WIDTH = 512
G_HEAD_DIM = 64
CHUNK = 128
HEAD_PAD = 128
D_IN = 2464
D_IN_PAD = 2560
KR_LO = NOPE
ROPE_THETA = 10000.0
DN_ALPHA = 2.0 ** 0.25
EPS = 1e-5
ATTN_SCALE = 1.0 / math.sqrt(NOPE + ROPE)
ADAM_LR, ADAM_B1, ADAM_B2, ADAM_EPS, ADAM_WD, ADAM_STEP = 0.001, 0.9, 0.999, 1e-08, 0.01, 10

LANES = 128
REP_ROWS = 136
SMALL_LEN = 8192
VMEM_LIMIT = 56 * 1024 * 1024
ATTN_BWD_VMEM_LIMIT = 61 * 1024 * 1024
BWD_TAIL_VMEM_LIMIT = 61 * 1024 * 1024

TOKEN_TILE = 256
PROJ_TILE = 512
ATTN_FWD_WIDE = 2048
ATTN_BWD_WIDE = 2048
ATTN_NARROW = 512
SOFTMAX_ROWS = 512
LOG2E = 1.4426950408889634
LN2 = 0.6931471805599453
Q_PRESCALE = ATTN_SCALE * LOG2E


def _cparams(sem=None, vmem_limit=VMEM_LIMIT):
    return pltpu.CompilerParams(dimension_semantics=sem, vmem_limit_bytes=vmem_limit)


def _dot(a, b):
    return jnp.dot(a, b, preferred_element_type=F32)


def _dot_nt(a, b):
    return lax.dot_general(a, b, (((1,), (1,)), ((), ())), preferred_element_type=F32)


def _dot_tn(a, b):
    return lax.dot_general(a, b, (((0,), (0,)), ((), ())), preferred_element_type=F32)


def _as_row(col):
    return jnp.transpose(jnp.broadcast_to(col, (col.shape[0], LANES)))[0:1, :]


def _sigmoid(z):
    return 1.0 / (1.0 + jnp.exp(-z))


def _gelu(x):
    return 0.5 * x * (1.0 + lax.erf(x * 0.7071067811865476))


def _gelu_grad(x):
    cdf = 0.5 * (1.0 + lax.erf(x * 0.7071067811865476))
    return cdf + x * jnp.exp(-0.5 * x * x) * 0.3989422804014327


def _exchange(srcs, *, name, per_destination):
    n = len(srcs)
    slab_shapes = [s.shape[1:] if per_destination else s.shape for s in srcs]

    def body(*refs):
        src_refs, out_refs = refs[:n], refs[n:2 * n]
        send_sems, recv_sems, local_sems = refs[2 * n:]
        x, y, c = lax.axis_index("x"), lax.axis_index("y"), lax.axis_index("c")
        me = 4 * x + 2 * y + c

        def slab_for(t, dest):
            return src_refs[t].at[dest] if per_destination else src_refs[t]

        mine = [pltpu.make_async_copy(slab_for(t, me), out_refs[t].at[me], local_sems.at[t]) for t in range(n)]
        for cp in mine:
            cp.start()
        sends, arrivals = [], []
        for k in (6, 7, 4, 5, 2, 3, 1):
            px = 1 - x if k & 4 else x
            py = 1 - y if k & 2 else y
            pc = 1 - c if k & 1 else c
            peer = 4 * px + 2 * py + pc
            for t in range(n):
                sem = (k - 1) * n + t
                cp = pltpu.make_async_remote_copy(
                    src_ref=slab_for(t, peer), dst_ref=out_refs[t].at[me],
                    send_sem=send_sems.at[sem], recv_sem=recv_sems.at[sem],
                    device_id=(px, py, pc), device_id_type=pl.DeviceIdType.MESH)
                cp.start()
                sends.append(cp)
                arrivals.append(pltpu.make_async_remote_copy(
                    src_ref=slab_for(t, peer), dst_ref=out_refs[t].at[peer],
                    send_sem=send_sems.at[sem], recv_sem=recv_sems.at[sem],
                    device_id=(x, y, c), device_id_type=pl.DeviceIdType.MESH))
        for cp in arrivals:
            cp.wait_recv()
        for cp in sends:
            cp.wait_send()
        for cp in mine:
            cp.wait()

    hbm = pl.BlockSpec(memory_space=pl.ANY)
    return pl.pallas_call(
        body, name=name,
        out_shape=[jax.ShapeDtypeStruct((N_DEV,) + tuple(shape), s.dtype) for shape, s in zip(slab_shapes, srcs)],
        in_specs=[hbm] * n, out_specs=[hbm] * n,
        scratch_shapes=[pltpu.SemaphoreType.DMA(((N_DEV - 1) * n,)), pltpu.SemaphoreType.DMA(((N_DEV - 1) * n,)),
                        pltpu.SemaphoreType.DMA((n,))],
    )(*srcs)


def _gather_two_level(srcs, *, name):
    n = len(srcs)

    def body(*refs):
        src_refs, out_refs = refs[:n], refs[n:2 * n]
        send_sems, recv_sems, local_sems = refs[2 * n:]
        x, y, c = lax.axis_index("x"), lax.axis_index("y"), lax.axis_index("c")
        me, sibling = (x, y, c), (x, y, 1 - c)
        chips = [(1 - x, 1 - y), (1 - x, y), (x, 1 - y)]
        index = lambda px, py, pc: 4 * px + 2 * py + pc

        def copy(k, t, block, to, src=None):
            place = out_refs[t].at[index(*block)]
            return pltpu.make_async_remote_copy(
                src_ref=place if src is None else src, dst_ref=place,
                send_sem=send_sems.at[k * n + t], recv_sem=recv_sems.at[k * n + t],
                device_id=to, device_id_type=pl.DeviceIdType.MESH)

        mine = [pltpu.make_async_copy(src_refs[t], out_refs[t].at[index(*me)], local_sems.at[t]) for t in range(n)]
        for cp in mine:
            cp.start()
        first = [copy(1 + j, t, me, (*chip, c), src=src_refs[t]) for j, chip in enumerate(chips) for t in range(n)]
        first += [copy(0, t, me, sibling, src=src_refs[t]) for t in range(n)]
        for cp in first:
            cp.start()
        passed = []
        for j, chip in enumerate(chips):
            for t in range(n):
                copy(1 + j, t, (*chip, c), me).wait_recv()
                cp = copy(4 + j, t, (*chip, c), sibling)
                cp.start()
                passed.append(cp)
        for t in range(n):
            copy(0, t, sibling, me).wait_recv()
        for j, chip in enumerate(chips):
            for t in range(n):
                copy(4 + j, t, (*chip, 1 - c), me).wait_recv()
        for cp in first + passed:
            cp.wait_send()
        for cp in mine:
            cp.wait()

    hbm = pl.BlockSpec(memory_space=pl.ANY)
    return pl.pallas_call(
        body, name=name,
        out_shape=[jax.ShapeDtypeStruct((N_DEV,) + s.shape, s.dtype) for s in srcs],
        in_specs=[hbm] * n, out_specs=[hbm] * n,
        scratch_shapes=[pltpu.SemaphoreType.DMA((7 * n,)), pltpu.SemaphoreType.DMA((7 * n,)),
                        pltpu.SemaphoreType.DMA((n,))],
    )(*srcs)


N_CHIPS = N_DEV // 2


def _sibling_swap(srcs, *, name):
    n = len(srcs)

    def body(*refs):
        src_refs, out_refs = refs[:n], refs[n:2 * n]
        send_sems, recv_sems = refs[2 * n:]
        x, y, c = lax.axis_index("x"), lax.axis_index("y"), lax.axis_index("c")
        sends = []
        for chip in range(N_CHIPS):
            for t in range(n):
                cp = pltpu.make_async_remote_copy(
                    src_ref=src_refs[t].at[chip, 1 - c], dst_ref=out_refs[t].at[chip],
                    send_sem=send_sems.at[chip * n + t], recv_sem=recv_sems.at[chip * n + t],
                    device_id=(x, y, 1 - c), device_id_type=pl.DeviceIdType.MESH)
                cp.start()
                sends.append(cp)
        for cp in sends:
            cp.wait_recv()
        for cp in sends:
            cp.wait_send()

    hbm = pl.BlockSpec(memory_space=pl.ANY)
    return pl.pallas_call(
        body, name=name,
        out_shape=[jax.ShapeDtypeStruct((N_CHIPS,) + s.shape[2:], s.dtype) for s in srcs],
        in_specs=[hbm] * n, out_specs=[hbm] * n,
        scratch_shapes=[pltpu.SemaphoreType.DMA((N_CHIPS * n,)), pltpu.SemaphoreType.DMA((N_CHIPS * n,))],
    )(*srcs)


def _pair_sum(mine, theirs, core, *, name, tile_rows, out_dtype):
    _, _, rows, cols = mine.shape

    def body(core_ref, a_ref, b_ref, o_ref):
        o_ref[...] = (a_ref[0] + b_ref[...]).astype(out_dtype)

    return pl.pallas_call(
        body, name=name,
        grid_spec=pltpu.PrefetchScalarGridSpec(
            num_scalar_prefetch=1, grid=(N_CHIPS, rows // tile_rows),
            in_specs=[pl.BlockSpec((1, 1, tile_rows, cols), lambda q, r, core_ref: (q, core_ref[0], r, 0)),
                      pl.BlockSpec((1, tile_rows, cols), lambda q, r, core_ref: (q, r, 0))],
            out_specs=pl.BlockSpec((1, tile_rows, cols), lambda q, r, core_ref: (q, r, 0))),
        out_shape=jax.ShapeDtypeStruct((N_CHIPS, rows, cols), out_dtype),
        compiler_params=_cparams(("arbitrary", "arbitrary")),
    )(core, mine, theirs)


def _chip_exchange(srcs, *, name):
    n = len(srcs)

    def body(*refs):
        src_refs, out_refs = refs[:n], refs[n:2 * n]
        send_sems, recv_sems, local_sems = refs[2 * n:]
        x, y, c = lax.axis_index("x"), lax.axis_index("y"), lax.axis_index("c")
        my_chip = 2 * x + y
        mine = [pltpu.make_async_copy(src_refs[t].at[my_chip], out_refs[t].at[my_chip], local_sems.at[t])
                for t in range(n)]
        for cp in mine:
            cp.start()
        sends, arrivals = [], []
        for k in (3, 2, 1):
            px = 1 - x if k & 2 else x
            py = 1 - y if k & 1 else y
            peer_chip = 2 * px + py
            for t in range(n):
                sem = (k - 1) * n + t
                cp = pltpu.make_async_remote_copy(
                    src_ref=src_refs[t].at[peer_chip], dst_ref=out_refs[t].at[my_chip],
                    send_sem=send_sems.at[sem], recv_sem=recv_sems.at[sem],
                    device_id=(px, py, c), device_id_type=pl.DeviceIdType.MESH)
                cp.start()
                sends.append(cp)
                arrivals.append(pltpu.make_async_remote_copy(
                    src_ref=src_refs[t].at[peer_chip], dst_ref=out_refs[t].at[peer_chip],
                    send_sem=send_sems.at[sem], recv_sem=recv_sems.at[sem],
                    device_id=(x, y, c), device_id_type=pl.DeviceIdType.MESH))
        for cp in arrivals:
            cp.wait_recv()
        for cp in sends:
            cp.wait_send()
        for cp in mine:
            cp.wait()

    hbm = pl.BlockSpec(memory_space=pl.ANY)
    return pl.pallas_call(
        body, name=name,
        out_shape=[jax.ShapeDtypeStruct(s.shape, s.dtype) for s in srcs],
        in_specs=[hbm] * n, out_specs=[hbm] * n,
        scratch_shapes=[pltpu.SemaphoreType.DMA((3 * n,)), pltpu.SemaphoreType.DMA((3 * n,)),
                        pltpu.SemaphoreType.DMA((n,))],
    )(*srcs)


def _rope_tables(pos_col, invf_row):
    ang = pos_col.astype(F32) * invf_row
    lane = lax.broadcasted_iota(jnp.int32, ang.shape, 1)
    cos, sin = jnp.cos(ang), jnp.sin(ang)
    first = (lane >= KR_LO) & (lane < KR_LO + HALF)
    second = (lane >= KR_LO + HALF) & (lane < KR_LO + ROPE)
    return cos, jnp.where(first, sin, 0.0), jnp.where(second, sin, 0.0)


def _rope(t, cos, sin_first, sin_second, sign):
    up = pltpu.roll(t, LANES - HALF, 1)
    down = pltpu.roll(t, HALF, 1)
    return t * cos - sign * (up * sin_first) + sign * (down * sin_second)


def _fwd_proj(x, pos_col, invf_row, wp_in, w_heads, q_g, kv_g):
    t = x.shape[0]
    tm = PROJ_TILE

    def body(x_ref, pos_ref, invf_ref, win_ref, wh_ref, qg_ref, kvg_ref,
             proj_ref, q_ref, k_ref, v_ref, vt_ref):
        proj = _dot(x_ref[...].astype(BF16), win_ref[...])
        proj_ref[...] = proj
        c_q = proj[:, :Q_LORA]
        c_kv = proj[:, Q_LORA:Q_LORA + KV_LORA]
        kr_raw = proj[:, Q_LORA + KV_LORA:Q_LORA + KV_LORA + LANES]
        cqn = (c_q * lax.rsqrt(jnp.mean(c_q * c_q, axis=-1, keepdims=True) + EPS) * qg_ref[...]).astype(BF16)
        ckvn = (c_kv * lax.rsqrt(jnp.mean(c_kv * c_kv, axis=-1, keepdims=True) + EPS) * kvg_ref[...]).astype(BF16)
        cos, s1, s2 = _rope_tables(pos_ref[...], invf_ref[...])
        kr = _rope(kr_raw, cos, s1, s2, 1.0)
        lane = lax.broadcasted_iota(jnp.int32, (tm, HEAD_PAD), 1)
        for h in range(HEADS):
            q_h = _dot(cqn, wh_ref[h, :Q_LORA, :])
            kv_h = _dot(ckvn, wh_ref[h, Q_LORA:, :])
            q_ref[h] = (_rope(q_h, cos, s1, s2, 1.0) * Q_PRESCALE).astype(BF16)
            k_ref[h] = jnp.where(lane < NOPE, kv_h, kr).astype(BF16)
            v_ref[h] = kv_h.astype(BF16)
            vt_ref[h] = jnp.transpose(kv_h).astype(BF16)

    full = lambda a: pl.BlockSpec(a.shape, lambda i: (0,) * a.ndim)
    head_spec = pl.BlockSpec((HEADS, tm, HEAD_PAD), lambda i: (0, i, 0))
    head_shape = jax.ShapeDtypeStruct((HEADS, t, HEAD_PAD), BF16)
    return pl.pallas_call(
        body, name="fwd_proj", grid=(t // tm,),
        in_specs=[pl.BlockSpec((tm, D_MODEL), lambda i: (i, 0)), pl.BlockSpec((tm, 1), lambda i: (i, 0)),
                  full(invf_row), full(wp_in), full(w_heads), full(q_g), full(kv_g)],
        out_specs=[pl.BlockSpec((tm, D_IN_PAD), lambda i: (i, 0)), head_spec, head_spec, head_spec,
                   pl.BlockSpec((HEADS, HEAD_PAD, tm), lambda i: (0, 0, i))],
        out_shape=[jax.ShapeDtypeStruct((t, D_IN_PAD), F32), head_shape, head_shape, head_shape,
                   jax.ShapeDtypeStruct((HEADS, HEAD_PAD, t), BF16)],
        compiler_params=_cparams(("arbitrary",)),
    )(x, pos_col, invf_row, wp_in, w_heads, q_g, kv_g)


def _attn_fwd(q, k, vt):
    t = q.shape[1]
    bq, bk = ATTN_FWD_WIDE, ATTN_NARROW
    n_diag = bq // bk
    chunk = SOFTMAX_ROWS

    def body(q_ref, k_ref, vt_ref, o_ref, lse_ref, s0, s1, p0, p1, x0, x1, m_scr, l_scr, a_scr, acc_scr):
        i = pl.program_id(1)
        at = lambda j: pl.ds(pl.multiple_of(j * bk, bk), bk)

        def exp_pass(s_in, block_max, p_out, diagonal=False, cols=slice(None)):
            width = bq if cols == slice(None) else cols.stop - cols.start

            def load(r):
                s = s_in[r:r + chunk, cols]
                if diagonal:
                    key = lax.broadcasted_iota(jnp.int32, (chunk, width), 0) + r
                    qry = lax.broadcasted_iota(jnp.int32, (chunk, width), 1)
                    s = jnp.where(qry >= key, s, -jnp.inf)
                return s

            if diagonal:
                block_max = jnp.max(load(0), axis=0, keepdims=True)
                for r in range(chunk, bk, chunk):
                    block_max = jnp.maximum(block_max, jnp.max(load(r), axis=0, keepdims=True))
            m_old = m_scr[:, cols]
            m_new = jnp.maximum(m_old, block_max)
            alpha = jnp.exp2(m_old - m_new)
            total = jnp.zeros((1, width), F32)
            for r in range(0, bk, chunk):
                p = jnp.exp2(load(r) - m_new)
                p_out[r:r + chunk, cols] = p.astype(BF16)
                total = total + jnp.sum(p, axis=0, keepdims=True)
            m_scr[:, cols] = m_new
            l_scr[:, cols] = alpha * l_scr[:, cols] + total
            return alpha

        def scores(j, s_out, x_out):
            s = _dot_nt(k_ref[0, at(j), :], q_ref[0])
            s_out[...] = s
            x_out[...] = jnp.max(s, axis=0, keepdims=True)

        def value_product(j, p_in):
            return _dot(vt_ref[0, NOPE:, at(j)], p_in[...])

        def one_pass(j, s_in, x_in, s_out, x_out, p_prev, p_cur):
            scores(j + 1, s_out, x_out)
            acc_scr[...] = a_scr[...] * acc_scr[...] + value_product(jnp.maximum(j - 1, 0), p_prev)
            a_scr[...] = exp_pass(s_in, x_in[...], p_cur)

        scores(0, s0, x0)
        p1[...] = jnp.zeros_like(p1)
        a_scr[...] = jnp.ones_like(a_scr)
        m_scr[...] = jnp.full(m_scr.shape, -jnp.inf, F32)
        l_scr[...] = jnp.zeros_like(l_scr)
        acc_scr[...] = jnp.zeros_like(acc_scr)

        def two_passes(n, _):
            one_pass(2 * n, s0, x0, s1, x1, p1, p0)
            one_pass(2 * n + 1, s1, x1, s0, x0, p0, p1)
            return 0

        lax.fori_loop(0, (n_diag // 2) * i, two_passes, 0)
        d = n_diag * i
        alpha, p_prev, cols = a_scr[...], p1, slice(0, bq)
        for u in range(n_diag + 1):
            s_in, s_next, p_cur = (s0, s1, p0) if u % 2 == 0 else (s1, s0, p1)
            if u + 1 < n_diag:
                ahead = slice((u + 1) * bk, bq)
                s_next[:, ahead] = _dot_nt(k_ref[0, at(d + u + 1), :], q_ref[0, ahead, :])
            acc_scr[:, cols] = alpha * acc_scr[:, cols] + _dot(vt_ref[0, NOPE:, at(jnp.maximum(d + u - 1, 0))],
                                                               p_prev[:, cols])
            if u < n_diag:
                cols = slice(u * bk, bq)
                alpha = exp_pass(s_in, None, p_cur, diagonal=True, cols=cols)
                p_prev = p_cur
        o = jnp.transpose(acc_scr[...] / l_scr[...])
        o_ref[0] = jnp.concatenate([jnp.zeros_like(o), o], axis=1)
        lse_ref[0] = m_scr[...] + jnp.log2(l_scr[...])

    tile = lambda dtype: pltpu.VMEM((bk, bq), dtype)
    stat = pltpu.VMEM((1, bq), F32)
    return pl.pallas_call(
        body, name="attn_fwd", grid=(HEADS, t // bq),
        in_specs=[pl.BlockSpec((1, bq, HEAD_PAD), lambda h, i: (h, i, 0)),
                  pl.BlockSpec((1, t, HEAD_PAD), lambda h, i: (h, 0, 0)),
                  pl.BlockSpec((1, HEAD_PAD, t), lambda h, i: (h, 0, 0))],
        out_specs=[pl.BlockSpec((1, bq, HEAD_PAD), lambda h, i: (h, i, 0)),
                   pl.BlockSpec((1, 1, bq), lambda h, i: (h, 0, i))],
        out_shape=[jax.ShapeDtypeStruct((HEADS, t, HEAD_PAD), F32), jax.ShapeDtypeStruct((HEADS, 1, t), F32)],
        scratch_shapes=[tile(F32), tile(F32), tile(BF16), tile(BF16), stat, stat, stat, stat, stat,
                        pltpu.VMEM((VDIM, bq), F32)],
        compiler_params=_cparams(("arbitrary", "arbitrary")),
    )(q, k, vt)


def _mid(x, target, proj, ol, w_out, ws_low, ws_low_t, bsp, sgu_g, sgu_b, ln_g, ln_b):
    t = x.shape[0]
    tm = TOKEN_TILE
    n_steps = t // tm

    def body(x_ref, tgt_ref, za_ref, u_ref, v_ref, zb_ref, ol_ref, wout_ref, ws_ref, wst_ref, bsp_ref,
             sg_ref, sb_ref, lg_ref, lb_ref,
             dr_ref, do_ref, drow_ref, drest_ref, dwout_ref, dws_ref, dbs_ref, dlg_ref, dlb_ref, dsg_ref, dsb_ref,
             loss_ref, dbsp_acc):
        step = pl.program_id(0)

        @pl.when(step == 0)
        def _():
            dwout_ref[...] = jnp.zeros_like(dwout_ref)
            dws_ref[...] = jnp.zeros_like(dws_ref)
            dbs_ref[...] = jnp.zeros_like(dbs_ref)
            dlg_ref[...] = jnp.zeros_like(dlg_ref)
            dlb_ref[...] = jnp.zeros_like(dlb_ref)
            dsg_ref[...] = jnp.zeros_like(dsg_ref)
            dsb_ref[...] = jnp.zeros_like(dsb_ref)
            loss_ref[...] = jnp.zeros_like(loss_ref)
            dbsp_acc[...] = jnp.zeros_like(dbsp_acc)

        n_chunks = tm // CHUNK
        groups = G_WIDTH // LANES

        def side_by_side(a):
            return [jnp.concatenate([a[c * CHUNK:(c + 1) * CHUNK, g * LANES:(g + 1) * LANES] for c in range(n_chunks)],
                                    axis=1) for g in range(groups)]

        def by_chunk(wide):
            return jnp.concatenate([jnp.concatenate([wide[g][:, c * LANES:(c + 1) * LANES] for g in range(groups)], axis=1)
                                    for c in range(n_chunks)], axis=0)

        def own_lanes(h):
            lane = lax.broadcasted_iota(jnp.int32, (CHUNK, n_chunks * LANES), 1)
            return (lane % LANES) // G_HEAD_DIM == h % 2

        def spatial(w_ref, wide):
            return [sum(jnp.where(own_lanes(h), _dot(w_ref[h], wide[g]), 0.0) for h in (2 * g, 2 * g + 1))
                    for g in range(groups)]

        attn = jnp.concatenate([ol_ref[h][:, NOPE:] for h in range(HEADS)], axis=-1)
        za = za_ref[...]
        sig_a = _sigmoid(za)
        silu_a = za * sig_a
        out_a = attn * silu_a
        u = u_ref[...]
        ug = _gelu(u)
        vpre = v_ref[...]
        gv = _gelu(vpre)
        mu_v = jnp.mean(gv, axis=-1, keepdims=True)
        cen_v = gv - mu_v
        rstd_v = lax.rsqrt(jnp.mean(cen_v * cen_v, axis=-1, keepdims=True) + EPS)
        vhat = cen_v * rstd_v
        vg = vhat * sg_ref[...] + sb_ref[...]
        vg_b = vg.astype(BF16)
        sv = by_chunk(spatial(ws_ref, side_by_side(vg_b))) + jnp.tile(bsp_ref[...], (n_chunks, 1))
        sgu = ug * sv
        zb = zb_ref[...]
        sig_b = _sigmoid(zb)
        silu_b = zb * sig_b
        out_b = sgu * silu_b
        merged = jnp.concatenate([out_a, out_b], axis=-1).astype(BF16)
        r = DN_ALPHA * x_ref[...] + _dot(merged, wout_ref[...])
        mu = jnp.mean(r, axis=-1, keepdims=True)
        cen = r - mu
        rstd = lax.rsqrt(jnp.mean(cen * cen, axis=-1, keepdims=True) + EPS)
        xhat = cen * rstd
        hout = xhat * lg_ref[...] + lb_ref[...]
        err = hout - tgt_ref[...]
        row_loss = jnp.mean(err * err, axis=-1, keepdims=True)
        loss_ref[...] += jnp.broadcast_to(0.5 * jnp.sum(row_loss, axis=0, keepdims=True), loss_ref.shape)

        dh = err * (1.0 / D_MODEL)
        dlg_ref[...] += jnp.sum(dh * xhat, axis=0, keepdims=True)
        dlb_ref[...] += jnp.sum(dh, axis=0, keepdims=True)
        dxhat = dh * lg_ref[...]
        dr = rstd * (dxhat - jnp.mean(dxhat, axis=-1, keepdims=True)
                     - xhat * jnp.mean(dxhat * xhat, axis=-1, keepdims=True))
        dr_ref[...] = dr
        dr_b = dr.astype(BF16)
        dwout_ref[...] += _dot_tn(merged, dr_b)
        dmerged = _dot_nt(dr_b, wout_ref[...])
        d_out_a = dmerged[:, :G_WIDTH]
        d_out_b = dmerged[:, G_WIDTH:]
        dattn = d_out_a * silu_a
        for h in range(HEADS):
            do_h = dattn[:, h * VDIM:(h + 1) * VDIM]
            do_ref[h] = jnp.concatenate([jnp.zeros((tm, NOPE), F32), do_h], axis=-1).astype(BF16)
        feature = lax.broadcasted_iota(jnp.int32, (G_WIDTH, LANES), 0) // VDIM
        column = lax.broadcasted_iota(jnp.int32, (G_WIDTH, LANES), 1)
        head_sums = jnp.dot(dattn * attn, jnp.where(feature == column, 1.0, 0.0).astype(F32),
                            preferred_element_type=F32, precision=lax.Precision.HIGHEST)
        dsums_t = jnp.transpose(head_sums)
        for h in range(HEADS):
            drow_ref[h] = dsums_t[h:h + 1, :]
        dza = d_out_a * attn * (sig_a * (1.0 + za * (1.0 - sig_a)))
        dsgu = d_out_b * silu_b
        dzb = d_out_b * sgu * (sig_b * (1.0 + zb * (1.0 - sig_b)))
        du = dsgu * sv * _gelu_grad(u)
        dsv = dsgu * ug
        dsv_b = dsv.astype(BF16)
        for cix in range(n_chunks):
            dbsp_acc[...] += dsv[cix * CHUNK:(cix + 1) * CHUNK, :]
        dsv_wide, vg_wide = side_by_side(dsv_b), side_by_side(vg_b)
        dvg = by_chunk(spatial(wst_ref, dsv_wide))
        for h in range(HEADS):
            mine = jnp.where(own_lanes(h), dsv_wide[h // 2], jnp.zeros_like(dsv_wide[h // 2]))
            dws_ref[h] += _dot_nt(mine, vg_wide[h // 2])
        dsg_ref[...] += jnp.sum(dvg * vhat, axis=0, keepdims=True)
        dsb_ref[...] += jnp.sum(dvg, axis=0, keepdims=True)
        dvhat = dvg * sg_ref[...]
        dgv = rstd_v * (dvhat - jnp.mean(dvhat, axis=-1, keepdims=True)
                        - vhat * jnp.mean(dvhat * vhat, axis=-1, keepdims=True))
        dv = dgv * _gelu_grad(vpre)
        drest_ref[...] = jnp.concatenate([dza, du, dv, dzb], axis=-1).astype(BF16)

        @pl.when(step == n_steps - 1)
        def _():
            tri = (lax.broadcasted_iota(jnp.int32, (CHUNK, CHUNK), 0)
                   >= lax.broadcasted_iota(jnp.int32, (CHUNK, CHUNK), 1))
            for h in range(HEADS):
                dws_ref[h] = jnp.where(tri, dws_ref[h], 0.0)
            tot = dbsp_acc[...]
            lane = lax.broadcasted_iota(jnp.int32, (CHUNK, LANES), 1)
            dbs = jnp.zeros((CHUNK, LANES), F32)
            for h in range(HEADS):
                head_sum = jnp.sum(tot[:, h * G_HEAD_DIM:(h + 1) * G_HEAD_DIM], axis=-1, keepdims=True)
                dbs = jnp.where(lane == h, head_sum, dbs)
            dbs_ref[...] = dbs

    full = lambda a: pl.BlockSpec(a.shape, lambda i: (0,) * a.ndim)
    tile = lambda w, j=0: pl.BlockSpec((tm, w), lambda i, j=j: (i, j))
    heads = pl.BlockSpec((HEADS, tm, HEAD_PAD), lambda i: (0, i, 0))
    acc = lambda shape: (pl.BlockSpec(shape, lambda i: (0,) * len(shape)), jax.ShapeDtypeStruct(shape, F32))
    accs = [acc((D_MODEL, D_MODEL)), acc((HEADS, CHUNK, CHUNK)), acc((CHUNK, LANES)), acc((1, D_MODEL)),
            acc((1, D_MODEL)), acc((1, G_WIDTH)), acc((1, G_WIDTH)), acc((1, LANES))]
    return pl.pallas_call(
        body, name="mid", grid=(n_steps,),
        in_specs=[tile(D_MODEL), tile(D_MODEL), tile(G_WIDTH, 1), tile(G_WIDTH, 2), tile(G_WIDTH, 3), tile(G_WIDTH, 4),
                  heads, full(w_out), full(ws_low), full(ws_low_t), full(bsp), full(sgu_g), full(sgu_b),
                  full(ln_g), full(ln_b)],
        out_specs=[tile(D_MODEL), heads, pl.BlockSpec((HEADS, 1, tm), lambda i: (0, 0, i)), tile(4 * G_WIDTH)]
        + [a[0] for a in accs],
        out_shape=[jax.ShapeDtypeStruct((t, D_MODEL), F32), jax.ShapeDtypeStruct((HEADS, t, HEAD_PAD), BF16),
                   jax.ShapeDtypeStruct((HEADS, 1, t), F32), jax.ShapeDtypeStruct((t, 4 * G_WIDTH), BF16)]
        + [a[1] for a in accs],
        scratch_shapes=[pltpu.VMEM((CHUNK, G_WIDTH), F32)],
        compiler_params=_cparams(("arbitrary",)),
    )(x, target, proj, proj, proj, proj, ol, w_out, ws_low, ws_low_t, bsp, sgu_g, sgu_b, ln_g, ln_b)


def _attn_bwd(q, k, v, do, lse_row, d_row):
    t = q.shape[1]
    bk, bq = ATTN_BWD_WIDE, ATTN_NARROW
    n_diag = bk // bq
    last = t // bq - 1
    chunk = SOFTMAX_ROWS

    def body(q_ref, k_ref, v_ref, do_ref, lse_ref, drow_ref, dqt_ref, dk_ref, dv_ref,
             s0, s1, e0, e1, p0, p1, g0, g1, kt_scr):
        j = pl.program_id(1)
        at = lambda i: pl.ds(pl.multiple_of(i * bq, bq), bq)

        @pl.when(j == 0)
        def _():
            dqt_ref[...] = jnp.zeros_like(dqt_ref)

        kt_scr[...] = jnp.transpose(k_ref[0].astype(F32)).astype(BF16)
        dk_ref[...] = jnp.zeros_like(dk_ref)
        dv_ref[...] = jnp.zeros_like(dv_ref)

        def products(i, s_out, e_out, keys=slice(0, bk)):
            i = jnp.minimum(i, last)
            s_out[keys, :] = _dot_nt(k_ref[0, keys, :], q_ref[0, at(i), :])
            e_out[keys, :] = _dot_nt(v_ref[0, keys, :], do_ref[0, at(i), :])

        def gradients(i, p_in, g_in, keys=slice(0, bk)):
            dv_ref[0, keys, :] += _dot(p_in[keys, :], do_ref[0, at(i), :])
            dk_ref[0, keys, :] += _dot(g_in[keys, :], q_ref[0, at(i), :])
            dqt_ref[0, :, at(i)] += _dot(kt_scr[:, keys], g_in[keys, :])

        def elementwise(i, s_in, e_in, p_out, g_out, qry0=None, keys=slice(0, bk)):
            lse = lse_ref[0, :, at(i)]
            dsum = drow_ref[0, :, at(i)]
            for r in range(keys.start, keys.stop, chunk):
                p = jnp.exp2(s_in[r:r + chunk, :] - lse)
                if qry0 is not None:
                    key = lax.broadcasted_iota(jnp.int32, (chunk, bq), 0) + r
                    qry = lax.broadcasted_iota(jnp.int32, (chunk, bq), 1) + qry0
                    p = jnp.where(qry >= key, p, 0.0)
                p_out[r:r + chunk, :] = p.astype(BF16)
                g_out[r:r + chunk, :] = (p * (e_in[r:r + chunk, :] - dsum)).astype(BF16)

        def one_pass(i, s_in, e_in, s_out, e_out, p_prev, g_prev, p_cur, g_cur):
            products(i + 1, s_out, e_out)
            gradients(i - 1, p_prev, g_prev)
            elementwise(i, s_in, e_in, p_cur, g_cur)

        first = n_diag * j
        keys_of = lambda u: slice(0, min((u + 1) * bq, bk))
        even, odd = (s0, e0, p0, g0), (s1, e1, p1, g1)
        products(first, s0, e0, keys_of(0))
        products(first + 1, s1, e1, keys_of(1))
        elementwise(first, s0, e0, p0, g0, qry0=0, keys=keys_of(0))
        for u in range(1, n_diag):
            (s_in, e_in, p_cur, g_cur), (s_out, e_out, p_prev, g_prev) = (odd, even) if u % 2 else (even, odd)
            products(first + u + 1, s_out, e_out, keys_of(u + 1))
            gradients(first + u - 1, p_prev, g_prev, keys_of(u - 1))
            elementwise(first + u, s_in, e_in, p_cur, g_cur, qry0=u * bq, keys=keys_of(u))

        def two_passes(n, _):
            i = first + n_diag + 2 * n
            one_pass(i, s0, e0, s1, e1, p1, g1, p0, g0)
            one_pass(i + 1, s1, e1, s0, e0, p0, g0, p1, g1)
            return 0

        lax.fori_loop(0, (last - first - n_diag + 1) // 2, two_passes, 0)
        gradients(last, p1, g1)
        dk_ref[0] = dk_ref[0] * LN2

    whole = pl.BlockSpec((1, t, HEAD_PAD), lambda h, j: (h, 0, 0))
    block = pl.BlockSpec((1, bk, HEAD_PAD), lambda h, j: (h, j, 0))
    rows = pl.BlockSpec((1, 1, t), lambda h, j: (h, 0, 0), pipeline_mode=pl.Buffered(1))
    shape = jax.ShapeDtypeStruct((HEADS, t, HEAD_PAD), F32)
    tile = lambda dtype: pltpu.VMEM((bk, bq), dtype)
    return pl.pallas_call(
        body, name="attn_bwd", grid=(HEADS, t // bk),
        in_specs=[whole, block, block, whole, rows, rows],
        out_specs=[pl.BlockSpec((1, HEAD_PAD, t), lambda h, j: (h, 0, 0)), block, block],
        out_shape=[jax.ShapeDtypeStruct((HEADS, HEAD_PAD, t), F32), shape, shape],
        scratch_shapes=[tile(F32), tile(F32), tile(F32), tile(F32), tile(BF16), tile(BF16),
                        tile(BF16), tile(BF16), pltpu.VMEM((HEAD_PAD, bk), BF16)],
        compiler_params=_cparams(("arbitrary", "arbitrary"), vmem_limit=ATTN_BWD_VMEM_LIMIT),
    )(q, k, v, do, lse_row, d_row)


def _bwd_tail(dq, dk, dv, proj, pos_col, invf_row, w_heads, q_g, kv_g, x, dr, drest, wp_in):
    t = proj.shape[0]
    tm = PROJ_TILE
    n_head = 4 * LANES

    def body(dq_ref, dk_ref, dv_ref, ph_ref, pos_ref, invf_ref, wh_ref, qg_ref, kvg_ref,
             x_ref, dr_ref, drest_ref, win_ref,
             gx_ref, dwin_ref, dwh_ref, dqg_ref, dkvg_ref):
        @pl.when(pl.program_id(0) == 0)
        def _():
            dwin_ref[...] = jnp.zeros_like(dwin_ref)
            dwh_ref[...] = jnp.zeros_like(dwh_ref)
            dqg_ref[...] = jnp.zeros_like(dqg_ref)
            dkvg_ref[...] = jnp.zeros_like(dkvg_ref)

        xb = x_ref[...].astype(BF16)
        dr_b = drest_ref[...]
        dwin_ref[:, n_head:] += _dot_tn(xb, dr_b)
        gx_rest = DN_ALPHA * dr_ref[...] + _dot_nt(dr_b, win_ref[:, n_head:])

        cos, s1, s2 = _rope_tables(pos_ref[...], invf_ref[...])
        lane = lax.broadcasted_iota(jnp.int32, (tm, LANES), 1)
        c_q = ph_ref[:, :Q_LORA]
        c_kv = ph_ref[:, Q_LORA:Q_LORA + KV_LORA]
        rstd_q = lax.rsqrt(jnp.mean(c_q * c_q, axis=-1, keepdims=True) + EPS)
        rstd_kv = lax.rsqrt(jnp.mean(c_kv * c_kv, axis=-1, keepdims=True) + EPS)
        qhat = c_q * rstd_q
        kvhat = c_kv * rstd_kv
        cqn = (qhat * qg_ref[...]).astype(BF16)
        ckvn = (kvhat * kvg_ref[...]).astype(BF16)
        dcqn = jnp.zeros((tm, Q_LORA), F32)
        dckvn = jnp.zeros((tm, KV_LORA), F32)
        dkr_rot = jnp.zeros((tm, LANES), F32)
        for h in range(HEADS):
            dq_b = _rope(jnp.transpose(dq_ref[h]) * ATTN_SCALE, cos, s1, s2, -1.0).astype(BF16)
            dk_h = dk_ref[h]
            dkv_b = jnp.where(lane < NOPE, dk_h, dv_ref[h]).astype(BF16)
            dkr_rot = dkr_rot + dk_h
            dwh_ref[h, :Q_LORA, :] += _dot_tn(cqn, dq_b)
            dwh_ref[h, Q_LORA:, :] += _dot_tn(ckvn, dkv_b)
            dcqn = dcqn + _dot_nt(dq_b, wh_ref[h, :Q_LORA, :])
            dckvn = dckvn + _dot_nt(dkv_b, wh_ref[h, Q_LORA:, :])
        rot_lanes = (lane >= KR_LO) & (lane < KR_LO + ROPE)
        dkr_raw = jnp.where(rot_lanes, _rope(dkr_rot, cos, s1, s2, -1.0), 0.0)
        dqg_ref[...] += jnp.sum(dcqn * qhat, axis=0, keepdims=True)
        dkvg_ref[...] += jnp.sum(dckvn * kvhat, axis=0, keepdims=True)
        dqh = dcqn * qg_ref[...]
        dkvh = dckvn * kvg_ref[...]
        dc_q = rstd_q * (dqh - qhat * jnp.mean(dqh * qhat, axis=-1, keepdims=True))
        dc_kv = rstd_kv * (dkvh - kvhat * jnp.mean(dkvh * kvhat, axis=-1, keepdims=True))
        dh_b = jnp.concatenate([dc_q, dc_kv, dkr_raw], axis=-1).astype(BF16)
        dwin_ref[:, :n_head] += _dot_tn(xb, dh_b)
        gx_ref[...] = gx_rest + _dot_nt(dh_b, win_ref[:, :n_head])

    full = lambda a: pl.BlockSpec(a.shape, lambda i: (0,) * a.ndim)
    tile = lambda w: pl.BlockSpec((tm, w), lambda i: (i, 0))
    heads = pl.BlockSpec((HEADS, tm, HEAD_PAD), lambda i: (0, i, 0))
    acc = lambda shape: (pl.BlockSpec(shape, lambda i: (0,) * len(shape)), jax.ShapeDtypeStruct(shape, F32))
    accs = [acc(wp_in.shape), acc(w_heads.shape), acc((1, Q_LORA)), acc((1, KV_LORA))]
    return pl.pallas_call(
        body, name="bwd_tail", grid=(t // tm,),
        in_specs=[pl.BlockSpec((HEADS, HEAD_PAD, tm), lambda i: (0, 0, i)), heads, heads, tile(n_head),
                  pl.BlockSpec((tm, 1), lambda i: (i, 0)), full(invf_row), full(w_heads), full(q_g), full(kv_g),
                  tile(D_MODEL), tile(D_MODEL), tile(drest.shape[1]), full(wp_in)],
        out_specs=[tile(D_MODEL)] + [a[0] for a in accs],
        out_shape=[jax.ShapeDtypeStruct((t, D_MODEL), F32)] + [a[1] for a in accs],
        compiler_params=_cparams(("arbitrary",), vmem_limit=BWD_TAIL_VMEM_LIMIT),
    )(dq, dk, dv, proj, pos_col, invf_row, w_heads, q_g, kv_g, x, dr, drest, wp_in)


def _adam(parts, w, m, v, *, name, tile_rows):
    n, rows, cols = parts.shape

    def body(p_ref, w_ref, m_ref, v_ref, g_ref, d_ref, nm_ref, nv_ref):
        g = p_ref[0].astype(F32)
        for s in range(1, n):
            g = g + p_ref[s].astype(F32)
        m_new = ADAM_B1 * m_ref[...] + (1.0 - ADAM_B1) * g
        v_new = ADAM_B2 * v_ref[...] + (1.0 - ADAM_B2) * (g * g)
        m_hat = m_new / (1.0 - ADAM_B1 ** ADAM_STEP)
        v_hat = v_new / (1.0 - ADAM_B2 ** ADAM_STEP)
        g_ref[...] = g
        d_ref[...] = -ADAM_LR * (m_hat / (jnp.sqrt(v_hat) + ADAM_EPS) + ADAM_WD * w_ref[...])
        nm_ref[...] = m_new
        nv_ref[...] = v_new

    flat = pl.BlockSpec((tile_rows, cols), lambda i: (i, 0))
    shape = jax.ShapeDtypeStruct((rows, cols), F32)
    return pl.pallas_call(
        body, name=name, grid=(rows // tile_rows,),
        in_specs=[pl.BlockSpec((n, tile_rows, cols), lambda i: (0, i, 0)), flat, flat, flat],
        out_specs=[flat] * 4, out_shape=[shape] * 4,
        compiler_params=_cparams(("arbitrary",)),
    )(parts, w, m, v)


SMALL_NAMES = ("q_norm_g", "kv_norm_g", "sgu_norm_g", "sgu_norm_b", "b_spatial", "ln_g", "ln_b")
SMALL_SIZES = (Q_LORA, KV_LORA, G_WIDTH, G_WIDTH, HEADS * CHUNK, D_MODEL, D_MODEL)


def _pack_small(vals, last=None):
    flat = jnp.concatenate([v.reshape(-1) for v in vals])
    pad = SMALL_LEN - flat.shape[0]
    if last is None:
        return jnp.pad(flat, (0, pad))
    return jnp.concatenate([flat, jnp.zeros((pad - 1,), F32), last.reshape(1)])


def _unpack_small(flat):
    out, at = [], 0
    for n in SMALL_SIZES:
        out.append(flat[at:at + n])
        at += n
    out[4] = out[4].reshape(HEADS, CHUNK)
    return out


UQ_SHARD = HEADS * (NOPE + ROPE) // N_DEV
HEAD_ROWS = Q_LORA + KV_LORA
MIXED_ROWS = HEAD_ROWS + CHUNK + SMALL_LEN // N_DEV // LANES


def _head_slab(w_uq_shard, w_ukv_shard):
    return jnp.concatenate([jnp.pad(w_uq_shard, ((0, 0), (0, LANES - UQ_SHARD))), w_ukv_shard])


IN_SHARD = D_IN // N_DEV


def _w_in_pieces():
    split = Q_LORA + KV_LORA
    moves = ((0, split, 0), (split, split + ROPE, KR_LO), (split + ROPE, D_IN, LANES - ROPE))
    pieces = []
    for s in range(N_DEV):
        lo, hi = s * IN_SHARD, (s + 1) * IN_SHARD
        for a, b, shift in moves:
            a, b = max(a, lo), min(b, hi)
            if a < b:
                pieces.append((s, a - lo, a + shift, b - a))
    return pieces


def _padded_w_in(shards):
    tr = TOKEN_TILE

    def body(sh_ref, o_ref):
        o_ref[...] = jnp.zeros_like(o_ref)
        for s, src, dst, width in _w_in_pieces():
            o_ref[:, dst:dst + width] = sh_ref[s, :, src:src + width]

    return pl.pallas_call(
        body, name="w_in_pad", grid=(D_MODEL // tr,),
        in_specs=[pl.BlockSpec((N_DEV, tr, IN_SHARD), lambda i: (0, i, 0))],
        out_specs=pl.BlockSpec((tr, D_IN_PAD), lambda i: (i, 0)),
        out_shape=jax.ShapeDtypeStruct((D_MODEL, D_IN_PAD), shards.dtype),
        compiler_params=_cparams(("arbitrary",)),
    )(shards)


def _w_in_shards(dwp_in):
    tr = TOKEN_TILE
    by_shard = [[p for p in _w_in_pieces() if p[0] == s] for s in range(N_DEV)]

    def body(w_ref, o_ref):
        for s, pieces in enumerate(by_shard):
            parts = [w_ref[:, dst:dst + width] for _, _, dst, width in pieces]
            o_ref[s] = parts[0] if len(parts) == 1 else jnp.concatenate(parts, axis=1)

    return pl.pallas_call(
        body, name="w_in_split", grid=(D_MODEL // tr,),
        in_specs=[pl.BlockSpec((tr, D_IN_PAD), lambda i: (i, 0))],
        out_specs=pl.BlockSpec((N_DEV, tr, IN_SHARD), lambda i: (0, i, 0)),
        out_shape=jax.ShapeDtypeStruct((N_DEV, D_MODEL, IN_SHARD), dwp_in.dtype),
        compiler_params=_cparams(("arbitrary",)),
    )(dwp_in)


def kernel(x, positions, w_in, q_norm_g, w_uq, kv_norm_g, w_ukv, sgu_norm_g, sgu_norm_b, w_spatial, b_spatial, w_out, ln_g, ln_b, loss_target, m_w_in, m_q_norm_g, m_w_uq, m_kv_norm_g, m_w_ukv, m_sgu_norm_g, m_sgu_norm_b, m_w_spatial, m_b_spatial, m_w_out, m_ln_g, m_ln_b, v_w_in, v_q_norm_g, v_w_uq, v_kv_norm_g, v_w_ukv, v_sgu_norm_g, v_sgu_norm_b, v_w_spatial, v_b_spatial, v_w_out, v_ln_g, v_ln_b):
    me = 4 * lax.axis_index("x") + 2 * lax.axis_index("y") + lax.axis_index("c")
    seq = x.shape[1]
    x2 = x.reshape(seq, D_MODEL)
    tgt2 = loss_target.reshape(seq, D_MODEL)
    pos_col = positions.reshape(seq, 1)

    w_in_shards, w_out_shards, w_heads = _gather_two_level(
        [w_in.astype(BF16), w_out.astype(BF16), _head_slab(w_uq, w_ukv).astype(BF16)],
        name="wgather")
    (loss_part, grad_x, d_in, d_heads, d_out, d_ws, d_bs_t, d_lng, d_lnb, d_sgug, d_sgub, d_qg, d_kvg) = _local_step(
        x2, tgt2, pos_col, w_in_shards, w_heads, w_out_shards.reshape(D_MODEL, D_MODEL), q_norm_g, kv_norm_g,
        sgu_norm_g, sgu_norm_b, w_spatial, b_spatial, ln_g, ln_b)

    small_part = _pack_small([d_qg, d_kvg, d_sgug, d_sgub, d_bs_t[:, :HEADS].T, d_lng, d_lnb], last=loss_part[0, :1])
    mixed = jnp.concatenate([d_heads, d_ws, small_part.reshape(N_DEV, -1, LANES)], axis=1)
    by_chip = [g.reshape((N_CHIPS, 2) + g.shape[1:])
               for g in (d_in, d_out.reshape(N_DEV, D_MODEL // N_DEV, D_MODEL), mixed)]
    from_sibling = _sibling_swap(by_chip, name="gswap")
    core = lax.axis_index("c").astype(jnp.int32).reshape(1)
    pair_sums = [_pair_sum(a, b, core, name=nm, tile_rows=tr, out_dtype=dt) for a, b, nm, tr, dt in zip(
        by_chip, from_sibling, ("gsum_in", "gsum_out", "gsum_mixed"), (TOKEN_TILE, D_MODEL // N_DEV, MIXED_ROWS),
        (BF16, BF16, F32))]
    recv_in, recv_out, recv_mixed = _chip_exchange(pair_sums, name="gexch")

    take = lambda a: lax.dynamic_index_in_dim(a, me, 0, keepdims=False)
    small_w = _pack_small([q_norm_g, kv_norm_g, sgu_norm_g, sgu_norm_b, b_spatial, ln_g, ln_b])
    small_m = _pack_small([m_q_norm_g, m_kv_norm_g, m_sgu_norm_g, m_sgu_norm_b, m_b_spatial, m_ln_g, m_ln_b])
    small_v = _pack_small([v_q_norm_g, v_kv_norm_g, v_sgu_norm_g, v_sgu_norm_b, v_b_spatial, v_ln_g, v_ln_b])
    own_mixed = lambda uq, ukv, sp, small: jnp.concatenate(
        [_head_slab(uq, ukv), take(sp), take(small.reshape(N_DEV, -1, LANES))])
    res_in = _adam(recv_in, w_in, m_w_in, v_w_in, name="adam_in", tile_rows=TOKEN_TILE)
    res_out = _adam(recv_out, w_out, m_w_out, v_w_out, name="adam_out", tile_rows=D_MODEL // N_DEV)
    res_mixed = _adam(recv_mixed, own_mixed(w_uq, w_ukv, w_spatial, small_w), own_mixed(m_w_uq, m_w_ukv, m_w_spatial, small_m),
                      own_mixed(v_w_uq, v_w_ukv, v_w_spatial, small_v), name="adam_mixed", tile_rows=MIXED_ROWS)

    rep_g, = _exchange([res_mixed[0][HEAD_ROWS:]], name="sgather", per_destination=False)
    rep_pack = lambda sp, small: jnp.concatenate(
        [sp.reshape(N_DEV, CHUNK, LANES), small.reshape(N_DEV, -1, LANES)], axis=1).reshape(-1, LANES)
    _, delta_rep, m_rep, v_rep = _adam(rep_g.reshape(1, N_DEV * REP_ROWS, LANES), rep_pack(w_spatial, small_w),
                                       rep_pack(m_w_spatial, small_m), rep_pack(v_w_spatial, small_v),
                                       name="adam_rep", tile_rows=N_DEV * REP_ROWS)

    def rep_unpack(a):
        a = a.reshape(N_DEV, REP_ROWS, LANES)
        small = _unpack_small(a[:, CHUNK:].reshape(-1))
        return [small[0], small[1], small[2], small[3], a[:, :CHUNK], small[4], small[5], small[6]]

    def ordered(which, rep):
        r_qg, r_kvg, r_sg, r_sb, r_ws, r_bs, r_lg, r_lb = rep_unpack(rep)
        heads = res_mixed[which]
        return [res_in[which], r_qg, heads[:Q_LORA, :UQ_SHARD], r_kvg, heads[Q_LORA:HEAD_ROWS], r_sg, r_sb, r_ws, r_bs,
                res_out[which], r_lg, r_lb]

    loss = rep_g[N_DEV - 1, REP_ROWS - 1, LANES - 1]
    outs = [loss, grad_x.reshape(x.shape)]
    outs += ordered(0, rep_g.reshape(-1, LANES))
    outs += ordered(1, delta_rep)
    outs += ordered(2, m_rep)
    outs += ordered(3, v_rep)
    return tuple(outs)


def _local_step(x2, tgt2, pos_col, w_in_shards, w_heads, w_out_full, q_norm_g, kv_norm_g, sgu_norm_g, sgu_norm_b,
                w_spatial, b_spatial, ln_g, ln_b):
    wp_in = _padded_w_in(w_in_shards)

    half = jnp.arange(HALF, dtype=F32)
    inv_freq = 1.0 / (ROPE_THETA ** (half / HALF))
    invf_row = jnp.concatenate([jnp.zeros((KR_LO,), F32), inv_freq, inv_freq,
                                jnp.zeros((LANES - KR_LO - ROPE,), F32)]).reshape(1, LANES)
    tri = jnp.tril(jnp.ones((CHUNK, CHUNK), dtype=bool))
    ws_low = jnp.where(tri[None], w_spatial, 0.0).astype(BF16)
    ws_low_t = ws_low.transpose(0, 2, 1)
    bsp = jnp.repeat(b_spatial.T, G_HEAD_DIM, axis=1)
    row = lambda a: a.reshape(1, -1)

    proj, q, k, v, vt = _fwd_proj(x2, pos_col, invf_row, wp_in, w_heads, row(q_norm_g), row(kv_norm_g))
    o, lse_row = _attn_fwd(q, k, vt)
    (dr, do, d_row, drest, d_out, d_ws, d_bs_t, d_lng, d_lnb, d_sgug, d_sgub, loss_part) = _mid(
        x2, tgt2, proj, o, w_out_full, ws_low, ws_low_t, bsp, row(sgu_norm_g), row(sgu_norm_b), row(ln_g), row(ln_b))
    dqt, dk, dv = _attn_bwd(q, k, v, do, lse_row, d_row)
    grad_x, dwp_in, d_heads, d_qg, d_kvg = _bwd_tail(dqt, dk, dv, proj, pos_col, invf_row, w_heads, row(q_norm_g),
                                                      row(kv_norm_g), x2, dr, drest, wp_in)
    return (loss_part, grad_x, _w_in_shards(dwp_in), d_heads, d_out, d_ws, d_bs_t, d_lng, d_lnb, d_sgug, d_sgub,
            d_qg, d_kvg)
```

```python
import functools
import math

import jax
import jax.numpy as jnp
from jax import lax
from jax.experimental import pallas as pl
from jax.experimental.pallas import tpu as pltpu

F32 = jnp.float32
BF16 = jnp.bfloat16

N_DEV = 8
D_MODEL = 1024
HEADS = 8
NOPE = 64
ROPE = 32
HALF = ROPE // 2
VDIM = 64
Q_LORA = 256
KV_LORA = 128
G_WIDTH = 512
G_HEAD_DIM = 64
CHUNK = 128
HEAD_PAD = 128
D_IN = 2464
D_IN_PAD = 2560
KR_LO = NOPE
ROPE_THETA = 10000.0
DN_ALPHA = 2.0 ** 0.25
EPS = 1e-5
ATTN_SCALE = 1.0 / math.sqrt(NOPE + ROPE)
ADAM_LR, ADAM_B1, ADAM_B2, ADAM_EPS, ADAM_WD, ADAM_STEP = 0.001, 0.9, 0.999, 1e-08, 0.01, 10

LANES = 128
REP_ROWS = 136
SMALL_LEN = 8192
VMEM_LIMIT = 56 * 1024 * 1024
ATTN_BWD_VMEM_LIMIT = 61 * 1024 * 1024
BWD_TAIL_VMEM_LIMIT = 61 * 1024 * 1024

TOKEN_TILE = 256
PROJ_TILE = 512
ATTN_FWD_WIDE = 2048
ATTN_BWD_WIDE = 2048
ATTN_NARROW = 512
SOFTMAX_ROWS = 512
LOG2E = 1.4426950408889634
LN2 = 0.6931471805599453
Q_PRESCALE = ATTN_SCALE * LOG2E


def _cparams(sem=None, vmem_limit=VMEM_LIMIT):
    return pltpu.CompilerParams(dimension_semantics=sem, vmem_limit_bytes=vmem_limit)


def _dot(a, b):
    return jnp.dot(a, b, preferred_element_type=F32)


def _dot_nt(a, b):
    return lax.dot_general(a, b, (((1,), (1,)), ((), ())), preferred_element_type=F32)


def _dot_tn(a, b):
    return lax.dot_general(a, b, (((0,), (0,)), ((), ())), preferred_element_type=F32)


def _as_row(col):
    return jnp.transpose(jnp.broadcast_to(col, (col.shape[0], LANES)))[0:1, :]


def _sigmoid(z):
    return 1.0 / (1.0 + jnp.exp(-z))


def _gelu(x):
    return 0.5 * x * (1.0 + lax.erf(x * 0.7071067811865476))


def _gelu_grad(x):
    cdf = 0.5 * (1.0 + lax.erf(x * 0.7071067811865476))
    return cdf + x * jnp.exp(-0.5 * x * x) * 0.3989422804014327


def _exchange(srcs, *, name, per_destination):
    n = len(srcs)
    slab_shapes = [s.shape[1:] if per_destination else s.shape for s in srcs]

    def body(*refs):
        src_refs, out_refs = refs[:n], refs[n:2 * n]
        send_sems, recv_sems, local_sems = refs[2 * n:]
        x, y, c = lax.axis_index("x"), lax.axis_index("y"), lax.axis_index("c")
        me = 4 * x + 2 * y + c

        def slab_for(t, dest):
            return src_refs[t].at[dest] if per_destination else src_refs[t]

        mine = [pltpu.make_async_copy(slab_for(t, me), out_refs[t].at[me], local_sems.at[t]) for t in range(n)]
        for cp in mine:
            cp.start()
        sends, arrivals = [], []
        for k in (6, 7, 4, 5, 2, 3, 1):
            px = 1 - x if k & 4 else x
            py = 1 - y if k & 2 else y
            pc = 1 - c if k & 1 else c
            peer = 4 * px + 2 * py + pc
            for t in range(n):
                sem = (k - 1) * n + t
                cp = pltpu.make_async_remote_copy(
                    src_ref=slab_for(t, peer), dst_ref=out_refs[t].at[me],
                    send_sem=send_sems.at[sem], recv_sem=recv_sems.at[sem],
                    device_id=(px, py, pc), device_id_type=pl.DeviceIdType.MESH)
                cp.start()
                sends.append(cp)
                arrivals.append(pltpu.make_async_remote_copy(
                    src_ref=slab_for(t, peer), dst_ref=out_refs[t].at[peer],
                    send_sem=send_sems.at[sem], recv_sem=recv_sems.at[sem],
                    device_id=(x, y, c), device_id_type=pl.DeviceIdType.MESH))
        for cp in arrivals:
            cp.wait_recv()
        for cp in sends:
            cp.wait_send()
        for cp in mine:
            cp.wait()

    hbm = pl.BlockSpec(memory_space=pl.ANY)
    return pl.pallas_call(
        body, name=name,
        out_shape=[jax.ShapeDtypeStruct((N_DEV,) + tuple(shape), s.dtype) for shape, s in zip(slab_shapes, srcs)],
        in_specs=[hbm] * n, out_specs=[hbm] * n,
        scratch_shapes=[pltpu.SemaphoreType.DMA(((N_DEV - 1) * n,)), pltpu.SemaphoreType.DMA(((N_DEV - 1) * n,)),
                        pltpu.SemaphoreType.DMA((n,))],
    )(*srcs)


def _gather_two_level(srcs, *, name):
    n = len(srcs)

    def body(*refs):
        src_refs, out_refs = refs[:n], refs[n:2 * n]
        send_sems, recv_sems, local_sems = refs[2 * n:]
        x, y, c = lax.axis_index("x"), lax.axis_index("y"), lax.axis_index("c")
        me, sibling = (x, y, c), (x, y, 1 - c)
        chips = [(1 - x, 1 - y), (1 - x, y), (x, 1 - y)]
        index = lambda px, py, pc: 4 * px + 2 * py + pc

        def copy(k, t, block, to, src=None):
            place = out_refs[t].at[index(*block)]
            return pltpu.make_async_remote_copy(
                src_ref=place if src is None else src, dst_ref=place,
                send_sem=send_sems.at[k * n + t], recv_sem=recv_sems.at[k * n + t],
                device_id=to, device_id_type=pl.DeviceIdType.MESH)

        mine = [pltpu.make_async_copy(src_refs[t], out_refs[t].at[index(*me)], local_sems.at[t]) for t in range(n)]
        for cp in mine:
            cp.start()
        first = [copy(1 + j, t, me, (*chip, c), src=src_refs[t]) for j, chip in enumerate(chips) for t in range(n)]
        first += [copy(0, t, me, sibling, src=src_refs[t]) for t in range(n)]
        for cp in first:
            cp.start()
        passed = []
        for j, chip in enumerate(chips):
            for t in range(n):
                copy(1 + j, t, (*chip, c), me).wait_recv()
                cp = copy(4 + j, t, (*chip, c), sibling)
                cp.start()
                passed.append(cp)
        for t in range(n):
            copy(0, t, sibling, me).wait_recv()
        for j, chip in enumerate(chips):
            for t in range(n):
                copy(4 + j, t, (*chip, 1 - c), me).wait_recv()
        for cp in first + passed:
            cp.wait_send()
        for cp in mine:
            cp.wait()

    hbm = pl.BlockSpec(memory_space=pl.ANY)
    return pl.pallas_call(
        body, name=name,
        out_shape=[jax.ShapeDtypeStruct((N_DEV,) + s.shape, s.dtype) for s in srcs],
        in_specs=[hbm] * n, out_specs=[hbm] * n,
        scratch_shapes=[pltpu.SemaphoreType.DMA((7 * n,)), pltpu.SemaphoreType.DMA((7 * n,)),
                        pltpu.SemaphoreType.DMA((n,))],
    )(*srcs)


N_CHIPS = N_DEV // 2


def _sibling_swap(srcs, *, name):
    n = len(srcs)

    def body(*refs):
        src_refs, out_refs = refs[:n], refs[n:2 * n]
        send_sems, recv_sems = refs[2 * n:]
        x, y, c = lax.axis_index("x"), lax.axis_index("y"), lax.axis_index("c")
        sends = []
        for chip in range(N_CHIPS):
            for t in range(n):
                cp = pltpu.make_async_remote_copy(
                    src_ref=src_refs[t].at[chip, 1 - c], dst_ref=out_refs[t].at[chip],
                    send_sem=send_sems.at[chip * n + t], recv_sem=recv_sems.at[chip * n + t],
                    device_id=(x, y, 1 - c), device_id_type=pl.DeviceIdType.MESH)
                cp.start()
                sends.append(cp)
        for cp in sends:
            cp.wait_recv()
        for cp in sends:
            cp.wait_send()

    hbm = pl.BlockSpec(memory_space=pl.ANY)
    return pl.pallas_call(
        body, name=name,
        out_shape=[jax.ShapeDtypeStruct((N_CHIPS,) + s.shape[2:], s.dtype) for s in srcs],
        in_specs=[hbm] * n, out_specs=[hbm] * n,
        scratch_shapes=[pltpu.SemaphoreType.DMA((N_CHIPS * n,)), pltpu.SemaphoreType.DMA((N_CHIPS * n,))],
    )(*srcs)


def _pair_sum(mine, theirs, core, *, name, tile_rows, out_dtype):
    _, _, rows, cols = mine.shape

    def body(core_ref, a_ref, b_ref, o_ref):
        o_ref[...] = (a_ref[0] + b_ref[...]).astype(out_dtype)

    return pl.pallas_call(
        body, name=name,
        grid_spec=pltpu.PrefetchScalarGridSpec(
            num_scalar_prefetch=1, grid=(N_CHIPS, rows // tile_rows),
            in_specs=[pl.BlockSpec((1, 1, tile_rows, cols), lambda q, r, core_ref: (q, core_ref[0], r, 0)),
                      pl.BlockSpec((1, tile_rows, cols), lambda q, r, core_ref: (q, r, 0))],
            out_specs=pl.BlockSpec((1, tile_rows, cols), lambda q, r, core_ref: (q, r, 0))),
        out_shape=jax.ShapeDtypeStruct((N_CHIPS, rows, cols), out_dtype),
        compiler_params=_cparams(("arbitrary", "arbitrary")),
    )(core, mine, theirs)


def _chip_exchange(srcs, *, name):
    n = len(srcs)

    def body(*refs):
        src_refs, out_refs = refs[:n], refs[n:2 * n]
        send_sems, recv_sems, local_sems = refs[2 * n:]
        x, y, c = lax.axis_index("x"), lax.axis_index("y"), lax.axis_index("c")
        my_chip = 2 * x + y
        mine = [pltpu.make_async_copy(src_refs[t].at[my_chip], out_refs[t].at[my_chip], local_sems.at[t])
                for t in range(n)]
        for cp in mine:
            cp.start()
        sends, arrivals = [], []
        for k in (3, 2, 1):
            px = 1 - x if k & 2 else x
            py = 1 - y if k & 1 else y
            peer_chip = 2 * px + py
            for t in range(n):
                sem = (k - 1) * n + t
                cp = pltpu.make_async_remote_copy(
                    src_ref=src_refs[t].at[peer_chip], dst_ref=out_refs[t].at[my_chip],
                    send_sem=send_sems.at[sem], recv_sem=recv_sems.at[sem],
                    device_id=(px, py, c), device_id_type=pl.DeviceIdType.MESH)
                cp.start()
                sends.append(cp)
                arrivals.append(pltpu.make_async_remote_copy(
                    src_ref=src_refs[t].at[peer_chip], dst_ref=out_refs[t].at[peer_chip],
                    send_sem=send_sems.at[sem], recv_sem=recv_sems.at[sem],
                    device_id=(x, y, c), device_id_type=pl.DeviceIdType.MESH))
        for cp in arrivals:
            cp.wait_recv()
        for cp in sends:
            cp.wait_send()
        for cp in mine:
            cp.wait()

    hbm = pl.BlockSpec(memory_space=pl.ANY)
    return pl.pallas_call(
        body, name=name,
        out_shape=[jax.ShapeDtypeStruct(s.shape, s.dtype) for s in srcs],
        in_specs=[hbm] * n, out_specs=[hbm] * n,
        scratch_shapes=[pltpu.SemaphoreType.DMA((3 * n,)), pltpu.SemaphoreType.DMA((3 * n,)),
                        pltpu.SemaphoreType.DMA((n,))],
    )(*srcs)


def _rope_tables(pos_col, invf_row):
    ang = pos_col.astype(F32) * invf_row
    lane = lax.broadcasted_iota(jnp.int32, ang.shape, 1)
    cos, sin = jnp.cos(ang), jnp.sin(ang)
    first = (lane >= KR_LO) & (lane < KR_LO + HALF)
    second = (lane >= KR_LO + HALF) & (lane < KR_LO + ROPE)
    return cos, jnp.where(first, sin, 0.0), jnp.where(second, sin, 0.0)


def _rope(t, cos, sin_first, sin_second, sign):
    up = pltpu.roll(t, LANES - HALF, 1)
    down = pltpu.roll(t, HALF, 1)
    return t * cos - sign * (up * sin_first) + sign * (down * sin_second)


def _fwd_proj(x, pos_col, invf_row, wp_in, w_heads, q_g, kv_g):
    t = x.shape[0]
    tm = PROJ_TILE

    def body(x_ref, pos_ref, invf_ref, win_ref, wh_ref, qg_ref, kvg_ref,
             proj_ref, q_ref, k_ref, v_ref, vt_ref):
        proj = _dot(x_ref[...].astype(BF16), win_ref[...])
        proj_ref[...] = proj
        c_q = proj[:, :Q_LORA]
        c_kv = proj[:, Q_LORA:Q_LORA + KV_LORA]
        kr_raw = proj[:, Q_LORA + KV_LORA:Q_LORA + KV_LORA + LANES]
        cqn = (c_q * lax.rsqrt(jnp.mean(c_q * c_q, axis=-1, keepdims=True) + EPS) * qg_ref[...]).astype(BF16)
        ckvn = (c_kv * lax.rsqrt(jnp.mean(c_kv * c_kv, axis=-1, keepdims=True) + EPS) * kvg_ref[...]).astype(BF16)
        cos, s1, s2 = _rope_tables(pos_ref[...], invf_ref[...])
        kr = _rope(kr_raw, cos, s1, s2, 1.0)
        lane = lax.broadcasted_iota(jnp.int32, (tm, HEAD_PAD), 1)
        for h in range(HEADS):
            q_h = _dot(cqn, wh_ref[h, :Q_LORA, :])
            kv_h = _dot(ckvn, wh_ref[h, Q_LORA:, :])
            q_ref[h] = (_rope(q_h, cos, s1, s2, 1.0) * Q_PRESCALE).astype(BF16)
            k_ref[h] = jnp.where(lane < NOPE, kv_h, kr).astype(BF16)
            v_ref[h] = kv_h.astype(BF16)
            vt_ref[h] = jnp.transpose(kv_h).astype(BF16)

    full = lambda a: pl.BlockSpec(a.shape, lambda i: (0,) * a.ndim)
    head_spec = pl.BlockSpec((HEADS, tm, HEAD_PAD), lambda i: (0, i, 0))
    head_shape = jax.ShapeDtypeStruct((HEADS, t, HEAD_PAD), BF16)
    return pl.pallas_call(
        body, name="fwd_proj", grid=(t // tm,),
        in_specs=[pl.BlockSpec((tm, D_MODEL), lambda i: (i, 0)), pl.BlockSpec((tm, 1), lambda i: (i, 0)),
                  full(invf_row), full(wp_in), full(w_heads), full(q_g), full(kv_g)],
        out_specs=[pl.BlockSpec((tm, D_IN_PAD), lambda i: (i, 0)), head_spec, head_spec, head_spec,
                   pl.BlockSpec((HEADS, HEAD_PAD, tm), lambda i: (0, 0, i))],
        out_shape=[jax.ShapeDtypeStruct((t, D_IN_PAD), F32), head_shape, head_shape, head_shape,
                   jax.ShapeDtypeStruct((HEADS, HEAD_PAD, t), BF16)],
        compiler_params=_cparams(("arbitrary",)),
    )(x, pos_col, invf_row, wp_in, w_heads, q_g, kv_g)


def _attn_fwd(q, k, vt):
    t = q.shape[1]
    bq, bk = ATTN_FWD_WIDE, ATTN_NARROW
    n_diag = bq // bk
    chunk = SOFTMAX_ROWS

    def body(q_ref, k_ref, vt_ref, o_ref, lse_ref, s0, s1, p0, p1, x0, x1, m_scr, l_scr, a_scr, acc_scr):
        i = pl.program_id(1)
        at = lambda j: pl.ds(pl.multiple_of(j * bk, bk), bk)

        def exp_pass(s_in, block_max, p_out, diagonal=False, cols=slice(None)):
            width = bq if cols == slice(None) else cols.stop - cols.start

            def load(r):
                s = s_in[r:r + chunk, cols]
                if diagonal:
                    key = lax.broadcasted_iota(jnp.int32, (chunk, width), 0) + r
                    qry = lax.broadcasted_iota(jnp.int32, (chunk, width), 1)
                    s = jnp.where(qry >= key, s, -jnp.inf)
                return s

            if diagonal:
                block_max = jnp.max(load(0), axis=0, keepdims=True)
                for r in range(chunk, bk, chunk):
                    block_max = jnp.maximum(block_max, jnp.max(load(r), axis=0, keepdims=True))
            m_old = m_scr[:, cols]
            m_new = jnp.maximum(m_old, block_max)
            alpha = jnp.exp2(m_old - m_new)
            total = jnp.zeros((1, width), F32)
            for r in range(0, bk, chunk):
                p = jnp.exp2(load(r) - m_new)
                p_out[r:r + chunk, cols] = p.astype(BF16)
                total = total + jnp.sum(p, axis=0, keepdims=True)
            m_scr[:, cols] = m_new
            l_scr[:, cols] = alpha * l_scr[:, cols] + total
            return alpha

        def scores(j, s_out, x_out):
            s = _dot_nt(k_ref[0, at(j), :], q_ref[0])
            s_out[...] = s
            x_out[...] = jnp.max(s, axis=0, keepdims=True)

        def value_product(j, p_in):
            return _dot(vt_ref[0, NOPE:, at(j)], p_in[...])

        def one_pass(j, s_in, x_in, s_out, x_out, p_prev, p_cur):
            scores(j + 1, s_out, x_out)
            acc_scr[...] = a_scr[...] * acc_scr[...] + value_product(jnp.maximum(j - 1, 0), p_prev)
            a_scr[...] = exp_pass(s_in, x_in[...], p_cur)

        scores(0, s0, x0)
        p1[...] = jnp.zeros_like(p1)
        a_scr[...] = jnp.ones_like(a_scr)
        m_scr[...] = jnp.full(m_scr.shape, -jnp.inf, F32)
        l_scr[...] = jnp.zeros_like(l_scr)
        acc_scr[...] = jnp.zeros_like(acc_scr)

        def two_passes(n, _):
            one_pass(2 * n, s0, x0, s1, x1, p1, p0)
            one_pass(2 * n + 1, s1, x1, s0, x0, p0, p1)
            return 0

        lax.fori_loop(0, (n_diag // 2) * i, two_passes, 0)
        d = n_diag * i
        alpha, p_prev, cols = a_scr[...], p1, slice(0, bq)
        for u in range(n_diag + 1):
            s_in, s_next, p_cur = (s0, s1, p0) if u % 2 == 0 else (s1, s0, p1)
            if u + 1 < n_diag:
                ahead = slice((u + 1) * bk, bq)
                s_next[:, ahead] = _dot_nt(k_ref[0, at(d + u + 1), :], q_ref[0, ahead, :])
            acc_scr[:, cols] = alpha * acc_scr[:, cols] + _dot(vt_ref[0, NOPE:, at(jnp.maximum(d + u - 1, 0))],
                                                               p_prev[:, cols])
            if u < n_diag:
                cols = slice(u * bk, bq)
                alpha = exp_pass(s_in, None, p_cur, diagonal=True, cols=cols)
                p_prev = p_cur
        o = jnp.transpose(acc_scr[...] / l_scr[...])
        o_ref[0] = jnp.concatenate([jnp.zeros_like(o), o], axis=1)
        lse_ref[0] = m_scr[...] + jnp.log2(l_scr[...])

    tile = lambda dtype: pltpu.VMEM((bk, bq), dtype)
    stat = pltpu.VMEM((1, bq), F32)
    return pl.pallas_call(
        body, name="attn_fwd", grid=(HEADS, t // bq),
        in_specs=[pl.BlockSpec((1, bq, HEAD_PAD), lambda h, i: (h, i, 0)),
                  pl.BlockSpec((1, t, HEAD_PAD), lambda h, i: (h, 0, 0)),
                  pl.BlockSpec((1, HEAD_PAD, t), lambda h, i: (h, 0, 0))],
        out_specs=[pl.BlockSpec((1, bq, HEAD_PAD), lambda h, i: (h, i, 0)),
                   pl.BlockSpec((1, 1, bq), lambda h, i: (h, 0, i))],
        out_shape=[jax.ShapeDtypeStruct((HEADS, t, HEAD_PAD), F32), jax.ShapeDtypeStruct((HEADS, 1, t), F32)],
        scratch_shapes=[tile(F32), tile(F32), tile(BF16), tile(BF16), stat, stat, stat, stat, stat,
                        pltpu.VMEM((VDIM, bq), F32)],
        compiler_params=_cparams(("arbitrary", "arbitrary")),
    )(q, k, vt)


def _mid(x, target, proj, ol, w_out, ws_low, ws_low_t, bsp, sgu_g, sgu_b, ln_g, ln_b):
    t = x.shape[0]
    tm = TOKEN_TILE
    n_steps = t // tm

    def body(x_ref, tgt_ref, za_ref, u_ref, v_ref, zb_ref, ol_ref, wout_ref, ws_ref, wst_ref, bsp_ref,
             sg_ref, sb_ref, lg_ref, lb_ref,
             dr_ref, do_ref, drow_ref, drest_ref, dwout_ref, dws_ref, dbs_ref, dlg_ref, dlb_ref, dsg_ref, dsb_ref,
             loss_ref, dbsp_acc):
        step = pl.program_id(0)

        @pl.when(step == 0)
        def _():
            dwout_ref[...] = jnp.zeros_like(dwout_ref)
            dws_ref[...] = jnp.zeros_like(dws_ref)
            dbs_ref[...] = jnp.zeros_like(dbs_ref)
            dlg_ref[...] = jnp.zeros_like(dlg_ref)
            dlb_ref[...] = jnp.zeros_like(dlb_ref)
            dsg_ref[...] = jnp.zeros_like(dsg_ref)
            dsb_ref[...] = jnp.zeros_like(dsb_ref)
            loss_ref[...] = jnp.zeros_like(loss_ref)
            dbsp_acc[...] = jnp.zeros_like(dbsp_acc)

        n_chunks = tm // CHUNK
        groups = G_WIDTH // LANES

        def side_by_side(a):
            return [jnp.concatenate([a[c * CHUNK:(c + 1) * CHUNK, g * LANES:(g + 1) * LANES] for c in range(n_chunks)],
                                    axis=1) for g in range(groups)]

        def by_chunk(wide):
            return jnp.concatenate([jnp.concatenate([wide[g][:, c * LANES:(c + 1) * LANES] for g in range(groups)], axis=1)
                                    for c in range(n_chunks)], axis=0)

        def own_lanes(h):
            lane = lax.broadcasted_iota(jnp.int32, (CHUNK, n_chunks * LANES), 1)
            return (lane % LANES) // G_HEAD_DIM == h % 2

        def spatial(w_ref, wide):
            return [sum(jnp.where(own_lanes(h), _dot(w_ref[h], wide[g]), 0.0) for h in (2 * g, 2 * g + 1))
                    for g in range(groups)]

        attn = jnp.concatenate([ol_ref[h][:, NOPE:] for h in range(HEADS)], axis=-1)
        za = za_ref[...]
        sig_a = _sigmoid(za)
        silu_a = za * sig_a
        out_a = attn * silu_a
        u = u_ref[...]
        ug = _gelu(u)
        vpre = v_ref[...]
        gv = _gelu(vpre)
        mu_v = jnp.mean(gv, axis=-1, keepdims=True)
        cen_v = gv - mu_v
        rstd_v = lax.rsqrt(jnp.mean(cen_v * cen_v, axis=-1, keepdims=True) + EPS)
        vhat = cen_v * rstd_v
        vg = vhat * sg_ref[...] + sb_ref[...]
        vg_b = vg.astype(BF16)
        sv = by_chunk(spatial(ws_ref, side_by_side(vg_b))) + jnp.tile(bsp_ref[...], (n_chunks, 1))
        sgu = ug * sv
        zb = zb_ref[...]
        sig_b = _sigmoid(zb)
        silu_b = zb * sig_b
        out_b = sgu * silu_b
        merged = jnp.concatenate([out_a, out_b], axis=-1).astype(BF16)
        r = DN_ALPHA * x_ref[...] + _dot(merged, wout_ref[...])
        mu = jnp.mean(r, axis=-1, keepdims=True)
        cen = r - mu
        rstd = lax.rsqrt(jnp.mean(cen * cen, axis=-1, keepdims=True) + EPS)
        xhat = cen * rstd
        hout = xhat * lg_ref[...] + lb_ref[...]
        err = hout - tgt_ref[...]
        row_loss = jnp.mean(err * err, axis=-1, keepdims=True)
        loss_ref[...] += jnp.broadcast_to(0.5 * jnp.sum(row_loss, axis=0, keepdims=True), loss_ref.shape)

        dh = err * (1.0 / D_MODEL)
        dlg_ref[...] += jnp.sum(dh * xhat, axis=0, keepdims=True)
        dlb_ref[...] += jnp.sum(dh, axis=0, keepdims=True)
        dxhat = dh * lg_ref[...]
        dr = rstd * (dxhat - jnp.mean(dxhat, axis=-1, keepdims=True)
                     - xhat * jnp.mean(dxhat * xhat, axis=-1, keepdims=True))
        dr_ref[...] = dr
        dr_b = dr.astype(BF16)
        dwout_ref[...] += _dot_tn(merged, dr_b)
        dmerged = _dot_nt(dr_b, wout_ref[...])
        d_out_a = dmerged[:, :G_WIDTH]
        d_out_b = dmerged[:, G_WIDTH:]
        dattn = d_out_a * silu_a
        for h in range(HEADS):
            do_h = dattn[:, h * VDIM:(h + 1) * VDIM]
            do_ref[h] = jnp.concatenate([jnp.zeros((tm, NOPE), F32), do_h], axis=-1).astype(BF16)
        feature = lax.broadcasted_iota(jnp.int32, (G_WIDTH, LANES), 0) // VDIM
        column = lax.broadcasted_iota(jnp.int32, (G_WIDTH, LANES), 1)
        head_sums = jnp.dot(dattn * attn, jnp.where(feature == column, 1.0, 0.0).astype(F32),
                            preferred_element_type=F32, precision=lax.Precision.HIGHEST)
        dsums_t = jnp.transpose(head_sums)
        for h in range(HEADS):
            drow_ref[h] = dsums_t[h:h + 1, :]
        dza = d_out_a * attn * (sig_a * (1.0 + za * (1.0 - sig_a)))
        dsgu = d_out_b * silu_b
        dzb = d_out_b * sgu * (sig_b * (1.0 + zb * (1.0 - sig_b)))
        du = dsgu * sv * _gelu_grad(u)
        dsv = dsgu * ug
        dsv_b = dsv.astype(BF16)
        for cix in range(n_chunks):
            dbsp_acc[...] += dsv[cix * CHUNK:(cix + 1) * CHUNK, :]
        dsv_wide, vg_wide = side_by_side(dsv_b), side_by_side(vg_b)
        dvg = by_chunk(spatial(wst_ref, dsv_wide))
        for h in range(HEADS):
            mine = jnp.where(own_lanes(h), dsv_wide[h // 2], jnp.zeros_like(dsv_wide[h // 2]))
            dws_ref[h] += _dot_nt(mine, vg_wide[h // 2])
        dsg_ref[...] += jnp.sum(dvg * vhat, axis=0, keepdims=True)
        dsb_ref[...] += jnp.sum(dvg, axis=0, keepdims=True)
        dvhat = dvg * sg_ref[...]
        dgv = rstd_v * (dvhat - jnp.mean(dvhat, axis=-1, keepdims=True)
                        - vhat * jnp.mean(dvhat * vhat, axis=-1, keepdims=True))
        dv = dgv * _gelu_grad(vpre)
        drest_ref[...] = jnp.concatenate([dza, du, dv, dzb], axis=-1).astype(BF16)

        @pl.when(step == n_steps - 1)
        def _():
            tri = (lax.broadcasted_iota(jnp.int32, (CHUNK, CHUNK), 0)
                   >= lax.broadcasted_iota(jnp.int32, (CHUNK, CHUNK), 1))
            for h in range(HEADS):
                dws_ref[h] = jnp.where(tri, dws_ref[h], 0.0)
            tot = dbsp_acc[...]
            lane = lax.broadcasted_iota(jnp.int32, (CHUNK, LANES), 1)
            dbs = jnp.zeros((CHUNK, LANES), F32)
            for h in range(HEADS):
                head_sum = jnp.sum(tot[:, h * G_HEAD_DIM:(h + 1) * G_HEAD_DIM], axis=-1, keepdims=True)
                dbs = jnp.where(lane == h, head_sum, dbs)
            dbs_ref[...] = dbs

    full = lambda a: pl.BlockSpec(a.shape, lambda i: (0,) * a.ndim)
    tile = lambda w, j=0: pl.BlockSpec((tm, w), lambda i, j=j: (i, j))
    heads = pl.BlockSpec((HEADS, tm, HEAD_PAD), lambda i: (0, i, 0))
    acc = lambda shape: (pl.BlockSpec(shape, lambda i: (0,) * len(shape)), jax.ShapeDtypeStruct(shape, F32))
    accs = [acc((D_MODEL, D_MODEL)), acc((HEADS, CHUNK, CHUNK)), acc((CHUNK, LANES)), acc((1, D_MODEL)),
            acc((1, D_MODEL)), acc((1, G_WIDTH)), acc((1, G_WIDTH)), acc((1, LANES))]
    return pl.pallas_call(
        body, name="mid", grid=(n_steps,),
        in_specs=[tile(D_MODEL), tile(D_MODEL), tile(G_WIDTH, 1), tile(G_WIDTH, 2), tile(G_WIDTH, 3), tile(G_WIDTH, 4),
                  heads, full(w_out), full(ws_low), full(ws_low_t), full(bsp), full(sgu_g), full(sgu_b),
                  full(ln_g), full(ln_b)],
        out_specs=[tile(D_MODEL), heads, pl.BlockSpec((HEADS, 1, tm), lambda i: (0, 0, i)), tile(4 * G_WIDTH)]
        + [a[0] for a in accs],
        out_shape=[jax.ShapeDtypeStruct((t, D_MODEL), F32), jax.ShapeDtypeStruct((HEADS, t, HEAD_PAD), BF16),
                   jax.ShapeDtypeStruct((HEADS, 1, t), F32), jax.ShapeDtypeStruct((t, 4 * G_WIDTH), BF16)]
        + [a[1] for a in accs],
        scratch_shapes=[pltpu.VMEM((CHUNK, G_WIDTH), F32)],
        compiler_params=_cparams(("arbitrary",)),
    )(x, target, proj, proj, proj, proj, ol, w_out, ws_low, ws_low_t, bsp, sgu_g, sgu_b, ln_g, ln_b)


def _attn_bwd(q, k, v, do, lse_row, d_row):
    t = q.shape[1]
    bk, bq = ATTN_BWD_WIDE, ATTN_NARROW
    n_diag = bk // bq
    half = bq // 2
    last = t // bq - 1
    chunk = SOFTMAX_ROWS

    def body(q_ref, k_ref, v_ref, do_ref, lse_ref, drow_ref, dqt_ref, dk_ref, dv_ref,
             s0, s1, e0, e1, p0, p1, g0, g1, kt_scr):
        j = pl.program_id(1)
        at = lambda i: pl.ds(pl.multiple_of(i * bq, bq), bq)

        @pl.when(j == 0)
        def _():
            dqt_ref[...] = jnp.zeros_like(dqt_ref)

        kt_scr[...] = jnp.transpose(k_ref[0].astype(F32)).astype(BF16)
        dk_ref[...] = jnp.zeros_like(dk_ref)
        dv_ref[...] = jnp.zeros_like(dv_ref)

        whole_tile = ((slice(0, bk), slice(0, bq)),)

        def queries(i, lanes):
            return pl.ds(pl.multiple_of(i * bq + lanes.start, half), lanes.stop - lanes.start)

        def products(i, s_out, e_out, areas=whole_tile):
            i = jnp.minimum(i, last)
            for keys, lanes in areas:
                s_out[keys, lanes] = _dot_nt(k_ref[0, keys, :], q_ref[0, queries(i, lanes), :])
                e_out[keys, lanes] = _dot_nt(v_ref[0, keys, :], do_ref[0, queries(i, lanes), :])

        def gradients(i, p_in, g_in, areas=whole_tile):
            for keys, lanes in areas:
                dv_ref[0, keys, :] += _dot(p_in[keys, lanes], do_ref[0, queries(i, lanes), :])
                dk_ref[0, keys, :] += _dot(g_in[keys, lanes], q_ref[0, queries(i, lanes), :])
                dqt_ref[0, :, queries(i, lanes)] += _dot(kt_scr[:, keys], g_in[keys, lanes])

        def elementwise(i, s_in, e_in, p_out, g_out, qry0=None, areas=whole_tile):
            for keys, lanes in areas:
                width = lanes.stop - lanes.start
                step = chunk if qry0 is None else half
                lse = lse_ref[0, :, queries(i, lanes)]
                dsum = drow_ref[0, :, queries(i, lanes)]
                for r in range(keys.start, keys.stop, step):
                    p = jnp.exp2(s_in[r:r + step, lanes] - lse)
                    if qry0 is not None:
                        key = lax.broadcasted_iota(jnp.int32, (step, width), 0) + r
                        qry = lax.broadcasted_iota(jnp.int32, (step, width), 1) + (qry0 + lanes.start)
                        p = jnp.where(qry >= key, p, 0.0)
                    p_out[r:r + step, lanes] = p.astype(BF16)
                    g_out[r:r + step, lanes] = (p * (e_in[r:r + step, lanes] - dsum)).astype(BF16)

        def one_pass(i, s_in, e_in, s_out, e_out, p_prev, g_prev, p_cur, g_cur):
            products(i + 1, s_out, e_out)
            gradients(i - 1, p_prev, g_prev)
            elementwise(i, s_in, e_in, p_cur, g_cur)

        first = n_diag * j

        def areas_of(u):
            if u >= n_diag:
                return whole_tile
            return ((slice(0, u * bq + half), slice(0, bq)), (slice(u * bq + half, (u + 1) * bq), slice(half, bq)))

        even, odd = (s0, e0, p0, g0), (s1, e1, p1, g1)
        products(first, s0, e0, areas_of(0))
        products(first + 1, s1, e1, areas_of(1))
        elementwise(first, s0, e0, p0, g0, qry0=0, areas=areas_of(0))
        for u in range(1, n_diag):
            (s_in, e_in, p_cur, g_cur), (s_out, e_out, p_prev, g_prev) = (odd, even) if u % 2 else (even, odd)
            products(first + u + 1, s_out, e_out, areas_of(u + 1))
            gradients(first + u - 1, p_prev, g_prev, areas_of(u - 1))
            elementwise(first + u, s_in, e_in, p_cur, g_cur, qry0=u * bq, areas=areas_of(u))
        corner = (slice(bk - half, bk), slice(0, half))
        p1[corner] = jnp.zeros((half, half), BF16)
        g1[corner] = jnp.zeros((half, half), BF16)

        def two_passes(n, _):
            i = first + n_diag + 2 * n
            one_pass(i, s0, e0, s1, e1, p1, g1, p0, g0)
            one_pass(i + 1, s1, e1, s0, e0, p0, g0, p1, g1)
            return 0

        lax.fori_loop(0, (last - first - n_diag + 1) // 2, two_passes, 0)
        gradients(last, p1, g1)
        dk_ref[0] = dk_ref[0] * LN2

    whole = pl.BlockSpec((1, t, HEAD_PAD), lambda h, j: (h, 0, 0))
    block = pl.BlockSpec((1, bk, HEAD_PAD), lambda h, j: (h, j, 0))
    rows = pl.BlockSpec((1, 1, t), lambda h, j: (h, 0, 0), pipeline_mode=pl.Buffered(1))
    shape = jax.ShapeDtypeStruct((HEADS, t, HEAD_PAD), F32)
    tile = lambda dtype: pltpu.VMEM((bk, bq), dtype)
    return pl.pallas_call(
        body, name="attn_bwd", grid=(HEADS, t // bk),
        in_specs=[whole, block, block, whole, rows, rows],
        out_specs=[pl.BlockSpec((1, HEAD_PAD, t), lambda h, j: (h, 0, 0)), block, block],
        out_shape=[jax.ShapeDtypeStruct((HEADS, HEAD_PAD, t), F32), shape, shape],
        scratch_shapes=[tile(F32), tile(F32), tile(F32), tile(F32), tile(BF16), tile(BF16),
                        tile(BF16), tile(BF16), pltpu.VMEM((HEAD_PAD, bk), BF16)],
        compiler_params=_cparams(("arbitrary", "arbitrary"), vmem_limit=ATTN_BWD_VMEM_LIMIT),
    )(q, k, v, do, lse_row, d_row)


def _bwd_tail(dq, dk, dv, proj, pos_col, invf_row, w_heads, q_g, kv_g, x, dr, drest, wp_in):
    t = proj.shape[0]
    tm = PROJ_TILE
    n_head = 4 * LANES

    def body(dq_ref, dk_ref, dv_ref, ph_ref, pos_ref, invf_ref, wh_ref, qg_ref, kvg_ref,
             x_ref, dr_ref, drest_ref, win_ref,
             gx_ref, dwin_ref, dwh_ref, dqg_ref, dkvg_ref):
        @pl.when(pl.program_id(0) == 0)
        def _():
            dwin_ref[...] = jnp.zeros_like(dwin_ref)
            dwh_ref[...] = jnp.zeros_like(dwh_ref)
            dqg_ref[...] = jnp.zeros_like(dqg_ref)
            dkvg_ref[...] = jnp.zeros_like(dkvg_ref)

        xb = x_ref[...].astype(BF16)
        dr_b = drest_ref[...]
        dwin_ref[:, n_head:] += _dot_tn(xb, dr_b)
        gx_rest = DN_ALPHA * dr_ref[...] + _dot_nt(dr_b, win_ref[:, n_head:])

        cos, s1, s2 = _rope_tables(pos_ref[...], invf_ref[...])
        lane = lax.broadcasted_iota(jnp.int32, (tm, LANES), 1)
        c_q = ph_ref[:, :Q_LORA]
        c_kv = ph_ref[:, Q_LORA:Q_LORA + KV_LORA]
        rstd_q = lax.rsqrt(jnp.mean(c_q * c_q, axis=-1, keepdims=True) + EPS)
        rstd_kv = lax.rsqrt(jnp.mean(c_kv * c_kv, axis=-1, keepdims=True) + EPS)
        qhat = c_q * rstd_q
        kvhat = c_kv * rstd_kv
        cqn = (qhat * qg_ref[...]).astype(BF16)
        ckvn = (kvhat * kvg_ref[...]).astype(BF16)
        dcqn = jnp.zeros((tm, Q_LORA), F32)
        dckvn = jnp.zeros((tm, KV_LORA), F32)
        dkr_rot = jnp.zeros((tm, LANES), F32)
        for h in range(HEADS):
            dq_b = _rope(jnp.transpose(dq_ref[h]) * ATTN_SCALE, cos, s1, s2, -1.0).astype(BF16)
            dk_h = dk_ref[h]
            dkv_b = jnp.where(lane < NOPE, dk_h, dv_ref[h]).astype(BF16)
            dkr_rot = dkr_rot + dk_h
            dwh_ref[h, :Q_LORA, :] += _dot_tn(cqn, dq_b)
            dwh_ref[h, Q_LORA:, :] += _dot_tn(ckvn, dkv_b)
            dcqn = dcqn + _dot_nt(dq_b, wh_ref[h, :Q_LORA, :])
            dckvn = dckvn + _dot_nt(dkv_b, wh_ref[h, Q_LORA:, :])
        rot_lanes = (lane >= KR_LO) & (lane < KR_LO + ROPE)
        dkr_raw = jnp.where(rot_lanes, _rope(dkr_rot, cos, s1, s2, -1.0), 0.0)
        dqg_ref[...] += jnp.sum(dcqn * qhat, axis=0, keepdims=True)
        dkvg_ref[...] += jnp.sum(dckvn * kvhat, axis=0, keepdims=True)
        dqh = dcqn * qg_ref[...]
        dkvh = dckvn * kvg_ref[...]
        dc_q = rstd_q * (dqh - qhat * jnp.mean(dqh * qhat, axis=-1, keepdims=True))
        dc_kv = rstd_kv * (dkvh - kvhat * jnp.mean(dkvh * kvhat, axis=-1, keepdims=True))
        dh_b = jnp.concatenate([dc_q, dc_kv, dkr_raw], axis=-1).astype(BF16)
        dwin_ref[:, :n_head] += _dot_tn(xb, dh_b)
        gx_ref[...] = gx_rest + _dot_nt(dh_b, win_ref[:, :n_head])

    full = lambda a: pl.BlockSpec(a.shape, lambda i: (0,) * a.ndim)
    tile = lambda w: pl.BlockSpec((tm, w), lambda i: (i, 0))
    heads = pl.BlockSpec((HEADS, tm, HEAD_PAD), lambda i: (0, i, 0))
    acc = lambda shape: (pl.BlockSpec(shape, lambda i: (0,) * len(shape)), jax.ShapeDtypeStruct(shape, F32))
    accs = [acc(wp_in.shape), acc(w_heads.shape), acc((1, Q_LORA)), acc((1, KV_LORA))]
    return pl.pallas_call(
        body, name="bwd_tail", grid=(t // tm,),
        in_specs=[pl.BlockSpec((HEADS, HEAD_PAD, tm), lambda i: (0, 0, i)), heads, heads, tile(n_head),
                  pl.BlockSpec((tm, 1), lambda i: (i, 0)), full(invf_row), full(w_heads), full(q_g), full(kv_g),
                  tile(D_MODEL), tile(D_MODEL), tile(drest.shape[1]), full(wp_in)],
        out_specs=[tile(D_MODEL)] + [a[0] for a in accs],
        out_shape=[jax.ShapeDtypeStruct((t, D_MODEL), F32)] + [a[1] for a in accs],
        compiler_params=_cparams(("arbitrary",), vmem_limit=BWD_TAIL_VMEM_LIMIT),
    )(dq, dk, dv, proj, pos_col, invf_row, w_heads, q_g, kv_g, x, dr, drest, wp_in)


def _adam(parts, w, m, v, *, name, tile_rows):
    n, rows, cols = parts.shape

    def body(p_ref, w_ref, m_ref, v_ref, g_ref, d_ref, nm_ref, nv_ref):
        g = p_ref[0].astype(F32)
        for s in range(1, n):
            g = g + p_ref[s].astype(F32)
        m_new = ADAM_B1 * m_ref[...] + (1.0 - ADAM_B1) * g
        v_new = ADAM_B2 * v_ref[...] + (1.0 - ADAM_B2) * (g * g)
        m_hat = m_new / (1.0 - ADAM_B1 ** ADAM_STEP)
        v_hat = v_new / (1.0 - ADAM_B2 ** ADAM_STEP)
        g_ref[...] = g
        d_ref[...] = -ADAM_LR * (m_hat / (jnp.sqrt(v_hat) + ADAM_EPS) + ADAM_WD * w_ref[...])
        nm_ref[...] = m_new
        nv_ref[...] = v_new

    flat = pl.BlockSpec((tile_rows, cols), lambda i: (i, 0))
    shape = jax.ShapeDtypeStruct((rows, cols), F32)
    return pl.pallas_call(
        body, name=name, grid=(rows // tile_rows,),
        in_specs=[pl.BlockSpec((n, tile_rows, cols), lambda i: (0, i, 0)), flat, flat, flat],
        out_specs=[flat] * 4, out_shape=[shape] * 4,
        compiler_params=_cparams(("arbitrary",)),
    )(parts, w, m, v)


SMALL_NAMES = ("q_norm_g", "kv_norm_g", "sgu_norm_g", "sgu_norm_b", "b_spatial", "ln_g", "ln_b")
SMALL_SIZES = (Q_LORA, KV_LORA, G_WIDTH, G_WIDTH, HEADS * CHUNK, D_MODEL, D_MODEL)


def _pack_small(vals, last=None):
    flat = jnp.concatenate([v.reshape(-1) for v in vals])
    pad = SMALL_LEN - flat.shape[0]
    if last is None:
        return jnp.pad(flat, (0, pad))
    return jnp.concatenate([flat, jnp.zeros((pad - 1,), F32), last.reshape(1)])


def _unpack_small(flat):
    out, at = [], 0
    for n in SMALL_SIZES:
        out.append(flat[at:at + n])
        at += n
    out[4] = out[4].reshape(HEADS, CHUNK)
    return out


UQ_SHARD = HEADS * (NOPE + ROPE) // N_DEV
HEAD_ROWS = Q_LORA + KV_LORA
MIXED_ROWS = HEAD_ROWS + CHUNK + SMALL_LEN // N_DEV // LANES


def _head_slab(w_uq_shard, w_ukv_shard):
    return jnp.concatenate([jnp.pad(w_uq_shard, ((0, 0), (0, LANES - UQ_SHARD))), w_ukv_shard])


IN_SHARD = D_IN // N_DEV


def _w_in_pieces():
    split = Q_LORA + KV_LORA
    moves = ((0, split, 0), (split, split + ROPE, KR_LO), (split + ROPE, D_IN, LANES - ROPE))
    pieces = []
    for s in range(N_DEV):
        lo, hi = s * IN_SHARD, (s + 1) * IN_SHARD
        for a, b, shift in moves:
            a, b = max(a, lo), min(b, hi)
            if a < b:
                pieces.append((s, a - lo, a + shift, b - a))
    return pieces


def _padded_w_in(shards):
    tr = TOKEN_TILE

    def body(sh_ref, o_ref):
        o_ref[...] = jnp.zeros_like(o_ref)
        for s, src, dst, width in _w_in_pieces():
            o_ref[:, dst:dst + width] = sh_ref[s, :, src:src + width]

    return pl.pallas_call(
        body, name="w_in_pad", grid=(D_MODEL // tr,),
        in_specs=[pl.BlockSpec((N_DEV, tr, IN_SHARD), lambda i: (0, i, 0))],
        out_specs=pl.BlockSpec((tr, D_IN_PAD), lambda i: (i, 0)),
        out_shape=jax.ShapeDtypeStruct((D_MODEL, D_IN_PAD), shards.dtype),
        compiler_params=_cparams(("arbitrary",)),
    )(shards)


def _w_in_shards(dwp_in):
    tr = TOKEN_TILE
    by_shard = [[p for p in _w_in_pieces() if p[0] == s] for s in range(N_DEV)]

    def body(w_ref, o_ref):
        for s, pieces in enumerate(by_shard):
            parts = [w_ref[:, dst:dst + width] for _, _, dst, width in pieces]
            o_ref[s] = parts[0] if len(parts) == 1 else jnp.concatenate(parts, axis=1)

    return pl.pallas_call(
        body, name="w_in_split", grid=(D_MODEL // tr,),
        in_specs=[pl.BlockSpec((tr, D_IN_PAD), lambda i: (i, 0))],
        out_specs=pl.BlockSpec((N_DEV, tr, IN_SHARD), lambda i: (0, i, 0)),
        out_shape=jax.ShapeDtypeStruct((N_DEV, D_MODEL, IN_SHARD), dwp_in.dtype),
        compiler_params=_cparams(("arbitrary",)),
    )(dwp_in)


def kernel(x, positions, w_in, q_norm_g, w_uq, kv_norm_g, w_ukv, sgu_norm_g, sgu_norm_b, w_spatial, b_spatial, w_out, ln_g, ln_b, loss_target, m_w_in, m_q_norm_g, m_w_uq, m_kv_norm_g, m_w_ukv, m_sgu_norm_g, m_sgu_norm_b, m_w_spatial, m_b_spatial, m_w_out, m_ln_g, m_ln_b, v_w_in, v_q_norm_g, v_w_uq, v_kv_norm_g, v_w_ukv, v_sgu_norm_g, v_sgu_norm_b, v_w_spatial, v_b_spatial, v_w_out, v_ln_g, v_ln_b):
    me = 4 * lax.axis_index("x") + 2 * lax.axis_index("y") + lax.axis_index("c")
    seq = x.shape[1]
    x2 = x.reshape(seq, D_MODEL)
    tgt2 = loss_target.reshape(seq, D_MODEL)
    pos_col = positions.reshape(seq, 1)

    w_in_shards, w_out_shards, w_heads = _gather_two_level(
        [w_in.astype(BF16), w_out.astype(BF16), _head_slab(w_uq, w_ukv).astype(BF16)],
        name="wgather")
    (loss_part, grad_x, d_in, d_heads, d_out, d_ws, d_bs_t, d_lng, d_lnb, d_sgug, d_sgub, d_qg, d_kvg) = _local_step(
        x2, tgt2, pos_col, w_in_shards, w_heads, w_out_shards.reshape(D_MODEL, D_MODEL), q_norm_g, kv_norm_g,
        sgu_norm_g, sgu_norm_b, w_spatial, b_spatial, ln_g, ln_b)

    small_part = _pack_small([d_qg, d_kvg, d_sgug, d_sgub, d_bs_t[:, :HEADS].T, d_lng, d_lnb], last=loss_part[0, :1])
    mixed = jnp.concatenate([d_heads, d_ws, small_part.reshape(N_DEV, -1, LANES)], axis=1)
    by_chip = [g.reshape((N_CHIPS, 2) + g.shape[1:])
               for g in (d_in, d_out.reshape(N_DEV, D_MODEL // N_DEV, D_MODEL), mixed)]
    from_sibling = _sibling_swap(by_chip, name="gswap")
    core = lax.axis_index("c").astype(jnp.int32).reshape(1)
    pair_sums = [_pair_sum(a, b, core, name=nm, tile_rows=tr, out_dtype=dt) for a, b, nm, tr, dt in zip(
        by_chip, from_sibling, ("gsum_in", "gsum_out", "gsum_mixed"), (TOKEN_TILE, D_MODEL // N_DEV, MIXED_ROWS),
        (BF16, BF16, F32))]
    recv_in, recv_out, recv_mixed = _chip_exchange(pair_sums, name="gexch")

    take = lambda a: lax.dynamic_index_in_dim(a, me, 0, keepdims=False)
    small_w = _pack_small([q_norm_g, kv_norm_g, sgu_norm_g, sgu_norm_b, b_spatial, ln_g, ln_b])
    small_m = _pack_small([m_q_norm_g, m_kv_norm_g, m_sgu_norm_g, m_sgu_norm_b, m_b_spatial, m_ln_g, m_ln_b])
    small_v = _pack_small([v_q_norm_g, v_kv_norm_g, v_sgu_norm_g, v_sgu_norm_b, v_b_spatial, v_ln_g, v_ln_b])
    own_mixed = lambda uq, ukv, sp, small: jnp.concatenate(
        [_head_slab(uq, ukv), take(sp), take(small.reshape(N_DEV, -1, LANES))])
    res_in = _adam(recv_in, w_in, m_w_in, v_w_in, name="adam_in", tile_rows=TOKEN_TILE)
    res_out = _adam(recv_out, w_out, m_w_out, v_w_out, name="adam_out", tile_rows=D_MODEL // N_DEV)
    res_mixed = _adam(recv_mixed, own_mixed(w_uq, w_ukv, w_spatial, small_w), own_mixed(m_w_uq, m_w_ukv, m_w_spatial, small_m),
                      own_mixed(v_w_uq, v_w_ukv, v_w_spatial, small_v), name="adam_mixed", tile_rows=MIXED_ROWS)

    rep_g, = _exchange([res_mixed[0][HEAD_ROWS:]], name="sgather", per_destination=False)
    rep_pack = lambda sp, small: jnp.concatenate(
        [sp.reshape(N_DEV, CHUNK, LANES), small.reshape(N_DEV, -1, LANES)], axis=1).reshape(-1, LANES)
    _, delta_rep, m_rep, v_rep = _adam(rep_g.reshape(1, N_DEV * REP_ROWS, LANES), rep_pack(w_spatial, small_w),
                                       rep_pack(m_w_spatial, small_m), rep_pack(v_w_spatial, small_v),
                                       name="adam_rep", tile_rows=N_DEV * REP_ROWS)

    def rep_unpack(a):
        a = a.reshape(N_DEV, REP_ROWS, LANES)
        small = _unpack_small(a[:, CHUNK:].reshape(-1))
        return [small[0], small[1], small[2], small[3], a[:, :CHUNK], small[4], small[5], small[6]]

    def ordered(which, rep):
        r_qg, r_kvg, r_sg, r_sb, r_ws, r_bs, r_lg, r_lb = rep_unpack(rep)
        heads = res_mixed[which]
        return [res_in[which], r_qg, heads[:Q_LORA, :UQ_SHARD], r_kvg, heads[Q_LORA:HEAD_ROWS], r_sg, r_sb, r_ws, r_bs,
                res_out[which], r_lg, r_lb]

    loss = rep_g[N_DEV - 1, REP_ROWS - 1, LANES - 1]
    outs = [loss, grad_x.reshape(x.shape)]
    outs += ordered(0, rep_g.reshape(-1, LANES))
    outs += ordered(1, delta_rep)
    outs += ordered(2, m_rep)
    outs += ordered(3, v_rep)
    return tuple(outs)


def _local_step(x2, tgt2, pos_col, w_in_shards, w_heads, w_out_full, q_norm_g, kv_norm_g, sgu_norm_g, sgu_norm_b,
                w_spatial, b_spatial, ln_g, ln_b):
    wp_in = _padded_w_in(w_in_shards)

    half = jnp.arange(HALF, dtype=F32)
    inv_freq = 1.0 / (ROPE_THETA ** (half / HALF))
    invf_row = jnp.concatenate([jnp.zeros((KR_LO,), F32), inv_freq, inv_freq,
                                jnp.zeros((LANES - KR_LO - ROPE,), F32)]).reshape(1, LANES)
    tri = jnp.tril(jnp.ones((CHUNK, CHUNK), dtype=bool))
    ws_low = jnp.where(tri[None], w_spatial, 0.0).astype(BF16)
    ws_low_t = ws_low.transpose(0, 2, 1)
    bsp = jnp.repeat(b_spatial.T, G_HEAD_DIM, axis=1)
    row = lambda a: a.reshape(1, -1)

    proj, q, k, v, vt = _fwd_proj(x2, pos_col, invf_row, wp_in, w_heads, row(q_norm_g), row(kv_norm_g))
    o, lse_row = _attn_fwd(q, k, vt)
    (dr, do, d_row, drest, d_out, d_ws, d_bs_t, d_lng, d_lnb, d_sgug, d_sgub, loss_part) = _mid(
        x2, tgt2, proj, o, w_out_full, ws_low, ws_low_t, bsp, row(sgu_norm_g), row(sgu_norm_b), row(ln_g), row(ln_b))
    dqt, dk, dv = _attn_bwd(q, k, v, do, lse_row, d_row)
    grad_x, dwp_in, d_heads, d_qg, d_kvg = _bwd_tail(dqt, dk, dv, proj, pos_col, invf_row, w_heads, row(q_norm_g),
                                                      row(kv_norm_g), x2, dr, drest, wp_in)
    return (loss_part, grad_x, _w_in_shards(dwp_in), d_heads, d_out, d_ws, d_bs_t, d_lng, d_lnb, d_sgug, d_sgub,
            d_qg, d_kvg)
```

```python
import functools
import math

import jax
import jax.numpy as jnp
from jax import lax
from jax.experimental import pallas as pl
from jax.experimental.pallas import tpu as pltpu

F32 = jnp.float32
BF16 = jnp.bfloat16

N_DEV = 8
D_MODEL = 1024
HEADS = 8
NOPE = 64
ROPE = 32
HALF = ROPE // 2
VDIM = 64
Q_LORA = 256
KV_LORA = 128
G_WIDTH = 512
G_HEAD_DIM = 64
CHUNK = 128
HEAD_PAD = 128
D_IN = 2464
D_IN_PAD = 2560
KR_LO = NOPE
SUM_ROW = NOPE - 1
LIVE_ROWS = slice(NOPE - 16, HEAD_PAD)
ROPE_THETA = 10000.0
DN_ALPHA = 2.0 ** 0.25
EPS = 1e-5
ATTN_SCALE = 1.0 / math.sqrt(NOPE + ROPE)
ADAM_LR, ADAM_B1, ADAM_B2, ADAM_EPS, ADAM_WD, ADAM_STEP = 0.001, 0.9, 0.999, 1e-08, 0.01, 10

LANES = 128
REP_ROWS = 136
SMALL_LEN = 8192
VMEM_LIMIT = 56 * 1024 * 1024
ATTN_BWD_VMEM_LIMIT = 61 * 1024 * 1024
BWD_TAIL_VMEM_LIMIT = 61 * 1024 * 1024

TOKEN_TILE = 256
PROJ_TILE = 512
ATTN_FWD_WIDE = 2048
ATTN_BWD_WIDE = 2048
ATTN_NARROW = 512
SOFTMAX_ROWS = 512
LOG2E = 1.4426950408889634
LN2 = 0.6931471805599453
Q_PRESCALE = ATTN_SCALE * LOG2E


def _cparams(sem=None, vmem_limit=VMEM_LIMIT):
    return pltpu.CompilerParams(dimension_semantics=sem, vmem_limit_bytes=vmem_limit)


def _dot(a, b):
    return jnp.dot(a, b, preferred_element_type=F32)


def _dot_nt(a, b):
    return lax.dot_general(a, b, (((1,), (1,)), ((), ())), preferred_element_type=F32)


def _dot_tn(a, b):
    return lax.dot_general(a, b, (((0,), (0,)), ((), ())), preferred_element_type=F32)


def _as_row(col):
    return jnp.transpose(jnp.broadcast_to(col, (col.shape[0], LANES)))[0:1, :]


def _sigmoid(z):
    return 1.0 / (1.0 + jnp.exp(-z))


def _gelu(x):
    return 0.5 * x * (1.0 + lax.erf(x * 0.7071067811865476))


def _gelu_grad(x):
    cdf = 0.5 * (1.0 + lax.erf(x * 0.7071067811865476))
    return cdf + x * jnp.exp(-0.5 * x * x) * 0.3989422804014327


def _exchange(srcs, *, name, per_destination):
    n = len(srcs)
    slab_shapes = [s.shape[1:] if per_destination else s.shape for s in srcs]

    def body(*refs):
        src_refs, out_refs = refs[:n], refs[n:2 * n]
        send_sems, recv_sems, local_sems = refs[2 * n:]
        x, y, c = lax.axis_index("x"), lax.axis_index("y"), lax.axis_index("c")
        me = 4 * x + 2 * y + c

        def slab_for(t, dest):
            return src_refs[t].at[dest] if per_destination else src_refs[t]

        mine = [pltpu.make_async_copy(slab_for(t, me), out_refs[t].at[me], local_sems.at[t]) for t in range(n)]
        for cp in mine:
            cp.start()
        sends, arrivals = [], []
        for k in (6, 7, 4, 5, 2, 3, 1):
            px = 1 - x if k & 4 else x
            py = 1 - y if k & 2 else y
            pc = 1 - c if k & 1 else c
            peer = 4 * px + 2 * py + pc
            for t in range(n):
                sem = (k - 1) * n + t
                cp = pltpu.make_async_remote_copy(
                    src_ref=slab_for(t, peer), dst_ref=out_refs[t].at[me],
                    send_sem=send_sems.at[sem], recv_sem=recv_sems.at[sem],
                    device_id=(px, py, pc), device_id_type=pl.DeviceIdType.MESH)
                cp.start()
                sends.append(cp)
                arrivals.append(pltpu.make_async_remote_copy(
                    src_ref=slab_for(t, peer), dst_ref=out_refs[t].at[peer],
                    send_sem=send_sems.at[sem], recv_sem=recv_sems.at[sem],
                    device_id=(x, y, c), device_id_type=pl.DeviceIdType.MESH))
        for cp in arrivals:
            cp.wait_recv()
        for cp in sends:
            cp.wait_send()
        for cp in mine:
            cp.wait()

    hbm = pl.BlockSpec(memory_space=pl.ANY)
    return pl.pallas_call(
        body, name=name,
        out_shape=[jax.ShapeDtypeStruct((N_DEV,) + tuple(shape), s.dtype) for shape, s in zip(slab_shapes, srcs)],
        in_specs=[hbm] * n, out_specs=[hbm] * n,
        scratch_shapes=[pltpu.SemaphoreType.DMA(((N_DEV - 1) * n,)), pltpu.SemaphoreType.DMA(((N_DEV - 1) * n,)),
                        pltpu.SemaphoreType.DMA((n,))],
    )(*srcs)


def _gather_two_level(srcs, *, name):
    n = len(srcs)

    def body(*refs):
        src_refs, out_refs = refs[:n], refs[n:2 * n]
        send_sems, recv_sems, local_sems = refs[2 * n:]
        x, y, c = lax.axis_index("x"), lax.axis_index("y"), lax.axis_index("c")
        me, sibling = (x, y, c), (x, y, 1 - c)
        chips = [(1 - x, 1 - y), (1 - x, y), (x, 1 - y)]
        index = lambda px, py, pc: 4 * px + 2 * py + pc

        def copy(k, t, block, to, src=None):
            place = out_refs[t].at[index(*block)]
            return pltpu.make_async_remote_copy(
                src_ref=place if src is None else src, dst_ref=place,
                send_sem=send_sems.at[k * n + t], recv_sem=recv_sems.at[k * n + t],
                device_id=to, device_id_type=pl.DeviceIdType.MESH)

        mine = [pltpu.make_async_copy(src_refs[t], out_refs[t].at[index(*me)], local_sems.at[t]) for t in range(n)]
        for cp in mine:
            cp.start()
        first = [copy(1 + j, t, me, (*chip, c), src=src_refs[t]) for j, chip in enumerate(chips) for t in range(n)]
        first += [copy(0, t, me, sibling, src=src_refs[t]) for t in range(n)]
        for cp in first:
            cp.start()
        passed = []
        for j, chip in enumerate(chips):
            for t in range(n):
                copy(1 + j, t, (*chip, c), me).wait_recv()
                cp = copy(4 + j, t, (*chip, c), sibling)
                cp.start()
                passed.append(cp)
        for t in range(n):
            copy(0, t, sibling, me).wait_recv()
        for j, chip in enumerate(chips):
            for t in range(n):
                copy(4 + j, t, (*chip, 1 - c), me).wait_recv()
        for cp in first + passed:
            cp.wait_send()
        for cp in mine:
            cp.wait()

    hbm = pl.BlockSpec(memory_space=pl.ANY)
    return pl.pallas_call(
        body, name=name,
        out_shape=[jax.ShapeDtypeStruct((N_DEV,) + s.shape, s.dtype) for s in srcs],
        in_specs=[hbm] * n, out_specs=[hbm] * n,
        scratch_shapes=[pltpu.SemaphoreType.DMA((7 * n,)), pltpu.SemaphoreType.DMA((7 * n,)),
                        pltpu.SemaphoreType.DMA((n,))],
    )(*srcs)


N_CHIPS = N_DEV // 2


def _sibling_swap(srcs, *, name):
    n = len(srcs)

    def body(*refs):
        src_refs, out_refs = refs[:n], refs[n:2 * n]
        send_sems, recv_sems = refs[2 * n:]
        x, y, c = lax.axis_index("x"), lax.axis_index("y"), lax.axis_index("c")
        sends = []
        for chip in range(N_CHIPS):
            for t in range(n):
                cp = pltpu.make_async_remote_copy(
                    src_ref=src_refs[t].at[chip, 1 - c], dst_ref=out_refs[t].at[chip],
                    send_sem=send_sems.at[chip * n + t], recv_sem=recv_sems.at[chip * n + t],
                    device_id=(x, y, 1 - c), device_id_type=pl.DeviceIdType.MESH)
                cp.start()
                sends.append(cp)
        for cp in sends:
            cp.wait_recv()
        for cp in sends:
            cp.wait_send()

    hbm = pl.BlockSpec(memory_space=pl.ANY)
    return pl.pallas_call(
        body, name=name,
        out_shape=[jax.ShapeDtypeStruct((N_CHIPS,) + s.shape[2:], s.dtype) for s in srcs],
        in_specs=[hbm] * n, out_specs=[hbm] * n,
        scratch_shapes=[pltpu.SemaphoreType.DMA((N_CHIPS * n,)), pltpu.SemaphoreType.DMA((N_CHIPS * n,))],
    )(*srcs)


def _pair_sum(mine, theirs, core, *, name, tile_rows, out_dtype):
    _, _, rows, cols = mine.shape

    def body(core_ref, a_ref, b_ref, o_ref):
        o_ref[...] = (a_ref[0] + b_ref[...]).astype(out_dtype)

    return pl.pallas_call(
        body, name=name,
        grid_spec=pltpu.PrefetchScalarGridSpec(
            num_scalar_prefetch=1, grid=(N_CHIPS, rows // tile_rows),
            in_specs=[pl.BlockSpec((1, 1, tile_rows, cols), lambda q, r, core_ref: (q, core_ref[0], r, 0)),
                      pl.BlockSpec((1, tile_rows, cols), lambda q, r, core_ref: (q, r, 0))],
            out_specs=pl.BlockSpec((1, tile_rows, cols), lambda q, r, core_ref: (q, r, 0))),
        out_shape=jax.ShapeDtypeStruct((N_CHIPS, rows, cols), out_dtype),
        compiler_params=_cparams(("arbitrary", "arbitrary")),
    )(core, mine, theirs)


def _chip_exchange(srcs, *, name):
    n = len(srcs)

    def body(*refs):
        src_refs, out_refs = refs[:n], refs[n:2 * n]
        send_sems, recv_sems, local_sems = refs[2 * n:]
        x, y, c = lax.axis_index("x"), lax.axis_index("y"), lax.axis_index("c")
        my_chip = 2 * x + y
        mine = [pltpu.make_async_copy(src_refs[t].at[my_chip], out_refs[t].at[my_chip], local_sems.at[t])
                for t in range(n)]
        for cp in mine:
            cp.start()
        sends, arrivals = [], []
        for k in (3, 2, 1):
            px = 1 - x if k & 2 else x
            py = 1 - y if k & 1 else y
            peer_chip = 2 * px + py
            for t in range(n):
                sem = (k - 1) * n + t
                cp = pltpu.make_async_remote_copy(
                    src_ref=src_refs[t].at[peer_chip], dst_ref=out_refs[t].at[my_chip],
                    send_sem=send_sems.at[sem], recv_sem=recv_sems.at[sem],
                    device_id=(px, py, c), device_id_type=pl.DeviceIdType.MESH)
                cp.start()
                sends.append(cp)
                arrivals.append(pltpu.make_async_remote_copy(
                    src_ref=src_refs[t].at[peer_chip], dst_ref=out_refs[t].at[peer_chip],
                    send_sem=send_sems.at[sem], recv_sem=recv_sems.at[sem],
                    device_id=(x, y, c), device_id_type=pl.DeviceIdType.MESH))
        for cp in arrivals:
            cp.wait_recv()
        for cp in sends:
            cp.wait_send()
        for cp in mine:
            cp.wait()

    hbm = pl.BlockSpec(memory_space=pl.ANY)
    return pl.pallas_call(
        body, name=name,
        out_shape=[jax.ShapeDtypeStruct(s.shape, s.dtype) for s in srcs],
        in_specs=[hbm] * n, out_specs=[hbm] * n,
        scratch_shapes=[pltpu.SemaphoreType.DMA((3 * n,)), pltpu.SemaphoreType.DMA((3 * n,)),
                        pltpu.SemaphoreType.DMA((n,))],
    )(*srcs)


def _rope_tables(pos_col, invf_row):
    ang = pos_col.astype(F32) * invf_row
    lane = lax.broadcasted_iota(jnp.int32, ang.shape, 1)
    cos, sin = jnp.cos(ang), jnp.sin(ang)
    first = (lane >= KR_LO) & (lane < KR_LO + HALF)
    second = (lane >= KR_LO + HALF) & (lane < KR_LO + ROPE)
    return cos, jnp.where(first, sin, 0.0), jnp.where(second, sin, 0.0)


def _rope(t, cos, sin_first, sin_second, sign):
    up = pltpu.roll(t, LANES - HALF, 1)
    down = pltpu.roll(t, HALF, 1)
    return t * cos - sign * (up * sin_first) + sign * (down * sin_second)


def _fwd_proj(x, pos_col, invf_row, wp_in, w_heads, q_g, kv_g):
    t = x.shape[0]
    tm = PROJ_TILE

    def body(x_ref, pos_ref, invf_ref, win_ref, wh_ref, qg_ref, kvg_ref,
             proj_ref, q_ref, k_ref, v_ref, vt_ref):
        proj = _dot(x_ref[...].astype(BF16), win_ref[...])
        proj_ref[...] = proj
        c_q = proj[:, :Q_LORA]
        c_kv = proj[:, Q_LORA:Q_LORA + KV_LORA]
        kr_raw = proj[:, Q_LORA + KV_LORA:Q_LORA + KV_LORA + LANES]
        cqn = (c_q * lax.rsqrt(jnp.mean(c_q * c_q, axis=-1, keepdims=True) + EPS) * qg_ref[...]).astype(BF16)
        ckvn = (c_kv * lax.rsqrt(jnp.mean(c_kv * c_kv, axis=-1, keepdims=True) + EPS) * kvg_ref[...]).astype(BF16)
        cos, s1, s2 = _rope_tables(pos_ref[...], invf_ref[...])
        kr = _rope(kr_raw, cos, s1, s2, 1.0)
        lane = lax.broadcasted_iota(jnp.int32, (tm, HEAD_PAD), 1)
        for h in range(HEADS):
            q_h = _dot(cqn, wh_ref[h, :Q_LORA, :])
            kv_h = _dot(ckvn, wh_ref[h, Q_LORA:, :])
            q_ref[h] = (_rope(q_h, cos, s1, s2, 1.0) * Q_PRESCALE).astype(BF16)
            k_ref[h] = jnp.where(lane < NOPE, kv_h, kr).astype(BF16)
            v_ref[h] = kv_h.astype(BF16)
            vt_ref[h] = jnp.transpose(jnp.where(lane == SUM_ROW, 1.0, kv_h)).astype(BF16)

    full = lambda a: pl.BlockSpec(a.shape, lambda i: (0,) * a.ndim)
    head_spec = pl.BlockSpec((HEADS, tm, HEAD_PAD), lambda i: (0, i, 0))
    head_shape = jax.ShapeDtypeStruct((HEADS, t, HEAD_PAD), BF16)
    return pl.pallas_call(
        body, name="fwd_proj", grid=(t // tm,),
        in_specs=[pl.BlockSpec((tm, D_MODEL), lambda i: (i, 0)), pl.BlockSpec((tm, 1), lambda i: (i, 0)),
                  full(invf_row), full(wp_in), full(w_heads), full(q_g), full(kv_g)],
        out_specs=[pl.BlockSpec((tm, D_IN_PAD), lambda i: (i, 0)), head_spec, head_spec, head_spec,
                   pl.BlockSpec((HEADS, HEAD_PAD, tm), lambda i: (0, 0, i))],
        out_shape=[jax.ShapeDtypeStruct((t, D_IN_PAD), F32), head_shape, head_shape, head_shape,
                   jax.ShapeDtypeStruct((HEADS, HEAD_PAD, t), BF16)],
        compiler_params=_cparams(("arbitrary",)),
    )(x, pos_col, invf_row, wp_in, w_heads, q_g, kv_g)


def _attn_fwd(q, k, vt):
    t = q.shape[1]
    bq, bk = ATTN_FWD_WIDE, ATTN_NARROW
    n_diag = bq // bk
    chunk = SOFTMAX_ROWS

    def body(q_ref, k_ref, vt_ref, o_ref, lse_ref, s0, s1, p0, p1, x0, x1, m_scr, a_scr, acc_scr):
        i = pl.program_id(1)
        at = lambda j: pl.ds(pl.multiple_of(j * bk, bk), bk)

        def exp_pass(s_in, block_max, p_out, diagonal=False, cols=slice(None)):
            width = bq if cols == slice(None) else cols.stop - cols.start

            def load(r):
                s = s_in[r:r + chunk, cols]
                if diagonal:
                    key = lax.broadcasted_iota(jnp.int32, (chunk, width), 0) + r
                    qry = lax.broadcasted_iota(jnp.int32, (chunk, width), 1)
                    s = jnp.where(qry >= key, s, -jnp.inf)
                return s

            if diagonal:
                block_max = jnp.max(load(0), axis=0, keepdims=True)
                for r in range(chunk, bk, chunk):
                    block_max = jnp.maximum(block_max, jnp.max(load(r), axis=0, keepdims=True))
            m_old = m_scr[:, cols]
            m_new = jnp.maximum(m_old, block_max)
            alpha = jnp.exp2(m_old - m_new)
            for r in range(0, bk, chunk):
                p_out[r:r + chunk, cols] = jnp.exp2(load(r) - m_new).astype(BF16)
            m_scr[:, cols] = m_new
            return alpha

        def scores(j, s_out, x_out):
            s = _dot_nt(k_ref[0, at(j), :], q_ref[0])
            s_out[...] = s
            x_out[...] = jnp.max(s, axis=0, keepdims=True)

        def value_product(j, p_in):
            return _dot(vt_ref[0, LIVE_ROWS, at(j)], p_in[...])

        def one_pass(j, s_in, x_in, s_out, x_out, p_prev, p_cur):
            scores(j + 1, s_out, x_out)
            acc_scr[...] = a_scr[...] * acc_scr[...] + value_product(jnp.maximum(j - 1, 0), p_prev)
            a_scr[...] = exp_pass(s_in, x_in[...], p_cur)

        scores(0, s0, x0)
        p1[...] = jnp.zeros_like(p1)
        a_scr[...] = jnp.ones_like(a_scr)
        m_scr[...] = jnp.full(m_scr.shape, -jnp.inf, F32)
        acc_scr[...] = jnp.zeros_like(acc_scr)

        def two_passes(n, _):
            one_pass(2 * n, s0, x0, s1, x1, p1, p0)
            one_pass(2 * n + 1, s1, x1, s0, x0, p0, p1)
            return 0

        lax.fori_loop(0, (n_diag // 2) * i, two_passes, 0)
        d = n_diag * i
        alpha, p_prev, cols = a_scr[...], p1, slice(0, bq)
        for u in range(n_diag + 1):
            s_in, s_next, p_cur = (s0, s1, p0) if u % 2 == 0 else (s1, s0, p1)
            if u + 1 < n_diag:
                ahead = slice((u + 1) * bk, bq)
                s_next[:, ahead] = _dot_nt(k_ref[0, at(d + u + 1), :], q_ref[0, ahead, :])
            acc_scr[:, cols] = alpha * acc_scr[:, cols] + _dot(vt_ref[0, LIVE_ROWS, at(jnp.maximum(d + u - 1, 0))],
                                                               p_prev[:, cols])
            if u < n_diag:
                cols = slice(u * bk, bq)
                alpha = exp_pass(s_in, None, p_cur, diagonal=True, cols=cols)
                p_prev = p_cur
        denom = acc_scr[SUM_ROW - LIVE_ROWS.start:NOPE - LIVE_ROWS.start, :]
        o = jnp.transpose(acc_scr[NOPE - LIVE_ROWS.start:, :] / denom)
        o_ref[0] = jnp.concatenate([jnp.zeros_like(o), o], axis=1)
        lse_ref[0] = m_scr[...] + jnp.log2(denom)

    tile = lambda dtype: pltpu.VMEM((bk, bq), dtype)
    stat = pltpu.VMEM((1, bq), F32)
    return pl.pallas_call(
        body, name="attn_fwd", grid=(HEADS, t // bq),
        in_specs=[pl.BlockSpec((1, bq, HEAD_PAD), lambda h, i: (h, i, 0)),
                  pl.BlockSpec((1, t, HEAD_PAD), lambda h, i: (h, 0, 0)),
                  pl.BlockSpec((1, HEAD_PAD, t), lambda h, i: (h, 0, 0))],
        out_specs=[pl.BlockSpec((1, bq, HEAD_PAD), lambda h, i: (h, i, 0)),
                   pl.BlockSpec((1, 1, bq), lambda h, i: (h, 0, i))],
        out_shape=[jax.ShapeDtypeStruct((HEADS, t, HEAD_PAD), F32), jax.ShapeDtypeStruct((HEADS, 1, t), F32)],
        scratch_shapes=[tile(F32), tile(F32), tile(BF16), tile(BF16), stat, stat, stat, stat,
                        pltpu.VMEM((HEAD_PAD - LIVE_ROWS.start, bq), F32)],
        compiler_params=_cparams(("arbitrary", "arbitrary")),
    )(q, k, vt)


def _mid(x, target, proj, ol, w_out, ws_low, ws_low_t, bsp, sgu_g, sgu_b, ln_g, ln_b):
    t = x.shape[0]
    tm = TOKEN_TILE
    n_steps = t // tm

    def body(x_ref, tgt_ref, za_ref, u_ref, v_ref, zb_ref, ol_ref, wout_ref, ws_ref, wst_ref, bsp_ref,
             sg_ref, sb_ref, lg_ref, lb_ref,
             dr_ref, do_ref, drow_ref, drest_ref, dwout_ref, dws_ref, dbs_ref, dlg_ref, dlb_ref, dsg_ref, dsb_ref,
             loss_ref, dbsp_acc):
        step = pl.program_id(0)

        @pl.when(step == 0)
        def _():
            dwout_ref[...] = jnp.zeros_like(dwout_ref)
            dws_ref[...] = jnp.zeros_like(dws_ref)
            dbs_ref[...] = jnp.zeros_like(dbs_ref)
            dlg_ref[...] = jnp.zeros_like(dlg_ref)
            dlb_ref[...] = jnp.zeros_like(dlb_ref)
            dsg_ref[...] = jnp.zeros_like(dsg_ref)
            dsb_ref[...] = jnp.zeros_like(dsb_ref)
            loss_ref[...] = jnp.zeros_like(loss_ref)
            dbsp_acc[...] = jnp.zeros_like(dbsp_acc)

        n_chunks = tm // CHUNK
        groups = G_WIDTH // LANES

        def side_by_side(a):
            return [jnp.concatenate([a[c * CHUNK:(c + 1) * CHUNK, g * LANES:(g + 1) * LANES] for c in range(n_chunks)],
                                    axis=1) for g in range(groups)]

        def by_chunk(wide):
            return jnp.concatenate([jnp.concatenate([wide[g][:, c * LANES:(c + 1) * LANES] for g in range(groups)], axis=1)
                                    for c in range(n_chunks)], axis=0)

        def own_lanes(h):
            lane = lax.broadcasted_iota(jnp.int32, (CHUNK, n_chunks * LANES), 1)
            return (lane % LANES) // G_HEAD_DIM == h % 2

        def spatial(w_ref, wide):
            return [sum(jnp.where(own_lanes(h), _dot(w_ref[h], wide[g]), 0.0) for h in (2 * g, 2 * g + 1))
                    for g in range(groups)]

        attn = jnp.concatenate([ol_ref[h][:, NOPE:] for h in range(HEADS)], axis=-1)
        za = za_ref[...]
        sig_a = _sigmoid(za)
        silu_a = za * sig_a
        out_a = attn * silu_a
        u = u_ref[...]
        ug = _gelu(u)
        vpre = v_ref[...]
        gv = _gelu(vpre)
        mu_v = jnp.mean(gv, axis=-1, keepdims=True)
        cen_v = gv - mu_v
        rstd_v = lax.rsqrt(jnp.mean(cen_v * cen_v, axis=-1, keepdims=True) + EPS)
        vhat = cen_v * rstd_v
        vg = vhat * sg_ref[...] + sb_ref[...]
        vg_b = vg.astype(BF16)
        sv = by_chunk(spatial(ws_ref, side_by_side(vg_b))) + jnp.tile(bsp_ref[...], (n_chunks, 1))
        sgu = ug * sv
        zb = zb_ref[...]
        sig_b = _sigmoid(zb)
        silu_b = zb * sig_b
        out_b = sgu * silu_b
        merged = jnp.concatenate([out_a, out_b], axis=-1).astype(BF16)
        r = DN_ALPHA * x_ref[...] + _dot(merged, wout_ref[...])
        mu = jnp.mean(r, axis=-1, keepdims=True)
        cen = r - mu
        rstd = lax.rsqrt(jnp.mean(cen * cen, axis=-1, keepdims=True) + EPS)
        xhat = cen * rstd
        hout = xhat * lg_ref[...] + lb_ref[...]
        err = hout - tgt_ref[...]
        row_loss = jnp.mean(err * err, axis=-1, keepdims=True)
        loss_ref[...] += jnp.broadcast_to(0.5 * jnp.sum(row_loss, axis=0, keepdims=True), loss_ref.shape)

        dh = err * (1.0 / D_MODEL)
        dlg_ref[...] += jnp.sum(dh * xhat, axis=0, keepdims=True)
        dlb_ref[...] += jnp.sum(dh, axis=0, keepdims=True)
        dxhat = dh * lg_ref[...]
        dr = rstd * (dxhat - jnp.mean(dxhat, axis=-1, keepdims=True)
                     - xhat * jnp.mean(dxhat * xhat, axis=-1, keepdims=True))
        dr_ref[...] = dr
        dr_b = dr.astype(BF16)
        dwout_ref[...] += _dot_tn(merged, dr_b)
        dmerged = _dot_nt(dr_b, wout_ref[...])
        d_out_a = dmerged[:, :G_WIDTH]
        d_out_b = dmerged[:, G_WIDTH:]
        dattn = d_out_a * silu_a
        for h in range(HEADS):
            do_h = dattn[:, h * VDIM:(h + 1) * VDIM]
            do_ref[h] = jnp.concatenate([jnp.zeros((tm, NOPE), F32), do_h], axis=-1).astype(BF16)
        feature = lax.broadcasted_iota(jnp.int32, (G_WIDTH, LANES), 0) // VDIM
        column = lax.broadcasted_iota(jnp.int32, (G_WIDTH, LANES), 1)
        head_sums = jnp.dot(dattn * attn, jnp.where(feature == column, 1.0, 0.0).astype(F32),
                            preferred_element_type=F32, precision=lax.Precision.HIGHEST)
        dsums_t = jnp.transpose(head_sums)
        for h in range(HEADS):
            drow_ref[h] = dsums_t[h:h + 1, :]
        dza = d_out_a * attn * (sig_a * (1.0 + za * (1.0 - sig_a)))
        dsgu = d_out_b * silu_b
        dzb = d_out_b * sgu * (sig_b * (1.0 + zb * (1.0 - sig_b)))
        du = dsgu * sv * _gelu_grad(u)
        dsv = dsgu * ug
        dsv_b = dsv.astype(BF16)
        for cix in range(n_chunks):
            dbsp_acc[...] += dsv[cix * CHUNK:(cix + 1) * CHUNK, :]
        dsv_wide, vg_wide = side_by_side(dsv_b), side_by_side(vg_b)
        dvg = by_chunk(spatial(wst_ref, dsv_wide))
        for h in range(HEADS):
            mine = jnp.where(own_lanes(h), dsv_wide[h // 2], jnp.zeros_like(dsv_wide[h // 2]))
            dws_ref[h] += _dot_nt(mine, vg_wide[h // 2])
        dsg_ref[...] += jnp.sum(dvg * vhat, axis=0, keepdims=True)
        dsb_ref[...] += jnp.sum(dvg, axis=0, keepdims=True)
        dvhat = dvg * sg_ref[...]
        dgv = rstd_v * (dvhat - jnp.mean(dvhat, axis=-1, keepdims=True)
                        - vhat * jnp.mean(dvhat * vhat, axis=-1, keepdims=True))
        dv = dgv * _gelu_grad(vpre)
        drest_ref[...] = jnp.concatenate([dza, du, dv, dzb], axis=-1).astype(BF16)

        @pl.when(step == n_steps - 1)
        def _():
            tri = (lax.broadcasted_iota(jnp.int32, (CHUNK, CHUNK), 0)
                   >= lax.broadcasted_iota(jnp.int32, (CHUNK, CHUNK), 1))
            for h in range(HEADS):
                dws_ref[h] = jnp.where(tri, dws_ref[h], 0.0)
            tot = dbsp_acc[...]
            lane = lax.broadcasted_iota(jnp.int32, (CHUNK, LANES), 1)
            dbs = jnp.zeros((CHUNK, LANES), F32)
            for h in range(HEADS):
                head_sum = jnp.sum(tot[:, h * G_HEAD_DIM:(h + 1) * G_HEAD_DIM], axis=-1, keepdims=True)
                dbs = jnp.where(lane == h, head_sum, dbs)
            dbs_ref[...] = dbs

    full = lambda a: pl.BlockSpec(a.shape, lambda i: (0,) * a.ndim)
    tile = lambda w, j=0: pl.BlockSpec((tm, w), lambda i, j=j: (i, j))
    heads = pl.BlockSpec((HEADS, tm, HEAD_PAD), lambda i: (0, i, 0))
    acc = lambda shape: (pl.BlockSpec(shape, lambda i: (0,) * len(shape)), jax.ShapeDtypeStruct(shape, F32))
    accs = [acc((D_MODEL, D_MODEL)), acc((HEADS, CHUNK, CHUNK)), acc((CHUNK, LANES)), acc((1, D_MODEL)),
            acc((1, D_MODEL)), acc((1, G_WIDTH)), acc((1, G_WIDTH)), acc((1, LANES))]
    return pl.pallas_call(
        body, name="mid", grid=(n_steps,),
        in_specs=[tile(D_MODEL), tile(D_MODEL), tile(G_WIDTH, 1), tile(G_WIDTH, 2), tile(G_WIDTH, 3), tile(G_WIDTH, 4),
                  heads, full(w_out), full(ws_low), full(ws_low_t), full(bsp), full(sgu_g), full(sgu_b),
                  full(ln_g), full(ln_b)],
        out_specs=[tile(D_MODEL), heads, pl.BlockSpec((HEADS, 1, tm), lambda i: (0, 0, i)), tile(4 * G_WIDTH)]
        + [a[0] for a in accs],
        out_shape=[jax.ShapeDtypeStruct((t, D_MODEL), F32), jax.ShapeDtypeStruct((HEADS, t, HEAD_PAD), BF16),
                   jax.ShapeDtypeStruct((HEADS, 1, t), F32), jax.ShapeDtypeStruct((t, 4 * G_WIDTH), BF16)]
        + [a[1] for a in accs],
        scratch_shapes=[pltpu.VMEM((CHUNK, G_WIDTH), F32)],
        compiler_params=_cparams(("arbitrary",)),
    )(x, target, proj, proj, proj, proj, ol, w_out, ws_low, ws_low_t, bsp, sgu_g, sgu_b, ln_g, ln_b)


def _attn_bwd(q, k, v, do, lse_row, d_row):
    t = q.shape[1]
    bk, bq = ATTN_BWD_WIDE, ATTN_NARROW
    n_diag = bk // bq
    half = bq // 2
    last = t // bq - 1
    chunk = SOFTMAX_ROWS

    def body(q_ref, k_ref, v_ref, do_ref, lse_ref, drow_ref, dqt_ref, dk_ref, dv_ref,
             s0, s1, e0, e1, p0, p1, g0, g1, kt_scr):
        j = pl.program_id(1)
        at = lambda i: pl.ds(pl.multiple_of(i * bq, bq), bq)

        @pl.when(j == 0)
        def _():
            dqt_ref[...] = jnp.zeros_like(dqt_ref)

        kt_scr[...] = jnp.transpose(k_ref[0].astype(F32)).astype(BF16)
        dk_ref[...] = jnp.zeros_like(dk_ref)
        dv_ref[...] = jnp.zeros_like(dv_ref)

        whole_tile = ((slice(0, bk), slice(0, bq)),)

        def queries(i, lanes):
            return pl.ds(pl.multiple_of(i * bq + lanes.start, half), lanes.stop - lanes.start)

        def products(i, s_out, e_out, areas=whole_tile):
            i = jnp.minimum(i, last)
            for keys, lanes in areas:
                s_out[keys, lanes] = _dot_nt(k_ref[0, keys, :], q_ref[0, queries(i, lanes), :])
                e_out[keys, lanes] = _dot_nt(v_ref[0, keys, :], do_ref[0, queries(i, lanes), :])

        def gradients(i, p_in, g_in, areas=whole_tile):
            for keys, lanes in areas:
                dv_ref[0, keys, :] += _dot(p_in[keys, lanes], do_ref[0, queries(i, lanes), :])
                dk_ref[0, keys, :] += _dot(g_in[keys, lanes], q_ref[0, queries(i, lanes), :])
                dqt_ref[0, :, queries(i, lanes)] += _dot(kt_scr[:, keys], g_in[keys, lanes])

        def elementwise(i, s_in, e_in, p_out, g_out, qry0=None, areas=whole_tile):
            for keys, lanes in areas:
                width = lanes.stop - lanes.start
                step = chunk if qry0 is None else half
                lse = lse_ref[0, :, queries(i, lanes)]
                dsum = drow_ref[0, :, queries(i, lanes)]
                for r in range(keys.start, keys.stop, step):
                    p = jnp.exp2(s_in[r:r + step, lanes] - lse)
                    if qry0 is not None:
                        key = lax.broadcasted_iota(jnp.int32, (step, width), 0) + r
                        qry = lax.broadcasted_iota(jnp.int32, (step, width), 1) + (qry0 + lanes.start)
                        p = jnp.where(qry >= key, p, 0.0)
                    p_out[r:r + step, lanes] = p.astype(BF16)
                    g_out[r:r + step, lanes] = (p * (e_in[r:r + step, lanes] - dsum)).astype(BF16)

        def one_pass(i, s_in, e_in, s_out, e_out, p_prev, g_prev, p_cur, g_cur):
            products(i + 1, s_out, e_out)
            gradients(i - 1, p_prev, g_prev)
            elementwise(i, s_in, e_in, p_cur, g_cur)

        first = n_diag * j

        def areas_of(u):
            if u >= n_diag:
                return whole_tile
            return ((slice(0, u * bq + half), slice(0, bq)), (slice(u * bq + half, (u + 1) * bq), slice(half, bq)))

        even, odd = (s0, e0, p0, g0), (s1, e1, p1, g1)
        products(first, s0, e0, areas_of(0))
        products(first + 1, s1, e1, areas_of(1))
        elementwise(first, s0, e0, p0, g0, qry0=0, areas=areas_of(0))
        for u in range(1, n_diag):
            (s_in, e_in, p_cur, g_cur), (s_out, e_out, p_prev, g_prev) = (odd, even) if u % 2 else (even, odd)
            products(first + u + 1, s_out, e_out, areas_of(u + 1))
            gradients(first + u - 1, p_prev, g_prev, areas_of(u - 1))
            elementwise(first + u, s_in, e_in, p_cur, g_cur, qry0=u * bq, areas=areas_of(u))
        corner = (slice(bk - half, bk), slice(0, half))
        p1[corner] = jnp.zeros((half, half), BF16)
        g1[corner] = jnp.zeros((half, half), BF16)

        def two_passes(n, _):
            i = first + n_diag + 2 * n
            one_pass(i, s0, e0, s1, e1, p1, g1, p0, g0)
            one_pass(i + 1, s1, e1, s0, e0, p0, g0, p1, g1)
            return 0

        lax.fori_loop(0, (last - first - n_diag + 1) // 2, two_passes, 0)
        gradients(last, p1, g1)
        dk_ref[0] = dk_ref[0] * LN2

    whole = pl.BlockSpec((1, t, HEAD_PAD), lambda h, j: (h, 0, 0))
    block = pl.BlockSpec((1, bk, HEAD_PAD), lambda h, j: (h, j, 0))
    rows = pl.BlockSpec((1, 1, t), lambda h, j: (h, 0, 0), pipeline_mode=pl.Buffered(1))
    shape = jax.ShapeDtypeStruct((HEADS, t, HEAD_PAD), F32)
    tile = lambda dtype: pltpu.VMEM((bk, bq), dtype)
    return pl.pallas_call(
        body, name="attn_bwd", grid=(HEADS, t // bk),
        in_specs=[whole, block, block, whole, rows, rows],
        out_specs=[pl.BlockSpec((1, HEAD_PAD, t), lambda h, j: (h, 0, 0)), block, block],
        out_shape=[jax.ShapeDtypeStruct((HEADS, HEAD_PAD, t), F32), shape, shape],
        scratch_shapes=[tile(F32), tile(F32), tile(F32), tile(F32), tile(BF16), tile(BF16),
                        tile(BF16), tile(BF16), pltpu.VMEM((HEAD_PAD, bk), BF16)],
        compiler_params=_cparams(("arbitrary", "arbitrary"), vmem_limit=ATTN_BWD_VMEM_LIMIT),
    )(q, k, v, do, lse_row, d_row)


def _bwd_tail(dq, dk, dv, proj, pos_col, invf_row, w_heads, q_g, kv_g, x, dr, drest, wp_in):
    t = proj.shape[0]
    tm = PROJ_TILE
    n_head = 4 * LANES

    def body(dq_ref, dk_ref, dv_ref, ph_ref, pos_ref, invf_ref, wh_ref, qg_ref, kvg_ref,
             x_ref, dr_ref, drest_ref, win_ref,
             gx_ref, dwin_ref, dwh_ref, dqg_ref, dkvg_ref):
        @pl.when(pl.program_id(0) == 0)
        def _():
            dwin_ref[...] = jnp.zeros_like(dwin_ref)
            dwh_ref[...] = jnp.zeros_like(dwh_ref)
            dqg_ref[...] = jnp.zeros_like(dqg_ref)
            dkvg_ref[...] = jnp.zeros_like(dkvg_ref)

        xb = x_ref[...].astype(BF16)
        dr_b = drest_ref[...]
        dwin_ref[:, n_head:] += _dot_tn(xb, dr_b)
        gx_rest = DN_ALPHA * dr_ref[...] + _dot_nt(dr_b, win_ref[:, n_head:])

        cos, s1, s2 = _rope_tables(pos_ref[...], invf_ref[...])
        lane = lax.broadcasted_iota(jnp.int32, (tm, LANES), 1)
        c_q = ph_ref[:, :Q_LORA]
        c_kv = ph_ref[:, Q_LORA:Q_LORA + KV_LORA]
        rstd_q = lax.rsqrt(jnp.mean(c_q * c_q, axis=-1, keepdims=True) + EPS)
        rstd_kv = lax.rsqrt(jnp.mean(c_kv * c_kv, axis=-1, keepdims=True) + EPS)
        qhat = c_q * rstd_q
        kvhat = c_kv * rstd_kv
        cqn = (qhat * qg_ref[...]).astype(BF16)
        ckvn = (kvhat * kvg_ref[...]).astype(BF16)
        dcqn = jnp.zeros((tm, Q_LORA), F32)
        dckvn = jnp.zeros((tm, KV_LORA), F32)
        dkr_rot = jnp.zeros((tm, LANES), F32)
        for h in range(HEADS):
            dq_b = _rope(jnp.transpose(dq_ref[h]) * ATTN_SCALE, cos, s1, s2, -1.0).astype(BF16)
            dk_h = dk_ref[h]
            dkv_b = jnp.where(lane < NOPE, dk_h, dv_ref[h]).astype(BF16)
            dkr_rot = dkr_rot + dk_h
            dwh_ref[h, :Q_LORA, :] += _dot_tn(cqn, dq_b)
            dwh_ref[h, Q_LORA:, :] += _dot_tn(ckvn, dkv_b)
            dcqn = dcqn + _dot_nt(dq_b, wh_ref[h, :Q_LORA, :])
            dckvn = dckvn + _dot_nt(dkv_b, wh_ref[h, Q_LORA:, :])
        rot_lanes = (lane >= KR_LO) & (lane < KR_LO + ROPE)
        dkr_raw = jnp.where(rot_lanes, _rope(dkr_rot, cos, s1, s2, -1.0), 0.0)
        dqg_ref[...] += jnp.sum(dcqn * qhat, axis=0, keepdims=True)
        dkvg_ref[...] += jnp.sum(dckvn * kvhat, axis=0, keepdims=True)
        dqh = dcqn * qg_ref[...]
        dkvh = dckvn * kvg_ref[...]
        dc_q = rstd_q * (dqh - qhat * jnp.mean(dqh * qhat, axis=-1, keepdims=True))
        dc_kv = rstd_kv * (dkvh - kvhat * jnp.mean(dkvh * kvhat, axis=-1, keepdims=True))
        dh_b = jnp.concatenate([dc_q, dc_kv, dkr_raw], axis=-1).astype(BF16)
        dwin_ref[:, :n_head] += _dot_tn(xb, dh_b)
        gx_ref[...] = gx_rest + _dot_nt(dh_b, win_ref[:, :n_head])

    full = lambda a: pl.BlockSpec(a.shape, lambda i: (0,) * a.ndim)
    tile = lambda w: pl.BlockSpec((tm, w), lambda i: (i, 0))
    heads = pl.BlockSpec((HEADS, tm, HEAD_PAD), lambda i: (0, i, 0))
    acc = lambda shape: (pl.BlockSpec(shape, lambda i: (0,) * len(shape)), jax.ShapeDtypeStruct(shape, F32))
    accs = [acc(wp_in.shape), acc(w_heads.shape), acc((1, Q_LORA)), acc((1, KV_LORA))]
    return pl.pallas_call(
        body, name="bwd_tail", grid=(t // tm,),
        in_specs=[pl.BlockSpec((HEADS, HEAD_PAD, tm), lambda i: (0, 0, i)), heads, heads, tile(n_head),
                  pl.BlockSpec((tm, 1), lambda i: (i, 0)), full(invf_row), full(w_heads), full(q_g), full(kv_g),
                  tile(D_MODEL), tile(D_MODEL), tile(drest.shape[1]), full(wp_in)],
        out_specs=[tile(D_MODEL)] + [a[0] for a in accs],
        out_shape=[jax.ShapeDtypeStruct((t, D_MODEL), F32)] + [a[1] for a in accs],
        compiler_params=_cparams(("arbitrary",), vmem_limit=BWD_TAIL_VMEM_LIMIT),
    )(dq, dk, dv, proj, pos_col, invf_row, w_heads, q_g, kv_g, x, dr, drest, wp_in)


def _adam(parts, w, m, v, *, name, tile_rows):
    n, rows, cols = parts.shape

    def body(p_ref, w_ref, m_ref, v_ref, g_ref, d_ref, nm_ref, nv_ref):
        g = p_ref[0].astype(F32)
        for s in range(1, n):
            g = g + p_ref[s].astype(F32)
        m_new = ADAM_B1 * m_ref[...] + (1.0 - ADAM_B1) * g
        v_new = ADAM_B2 * v_ref[...] + (1.0 - ADAM_B2) * (g * g)
        m_hat = m_new / (1.0 - ADAM_B1 ** ADAM_STEP)
        v_hat = v_new / (1.0 - ADAM_B2 ** ADAM_STEP)
        g_ref[...] = g
        d_ref[...] = -ADAM_LR * (m_hat / (jnp.sqrt(v_hat) + ADAM_EPS) + ADAM_WD * w_ref[...])
        nm_ref[...] = m_new
        nv_ref[...] = v_new

    flat = pl.BlockSpec((tile_rows, cols), lambda i: (i, 0))
    shape = jax.ShapeDtypeStruct((rows, cols), F32)
    return pl.pallas_call(
        body, name=name, grid=(rows // tile_rows,),
        in_specs=[pl.BlockSpec((n, tile_rows, cols), lambda i: (0, i, 0)), flat, flat, flat],
        out_specs=[flat] * 4, out_shape=[shape] * 4,
        compiler_params=_cparams(("arbitrary",)),
    )(parts, w, m, v)


SMALL_NAMES = ("q_norm_g", "kv_norm_g", "sgu_norm_g", "sgu_norm_b", "b_spatial", "ln_g", "ln_b")
SMALL_SIZES = (Q_LORA, KV_LORA, G_WIDTH, G_WIDTH, HEADS * CHUNK, D_MODEL, D_MODEL)


def _pack_small(vals, last=None):
    flat = jnp.concatenate([v.reshape(-1) for v in vals])
    pad = SMALL_LEN - flat.shape[0]
    if last is None:
        return jnp.pad(flat, (0, pad))
    return jnp.concatenate([flat, jnp.zeros((pad - 1,), F32), last.reshape(1)])


def _unpack_small(flat):
    out, at = [], 0
    for n in SMALL_SIZES:
        out.append(flat[at:at + n])
        at += n
    out[4] = out[4].reshape(HEADS, CHUNK)
    return out


UQ_SHARD = HEADS * (NOPE + ROPE) // N_DEV
HEAD_ROWS = Q_LORA + KV_LORA
MIXED_ROWS = HEAD_ROWS + CHUNK + SMALL_LEN // N_DEV // LANES


def _head_slab(w_uq_shard, w_ukv_shard):
    return jnp.concatenate([jnp.pad(w_uq_shard, ((0, 0), (0, LANES - UQ_SHARD))), w_ukv_shard])


IN_SHARD = D_IN // N_DEV


def _w_in_pieces():
    split = Q_LORA + KV_LORA
    moves = ((0, split, 0), (split, split + ROPE, KR_LO), (split + ROPE, D_IN, LANES - ROPE))
    pieces = []
    for s in range(N_DEV):
        lo, hi = s * IN_SHARD, (s + 1) * IN_SHARD
        for a, b, shift in moves:
            a, b = max(a, lo), min(b, hi)
            if a < b:
                pieces.append((s, a - lo, a + shift, b - a))
    return pieces


def _padded_w_in(shards):
    tr = TOKEN_TILE

    def body(sh_ref, o_ref):
        o_ref[...] = jnp.zeros_like(o_ref)
        for s, src, dst, width in _w_in_pieces():
            o_ref[:, dst:dst + width] = sh_ref[s, :, src:src + width]

    return pl.pallas_call(
        body, name="w_in_pad", grid=(D_MODEL // tr,),
        in_specs=[pl.BlockSpec((N_DEV, tr, IN_SHARD), lambda i: (0, i, 0))],
        out_specs=pl.BlockSpec((tr, D_IN_PAD), lambda i: (i, 0)),
        out_shape=jax.ShapeDtypeStruct((D_MODEL, D_IN_PAD), shards.dtype),
        compiler_params=_cparams(("arbitrary",)),
    )(shards)


def _w_in_shards(dwp_in):
    tr = TOKEN_TILE
    by_shard = [[p for p in _w_in_pieces() if p[0] == s] for s in range(N_DEV)]

    def body(w_ref, o_ref):
        for s, pieces in enumerate(by_shard):
            parts = [w_ref[:, dst:dst + width] for _, _, dst, width in pieces]
            o_ref[s] = parts[0] if len(parts) == 1 else jnp.concatenate(parts, axis=1)

    return pl.pallas_call(
        body, name="w_in_split", grid=(D_MODEL // tr,),
        in_specs=[pl.BlockSpec((tr, D_IN_PAD), lambda i: (i, 0))],
        out_specs=pl.BlockSpec((N_DEV, tr, IN_SHARD), lambda i: (0, i, 0)),
        out_shape=jax.ShapeDtypeStruct((N_DEV, D_MODEL, IN_SHARD), dwp_in.dtype),
        compiler_params=_cparams(("arbitrary",)),
    )(dwp_in)


def kernel(x, positions, w_in, q_norm_g, w_uq, kv_norm_g, w_ukv, sgu_norm_g, sgu_norm_b, w_spatial, b_spatial, w_out, ln_g, ln_b, loss_target, m_w_in, m_q_norm_g, m_w_uq, m_kv_norm_g, m_w_ukv, m_sgu_norm_g, m_sgu_norm_b, m_w_spatial, m_b_spatial, m_w_out, m_ln_g, m_ln_b, v_w_in, v_q_norm_g, v_w_uq, v_kv_norm_g, v_w_ukv, v_sgu_norm_g, v_sgu_norm_b, v_w_spatial, v_b_spatial, v_w_out, v_ln_g, v_ln_b):
    me = 4 * lax.axis_index("x") + 2 * lax.axis_index("y") + lax.axis_index("c")
    seq = x.shape[1]
    x2 = x.reshape(seq, D_MODEL)
    tgt2 = loss_target.reshape(seq, D_MODEL)
    pos_col = positions.reshape(seq, 1)

    w_in_shards, w_out_shards, w_heads = _gather_two_level(
        [w_in.astype(BF16), w_out.astype(BF16), _head_slab(w_uq, w_ukv).astype(BF16)],
        name="wgather")
    (loss_part, grad_x, d_in, d_heads, d_out, d_ws, d_bs_t, d_lng, d_lnb, d_sgug, d_sgub, d_qg, d_kvg) = _local_step(
        x2, tgt2, pos_col, w_in_shards, w_heads, w_out_shards.reshape(D_MODEL, D_MODEL), q_norm_g, kv_norm_g,
        sgu_norm_g, sgu_norm_b, w_spatial, b_spatial, ln_g, ln_b)

    small_part = _pack_small([d_qg, d_kvg, d_sgug, d_sgub, d_bs_t[:, :HEADS].T, d_lng, d_lnb], last=loss_part[0, :1])
    mixed = jnp.concatenate([d_heads, d_ws, small_part.reshape(N_DEV, -1, LANES)], axis=1)
    by_chip = [g.reshape((N_CHIPS, 2) + g.shape[1:])
               for g in (d_in, d_out.reshape(N_DEV, D_MODEL // N_DEV, D_MODEL), mixed)]
    from_sibling = _sibling_swap(by_chip, name="gswap")
    core = lax.axis_index("c").astype(jnp.int32).reshape(1)
    pair_sums = [_pair_sum(a, b, core, name=nm, tile_rows=tr, out_dtype=dt) for a, b, nm, tr, dt in zip(
        by_chip, from_sibling, ("gsum_in", "gsum_out", "gsum_mixed"), (TOKEN_TILE, D_MODEL // N_DEV, MIXED_ROWS),
        (BF16, BF16, F32))]
    recv_in, recv_out, recv_mixed = _chip_exchange(pair_sums, name="gexch")

    take = lambda a: lax.dynamic_index_in_dim(a, me, 0, keepdims=False)
    small_w = _pack_small([q_norm_g, kv_norm_g, sgu_norm_g, sgu_norm_b, b_spatial, ln_g, ln_b])
    small_m = _pack_small([m_q_norm_g, m_kv_norm_g, m_sgu_norm_g, m_sgu_norm_b, m_b_spatial, m_ln_g, m_ln_b])
    small_v = _pack_small([v_q_norm_g, v_kv_norm_g, v_sgu_norm_g, v_sgu_norm_b, v_b_spatial, v_ln_g, v_ln_b])
    own_mixed = lambda uq, ukv, sp, small: jnp.concatenate(
        [_head_slab(uq, ukv), take(sp), take(small.reshape(N_DEV, -1, LANES))])
    res_in = _adam(recv_in, w_in, m_w_in, v_w_in, name="adam_in", tile_rows=TOKEN_TILE)
    res_out = _adam(recv_out, w_out, m_w_out, v_w_out, name="adam_out", tile_rows=D_MODEL // N_DEV)
    res_mixed = _adam(recv_mixed, own_mixed(w_uq, w_ukv, w_spatial, small_w), own_mixed(m_w_uq, m_w_ukv, m_w_spatial, small_m),
                      own_mixed(v_w_uq, v_w_ukv, v_w_spatial, small_v), name="adam_mixed", tile_rows=MIXED_ROWS)

    rep_g, = _exchange([res_mixed[0][HEAD_ROWS:]], name="sgather", per_destination=False)
    rep_pack = lambda sp, small: jnp.concatenate(
        [sp.reshape(N_DEV, CHUNK, LANES), small.reshape(N_DEV, -1, LANES)], axis=1).reshape(-1, LANES)
    _, delta_rep, m_rep, v_rep = _adam(rep_g.reshape(1, N_DEV * REP_ROWS, LANES), rep_pack(w_spatial, small_w),
                                       rep_pack(m_w_spatial, small_m), rep_pack(v_w_spatial, small_v),
                                       name="adam_rep", tile_rows=N_DEV * REP_ROWS)

    def rep_unpack(a):
        a = a.reshape(N_DEV, REP_ROWS, LANES)
        small = _unpack_small(a[:, CHUNK:].reshape(-1))
        return [small[0], small[1], small[2], small[3], a[:, :CHUNK], small[4], small[5], small[6]]

    def ordered(which, rep):
        r_qg, r_kvg, r_sg, r_sb, r_ws, r_bs, r_lg, r_lb = rep_unpack(rep)
        heads = res_mixed[which]
        return [res_in[which], r_qg, heads[:Q_LORA, :UQ_SHARD], r_kvg, heads[Q_LORA:HEAD_ROWS], r_sg, r_sb, r_ws, r_bs,
                res_out[which], r_lg, r_lb]

    loss = rep_g[N_DEV - 1, REP_ROWS - 1, LANES - 1]
    outs = [loss, grad_x.reshape(x.shape)]
    outs += ordered(0, rep_g.reshape(-1, LANES))
    outs += ordered(1, delta_rep)
    outs += ordered(2, m_rep)
    outs += ordered(3, v_rep)
    return tuple(outs)


def _local_step(x2, tgt2, pos_col, w_in_shards, w_heads, w_out_full, q_norm_g, kv_norm_g, sgu_norm_g, sgu_norm_b,
                w_spatial, b_spatial, ln_g, ln_b):
    wp_in = _padded_w_in(w_in_shards)

    half = jnp.arange(HALF, dtype=F32)
    inv_freq = 1.0 / (ROPE_THETA ** (half / HALF))
    invf_row = jnp.concatenate([jnp.zeros((KR_LO,), F32), inv_freq, inv_freq,
                                jnp.zeros((LANES - KR_LO - ROPE,), F32)]).reshape(1, LANES)
    tri = jnp.tril(jnp.ones((CHUNK, CHUNK), dtype=bool))
    ws_low = jnp.where(tri[None], w_spatial, 0.0).astype(BF16)
    ws_low_t = ws_low.transpose(0, 2, 1)
    bsp = jnp.repeat(b_spatial.T, G_HEAD_DIM, axis=1)
    row = lambda a: a.reshape(1, -1)

    proj, q, k, v, vt = _fwd_proj(x2, pos_col, invf_row, wp_in, w_heads, row(q_norm_g), row(kv_norm_g))
    o, lse_row = _attn_fwd(q, k, vt)
    (dr, do, d_row, drest, d_out, d_ws, d_bs_t, d_lng, d_lnb, d_sgug, d_sgub, loss_part) = _mid(
        x2, tgt2, proj, o, w_out_full, ws_low, ws_low_t, bsp, row(sgu_norm_g), row(sgu_norm_b), row(ln_g), row(ln_b))
    dqt, dk, dv = _attn_bwd(q, k, v, do, lse_row, d_row)
    grad_x, dwp_in, d_heads, d_qg, d_kvg = _bwd_tail(dqt, dk, dv, proj, pos_col, invf_row, w_heads, row(q_norm_g),
                                                      row(kv_norm_g), x2, dr, drest, wp_in)
    return (loss_part, grad_x, _w_in_shards(dwp_in), d_heads, d_out, d_ws, d_bs_t, d_lng, d_lnb, d_sgug, d_sgub,
            d_qg, d_kvg)
```

```python
import functools
import math

import jax
import jax.numpy as jnp
from jax import lax
from jax.experimental import pallas as pl
from jax.experimental.pallas import tpu as pltpu

F32 = jnp.float32
BF16 = jnp.bfloat16

N_DEV = 8
D_MODEL = 1024
HEADS = 8
NOPE = 64
ROPE = 32
HALF = ROPE // 2
VDIM = 64
Q_LORA = 256
KV_LORA = 128
G_WIDTH = 512
G_HEAD_DIM = 64
CHUNK = 128
HEAD_PAD = 128
D_IN = 2464
D_IN_PAD = 2560
KR_LO = NOPE
SUM_ROW = NOPE - 1
LIVE_ROWS = slice(NOPE - 16, HEAD_PAD)
ROPE_THETA = 10000.0
DN_ALPHA = 2.0 ** 0.25
EPS = 1e-5
ATTN_SCALE = 1.0 / math.sqrt(NOPE + ROPE)
ADAM_LR, ADAM_B1, ADAM_B2, ADAM_EPS, ADAM_WD, ADAM_STEP = 0.001, 0.9, 0.999, 1e-08, 0.01, 10

LANES = 128
REP_ROWS = 136
SMALL_LEN = 8192
VMEM_LIMIT = 56 * 1024 * 1024
ATTN_BWD_VMEM_LIMIT = 61 * 1024 * 1024
BWD_TAIL_VMEM_LIMIT = 61 * 1024 * 1024

TOKEN_TILE = 256
PROJ_TILE = 512
ATTN_FWD_WIDE = 2048
ATTN_BWD_WIDE = 2048
ATTN_NARROW = 512
SOFTMAX_ROWS = 512
LOG2E = 1.4426950408889634
LN2 = 0.6931471805599453
Q_PRESCALE = ATTN_SCALE * LOG2E


def _cparams(sem=None, vmem_limit=VMEM_LIMIT):
    return pltpu.CompilerParams(dimension_semantics=sem, vmem_limit_bytes=vmem_limit)


def _dot(a, b):
    return jnp.dot(a, b, preferred_element_type=F32)


def _dot_nt(a, b):
    return lax.dot_general(a, b, (((1,), (1,)), ((), ())), preferred_element_type=F32)


def _dot_tn(a, b):
    return lax.dot_general(a, b, (((0,), (0,)), ((), ())), preferred_element_type=F32)


def _as_row(col):
    return jnp.transpose(jnp.broadcast_to(col, (col.shape[0], LANES)))[0:1, :]


def _sigmoid(z):
    return 1.0 / (1.0 + jnp.exp(-z))


def _gelu(x):
    return 0.5 * x * (1.0 + lax.erf(x * 0.7071067811865476))


def _gelu_grad(x):
    cdf = 0.5 * (1.0 + lax.erf(x * 0.7071067811865476))
    return cdf + x * jnp.exp(-0.5 * x * x) * 0.3989422804014327


def _exchange(srcs, *, name, per_destination):
    n = len(srcs)
    slab_shapes = [s.shape[1:] if per_destination else s.shape for s in srcs]

    def body(*refs):
        src_refs, out_refs = refs[:n], refs[n:2 * n]
        send_sems, recv_sems, local_sems = refs[2 * n:]
        x, y, c = lax.axis_index("x"), lax.axis_index("y"), lax.axis_index("c")
        me = 4 * x + 2 * y + c

        def slab_for(t, dest):
            return src_refs[t].at[dest] if per_destination else src_refs[t]

        mine = [pltpu.make_async_copy(slab_for(t, me), out_refs[t].at[me], local_sems.at[t]) for t in range(n)]
        for cp in mine:
            cp.start()
        sends, arrivals = [], []
        for k in (6, 7, 4, 5, 2, 3, 1):
            px = 1 - x if k & 4 else x
            py = 1 - y if k & 2 else y
            pc = 1 - c if k & 1 else c
            peer = 4 * px + 2 * py + pc
            for t in range(n):
                sem = (k - 1) * n + t
                cp = pltpu.make_async_remote_copy(
                    src_ref=slab_for(t, peer), dst_ref=out_refs[t].at[me],
                    send_sem=send_sems.at[sem], recv_sem=recv_sems.at[sem],
                    device_id=(px, py, pc), device_id_type=pl.DeviceIdType.MESH)
                cp.start()
                sends.append(cp)
                arrivals.append(pltpu.make_async_remote_copy(
                    src_ref=slab_for(t, peer), dst_ref=out_refs[t].at[peer],
                    send_sem=send_sems.at[sem], recv_sem=recv_sems.at[sem],
                    device_id=(x, y, c), device_id_type=pl.DeviceIdType.MESH))
        for cp in arrivals:
            cp.wait_recv()
        for cp in sends:
            cp.wait_send()
        for cp in mine:
            cp.wait()

    hbm = pl.BlockSpec(memory_space=pl.ANY)
    return pl.pallas_call(
        body, name=name,
        out_shape=[jax.ShapeDtypeStruct((N_DEV,) + tuple(shape), s.dtype) for shape, s in zip(slab_shapes, srcs)],
        in_specs=[hbm] * n, out_specs=[hbm] * n,
        scratch_shapes=[pltpu.SemaphoreType.DMA(((N_DEV - 1) * n,)), pltpu.SemaphoreType.DMA(((N_DEV - 1) * n,)),
                        pltpu.SemaphoreType.DMA((n,))],
    )(*srcs)


def _gather_two_level(srcs, *, name):
    n = len(srcs)

    def body(*refs):
        src_refs, out_refs = refs[:n], refs[n:2 * n]
        send_sems, recv_sems, local_sems = refs[2 * n:]
        x, y, c = lax.axis_index("x"), lax.axis_index("y"), lax.axis_index("c")
        me, sibling = (x, y, c), (x, y, 1 - c)
        chips = [(1 - x, 1 - y), (1 - x, y), (x, 1 - y)]
        index = lambda px, py, pc: 4 * px + 2 * py + pc

        def copy(k, t, block, to, src=None):
            place = out_refs[t].at[index(*block)]
            return pltpu.make_async_remote_copy(
                src_ref=place if src is None else src, dst_ref=place,
                send_sem=send_sems.at[k * n + t], recv_sem=recv_sems.at[k * n + t],
                device_id=to, device_id_type=pl.DeviceIdType.MESH)

        mine = [pltpu.make_async_copy(src_refs[t], out_refs[t].at[index(*me)], local_sems.at[t]) for t in range(n)]
        for cp in mine:
            cp.start()
        first = [copy(1 + j, t, me, (*chip, c), src=src_refs[t]) for j, chip in enumerate(chips) for t in range(n)]
        first += [copy(0, t, me, sibling, src=src_refs[t]) for t in range(n)]
        for cp in first:
            cp.start()
        passed = []
        for j, chip in enumerate(chips):
            for t in range(n):
                copy(1 + j, t, (*chip, c), me).wait_recv()
                cp = copy(4 + j, t, (*chip, c), sibling)
                cp.start()
                passed.append(cp)
        for t in range(n):
            copy(0, t, sibling, me).wait_recv()
        for j, chip in enumerate(chips):
            for t in range(n):
                copy(4 + j, t, (*chip, 1 - c), me).wait_recv()
        for cp in first + passed:
            cp.wait_send()
        for cp in mine:
            cp.wait()

    hbm = pl.BlockSpec(memory_space=pl.ANY)
    return pl.pallas_call(
        body, name=name,
        out_shape=[jax.ShapeDtypeStruct((N_DEV,) + s.shape, s.dtype) for s in srcs],
        in_specs=[hbm] * n, out_specs=[hbm] * n,
        scratch_shapes=[pltpu.SemaphoreType.DMA((7 * n,)), pltpu.SemaphoreType.DMA((7 * n,)),
                        pltpu.SemaphoreType.DMA((n,))],
    )(*srcs)


N_CHIPS = N_DEV // 2


def _sibling_swap(srcs, *, name):
    n = len(srcs)

    def body(*refs):
        src_refs, out_refs = refs[:n], refs[n:2 * n]
        send_sems, recv_sems = refs[2 * n:]
        x, y, c = lax.axis_index("x"), lax.axis_index("y"), lax.axis_index("c")
        sends = []
        for chip in range(N_CHIPS):
            for t in range(n):
                cp = pltpu.make_async_remote_copy(
                    src_ref=src_refs[t].at[chip, 1 - c], dst_ref=out_refs[t].at[chip],
                    send_sem=send_sems.at[chip * n + t], recv_sem=recv_sems.at[chip * n + t],
                    device_id=(x, y, 1 - c), device_id_type=pl.DeviceIdType.MESH)
                cp.start()
                sends.append(cp)
        for cp in sends:
            cp.wait_recv()
        for cp in sends:
            cp.wait_send()

    hbm = pl.BlockSpec(memory_space=pl.ANY)
    return pl.pallas_call(
        body, name=name,
        out_shape=[jax.ShapeDtypeStruct((N_CHIPS,) + s.shape[2:], s.dtype) for s in srcs],
        in_specs=[hbm] * n, out_specs=[hbm] * n,
        scratch_shapes=[pltpu.SemaphoreType.DMA((N_CHIPS * n,)), pltpu.SemaphoreType.DMA((N_CHIPS * n,))],
    )(*srcs)


def _pair_sum(mine, theirs, core, *, name, tile_rows, out_dtype):
    _, _, rows, cols = mine.shape

    def body(core_ref, a_ref, b_ref, o_ref):
        o_ref[...] = (a_ref[0] + b_ref[...]).astype(out_dtype)

    return pl.pallas_call(
        body, name=name,
        grid_spec=pltpu.PrefetchScalarGridSpec(
            num_scalar_prefetch=1, grid=(N_CHIPS, rows // tile_rows),
            in_specs=[pl.BlockSpec((1, 1, tile_rows, cols), lambda q, r, core_ref: (q, core_ref[0], r, 0)),
                      pl.BlockSpec((1, tile_rows, cols), lambda q, r, core_ref: (q, r, 0))],
            out_specs=pl.BlockSpec((1, tile_rows, cols), lambda q, r, core_ref: (q, r, 0))),
        out_shape=jax.ShapeDtypeStruct((N_CHIPS, rows, cols), out_dtype),
        compiler_params=_cparams(("arbitrary", "arbitrary")),
    )(core, mine, theirs)


def _chip_exchange(srcs, *, name):
    n = len(srcs)

    def body(*refs):
        src_refs, out_refs = refs[:n], refs[n:2 * n]
        send_sems, recv_sems, local_sems = refs[2 * n:]
        x, y, c = lax.axis_index("x"), lax.axis_index("y"), lax.axis_index("c")
        my_chip = 2 * x + y
        mine = [pltpu.make_async_copy(src_refs[t].at[my_chip], out_refs[t].at[my_chip], local_sems.at[t])
                for t in range(n)]
        for cp in mine:
            cp.start()
        sends, arrivals = [], []
        for k in (3, 2, 1):
            px = 1 - x if k & 2 else x
            py = 1 - y if k & 1 else y
            peer_chip = 2 * px + py
            for t in range(n):
                sem = (k - 1) * n + t
                cp = pltpu.make_async_remote_copy(
                    src_ref=src_refs[t].at[peer_chip], dst_ref=out_refs[t].at[my_chip],
                    send_sem=send_sems.at[sem], recv_sem=recv_sems.at[sem],
                    device_id=(px, py, c), device_id_type=pl.DeviceIdType.MESH)
                cp.start()
                sends.append(cp)
                arrivals.append(pltpu.make_async_remote_copy(
                    src_ref=src_refs[t].at[peer_chip], dst_ref=out_refs[t].at[peer_chip],
                    send_sem=send_sems.at[sem], recv_sem=recv_sems.at[sem],
                    device_id=(x, y, c), device_id_type=pl.DeviceIdType.MESH))
        for cp in arrivals:
            cp.wait_recv()
        for cp in sends:
            cp.wait_send()
        for cp in mine:
            cp.wait()

    hbm = pl.BlockSpec(memory_space=pl.ANY)
    return pl.pallas_call(
        body, name=name,
        out_shape=[jax.ShapeDtypeStruct(s.shape, s.dtype) for s in srcs],
        in_specs=[hbm] * n, out_specs=[hbm] * n,
        scratch_shapes=[pltpu.SemaphoreType.DMA((3 * n,)), pltpu.SemaphoreType.DMA((3 * n,)),
                        pltpu.SemaphoreType.DMA((n,))],
    )(*srcs)


def _rope_tables(pos_col, invf_row):
    ang = pos_col.astype(F32) * invf_row
    lane = lax.broadcasted_iota(jnp.int32, ang.shape, 1)
    cos, sin = jnp.cos(ang), jnp.sin(ang)
    first = (lane >= KR_LO) & (lane < KR_LO + HALF)
    second = (lane >= KR_LO + HALF) & (lane < KR_LO + ROPE)
    return cos, jnp.where(first, sin, 0.0), jnp.where(second, sin, 0.0)


def _rope(t, cos, sin_first, sin_second, sign):
    up = pltpu.roll(t, LANES - HALF, 1)
    down = pltpu.roll(t, HALF, 1)
    return t * cos - sign * (up * sin_first) + sign * (down * sin_second)


def _fwd_proj(x, pos_col, invf_row, wp_in, w_heads, q_g, kv_g):
    t = x.shape[0]
    tm = PROJ_TILE

    def body(x_ref, pos_ref, invf_ref, win_ref, wh_ref, qg_ref, kvg_ref,
             proj_ref, q_ref, k_ref, v_ref, vt_ref):
        proj = _dot(x_ref[...].astype(BF16), win_ref[...])
        proj_ref[...] = proj
        c_q = proj[:, :Q_LORA]
        c_kv = proj[:, Q_LORA:Q_LORA + KV_LORA]
        kr_raw = proj[:, Q_LORA + KV_LORA:Q_LORA + KV_LORA + LANES]
        cqn = (c_q * lax.rsqrt(jnp.mean(c_q * c_q, axis=-1, keepdims=True) + EPS) * qg_ref[...]).astype(BF16)
        ckvn = (c_kv * lax.rsqrt(jnp.mean(c_kv * c_kv, axis=-1, keepdims=True) + EPS) * kvg_ref[...]).astype(BF16)
        cos, s1, s2 = _rope_tables(pos_ref[...], invf_ref[...])
        kr = _rope(kr_raw, cos, s1, s2, 1.0)
        lane = lax.broadcasted_iota(jnp.int32, (tm, HEAD_PAD), 1)
        q_all = _dot(cqn, jnp.concatenate([wh_ref[h, :Q_LORA, :] for h in range(HEADS)], axis=1))
        kv_all = _dot(ckvn, jnp.concatenate([wh_ref[h, Q_LORA:, :] for h in range(HEADS)], axis=1))
        for h in range(HEADS):
            q_h = q_all[:, h * HEAD_PAD:(h + 1) * HEAD_PAD]
            kv_h = kv_all[:, h * HEAD_PAD:(h + 1) * HEAD_PAD]
            q_ref[h] = (_rope(q_h, cos, s1, s2, 1.0) * Q_PRESCALE).astype(BF16)
            k_ref[h] = jnp.where(lane < NOPE, kv_h, kr).astype(BF16)
            v_ref[h] = kv_h.astype(BF16)
            vt_ref[h] = jnp.transpose(jnp.where(lane == SUM_ROW, 1.0, kv_h)).astype(BF16)

    full = lambda a: pl.BlockSpec(a.shape, lambda i: (0,) * a.ndim)
    head_spec = pl.BlockSpec((HEADS, tm, HEAD_PAD), lambda i: (0, i, 0))
    head_shape = jax.ShapeDtypeStruct((HEADS, t, HEAD_PAD), BF16)
    return pl.pallas_call(
        body, name="fwd_proj", grid=(t // tm,),
        in_specs=[pl.BlockSpec((tm, D_MODEL), lambda i: (i, 0)), pl.BlockSpec((tm, 1), lambda i: (i, 0)),
                  full(invf_row), full(wp_in), full(w_heads), full(q_g), full(kv_g)],
        out_specs=[pl.BlockSpec((tm, D_IN_PAD), lambda i: (i, 0)), head_spec, head_spec, head_spec,
                   pl.BlockSpec((HEADS, HEAD_PAD, tm), lambda i: (0, 0, i))],
        out_shape=[jax.ShapeDtypeStruct((t, D_IN_PAD), F32), head_shape, head_shape, head_shape,
                   jax.ShapeDtypeStruct((HEADS, HEAD_PAD, t), BF16)],
        compiler_params=_cparams(("arbitrary",)),
    )(x, pos_col, invf_row, wp_in, w_heads, q_g, kv_g)


def _attn_fwd(q, k, vt):
    t = q.shape[1]
    bq, bk = ATTN_FWD_WIDE, ATTN_NARROW
    n_diag = bq // bk
    chunk = SOFTMAX_ROWS

    def body(q_ref, k_ref, vt_ref, o_ref, lse_ref, s0, s1, p0, p1, x0, x1, m_scr, a_scr, acc_scr):
        i = pl.program_id(1)
        at = lambda j: pl.ds(pl.multiple_of(j * bk, bk), bk)

        def exp_pass(s_in, block_max, p_out, diagonal=False, cols=slice(None)):
            width = bq if cols == slice(None) else cols.stop - cols.start

            def load(r):
                s = s_in[r:r + chunk, cols]
                if diagonal:
                    key = lax.broadcasted_iota(jnp.int32, (chunk, width), 0) + r
                    qry = lax.broadcasted_iota(jnp.int32, (chunk, width), 1)
                    s = jnp.where(qry >= key, s, -jnp.inf)
                return s

            if diagonal:
                block_max = jnp.max(load(0), axis=0, keepdims=True)
                for r in range(chunk, bk, chunk):
                    block_max = jnp.maximum(block_max, jnp.max(load(r), axis=0, keepdims=True))
            m_old = m_scr[:, cols]
            m_new = jnp.maximum(m_old, block_max)
            alpha = jnp.exp2(m_old - m_new)
            for r in range(0, bk, chunk):
                p_out[r:r + chunk, cols] = jnp.exp2(load(r) - m_new).astype(BF16)
            m_scr[:, cols] = m_new
            return alpha

        def scores(j, s_out, x_out):
            s = _dot_nt(k_ref[0, at(j), :], q_ref[0])
            s_out[...] = s
            x_out[...] = jnp.max(s, axis=0, keepdims=True)

        def value_product(j, p_in):
            return _dot(vt_ref[0, LIVE_ROWS, at(j)], p_in[...])

        def one_pass(j, s_in, x_in, s_out, x_out, p_prev, p_cur):
            scores(j + 1, s_out, x_out)
            acc_scr[...] = a_scr[...] * acc_scr[...] + value_product(jnp.maximum(j - 1, 0), p_prev)
            a_scr[...] = exp_pass(s_in, x_in[...], p_cur)

        scores(0, s0, x0)
        p1[...] = jnp.zeros_like(p1)
        a_scr[...] = jnp.ones_like(a_scr)
        m_scr[...] = jnp.full(m_scr.shape, -jnp.inf, F32)
        acc_scr[...] = jnp.zeros_like(acc_scr)

        def two_passes(n, _):
            one_pass(2 * n, s0, x0, s1, x1, p1, p0)
            one_pass(2 * n + 1, s1, x1, s0, x0, p0, p1)
            return 0

        lax.fori_loop(0, (n_diag // 2) * i, two_passes, 0)
        d = n_diag * i
        alpha, p_prev, cols = a_scr[...], p1, slice(0, bq)
        for u in range(n_diag + 1):
            s_in, s_next, p_cur = (s0, s1, p0) if u % 2 == 0 else (s1, s0, p1)
            if u + 1 < n_diag:
                ahead = slice((u + 1) * bk, bq)
                s_next[:, ahead] = _dot_nt(k_ref[0, at(d + u + 1), :], q_ref[0, ahead, :])
            acc_scr[:, cols] = alpha * acc_scr[:, cols] + _dot(vt_ref[0, LIVE_ROWS, at(jnp.maximum(d + u - 1, 0))],
                                                               p_prev[:, cols])
            if u < n_diag:
                cols = slice(u * bk, bq)
                alpha = exp_pass(s_in, None, p_cur, diagonal=True, cols=cols)
                p_prev = p_cur
        denom = acc_scr[SUM_ROW - LIVE_ROWS.start:NOPE - LIVE_ROWS.start, :]
        o = jnp.transpose(acc_scr[NOPE - LIVE_ROWS.start:, :] / denom)
        o_ref[0] = jnp.concatenate([jnp.zeros_like(o), o], axis=1)
        lse_ref[0] = m_scr[...] + jnp.log2(denom)

    tile = lambda dtype: pltpu.VMEM((bk, bq), dtype)
    stat = pltpu.VMEM((1, bq), F32)
    return pl.pallas_call(
        body, name="attn_fwd", grid=(HEADS, t // bq),
        in_specs=[pl.BlockSpec((1, bq, HEAD_PAD), lambda h, i: (h, i, 0)),
                  pl.BlockSpec((1, t, HEAD_PAD), lambda h, i: (h, 0, 0)),
                  pl.BlockSpec((1, HEAD_PAD, t), lambda h, i: (h, 0, 0))],
        out_specs=[pl.BlockSpec((1, bq, HEAD_PAD), lambda h, i: (h, i, 0)),
                   pl.BlockSpec((1, 1, bq), lambda h, i: (h, 0, i))],
        out_shape=[jax.ShapeDtypeStruct((HEADS, t, HEAD_PAD), F32), jax.ShapeDtypeStruct((HEADS, 1, t), F32)],
        scratch_shapes=[tile(F32), tile(F32), tile(BF16), tile(BF16), stat, stat, stat, stat,
                        pltpu.VMEM((HEAD_PAD - LIVE_ROWS.start, bq), F32)],
        compiler_params=_cparams(("arbitrary", "arbitrary")),
    )(q, k, vt)


def _mid(x, target, proj, ol, w_out, ws_low, ws_low_t, bsp, sgu_g, sgu_b, ln_g, ln_b):
    t = x.shape[0]
    tm = TOKEN_TILE
    n_steps = t // tm

    def body(x_ref, tgt_ref, za_ref, u_ref, v_ref, zb_ref, ol_ref, wout_ref, ws_ref, wst_ref, bsp_ref,
             sg_ref, sb_ref, lg_ref, lb_ref,
             dr_ref, do_ref, drow_ref, drest_ref, dwout_ref, dws_ref, dbs_ref, dlg_ref, dlb_ref, dsg_ref, dsb_ref,
             loss_ref, dbsp_acc):
        step = pl.program_id(0)

        @pl.when(step == 0)
        def _():
            dwout_ref[...] = jnp.zeros_like(dwout_ref)
            dws_ref[...] = jnp.zeros_like(dws_ref)
            dbs_ref[...] = jnp.zeros_like(dbs_ref)
            dlg_ref[...] = jnp.zeros_like(dlg_ref)
            dlb_ref[...] = jnp.zeros_like(dlb_ref)
            dsg_ref[...] = jnp.zeros_like(dsg_ref)
            dsb_ref[...] = jnp.zeros_like(dsb_ref)
            loss_ref[...] = jnp.zeros_like(loss_ref)
            dbsp_acc[...] = jnp.zeros_like(dbsp_acc)

        n_chunks = tm // CHUNK
        groups = G_WIDTH // LANES

        def side_by_side(a):
            return [jnp.concatenate([a[c * CHUNK:(c + 1) * CHUNK, g * LANES:(g + 1) * LANES] for c in range(n_chunks)],
                                    axis=1) for g in range(groups)]

        def by_chunk(wide):
            return jnp.concatenate([jnp.concatenate([wide[g][:, c * LANES:(c + 1) * LANES] for g in range(groups)], axis=1)
                                    for c in range(n_chunks)], axis=0)

        def own_lanes(h):
            lane = lax.broadcasted_iota(jnp.int32, (CHUNK, n_chunks * LANES), 1)
            return (lane % LANES) // G_HEAD_DIM == h % 2

        def spatial(w_ref, wide):
            return [sum(jnp.where(own_lanes(h), _dot(w_ref[h], wide[g]), 0.0) for h in (2 * g, 2 * g + 1))
                    for g in range(groups)]

        attn = jnp.concatenate([ol_ref[h][:, NOPE:] for h in range(HEADS)], axis=-1)
        za = za_ref[...]
        sig_a = _sigmoid(za)
        silu_a = za * sig_a
        out_a = attn * silu_a
        u = u_ref[...]
        ug = _gelu(u)
        vpre = v_ref[...]
        gv = _gelu(vpre)
        mu_v = jnp.mean(gv, axis=-1, keepdims=True)
        cen_v = gv - mu_v
        rstd_v = lax.rsqrt(jnp.mean(cen_v * cen_v, axis=-1, keepdims=True) + EPS)
        vhat = cen_v * rstd_v
        vg = vhat * sg_ref[...] + sb_ref[...]
        vg_b = vg.astype(BF16)
        sv = by_chunk(spatial(ws_ref, side_by_side(vg_b))) + jnp.tile(bsp_ref[...], (n_chunks, 1))
        sgu = ug * sv
        zb = zb_ref[...]
        sig_b = _sigmoid(zb)
        silu_b = zb * sig_b
        out_b = sgu * silu_b
        merged = jnp.concatenate([out_a, out_b], axis=-1).astype(BF16)
        r = DN_ALPHA * x_ref[...] + _dot(merged, wout_ref[...])
        mu = jnp.mean(r, axis=-1, keepdims=True)
        cen = r - mu
        rstd = lax.rsqrt(jnp.mean(cen * cen, axis=-1, keepdims=True) + EPS)
        xhat = cen * rstd
        hout = xhat * lg_ref[...] + lb_ref[...]
        err = hout - tgt_ref[...]
        row_loss = jnp.mean(err * err, axis=-1, keepdims=True)
        loss_ref[...] += jnp.broadcast_to(0.5 * jnp.sum(row_loss, axis=0, keepdims=True), loss_ref.shape)

        dh = err * (1.0 / D_MODEL)
        dlg_ref[...] += jnp.sum(dh * xhat, axis=0, keepdims=True)
        dlb_ref[...] += jnp.sum(dh, axis=0, keepdims=True)
        dxhat = dh * lg_ref[...]
        dr = rstd * (dxhat - jnp.mean(dxhat, axis=-1, keepdims=True)
                     - xhat * jnp.mean(dxhat * xhat, axis=-1, keepdims=True))
        dr_ref[...] = dr
        dr_b = dr.astype(BF16)
        dwout_ref[...] += _dot_tn(merged, dr_b)
        dmerged = _dot_nt(dr_b, wout_ref[...])
        d_out_a = dmerged[:, :G_WIDTH]
        d_out_b = dmerged[:, G_WIDTH:]
        dattn = d_out_a * silu_a
        for h in range(HEADS):
            do_h = dattn[:, h * VDIM:(h + 1) * VDIM]
            do_ref[h] = jnp.concatenate([jnp.zeros((tm, NOPE), F32), do_h], axis=-1).astype(BF16)
        feature = lax.broadcasted_iota(jnp.int32, (G_WIDTH, LANES), 0) // VDIM
        column = lax.broadcasted_iota(jnp.int32, (G_WIDTH, LANES), 1)
        head_sums = jnp.dot(dattn * attn, jnp.where(feature == column, 1.0, 0.0).astype(F32),
                            preferred_element_type=F32, precision=lax.Precision.HIGHEST)
        dsums_t = jnp.transpose(head_sums)
        for h in range(HEADS):
            drow_ref[h] = dsums_t[h:h + 1, :]
        dza = d_out_a * attn * (sig_a * (1.0 + za * (1.0 - sig_a)))
        dsgu = d_out_b * silu_b
        dzb = d_out_b * sgu * (sig_b * (1.0 + zb * (1.0 - sig_b)))
        du = dsgu * sv * _gelu_grad(u)
        dsv = dsgu * ug
        dsv_b = dsv.astype(BF16)
        for cix in range(n_chunks):
            dbsp_acc[...] += dsv[cix * CHUNK:(cix + 1) * CHUNK, :]
        dsv_wide, vg_wide = side_by_side(dsv_b), side_by_side(vg_b)
        dvg = by_chunk(spatial(wst_ref, dsv_wide))
        for h in range(HEADS):
            mine = jnp.where(own_lanes(h), dsv_wide[h // 2], jnp.zeros_like(dsv_wide[h // 2]))
            dws_ref[h] += _dot_nt(mine, vg_wide[h // 2])
        dsg_ref[...] += jnp.sum(dvg * vhat, axis=0, keepdims=True)
        dsb_ref[...] += jnp.sum(dvg, axis=0, keepdims=True)
        dvhat = dvg * sg_ref[...]
        dgv = rstd_v * (dvhat - jnp.mean(dvhat, axis=-1, keepdims=True)
                        - vhat * jnp.mean(dvhat * vhat, axis=-1, keepdims=True))
        dv = dgv * _gelu_grad(vpre)
        drest_ref[...] = jnp.concatenate([dza, du, dv, dzb], axis=-1).astype(BF16)

        @pl.when(step == n_steps - 1)
        def _():
            tri = (lax.broadcasted_iota(jnp.int32, (CHUNK, CHUNK), 0)
                   >= lax.broadcasted_iota(jnp.int32, (CHUNK, CHUNK), 1))
            for h in range(HEADS):
                dws_ref[h] = jnp.where(tri, dws_ref[h], 0.0)
            tot = dbsp_acc[...]
            lane = lax.broadcasted_iota(jnp.int32, (CHUNK, LANES), 1)
            dbs = jnp.zeros((CHUNK, LANES), F32)
            for h in range(HEADS):
                head_sum = jnp.sum(tot[:, h * G_HEAD_DIM:(h + 1) * G_HEAD_DIM], axis=-1, keepdims=True)
                dbs = jnp.where(lane == h, head_sum, dbs)
            dbs_ref[...] = dbs

    full = lambda a: pl.BlockSpec(a.shape, lambda i: (0,) * a.ndim)
    tile = lambda w, j=0: pl.BlockSpec((tm, w), lambda i, j=j: (i, j))
    heads = pl.BlockSpec((HEADS, tm, HEAD_PAD), lambda i: (0, i, 0))
    acc = lambda shape: (pl.BlockSpec(shape, lambda i: (0,) * len(shape)), jax.ShapeDtypeStruct(shape, F32))
    accs = [acc((D_MODEL, D_MODEL)), acc((HEADS, CHUNK, CHUNK)), acc((CHUNK, LANES)), acc((1, D_MODEL)),
            acc((1, D_MODEL)), acc((1, G_WIDTH)), acc((1, G_WIDTH)), acc((1, LANES))]
    return pl.pallas_call(
        body, name="mid", grid=(n_steps,),
        in_specs=[tile(D_MODEL), tile(D_MODEL), tile(G_WIDTH, 1), tile(G_WIDTH, 2), tile(G_WIDTH, 3), tile(G_WIDTH, 4),
                  heads, full(w_out), full(ws_low), full(ws_low_t), full(bsp), full(sgu_g), full(sgu_b),
                  full(ln_g), full(ln_b)],
        out_specs=[tile(D_MODEL), heads, pl.BlockSpec((HEADS, 1, tm), lambda i: (0, 0, i)), tile(4 * G_WIDTH)]
        + [a[0] for a in accs],
        out_shape=[jax.ShapeDtypeStruct((t, D_MODEL), F32), jax.ShapeDtypeStruct((HEADS, t, HEAD_PAD), BF16),
                   jax.ShapeDtypeStruct((HEADS, 1, t), F32), jax.ShapeDtypeStruct((t, 4 * G_WIDTH), BF16)]
        + [a[1] for a in accs],
        scratch_shapes=[pltpu.VMEM((CHUNK, G_WIDTH), F32)],
        compiler_params=_cparams(("arbitrary",)),
    )(x, target, proj, proj, proj, proj, ol, w_out, ws_low, ws_low_t, bsp, sgu_g, sgu_b, ln_g, ln_b)


def _attn_bwd(q, k, v, do, lse_row, d_row):
    t = q.shape[1]
    bk, bq = ATTN_BWD_WIDE, ATTN_NARROW
    n_diag = bk // bq
    half = bq // 2
    last = t // bq - 1
    chunk = SOFTMAX_ROWS

    def body(q_ref, k_ref, v_ref, do_ref, lse_ref, drow_ref, dqt_ref, dk_ref, dv_ref,
             s0, s1, e0, e1, p0, p1, g0, g1, kt_scr):
        j = pl.program_id(1)
        at = lambda i: pl.ds(pl.multiple_of(i * bq, bq), bq)

        @pl.when(j == 0)
        def _():
            dqt_ref[...] = jnp.zeros_like(dqt_ref)

        kt_scr[...] = jnp.transpose(k_ref[0].astype(F32)).astype(BF16)
        dk_ref[...] = jnp.zeros_like(dk_ref)
        dv_ref[...] = jnp.zeros_like(dv_ref)

        whole_tile = ((slice(0, bk), slice(0, bq)),)

        def queries(i, lanes):
            return pl.ds(pl.multiple_of(i * bq + lanes.start, half), lanes.stop - lanes.start)

        def products(i, s_out, e_out, areas=whole_tile):
            i = jnp.minimum(i, last)
            for keys, lanes in areas:
                s_out[keys, lanes] = _dot_nt(k_ref[0, keys, :], q_ref[0, queries(i, lanes), :])
                e_out[keys, lanes] = _dot_nt(v_ref[0, keys, :], do_ref[0, queries(i, lanes), :])

        def gradients(i, p_in, g_in, areas=whole_tile):
            for keys, lanes in areas:
                dv_ref[0, keys, :] += _dot(p_in[keys, lanes], do_ref[0, queries(i, lanes), :])
                dk_ref[0, keys, :] += _dot(g_in[keys, lanes], q_ref[0, queries(i, lanes), :])
                dqt_ref[0, :, queries(i, lanes)] += _dot(kt_scr[:, keys], g_in[keys, lanes])

        def elementwise(i, s_in, e_in, p_out, g_out, qry0=None, areas=whole_tile):
            for keys, lanes in areas:
                width = lanes.stop - lanes.start
                step = chunk if qry0 is None else half
                lse = lse_ref[0, :, queries(i, lanes)]
                dsum = drow_ref[0, :, queries(i, lanes)]
                for r in range(keys.start, keys.stop, step):
                    p = jnp.exp2(s_in[r:r + step, lanes] - lse)
                    if qry0 is not None:
                        key = lax.broadcasted_iota(jnp.int32, (step, width), 0) + r
                        qry = lax.broadcasted_iota(jnp.int32, (step, width), 1) + (qry0 + lanes.start)
                        p = jnp.where(qry >= key, p, 0.0)
                    p_out[r:r + step, lanes] = p.astype(BF16)
                    g_out[r:r + step, lanes] = (p * (e_in[r:r + step, lanes] - dsum)).astype(BF16)

        def one_pass(i, s_in, e_in, s_out, e_out, p_prev, g_prev, p_cur, g_cur):
            products(i + 1, s_out, e_out)
            gradients(i - 1, p_prev, g_prev)
            elementwise(i, s_in, e_in, p_cur, g_cur)

        first = n_diag * j

        def areas_of(u):
            if u >= n_diag:
                return whole_tile
            return ((slice(0, u * bq + half), slice(0, bq)), (slice(u * bq + half, (u + 1) * bq), slice(half, bq)))

        even, odd = (s0, e0, p0, g0), (s1, e1, p1, g1)
        products(first, s0, e0, areas_of(0))
        products(first + 1, s1, e1, areas_of(1))
        elementwise(first, s0, e0, p0, g0, qry0=0, areas=areas_of(0))
        for u in range(1, n_diag):
            (s_in, e_in, p_cur, g_cur), (s_out, e_out, p_prev, g_prev) = (odd, even) if u % 2 else (even, odd)
            products(first + u + 1, s_out, e_out, areas_of(u + 1))
            gradients(first + u - 1, p_prev, g_prev, areas_of(u - 1))
            elementwise(first + u, s_in, e_in, p_cur, g_cur, qry0=u * bq, areas=areas_of(u))
        corner = (slice(bk - half, bk), slice(0, half))
        p1[corner] = jnp.zeros((half, half), BF16)
        g1[corner] = jnp.zeros((half, half), BF16)

        def two_passes(n, _):
            i = first + n_diag + 2 * n
            one_pass(i, s0, e0, s1, e1, p1, g1, p0, g0)
            one_pass(i + 1, s1, e1, s0, e0, p0, g0, p1, g1)
            return 0

        lax.fori_loop(0, (last - first - n_diag + 1) // 2, two_passes, 0)
        gradients(last, p1, g1)
        dk_ref[0] = dk_ref[0] * LN2

    whole = pl.BlockSpec((1, t, HEAD_PAD), lambda h, j: (h, 0, 0))
    block = pl.BlockSpec((1, bk, HEAD_PAD), lambda h, j: (h, j, 0))
    rows = pl.BlockSpec((1, 1, t), lambda h, j: (h, 0, 0), pipeline_mode=pl.Buffered(1))
    shape = jax.ShapeDtypeStruct((HEADS, t, HEAD_PAD), F32)
    tile = lambda dtype: pltpu.VMEM((bk, bq), dtype)
    return pl.pallas_call(
        body, name="attn_bwd", grid=(HEADS, t // bk),
        in_specs=[whole, block, block, whole, rows, rows],
        out_specs=[pl.BlockSpec((1, HEAD_PAD, t), lambda h, j: (h, 0, 0)), block, block],
        out_shape=[jax.ShapeDtypeStruct((HEADS, HEAD_PAD, t), F32), shape, shape],
        scratch_shapes=[tile(F32), tile(F32), tile(F32), tile(F32), tile(BF16), tile(BF16),
                        tile(BF16), tile(BF16), pltpu.VMEM((HEAD_PAD, bk), BF16)],
        compiler_params=_cparams(("arbitrary", "arbitrary"), vmem_limit=ATTN_BWD_VMEM_LIMIT),
    )(q, k, v, do, lse_row, d_row)


def _bwd_tail(dq, dk, dv, proj, pos_col, invf_row, w_heads, q_g, kv_g, x, dr, drest, wp_in):
    t = proj.shape[0]
    tm = PROJ_TILE
    n_head = 4 * LANES

    def body(dq_ref, dk_ref, dv_ref, ph_ref, pos_ref, invf_ref, wh_ref, qg_ref, kvg_ref,
             x_ref, dr_ref, drest_ref, win_ref,
             gx_ref, dwin_ref, dwh_ref, dqg_ref, dkvg_ref):
        @pl.when(pl.program_id(0) == 0)
        def _():
            dwin_ref[...] = jnp.zeros_like(dwin_ref)
            dwh_ref[...] = jnp.zeros_like(dwh_ref)
            dqg_ref[...] = jnp.zeros_like(dqg_ref)
            dkvg_ref[...] = jnp.zeros_like(dkvg_ref)

        xb = x_ref[...].astype(BF16)
        dr_b = drest_ref[...]
        dwin_ref[:, n_head:] += _dot_tn(xb, dr_b)
        gx_rest = DN_ALPHA * dr_ref[...] + _dot_nt(dr_b, win_ref[:, n_head:])

        cos, s1, s2 = _rope_tables(pos_ref[...], invf_ref[...])
        lane = lax.broadcasted_iota(jnp.int32, (tm, LANES), 1)
        c_q = ph_ref[:, :Q_LORA]
        c_kv = ph_ref[:, Q_LORA:Q_LORA + KV_LORA]
        rstd_q = lax.rsqrt(jnp.mean(c_q * c_q, axis=-1, keepdims=True) + EPS)
        rstd_kv = lax.rsqrt(jnp.mean(c_kv * c_kv, axis=-1, keepdims=True) + EPS)
        qhat = c_q * rstd_q
        kvhat = c_kv * rstd_kv
        cqn = (qhat * qg_ref[...]).astype(BF16)
        ckvn = (kvhat * kvg_ref[...]).astype(BF16)
        dkr_rot = jnp.zeros((tm, LANES), F32)
        dq_heads, dkv_heads = [], []
        for h in range(HEADS):
            dq_heads.append(_rope(jnp.transpose(dq_ref[h]) * ATTN_SCALE, cos, s1, s2, -1.0).astype(BF16))
            dk_h = dk_ref[h]
            dkv_heads.append(jnp.where(lane < NOPE, dk_h, dv_ref[h]).astype(BF16))
            dkr_rot = dkr_rot + dk_h
        dq_all = jnp.concatenate(dq_heads, axis=1)
        dkv_all = jnp.concatenate(dkv_heads, axis=1)
        dwq_all = _dot_tn(cqn, dq_all)
        dwkv_all = _dot_tn(ckvn, dkv_all)
        for h in range(HEADS):
            dwh_ref[h, :Q_LORA, :] += dwq_all[:, h * HEAD_PAD:(h + 1) * HEAD_PAD]
            dwh_ref[h, Q_LORA:, :] += dwkv_all[:, h * HEAD_PAD:(h + 1) * HEAD_PAD]
        dcqn = _dot_nt(dq_all, jnp.concatenate([wh_ref[h, :Q_LORA, :] for h in range(HEADS)], axis=1))
        dckvn = _dot_nt(dkv_all, jnp.concatenate([wh_ref[h, Q_LORA:, :] for h in range(HEADS)], axis=1))
        rot_lanes = (lane >= KR_LO) & (lane < KR_LO + ROPE)
        dkr_raw = jnp.where(rot_lanes, _rope(dkr_rot, cos, s1, s2, -1.0), 0.0)
        dqg_ref[...] += jnp.sum(dcqn * qhat, axis=0, keepdims=True)
        dkvg_ref[...] += jnp.sum(dckvn * kvhat, axis=0, keepdims=True)
        dqh = dcqn * qg_ref[...]
        dkvh = dckvn * kvg_ref[...]
        dc_q = rstd_q * (dqh - qhat * jnp.mean(dqh * qhat, axis=-1, keepdims=True))
        dc_kv = rstd_kv * (dkvh - kvhat * jnp.mean(dkvh * kvhat, axis=-1, keepdims=True))
        dh_b = jnp.concatenate([dc_q, dc_kv, dkr_raw], axis=-1).astype(BF16)
        dwin_ref[:, :n_head] += _dot_tn(xb, dh_b)
        gx_ref[...] = gx_rest + _dot_nt(dh_b, win_ref[:, :n_head])

    full = lambda a: pl.BlockSpec(a.shape, lambda i: (0,) * a.ndim)
    tile = lambda w: pl.BlockSpec((tm, w), lambda i: (i, 0))
    heads = pl.BlockSpec((HEADS, tm, HEAD_PAD), lambda i: (0, i, 0))
    acc = lambda shape: (pl.BlockSpec(shape, lambda i: (0,) * len(shape)), jax.ShapeDtypeStruct(shape, F32))
    accs = [acc(wp_in.shape), acc(w_heads.shape), acc((1, Q_LORA)), acc((1, KV_LORA))]
    return pl.pallas_call(
        body, name="bwd_tail", grid=(t // tm,),
        in_specs=[pl.BlockSpec((HEADS, HEAD_PAD, tm), lambda i: (0, 0, i)), heads, heads, tile(n_head),
                  pl.BlockSpec((tm, 1), lambda i: (i, 0)), full(invf_row), full(w_heads), full(q_g), full(kv_g),
                  tile(D_MODEL), tile(D_MODEL), tile(drest.shape[1]), full(wp_in)],
        out_specs=[tile(D_MODEL)] + [a[0] for a in accs],
        out_shape=[jax.ShapeDtypeStruct((t, D_MODEL), F32)] + [a[1] for a in accs],
        compiler_params=_cparams(("arbitrary",), vmem_limit=BWD_TAIL_VMEM_LIMIT),
    )(dq, dk, dv, proj, pos_col, invf_row, w_heads, q_g, kv_g, x, dr, drest, wp_in)


def _adam(parts, w, m, v, *, name, tile_rows):
    n, rows, cols = parts.shape

    def body(p_ref, w_ref, m_ref, v_ref, g_ref, d_ref, nm_ref, nv_ref):
        g = p_ref[0].astype(F32)
        for s in range(1, n):
            g = g + p_ref[s].astype(F32)
        m_new = ADAM_B1 * m_ref[...] + (1.0 - ADAM_B1) * g
        v_new = ADAM_B2 * v_ref[...] + (1.0 - ADAM_B2) * (g * g)
        m_hat = m_new / (1.0 - ADAM_B1 ** ADAM_STEP)
        v_hat = v_new / (1.0 - ADAM_B2 ** ADAM_STEP)
        g_ref[...] = g
        d_ref[...] = -ADAM_LR * (m_hat / (jnp.sqrt(v_hat) + ADAM_EPS) + ADAM_WD * w_ref[...])
        nm_ref[...] = m_new
        nv_ref[...] = v_new

    flat = pl.BlockSpec((tile_rows, cols), lambda i: (i, 0))
    shape = jax.ShapeDtypeStruct((rows, cols), F32)
    return pl.pallas_call(
        body, name=name, grid=(rows // tile_rows,),
        in_specs=[pl.BlockSpec((n, tile_rows, cols), lambda i: (0, i, 0)), flat, flat, flat],
        out_specs=[flat] * 4, out_shape=[shape] * 4,
        compiler_params=_cparams(("arbitrary",)),
    )(parts, w, m, v)


SMALL_NAMES = ("q_norm_g", "kv_norm_g", "sgu_norm_g", "sgu_norm_b", "b_spatial", "ln_g", "ln_b")
SMALL_SIZES = (Q_LORA, KV_LORA, G_WIDTH, G_WIDTH, HEADS * CHUNK, D_MODEL, D_MODEL)


def _pack_small(vals, last=None):
    flat = jnp.concatenate([v.reshape(-1) for v in vals])
    pad = SMALL_LEN - flat.shape[0]
    if last is None:
        return jnp.pad(flat, (0, pad))
    return jnp.concatenate([flat, jnp.zeros((pad - 1,), F32), last.reshape(1)])


def _unpack_small(flat):
    out, at = [], 0
    for n in SMALL_SIZES:
        out.append(flat[at:at + n])
        at += n
    out[4] = out[4].reshape(HEADS, CHUNK)
    return out


UQ_SHARD = HEADS * (NOPE + ROPE) // N_DEV
HEAD_ROWS = Q_LORA + KV_LORA
MIXED_ROWS = HEAD_ROWS + CHUNK + SMALL_LEN // N_DEV // LANES


def _head_slab(w_uq_shard, w_ukv_shard):
    return jnp.concatenate([jnp.pad(w_uq_shard, ((0, 0), (0, LANES - UQ_SHARD))), w_ukv_shard])


IN_SHARD = D_IN // N_DEV


def _w_in_pieces():
    split = Q_LORA + KV_LORA
    moves = ((0, split, 0), (split, split + ROPE, KR_LO), (split + ROPE, D_IN, LANES - ROPE))
    pieces = []
    for s in range(N_DEV):
        lo, hi = s * IN_SHARD, (s + 1) * IN_SHARD
        for a, b, shift in moves:
            a, b = max(a, lo), min(b, hi)
            if a < b:
                pieces.append((s, a - lo, a + shift, b - a))
    return pieces


def _padded_w_in(shards):
    tr = TOKEN_TILE

    def body(sh_ref, o_ref):
        o_ref[...] = jnp.zeros_like(o_ref)
        for s, src, dst, width in _w_in_pieces():
            o_ref[:, dst:dst + width] = sh_ref[s, :, src:src + width]

    return pl.pallas_call(
        body, name="w_in_pad", grid=(D_MODEL // tr,),
        in_specs=[pl.BlockSpec((N_DEV, tr, IN_SHARD), lambda i: (0, i, 0))],
        out_specs=pl.BlockSpec((tr, D_IN_PAD), lambda i: (i, 0)),
        out_shape=jax.ShapeDtypeStruct((D_MODEL, D_IN_PAD), shards.dtype),
        compiler_params=_cparams(("arbitrary",)),
    )(shards)


def _w_in_shards(dwp_in):
    tr = TOKEN_TILE
    by_shard = [[p for p in _w_in_pieces() if p[0] == s] for s in range(N_DEV)]

    def body(w_ref, o_ref):
        for s, pieces in enumerate(by_shard):
            parts = [w_ref[:, dst:dst + width] for _, _, dst, width in pieces]
            o_ref[s] = parts[0] if len(parts) == 1 else jnp.concatenate(parts, axis=1)

    return pl.pallas_call(
        body, name="w_in_split", grid=(D_MODEL // tr,),
        in_specs=[pl.BlockSpec((tr, D_IN_PAD), lambda i: (i, 0))],
        out_specs=pl.BlockSpec((N_DEV, tr, IN_SHARD), lambda i: (0, i, 0)),
        out_shape=jax.ShapeDtypeStruct((N_DEV, D_MODEL, IN_SHARD), dwp_in.dtype),
        compiler_params=_cparams(("arbitrary",)),
    )(dwp_in)


def kernel(x, positions, w_in, q_norm_g, w_uq, kv_norm_g, w_ukv, sgu_norm_g, sgu_norm_b, w_spatial, b_spatial, w_out, ln_g, ln_b, loss_target, m_w_in, m_q_norm_g, m_w_uq, m_kv_norm_g, m_w_ukv, m_sgu_norm_g, m_sgu_norm_b, m_w_spatial, m_b_spatial, m_w_out, m_ln_g, m_ln_b, v_w_in, v_q_norm_g, v_w_uq, v_kv_norm_g, v_w_ukv, v_sgu_norm_g, v_sgu_norm_b, v_w_spatial, v_b_spatial, v_w_out, v_ln_g, v_ln_b):
    me = 4 * lax.axis_index("x") + 2 * lax.axis_index("y") + lax.axis_index("c")
    seq = x.shape[1]
    x2 = x.reshape(seq, D_MODEL)
    tgt2 = loss_target.reshape(seq, D_MODEL)
    pos_col = positions.reshape(seq, 1)

    w_in_shards, w_out_shards, w_heads = _gather_two_level(
        [w_in.astype(BF16), w_out.astype(BF16), _head_slab(w_uq, w_ukv).astype(BF16)],
        name="wgather")
    (loss_part, grad_x, d_in, d_heads, d_out, d_ws, d_bs_t, d_lng, d_lnb, d_sgug, d_sgub, d_qg, d_kvg) = _local_step(
        x2, tgt2, pos_col, w_in_shards, w_heads, w_out_shards.reshape(D_MODEL, D_MODEL), q_norm_g, kv_norm_g,
        sgu_norm_g, sgu_norm_b, w_spatial, b_spatial, ln_g, ln_b)

    small_part = _pack_small([d_qg, d_kvg, d_sgug, d_sgub, d_bs_t[:, :HEADS].T, d_lng, d_lnb], last=loss_part[0, :1])
    mixed = jnp.concatenate([d_heads, d_ws, small_part.reshape(N_DEV, -1, LANES)], axis=1)
    by_chip = [g.reshape((N_CHIPS, 2) + g.shape[1:])
               for g in (d_in, d_out.reshape(N_DEV, D_MODEL // N_DEV, D_MODEL), mixed)]
    from_sibling = _sibling_swap(by_chip, name="gswap")
    core = lax.axis_index("c").astype(jnp.int32).reshape(1)
    pair_sums = [_pair_sum(a, b, core, name=nm, tile_rows=tr, out_dtype=dt) for a, b, nm, tr, dt in zip(
        by_chip, from_sibling, ("gsum_in", "gsum_out", "gsum_mixed"), (TOKEN_TILE, D_MODEL // N_DEV, MIXED_ROWS),
        (BF16, BF16, F32))]
    recv_in, recv_out, recv_mixed = _chip_exchange(pair_sums, name="gexch")

    take = lambda a: lax.dynamic_index_in_dim(a, me, 0, keepdims=False)
    small_w = _pack_small([q_norm_g, kv_norm_g, sgu_norm_g, sgu_norm_b, b_spatial, ln_g, ln_b])
    small_m = _pack_small([m_q_norm_g, m_kv_norm_g, m_sgu_norm_g, m_sgu_norm_b, m_b_spatial, m_ln_g, m_ln_b])
    small_v = _pack_small([v_q_norm_g, v_kv_norm_g, v_sgu_norm_g, v_sgu_norm_b, v_b_spatial, v_ln_g, v_ln_b])
    own_mixed = lambda uq, ukv, sp, small: jnp.concatenate(
        [_head_slab(uq, ukv), take(sp), take(small.reshape(N_DEV, -1, LANES))])
    res_in = _adam(recv_in, w_in, m_w_in, v_w_in, name="adam_in", tile_rows=TOKEN_TILE)
    res_out = _adam(recv_out, w_out, m_w_out, v_w_out, name="adam_out", tile_rows=D_MODEL // N_DEV)
    res_mixed = _adam(recv_mixed, own_mixed(w_uq, w_ukv, w_spatial, small_w), own_mixed(m_w_uq, m_w_ukv, m_w_spatial, small_m),
                      own_mixed(v_w_uq, v_w_ukv, v_w_spatial, small_v), name="adam_mixed", tile_rows=MIXED_ROWS)

    rep_g, = _exchange([res_mixed[0][HEAD_ROWS:]], name="sgather", per_destination=False)
    rep_pack = lambda sp, small: jnp.concatenate(
        [sp.reshape(N_DEV, CHUNK, LANES), small.reshape(N_DEV, -1, LANES)], axis=1).reshape(-1, LANES)
    _, delta_rep, m_rep, v_rep = _adam(rep_g.reshape(1, N_DEV * REP_ROWS, LANES), rep_pack(w_spatial, small_w),
                                       rep_pack(m_w_spatial, small_m), rep_pack(v_w_spatial, small_v),
                                       name="adam_rep", tile_rows=N_DEV * REP_ROWS)

    def rep_unpack(a):
        a = a.reshape(N_DEV, REP_ROWS, LANES)
        small = _unpack_small(a[:, CHUNK:].reshape(-1))
        return [small[0], small[1], small[2], small[3], a[:, :CHUNK], small[4], small[5], small[6]]

    def ordered(which, rep):
        r_qg, r_kvg, r_sg, r_sb, r_ws, r_bs, r_lg, r_lb = rep_unpack(rep)
        heads = res_mixed[which]
        return [res_in[which], r_qg, heads[:Q_LORA, :UQ_SHARD], r_kvg, heads[Q_LORA:HEAD_ROWS], r_sg, r_sb, r_ws, r_bs,
                res_out[which], r_lg, r_lb]

    loss = rep_g[N_DEV - 1, REP_ROWS - 1, LANES - 1]
    outs = [loss, grad_x.reshape(x.shape)]
    outs += ordered(0, rep_g.reshape(-1, LANES))
    outs += ordered(1, delta_rep)
    outs += ordered(2, m_rep)
    outs += ordered(3, v_rep)
    return tuple(outs)


def _local_step(x2, tgt2, pos_col, w_in_shards, w_heads, w_out_full, q_norm_g, kv_norm_g, sgu_norm_g, sgu_norm_b,
                w_spatial, b_spatial, ln_g, ln_b):
    wp_in = _padded_w_in(w_in_shards)

    half = jnp.arange(HALF, dtype=F32)
    inv_freq = 1.0 / (ROPE_THETA ** (half / HALF))
    invf_row = jnp.concatenate([jnp.zeros((KR_LO,), F32), inv_freq, inv_freq,
                                jnp.zeros((LANES - KR_LO - ROPE,), F32)]).reshape(1, LANES)
    tri = jnp.tril(jnp.ones((CHUNK, CHUNK), dtype=bool))
    ws_low = jnp.where(tri[None], w_spatial, 0.0).astype(BF16)
    ws_low_t = ws_low.transpose(0, 2, 1)
    bsp = jnp.repeat(b_spatial.T, G_HEAD_DIM, axis=1)
    row = lambda a: a.reshape(1, -1)

    proj, q, k, v, vt = _fwd_proj(x2, pos_col, invf_row, wp_in, w_heads, row(q_norm_g), row(kv_norm_g))
    o, lse_row = _attn_fwd(q, k, vt)
    (dr, do, d_row, drest, d_out, d_ws, d_bs_t, d_lng, d_lnb, d_sgug, d_sgub, loss_part) = _mid(
        x2, tgt2, proj, o, w_out_full, ws_low, ws_low_t, bsp, row(sgu_norm_g), row(sgu_norm_b), row(ln_g), row(ln_b))
    dqt, dk, dv = _attn_bwd(q, k, v, do, lse_row, d_row)
    grad_x, dwp_in, d_heads, d_qg, d_kvg = _bwd_tail(dqt, dk, dv, proj, pos_col, invf_row, w_heads, row(q_norm_g),
                                                      row(kv_norm_g), x2, dr, drest, wp_in)
    return (loss_part, grad_x, _w_in_shards(dwp_in), d_heads, d_out, d_ws, d_bs_t, d_lng, d_lnb, d_sgug, d_sgub,
            d_qg, d_kvg)
```

```python
import math

import jax
import jax.numpy as jnp
from jax import lax
from jax.experimental import pallas as pl
from jax.experimental.pallas import tpu as pltpu

F32 = jnp.float32
BF16 = jnp.bfloat16

N_DEV = 8
D_MODEL = 1024
HEADS = 8
NOPE = 64
ROPE = 32
HALF = ROPE // 2
VDIM = 64
Q_LORA = 256
KV_LORA = 128
G_WIDTH = 512
G_HEAD_DIM = 64
CHUNK = 128
HEAD_PAD = 128
D_IN = 2464
D_IN_PAD = 2560
KR_LO = NOPE
SUM_ROW = NOPE - 1
LIVE_ROWS = slice(NOPE - 16, HEAD_PAD)
ROPE_THETA = 10000.0
DN_ALPHA = 2.0 ** 0.25
EPS = 1e-5
ATTN_SCALE = 1.0 / math.sqrt(NOPE + ROPE)
ADAM_LR, ADAM_B1, ADAM_B2, ADAM_EPS, ADAM_WD, ADAM_STEP = 0.001, 0.9, 0.999, 1e-08, 0.01, 10

LANES = 128
REP_ROWS = 136
SMALL_LEN = 8192
VMEM_LIMIT = 56 * 1024 * 1024
ATTN_BWD_VMEM_LIMIT = 61 * 1024 * 1024
BWD_TAIL_VMEM_LIMIT = 61 * 1024 * 1024

TOKEN_TILE = 256
PROJ_TILE = 512
ATTN_FWD_WIDE = 2048
ATTN_BWD_WIDE = 2048
ATTN_NARROW = 512
SOFTMAX_ROWS = 512
LOG2E = 1.4426950408889634
LN2 = 0.6931471805599453
Q_PRESCALE = ATTN_SCALE * LOG2E


def _cparams(sem=None, vmem_limit=VMEM_LIMIT):
    return pltpu.CompilerParams(dimension_semantics=sem, vmem_limit_bytes=vmem_limit)


def _dot(a, b):
    return jnp.dot(a, b, preferred_element_type=F32)


def _dot_nt(a, b):
    return lax.dot_general(a, b, (((1,), (1,)), ((), ())), preferred_element_type=F32)


def _dot_tn(a, b):
    return lax.dot_general(a, b, (((0,), (0,)), ((), ())), preferred_element_type=F32)


def _sigmoid(z):
    return 1.0 / (1.0 + jnp.exp(-z))


def _gelu(x):
    return 0.5 * x * (1.0 + lax.erf(x * 0.7071067811865476))


def _gelu_grad(x):
    cdf = 0.5 * (1.0 + lax.erf(x * 0.7071067811865476))
    return cdf + x * jnp.exp(-0.5 * x * x) * 0.3989422804014327


def _gather_direct(srcs, *, name):
    n = len(srcs)

    def body(*refs):
        src_refs, out_refs = refs[:n], refs[n:2 * n]
        send_sems, recv_sems, local_sems = refs[2 * n:]
        x, y, c = lax.axis_index("x"), lax.axis_index("y"), lax.axis_index("c")
        me = 4 * x + 2 * y + c
        mine = [pltpu.make_async_copy(src_refs[t], out_refs[t].at[me], local_sems.at[t]) for t in range(n)]
        for cp in mine:
            cp.start()
        sends, arrivals = [], []
        for k in (6, 7, 4, 5, 2, 3, 1):
            px = 1 - x if k & 4 else x
            py = 1 - y if k & 2 else y
            pc = 1 - c if k & 1 else c
            peer = 4 * px + 2 * py + pc
            for t in range(n):
                sem = (k - 1) * n + t
                cp = pltpu.make_async_remote_copy(
                    src_ref=src_refs[t], dst_ref=out_refs[t].at[me],
                    send_sem=send_sems.at[sem], recv_sem=recv_sems.at[sem],
                    device_id=(px, py, pc), device_id_type=pl.DeviceIdType.MESH)
                cp.start()
                sends.append(cp)
                arrivals.append(pltpu.make_async_remote_copy(
                    src_ref=src_refs[t], dst_ref=out_refs[t].at[peer],
                    send_sem=send_sems.at[sem], recv_sem=recv_sems.at[sem],
                    device_id=(x, y, c), device_id_type=pl.DeviceIdType.MESH))
        for cp in arrivals:
            cp.wait_recv()
        for cp in sends:
            cp.wait_send()
        for cp in mine:
            cp.wait()

    hbm = pl.BlockSpec(memory_space=pl.ANY)
    return pl.pallas_call(
        body, name=name,
        out_shape=[jax.ShapeDtypeStruct((N_DEV,) + s.shape, s.dtype) for s in srcs],
        in_specs=[hbm] * n, out_specs=[hbm] * n,
        scratch_shapes=[pltpu.SemaphoreType.DMA(((N_DEV - 1) * n,)), pltpu.SemaphoreType.DMA(((N_DEV - 1) * n,)),
                        pltpu.SemaphoreType.DMA((n,))],
    )(*srcs)


def _gather_two_level(srcs, *, name):
    n = len(srcs)

    def body(*refs):
        src_refs, out_refs = refs[:n], refs[n:2 * n]
        send_sems, recv_sems, local_sems = refs[2 * n:]
        x, y, c = lax.axis_index("x"), lax.axis_index("y"), lax.axis_index("c")
        me, sibling = (x, y, c), (x, y, 1 - c)
        chips = [(1 - x, 1 - y), (1 - x, y), (x, 1 - y)]
        index = lambda px, py, pc: 4 * px + 2 * py + pc

        def copy(k, t, block, to, src=None):
            place = out_refs[t].at[index(*block)]
            return pltpu.make_async_remote_copy(
                src_ref=place if src is None else src, dst_ref=place,
                send_sem=send_sems.at[k * n + t], recv_sem=recv_sems.at[k * n + t],
                device_id=to, device_id_type=pl.DeviceIdType.MESH)

        mine = [pltpu.make_async_copy(src_refs[t], out_refs[t].at[index(*me)], local_sems.at[t]) for t in range(n)]
        for cp in mine:
            cp.start()
        first = [copy(1 + j, t, me, (*chip, c), src=src_refs[t]) for j, chip in enumerate(chips) for t in range(n)]
        first += [copy(0, t, me, sibling, src=src_refs[t]) for t in range(n)]
        for cp in first:
            cp.start()
        passed = []
        for j, chip in enumerate(chips):
            for t in range(n):
                copy(1 + j, t, (*chip, c), me).wait_recv()
                cp = copy(4 + j, t, (*chip, c), sibling)
                cp.start()
                passed.append(cp)
        for t in range(n):
            copy(0, t, sibling, me).wait_recv()
        for j, chip in enumerate(chips):
            for t in range(n):
                copy(4 + j, t, (*chip, 1 - c), me).wait_recv()
        for cp in first + passed:
            cp.wait_send()
        for cp in mine:
            cp.wait()

    hbm = pl.BlockSpec(memory_space=pl.ANY)
    return pl.pallas_call(
        body, name=name,
        out_shape=[jax.ShapeDtypeStruct((N_DEV,) + s.shape, s.dtype) for s in srcs],
        in_specs=[hbm] * n, out_specs=[hbm] * n,
        scratch_shapes=[pltpu.SemaphoreType.DMA((7 * n,)), pltpu.SemaphoreType.DMA((7 * n,)),
                        pltpu.SemaphoreType.DMA((n,))],
    )(*srcs)


N_CHIPS = N_DEV // 2


def _sibling_swap(srcs, *, name):
    n = len(srcs)

    def body(*refs):
        src_refs, out_refs = refs[:n], refs[n:2 * n]
        send_sems, recv_sems = refs[2 * n:]
        x, y, c = lax.axis_index("x"), lax.axis_index("y"), lax.axis_index("c")
        sends = []
        for chip in range(N_CHIPS):
            for t in range(n):
                cp = pltpu.make_async_remote_copy(
                    src_ref=src_refs[t].at[chip, 1 - c], dst_ref=out_refs[t].at[chip],
                    send_sem=send_sems.at[chip * n + t], recv_sem=recv_sems.at[chip * n + t],
                    device_id=(x, y, 1 - c), device_id_type=pl.DeviceIdType.MESH)
                cp.start()
                sends.append(cp)
        for cp in sends:
            cp.wait_recv()
        for cp in sends:
            cp.wait_send()

    hbm = pl.BlockSpec(memory_space=pl.ANY)
    return pl.pallas_call(
        body, name=name,
        out_shape=[jax.ShapeDtypeStruct((N_CHIPS,) + s.shape[2:], s.dtype) for s in srcs],
        in_specs=[hbm] * n, out_specs=[hbm] * n,
        scratch_shapes=[pltpu.SemaphoreType.DMA((N_CHIPS * n,)), pltpu.SemaphoreType.DMA((N_CHIPS * n,))],
    )(*srcs)


def _pair_sum(mine, theirs, core, *, name, tile_rows, out_dtype):
    _, _, rows, cols = mine.shape

    def body(core_ref, a_ref, b_ref, o_ref):
        o_ref[...] = (a_ref[0] + b_ref[...]).astype(out_dtype)

    return pl.pallas_call(
        body, name=name,
        grid_spec=pltpu.PrefetchScalarGridSpec(
            num_scalar_prefetch=1, grid=(N_CHIPS, rows // tile_rows),
            in_specs=[pl.BlockSpec((1, 1, tile_rows, cols), lambda q, r, core_ref: (q, core_ref[0], r, 0)),
                      pl.BlockSpec((1, tile_rows, cols), lambda q, r, core_ref: (q, r, 0))],
            out_specs=pl.BlockSpec((1, tile_rows, cols), lambda q, r, core_ref: (q, r, 0))),
        out_shape=jax.ShapeDtypeStruct((N_CHIPS, rows, cols), out_dtype),
        compiler_params=_cparams(("arbitrary", "arbitrary")),
    )(core, mine, theirs)


def _chip_exchange(srcs, *, name):
    n = len(srcs)

    def body(*refs):
        src_refs, out_refs = refs[:n], refs[n:2 * n]
        send_sems, recv_sems, local_sems = refs[2 * n:]
        x, y, c = lax.axis_index("x"), lax.axis_index("y"), lax.axis_index("c")
        my_chip = 2 * x + y
        mine = [pltpu.make_async_copy(src_refs[t].at[my_chip], out_refs[t].at[my_chip], local_sems.at[t])
                for t in range(n)]
        for cp in mine:
            cp.start()
        sends, arrivals = [], []
        for k in (3, 2, 1):
            px = 1 - x if k & 2 else x
            py = 1 - y if k & 1 else y
            peer_chip = 2 * px + py
            for t in range(n):
                sem = (k - 1) * n + t
                cp = pltpu.make_async_remote_copy(
                    src_ref=src_refs[t].at[peer_chip], dst_ref=out_refs[t].at[my_chip],
                    send_sem=send_sems.at[sem], recv_sem=recv_sems.at[sem],
                    device_id=(px, py, c), device_id_type=pl.DeviceIdType.MESH)
                cp.start()
                sends.append(cp)
                arrivals.append(pltpu.make_async_remote_copy(
                    src_ref=src_refs[t].at[peer_chip], dst_ref=out_refs[t].at[peer_chip],
                    send_sem=send_sems.at[sem], recv_sem=recv_sems.at[sem],
                    device_id=(x, y, c), device_id_type=pl.DeviceIdType.MESH))
        for cp in arrivals:
            cp.wait_recv()
        for cp in sends:
            cp.wait_send()
        for cp in mine:
            cp.wait()

    hbm = pl.BlockSpec(memory_space=pl.ANY)
    return pl.pallas_call(
        body, name=name,
        out_shape=[jax.ShapeDtypeStruct(s.shape, s.dtype) for s in srcs],
        in_specs=[hbm] * n, out_specs=[hbm] * n,
        scratch_shapes=[pltpu.SemaphoreType.DMA((3 * n,)), pltpu.SemaphoreType.DMA((3 * n,)),
                        pltpu.SemaphoreType.DMA((n,))],
    )(*srcs)


def _rope_tables(pos_col, invf_row):
    ang = pos_col.astype(F32) * invf_row
    lane = lax.broadcasted_iota(jnp.int32, ang.shape, 1)
    cos, sin = jnp.cos(ang), jnp.sin(ang)
    first = (lane >= KR_LO) & (lane < KR_LO + HALF)
    second = (lane >= KR_LO + HALF) & (lane < KR_LO + ROPE)
    return cos, jnp.where(first, sin, 0.0), jnp.where(second, sin, 0.0)


def _rope(t, cos, sin_first, sin_second, sign):
    up = pltpu.roll(t, LANES - HALF, 1)
    down = pltpu.roll(t, HALF, 1)
    return t * cos - sign * (up * sin_first) + sign * (down * sin_second)


def _fwd_proj(x, pos_col, invf_row, w_in_shards, w_heads, q_g, kv_g):
    t = x.shape[0]
    tm = PROJ_TILE

    def body(x_ref, pos_ref, invf_ref, sh_ref, wh_ref, qg_ref, kvg_ref,
             proj_ref, q_ref, k_ref, v_ref, vt_ref, win_ref):
        @pl.when(pl.program_id(0) == 0)
        def _():
            win_ref[...] = jnp.zeros_like(win_ref)
            for s, src, dst, width in _w_in_pieces():
                win_ref[:, dst:dst + width] = sh_ref[s, :, src:src + width]

        proj = _dot(x_ref[...].astype(BF16), win_ref[...])
        proj_ref[...] = proj
        c_q = proj[:, :Q_LORA]
        c_kv = proj[:, Q_LORA:Q_LORA + KV_LORA]
        kr_raw = proj[:, Q_LORA + KV_LORA:Q_LORA + KV_LORA + LANES]
        cqn = (c_q * lax.rsqrt(jnp.mean(c_q * c_q, axis=-1, keepdims=True) + EPS) * qg_ref[...]).astype(BF16)
        ckvn = (c_kv * lax.rsqrt(jnp.mean(c_kv * c_kv, axis=-1, keepdims=True) + EPS) * kvg_ref[...]).astype(BF16)
        cos, s1, s2 = _rope_tables(pos_ref[...], invf_ref[...])
        kr = _rope(kr_raw, cos, s1, s2, 1.0)
        lane = lax.broadcasted_iota(jnp.int32, (tm, HEAD_PAD), 1)
        q_all = _dot(cqn, jnp.concatenate([wh_ref[h, :Q_LORA, :] for h in range(HEADS)], axis=1))
        kv_all = _dot(ckvn, jnp.concatenate([wh_ref[h, Q_LORA:, :] for h in range(HEADS)], axis=1))
        for h in range(HEADS):
            q_h = q_all[:, h * HEAD_PAD:(h + 1) * HEAD_PAD]
            kv_h = kv_all[:, h * HEAD_PAD:(h + 1) * HEAD_PAD]
            q_ref[h] = (_rope(q_h, cos, s1, s2, 1.0) * Q_PRESCALE).astype(BF16)
            k_ref[h] = jnp.where(lane < NOPE, kv_h, kr).astype(BF16)
            v_ref[h] = kv_h.astype(BF16)
            vt_ref[h] = jnp.transpose(jnp.where(lane == SUM_ROW, 1.0, kv_h)).astype(BF16)

    full = lambda a: pl.BlockSpec(a.shape, lambda i: (0,) * a.ndim)
    head_spec = pl.BlockSpec((HEADS, tm, HEAD_PAD), lambda i: (0, i, 0))
    head_shape = jax.ShapeDtypeStruct((HEADS, t, HEAD_PAD), BF16)
    return pl.pallas_call(
        body, name="fwd_proj", grid=(t // tm,),
        in_specs=[pl.BlockSpec((tm, D_MODEL), lambda i: (i, 0)), pl.BlockSpec((tm, 1), lambda i: (i, 0)),
                  full(invf_row), full(w_in_shards), full(w_heads), full(q_g), full(kv_g)],
        out_specs=[pl.BlockSpec((tm, D_IN_PAD), lambda i: (i, 0)), head_spec, head_spec, head_spec,
                   pl.BlockSpec((HEADS, HEAD_PAD, tm), lambda i: (0, 0, i)),
                   pl.BlockSpec((D_MODEL, D_IN_PAD), lambda i: (0, 0))],
        out_shape=[jax.ShapeDtypeStruct((t, D_IN_PAD), F32), head_shape, head_shape, head_shape,
                   jax.ShapeDtypeStruct((HEADS, HEAD_PAD, t), BF16),
                   jax.ShapeDtypeStruct((D_MODEL, D_IN_PAD), w_in_shards.dtype)],
        compiler_params=_cparams(("arbitrary",)),
    )(x, pos_col, invf_row, w_in_shards, w_heads, q_g, kv_g)


def _attn_fwd(q, k, vt):
    t = q.shape[1]
    bq, bk = ATTN_FWD_WIDE, ATTN_NARROW
    n_diag = bq // bk
    chunk = SOFTMAX_ROWS

    def body(q_ref, k_ref, vt_ref, o_ref, lse_ref, s0, s1, p0, p1, x0, x1, m_scr, a_scr, acc_scr):
        i = pl.program_id(1)
        at = lambda j: pl.ds(pl.multiple_of(j * bk, bk), bk)

        def exp_pass(s_in, block_max, p_out, diagonal=False, cols=slice(None)):
            width = bq if cols == slice(None) else cols.stop - cols.start

            def load(r):
                s = s_in[r:r + chunk, cols]
                if diagonal:
                    key = lax.broadcasted_iota(jnp.int32, (chunk, width), 0) + r
                    qry = lax.broadcasted_iota(jnp.int32, (chunk, width), 1)
                    s = jnp.where(qry >= key, s, -jnp.inf)
                return s

            if diagonal:
                block_max = jnp.max(load(0), axis=0, keepdims=True)
                for r in range(chunk, bk, chunk):
                    block_max = jnp.maximum(block_max, jnp.max(load(r), axis=0, keepdims=True))
            m_old = m_scr[:, cols]
            m_new = jnp.maximum(m_old, block_max)
            alpha = jnp.exp2(m_old - m_new)
            for r in range(0, bk, chunk):
                p_out[r:r + chunk, cols] = jnp.exp2(load(r) - m_new).astype(BF16)
            m_scr[:, cols] = m_new
            return alpha

        def scores(j, s_out, x_out):
            s = _dot_nt(k_ref[0, at(j), :], q_ref[0])
            s_out[...] = s
            x_out[...] = jnp.max(s, axis=0, keepdims=True)

        def value_product(j, p_in):
            return _dot(vt_ref[0, LIVE_ROWS, at(j)], p_in[...])

        def one_pass(j, s_in, x_in, s_out, x_out, p_prev, p_cur):
            scores(j + 1, s_out, x_out)
            acc_scr[...] = a_scr[...] * acc_scr[...] + value_product(jnp.maximum(j - 1, 0), p_prev)
            a_scr[...] = exp_pass(s_in, x_in[...], p_cur)

        scores(0, s0, x0)
        p1[...] = jnp.zeros_like(p1)
        a_scr[...] = jnp.ones_like(a_scr)
        m_scr[...] = jnp.full(m_scr.shape, -jnp.inf, F32)
        acc_scr[...] = jnp.zeros_like(acc_scr)

        def two_passes(n, _):
            one_pass(2 * n, s0, x0, s1, x1, p1, p0)
            one_pass(2 * n + 1, s1, x1, s0, x0, p0, p1)
            return 0

        lax.fori_loop(0, (n_diag // 2) * i, two_passes, 0)
        d = n_diag * i
        alpha, p_prev, cols = a_scr[...], p1, slice(0, bq)
        for u in range(n_diag + 1):
            s_in, s_next, p_cur = (s0, s1, p0) if u % 2 == 0 else (s1, s0, p1)
            if u + 1 < n_diag:
                ahead = slice((u + 1) * bk, bq)
                s_next[:, ahead] = _dot_nt(k_ref[0, at(d + u + 1), :], q_ref[0, ahead, :])
            acc_scr[:, cols] = alpha * acc_scr[:, cols] + _dot(vt_ref[0, LIVE_ROWS, at(jnp.maximum(d + u - 1, 0))],
                                                               p_prev[:, cols])
            if u < n_diag:
                cols = slice(u * bk, bq)
                alpha = exp_pass(s_in, None, p_cur, diagonal=True, cols=cols)
                p_prev = p_cur
        denom = acc_scr[SUM_ROW - LIVE_ROWS.start:NOPE - LIVE_ROWS.start, :]
        o = jnp.transpose(acc_scr[NOPE - LIVE_ROWS.start:, :] / denom)
        o_ref[0] = jnp.concatenate([jnp.zeros_like(o), o], axis=1)
        lse_ref[0] = m_scr[...] + jnp.log2(denom)

    tile = lambda dtype: pltpu.VMEM((bk, bq), dtype)
    stat = pltpu.VMEM((1, bq), F32)
    return pl.pallas_call(
        body, name="attn_fwd", grid=(HEADS, t // bq),
        in_specs=[pl.BlockSpec((1, bq, HEAD_PAD), lambda h, i: (h, i, 0)),
                  pl.BlockSpec((1, t, HEAD_PAD), lambda h, i: (h, 0, 0)),
                  pl.BlockSpec((1, HEAD_PAD, t), lambda h, i: (h, 0, 0))],
        out_specs=[pl.BlockSpec((1, bq, HEAD_PAD), lambda h, i: (h, i, 0)),
                   pl.BlockSpec((1, 1, bq), lambda h, i: (h, 0, i))],
        out_shape=[jax.ShapeDtypeStruct((HEADS, t, HEAD_PAD), F32), jax.ShapeDtypeStruct((HEADS, 1, t), F32)],
        scratch_shapes=[tile(F32), tile(F32), tile(BF16), tile(BF16), stat, stat, stat, stat,
                        pltpu.VMEM((HEAD_PAD - LIVE_ROWS.start, bq), F32)],
        compiler_params=_cparams(("arbitrary", "arbitrary")),
    )(q, k, vt)


def _mid(x, target, proj, ol, w_out, ws_low, ws_low_t, bsp, sgu_g, sgu_b, ln_g, ln_b):
    t = x.shape[0]
    tm = TOKEN_TILE
    n_steps = t // tm

    def body(x_ref, tgt_ref, za_ref, u_ref, v_ref, zb_ref, ol_ref, wout_ref, ws_ref, wst_ref, bsp_ref,
             sg_ref, sb_ref, lg_ref, lb_ref,
             dr_ref, do_ref, drow_ref, drest_ref, dwout_ref, dws_ref, dbs_ref, dlg_ref, dlb_ref, dsg_ref, dsb_ref,
             loss_ref, dbsp_acc):
        step = pl.program_id(0)

        @pl.when(step == 0)
        def _():
            dwout_ref[...] = jnp.zeros_like(dwout_ref)
            dws_ref[...] = jnp.zeros_like(dws_ref)
            dbs_ref[...] = jnp.zeros_like(dbs_ref)
            dlg_ref[...] = jnp.zeros_like(dlg_ref)
            dlb_ref[...] = jnp.zeros_like(dlb_ref)
            dsg_ref[...] = jnp.zeros_like(dsg_ref)
            dsb_ref[...] = jnp.zeros_like(dsb_ref)
            loss_ref[...] = jnp.zeros_like(loss_ref)
            dbsp_acc[...] = jnp.zeros_like(dbsp_acc)

        n_chunks = tm // CHUNK
        groups = G_WIDTH // LANES

        def side_by_side(a):
            return [jnp.concatenate([a[c * CHUNK:(c + 1) * CHUNK, g * LANES:(g + 1) * LANES] for c in range(n_chunks)],
                                    axis=1) for g in range(groups)]

        def by_chunk(wide):
            return jnp.concatenate([jnp.concatenate([wide[g][:, c * LANES:(c + 1) * LANES] for g in range(groups)], axis=1)
                                    for c in range(n_chunks)], axis=0)

        def own_lanes(h):
            lane = lax.broadcasted_iota(jnp.int32, (CHUNK, n_chunks * LANES), 1)
            return (lane % LANES) // G_HEAD_DIM == h % 2

        def spatial(w_ref, wide):
            return [sum(jnp.where(own_lanes(h), _dot(w_ref[h], wide[g]), 0.0) for h in (2 * g, 2 * g + 1))
                    for g in range(groups)]

        attn = jnp.concatenate([ol_ref[h][:, NOPE:] for h in range(HEADS)], axis=-1)
        za = za_ref[...]
        sig_a = _sigmoid(za)
        silu_a = za * sig_a
        out_a = attn * silu_a
        u = u_ref[...]
        ug = _gelu(u)
        vpre = v_ref[...]
        gv = _gelu(vpre)
        mu_v = jnp.mean(gv, axis=-1, keepdims=True)
        cen_v = gv - mu_v
        rstd_v = lax.rsqrt(jnp.mean(cen_v * cen_v, axis=-1, keepdims=True) + EPS)
        vhat = cen_v * rstd_v
        vg = vhat * sg_ref[...] + sb_ref[...]
        vg_b = vg.astype(BF16)
        sv = by_chunk(spatial(ws_ref, side_by_side(vg_b))) + jnp.tile(bsp_ref[...], (n_chunks, 1))
        sgu = ug * sv
        zb = zb_ref[...]
        sig_b = _sigmoid(zb)
        silu_b = zb * sig_b
        out_b = sgu * silu_b
        merged = jnp.concatenate([out_a, out_b], axis=-1).astype(BF16)
        r = DN_ALPHA * x_ref[...] + _dot(merged, wout_ref[...])
        mu = jnp.mean(r, axis=-1, keepdims=True)
        cen = r - mu
        rstd = lax.rsqrt(jnp.mean(cen * cen, axis=-1, keepdims=True) + EPS)
        xhat = cen * rstd
        hout = xhat * lg_ref[...] + lb_ref[...]
        err = hout - tgt_ref[...]
        row_loss = jnp.mean(err * err, axis=-1, keepdims=True)
        loss_ref[...] += jnp.broadcast_to(0.5 * jnp.sum(row_loss, axis=0, keepdims=True), loss_ref.shape)

        dh = err * (1.0 / D_MODEL)
        dlg_ref[...] += jnp.sum(dh * xhat, axis=0, keepdims=True)
        dlb_ref[...] += jnp.sum(dh, axis=0, keepdims=True)
        dxhat = dh * lg_ref[...]
        dr = rstd * (dxhat - jnp.mean(dxhat, axis=-1, keepdims=True)
                     - xhat * jnp.mean(dxhat * xhat, axis=-1, keepdims=True))
        dr_ref[...] = dr
        dr_b = dr.astype(BF16)
        dwout_ref[...] += _dot_tn(merged, dr_b)
        dmerged = _dot_nt(dr_b, wout_ref[...])
        d_out_a = dmerged[:, :G_WIDTH]
        d_out_b = dmerged[:, G_WIDTH:]
        dattn = d_out_a * silu_a
        for h in range(HEADS):
            do_h = dattn[:, h * VDIM:(h + 1) * VDIM]
            do_ref[h] = jnp.concatenate([jnp.zeros((tm, NOPE), F32), do_h], axis=-1).astype(BF16)
        feature = lax.broadcasted_iota(jnp.int32, (G_WIDTH, LANES), 0) // VDIM
        column = lax.broadcasted_iota(jnp.int32, (G_WIDTH, LANES), 1)
        head_sums = jnp.dot(dattn * attn, jnp.where(feature == column, 1.0, 0.0).astype(F32),
                            preferred_element_type=F32, precision=lax.Precision.HIGH)
        dsums_t = jnp.transpose(head_sums)
        for h in range(HEADS):
            drow_ref[h] = dsums_t[h:h + 1, :]
        dza = d_out_a * attn * (sig_a * (1.0 + za * (1.0 - sig_a)))
        dsgu = d_out_b * silu_b
        dzb = d_out_b * sgu * (sig_b * (1.0 + zb * (1.0 - sig_b)))
        du = dsgu * sv * _gelu_grad(u)
        dsv = dsgu * ug
        dsv_b = dsv.astype(BF16)
        for cix in range(n_chunks):
            dbsp_acc[...] += dsv[cix * CHUNK:(cix + 1) * CHUNK, :]
        dsv_wide, vg_wide = side_by_side(dsv_b), side_by_side(vg_b)
        dvg = by_chunk(spatial(wst_ref, dsv_wide))
        for h in range(HEADS):
            mine = jnp.where(own_lanes(h), dsv_wide[h // 2], jnp.zeros_like(dsv_wide[h // 2]))
            dws_ref[h] += _dot_nt(mine, vg_wide[h // 2])
        dsg_ref[...] += jnp.sum(dvg * vhat, axis=0, keepdims=True)
        dsb_ref[...] += jnp.sum(dvg, axis=0, keepdims=True)
        dvhat = dvg * sg_ref[...]
        dgv = rstd_v * (dvhat - jnp.mean(dvhat, axis=-1, keepdims=True)
                        - vhat * jnp.mean(dvhat * vhat, axis=-1, keepdims=True))
        dv = dgv * _gelu_grad(vpre)
        drest_ref[...] = jnp.concatenate([dza, du, dv, dzb], axis=-1).astype(BF16)

        @pl.when(step == n_steps - 1)
        def _():
            tri = (lax.broadcasted_iota(jnp.int32, (CHUNK, CHUNK), 0)
                   >= lax.broadcasted_iota(jnp.int32, (CHUNK, CHUNK), 1))
            for h in range(HEADS):
                dws_ref[h] = jnp.where(tri, dws_ref[h], 0.0)
            tot = dbsp_acc[...]
            lane = lax.broadcasted_iota(jnp.int32, (CHUNK, LANES), 1)
            dbs = jnp.zeros((CHUNK, LANES), F32)
            for h in range(HEADS):
                head_sum = jnp.sum(tot[:, h * G_HEAD_DIM:(h + 1) * G_HEAD_DIM], axis=-1, keepdims=True)
                dbs = jnp.where(lane == h, head_sum, dbs)
            dbs_ref[...] = dbs

    full = lambda a: pl.BlockSpec(a.shape, lambda i: (0,) * a.ndim)
    tile = lambda w, j=0: pl.BlockSpec((tm, w), lambda i, j=j: (i, j))
    heads = pl.BlockSpec((HEADS, tm, HEAD_PAD), lambda i: (0, i, 0))
    acc = lambda shape: (pl.BlockSpec(shape, lambda i: (0,) * len(shape)), jax.ShapeDtypeStruct(shape, F32))
    accs = [acc((D_MODEL, D_MODEL)), acc((HEADS, CHUNK, CHUNK)), acc((CHUNK, LANES)), acc((1, D_MODEL)),
            acc((1, D_MODEL)), acc((1, G_WIDTH)), acc((1, G_WIDTH)), acc((1, LANES))]
    return pl.pallas_call(
        body, name="mid", grid=(n_steps,),
        in_specs=[tile(D_MODEL), tile(D_MODEL), tile(G_WIDTH, 1), tile(G_WIDTH, 2), tile(G_WIDTH, 3), tile(G_WIDTH, 4),
                  heads, full(w_out), full(ws_low), full(ws_low_t), full(bsp), full(sgu_g), full(sgu_b),
                  full(ln_g), full(ln_b)],
        out_specs=[tile(D_MODEL), heads, pl.BlockSpec((HEADS, 1, tm), lambda i: (0, 0, i)), tile(4 * G_WIDTH)]
        + [a[0] for a in accs],
        out_shape=[jax.ShapeDtypeStruct((t, D_MODEL), F32), jax.ShapeDtypeStruct((HEADS, t, HEAD_PAD), BF16),
                   jax.ShapeDtypeStruct((HEADS, 1, t), F32), jax.ShapeDtypeStruct((t, 4 * G_WIDTH), BF16)]
        + [a[1] for a in accs],
        scratch_shapes=[pltpu.VMEM((CHUNK, G_WIDTH), F32)],
        compiler_params=_cparams(("arbitrary",)),
    )(x, target, proj, proj, proj, proj, ol, w_out, ws_low, ws_low_t, bsp, sgu_g, sgu_b, ln_g, ln_b)


def _attn_bwd(q, k, v, do, lse_row, d_row):
    t = q.shape[1]
    bk, bq = ATTN_BWD_WIDE, ATTN_NARROW
    n_diag = bk // bq
    half = bq // 2
    last = t // bq - 1
    chunk = SOFTMAX_ROWS

    def body(q_ref, k_ref, v_ref, do_ref, lse_ref, drow_ref, dqt_ref, dk_ref, dv_ref,
             s0, s1, e0, e1, p0, p1, g0, g1, kt_scr):
        j = pl.program_id(1)
        at = lambda i: pl.ds(pl.multiple_of(i * bq, bq), bq)

        @pl.when(j == 0)
        def _():
            dqt_ref[...] = jnp.zeros_like(dqt_ref)

        kt_scr[...] = jnp.transpose(k_ref[0].astype(F32)).astype(BF16)
        dk_ref[...] = jnp.zeros_like(dk_ref)
        dv_ref[...] = jnp.zeros_like(dv_ref)

        whole_tile = ((slice(0, bk), slice(0, bq)),)

        def queries(i, lanes):
            return pl.ds(pl.multiple_of(i * bq + lanes.start, half), lanes.stop - lanes.start)

        def products(i, s_out, e_out, areas=whole_tile):
            i = jnp.minimum(i, last)
            for keys, lanes in areas:
                s_out[keys, lanes] = _dot_nt(k_ref[0, keys, :], q_ref[0, queries(i, lanes), :])
                e_out[keys, lanes] = _dot_nt(v_ref[0, keys, :], do_ref[0, queries(i, lanes), :])

        def gradients(i, p_in, g_in, areas=whole_tile):
            for keys, lanes in areas:
                dv_ref[0, keys, :] += _dot(p_in[keys, lanes], do_ref[0, queries(i, lanes), :])
                dk_ref[0, keys, :] += _dot(g_in[keys, lanes], q_ref[0, queries(i, lanes), :])
                dqt_ref[0, :, queries(i, lanes)] += _dot(kt_scr[:, keys], g_in[keys, lanes])

        def elementwise(i, s_in, e_in, p_out, g_out, qry0=None, areas=whole_tile):
            for keys, lanes in areas:
                width = lanes.stop - lanes.start
                step = chunk if qry0 is None else half
                lse = lse_ref[0, :, queries(i, lanes)]
                dsum = drow_ref[0, :, queries(i, lanes)]
                for r in range(keys.start, keys.stop, step):
                    p = jnp.exp2(s_in[r:r + step, lanes] - lse)
                    if qry0 is not None:
                        key = lax.broadcasted_iota(jnp.int32, (step, width), 0) + r
                        qry = lax.broadcasted_iota(jnp.int32, (step, width), 1) + (qry0 + lanes.start)
                        p = jnp.where(qry >= key, p, 0.0)
                    p_out[r:r + step, lanes] = p.astype(BF16)
                    g_out[r:r + step, lanes] = (p * (e_in[r:r + step, lanes] - dsum)).astype(BF16)

        def one_pass(i, s_in, e_in, s_out, e_out, p_prev, g_prev, p_cur, g_cur):
            products(i + 1, s_out, e_out)
            gradients(i - 1, p_prev, g_prev)
            elementwise(i, s_in, e_in, p_cur, g_cur)

        first = n_diag * j

        def areas_of(u):
            if u >= n_diag:
                return whole_tile
            return ((slice(0, u * bq + half), slice(0, bq)), (slice(u * bq + half, (u + 1) * bq), slice(half, bq)))

        even, odd = (s0, e0, p0, g0), (s1, e1, p1, g1)
        products(first, s0, e0, areas_of(0))
        products(first + 1, s1, e1, areas_of(1))
        elementwise(first, s0, e0, p0, g0, qry0=0, areas=areas_of(0))
        for u in range(1, n_diag):
            (s_in, e_in, p_cur, g_cur), (s_out, e_out, p_prev, g_prev) = (odd, even) if u % 2 else (even, odd)
            products(first + u + 1, s_out, e_out, areas_of(u + 1))
            gradients(first + u - 1, p_prev, g_prev, areas_of(u - 1))
            elementwise(first + u, s_in, e_in, p_cur, g_cur, qry0=u * bq, areas=areas_of(u))
        corner = (slice(bk - half, bk), slice(0, half))
        p1[corner] = jnp.zeros((half, half), BF16)
        g1[corner] = jnp.zeros((half, half), BF16)

        def two_passes(n, _):
            i = first + n_diag + 2 * n
            one_pass(i, s0, e0, s1, e1, p1, g1, p0, g0)
            one_pass(i + 1, s1, e1, s0, e0, p0, g0, p1, g1)
            return 0

        lax.fori_loop(0, (last - first - n_diag + 1) // 2, two_passes, 0)
        gradients(last, p1, g1)
        dk_ref[0] = dk_ref[0] * LN2

    whole = pl.BlockSpec((1, t, HEAD_PAD), lambda h, j: (h, 0, 0))
    block = pl.BlockSpec((1, bk, HEAD_PAD), lambda h, j: (h, j, 0))
    rows = pl.BlockSpec((1, 1, t), lambda h, j: (h, 0, 0), pipeline_mode=pl.Buffered(1))
    shape = jax.ShapeDtypeStruct((HEADS, t, HEAD_PAD), F32)
    tile = lambda dtype: pltpu.VMEM((bk, bq), dtype)
    return pl.pallas_call(
        body, name="attn_bwd", grid=(HEADS, t // bk),
        in_specs=[whole, block, block, whole, rows, rows],
        out_specs=[pl.BlockSpec((1, HEAD_PAD, t), lambda h, j: (h, 0, 0)), block, block],
        out_shape=[jax.ShapeDtypeStruct((HEADS, HEAD_PAD, t), F32), shape, shape],
        scratch_shapes=[tile(F32), tile(F32), tile(F32), tile(F32), tile(BF16), tile(BF16),
                        tile(BF16), tile(BF16), pltpu.VMEM((HEAD_PAD, bk), BF16)],
        compiler_params=_cparams(("arbitrary", "arbitrary"), vmem_limit=ATTN_BWD_VMEM_LIMIT),
    )(q, k, v, do, lse_row, d_row)


def _bwd_tail(dq, dk, dv, proj, pos_col, invf_row, w_heads, q_g, kv_g, x, dr, drest, wp_in):
    t = proj.shape[0]
    tm = PROJ_TILE
    n_head = 4 * LANES

    def body(dq_ref, dk_ref, dv_ref, ph_ref, pos_ref, invf_ref, wh_ref, qg_ref, kvg_ref,
             x_ref, dr_ref, drest_ref, win_ref,
             gx_ref, dwin_ref, dwh_ref, dqg_ref, dkvg_ref):
        @pl.when(pl.program_id(0) == 0)
        def _():
            dwin_ref[...] = jnp.zeros_like(dwin_ref)
            dwh_ref[...] = jnp.zeros_like(dwh_ref)
            dqg_ref[...] = jnp.zeros_like(dqg_ref)
            dkvg_ref[...] = jnp.zeros_like(dkvg_ref)

        xb = x_ref[...].astype(BF16)
        dr_b = drest_ref[...]
        dwin_ref[:, n_head:] += _dot_tn(xb, dr_b)
        gx_rest = DN_ALPHA * dr_ref[...] + _dot_nt(dr_b, win_ref[:, n_head:])

        cos, s1, s2 = _rope_tables(pos_ref[...], invf_ref[...])
        lane = lax.broadcasted_iota(jnp.int32, (tm, LANES), 1)
        c_q = ph_ref[:, :Q_LORA]
        c_kv = ph_ref[:, Q_LORA:Q_LORA + KV_LORA]
        rstd_q = lax.rsqrt(jnp.mean(c_q * c_q, axis=-1, keepdims=True) + EPS)
        rstd_kv = lax.rsqrt(jnp.mean(c_kv * c_kv, axis=-1, keepdims=True) + EPS)
        qhat = c_q * rstd_q
        kvhat = c_kv * rstd_kv
        cqn = (qhat * qg_ref[...]).astype(BF16)
        ckvn = (kvhat * kvg_ref[...]).astype(BF16)
        dkr_rot = jnp.zeros((tm, LANES), F32)
        dq_heads, dkv_heads = [], []
        for h in range(HEADS):
            dq_heads.append(_rope(jnp.transpose(dq_ref[h]) * ATTN_SCALE, cos, s1, s2, -1.0).astype(BF16))
            dk_h = dk_ref[h]
            dkv_heads.append(jnp.where(lane < NOPE, dk_h, dv_ref[h]).astype(BF16))
            dkr_rot = dkr_rot + dk_h
        dq_all = jnp.concatenate(dq_heads, axis=1)
        dkv_all = jnp.concatenate(dkv_heads, axis=1)
        dwq_all = _dot_tn(cqn, dq_all)
        dwkv_all = _dot_tn(ckvn, dkv_all)
        for h in range(HEADS):
            dwh_ref[h, :Q_LORA, :] += dwq_all[:, h * HEAD_PAD:(h + 1) * HEAD_PAD]
            dwh_ref[h, Q_LORA:, :] += dwkv_all[:, h * HEAD_PAD:(h + 1) * HEAD_PAD]
        dcqn = _dot_nt(dq_all, jnp.concatenate([wh_ref[h, :Q_LORA, :] for h in range(HEADS)], axis=1))
        dckvn = _dot_nt(dkv_all, jnp.concatenate([wh_ref[h, Q_LORA:, :] for h in range(HEADS)], axis=1))
        rot_lanes = (lane >= KR_LO) & (lane < KR_LO + ROPE)
        dkr_raw = jnp.where(rot_lanes, _rope(dkr_rot, cos, s1, s2, -1.0), 0.0)
        dqg_ref[...] += jnp.sum(dcqn * qhat, axis=0, keepdims=True)
        dkvg_ref[...] += jnp.sum(dckvn * kvhat, axis=0, keepdims=True)
        dqh = dcqn * qg_ref[...]
        dkvh = dckvn * kvg_ref[...]
        dc_q = rstd_q * (dqh - qhat * jnp.mean(dqh * qhat, axis=-1, keepdims=True))
        dc_kv = rstd_kv * (dkvh - kvhat * jnp.mean(dkvh * kvhat, axis=-1, keepdims=True))
        dh_b = jnp.concatenate([dc_q, dc_kv, dkr_raw], axis=-1).astype(BF16)
        dwin_ref[:, :n_head] += _dot_tn(xb, dh_b)
        gx_ref[...] = gx_rest + _dot_nt(dh_b, win_ref[:, :n_head])

    full = lambda a: pl.BlockSpec(a.shape, lambda i: (0,) * a.ndim)
    tile = lambda w: pl.BlockSpec((tm, w), lambda i: (i, 0))
    heads = pl.BlockSpec((HEADS, tm, HEAD_PAD), lambda i: (0, i, 0))
    acc = lambda shape: (pl.BlockSpec(shape, lambda i: (0,) * len(shape)), jax.ShapeDtypeStruct(shape, F32))
    accs = [acc(wp_in.shape), acc(w_heads.shape), acc((1, Q_LORA)), acc((1, KV_LORA))]
    return pl.pallas_call(
        body, name="bwd_tail", grid=(t // tm,),
        in_specs=[pl.BlockSpec((HEADS, HEAD_PAD, tm), lambda i: (0, 0, i)), heads, heads, tile(n_head),
                  pl.BlockSpec((tm, 1), lambda i: (i, 0)), full(invf_row), full(w_heads), full(q_g), full(kv_g),
                  tile(D_MODEL), tile(D_MODEL), tile(drest.shape[1]), full(wp_in)],
        out_specs=[tile(D_MODEL)] + [a[0] for a in accs],
        out_shape=[jax.ShapeDtypeStruct((t, D_MODEL), F32)] + [a[1] for a in accs],
        compiler_params=_cparams(("arbitrary",), vmem_limit=BWD_TAIL_VMEM_LIMIT),
    )(dq, dk, dv, proj, pos_col, invf_row, w_heads, q_g, kv_g, x, dr, drest, wp_in)


def _adam(parts, w, m, v, *, name, tile_rows):
    n, rows, cols = parts.shape

    def body(p_ref, w_ref, m_ref, v_ref, g_ref, d_ref, nm_ref, nv_ref):
        g = p_ref[0].astype(F32)
        for s in range(1, n):
            g = g + p_ref[s].astype(F32)
        m_new = ADAM_B1 * m_ref[...] + (1.0 - ADAM_B1) * g
        v_new = ADAM_B2 * v_ref[...] + (1.0 - ADAM_B2) * (g * g)
        m_hat = m_new / (1.0 - ADAM_B1 ** ADAM_STEP)
        v_hat = v_new / (1.0 - ADAM_B2 ** ADAM_STEP)
        g_ref[...] = g
        d_ref[...] = -ADAM_LR * (m_hat / (jnp.sqrt(v_hat) + ADAM_EPS) + ADAM_WD * w_ref[...])
        nm_ref[...] = m_new
        nv_ref[...] = v_new

    flat = pl.BlockSpec((tile_rows, cols), lambda i: (i, 0))
    shape = jax.ShapeDtypeStruct((rows, cols), F32)
    return pl.pallas_call(
        body, name=name, grid=(rows // tile_rows,),
        in_specs=[pl.BlockSpec((n, tile_rows, cols), lambda i: (0, i, 0)), flat, flat, flat],
        out_specs=[flat] * 4, out_shape=[shape] * 4,
        compiler_params=_cparams(("arbitrary",)),
    )(parts, w, m, v)


SMALL_SIZES = (Q_LORA, KV_LORA, G_WIDTH, G_WIDTH, HEADS * CHUNK, D_MODEL, D_MODEL)


def _pack_small(vals, last=None):
    flat = jnp.concatenate([v.reshape(-1) for v in vals])
    pad = SMALL_LEN - flat.shape[0]
    if last is None:
        return jnp.pad(flat, (0, pad))
    return jnp.concatenate([flat, jnp.zeros((pad - 1,), F32), last.reshape(1)])


def _unpack_small(flat):
    out, at = [], 0
    for n in SMALL_SIZES:
        out.append(flat[at:at + n])
        at += n
    out[4] = out[4].reshape(HEADS, CHUNK)
    return out


UQ_SHARD = HEADS * (NOPE + ROPE) // N_DEV
HEAD_ROWS = Q_LORA + KV_LORA
MIXED_ROWS = HEAD_ROWS + CHUNK + SMALL_LEN // N_DEV // LANES


def _head_slab(w_uq_shard, w_ukv_shard):
    return jnp.concatenate([jnp.pad(w_uq_shard, ((0, 0), (0, LANES - UQ_SHARD))), w_ukv_shard])


IN_SHARD = D_IN // N_DEV


def _w_in_pieces():
    split = Q_LORA + KV_LORA
    moves = ((0, split, 0), (split, split + ROPE, KR_LO), (split + ROPE, D_IN, LANES - ROPE))
    pieces = []
    for s in range(N_DEV):
        lo, hi = s * IN_SHARD, (s + 1) * IN_SHARD
        for a, b, shift in moves:
            a, b = max(a, lo), min(b, hi)
            if a < b:
                pieces.append((s, a - lo, a + shift, b - a))
    return pieces


def _w_in_shards(dwp_in):
    tr = TOKEN_TILE
    by_shard = [[p for p in _w_in_pieces() if p[0] == s] for s in range(N_DEV)]

    def body(w_ref, o_ref):
        for s, pieces in enumerate(by_shard):
            parts = [w_ref[:, dst:dst + width] for _, _, dst, width in pieces]
            o_ref[s] = parts[0] if len(parts) == 1 else jnp.concatenate(parts, axis=1)

    return pl.pallas_call(
        body, name="w_in_split", grid=(D_MODEL // tr,),
        in_specs=[pl.BlockSpec((tr, D_IN_PAD), lambda i: (i, 0))],
        out_specs=pl.BlockSpec((N_DEV, tr, IN_SHARD), lambda i: (0, i, 0)),
        out_shape=jax.ShapeDtypeStruct((N_DEV, D_MODEL, IN_SHARD), dwp_in.dtype),
        compiler_params=_cparams(("arbitrary",)),
    )(dwp_in)


def kernel(x, positions, w_in, q_norm_g, w_uq, kv_norm_g, w_ukv, sgu_norm_g, sgu_norm_b, w_spatial, b_spatial, w_out, ln_g, ln_b, loss_target, m_w_in, m_q_norm_g, m_w_uq, m_kv_norm_g, m_w_ukv, m_sgu_norm_g, m_sgu_norm_b, m_w_spatial, m_b_spatial, m_w_out, m_ln_g, m_ln_b, v_w_in, v_q_norm_g, v_w_uq, v_kv_norm_g, v_w_ukv, v_sgu_norm_g, v_sgu_norm_b, v_w_spatial, v_b_spatial, v_w_out, v_ln_g, v_ln_b):
    me = 4 * lax.axis_index("x") + 2 * lax.axis_index("y") + lax.axis_index("c")
    seq = x.shape[1]
    x2 = x.reshape(seq, D_MODEL)
    tgt2 = loss_target.reshape(seq, D_MODEL)
    pos_col = positions.reshape(seq, 1)

    w_in_shards, w_out_shards, w_heads = _gather_two_level(
        [w_in.astype(BF16), w_out.astype(BF16), _head_slab(w_uq, w_ukv).astype(BF16)],
        name="wgather")
    (loss_part, grad_x, d_in, d_heads, d_out, d_ws, d_bs_t, d_lng, d_lnb, d_sgug, d_sgub, d_qg, d_kvg) = _local_step(
        x2, tgt2, pos_col, w_in_shards, w_heads, w_out_shards.reshape(D_MODEL, D_MODEL), q_norm_g, kv_norm_g,
        sgu_norm_g, sgu_norm_b, w_spatial, b_spatial, ln_g, ln_b)

    small_part = _pack_small([d_qg, d_kvg, d_sgug, d_sgub, d_bs_t[:, :HEADS].T, d_lng, d_lnb], last=loss_part[0, :1])
    mixed = jnp.concatenate([d_heads, d_ws, small_part.reshape(N_DEV, -1, LANES)], axis=1)
    by_chip = [g.reshape((N_CHIPS, 2) + g.shape[1:])
               for g in (d_in, d_out.reshape(N_DEV, D_MODEL // N_DEV, D_MODEL), mixed)]
    from_sibling = _sibling_swap(by_chip, name="gswap")
    core = lax.axis_index("c").astype(jnp.int32).reshape(1)
    pair_sums = [_pair_sum(a, b, core, name=nm, tile_rows=tr, out_dtype=dt) for a, b, nm, tr, dt in zip(
        by_chip, from_sibling, ("gsum_in", "gsum_out", "gsum_mixed"), (TOKEN_TILE, D_MODEL // N_DEV, MIXED_ROWS),
        (BF16, BF16, F32))]
    recv_in, recv_out, recv_mixed = _chip_exchange(pair_sums, name="gexch")

    take = lambda a: lax.dynamic_index_in_dim(a, me, 0, keepdims=False)
    small_w = _pack_small([q_norm_g, kv_norm_g, sgu_norm_g, sgu_norm_b, b_spatial, ln_g, ln_b])
    small_m = _pack_small([m_q_norm_g, m_kv_norm_g, m_sgu_norm_g, m_sgu_norm_b, m_b_spatial, m_ln_g, m_ln_b])
    small_v = _pack_small([v_q_norm_g, v_kv_norm_g, v_sgu_norm_g, v_sgu_norm_b, v_b_spatial, v_ln_g, v_ln_b])
    own_mixed = lambda uq, ukv, sp, small: jnp.concatenate(
        [_head_slab(uq, ukv), take(sp), take(small.reshape(N_DEV, -1, LANES))])
    res_in = _adam(recv_in, w_in, m_w_in, v_w_in, name="adam_in", tile_rows=TOKEN_TILE)
    res_out = _adam(recv_out, w_out, m_w_out, v_w_out, name="adam_out", tile_rows=D_MODEL // N_DEV)
    res_mixed = _adam(recv_mixed, own_mixed(w_uq, w_ukv, w_spatial, small_w), own_mixed(m_w_uq, m_w_ukv, m_w_spatial, small_m),
                      own_mixed(v_w_uq, v_w_ukv, v_w_spatial, small_v), name="adam_mixed", tile_rows=MIXED_ROWS)

    rep_g, = _gather_direct([res_mixed[0][HEAD_ROWS:]], name="sgather")
    rep_pack = lambda sp, small: jnp.concatenate(
        [sp.reshape(N_DEV, CHUNK, LANES), small.reshape(N_DEV, -1, LANES)], axis=1).reshape(-1, LANES)
    _, delta_rep, m_rep, v_rep = _adam(rep_g.reshape(1, N_DEV * REP_ROWS, LANES), rep_pack(w_spatial, small_w),
                                       rep_pack(m_w_spatial, small_m), rep_pack(v_w_spatial, small_v),
                                       name="adam_rep", tile_rows=N_DEV * REP_ROWS)

    def rep_unpack(a):
        a = a.reshape(N_DEV, REP_ROWS, LANES)
        small = _unpack_small(a[:, CHUNK:].reshape(-1))
        return [small[0], small[1], small[2], small[3], a[:, :CHUNK], small[4], small[5], small[6]]

    def ordered(which, rep):
        r_qg, r_kvg, r_sg, r_sb, r_ws, r_bs, r_lg, r_lb = rep_unpack(rep)
        heads = res_mixed[which]
        return [res_in[which], r_qg, heads[:Q_LORA, :UQ_SHARD], r_kvg, heads[Q_LORA:HEAD_ROWS], r_sg, r_sb, r_ws, r_bs,
                res_out[which], r_lg, r_lb]

    loss = rep_g[N_DEV - 1, REP_ROWS - 1, LANES - 1]
    outs = [loss, grad_x.reshape(x.shape)]
    outs += ordered(0, rep_g.reshape(-1, LANES))
    outs += ordered(1, delta_rep)
    outs += ordered(2, m_rep)
    outs += ordered(3, v_rep)
    return tuple(outs)


def _local_step(x2, tgt2, pos_col, w_in_shards, w_heads, w_out_full, q_norm_g, kv_norm_g, sgu_norm_g, sgu_norm_b,
                w_spatial, b_spatial, ln_g, ln_b):
    half = jnp.arange(HALF, dtype=F32)
    inv_freq = 1.0 / (ROPE_THETA ** (half / HALF))
    invf_row = jnp.concatenate([jnp.zeros((KR_LO,), F32), inv_freq, inv_freq,
                                jnp.zeros((LANES - KR_LO - ROPE,), F32)]).reshape(1, LANES)
    tri = jnp.tril(jnp.ones((CHUNK, CHUNK), dtype=bool))
    ws_low = jnp.where(tri[None], w_spatial, 0.0).astype(BF16)
    ws_low_t = ws_low.transpose(0, 2, 1)
    bsp = jnp.repeat(b_spatial.T, G_HEAD_DIM, axis=1)
    row = lambda a: a.reshape(1, -1)

    proj, q, k, v, vt, wp_in = _fwd_proj(x2, pos_col, invf_row, w_in_shards, w_heads, row(q_norm_g), row(kv_norm_g))
    o, lse_row = _attn_fwd(q, k, vt)
    (dr, do, d_row, drest, d_out, d_ws, d_bs_t, d_lng, d_lnb, d_sgug, d_sgub, loss_part) = _mid(
        x2, tgt2, proj, o, w_out_full, ws_low, ws_low_t, bsp, row(sgu_norm_g), row(sgu_norm_b), row(ln_g), row(ln_b))
    dqt, dk, dv = _attn_bwd(q, k, v, do, lse_row, d_row)
    grad_x, dwp_in, d_heads, d_qg, d_kvg = _bwd_tail(dqt, dk, dv, proj, pos_col, invf_row, w_heads, row(q_norm_g),
                                                      row(kv_norm_g), x2, dr, drest, wp_in)
    return (loss_part, grad_x, _w_in_shards(dwp_in), d_heads, d_out, d_ws, d_bs_t, d_lng, d_lnb, d_sgug, d_sgub,
            d_qg, d_kvg)
```

```python
import math

import jax
import jax.numpy as jnp
from jax import lax
from jax.experimental import pallas as pl
from jax.experimental.pallas import tpu as pltpu

F32 = jnp.float32
BF16 = jnp.bfloat16

N_DEV = 8
D_MODEL = 1024
HEADS = 8
NOPE = 64
ROPE = 32
HALF = ROPE // 2
VDIM = 64
Q_LORA = 256
KV_LORA = 128
G_WIDTH = 512
G_HEAD_DIM = 64
CHUNK = 128
HEAD_PAD = 128
D_IN = 2464
D_IN_PAD = 2560
KR_LO = NOPE
SUM_ROW = NOPE - 1
LIVE_ROWS = slice(NOPE - 16, HEAD_PAD)
ROPE_THETA = 10000.0
DN_ALPHA = 2.0 ** 0.25
EPS = 1e-5
ATTN_SCALE = 1.0 / math.sqrt(NOPE + ROPE)
ADAM_LR, ADAM_B1, ADAM_B2, ADAM_EPS, ADAM_WD, ADAM_STEP = 0.001, 0.9, 0.999, 1e-08, 0.01, 10

LANES = 128
REP_ROWS = 136
SMALL_LEN = 8192
VMEM_LIMIT = 56 * 1024 * 1024
ATTN_BWD_VMEM_LIMIT = 61 * 1024 * 1024
BWD_TAIL_VMEM_LIMIT = 61 * 1024 * 1024

TOKEN_TILE = 256
PROJ_TILE = 512
ATTN_FWD_WIDE = 2048
ATTN_BWD_WIDE = 2048
ATTN_NARROW = 512
SOFTMAX_ROWS = 512
LOG2E = 1.4426950408889634
LN2 = 0.6931471805599453
Q_PRESCALE = ATTN_SCALE * LOG2E


def _cparams(sem=None, vmem_limit=VMEM_LIMIT):
    return pltpu.CompilerParams(dimension_semantics=sem, vmem_limit_bytes=vmem_limit)


def _dot(a, b):
    return jnp.dot(a, b, preferred_element_type=F32)


def _dot_nt(a, b):
    return lax.dot_general(a, b, (((1,), (1,)), ((), ())), preferred_element_type=F32)


def _dot_tn(a, b):
    return lax.dot_general(a, b, (((0,), (0,)), ((), ())), preferred_element_type=F32)


def _sigmoid(z):
    return 1.0 / (1.0 + jnp.exp(-z))


def _gelu(x):
    return 0.5 * x * (1.0 + lax.erf(x * 0.7071067811865476))


def _gelu_grad(x):
    cdf = 0.5 * (1.0 + lax.erf(x * 0.7071067811865476))
    return cdf + x * jnp.exp(-0.5 * x * x) * 0.3989422804014327


def _gather_direct(srcs, *, name):
    n = len(srcs)

    def body(*refs):
        src_refs, out_refs = refs[:n], refs[n:2 * n]
        send_sems, recv_sems, local_sems = refs[2 * n:]
        x, y, c = lax.axis_index("x"), lax.axis_index("y"), lax.axis_index("c")
        me = 4 * x + 2 * y + c
        mine = [pltpu.make_async_copy(src_refs[t], out_refs[t].at[me], local_sems.at[t]) for t in range(n)]
        for cp in mine:
            cp.start()
        sends, arrivals = [], []
        for k in (6, 7, 4, 5, 2, 3, 1):
            px = 1 - x if k & 4 else x
            py = 1 - y if k & 2 else y
            pc = 1 - c if k & 1 else c
            peer = 4 * px + 2 * py + pc
            for t in range(n):
                sem = (k - 1) * n + t
                cp = pltpu.make_async_remote_copy(
                    src_ref=src_refs[t], dst_ref=out_refs[t].at[me],
                    send_sem=send_sems.at[sem], recv_sem=recv_sems.at[sem],
                    device_id=(px, py, pc), device_id_type=pl.DeviceIdType.MESH)
                cp.start()
                sends.append(cp)
                arrivals.append(pltpu.make_async_remote_copy(
                    src_ref=src_refs[t], dst_ref=out_refs[t].at[peer],
                    send_sem=send_sems.at[sem], recv_sem=recv_sems.at[sem],
                    device_id=(x, y, c), device_id_type=pl.DeviceIdType.MESH))
        for cp in arrivals:
            cp.wait_recv()
        for cp in sends:
            cp.wait_send()
        for cp in mine:
            cp.wait()

    hbm = pl.BlockSpec(memory_space=pl.ANY)
    return pl.pallas_call(
        body, name=name,
        out_shape=[jax.ShapeDtypeStruct((N_DEV,) + s.shape, s.dtype) for s in srcs],
        in_specs=[hbm] * n, out_specs=[hbm] * n,
        scratch_shapes=[pltpu.SemaphoreType.DMA(((N_DEV - 1) * n,)), pltpu.SemaphoreType.DMA(((N_DEV - 1) * n,)),
                        pltpu.SemaphoreType.DMA((n,))],
    )(*srcs)


def _gather_two_level(srcs, *, name):
    n = len(srcs)

    def body(*refs):
        src_refs, out_refs = refs[:n], refs[n:2 * n]
        send_sems, recv_sems, local_sems = refs[2 * n:]
        x, y, c = lax.axis_index("x"), lax.axis_index("y"), lax.axis_index("c")
        me, sibling = (x, y, c), (x, y, 1 - c)
        chips = [(1 - x, 1 - y), (1 - x, y), (x, 1 - y)]
        index = lambda px, py, pc: 4 * px + 2 * py + pc

        def copy(k, t, block, to, src=None):
            place = out_refs[t].at[index(*block)]
            return pltpu.make_async_remote_copy(
                src_ref=place if src is None else src, dst_ref=place,
                send_sem=send_sems.at[k * n + t], recv_sem=recv_sems.at[k * n + t],
                device_id=to, device_id_type=pl.DeviceIdType.MESH)

        mine = [pltpu.make_async_copy(src_refs[t], out_refs[t].at[index(*me)], local_sems.at[t]) for t in range(n)]
        for cp in mine:
            cp.start()
        first = [copy(1 + j, t, me, (*chip, c), src=src_refs[t]) for j, chip in enumerate(chips) for t in range(n)]
        first += [copy(0, t, me, sibling, src=src_refs[t]) for t in range(n)]
        for cp in first:
            cp.start()
        passed = []
        for j, chip in enumerate(chips):
            for t in range(n):
                copy(1 + j, t, (*chip, c), me).wait_recv()
                cp = copy(4 + j, t, (*chip, c), sibling)
                cp.start()
                passed.append(cp)
        for t in range(n):
            copy(0, t, sibling, me).wait_recv()
        for j, chip in enumerate(chips):
            for t in range(n):
                copy(4 + j, t, (*chip, 1 - c), me).wait_recv()
        for cp in first + passed:
            cp.wait_send()
        for cp in mine:
            cp.wait()

    hbm = pl.BlockSpec(memory_space=pl.ANY)
    return pl.pallas_call(
        body, name=name,
        out_shape=[jax.ShapeDtypeStruct((N_DEV,) + s.shape, s.dtype) for s in srcs],
        in_specs=[hbm] * n, out_specs=[hbm] * n,
        scratch_shapes=[pltpu.SemaphoreType.DMA((7 * n,)), pltpu.SemaphoreType.DMA((7 * n,)),
                        pltpu.SemaphoreType.DMA((n,))],
    )(*srcs)


N_CHIPS = N_DEV // 2


def _sibling_swap(srcs, *, name):
    n = len(srcs)

    def body(*refs):
        src_refs, out_refs = refs[:n], refs[n:2 * n]
        send_sems, recv_sems = refs[2 * n:]
        x, y, c = lax.axis_index("x"), lax.axis_index("y"), lax.axis_index("c")
        sends = []
        for chip in range(N_CHIPS):
            for t in range(n):
                cp = pltpu.make_async_remote_copy(
                    src_ref=src_refs[t].at[chip, 1 - c], dst_ref=out_refs[t].at[chip],
                    send_sem=send_sems.at[chip * n + t], recv_sem=recv_sems.at[chip * n + t],
                    device_id=(x, y, 1 - c), device_id_type=pl.DeviceIdType.MESH)
                cp.start()
                sends.append(cp)
        for cp in sends:
            cp.wait_recv()
        for cp in sends:
            cp.wait_send()

    hbm = pl.BlockSpec(memory_space=pl.ANY)
    return pl.pallas_call(
        body, name=name,
        out_shape=[jax.ShapeDtypeStruct((N_CHIPS,) + s.shape[2:], s.dtype) for s in srcs],
        in_specs=[hbm] * n, out_specs=[hbm] * n,
        scratch_shapes=[pltpu.SemaphoreType.DMA((N_CHIPS * n,)), pltpu.SemaphoreType.DMA((N_CHIPS * n,))],
    )(*srcs)


def _pair_sum(mine, theirs, core, *, name, tile_rows, out_dtype):
    _, _, rows, cols = mine.shape

    def body(core_ref, a_ref, b_ref, o_ref):
        o_ref[...] = (a_ref[0] + b_ref[...]).astype(out_dtype)

    return pl.pallas_call(
        body, name=name,
        grid_spec=pltpu.PrefetchScalarGridSpec(
            num_scalar_prefetch=1, grid=(N_CHIPS, rows // tile_rows),
            in_specs=[pl.BlockSpec((1, 1, tile_rows, cols), lambda q, r, core_ref: (q, core_ref[0], r, 0)),
                      pl.BlockSpec((1, tile_rows, cols), lambda q, r, core_ref: (q, r, 0))],
            out_specs=pl.BlockSpec((1, tile_rows, cols), lambda q, r, core_ref: (q, r, 0))),
        out_shape=jax.ShapeDtypeStruct((N_CHIPS, rows, cols), out_dtype),
        compiler_params=_cparams(("arbitrary", "arbitrary")),
    )(core, mine, theirs)


def _chip_exchange(srcs, *, name):
    n = len(srcs)

    def body(*refs):
        src_refs, out_refs = refs[:n], refs[n:2 * n]
        send_sems, recv_sems, local_sems = refs[2 * n:]
        x, y, c = lax.axis_index("x"), lax.axis_index("y"), lax.axis_index("c")
        my_chip = 2 * x + y
        mine = [pltpu.make_async_copy(src_refs[t].at[my_chip], out_refs[t].at[my_chip], local_sems.at[t])
                for t in range(n)]
        for cp in mine:
            cp.start()
        sends, arrivals = [], []
        for k in (3, 2, 1):
            px = 1 - x if k & 2 else x
            py = 1 - y if k & 1 else y
            peer_chip = 2 * px + py
            for t in range(n):
                sem = (k - 1) * n + t
                cp = pltpu.make_async_remote_copy(
                    src_ref=src_refs[t].at[peer_chip], dst_ref=out_refs[t].at[my_chip],
                    send_sem=send_sems.at[sem], recv_sem=recv_sems.at[sem],
                    device_id=(px, py, c), device_id_type=pl.DeviceIdType.MESH)
                cp.start()
                sends.append(cp)
                arrivals.append(pltpu.make_async_remote_copy(
                    src_ref=src_refs[t].at[peer_chip], dst_ref=out_refs[t].at[peer_chip],
                    send_sem=send_sems.at[sem], recv_sem=recv_sems.at[sem],
                    device_id=(x, y, c), device_id_type=pl.DeviceIdType.MESH))
        for cp in arrivals:
            cp.wait_recv()
        for cp in sends:
            cp.wait_send()
        for cp in mine:
            cp.wait()

    hbm = pl.BlockSpec(memory_space=pl.ANY)
    return pl.pallas_call(
        body, name=name,
        out_shape=[jax.ShapeDtypeStruct(s.shape, s.dtype) for s in srcs],
        in_specs=[hbm] * n, out_specs=[hbm] * n,
        scratch_shapes=[pltpu.SemaphoreType.DMA((3 * n,)), pltpu.SemaphoreType.DMA((3 * n,)),
                        pltpu.SemaphoreType.DMA((n,))],
    )(*srcs)


def _rope_tables(pos_col, invf_row):
    ang = pos_col.astype(F32) * invf_row
    lane = lax.broadcasted_iota(jnp.int32, ang.shape, 1)
    cos, sin = jnp.cos(ang), jnp.sin(ang)
    first = (lane >= KR_LO) & (lane < KR_LO + HALF)
    second = (lane >= KR_LO + HALF) & (lane < KR_LO + ROPE)
    return cos, jnp.where(first, sin, 0.0), jnp.where(second, sin, 0.0)


def _rope(t, cos, sin_first, sin_second, sign):
    up = pltpu.roll(t, LANES - HALF, 1)
    down = pltpu.roll(t, HALF, 1)
    return t * cos - sign * (up * sin_first) + sign * (down * sin_second)


def _fwd_proj(x, pos_col, invf_row, w_in_shards, w_heads, q_g, kv_g):
    t = x.shape[0]
    tm = PROJ_TILE

    def body(x_ref, pos_ref, invf_ref, sh_ref, wh_ref, qg_ref, kvg_ref,
             proj_ref, q_ref, k_ref, v_ref, vt_ref, win_ref):
        @pl.when(pl.program_id(0) == 0)
        def _():
            win_ref[...] = jnp.zeros_like(win_ref)
            for s, src, dst, width in _w_in_pieces():
                win_ref[:, dst:dst + width] = sh_ref[s, :, src:src + width]

        proj = _dot(x_ref[...].astype(BF16), win_ref[...])
        proj_ref[...] = proj
        c_q = proj[:, :Q_LORA]
        c_kv = proj[:, Q_LORA:Q_LORA + KV_LORA]
        kr_raw = proj[:, Q_LORA + KV_LORA:Q_LORA + KV_LORA + LANES]
        cqn = (c_q * lax.rsqrt(jnp.mean(c_q * c_q, axis=-1, keepdims=True) + EPS) * qg_ref[...]).astype(BF16)
        ckvn = (c_kv * lax.rsqrt(jnp.mean(c_kv * c_kv, axis=-1, keepdims=True) + EPS) * kvg_ref[...]).astype(BF16)
        cos, s1, s2 = _rope_tables(pos_ref[...], invf_ref[...])
        kr = _rope(kr_raw, cos, s1, s2, 1.0)
        lane = lax.broadcasted_iota(jnp.int32, (tm, HEAD_PAD), 1)
        q_all = _dot(cqn, jnp.concatenate([wh_ref[h, :Q_LORA, :] for h in range(HEADS)], axis=1))
        kv_all = _dot(ckvn, jnp.concatenate([wh_ref[h, Q_LORA:, :] for h in range(HEADS)], axis=1))
        for h in range(HEADS):
            q_h = q_all[:, h * HEAD_PAD:(h + 1) * HEAD_PAD]
            kv_h = kv_all[:, h * HEAD_PAD:(h + 1) * HEAD_PAD]
            q_ref[h] = (_rope(q_h, cos, s1, s2, 1.0) * Q_PRESCALE).astype(BF16)
            k_ref[h] = jnp.where(lane < NOPE, kv_h, kr).astype(BF16)
            v_ref[h] = kv_h.astype(BF16)
            vt_ref[h] = jnp.transpose(jnp.where(lane == SUM_ROW, 1.0, kv_h)).astype(BF16)

    full = lambda a: pl.BlockSpec(a.shape, lambda i: (0,) * a.ndim)
    head_spec = pl.BlockSpec((HEADS, tm, HEAD_PAD), lambda i: (0, i, 0))
    head_shape = jax.ShapeDtypeStruct((HEADS, t, HEAD_PAD), BF16)
    return pl.pallas_call(
        body, name="fwd_proj", grid=(t // tm,),
        in_specs=[pl.BlockSpec((tm, D_MODEL), lambda i: (i, 0)), pl.BlockSpec((tm, 1), lambda i: (i, 0)),
                  full(invf_row), full(w_in_shards), full(w_heads), full(q_g), full(kv_g)],
        out_specs=[pl.BlockSpec((tm, D_IN_PAD), lambda i: (i, 0)), head_spec, head_spec, head_spec,
                   pl.BlockSpec((HEADS, HEAD_PAD, tm), lambda i: (0, 0, i)),
                   pl.BlockSpec((D_MODEL, D_IN_PAD), lambda i: (0, 0))],
        out_shape=[jax.ShapeDtypeStruct((t, D_IN_PAD), F32), head_shape, head_shape, head_shape,
                   jax.ShapeDtypeStruct((HEADS, HEAD_PAD, t), BF16),
                   jax.ShapeDtypeStruct((D_MODEL, D_IN_PAD), w_in_shards.dtype)],
        compiler_params=_cparams(("arbitrary",)),
    )(x, pos_col, invf_row, w_in_shards, w_heads, q_g, kv_g)


def _attn_fwd(q, k, vt):
    t = q.shape[1]
    bq, bk = ATTN_FWD_WIDE, ATTN_NARROW
    n_diag = bq // bk
    chunk = SOFTMAX_ROWS

    def body(q_ref, k_ref, vt_ref, o_ref, lse_ref, s0, s1, p0, p1, x0, x1, m_scr, a_scr, acc_scr):
        i = pl.program_id(1)
        at = lambda j: pl.ds(pl.multiple_of(j * bk, bk), bk)

        def exp_pass(s_in, block_max, p_out, diagonal=False, cols=slice(None)):
            width = bq if cols == slice(None) else cols.stop - cols.start

            def load(r):
                s = s_in[r:r + chunk, cols]
                if diagonal:
                    key = lax.broadcasted_iota(jnp.int32, (chunk, width), 0) + r
                    qry = lax.broadcasted_iota(jnp.int32, (chunk, width), 1)
                    s = jnp.where(qry >= key, s, -jnp.inf)
                return s

            if diagonal:
                block_max = jnp.max(load(0), axis=0, keepdims=True)
                for r in range(chunk, bk, chunk):
                    block_max = jnp.maximum(block_max, jnp.max(load(r), axis=0, keepdims=True))
            m_old = m_scr[:, cols]
            m_new = jnp.maximum(m_old, block_max)
            alpha = jnp.exp2(m_old - m_new)
            for r in range(0, bk, chunk):
                p_out[r:r + chunk, cols] = jnp.exp2(load(r) - m_new).astype(BF16)
            m_scr[:, cols] = m_new
            return alpha

        def scores(j, s_out, x_out):
            s = _dot_nt(k_ref[0, at(j), :], q_ref[0])
            s_out[...] = s
            x_out[...] = jnp.max(s, axis=0, keepdims=True)

        def value_product(j, p_in):
            return _dot(vt_ref[0, LIVE_ROWS, at(j)], p_in[...])

        def one_pass(j, s_in, x_in, s_out, x_out, p_prev, p_cur):
            scores(j + 1, s_out, x_out)
            acc_scr[...] = a_scr[...] * acc_scr[...] + value_product(jnp.maximum(j - 1, 0), p_prev)
            a_scr[...] = exp_pass(s_in, x_in[...], p_cur)

        scores(0, s0, x0)
        p1[...] = jnp.zeros_like(p1)
        a_scr[...] = jnp.ones_like(a_scr)
        m_scr[...] = jnp.full(m_scr.shape, -jnp.inf, F32)
        acc_scr[...] = jnp.zeros_like(acc_scr)

        def two_passes(n, _):
            one_pass(2 * n, s0, x0, s1, x1, p1, p0)
            one_pass(2 * n + 1, s1, x1, s0, x0, p0, p1)
            return 0

        lax.fori_loop(0, (n_diag // 2) * i, two_passes, 0)
        d = n_diag * i
        alpha, p_prev, cols = a_scr[...], p1, slice(0, bq)
        for u in range(n_diag + 1):
            s_in, s_next, p_cur = (s0, s1, p0) if u % 2 == 0 else (s1, s0, p1)
            if u + 1 < n_diag:
                ahead = slice((u + 1) * bk, bq)
                s_next[:, ahead] = _dot_nt(k_ref[0, at(d + u + 1), :], q_ref[0, ahead, :])
            acc_scr[:, cols] = alpha * acc_scr[:, cols] + _dot(vt_ref[0, LIVE_ROWS, at(jnp.maximum(d + u - 1, 0))],
                                                               p_prev[:, cols])
            if u < n_diag:
                cols = slice(u * bk, bq)
                alpha = exp_pass(s_in, None, p_cur, diagonal=True, cols=cols)
                p_prev = p_cur
        denom = acc_scr[SUM_ROW - LIVE_ROWS.start:NOPE - LIVE_ROWS.start, :]
        o = jnp.transpose(acc_scr[NOPE - LIVE_ROWS.start:, :] / denom)
        o_ref[0] = jnp.concatenate([jnp.zeros_like(o), o], axis=1)
        lse_ref[0] = m_scr[...] + jnp.log2(denom)

    tile = lambda dtype: pltpu.VMEM((bk, bq), dtype)
    stat = pltpu.VMEM((1, bq), F32)
    return pl.pallas_call(
        body, name="attn_fwd", grid=(HEADS, t // bq),
        in_specs=[pl.BlockSpec((1, bq, HEAD_PAD), lambda h, i: (h, i, 0)),
                  pl.BlockSpec((1, t, HEAD_PAD), lambda h, i: (h, 0, 0)),
                  pl.BlockSpec((1, HEAD_PAD, t), lambda h, i: (h, 0, 0))],
        out_specs=[pl.BlockSpec((1, bq, HEAD_PAD), lambda h, i: (h, i, 0)),
                   pl.BlockSpec((1, 1, bq), lambda h, i: (h, 0, i))],
        out_shape=[jax.ShapeDtypeStruct((HEADS, t, HEAD_PAD), F32), jax.ShapeDtypeStruct((HEADS, 1, t), F32)],
        scratch_shapes=[tile(F32), tile(F32), tile(BF16), tile(BF16), stat, stat, stat, stat,
                        pltpu.VMEM((HEAD_PAD - LIVE_ROWS.start, bq), F32)],
        compiler_params=_cparams(("arbitrary", "arbitrary")),
    )(q, k, vt)


def _mid(x, target, proj, ol, w_out, ws_low, ws_low_t, bsp, sgu_g, sgu_b, ln_g, ln_b):
    t = x.shape[0]
    tm = TOKEN_TILE
    n_steps = t // tm

    def body(x_ref, tgt_ref, za_ref, u_ref, v_ref, zb_ref, ol_ref, wout_ref, ws_ref, wst_ref, bsp_ref,
             sg_ref, sb_ref, lg_ref, lb_ref,
             dr_ref, do_ref, drow_ref, drest_ref, dwout_ref, dws_ref, dbs_ref, dlg_ref, dlb_ref, dsg_ref, dsb_ref,
             loss_ref, dbsp_acc):
        step = pl.program_id(0)

        @pl.when(step == 0)
        def _():
            dwout_ref[...] = jnp.zeros_like(dwout_ref)
            dws_ref[...] = jnp.zeros_like(dws_ref)
            dbs_ref[...] = jnp.zeros_like(dbs_ref)
            dlg_ref[...] = jnp.zeros_like(dlg_ref)
            dlb_ref[...] = jnp.zeros_like(dlb_ref)
            dsg_ref[...] = jnp.zeros_like(dsg_ref)
            dsb_ref[...] = jnp.zeros_like(dsb_ref)
            loss_ref[...] = jnp.zeros_like(loss_ref)
            dbsp_acc[...] = jnp.zeros_like(dbsp_acc)

        n_chunks = tm // CHUNK
        groups = G_WIDTH // LANES

        def side_by_side(a):
            return [jnp.concatenate([a[c * CHUNK:(c + 1) * CHUNK, g * LANES:(g + 1) * LANES] for c in range(n_chunks)],
                                    axis=1) for g in range(groups)]

        def by_chunk(wide):
            return jnp.concatenate([jnp.concatenate([wide[g][:, c * LANES:(c + 1) * LANES] for g in range(groups)], axis=1)
                                    for c in range(n_chunks)], axis=0)

        def own_lanes(h):
            lane = lax.broadcasted_iota(jnp.int32, (CHUNK, n_chunks * LANES), 1)
            return (lane % LANES) // G_HEAD_DIM == h % 2

        def spatial(w_ref, wide):
            return [sum(jnp.where(own_lanes(h), _dot(w_ref[h], wide[g]), 0.0) for h in (2 * g, 2 * g + 1))
                    for g in range(groups)]

        attn = jnp.concatenate([ol_ref[h][:, NOPE:] for h in range(HEADS)], axis=-1)
        za = za_ref[...]
        sig_a = _sigmoid(za)
        silu_a = za * sig_a
        out_a = attn * silu_a
        u = u_ref[...]
        ug = _gelu(u)
        vpre = v_ref[...]
        gv = _gelu(vpre)
        mu_v = jnp.mean(gv, axis=-1, keepdims=True)
        cen_v = gv - mu_v
        rstd_v = lax.rsqrt(jnp.mean(cen_v * cen_v, axis=-1, keepdims=True) + EPS)
        vhat = cen_v * rstd_v
        vg = vhat * sg_ref[...] + sb_ref[...]
        vg_b = vg.astype(BF16)
        sv = by_chunk(spatial(ws_ref, side_by_side(vg_b))) + jnp.tile(bsp_ref[...], (n_chunks, 1))
        sgu = ug * sv
        zb = zb_ref[...]
        sig_b = _sigmoid(zb)
        silu_b = zb * sig_b
        out_b = sgu * silu_b
        merged = jnp.concatenate([out_a, out_b], axis=-1).astype(BF16)
        r = DN_ALPHA * x_ref[...] + _dot(merged, wout_ref[...])
        mu = jnp.mean(r, axis=-1, keepdims=True)
        cen = r - mu
        rstd = lax.rsqrt(jnp.mean(cen * cen, axis=-1, keepdims=True) + EPS)
        xhat = cen * rstd
        hout = xhat * lg_ref[...] + lb_ref[...]
        err = hout - tgt_ref[...]
        row_loss = jnp.mean(err * err, axis=-1, keepdims=True)
        loss_ref[...] += jnp.broadcast_to(0.5 * jnp.sum(row_loss, axis=0, keepdims=True), loss_ref.shape)

        dh = err * (1.0 / D_MODEL)
        dlg_ref[...] += jnp.sum(dh * xhat, axis=0, keepdims=True)
        dlb_ref[...] += jnp.sum(dh, axis=0, keepdims=True)
        dxhat = dh * lg_ref[...]
        dr = rstd * (dxhat - jnp.mean(dxhat, axis=-1, keepdims=True)
                     - xhat * jnp.mean(dxhat * xhat, axis=-1, keepdims=True))
        dr_ref[...] = dr
        dr_b = dr.astype(BF16)
        dwout_ref[...] += _dot_tn(merged, dr_b)
        dmerged = _dot_nt(dr_b, wout_ref[...])
        d_out_a = dmerged[:, :G_WIDTH]
        d_out_b = dmerged[:, G_WIDTH:]
        dattn = d_out_a * silu_a
        for h in range(HEADS):
            do_h = dattn[:, h * VDIM:(h + 1) * VDIM]
            do_ref[h] = jnp.concatenate([jnp.zeros((tm, NOPE), F32), do_h], axis=-1).astype(BF16)
        feature = lax.broadcasted_iota(jnp.int32, (G_WIDTH, LANES), 0) // VDIM
        column = lax.broadcasted_iota(jnp.int32, (G_WIDTH, LANES), 1)
        head_sums = jnp.dot(dattn * attn, jnp.where(feature == column, 1.0, 0.0).astype(F32),
                            preferred_element_type=F32, precision=lax.Precision.HIGH)
        dsums_t = jnp.transpose(head_sums)
        for h in range(HEADS):
            drow_ref[h] = dsums_t[h:h + 1, :]
        dza = d_out_a * attn * (sig_a * (1.0 + za * (1.0 - sig_a)))
        dsgu = d_out_b * silu_b
        dzb = d_out_b * sgu * (sig_b * (1.0 + zb * (1.0 - sig_b)))
        du = dsgu * sv * _gelu_grad(u)
        dsv = dsgu * ug
        dsv_b = dsv.astype(BF16)
        for cix in range(n_chunks):
            dbsp_acc[...] += dsv[cix * CHUNK:(cix + 1) * CHUNK, :]
        dsv_wide, vg_wide = side_by_side(dsv_b), side_by_side(vg_b)
        dvg = by_chunk(spatial(wst_ref, dsv_wide))
        for h in range(HEADS):
            mine = jnp.where(own_lanes(h), dsv_wide[h // 2], jnp.zeros_like(dsv_wide[h // 2]))
            dws_ref[h] += _dot_nt(mine, vg_wide[h // 2])
        dsg_ref[...] += jnp.sum(dvg * vhat, axis=0, keepdims=True)
        dsb_ref[...] += jnp.sum(dvg, axis=0, keepdims=True)
        dvhat = dvg * sg_ref[...]
        dgv = rstd_v * (dvhat - jnp.mean(dvhat, axis=-1, keepdims=True)
                        - vhat * jnp.mean(dvhat * vhat, axis=-1, keepdims=True))
        dv = dgv * _gelu_grad(vpre)
        drest_ref[...] = jnp.concatenate([dza, du, dv, dzb], axis=-1).astype(BF16)

        @pl.when(step == n_steps - 1)
        def _():
            tri = (lax.broadcasted_iota(jnp.int32, (CHUNK, CHUNK), 0)
                   >= lax.broadcasted_iota(jnp.int32, (CHUNK, CHUNK), 1))
            for h in range(HEADS):
                dws_ref[h] = jnp.where(tri, dws_ref[h], 0.0)
            tot = dbsp_acc[...]
            lane = lax.broadcasted_iota(jnp.int32, (CHUNK, LANES), 1)
            dbs = jnp.zeros((CHUNK, LANES), F32)
            for h in range(HEADS):
                head_sum = jnp.sum(tot[:, h * G_HEAD_DIM:(h + 1) * G_HEAD_DIM], axis=-1, keepdims=True)
                dbs = jnp.where(lane == h, head_sum, dbs)
            dbs_ref[...] = dbs

    full = lambda a: pl.BlockSpec(a.shape, lambda i: (0,) * a.ndim)
    tile = lambda w, j=0: pl.BlockSpec((tm, w), lambda i, j=j: (i, j))
    heads = pl.BlockSpec((HEADS, tm, HEAD_PAD), lambda i: (0, i, 0))
    acc = lambda shape: (pl.BlockSpec(shape, lambda i: (0,) * len(shape)), jax.ShapeDtypeStruct(shape, F32))
    accs = [acc((D_MODEL, D_MODEL)), acc((HEADS, CHUNK, CHUNK)), acc((CHUNK, LANES)), acc((1, D_MODEL)),
            acc((1, D_MODEL)), acc((1, G_WIDTH)), acc((1, G_WIDTH)), acc((1, LANES))]
    return pl.pallas_call(
        body, name="mid", grid=(n_steps,),
        in_specs=[tile(D_MODEL), tile(D_MODEL), tile(G_WIDTH, 1), tile(G_WIDTH, 2), tile(G_WIDTH, 3), tile(G_WIDTH, 4),
                  heads, full(w_out), full(ws_low), full(ws_low_t), full(bsp), full(sgu_g), full(sgu_b),
                  full(ln_g), full(ln_b)],
        out_specs=[tile(D_MODEL), heads, pl.BlockSpec((HEADS, 1, tm), lambda i: (0, 0, i)), tile(4 * G_WIDTH)]
        + [a[0] for a in accs],
        out_shape=[jax.ShapeDtypeStruct((t, D_MODEL), F32), jax.ShapeDtypeStruct((HEADS, t, HEAD_PAD), BF16),
                   jax.ShapeDtypeStruct((HEADS, 1, t), F32), jax.ShapeDtypeStruct((t, 4 * G_WIDTH), BF16)]
        + [a[1] for a in accs],
        scratch_shapes=[pltpu.VMEM((CHUNK, G_WIDTH), F32)],
        compiler_params=_cparams(("arbitrary",)),
    )(x, target, proj, proj, proj, proj, ol, w_out, ws_low, ws_low_t, bsp, sgu_g, sgu_b, ln_g, ln_b)


def _attn_bwd(q, k, v, do, lse_row, d_row):
    t = q.shape[1]
    bk, bq = ATTN_BWD_WIDE, ATTN_NARROW
    n_diag = bk // bq
    half = bq // 2
    last = t // bq - 1
    chunk = SOFTMAX_ROWS

    def body(q_ref, k_ref, v_ref, do_ref, lse_ref, drow_ref, dqt_ref, dk_ref, dv_ref,
             s0, s1, e0, e1, p0, p1, g0, g1, kt_scr):
        j = pl.program_id(1)
        at = lambda i: pl.ds(pl.multiple_of(i * bq, bq), bq)

        @pl.when(j == 0)
        def _():
            dqt_ref[...] = jnp.zeros_like(dqt_ref)

        kt_scr[...] = jnp.transpose(k_ref[0].astype(F32)).astype(BF16)
        dk_ref[...] = jnp.zeros_like(dk_ref)
        dv_ref[...] = jnp.zeros_like(dv_ref)

        whole_tile = ((slice(0, bk), slice(0, bq)),)

        def queries(i, lanes):
            return pl.ds(pl.multiple_of(i * bq + lanes.start, half), lanes.stop - lanes.start)

        def products(i, s_out, e_out, areas=whole_tile):
            i = jnp.minimum(i, last)
            for keys, lanes in areas:
                s_out[keys, lanes] = _dot_nt(k_ref[0, keys, :], q_ref[0, queries(i, lanes), :])
                e_out[keys, lanes] = _dot_nt(v_ref[0, keys, :], do_ref[0, queries(i, lanes), :])

        def gradients(i, p_in, g_in, areas=whole_tile):
            for keys, lanes in areas:
                dv_ref[0, keys, :] += _dot(p_in[keys, lanes], do_ref[0, queries(i, lanes), :])
                dk_ref[0, keys, :] += _dot(g_in[keys, lanes], q_ref[0, queries(i, lanes), :])
                dqt_ref[0, :, queries(i, lanes)] += _dot(kt_scr[:, keys], g_in[keys, lanes])

        def elementwise(i, s_in, e_in, p_out, g_out, qry0=None, areas=whole_tile):
            for keys, lanes in areas:
                width = lanes.stop - lanes.start
                step = chunk if qry0 is None else half
                lse = lse_ref[0, :, queries(i, lanes)]
                dsum = drow_ref[0, :, queries(i, lanes)]
                for r in range(keys.start, keys.stop, step):
                    p = jnp.exp2(s_in[r:r + step, lanes] - lse)
                    if qry0 is not None:
                        key = lax.broadcasted_iota(jnp.int32, (step, width), 0) + r
                        qry = lax.broadcasted_iota(jnp.int32, (step, width), 1) + (qry0 + lanes.start)
                        p = jnp.where(qry >= key, p, 0.0)
                    p_out[r:r + step, lanes] = p.astype(BF16)
                    g_out[r:r + step, lanes] = (p * (e_in[r:r + step, lanes] - dsum)).astype(BF16)

        def one_pass(i, s_in, e_in, s_out, e_out, p_prev, g_prev, p_cur, g_cur):
            products(i + 1, s_out, e_out)
            gradients(i - 1, p_prev, g_prev)
            elementwise(i, s_in, e_in, p_cur, g_cur)

        first = n_diag * j

        def areas_of(u):
            if u >= n_diag:
                return whole_tile
            return ((slice(0, u * bq + half), slice(0, bq)), (slice(u * bq + half, (u + 1) * bq), slice(half, bq)))

        even, odd = (s0, e0, p0, g0), (s1, e1, p1, g1)
        products(first, s0, e0, areas_of(0))
        products(first + 1, s1, e1, areas_of(1))
        elementwise(first, s0, e0, p0, g0, qry0=0, areas=areas_of(0))
        for u in range(1, n_diag):
            (s_in, e_in, p_cur, g_cur), (s_out, e_out, p_prev, g_prev) = (odd, even) if u % 2 else (even, odd)
            products(first + u + 1, s_out, e_out, areas_of(u + 1))
            gradients(first + u - 1, p_prev, g_prev, areas_of(u - 1))
            elementwise(first + u, s_in, e_in, p_cur, g_cur, qry0=u * bq, areas=areas_of(u))
        corner = (slice(bk - half, bk), slice(0, half))
        p1[corner] = jnp.zeros((half, half), BF16)
        g1[corner] = jnp.zeros((half, half), BF16)

        def two_passes(n, _):
            i = first + n_diag + 2 * n
            one_pass(i, s0, e0, s1, e1, p1, g1, p0, g0)
            one_pass(i + 1, s1, e1, s0, e0, p0, g0, p1, g1)
            return 0

        lax.fori_loop(0, (last - first - n_diag + 1) // 2, two_passes, 0)
        gradients(last, p1, g1)
        dk_ref[0] = dk_ref[0] * LN2

    whole = pl.BlockSpec((1, t, HEAD_PAD), lambda h, j: (h, 0, 0))
    block = pl.BlockSpec((1, bk, HEAD_PAD), lambda h, j: (h, j, 0))
    rows = pl.BlockSpec((1, 1, t), lambda h, j: (h, 0, 0), pipeline_mode=pl.Buffered(1))
    shape = jax.ShapeDtypeStruct((HEADS, t, HEAD_PAD), F32)
    tile = lambda dtype: pltpu.VMEM((bk, bq), dtype)
    return pl.pallas_call(
        body, name="attn_bwd", grid=(HEADS, t // bk),
        in_specs=[whole, block, block, whole, rows, rows],
        out_specs=[pl.BlockSpec((1, HEAD_PAD, t), lambda h, j: (h, 0, 0)), block, block],
        out_shape=[jax.ShapeDtypeStruct((HEADS, HEAD_PAD, t), F32), shape, shape],
        scratch_shapes=[tile(F32), tile(F32), tile(F32), tile(F32), tile(BF16), tile(BF16),
                        tile(BF16), tile(BF16), pltpu.VMEM((HEAD_PAD, bk), BF16)],
        compiler_params=_cparams(("arbitrary", "arbitrary"), vmem_limit=ATTN_BWD_VMEM_LIMIT),
    )(q, k, v, do, lse_row, d_row)


def _bwd_tail(dq, dk, dv, proj, pos_col, invf_row, w_heads, q_g, kv_g, x, dr, drest, wp_in):
    t = proj.shape[0]
    tm = PROJ_TILE
    n_head = 4 * LANES

    def body(dq_ref, dk_ref, dv_ref, ph_ref, pos_ref, invf_ref, wh_ref, qg_ref, kvg_ref,
             x_ref, dr_ref, drest_ref, win_ref,
             gx_ref, dwin_ref, dwh_ref, dqg_ref, dkvg_ref):
        @pl.when(pl.program_id(0) == 0)
        def _():
            dwin_ref[...] = jnp.zeros_like(dwin_ref)
            dwh_ref[...] = jnp.zeros_like(dwh_ref)
            dqg_ref[...] = jnp.zeros_like(dqg_ref)
            dkvg_ref[...] = jnp.zeros_like(dkvg_ref)

        xb = x_ref[...].astype(BF16)
        dr_b = drest_ref[...]
        dwin_ref[:, n_head:] += _dot_tn(xb, dr_b)
        gx_rest = DN_ALPHA * dr_ref[...] + _dot_nt(dr_b, win_ref[:, n_head:])

        cos, s1, s2 = _rope_tables(pos_ref[...], invf_ref[...])
        lane = lax.broadcasted_iota(jnp.int32, (tm, LANES), 1)
        c_q = ph_ref[:, :Q_LORA]
        c_kv = ph_ref[:, Q_LORA:Q_LORA + KV_LORA]
        rstd_q = lax.rsqrt(jnp.mean(c_q * c_q, axis=-1, keepdims=True) + EPS)
        rstd_kv = lax.rsqrt(jnp.mean(c_kv * c_kv, axis=-1, keepdims=True) + EPS)
        qhat = c_q * rstd_q
        kvhat = c_kv * rstd_kv
        cqn = (qhat * qg_ref[...]).astype(BF16)
        ckvn = (kvhat * kvg_ref[...]).astype(BF16)
        dkr_rot = jnp.zeros((tm, LANES), F32)
        dq_heads, dkv_heads = [], []
        for h in range(HEADS):
            dq_heads.append(_rope(jnp.transpose(dq_ref[h]) * ATTN_SCALE, cos, s1, s2, -1.0).astype(BF16))
            dk_h = dk_ref[h]
            dkv_heads.append(jnp.where(lane < NOPE, dk_h, dv_ref[h]).astype(BF16))
            dkr_rot = dkr_rot + dk_h
        dq_all = jnp.concatenate(dq_heads, axis=1)
        dkv_all = jnp.concatenate(dkv_heads, axis=1)
        dwq_all = _dot_tn(cqn, dq_all)
        dwkv_all = _dot_tn(ckvn, dkv_all)
        for h in range(HEADS):
            dwh_ref[h, :Q_LORA, :] += dwq_all[:, h * HEAD_PAD:(h + 1) * HEAD_PAD]
            dwh_ref[h, Q_LORA:, :] += dwkv_all[:, h * HEAD_PAD:(h + 1) * HEAD_PAD]
        dcqn = _dot_nt(dq_all, jnp.concatenate([wh_ref[h, :Q_LORA, :] for h in range(HEADS)], axis=1))
        dckvn = _dot_nt(dkv_all, jnp.concatenate([wh_ref[h, Q_LORA:, :] for h in range(HEADS)], axis=1))
        rot_lanes = (lane >= KR_LO) & (lane < KR_LO + ROPE)
        dkr_raw = jnp.where(rot_lanes, _rope(dkr_rot, cos, s1, s2, -1.0), 0.0)
        dqg_ref[...] += jnp.sum(dcqn * qhat, axis=0, keepdims=True)
        dkvg_ref[...] += jnp.sum(dckvn * kvhat, axis=0, keepdims=True)
        dqh = dcqn * qg_ref[...]
        dkvh = dckvn * kvg_ref[...]
        dc_q = rstd_q * (dqh - qhat * jnp.mean(dqh * qhat, axis=-1, keepdims=True))
        dc_kv = rstd_kv * (dkvh - kvhat * jnp.mean(dkvh * kvhat, axis=-1, keepdims=True))
        dh_b = jnp.concatenate([dc_q, dc_kv, dkr_raw], axis=-1).astype(BF16)
        dwin_ref[:, :n_head] += _dot_tn(xb, dh_b)
        gx_ref[...] = gx_rest + _dot_nt(dh_b, win_ref[:, :n_head])

    full = lambda a: pl.BlockSpec(a.shape, lambda i: (0,) * a.ndim)
    tile = lambda w: pl.BlockSpec((tm, w), lambda i: (i, 0))
    heads = pl.BlockSpec((HEADS, tm, HEAD_PAD), lambda i: (0, i, 0))
    acc = lambda shape: (pl.BlockSpec(shape, lambda i: (0,) * len(shape)), jax.ShapeDtypeStruct(shape, F32))
    accs = [acc(wp_in.shape), acc(w_heads.shape), acc((1, Q_LORA)), acc((1, KV_LORA))]
    return pl.pallas_call(
        body, name="bwd_tail", grid=(t // tm,),
        in_specs=[pl.BlockSpec((HEADS, HEAD_PAD, tm), lambda i: (0, 0, i)), heads, heads, tile(n_head),
                  pl.BlockSpec((tm, 1), lambda i: (i, 0)), full(invf_row), full(w_heads), full(q_g), full(kv_g),
                  tile(D_MODEL), tile(D_MODEL), tile(drest.shape[1]), full(wp_in)],
        out_specs=[tile(D_MODEL)] + [a[0] for a in accs],
        out_shape=[jax.ShapeDtypeStruct((t, D_MODEL), F32)] + [a[1] for a in accs],
        compiler_params=_cparams(("arbitrary",), vmem_limit=BWD_TAIL_VMEM_LIMIT),
    )(dq, dk, dv, proj, pos_col, invf_row, w_heads, q_g, kv_g, x, dr, drest, wp_in)


def _adam(parts, w, m, v, *, name, tile_rows, transposed=False):
    n, rows, cols = parts.shape
    lane_pad = -(-cols // LANES) * LANES

    def body(p_ref, w_ref, m_ref, v_ref, g_ref, d_ref, nm_ref, nv_ref, *scratch):
        g = p_ref[0].astype(F32)
        for s in range(1, n):
            g = g + p_ref[s].astype(F32)
        if transposed:
            wide_ref, = scratch
            wide_ref[:, lane_pad - LANES:] = jnp.zeros((tile_rows, LANES), F32)
            wide_ref[:, :cols] = g
            g = jnp.transpose(wide_ref[...])[:cols]
        m_new = ADAM_B1 * m_ref[...] + (1.0 - ADAM_B1) * g
        v_new = ADAM_B2 * v_ref[...] + (1.0 - ADAM_B2) * (g * g)
        m_hat = m_new / (1.0 - ADAM_B1 ** ADAM_STEP)
        v_hat = v_new / (1.0 - ADAM_B2 ** ADAM_STEP)
        g_ref[...] = g
        d_ref[...] = -ADAM_LR * (m_hat / (jnp.sqrt(v_hat) + ADAM_EPS) + ADAM_WD * w_ref[...])
        nm_ref[...] = m_new
        nv_ref[...] = v_new

    if transposed:
        flat = pl.BlockSpec((cols, tile_rows), lambda i: (0, i))
        shape = jax.ShapeDtypeStruct((cols, rows), F32)
        scratch = [pltpu.VMEM((tile_rows, lane_pad), F32)]
    else:
        flat = pl.BlockSpec((tile_rows, cols), lambda i: (i, 0))
        shape = jax.ShapeDtypeStruct((rows, cols), F32)
        scratch = []
    return pl.pallas_call(
        body, name=name, grid=(rows // tile_rows,),
        in_specs=[pl.BlockSpec((n, tile_rows, cols), lambda i: (0, i, 0)), flat, flat, flat],
        out_specs=[flat] * 4, out_shape=[shape] * 4, scratch_shapes=scratch,
        compiler_params=_cparams(("arbitrary",)),
    )(parts, w, m, v)


SMALL_SIZES = (Q_LORA, KV_LORA, G_WIDTH, G_WIDTH, HEADS * CHUNK, D_MODEL, D_MODEL)


def _pack_small(vals, last=None):
    flat = jnp.concatenate([v.reshape(-1) for v in vals])
    pad = SMALL_LEN - flat.shape[0]
    if last is None:
        return jnp.pad(flat, (0, pad))
    return jnp.concatenate([flat, jnp.zeros((pad - 1,), F32), last.reshape(1)])


def _unpack_small(flat):
    out, at = [], 0
    for n in SMALL_SIZES:
        out.append(flat[at:at + n])
        at += n
    out[4] = out[4].reshape(HEADS, CHUNK)
    return out


UQ_SHARD = HEADS * (NOPE + ROPE) // N_DEV
HEAD_ROWS = Q_LORA + KV_LORA
MIXED_ROWS = HEAD_ROWS + CHUNK + SMALL_LEN // N_DEV // LANES


def _head_slab(w_uq_shard, w_ukv_shard):
    return jnp.concatenate([jnp.pad(w_uq_shard, ((0, 0), (0, LANES - UQ_SHARD))), w_ukv_shard])


IN_SHARD = D_IN // N_DEV


def _w_in_pieces():
    split = Q_LORA + KV_LORA
    moves = ((0, split, 0), (split, split + ROPE, KR_LO), (split + ROPE, D_IN, LANES - ROPE))
    pieces = []
    for s in range(N_DEV):
        lo, hi = s * IN_SHARD, (s + 1) * IN_SHARD
        for a, b, shift in moves:
            a, b = max(a, lo), min(b, hi)
            if a < b:
                pieces.append((s, a - lo, a + shift, b - a))
    return pieces


def _w_in_shards(dwp_in):
    tr = TOKEN_TILE
    by_shard = [[p for p in _w_in_pieces() if p[0] == s] for s in range(N_DEV)]

    def body(w_ref, o_ref):
        for s, pieces in enumerate(by_shard):
            parts = [w_ref[:, dst:dst + width] for _, _, dst, width in pieces]
            o_ref[s] = parts[0] if len(parts) == 1 else jnp.concatenate(parts, axis=1)

    return pl.pallas_call(
        body, name="w_in_split", grid=(D_MODEL // tr,),
        in_specs=[pl.BlockSpec((tr, D_IN_PAD), lambda i: (i, 0))],
        out_specs=pl.BlockSpec((N_DEV, tr, IN_SHARD), lambda i: (0, i, 0)),
        out_shape=jax.ShapeDtypeStruct((N_DEV, D_MODEL, IN_SHARD), dwp_in.dtype),
        compiler_params=_cparams(("arbitrary",)),
    )(dwp_in)


def kernel(x, positions, w_in, q_norm_g, w_uq, kv_norm_g, w_ukv, sgu_norm_g, sgu_norm_b, w_spatial, b_spatial, w_out, ln_g, ln_b, loss_target, m_w_in, m_q_norm_g, m_w_uq, m_kv_norm_g, m_w_ukv, m_sgu_norm_g, m_sgu_norm_b, m_w_spatial, m_b_spatial, m_w_out, m_ln_g, m_ln_b, v_w_in, v_q_norm_g, v_w_uq, v_kv_norm_g, v_w_ukv, v_sgu_norm_g, v_sgu_norm_b, v_w_spatial, v_b_spatial, v_w_out, v_ln_g, v_ln_b):
    me = 4 * lax.axis_index("x") + 2 * lax.axis_index("y") + lax.axis_index("c")
    seq = x.shape[1]
    x2 = x.reshape(seq, D_MODEL)
    tgt2 = loss_target.reshape(seq, D_MODEL)
    pos_col = positions.reshape(seq, 1)

    w_in_shards, w_out_shards, w_heads = _gather_two_level(
        [w_in.astype(BF16), w_out.astype(BF16), _head_slab(w_uq, w_ukv).astype(BF16)],
        name="wgather")
    (loss_part, grad_x, d_in, d_heads, d_out, d_ws, d_bs_t, d_lng, d_lnb, d_sgug, d_sgub, d_qg, d_kvg) = _local_step(
        x2, tgt2, pos_col, w_in_shards, w_heads, w_out_shards.reshape(D_MODEL, D_MODEL), q_norm_g, kv_norm_g,
        sgu_norm_g, sgu_norm_b, w_spatial, b_spatial, ln_g, ln_b)

    small_part = _pack_small([d_qg, d_kvg, d_sgug, d_sgub, d_bs_t[:, :HEADS].T, d_lng, d_lnb], last=loss_part[0, :1])
    mixed = jnp.concatenate([d_heads, d_ws, small_part.reshape(N_DEV, -1, LANES)], axis=1)
    by_chip = [g.reshape((N_CHIPS, 2) + g.shape[1:])
               for g in (d_in, d_out.reshape(N_DEV, D_MODEL // N_DEV, D_MODEL), mixed)]
    from_sibling = _sibling_swap(by_chip, name="gswap")
    core = lax.axis_index("c").astype(jnp.int32).reshape(1)
    pair_sums = [_pair_sum(a, b, core, name=nm, tile_rows=tr, out_dtype=dt) for a, b, nm, tr, dt in zip(
        by_chip, from_sibling, ("gsum_in", "gsum_out", "gsum_mixed"), (TOKEN_TILE, D_MODEL // N_DEV, MIXED_ROWS),
        (BF16, BF16, F32))]
    recv_in, recv_out, recv_mixed = _chip_exchange(pair_sums, name="gexch")

    take = lambda a: lax.dynamic_index_in_dim(a, me, 0, keepdims=False)
    small_w = _pack_small([q_norm_g, kv_norm_g, sgu_norm_g, sgu_norm_b, b_spatial, ln_g, ln_b])
    small_m = _pack_small([m_q_norm_g, m_kv_norm_g, m_sgu_norm_g, m_sgu_norm_b, m_b_spatial, m_ln_g, m_ln_b])
    small_v = _pack_small([v_q_norm_g, v_kv_norm_g, v_sgu_norm_g, v_sgu_norm_b, v_b_spatial, v_ln_g, v_ln_b])
    own_mixed = lambda uq, ukv, sp, small: jnp.concatenate(
        [_head_slab(uq, ukv), take(sp), take(small.reshape(N_DEV, -1, LANES))])
    res_in = [a.T for a in _adam(recv_in, w_in.T, m_w_in.T, v_w_in.T, name="adam_in", tile_rows=TOKEN_TILE,
                                 transposed=True)]
    res_out = _adam(recv_out, w_out, m_w_out, v_w_out, name="adam_out", tile_rows=D_MODEL // N_DEV)
    res_mixed = _adam(recv_mixed, own_mixed(w_uq, w_ukv, w_spatial, small_w), own_mixed(m_w_uq, m_w_ukv, m_w_spatial, small_m),
                      own_mixed(v_w_uq, v_w_ukv, v_w_spatial, small_v), name="adam_mixed", tile_rows=MIXED_ROWS)

    rep_g, = _gather_direct([res_mixed[0][HEAD_ROWS:]], name="sgather")
    rep_pack = lambda sp, small: jnp.concatenate(
        [sp.reshape(N_DEV, CHUNK, LANES), small.reshape(N_DEV, -1, LANES)], axis=1).reshape(-1, LANES)
    _, delta_rep, m_rep, v_rep = _adam(rep_g.reshape(1, N_DEV * REP_ROWS, LANES), rep_pack(w_spatial, small_w),
                                       rep_pack(m_w_spatial, small_m), rep_pack(v_w_spatial, small_v),
                                       name="adam_rep", tile_rows=N_DEV * REP_ROWS)

    def rep_unpack(a):
        a = a.reshape(N_DEV, REP_ROWS, LANES)
        small = _unpack_small(a[:, CHUNK:].reshape(-1))
        return [small[0], small[1], small[2], small[3], a[:, :CHUNK], small[4], small[5], small[6]]

    def ordered(which, rep):
        r_qg, r_kvg, r_sg, r_sb, r_ws, r_bs, r_lg, r_lb = rep_unpack(rep)
        heads = res_mixed[which]
        return [res_in[which], r_qg, heads[:Q_LORA, :UQ_SHARD], r_kvg, heads[Q_LORA:HEAD_ROWS], r_sg, r_sb, r_ws, r_bs,
                res_out[which], r_lg, r_lb]

    loss = rep_g[N_DEV - 1, REP_ROWS - 1, LANES - 1]
    outs = [loss, grad_x.reshape(x.shape)]
    outs += ordered(0, rep_g.reshape(-1, LANES))
    outs += ordered(1, delta_rep)
    outs += ordered(2, m_rep)
    outs += ordered(3, v_rep)
    return tuple(outs)


def _local_step(x2, tgt2, pos_col, w_in_shards, w_heads, w_out_full, q_norm_g, kv_norm_g, sgu_norm_g, sgu_norm_b,
                w_spatial, b_spatial, ln_g, ln_b):
    half = jnp.arange(HALF, dtype=F32)
    inv_freq = 1.0 / (ROPE_THETA ** (half / HALF))
    invf_row = jnp.concatenate([jnp.zeros((KR_LO,), F32), inv_freq, inv_freq,
                                jnp.zeros((LANES - KR_LO - ROPE,), F32)]).reshape(1, LANES)
    tri = jnp.tril(jnp.ones((CHUNK, CHUNK), dtype=bool))
    ws_low = jnp.where(tri[None], w_spatial, 0.0).astype(BF16)
    ws_low_t = ws_low.transpose(0, 2, 1)
    bsp = jnp.repeat(b_spatial.T, G_HEAD_DIM, axis=1)
    row = lambda a: a.reshape(1, -1)

    proj, q, k, v, vt, wp_in = _fwd_proj(x2, pos_col, invf_row, w_in_shards, w_heads, row(q_norm_g), row(kv_norm_g))
    o, lse_row = _attn_fwd(q, k, vt)
    (dr, do, d_row, drest, d_out, d_ws, d_bs_t, d_lng, d_lnb, d_sgug, d_sgub, loss_part) = _mid(
        x2, tgt2, proj, o, w_out_full, ws_low, ws_low_t, bsp, row(sgu_norm_g), row(sgu_norm_b), row(ln_g), row(ln_b))
    dqt, dk, dv = _attn_bwd(q, k, v, do, lse_row, d_row)
    grad_x, dwp_in, d_heads, d_qg, d_kvg = _bwd_tail(dqt, dk, dv, proj, pos_col, invf_row, w_heads, row(q_norm_g),
                                                      row(kv_norm_g), x2, dr, drest, wp_in)
    return (loss_part, grad_x, _w_in_shards(dwp_in), d_heads, d_out, d_ws, d_bs_t, d_lng, d_lnb, d_sgug, d_sgub,
            d_qg, d_kvg)
```

```python
import math

import jax
import jax.numpy as jnp
from jax import lax
from jax.experimental import pallas as pl
from jax.experimental.pallas import tpu as pltpu

F32 = jnp.float32
BF16 = jnp.bfloat16

N_DEV = 8
D_MODEL = 1024
HEADS = 8
NOPE = 64
ROPE = 32
HALF = ROPE // 2
VDIM = 64
Q_LORA = 256
KV_LORA = 128
G_WIDTH = 512
G_HEAD_DIM = 64
CHUNK = 128
HEAD_PAD = 128
D_IN = 2464
D_IN_PAD = 2560
KR_LO = NOPE
SUM_ROW = NOPE - 1
LIVE_ROWS = slice(NOPE - 16, HEAD_PAD)
ROPE_THETA = 10000.0
DN_ALPHA = 2.0 ** 0.25
EPS = 1e-5
ATTN_SCALE = 1.0 / math.sqrt(NOPE + ROPE)
ADAM_LR, ADAM_B1, ADAM_B2, ADAM_EPS, ADAM_WD, ADAM_STEP = 0.001, 0.9, 0.999, 1e-08, 0.01, 10

LANES = 128
REP_ROWS = 136
SMALL_LEN = 8192
VMEM_LIMIT = 56 * 1024 * 1024
ATTN_BWD_VMEM_LIMIT = 61 * 1024 * 1024
BWD_TAIL_VMEM_LIMIT = 61 * 1024 * 1024

TOKEN_TILE = 256
PROJ_TILE = 512
ATTN_FWD_WIDE = 2048
ATTN_BWD_WIDE = 2048
ATTN_NARROW = 512
SOFTMAX_ROWS = 512
LOG2E = 1.4426950408889634
LN2 = 0.6931471805599453
Q_PRESCALE = ATTN_SCALE * LOG2E


def _cparams(sem=None, vmem_limit=VMEM_LIMIT):
    return pltpu.CompilerParams(dimension_semantics=sem, vmem_limit_bytes=vmem_limit)


def _dot(a, b):
    return jnp.dot(a, b, preferred_element_type=F32)


def _dot_nt(a, b):
    return lax.dot_general(a, b, (((1,), (1,)), ((), ())), preferred_element_type=F32)


def _dot_tn(a, b):
    return lax.dot_general(a, b, (((0,), (0,)), ((), ())), preferred_element_type=F32)


def _sigmoid(z):
    return 1.0 / (1.0 + jnp.exp(-z))


def _gelu(x):
    return 0.5 * x * (1.0 + lax.erf(x * 0.7071067811865476))


def _gelu_grad(x):
    cdf = 0.5 * (1.0 + lax.erf(x * 0.7071067811865476))
    return cdf + x * jnp.exp(-0.5 * x * x) * 0.3989422804014327


def _gather_direct(srcs, *, name):
    n = len(srcs)

    def body(*refs):
        src_refs, out_refs = refs[:n], refs[n:2 * n]
        send_sems, recv_sems, local_sems = refs[2 * n:]
        x, y, c = lax.axis_index("x"), lax.axis_index("y"), lax.axis_index("c")
        me = 4 * x + 2 * y + c
        mine = [pltpu.make_async_copy(src_refs[t], out_refs[t].at[me], local_sems.at[t]) for t in range(n)]
        for cp in mine:
            cp.start()
        sends, arrivals = [], []
        for k in (6, 7, 4, 5, 2, 3, 1):
            px = 1 - x if k & 4 else x
            py = 1 - y if k & 2 else y
            pc = 1 - c if k & 1 else c
            peer = 4 * px + 2 * py + pc
            for t in range(n):
                sem = (k - 1) * n + t
                cp = pltpu.make_async_remote_copy(
                    src_ref=src_refs[t], dst_ref=out_refs[t].at[me],
                    send_sem=send_sems.at[sem], recv_sem=recv_sems.at[sem],
                    device_id=(px, py, pc), device_id_type=pl.DeviceIdType.MESH)
                cp.start()
                sends.append(cp)
                arrivals.append(pltpu.make_async_remote_copy(
                    src_ref=src_refs[t], dst_ref=out_refs[t].at[peer],
                    send_sem=send_sems.at[sem], recv_sem=recv_sems.at[sem],
                    device_id=(x, y, c), device_id_type=pl.DeviceIdType.MESH))
        for cp in arrivals:
            cp.wait_recv()
        for cp in sends:
            cp.wait_send()
        for cp in mine:
            cp.wait()

    hbm = pl.BlockSpec(memory_space=pl.ANY)
    return pl.pallas_call(
        body, name=name,
        out_shape=[jax.ShapeDtypeStruct((N_DEV,) + s.shape, s.dtype) for s in srcs],
        in_specs=[hbm] * n, out_specs=[hbm] * n,
        scratch_shapes=[pltpu.SemaphoreType.DMA(((N_DEV - 1) * n,)), pltpu.SemaphoreType.DMA(((N_DEV - 1) * n,)),
                        pltpu.SemaphoreType.DMA((n,))],
    )(*srcs)


def _gather_two_level(srcs, *, name):
    n = len(srcs)

    def body(*refs):
        src_refs, out_refs = refs[:n], refs[n:2 * n]
        send_sems, recv_sems, local_sems = refs[2 * n:]
        x, y, c = lax.axis_index("x"), lax.axis_index("y"), lax.axis_index("c")
        me, sibling = (x, y, c), (x, y, 1 - c)
        chips = [(1 - x, 1 - y), (1 - x, y), (x, 1 - y)]
        index = lambda px, py, pc: 4 * px + 2 * py + pc

        def copy(k, t, block, to, src=None):
            place = out_refs[t].at[index(*block)]
            return pltpu.make_async_remote_copy(
                src_ref=place if src is None else src, dst_ref=place,
                send_sem=send_sems.at[k * n + t], recv_sem=recv_sems.at[k * n + t],
                device_id=to, device_id_type=pl.DeviceIdType.MESH)

        mine = [pltpu.make_async_copy(src_refs[t], out_refs[t].at[index(*me)], local_sems.at[t]) for t in range(n)]
        for cp in mine:
            cp.start()
        first = [copy(1 + j, t, me, (*chip, c), src=src_refs[t]) for j, chip in enumerate(chips) for t in range(n)]
        first += [copy(0, t, me, sibling, src=src_refs[t]) for t in range(n)]
        for cp in first:
            cp.start()
        passed = []
        for j, chip in enumerate(chips):
            for t in range(n):
                copy(1 + j, t, (*chip, c), me).wait_recv()
                cp = copy(4 + j, t, (*chip, c), sibling)
                cp.start()
                passed.append(cp)
        for t in range(n):
            copy(0, t, sibling, me).wait_recv()
        for j, chip in enumerate(chips):
            for t in range(n):
                copy(4 + j, t, (*chip, 1 - c), me).wait_recv()
        for cp in first + passed:
            cp.wait_send()
        for cp in mine:
            cp.wait()

    hbm = pl.BlockSpec(memory_space=pl.ANY)
    return pl.pallas_call(
        body, name=name,
        out_shape=[jax.ShapeDtypeStruct((N_DEV,) + s.shape, s.dtype) for s in srcs],
        in_specs=[hbm] * n, out_specs=[hbm] * n,
        scratch_shapes=[pltpu.SemaphoreType.DMA((7 * n,)), pltpu.SemaphoreType.DMA((7 * n,)),
                        pltpu.SemaphoreType.DMA((n,))],
    )(*srcs)


N_CHIPS = N_DEV // 2


def _sibling_swap(srcs, *, name):
    n = len(srcs)

    def body(*refs):
        src_refs, out_refs = refs[:n], refs[n:2 * n]
        send_sems, recv_sems = refs[2 * n:]
        x, y, c = lax.axis_index("x"), lax.axis_index("y"), lax.axis_index("c")
        sends = []
        for chip in range(N_CHIPS):
            for t in range(n):
                cp = pltpu.make_async_remote_copy(
                    src_ref=src_refs[t].at[chip, 1 - c], dst_ref=out_refs[t].at[chip],
                    send_sem=send_sems.at[chip * n + t], recv_sem=recv_sems.at[chip * n + t],
                    device_id=(x, y, 1 - c), device_id_type=pl.DeviceIdType.MESH)
                cp.start()
                sends.append(cp)
        for cp in sends:
            cp.wait_recv()
        for cp in sends:
            cp.wait_send()

    hbm = pl.BlockSpec(memory_space=pl.ANY)
    return pl.pallas_call(
        body, name=name,
        out_shape=[jax.ShapeDtypeStruct((N_CHIPS,) + s.shape[2:], s.dtype) for s in srcs],
        in_specs=[hbm] * n, out_specs=[hbm] * n,
        scratch_shapes=[pltpu.SemaphoreType.DMA((N_CHIPS * n,)), pltpu.SemaphoreType.DMA((N_CHIPS * n,))],
    )(*srcs)


def _pair_sum(mine, theirs, core, *, name, tile_rows, out_dtype):
    _, _, rows, cols = mine.shape

    def body(core_ref, a_ref, b_ref, o_ref):
        o_ref[...] = (a_ref[0] + b_ref[...]).astype(out_dtype)

    return pl.pallas_call(
        body, name=name,
        grid_spec=pltpu.PrefetchScalarGridSpec(
            num_scalar_prefetch=1, grid=(N_CHIPS, rows // tile_rows),
            in_specs=[pl.BlockSpec((1, 1, tile_rows, cols), lambda q, r, core_ref: (q, core_ref[0], r, 0)),
                      pl.BlockSpec((1, tile_rows, cols), lambda q, r, core_ref: (q, r, 0))],
            out_specs=pl.BlockSpec((1, tile_rows, cols), lambda q, r, core_ref: (q, r, 0))),
        out_shape=jax.ShapeDtypeStruct((N_CHIPS, rows, cols), out_dtype),
        compiler_params=_cparams(("arbitrary", "arbitrary")),
    )(core, mine, theirs)


def _chip_exchange(srcs, *, name):
    n = len(srcs)

    def body(*refs):
        src_refs, out_refs = refs[:n], refs[n:2 * n]
        send_sems, recv_sems, local_sems = refs[2 * n:]
        x, y, c = lax.axis_index("x"), lax.axis_index("y"), lax.axis_index("c")
        my_chip = 2 * x + y
        mine = [pltpu.make_async_copy(src_refs[t].at[my_chip], out_refs[t].at[my_chip], local_sems.at[t])
                for t in range(n)]
        for cp in mine:
            cp.start()
        sends, arrivals = [], []
        for k in (3, 2, 1):
            px = 1 - x if k & 2 else x
            py = 1 - y if k & 1 else y
            peer_chip = 2 * px + py
            for t in range(n):
                sem = (k - 1) * n + t
                cp = pltpu.make_async_remote_copy(
                    src_ref=src_refs[t].at[peer_chip], dst_ref=out_refs[t].at[my_chip],
                    send_sem=send_sems.at[sem], recv_sem=recv_sems.at[sem],
                    device_id=(px, py, c), device_id_type=pl.DeviceIdType.MESH)
                cp.start()
                sends.append(cp)
                arrivals.append(pltpu.make_async_remote_copy(
                    src_ref=src_refs[t].at[peer_chip], dst_ref=out_refs[t].at[peer_chip],
                    send_sem=send_sems.at[sem], recv_sem=recv_sems.at[sem],
                    device_id=(x, y, c), device_id_type=pl.DeviceIdType.MESH))
        for cp in arrivals:
            cp.wait_recv()
        for cp in sends:
            cp.wait_send()
        for cp in mine:
            cp.wait()

    hbm = pl.BlockSpec(memory_space=pl.ANY)
    return pl.pallas_call(
        body, name=name,
        out_shape=[jax.ShapeDtypeStruct(s.shape, s.dtype) for s in srcs],
        in_specs=[hbm] * n, out_specs=[hbm] * n,
        scratch_shapes=[pltpu.SemaphoreType.DMA((3 * n,)), pltpu.SemaphoreType.DMA((3 * n,)),
                        pltpu.SemaphoreType.DMA((n,))],
    )(*srcs)


def _rope_tables(pos_col, invf_row):
    ang = pos_col.astype(F32) * invf_row
    lane = lax.broadcasted_iota(jnp.int32, ang.shape, 1)
    cos, sin = jnp.cos(ang), jnp.sin(ang)
    first = (lane >= KR_LO) & (lane < KR_LO + HALF)
    second = (lane >= KR_LO + HALF) & (lane < KR_LO + ROPE)
    return cos, jnp.where(first, sin, 0.0), jnp.where(second, sin, 0.0)


def _rope(t, cos, sin_first, sin_second, sign):
    up = pltpu.roll(t, LANES - HALF, 1)
    down = pltpu.roll(t, HALF, 1)
    return t * cos - sign * (up * sin_first) + sign * (down * sin_second)


def _fwd_proj(x, pos_col, invf_row, w_in_shards, w_heads, q_g, kv_g):
    t = x.shape[0]
    tm = PROJ_TILE

    def body(x_ref, pos_ref, invf_ref, sh_ref, wh_ref, qg_ref, kvg_ref,
             proj_ref, q_ref, k_ref, v_ref, vt_ref, win_ref):
        @pl.when(pl.program_id(0) == 0)
        def _():
            win_ref[...] = jnp.zeros_like(win_ref)
            for s, src, dst, width in _w_in_pieces():
                win_ref[:, dst:dst + width] = sh_ref[s, :, src:src + width]

        proj = _dot(x_ref[...].astype(BF16), win_ref[...])
        proj_ref[...] = proj
        c_q = proj[:, :Q_LORA]
        c_kv = proj[:, Q_LORA:Q_LORA + KV_LORA]
        kr_raw = proj[:, Q_LORA + KV_LORA:Q_LORA + KV_LORA + LANES]
        cqn = (c_q * lax.rsqrt(jnp.mean(c_q * c_q, axis=-1, keepdims=True) + EPS) * qg_ref[...]).astype(BF16)
        ckvn = (c_kv * lax.rsqrt(jnp.mean(c_kv * c_kv, axis=-1, keepdims=True) + EPS) * kvg_ref[...]).astype(BF16)
        cos, s1, s2 = _rope_tables(pos_ref[...], invf_ref[...])
        kr = _rope(kr_raw, cos, s1, s2, 1.0)
        lane = lax.broadcasted_iota(jnp.int32, (tm, HEAD_PAD), 1)
        q_all = _dot(cqn, jnp.concatenate([wh_ref[h, :Q_LORA, :] for h in range(HEADS)], axis=1))
        kv_all = _dot(ckvn, jnp.concatenate([wh_ref[h, Q_LORA:, :] for h in range(HEADS)], axis=1))
        for h in range(HEADS):
            q_h = q_all[:, h * HEAD_PAD:(h + 1) * HEAD_PAD]
            kv_h = kv_all[:, h * HEAD_PAD:(h + 1) * HEAD_PAD]
            q_ref[h] = (_rope(q_h, cos, s1, s2, 1.0) * Q_PRESCALE).astype(BF16)
            k_ref[h] = jnp.where(lane < NOPE, kv_h, kr).astype(BF16)
            v_ref[h] = kv_h.astype(BF16)
            vt_ref[h] = jnp.transpose(jnp.where(lane == SUM_ROW, 1.0, kv_h)).astype(BF16)

    full = lambda a: pl.BlockSpec(a.shape, lambda i: (0,) * a.ndim)
    head_spec = pl.BlockSpec((HEADS, tm, HEAD_PAD), lambda i: (0, i, 0))
    head_shape = jax.ShapeDtypeStruct((HEADS, t, HEAD_PAD), BF16)
    return pl.pallas_call(
        body, name="fwd_proj", grid=(t // tm,),
        in_specs=[pl.BlockSpec((tm, D_MODEL), lambda i: (i, 0)), pl.BlockSpec((tm, 1), lambda i: (i, 0)),
                  full(invf_row), full(w_in_shards), full(w_heads), full(q_g), full(kv_g)],
        out_specs=[pl.BlockSpec((tm, D_IN_PAD), lambda i: (i, 0)), head_spec, head_spec, head_spec,
                   pl.BlockSpec((HEADS, HEAD_PAD, tm), lambda i: (0, 0, i)),
                   pl.BlockSpec((D_MODEL, D_IN_PAD), lambda i: (0, 0))],
        out_shape=[jax.ShapeDtypeStruct((t, D_IN_PAD), F32), head_shape, head_shape, head_shape,
                   jax.ShapeDtypeStruct((HEADS, HEAD_PAD, t), BF16),
                   jax.ShapeDtypeStruct((D_MODEL, D_IN_PAD), w_in_shards.dtype)],
        compiler_params=_cparams(("arbitrary",)),
    )(x, pos_col, invf_row, w_in_shards, w_heads, q_g, kv_g)


def _attn_fwd(q, k, vt):
    t = q.shape[1]
    bq, bk = ATTN_FWD_WIDE, ATTN_NARROW
    n_diag = bq // bk
    chunk = SOFTMAX_ROWS

    def body(q_ref, k_ref, vt_ref, o_ref, lse_ref, s0, s1, p0, p1, x0, x1, m_scr, a_scr, acc_scr):
        i = pl.program_id(1)
        at = lambda j: pl.ds(pl.multiple_of(j * bk, bk), bk)

        def exp_pass(s_in, block_max, p_out, diagonal=False, cols=slice(None)):
            width = bq if cols == slice(None) else cols.stop - cols.start

            def load(r):
                s = s_in[r:r + chunk, cols]
                if diagonal:
                    key = lax.broadcasted_iota(jnp.int32, (chunk, width), 0) + r
                    qry = lax.broadcasted_iota(jnp.int32, (chunk, width), 1)
                    s = jnp.where(qry >= key, s, -jnp.inf)
                return s

            if diagonal:
                block_max = jnp.max(load(0), axis=0, keepdims=True)
                for r in range(chunk, bk, chunk):
                    block_max = jnp.maximum(block_max, jnp.max(load(r), axis=0, keepdims=True))
            m_old = m_scr[:, cols]
            m_new = jnp.maximum(m_old, block_max)
            alpha = jnp.exp2(m_old - m_new)
            for r in range(0, bk, chunk):
                p_out[r:r + chunk, cols] = jnp.exp2(load(r) - m_new).astype(BF16)
            m_scr[:, cols] = m_new
            return alpha

        def scores(j, s_out, x_out):
            s = _dot_nt(k_ref[0, at(j), :], q_ref[0])
            s_out[...] = s
            x_out[...] = jnp.max(s, axis=0, keepdims=True)

        def value_product(j, p_in):
            return _dot(vt_ref[0, LIVE_ROWS, at(j)], p_in[...])

        def one_pass(j, s_in, x_in, s_out, x_out, p_prev, p_cur):
            scores(j + 1, s_out, x_out)
            acc_scr[...] = a_scr[...] * acc_scr[...] + value_product(jnp.maximum(j - 1, 0), p_prev)
            a_scr[...] = exp_pass(s_in, x_in[...], p_cur)

        scores(0, s0, x0)
        p1[...] = jnp.zeros_like(p1)
        a_scr[...] = jnp.ones_like(a_scr)
        m_scr[...] = jnp.full(m_scr.shape, -jnp.inf, F32)
        acc_scr[...] = jnp.zeros_like(acc_scr)

        def two_passes(n, _):
            one_pass(2 * n, s0, x0, s1, x1, p1, p0)
            one_pass(2 * n + 1, s1, x1, s0, x0, p0, p1)
            return 0

        lax.fori_loop(0, (n_diag // 2) * i, two_passes, 0)
        d = n_diag * i
        alpha, p_prev, cols = a_scr[...], p1, slice(0, bq)
        for u in range(n_diag + 1):
            s_in, s_next, p_cur = (s0, s1, p0) if u % 2 == 0 else (s1, s0, p1)
            if u + 1 < n_diag:
                ahead = slice((u + 1) * bk, bq)
                s_next[:, ahead] = _dot_nt(k_ref[0, at(d + u + 1), :], q_ref[0, ahead, :])
            acc_scr[:, cols] = alpha * acc_scr[:, cols] + _dot(vt_ref[0, LIVE_ROWS, at(jnp.maximum(d + u - 1, 0))],
                                                               p_prev[:, cols])
            if u < n_diag:
                cols = slice(u * bk, bq)
                alpha = exp_pass(s_in, None, p_cur, diagonal=True, cols=cols)
                p_prev = p_cur
        denom = acc_scr[SUM_ROW - LIVE_ROWS.start:NOPE - LIVE_ROWS.start, :]
        o = jnp.transpose(acc_scr[NOPE - LIVE_ROWS.start:, :] / denom)
        o_ref[0] = jnp.concatenate([jnp.zeros_like(o), o], axis=1)
        lse_ref[0] = m_scr[...] + jnp.log2(denom)

    tile = lambda dtype: pltpu.VMEM((bk, bq), dtype)
    stat = pltpu.VMEM((1, bq), F32)
    return pl.pallas_call(
        body, name="attn_fwd", grid=(HEADS, t // bq),
        in_specs=[pl.BlockSpec((1, bq, HEAD_PAD), lambda h, i: (h, i, 0)),
                  pl.BlockSpec((1, t, HEAD_PAD), lambda h, i: (h, 0, 0)),
                  pl.BlockSpec((1, HEAD_PAD, t), lambda h, i: (h, 0, 0))],
        out_specs=[pl.BlockSpec((1, bq, HEAD_PAD), lambda h, i: (h, i, 0)),
                   pl.BlockSpec((1, 1, bq), lambda h, i: (h, 0, i))],
        out_shape=[jax.ShapeDtypeStruct((HEADS, t, HEAD_PAD), F32), jax.ShapeDtypeStruct((HEADS, 1, t), F32)],
        scratch_shapes=[tile(F32), tile(F32), tile(BF16), tile(BF16), stat, stat, stat, stat,
                        pltpu.VMEM((HEAD_PAD - LIVE_ROWS.start, bq), F32)],
        compiler_params=_cparams(("arbitrary", "arbitrary")),
    )(q, k, vt)


def _mid(x, target, proj, ol, w_out, ws_low, ws_low_t, bsp, sgu_g, sgu_b, ln_g, ln_b):
    t = x.shape[0]
    tm = TOKEN_TILE
    n_steps = t // tm

    def body(x_ref, tgt_ref, za_ref, u_ref, v_ref, zb_ref, ol_ref, wout_ref, ws_ref, wst_ref, bsp_ref,
             sg_ref, sb_ref, lg_ref, lb_ref,
             dr_ref, do_ref, drow_ref, drest_ref, dwout_ref, dws_ref, dbs_ref, dlg_ref, dlb_ref, dsg_ref, dsb_ref,
             loss_ref, dbsp_acc):
        step = pl.program_id(0)

        @pl.when(step == 0)
        def _():
            dwout_ref[...] = jnp.zeros_like(dwout_ref)
            dws_ref[...] = jnp.zeros_like(dws_ref)
            dbs_ref[...] = jnp.zeros_like(dbs_ref)
            dlg_ref[...] = jnp.zeros_like(dlg_ref)
            dlb_ref[...] = jnp.zeros_like(dlb_ref)
            dsg_ref[...] = jnp.zeros_like(dsg_ref)
            dsb_ref[...] = jnp.zeros_like(dsb_ref)
            loss_ref[...] = jnp.zeros_like(loss_ref)
            dbsp_acc[...] = jnp.zeros_like(dbsp_acc)

        n_chunks = tm // CHUNK
        groups = G_WIDTH // LANES

        def side_by_side(a):
            return [jnp.concatenate([a[c * CHUNK:(c + 1) * CHUNK, g * LANES:(g + 1) * LANES] for c in range(n_chunks)],
                                    axis=1) for g in range(groups)]

        def by_chunk(wide):
            return jnp.concatenate([jnp.concatenate([wide[g][:, c * LANES:(c + 1) * LANES] for g in range(groups)], axis=1)
                                    for c in range(n_chunks)], axis=0)

        def own_lanes(h):
            lane = lax.broadcasted_iota(jnp.int32, (CHUNK, n_chunks * LANES), 1)
            return (lane % LANES) // G_HEAD_DIM == h % 2

        def spatial(w_ref, wide):
            return [sum(jnp.where(own_lanes(h), _dot(w_ref[h], wide[g]), 0.0) for h in (2 * g, 2 * g + 1))
                    for g in range(groups)]

        attn = jnp.concatenate([ol_ref[h][:, NOPE:] for h in range(HEADS)], axis=-1)
        za = za_ref[...]
        sig_a = _sigmoid(za)
        silu_a = za * sig_a
        out_a = attn * silu_a
        u = u_ref[...]
        ug = _gelu(u)
        vpre = v_ref[...]
        gv = _gelu(vpre)
        mu_v = jnp.mean(gv, axis=-1, keepdims=True)
        cen_v = gv - mu_v
        rstd_v = lax.rsqrt(jnp.mean(cen_v * cen_v, axis=-1, keepdims=True) + EPS)
        vhat = cen_v * rstd_v
        vg = vhat * sg_ref[...] + sb_ref[...]
        vg_b = vg.astype(BF16)
        sv = by_chunk(spatial(ws_ref, side_by_side(vg_b))) + jnp.tile(bsp_ref[...], (n_chunks, 1))
        sgu = ug * sv
        zb = zb_ref[...]
        sig_b = _sigmoid(zb)
        silu_b = zb * sig_b
        out_b = sgu * silu_b
        merged = jnp.concatenate([out_a, out_b], axis=-1).astype(BF16)
        r = DN_ALPHA * x_ref[...] + _dot(merged, wout_ref[...])
        mu = jnp.mean(r, axis=-1, keepdims=True)
        cen = r - mu
        rstd = lax.rsqrt(jnp.mean(cen * cen, axis=-1, keepdims=True) + EPS)
        xhat = cen * rstd
        hout = xhat * lg_ref[...] + lb_ref[...]
        err = hout - tgt_ref[...]
        row_loss = jnp.mean(err * err, axis=-1, keepdims=True)
        loss_ref[...] += jnp.broadcast_to(0.5 * jnp.sum(row_loss, axis=0, keepdims=True), loss_ref.shape)

        dh = err * (1.0 / D_MODEL)
        dlg_ref[...] += jnp.sum(dh * xhat, axis=0, keepdims=True)
        dlb_ref[...] += jnp.sum(dh, axis=0, keepdims=True)
        dxhat = dh * lg_ref[...]
        dr = rstd * (dxhat - jnp.mean(dxhat, axis=-1, keepdims=True)
                     - xhat * jnp.mean(dxhat * xhat, axis=-1, keepdims=True))
        dr_ref[...] = dr
        dr_b = dr.astype(BF16)
        dwout_ref[...] += _dot_tn(merged, dr_b)
        dmerged = _dot_nt(dr_b, wout_ref[...])
        d_out_a = dmerged[:, :G_WIDTH]
        d_out_b = dmerged[:, G_WIDTH:]
        dattn = d_out_a * silu_a
        for h in range(HEADS):
            do_h = dattn[:, h * VDIM:(h + 1) * VDIM]
            do_ref[h] = jnp.concatenate([jnp.zeros((tm, NOPE), F32), do_h], axis=-1).astype(BF16)
        feature = lax.broadcasted_iota(jnp.int32, (G_WIDTH, LANES), 0) // VDIM
        column = lax.broadcasted_iota(jnp.int32, (G_WIDTH, LANES), 1)
        head_sums = jnp.dot(dattn * attn, jnp.where(feature == column, 1.0, 0.0).astype(F32),
                            preferred_element_type=F32, precision=lax.Precision.HIGH)
        dsums_t = jnp.transpose(head_sums)
        for h in range(HEADS):
            drow_ref[h] = dsums_t[h:h + 1, :]
        dza = d_out_a * attn * (sig_a * (1.0 + za * (1.0 - sig_a)))
        dsgu = d_out_b * silu_b
        dzb = d_out_b * sgu * (sig_b * (1.0 + zb * (1.0 - sig_b)))
        du = dsgu * sv * _gelu_grad(u)
        dsv = dsgu * ug
        dsv_b = dsv.astype(BF16)
        for cix in range(n_chunks):
            dbsp_acc[...] += dsv[cix * CHUNK:(cix + 1) * CHUNK, :]
        dsv_wide, vg_wide = side_by_side(dsv_b), side_by_side(vg_b)
        dvg = by_chunk(spatial(wst_ref, dsv_wide))
        for h in range(HEADS):
            mine = jnp.where(own_lanes(h), dsv_wide[h // 2], jnp.zeros_like(dsv_wide[h // 2]))
            dws_ref[h] += _dot_nt(mine, vg_wide[h // 2])
        dsg_ref[...] += jnp.sum(dvg * vhat, axis=0, keepdims=True)
        dsb_ref[...] += jnp.sum(dvg, axis=0, keepdims=True)
        dvhat = dvg * sg_ref[...]
        dgv = rstd_v * (dvhat - jnp.mean(dvhat, axis=-1, keepdims=True)
                        - vhat * jnp.mean(dvhat * vhat, axis=-1, keepdims=True))
        dv = dgv * _gelu_grad(vpre)
        drest_ref[...] = jnp.concatenate([dza, du, dv, dzb], axis=-1).astype(BF16)

        @pl.when(step == n_steps - 1)
        def _():
            tri = (lax.broadcasted_iota(jnp.int32, (CHUNK, CHUNK), 0)
                   >= lax.broadcasted_iota(jnp.int32, (CHUNK, CHUNK), 1))
            for h in range(HEADS):
                dws_ref[h] = jnp.where(tri, dws_ref[h], 0.0)
            tot = dbsp_acc[...]
            lane = lax.broadcasted_iota(jnp.int32, (CHUNK, LANES), 1)
            dbs = jnp.zeros((CHUNK, LANES), F32)
            for h in range(HEADS):
                head_sum = jnp.sum(tot[:, h * G_HEAD_DIM:(h + 1) * G_HEAD_DIM], axis=-1, keepdims=True)
                dbs = jnp.where(lane == h, head_sum, dbs)
            dbs_ref[...] = dbs

    full = lambda a: pl.BlockSpec(a.shape, lambda i: (0,) * a.ndim)
    tile = lambda w, j=0: pl.BlockSpec((tm, w), lambda i, j=j: (i, j))
    heads = pl.BlockSpec((HEADS, tm, HEAD_PAD), lambda i: (0, i, 0))
    acc = lambda shape: (pl.BlockSpec(shape, lambda i: (0,) * len(shape)), jax.ShapeDtypeStruct(shape, F32))
    accs = [acc((D_MODEL, D_MODEL)), acc((HEADS, CHUNK, CHUNK)), acc((CHUNK, LANES)), acc((1, D_MODEL)),
            acc((1, D_MODEL)), acc((1, G_WIDTH)), acc((1, G_WIDTH)), acc((1, LANES))]
    return pl.pallas_call(
        body, name="mid", grid=(n_steps,),
        in_specs=[tile(D_MODEL), tile(D_MODEL), tile(G_WIDTH, 1), tile(G_WIDTH, 2), tile(G_WIDTH, 3), tile(G_WIDTH, 4),
                  heads, full(w_out), full(ws_low), full(ws_low_t), full(bsp), full(sgu_g), full(sgu_b),
                  full(ln_g), full(ln_b)],
        out_specs=[tile(D_MODEL), heads, pl.BlockSpec((HEADS, 1, tm), lambda i: (0, 0, i)), tile(4 * G_WIDTH)]
        + [a[0] for a in accs],
        out_shape=[jax.ShapeDtypeStruct((t, D_MODEL), F32), jax.ShapeDtypeStruct((HEADS, t, HEAD_PAD), BF16),
                   jax.ShapeDtypeStruct((HEADS, 1, t), F32), jax.ShapeDtypeStruct((t, 4 * G_WIDTH), BF16)]
        + [a[1] for a in accs],
        scratch_shapes=[pltpu.VMEM((CHUNK, G_WIDTH), F32)],
        compiler_params=_cparams(("arbitrary",)),
    )(x, target, proj, proj, proj, proj, ol, w_out, ws_low, ws_low_t, bsp, sgu_g, sgu_b, ln_g, ln_b)


def _attn_bwd(q, k, v, do, lse_row, d_row):
    t = q.shape[1]
    bk, bq = ATTN_BWD_WIDE, ATTN_NARROW
    n_diag = bk // bq
    half = bq // 2
    last = t // bq - 1
    chunk = SOFTMAX_ROWS

    def body(q_ref, k_ref, v_ref, do_ref, lse_ref, drow_ref, dqt_ref, dk_ref, dv_ref,
             s0, s1, e0, e1, p0, p1, g0, g1, kt_scr):
        j = pl.program_id(1)
        at = lambda i: pl.ds(pl.multiple_of(i * bq, bq), bq)

        @pl.when(j == 0)
        def _():
            dqt_ref[...] = jnp.zeros_like(dqt_ref)

        kt_scr[...] = jnp.transpose(k_ref[0].astype(F32)).astype(BF16)
        dk_ref[...] = jnp.zeros_like(dk_ref)
        dv_ref[...] = jnp.zeros_like(dv_ref)

        whole_tile = ((slice(0, bk), slice(0, bq)),)

        def queries(i, lanes):
            return pl.ds(pl.multiple_of(i * bq + lanes.start, half), lanes.stop - lanes.start)

        def products(i, s_out, e_out, areas=whole_tile):
            i = jnp.minimum(i, last)
            for keys, lanes in areas:
                s_out[keys, lanes] = _dot_nt(k_ref[0, keys, :], q_ref[0, queries(i, lanes), :])
                e_out[keys, lanes] = _dot_nt(v_ref[0, keys, :], do_ref[0, queries(i, lanes), :])

        def gradients(i, p_in, g_in, areas=whole_tile):
            for keys, lanes in areas:
                dv_ref[0, keys, :] += _dot(p_in[keys, lanes], do_ref[0, queries(i, lanes), :])
                dk_ref[0, keys, :] += _dot(g_in[keys, lanes], q_ref[0, queries(i, lanes), :])
                dqt_ref[0, :, queries(i, lanes)] += _dot(kt_scr[:, keys], g_in[keys, lanes])

        def elementwise(i, s_in, e_in, p_out, g_out, qry0=None, areas=whole_tile):
            for keys, lanes in areas:
                width = lanes.stop - lanes.start
                step = chunk if qry0 is None else half
                lse = lse_ref[0, :, queries(i, lanes)]
                dsum = drow_ref[0, :, queries(i, lanes)]
                for r in range(keys.start, keys.stop, step):
                    p = jnp.exp2(s_in[r:r + step, lanes] - lse)
                    if qry0 is not None:
                        key = lax.broadcasted_iota(jnp.int32, (step, width), 0) + r
                        qry = lax.broadcasted_iota(jnp.int32, (step, width), 1) + (qry0 + lanes.start)
                        p = jnp.where(qry >= key, p, 0.0)
                    p_out[r:r + step, lanes] = p.astype(BF16)
                    g_out[r:r + step, lanes] = (p * (e_in[r:r + step, lanes] - dsum)).astype(BF16)

        def one_pass(i, s_in, e_in, s_out, e_out, p_prev, g_prev, p_cur, g_cur):
            products(i + 1, s_out, e_out)
            gradients(i - 1, p_prev, g_prev)
            elementwise(i, s_in, e_in, p_cur, g_cur)

        first = n_diag * j

        def areas_of(u):
            if u >= n_diag:
                return whole_tile
            return ((slice(0, u * bq + half), slice(0, bq)), (slice(u * bq + half, (u + 1) * bq), slice(half, bq)))

        even, odd = (s0, e0, p0, g0), (s1, e1, p1, g1)
        products(first, s0, e0, areas_of(0))
        products(first + 1, s1, e1, areas_of(1))
        elementwise(first, s0, e0, p0, g0, qry0=0, areas=areas_of(0))
        for u in range(1, n_diag):
            (s_in, e_in, p_cur, g_cur), (s_out, e_out, p_prev, g_prev) = (odd, even) if u % 2 else (even, odd)
            products(first + u + 1, s_out, e_out, areas_of(u + 1))
            gradients(first + u - 1, p_prev, g_prev, areas_of(u - 1))
            elementwise(first + u, s_in, e_in, p_cur, g_cur, qry0=u * bq, areas=areas_of(u))
        corner = (slice(bk - half, bk), slice(0, half))
        p1[corner] = jnp.zeros((half, half), BF16)
        g1[corner] = jnp.zeros((half, half), BF16)

        def two_passes(n, _):
            i = first + n_diag + 2 * n
            one_pass(i, s0, e0, s1, e1, p1, g1, p0, g0)
            one_pass(i + 1, s1, e1, s0, e0, p0, g0, p1, g1)
            return 0

        lax.fori_loop(0, (last - first - n_diag + 1) // 2, two_passes, 0)
        gradients(last, p1, g1)
        dk_ref[0] = dk_ref[0] * LN2

    whole = pl.BlockSpec((1, t, HEAD_PAD), lambda h, j: (h, 0, 0))
    block = pl.BlockSpec((1, bk, HEAD_PAD), lambda h, j: (h, j, 0))
    rows = pl.BlockSpec((1, 1, t), lambda h, j: (h, 0, 0), pipeline_mode=pl.Buffered(1))
    shape = jax.ShapeDtypeStruct((HEADS, t, HEAD_PAD), F32)
    tile = lambda dtype: pltpu.VMEM((bk, bq), dtype)
    return pl.pallas_call(
        body, name="attn_bwd", grid=(HEADS, t // bk),
        in_specs=[whole, block, block, whole, rows, rows],
        out_specs=[pl.BlockSpec((1, HEAD_PAD, t), lambda h, j: (h, 0, 0)), block, block],
        out_shape=[jax.ShapeDtypeStruct((HEADS, HEAD_PAD, t), F32), shape, shape],
        scratch_shapes=[tile(F32), tile(F32), tile(F32), tile(F32), tile(BF16), tile(BF16),
                        tile(BF16), tile(BF16), pltpu.VMEM((HEAD_PAD, bk), BF16)],
        compiler_params=_cparams(("arbitrary", "arbitrary"), vmem_limit=ATTN_BWD_VMEM_LIMIT),
    )(q, k, v, do, lse_row, d_row)


def _bwd_tail(dq, dk, dv, proj, pos_col, invf_row, w_heads, q_g, kv_g, x, dr, drest, wp_in):
    t = proj.shape[0]
    tm = PROJ_TILE
    n_head = 4 * LANES

    def body(dq_ref, dk_ref, dv_ref, ph_ref, pos_ref, invf_ref, wh_ref, qg_ref, kvg_ref,
             x_ref, dr_ref, drest_ref, win_ref,
             gx_ref, dwin_ref, dwh_ref, dqg_ref, dkvg_ref):
        @pl.when(pl.program_id(0) == 0)
        def _():
            dwin_ref[...] = jnp.zeros_like(dwin_ref)
            dwh_ref[...] = jnp.zeros_like(dwh_ref)
            dqg_ref[...] = jnp.zeros_like(dqg_ref)
            dkvg_ref[...] = jnp.zeros_like(dkvg_ref)

        xb = x_ref[...].astype(BF16)
        dr_b = drest_ref[...]
        dwin_ref[:, n_head:] += _dot_tn(xb, dr_b)
        gx_rest = DN_ALPHA * dr_ref[...] + _dot_nt(dr_b, win_ref[:, n_head:])

        cos, s1, s2 = _rope_tables(pos_ref[...], invf_ref[...])
        lane = lax.broadcasted_iota(jnp.int32, (tm, LANES), 1)
        c_q = ph_ref[:, :Q_LORA]
        c_kv = ph_ref[:, Q_LORA:Q_LORA + KV_LORA]
        rstd_q = lax.rsqrt(jnp.mean(c_q * c_q, axis=-1, keepdims=True) + EPS)
        rstd_kv = lax.rsqrt(jnp.mean(c_kv * c_kv, axis=-1, keepdims=True) + EPS)
        qhat = c_q * rstd_q
        kvhat = c_kv * rstd_kv
        cqn = (qhat * qg_ref[...]).astype(BF16)
        ckvn = (kvhat * kvg_ref[...]).astype(BF16)
        dkr_rot = jnp.zeros((tm, LANES), F32)
        dq_heads, dkv_heads = [], []
        for h in range(HEADS):
            dq_heads.append(_rope(jnp.transpose(dq_ref[h]) * ATTN_SCALE, cos, s1, s2, -1.0).astype(BF16))
            dk_h = dk_ref[h]
            dkv_heads.append(jnp.where(lane < NOPE, dk_h, dv_ref[h]).astype(BF16))
            dkr_rot = dkr_rot + dk_h
        dq_all = jnp.concatenate(dq_heads, axis=1)
        dkv_all = jnp.concatenate(dkv_heads, axis=1)
        dwq_all = _dot_tn(cqn, dq_all)
        dwkv_all = _dot_tn(ckvn, dkv_all)
        for h in range(HEADS):
            dwh_ref[h, :Q_LORA, :] += dwq_all[:, h * HEAD_PAD:(h + 1) * HEAD_PAD]
            dwh_ref[h, Q_LORA:, :] += dwkv_all[:, h * HEAD_PAD:(h + 1) * HEAD_PAD]
        dcqn = _dot_nt(dq_all, jnp.concatenate([wh_ref[h, :Q_LORA, :] for h in range(HEADS)], axis=1))
        dckvn = _dot_nt(dkv_all, jnp.concatenate([wh_ref[h, Q_LORA:, :] for h in range(HEADS)], axis=1))
        rot_lanes = (lane >= KR_LO) & (lane < KR_LO + ROPE)
        dkr_raw = jnp.where(rot_lanes, _rope(dkr_rot, cos, s1, s2, -1.0), 0.0)
        dqg_ref[...] += jnp.sum(dcqn * qhat, axis=0, keepdims=True)
        dkvg_ref[...] += jnp.sum(dckvn * kvhat, axis=0, keepdims=True)
        dqh = dcqn * qg_ref[...]
        dkvh = dckvn * kvg_ref[...]
        dc_q = rstd_q * (dqh - qhat * jnp.mean(dqh * qhat, axis=-1, keepdims=True))
        dc_kv = rstd_kv * (dkvh - kvhat * jnp.mean(dkvh * kvhat, axis=-1, keepdims=True))
        dh_b = jnp.concatenate([dc_q, dc_kv, dkr_raw], axis=-1).astype(BF16)
        dwin_ref[:, :n_head] += _dot_tn(xb, dh_b)
        gx_ref[...] = gx_rest + _dot_nt(dh_b, win_ref[:, :n_head])

    full = lambda a: pl.BlockSpec(a.shape, lambda i: (0,) * a.ndim)
    tile = lambda w: pl.BlockSpec((tm, w), lambda i: (i, 0))
    heads = pl.BlockSpec((HEADS, tm, HEAD_PAD), lambda i: (0, i, 0))
    acc = lambda shape: (pl.BlockSpec(shape, lambda i: (0,) * len(shape)), jax.ShapeDtypeStruct(shape, F32))
    accs = [acc(wp_in.shape), acc(w_heads.shape), acc((1, Q_LORA)), acc((1, KV_LORA))]
    return pl.pallas_call(
        body, name="bwd_tail", grid=(t // tm,),
        in_specs=[pl.BlockSpec((HEADS, HEAD_PAD, tm), lambda i: (0, 0, i)), heads, heads, tile(n_head),
                  pl.BlockSpec((tm, 1), lambda i: (i, 0)), full(invf_row), full(w_heads), full(q_g), full(kv_g),
                  tile(D_MODEL), tile(D_MODEL), tile(drest.shape[1]), full(wp_in)],
        out_specs=[tile(D_MODEL)] + [a[0] for a in accs],
        out_shape=[jax.ShapeDtypeStruct((t, D_MODEL), F32)] + [a[1] for a in accs],
        compiler_params=_cparams(("arbitrary",), vmem_limit=BWD_TAIL_VMEM_LIMIT),
    )(dq, dk, dv, proj, pos_col, invf_row, w_heads, q_g, kv_g, x, dr, drest, wp_in)


def _adam_update(g, w, m, v):
    m_new = ADAM_B1 * m + (1.0 - ADAM_B1) * g
    v_new = ADAM_B2 * v + (1.0 - ADAM_B2) * (g * g)
    m_hat = m_new / (1.0 - ADAM_B1 ** ADAM_STEP)
    v_hat = v_new / (1.0 - ADAM_B2 ** ADAM_STEP)
    return -ADAM_LR * (m_hat / (jnp.sqrt(v_hat) + ADAM_EPS) + ADAM_WD * w), m_new, v_new


def _adam(parts, w, m, v, *, name, tile_rows, transposed=False):
    n, rows, cols = parts.shape
    lane_pad = -(-cols // LANES) * LANES
    own_rows = rows if transposed else w.shape[0]
    assert own_rows == rows or tile_rows == rows

    def body(p_ref, w_ref, m_ref, v_ref, g_ref, d_ref, nm_ref, nv_ref, *scratch):
        g = p_ref[0].astype(F32)
        for s in range(1, n):
            g = g + p_ref[s].astype(F32)
        if transposed:
            wide_ref, = scratch
            wide_ref[:, lane_pad - LANES:] = jnp.zeros((tile_rows, LANES), F32)
            wide_ref[:, :cols] = g
            g = jnp.transpose(wide_ref[...])[:cols]
        g_ref[...] = g
        d_ref[...], nm_ref[...], nv_ref[...] = _adam_update(g[:w_ref.shape[0]], w_ref[...], m_ref[...], v_ref[...])

    if transposed:
        flat = grad = pl.BlockSpec((cols, tile_rows), lambda i: (0, i))
        shape = grad_shape = jax.ShapeDtypeStruct((cols, rows), F32)
        scratch = [pltpu.VMEM((tile_rows, lane_pad), F32)]
    else:
        own_tile = min(tile_rows, own_rows)
        flat = pl.BlockSpec((own_tile, cols), lambda i: (i, 0))
        grad = pl.BlockSpec((tile_rows, cols), lambda i: (i, 0))
        shape, grad_shape = jax.ShapeDtypeStruct((own_rows, cols), F32), jax.ShapeDtypeStruct((rows, cols), F32)
        scratch = []
    return pl.pallas_call(
        body, name=name, grid=(rows // tile_rows,),
        in_specs=[pl.BlockSpec((n, tile_rows, cols), lambda i: (0, i, 0)), flat, flat, flat],
        out_specs=[grad, flat, flat, flat], out_shape=[grad_shape, shape, shape, shape], scratch_shapes=scratch,
        compiler_params=_cparams(("arbitrary",)),
    )(parts, w, m, v)


def _adam_replicated(rep_g, ws, ms, vs):
    count = len(ws)
    small_rows = REP_ROWS - CHUNK

    def body(g_ref, *refs):
        w_refs, m_refs, v_refs = refs[:count], refs[count:2 * count], refs[2 * count:3 * count]
        outs, slab_ref = refs[3 * count:7 * count], refs[7 * count]
        for d in range(N_DEV):
            slab_ref[d * small_rows:(d + 1) * small_rows, :] = g_ref[d, CHUNK:, :]
        at = 0
        for k, w_ref in enumerate(w_refs):
            if w_ref.ndim == 3:
                g = g_ref[:, :CHUNK, :]
            else:
                n_rows = w_ref.size // LANES
                g = slab_ref[at:at + n_rows, :].reshape(w_ref.shape)
                at += n_rows
            delta, m_new, v_new = _adam_update(g, w_ref[...], m_refs[k][...], v_refs[k][...])
            for which, val in enumerate((g, delta, m_new, v_new)):
                outs[which * count + k][...] = val

    shapes = [jax.ShapeDtypeStruct(w.shape, F32) for w in ws]
    res = pl.pallas_call(
        body, name="adam_rep", out_shape=shapes * 4,
        scratch_shapes=[pltpu.VMEM((N_DEV * small_rows, LANES), F32)],
        compiler_params=_cparams(),
    )(rep_g, *ws, *ms, *vs)
    return [res[which * count:(which + 1) * count] for which in range(4)]


def _pack_small(vals, last):
    flat = jnp.concatenate([v.reshape(-1) for v in vals])
    pad = SMALL_LEN - flat.shape[0]
    return jnp.concatenate([flat, jnp.zeros((pad - 1,), F32), last.reshape(1)])


UQ_SHARD = HEADS * (NOPE + ROPE) // N_DEV
HEAD_ROWS = Q_LORA + KV_LORA
MIXED_ROWS = HEAD_ROWS + CHUNK + SMALL_LEN // N_DEV // LANES


def _head_slab(w_uq_shard, w_ukv_shard):
    return jnp.concatenate([jnp.pad(w_uq_shard, ((0, 0), (0, LANES - UQ_SHARD))), w_ukv_shard])


IN_SHARD = D_IN // N_DEV


def _w_in_pieces():
    split = Q_LORA + KV_LORA
    moves = ((0, split, 0), (split, split + ROPE, KR_LO), (split + ROPE, D_IN, LANES - ROPE))
    pieces = []
    for s in range(N_DEV):
        lo, hi = s * IN_SHARD, (s + 1) * IN_SHARD
        for a, b, shift in moves:
            a, b = max(a, lo), min(b, hi)
            if a < b:
                pieces.append((s, a - lo, a + shift, b - a))
    return pieces


def _w_in_shards(dwp_in):
    tr = TOKEN_TILE
    by_shard = [[p for p in _w_in_pieces() if p[0] == s] for s in range(N_DEV)]

    def body(w_ref, o_ref):
        for s, pieces in enumerate(by_shard):
            parts = [w_ref[:, dst:dst + width] for _, _, dst, width in pieces]
            o_ref[s] = parts[0] if len(parts) == 1 else jnp.concatenate(parts, axis=1)

    return pl.pallas_call(
        body, name="w_in_split", grid=(D_MODEL // tr,),
        in_specs=[pl.BlockSpec((tr, D_IN_PAD), lambda i: (i, 0))],
        out_specs=pl.BlockSpec((N_DEV, tr, IN_SHARD), lambda i: (0, i, 0)),
        out_shape=jax.ShapeDtypeStruct((N_DEV, D_MODEL, IN_SHARD), dwp_in.dtype),
        compiler_params=_cparams(("arbitrary",)),
    )(dwp_in)


def kernel(x, positions, w_in, q_norm_g, w_uq, kv_norm_g, w_ukv, sgu_norm_g, sgu_norm_b, w_spatial, b_spatial, w_out, ln_g, ln_b, loss_target, m_w_in, m_q_norm_g, m_w_uq, m_kv_norm_g, m_w_ukv, m_sgu_norm_g, m_sgu_norm_b, m_w_spatial, m_b_spatial, m_w_out, m_ln_g, m_ln_b, v_w_in, v_q_norm_g, v_w_uq, v_kv_norm_g, v_w_ukv, v_sgu_norm_g, v_sgu_norm_b, v_w_spatial, v_b_spatial, v_w_out, v_ln_g, v_ln_b):
    seq = x.shape[1]
    x2 = x.reshape(seq, D_MODEL)
    tgt2 = loss_target.reshape(seq, D_MODEL)
    pos_col = positions.reshape(seq, 1)

    w_in_shards, w_out_shards, w_heads = _gather_two_level(
        [w_in.astype(BF16), w_out.astype(BF16), _head_slab(w_uq, w_ukv).astype(BF16)],
        name="wgather")
    (loss_part, grad_x, d_in, d_heads, d_out, d_ws, d_bs_t, d_lng, d_lnb, d_sgug, d_sgub, d_qg, d_kvg) = _local_step(
        x2, tgt2, pos_col, w_in_shards, w_heads, w_out_shards.reshape(D_MODEL, D_MODEL), q_norm_g, kv_norm_g,
        sgu_norm_g, sgu_norm_b, w_spatial, b_spatial, ln_g, ln_b)

    small_part = _pack_small([d_qg, d_kvg, d_sgug, d_sgub, d_bs_t[:, :HEADS].T, d_lng, d_lnb], last=loss_part[0, :1])
    mixed = jnp.concatenate([d_heads, d_ws, small_part.reshape(N_DEV, -1, LANES)], axis=1)
    by_chip = [g.reshape((N_CHIPS, 2) + g.shape[1:])
               for g in (d_in, d_out.reshape(N_DEV, D_MODEL // N_DEV, D_MODEL), mixed)]
    from_sibling = _sibling_swap(by_chip, name="gswap")
    core = lax.axis_index("c").astype(jnp.int32).reshape(1)
    pair_sums = [_pair_sum(a, b, core, name=nm, tile_rows=tr, out_dtype=dt) for a, b, nm, tr, dt in zip(
        by_chip, from_sibling, ("gsum_in", "gsum_out", "gsum_mixed"), (TOKEN_TILE, D_MODEL // N_DEV, MIXED_ROWS),
        (BF16, BF16, F32))]
    recv_in, recv_out, recv_mixed = _chip_exchange(pair_sums, name="gexch")

    res_in = [a.T for a in _adam(recv_in, w_in.T, m_w_in.T, v_w_in.T, name="adam_in", tile_rows=TOKEN_TILE,
                                 transposed=True)]
    res_out = _adam(recv_out, w_out, m_w_out, v_w_out, name="adam_out", tile_rows=D_MODEL // N_DEV)
    res_mixed = _adam(recv_mixed, _head_slab(w_uq, w_ukv), _head_slab(m_w_uq, m_w_ukv), _head_slab(v_w_uq, v_w_ukv),
                      name="adam_mixed", tile_rows=MIXED_ROWS)

    rep_g, = _gather_direct([res_mixed[0][HEAD_ROWS:]], name="sgather")
    res_rep = _adam_replicated(
        rep_g,
        [q_norm_g, kv_norm_g, sgu_norm_g, sgu_norm_b, w_spatial, b_spatial, ln_g, ln_b],
        [m_q_norm_g, m_kv_norm_g, m_sgu_norm_g, m_sgu_norm_b, m_w_spatial, m_b_spatial, m_ln_g, m_ln_b],
        [v_q_norm_g, v_kv_norm_g, v_sgu_norm_g, v_sgu_norm_b, v_w_spatial, v_b_spatial, v_ln_g, v_ln_b])

    def ordered(which):
        r_qg, r_kvg, r_sg, r_sb, r_ws, r_bs, r_lg, r_lb = res_rep[which]
        heads = res_mixed[which]
        return [res_in[which], r_qg, heads[:Q_LORA, :UQ_SHARD], r_kvg, heads[Q_LORA:HEAD_ROWS], r_sg, r_sb, r_ws, r_bs,
                res_out[which], r_lg, r_lb]

    loss = rep_g[N_DEV - 1, REP_ROWS - 1, LANES - 1]
    outs = [loss, grad_x.reshape(x.shape)]
    for which in range(4):
        outs += ordered(which)
    return tuple(outs)


def _local_step(x2, tgt2, pos_col, w_in_shards, w_heads, w_out_full, q_norm_g, kv_norm_g, sgu_norm_g, sgu_norm_b,
                w_spatial, b_spatial, ln_g, ln_b):
    half = jnp.arange(HALF, dtype=F32)
    inv_freq = 1.0 / (ROPE_THETA ** (half / HALF))
    invf_row = jnp.concatenate([jnp.zeros((KR_LO,), F32), inv_freq, inv_freq,
                                jnp.zeros((LANES - KR_LO - ROPE,), F32)]).reshape(1, LANES)
    tri = jnp.tril(jnp.ones((CHUNK, CHUNK), dtype=bool))
    ws_low = jnp.where(tri[None], w_spatial, 0.0).astype(BF16)
    ws_low_t = ws_low.transpose(0, 2, 1)
    bsp = jnp.repeat(b_spatial.T, G_HEAD_DIM, axis=1)
    row = lambda a: a.reshape(1, -1)

    proj, q, k, v, vt, wp_in = _fwd_proj(x2, pos_col, invf_row, w_in_shards, w_heads, row(q_norm_g), row(kv_norm_g))
    o, lse_row = _attn_fwd(q, k, vt)
    (dr, do, d_row, drest, d_out, d_ws, d_bs_t, d_lng, d_lnb, d_sgug, d_sgub, loss_part) = _mid(
        x2, tgt2, proj, o, w_out_full, ws_low, ws_low_t, bsp, row(sgu_norm_g), row(sgu_norm_b), row(ln_g), row(ln_b))
    dqt, dk, dv = _attn_bwd(q, k, v, do, lse_row, d_row)
    grad_x, dwp_in, d_heads, d_qg, d_kvg = _bwd_tail(dqt, dk, dv, proj, pos_col, invf_row, w_heads, row(q_norm_g),
                                                      row(kv_norm_g), x2, dr, drest, wp_in)
    return (loss_part, grad_x, _w_in_shards(dwp_in), d_heads, d_out, d_ws, d_bs_t, d_lng, d_lnb, d_sgug, d_sgub,
            d_qg, d_kvg)
```

```python
import math

import jax
import jax.numpy as jnp
from jax import lax
from jax.experimental import pallas as pl
from jax.experimental.pallas import tpu as pltpu

F32 = jnp.float32
BF16 = jnp.bfloat16

N_DEV = 8
D_MODEL = 1024
HEADS = 8
NOPE = 64
ROPE = 32
HALF = ROPE // 2
VDIM = 64
Q_LORA = 256
KV_LORA = 128
G_WIDTH = 512
G_HEAD_DIM = 64
CHUNK = 128
HEAD_PAD = 128
D_IN = 2464
D_IN_PAD = 2560
KR_LO = NOPE
SUM_ROW = NOPE - 1
LIVE_ROWS = slice(NOPE - 16, HEAD_PAD)
ROPE_THETA = 10000.0
DN_ALPHA = 2.0 ** 0.25
EPS = 1e-5
ATTN_SCALE = 1.0 / math.sqrt(NOPE + ROPE)
ADAM_LR, ADAM_B1, ADAM_B2, ADAM_EPS, ADAM_WD, ADAM_STEP = 0.001, 0.9, 0.999, 1e-08, 0.01, 10

LANES = 128
REP_ROWS = 136
SMALL_LEN = 8192
VMEM_LIMIT = 56 * 1024 * 1024
ATTN_BWD_VMEM_LIMIT = 61 * 1024 * 1024
BWD_TAIL_VMEM_LIMIT = 61 * 1024 * 1024

TOKEN_TILE = 256
PROJ_TILE = 512
ATTN_FWD_WIDE = 2048
ATTN_BWD_WIDE = 2048
ATTN_NARROW = 512
SOFTMAX_ROWS = 512
LOG2E = 1.4426950408889634
LN2 = 0.6931471805599453
Q_PRESCALE = ATTN_SCALE * LOG2E


def _cparams(sem=None, vmem_limit=VMEM_LIMIT):
    return pltpu.CompilerParams(dimension_semantics=sem, vmem_limit_bytes=vmem_limit)


def _dot(a, b):
    return jnp.dot(a, b, preferred_element_type=F32)


def _dot_nt(a, b):
    return lax.dot_general(a, b, (((1,), (1,)), ((), ())), preferred_element_type=F32)


def _dot_tn(a, b):
    return lax.dot_general(a, b, (((0,), (0,)), ((), ())), preferred_element_type=F32)


def _sigmoid(z):
    return 1.0 / (1.0 + jnp.exp(-z))


def _gelu(x):
    return 0.5 * x * (1.0 + lax.erf(x * 0.7071067811865476))


def _gelu_grad(x):
    cdf = 0.5 * (1.0 + lax.erf(x * 0.7071067811865476))
    return cdf + x * jnp.exp(-0.5 * x * x) * 0.3989422804014327


def _gather_direct(srcs, *, name, first_row=0):
    n = len(srcs)
    shapes = [(s.shape[0] - first_row,) + s.shape[1:] for s in srcs]

    def body(*refs):
        src_refs, out_refs = [r.at[pl.ds(first_row, shape[0])] for r, shape in zip(refs[:n], shapes)], refs[n:2 * n]
        send_sems, recv_sems, local_sems = refs[2 * n:]
        x, y, c = lax.axis_index("x"), lax.axis_index("y"), lax.axis_index("c")
        me = 4 * x + 2 * y + c
        mine = [pltpu.make_async_copy(src_refs[t], out_refs[t].at[me], local_sems.at[t]) for t in range(n)]
        for cp in mine:
            cp.start()
        sends, arrivals = [], []
        for k in (6, 7, 4, 5, 2, 3, 1):
            px = 1 - x if k & 4 else x
            py = 1 - y if k & 2 else y
            pc = 1 - c if k & 1 else c
            peer = 4 * px + 2 * py + pc
            for t in range(n):
                sem = (k - 1) * n + t
                cp = pltpu.make_async_remote_copy(
                    src_ref=src_refs[t], dst_ref=out_refs[t].at[me],
                    send_sem=send_sems.at[sem], recv_sem=recv_sems.at[sem],
                    device_id=(px, py, pc), device_id_type=pl.DeviceIdType.MESH)
                cp.start()
                sends.append(cp)
                arrivals.append(pltpu.make_async_remote_copy(
                    src_ref=src_refs[t], dst_ref=out_refs[t].at[peer],
                    send_sem=send_sems.at[sem], recv_sem=recv_sems.at[sem],
                    device_id=(x, y, c), device_id_type=pl.DeviceIdType.MESH))
        for cp in arrivals:
            cp.wait_recv()
        for cp in sends:
            cp.wait_send()
        for cp in mine:
            cp.wait()

    hbm = pl.BlockSpec(memory_space=pl.ANY)
    return pl.pallas_call(
        body, name=name,
        out_shape=[jax.ShapeDtypeStruct((N_DEV,) + shape, s.dtype) for shape, s in zip(shapes, srcs)],
        in_specs=[hbm] * n, out_specs=[hbm] * n,
        scratch_shapes=[pltpu.SemaphoreType.DMA(((N_DEV - 1) * n,)), pltpu.SemaphoreType.DMA(((N_DEV - 1) * n,)),
                        pltpu.SemaphoreType.DMA((n,))],
    )(*srcs)


def _gather_two_level(srcs, *, name):
    n = len(srcs)

    def body(*refs):
        src_refs, out_refs = refs[:n], refs[n:2 * n]
        send_sems, recv_sems, local_sems = refs[2 * n:]
        x, y, c = lax.axis_index("x"), lax.axis_index("y"), lax.axis_index("c")
        me, sibling = (x, y, c), (x, y, 1 - c)
        chips = [(1 - x, 1 - y), (1 - x, y), (x, 1 - y)]
        index = lambda px, py, pc: 4 * px + 2 * py + pc

        def copy(k, t, block, to, src=None):
            place = out_refs[t].at[index(*block)]
            return pltpu.make_async_remote_copy(
                src_ref=place if src is None else src, dst_ref=place,
                send_sem=send_sems.at[k * n + t], recv_sem=recv_sems.at[k * n + t],
                device_id=to, device_id_type=pl.DeviceIdType.MESH)

        mine = [pltpu.make_async_copy(src_refs[t], out_refs[t].at[index(*me)], local_sems.at[t]) for t in range(n)]
        for cp in mine:
            cp.start()
        first = [copy(1 + j, t, me, (*chip, c), src=src_refs[t]) for j, chip in enumerate(chips) for t in range(n)]
        first += [copy(0, t, me, sibling, src=src_refs[t]) for t in range(n)]
        for cp in first:
            cp.start()
        passed = []
        for j, chip in enumerate(chips):
            for t in range(n):
                copy(1 + j, t, (*chip, c), me).wait_recv()
                cp = copy(4 + j, t, (*chip, c), sibling)
                cp.start()
                passed.append(cp)
        for t in range(n):
            copy(0, t, sibling, me).wait_recv()
        for j, chip in enumerate(chips):
            for t in range(n):
                copy(4 + j, t, (*chip, 1 - c), me).wait_recv()
        for cp in first + passed:
            cp.wait_send()
        for cp in mine:
            cp.wait()

    hbm = pl.BlockSpec(memory_space=pl.ANY)
    return pl.pallas_call(
        body, name=name,
        out_shape=[jax.ShapeDtypeStruct((N_DEV,) + s.shape, s.dtype) for s in srcs],
        in_specs=[hbm] * n, out_specs=[hbm] * n,
        scratch_shapes=[pltpu.SemaphoreType.DMA((7 * n,)), pltpu.SemaphoreType.DMA((7 * n,)),
                        pltpu.SemaphoreType.DMA((n,))],
    )(*srcs)


N_CHIPS = N_DEV // 2


def _sibling_swap(srcs, *, name):
    n = len(srcs)

    def body(*refs):
        src_refs, out_refs = refs[:n], refs[n:2 * n]
        send_sems, recv_sems = refs[2 * n:]
        x, y, c = lax.axis_index("x"), lax.axis_index("y"), lax.axis_index("c")
        sends = []
        for chip in range(N_CHIPS):
            for t in range(n):
                cp = pltpu.make_async_remote_copy(
                    src_ref=src_refs[t].at[chip, 1 - c], dst_ref=out_refs[t].at[chip],
                    send_sem=send_sems.at[chip * n + t], recv_sem=recv_sems.at[chip * n + t],
                    device_id=(x, y, 1 - c), device_id_type=pl.DeviceIdType.MESH)
                cp.start()
                sends.append(cp)
        for cp in sends:
            cp.wait_recv()
        for cp in sends:
            cp.wait_send()

    hbm = pl.BlockSpec(memory_space=pl.ANY)
    return pl.pallas_call(
        body, name=name,
        out_shape=[jax.ShapeDtypeStruct((N_CHIPS,) + s.shape[2:], s.dtype) for s in srcs],
        in_specs=[hbm] * n, out_specs=[hbm] * n,
        scratch_shapes=[pltpu.SemaphoreType.DMA((N_CHIPS * n,)), pltpu.SemaphoreType.DMA((N_CHIPS * n,))],
    )(*srcs)


def _pair_sum(mine, theirs, core, *, name, out_dtype):
    _, _, rows, cols = mine.shape

    def body(core_ref, a_ref, b_ref, o_ref):
        o_ref[...] = (a_ref[0] + b_ref[...]).astype(out_dtype)

    return pl.pallas_call(
        body, name=name,
        grid_spec=pltpu.PrefetchScalarGridSpec(
            num_scalar_prefetch=1, grid=(N_CHIPS,),
            in_specs=[pl.BlockSpec((1, 1, rows, cols), lambda q, core_ref: (q, core_ref[0], 0, 0)),
                      pl.BlockSpec((1, rows, cols), lambda q, core_ref: (q, 0, 0))],
            out_specs=pl.BlockSpec((1, rows, cols), lambda q, core_ref: (q, 0, 0))),
        out_shape=jax.ShapeDtypeStruct((N_CHIPS, rows, cols), out_dtype),
        compiler_params=_cparams(("arbitrary",)),
    )(core, mine, theirs)


def _chip_exchange(srcs, *, name):
    n = len(srcs)

    def body(*refs):
        src_refs, out_refs = refs[:n], refs[n:2 * n]
        send_sems, recv_sems, local_sems = refs[2 * n:]
        x, y, c = lax.axis_index("x"), lax.axis_index("y"), lax.axis_index("c")
        my_chip = 2 * x + y
        mine = [pltpu.make_async_copy(src_refs[t].at[my_chip], out_refs[t].at[my_chip], local_sems.at[t])
                for t in range(n)]
        for cp in mine:
            cp.start()
        sends, arrivals = [], []
        for k in (3, 2, 1):
            px = 1 - x if k & 2 else x
            py = 1 - y if k & 1 else y
            peer_chip = 2 * px + py
            for t in range(n):
                sem = (k - 1) * n + t
                cp = pltpu.make_async_remote_copy(
                    src_ref=src_refs[t].at[peer_chip], dst_ref=out_refs[t].at[my_chip],
                    send_sem=send_sems.at[sem], recv_sem=recv_sems.at[sem],
                    device_id=(px, py, c), device_id_type=pl.DeviceIdType.MESH)
                cp.start()
                sends.append(cp)
                arrivals.append(pltpu.make_async_remote_copy(
                    src_ref=src_refs[t].at[peer_chip], dst_ref=out_refs[t].at[peer_chip],
                    send_sem=send_sems.at[sem], recv_sem=recv_sems.at[sem],
                    device_id=(x, y, c), device_id_type=pl.DeviceIdType.MESH))
        for cp in arrivals:
            cp.wait_recv()
        for cp in sends:
            cp.wait_send()
        for cp in mine:
            cp.wait()

    hbm = pl.BlockSpec(memory_space=pl.ANY)
    return pl.pallas_call(
        body, name=name,
        out_shape=[jax.ShapeDtypeStruct(s.shape, s.dtype) for s in srcs],
        in_specs=[hbm] * n, out_specs=[hbm] * n,
        scratch_shapes=[pltpu.SemaphoreType.DMA((3 * n,)), pltpu.SemaphoreType.DMA((3 * n,)),
                        pltpu.SemaphoreType.DMA((n,))],
    )(*srcs)


def _rope_tables(pos_col, invf_row):
    ang = pos_col.astype(F32) * invf_row
    lane = lax.broadcasted_iota(jnp.int32, ang.shape, 1)
    cos, sin = jnp.cos(ang), jnp.sin(ang)
    first = (lane >= KR_LO) & (lane < KR_LO + HALF)
    second = (lane >= KR_LO + HALF) & (lane < KR_LO + ROPE)
    return cos, jnp.where(first, sin, 0.0), jnp.where(second, sin, 0.0)


def _rope(t, cos, sin_first, sin_second, sign):
    up = pltpu.roll(t, LANES - HALF, 1)
    down = pltpu.roll(t, HALF, 1)
    return t * cos - sign * (up * sin_first) + sign * (down * sin_second)


def _fwd_proj(x, pos_col, invf_row, w_in_shards, w_heads, q_g, kv_g):
    t = x.shape[0]
    tm = PROJ_TILE

    def body(x_ref, pos_ref, invf_ref, sh_ref, wh_ref, qg_ref, kvg_ref,
             proj_ref, q_ref, k_ref, v_ref, vt_ref, win_ref):
        @pl.when(pl.program_id(0) == 0)
        def _():
            win_ref[...] = jnp.zeros_like(win_ref)
            for s, src, dst, width in _w_in_pieces():
                win_ref[:, dst:dst + width] = sh_ref[s, :, src:src + width]

        proj = _dot(x_ref[...].astype(BF16), win_ref[...])
        proj_ref[...] = proj
        c_q = proj[:, :Q_LORA]
        c_kv = proj[:, Q_LORA:Q_LORA + KV_LORA]
        kr_raw = proj[:, Q_LORA + KV_LORA:Q_LORA + KV_LORA + LANES]
        cqn = (c_q * lax.rsqrt(jnp.mean(c_q * c_q, axis=-1, keepdims=True) + EPS) * qg_ref[...]).astype(BF16)
        ckvn = (c_kv * lax.rsqrt(jnp.mean(c_kv * c_kv, axis=-1, keepdims=True) + EPS) * kvg_ref[...]).astype(BF16)
        cos, s1, s2 = _rope_tables(pos_ref[...], invf_ref[...])
        kr = _rope(kr_raw, cos, s1, s2, 1.0)
        lane = lax.broadcasted_iota(jnp.int32, (tm, HEAD_PAD), 1)
        q_all = _dot(cqn, jnp.concatenate([wh_ref[h, :Q_LORA, :] for h in range(HEADS)], axis=1))
        kv_all = _dot(ckvn, jnp.concatenate([wh_ref[h, Q_LORA:, :] for h in range(HEADS)], axis=1))
        for h in range(HEADS):
            q_h = q_all[:, h * HEAD_PAD:(h + 1) * HEAD_PAD]
            kv_h = kv_all[:, h * HEAD_PAD:(h + 1) * HEAD_PAD]
            q_ref[h] = (_rope(q_h, cos, s1, s2, 1.0) * Q_PRESCALE).astype(BF16)
            k_ref[h] = jnp.where(lane < NOPE, kv_h, kr).astype(BF16)
            v_ref[h] = kv_h.astype(BF16)
            vt_ref[h] = jnp.transpose(jnp.where(lane == SUM_ROW, 1.0, kv_h)).astype(BF16)

    full = lambda a: pl.BlockSpec(a.shape, lambda i: (0,) * a.ndim)
    head_spec = pl.BlockSpec((HEADS, tm, HEAD_PAD), lambda i: (0, i, 0))
    head_shape = jax.ShapeDtypeStruct((HEADS, t, HEAD_PAD), BF16)
    return pl.pallas_call(
        body, name="fwd_proj", grid=(t // tm,),
        in_specs=[pl.BlockSpec((tm, D_MODEL), lambda i: (i, 0)), pl.BlockSpec((tm, 1), lambda i: (i, 0)),
                  full(invf_row), full(w_in_shards), full(w_heads), full(q_g), full(kv_g)],
        out_specs=[pl.BlockSpec((tm, D_IN_PAD), lambda i: (i, 0)), head_spec, head_spec, head_spec,
                   pl.BlockSpec((HEADS, HEAD_PAD, tm), lambda i: (0, 0, i)),
                   pl.BlockSpec((D_MODEL, D_IN_PAD), lambda i: (0, 0))],
        out_shape=[jax.ShapeDtypeStruct((t, D_IN_PAD), F32), head_shape, head_shape, head_shape,
                   jax.ShapeDtypeStruct((HEADS, HEAD_PAD, t), BF16),
                   jax.ShapeDtypeStruct((D_MODEL, D_IN_PAD), w_in_shards.dtype)],
        compiler_params=_cparams(("arbitrary",)),
    )(x, pos_col, invf_row, w_in_shards, w_heads, q_g, kv_g)


def _attn_fwd(q, k, vt):
    t = q.shape[1]
    bq, bk = ATTN_FWD_WIDE, ATTN_NARROW
    n_diag = bq // bk
    chunk = SOFTMAX_ROWS

    def body(q_ref, k_ref, vt_ref, o_ref, lse_ref, s0, s1, p0, p1, x0, x1, m_scr, a_scr, acc_scr):
        i = pl.program_id(1)
        at = lambda j: pl.ds(pl.multiple_of(j * bk, bk), bk)

        def exp_pass(s_in, block_max, p_out, diagonal=False, cols=slice(None)):
            width = bq if cols == slice(None) else cols.stop - cols.start

            def load(r):
                s = s_in[r:r + chunk, cols]
                if diagonal:
                    key = lax.broadcasted_iota(jnp.int32, (chunk, width), 0) + r
                    qry = lax.broadcasted_iota(jnp.int32, (chunk, width), 1)
                    s = jnp.where(qry >= key, s, -jnp.inf)
                return s

            if diagonal:
                block_max = jnp.max(load(0), axis=0, keepdims=True)
                for r in range(chunk, bk, chunk):
                    block_max = jnp.maximum(block_max, jnp.max(load(r), axis=0, keepdims=True))
            m_old = m_scr[:, cols]
            m_new = jnp.maximum(m_old, block_max)
            alpha = jnp.exp2(m_old - m_new)
            for r in range(0, bk, chunk):
                p_out[r:r + chunk, cols] = jnp.exp2(load(r) - m_new).astype(BF16)
            m_scr[:, cols] = m_new
            return alpha

        def scores(j, s_out, x_out):
            s = _dot_nt(k_ref[0, at(j), :], q_ref[0])
            s_out[...] = s
            x_out[...] = jnp.max(s, axis=0, keepdims=True)

        def value_product(j, p_in):
            return _dot(vt_ref[0, LIVE_ROWS, at(j)], p_in[...])

        def one_pass(j, s_in, x_in, s_out, x_out, p_prev, p_cur):
            scores(j + 1, s_out, x_out)
            acc_scr[...] = a_scr[...] * acc_scr[...] + value_product(jnp.maximum(j - 1, 0), p_prev)
            a_scr[...] = exp_pass(s_in, x_in[...], p_cur)

        scores(0, s0, x0)
        p1[...] = jnp.zeros_like(p1)
        a_scr[...] = jnp.ones_like(a_scr)
        m_scr[...] = jnp.full(m_scr.shape, -jnp.inf, F32)
        acc_scr[...] = jnp.zeros_like(acc_scr)

        def two_passes(n, _):
            one_pass(2 * n, s0, x0, s1, x1, p1, p0)
            one_pass(2 * n + 1, s1, x1, s0, x0, p0, p1)
            return 0

        lax.fori_loop(0, (n_diag // 2) * i, two_passes, 0)
        d = n_diag * i
        alpha, p_prev, cols = a_scr[...], p1, slice(0, bq)
        for u in range(n_diag + 1):
            s_in, s_next, p_cur = (s0, s1, p0) if u % 2 == 0 else (s1, s0, p1)
            if u + 1 < n_diag:
                ahead = slice((u + 1) * bk, bq)
                s_next[:, ahead] = _dot_nt(k_ref[0, at(d + u + 1), :], q_ref[0, ahead, :])
            acc_scr[:, cols] = alpha * acc_scr[:, cols] + _dot(vt_ref[0, LIVE_ROWS, at(jnp.maximum(d + u - 1, 0))],
                                                               p_prev[:, cols])
            if u < n_diag:
                cols = slice(u * bk, bq)
                alpha = exp_pass(s_in, None, p_cur, diagonal=True, cols=cols)
                p_prev = p_cur
        denom = acc_scr[SUM_ROW - LIVE_ROWS.start:NOPE - LIVE_ROWS.start, :]
        o = jnp.transpose(acc_scr[NOPE - LIVE_ROWS.start:, :] / denom)
        o_ref[0] = jnp.concatenate([jnp.zeros_like(o), o], axis=1)
        lse_ref[0] = m_scr[...] + jnp.log2(denom)

    tile = lambda dtype: pltpu.VMEM((bk, bq), dtype)
    stat = pltpu.VMEM((1, bq), F32)
    return pl.pallas_call(
        body, name="attn_fwd", grid=(HEADS, t // bq),
        in_specs=[pl.BlockSpec((1, bq, HEAD_PAD), lambda h, i: (h, i, 0)),
                  pl.BlockSpec((1, t, HEAD_PAD), lambda h, i: (h, 0, 0)),
                  pl.BlockSpec((1, HEAD_PAD, t), lambda h, i: (h, 0, 0))],
        out_specs=[pl.BlockSpec((1, bq, HEAD_PAD), lambda h, i: (h, i, 0)),
                   pl.BlockSpec((1, 1, bq), lambda h, i: (h, 0, i))],
        out_shape=[jax.ShapeDtypeStruct((HEADS, t, HEAD_PAD), F32), jax.ShapeDtypeStruct((HEADS, 1, t), F32)],
        scratch_shapes=[tile(F32), tile(F32), tile(BF16), tile(BF16), stat, stat, stat, stat,
                        pltpu.VMEM((HEAD_PAD - LIVE_ROWS.start, bq), F32)],
        compiler_params=_cparams(("arbitrary", "arbitrary")),
    )(q, k, vt)


def _mid(x, target, proj, ol, w_out, ws_low, ws_low_t, bsp, sgu_g, sgu_b, ln_g, ln_b):
    t = x.shape[0]
    tm = TOKEN_TILE
    n_steps = t // tm

    def body(x_ref, tgt_ref, za_ref, u_ref, v_ref, zb_ref, ol_ref, wout_ref, ws_ref, wst_ref, bsp_ref,
             sg_ref, sb_ref, lg_ref, lb_ref,
             dr_ref, do_ref, drow_ref, drest_ref, dwout_ref, dws_ref, dbs_ref, dlg_ref, dlb_ref, dsg_ref, dsb_ref,
             loss_ref, dbsp_acc):
        step = pl.program_id(0)

        @pl.when(step == 0)
        def _():
            dwout_ref[...] = jnp.zeros_like(dwout_ref)
            dws_ref[...] = jnp.zeros_like(dws_ref)
            dbs_ref[...] = jnp.zeros_like(dbs_ref)
            dlg_ref[...] = jnp.zeros_like(dlg_ref)
            dlb_ref[...] = jnp.zeros_like(dlb_ref)
            dsg_ref[...] = jnp.zeros_like(dsg_ref)
            dsb_ref[...] = jnp.zeros_like(dsb_ref)
            loss_ref[...] = jnp.zeros_like(loss_ref)
            dbsp_acc[...] = jnp.zeros_like(dbsp_acc)

        n_chunks = tm // CHUNK
        groups = G_WIDTH // LANES

        def side_by_side(a):
            return [jnp.concatenate([a[c * CHUNK:(c + 1) * CHUNK, g * LANES:(g + 1) * LANES] for c in range(n_chunks)],
                                    axis=1) for g in range(groups)]

        def by_chunk(wide):
            return jnp.concatenate([jnp.concatenate([wide[g][:, c * LANES:(c + 1) * LANES] for g in range(groups)], axis=1)
                                    for c in range(n_chunks)], axis=0)

        def own_lanes(h):
            lane = lax.broadcasted_iota(jnp.int32, (CHUNK, n_chunks * LANES), 1)
            return (lane % LANES) // G_HEAD_DIM == h % 2

        def spatial(w_ref, wide):
            return [sum(jnp.where(own_lanes(h), _dot(w_ref[h], wide[g]), 0.0) for h in (2 * g, 2 * g + 1))
                    for g in range(groups)]

        attn = jnp.concatenate([ol_ref[h][:, NOPE:] for h in range(HEADS)], axis=-1)
        za = za_ref[...]
        sig_a = _sigmoid(za)
        silu_a = za * sig_a
        out_a = attn * silu_a
        u = u_ref[...]
        ug = _gelu(u)
        vpre = v_ref[...]
        gv = _gelu(vpre)
        mu_v = jnp.mean(gv, axis=-1, keepdims=True)
        cen_v = gv - mu_v
        rstd_v = lax.rsqrt(jnp.mean(cen_v * cen_v, axis=-1, keepdims=True) + EPS)
        vhat = cen_v * rstd_v
        vg = vhat * sg_ref[...] + sb_ref[...]
        vg_b = vg.astype(BF16)
        sv = by_chunk(spatial(ws_ref, side_by_side(vg_b))) + jnp.tile(bsp_ref[...], (n_chunks, 1))
        sgu = ug * sv
        zb = zb_ref[...]
        sig_b = _sigmoid(zb)
        silu_b = zb * sig_b
        out_b = sgu * silu_b
        merged = jnp.concatenate([out_a, out_b], axis=-1).astype(BF16)
        r = DN_ALPHA * x_ref[...] + _dot(merged, wout_ref[...])
        mu = jnp.mean(r, axis=-1, keepdims=True)
        cen = r - mu
        rstd = lax.rsqrt(jnp.mean(cen * cen, axis=-1, keepdims=True) + EPS)
        xhat = cen * rstd
        hout = xhat * lg_ref[...] + lb_ref[...]
        err = hout - tgt_ref[...]
        row_loss = jnp.mean(err * err, axis=-1, keepdims=True)
        loss_ref[...] += jnp.broadcast_to(0.5 * jnp.sum(row_loss, axis=0, keepdims=True), loss_ref.shape)

        dh = err * (1.0 / D_MODEL)
        dlg_ref[...] += jnp.sum(dh * xhat, axis=0, keepdims=True)
        dlb_ref[...] += jnp.sum(dh, axis=0, keepdims=True)
        dxhat = dh * lg_ref[...]
        dr = rstd * (dxhat - jnp.mean(dxhat, axis=-1, keepdims=True)
                     - xhat * jnp.mean(dxhat * xhat, axis=-1, keepdims=True))
        dr_ref[...] = dr
        dr_b = dr.astype(BF16)
        dwout_ref[...] += _dot_tn(merged, dr_b)
        dmerged = _dot_nt(dr_b, wout_ref[...])
        d_out_a = dmerged[:, :G_WIDTH]
        d_out_b = dmerged[:, G_WIDTH:]
        dattn = d_out_a * silu_a
        for h in range(HEADS):
            do_h = dattn[:, h * VDIM:(h + 1) * VDIM]
            do_ref[h] = jnp.concatenate([jnp.zeros((tm, NOPE), F32), do_h], axis=-1).astype(BF16)
        feature = lax.broadcasted_iota(jnp.int32, (G_WIDTH, LANES), 0) // VDIM
        column = lax.broadcasted_iota(jnp.int32, (G_WIDTH, LANES), 1)
        head_sums = jnp.dot(dattn * attn, jnp.where(feature == column, 1.0, 0.0).astype(F32),
                            preferred_element_type=F32, precision=lax.Precision.HIGH)
        dsums_t = jnp.transpose(head_sums)
        for h in range(HEADS):
            drow_ref[h] = dsums_t[h:h + 1, :]
        dza = d_out_a * attn * (sig_a * (1.0 + za * (1.0 - sig_a)))
        dsgu = d_out_b * silu_b
        dzb = d_out_b * sgu * (sig_b * (1.0 + zb * (1.0 - sig_b)))
        du = dsgu * sv * _gelu_grad(u)
        dsv = dsgu * ug
        dsv_b = dsv.astype(BF16)
        for cix in range(n_chunks):
            dbsp_acc[...] += dsv[cix * CHUNK:(cix + 1) * CHUNK, :]
        dsv_wide, vg_wide = side_by_side(dsv_b), side_by_side(vg_b)
        dvg = by_chunk(spatial(wst_ref, dsv_wide))
        for h in range(HEADS):
            mine = jnp.where(own_lanes(h), dsv_wide[h // 2], jnp.zeros_like(dsv_wide[h // 2]))
            dws_ref[h] += _dot_nt(mine, vg_wide[h // 2])
        dsg_ref[...] += jnp.sum(dvg * vhat, axis=0, keepdims=True)
        dsb_ref[...] += jnp.sum(dvg, axis=0, keepdims=True)
        dvhat = dvg * sg_ref[...]
        dgv = rstd_v * (dvhat - jnp.mean(dvhat, axis=-1, keepdims=True)
                        - vhat * jnp.mean(dvhat * vhat, axis=-1, keepdims=True))
        dv = dgv * _gelu_grad(vpre)
        drest_ref[...] = jnp.concatenate([dza, du, dv, dzb], axis=-1).astype(BF16)

        @pl.when(step == n_steps - 1)
        def _():
            tri = (lax.broadcasted_iota(jnp.int32, (CHUNK, CHUNK), 0)
                   >= lax.broadcasted_iota(jnp.int32, (CHUNK, CHUNK), 1))
            for h in range(HEADS):
                dws_ref[h] = jnp.where(tri, dws_ref[h], 0.0)
            tot = dbsp_acc[...]
            lane = lax.broadcasted_iota(jnp.int32, (CHUNK, LANES), 1)
            dbs = jnp.zeros((CHUNK, LANES), F32)
            for h in range(HEADS):
                head_sum = jnp.sum(tot[:, h * G_HEAD_DIM:(h + 1) * G_HEAD_DIM], axis=-1, keepdims=True)
                dbs = jnp.where(lane == h, head_sum, dbs)
            dbs_ref[...] = dbs

    full = lambda a: pl.BlockSpec(a.shape, lambda i: (0,) * a.ndim)
    tile = lambda w, j=0: pl.BlockSpec((tm, w), lambda i, j=j: (i, j))
    heads = pl.BlockSpec((HEADS, tm, HEAD_PAD), lambda i: (0, i, 0))
    acc = lambda shape: (pl.BlockSpec(shape, lambda i: (0,) * len(shape)), jax.ShapeDtypeStruct(shape, F32))
    accs = [acc((D_MODEL, D_MODEL)), acc((HEADS, CHUNK, CHUNK)), acc((CHUNK, LANES)), acc((1, D_MODEL)),
            acc((1, D_MODEL)), acc((1, G_WIDTH)), acc((1, G_WIDTH)), acc((1, LANES))]
    return pl.pallas_call(
        body, name="mid", grid=(n_steps,),
        in_specs=[tile(D_MODEL), tile(D_MODEL), tile(G_WIDTH, 1), tile(G_WIDTH, 2), tile(G_WIDTH, 3), tile(G_WIDTH, 4),
                  heads, full(w_out), full(ws_low), full(ws_low_t), full(bsp), full(sgu_g), full(sgu_b),
                  full(ln_g), full(ln_b)],
        out_specs=[tile(D_MODEL), heads, pl.BlockSpec((HEADS, 1, tm), lambda i: (0, 0, i)), tile(4 * G_WIDTH)]
        + [a[0] for a in accs],
        out_shape=[jax.ShapeDtypeStruct((t, D_MODEL), F32), jax.ShapeDtypeStruct((HEADS, t, HEAD_PAD), BF16),
                   jax.ShapeDtypeStruct((HEADS, 1, t), F32), jax.ShapeDtypeStruct((t, 4 * G_WIDTH), BF16)]
        + [a[1] for a in accs],
        scratch_shapes=[pltpu.VMEM((CHUNK, G_WIDTH), F32)],
        compiler_params=_cparams(("arbitrary",)),
    )(x, target, proj, proj, proj, proj, ol, w_out, ws_low, ws_low_t, bsp, sgu_g, sgu_b, ln_g, ln_b)


def _attn_bwd(q, k, v, do, lse_row, d_row):
    t = q.shape[1]
    bk, bq = ATTN_BWD_WIDE, ATTN_NARROW
    n_diag = bk // bq
    half = bq // 2
    last = t // bq - 1
    chunk = SOFTMAX_ROWS

    def body(q_ref, k_ref, v_ref, do_ref, lse_ref, drow_ref, dqt_ref, dk_ref, dv_ref,
             s0, s1, e0, e1, p0, p1, g0, g1, kt_scr):
        j = pl.program_id(1)
        at = lambda i: pl.ds(pl.multiple_of(i * bq, bq), bq)

        @pl.when(j == 0)
        def _():
            dqt_ref[...] = jnp.zeros_like(dqt_ref)

        kt_scr[...] = jnp.transpose(k_ref[0].astype(F32)).astype(BF16)
        dk_ref[...] = jnp.zeros_like(dk_ref)
        dv_ref[...] = jnp.zeros_like(dv_ref)

        whole_tile = ((slice(0, bk), slice(0, bq)),)

        def queries(i, lanes):
            return pl.ds(pl.multiple_of(i * bq + lanes.start, half), lanes.stop - lanes.start)

        def products(i, s_out, e_out, areas=whole_tile):
            i = jnp.minimum(i, last)
            for keys, lanes in areas:
                s_out[keys, lanes] = _dot_nt(k_ref[0, keys, :], q_ref[0, queries(i, lanes), :])
                e_out[keys, lanes] = _dot_nt(v_ref[0, keys, :], do_ref[0, queries(i, lanes), :])

        def gradients(i, p_in, g_in, areas=whole_tile):
            for keys, lanes in areas:
                dv_ref[0, keys, :] += _dot(p_in[keys, lanes], do_ref[0, queries(i, lanes), :])
                dk_ref[0, keys, :] += _dot(g_in[keys, lanes], q_ref[0, queries(i, lanes), :])
                dqt_ref[0, :, queries(i, lanes)] += _dot(kt_scr[:, keys], g_in[keys, lanes])

        def elementwise(i, s_in, e_in, p_out, g_out, qry0=None, areas=whole_tile):
            for keys, lanes in areas:
                width = lanes.stop - lanes.start
                step = chunk if qry0 is None else half
                lse = lse_ref[0, :, queries(i, lanes)]
                dsum = drow_ref[0, :, queries(i, lanes)]
                for r in range(keys.start, keys.stop, step):
                    p = jnp.exp2(s_in[r:r + step, lanes] - lse)
                    if qry0 is not None:
                        key = lax.broadcasted_iota(jnp.int32, (step, width), 0) + r
                        qry = lax.broadcasted_iota(jnp.int32, (step, width), 1) + (qry0 + lanes.start)
                        p = jnp.where(qry >= key, p, 0.0)
                    p_out[r:r + step, lanes] = p.astype(BF16)
                    g_out[r:r + step, lanes] = (p * (e_in[r:r + step, lanes] - dsum)).astype(BF16)

        def one_pass(i, s_in, e_in, s_out, e_out, p_prev, g_prev, p_cur, g_cur):
            products(i + 1, s_out, e_out)
            gradients(i - 1, p_prev, g_prev)
            elementwise(i, s_in, e_in, p_cur, g_cur)

        first = n_diag * j

        def areas_of(u):
            if u >= n_diag:
                return whole_tile
            return ((slice(0, u * bq + half), slice(0, bq)), (slice(u * bq + half, (u + 1) * bq), slice(half, bq)))

        even, odd = (s0, e0, p0, g0), (s1, e1, p1, g1)
        products(first, s0, e0, areas_of(0))
        products(first + 1, s1, e1, areas_of(1))
        elementwise(first, s0, e0, p0, g0, qry0=0, areas=areas_of(0))
        for u in range(1, n_diag):
            (s_in, e_in, p_cur, g_cur), (s_out, e_out, p_prev, g_prev) = (odd, even) if u % 2 else (even, odd)
            products(first + u + 1, s_out, e_out, areas_of(u + 1))
            gradients(first + u - 1, p_prev, g_prev, areas_of(u - 1))
            elementwise(first + u, s_in, e_in, p_cur, g_cur, qry0=u * bq, areas=areas_of(u))
        corner = (slice(bk - half, bk), slice(0, half))
        p1[corner] = jnp.zeros((half, half), BF16)
        g1[corner] = jnp.zeros((half, half), BF16)

        def two_passes(n, _):
            i = first + n_diag + 2 * n
            one_pass(i, s0, e0, s1, e1, p1, g1, p0, g0)
            one_pass(i + 1, s1, e1, s0, e0, p0, g0, p1, g1)
            return 0

        lax.fori_loop(0, (last - first - n_diag + 1) // 2, two_passes, 0)
        gradients(last, p1, g1)
        dk_ref[0] = dk_ref[0] * LN2

    whole = pl.BlockSpec((1, t, HEAD_PAD), lambda h, j: (h, 0, 0))
    block = pl.BlockSpec((1, bk, HEAD_PAD), lambda h, j: (h, j, 0))
    rows = pl.BlockSpec((1, 1, t), lambda h, j: (h, 0, 0), pipeline_mode=pl.Buffered(1))
    shape = jax.ShapeDtypeStruct((HEADS, t, HEAD_PAD), F32)
    tile = lambda dtype: pltpu.VMEM((bk, bq), dtype)
    return pl.pallas_call(
        body, name="attn_bwd", grid=(HEADS, t // bk),
        in_specs=[whole, block, block, whole, rows, rows],
        out_specs=[pl.BlockSpec((1, HEAD_PAD, t), lambda h, j: (h, 0, 0)), block, block],
        out_shape=[jax.ShapeDtypeStruct((HEADS, HEAD_PAD, t), F32), shape, shape],
        scratch_shapes=[tile(F32), tile(F32), tile(F32), tile(F32), tile(BF16), tile(BF16),
                        tile(BF16), tile(BF16), pltpu.VMEM((HEAD_PAD, bk), BF16)],
        compiler_params=_cparams(("arbitrary", "arbitrary"), vmem_limit=ATTN_BWD_VMEM_LIMIT),
    )(q, k, v, do, lse_row, d_row)


def _bwd_tail(dq, dk, dv, proj, pos_col, invf_row, w_heads, q_g, kv_g, x, dr, drest, wp_in):
    t = proj.shape[0]
    tm = PROJ_TILE
    n_head = 4 * LANES

    def body(dq_ref, dk_ref, dv_ref, ph_ref, pos_ref, invf_ref, wh_ref, qg_ref, kvg_ref,
             x_ref, dr_ref, drest_ref, win_ref,
             gx_ref, dwin_ref, dwh_ref, dqg_ref, dkvg_ref):
        @pl.when(pl.program_id(0) == 0)
        def _():
            dwin_ref[...] = jnp.zeros_like(dwin_ref)
            dwh_ref[...] = jnp.zeros_like(dwh_ref)
            dqg_ref[...] = jnp.zeros_like(dqg_ref)
            dkvg_ref[...] = jnp.zeros_like(dkvg_ref)

        xb = x_ref[...].astype(BF16)
        dr_b = drest_ref[...]
        dwin_ref[:, n_head:] += _dot_tn(xb, dr_b)
        gx_rest = DN_ALPHA * dr_ref[...] + _dot_nt(dr_b, win_ref[:, n_head:])

        cos, s1, s2 = _rope_tables(pos_ref[...], invf_ref[...])
        lane = lax.broadcasted_iota(jnp.int32, (tm, LANES), 1)
        c_q = ph_ref[:, :Q_LORA]
        c_kv = ph_ref[:, Q_LORA:Q_LORA + KV_LORA]
        rstd_q = lax.rsqrt(jnp.mean(c_q * c_q, axis=-1, keepdims=True) + EPS)
        rstd_kv = lax.rsqrt(jnp.mean(c_kv * c_kv, axis=-1, keepdims=True) + EPS)
        qhat = c_q * rstd_q
        kvhat = c_kv * rstd_kv
        cqn = (qhat * qg_ref[...]).astype(BF16)
        ckvn = (kvhat * kvg_ref[...]).astype(BF16)
        dkr_rot = jnp.zeros((tm, LANES), F32)
        dq_heads, dkv_heads = [], []
        for h in range(HEADS):
            dq_heads.append(_rope(jnp.transpose(dq_ref[h]) * ATTN_SCALE, cos, s1, s2, -1.0).astype(BF16))
            dk_h = dk_ref[h]
            dkv_heads.append(jnp.where(lane < NOPE, dk_h, dv_ref[h]).astype(BF16))
            dkr_rot = dkr_rot + dk_h
        dq_all = jnp.concatenate(dq_heads, axis=1)
        dkv_all = jnp.concatenate(dkv_heads, axis=1)
        dwq_all = _dot_tn(cqn, dq_all)
        dwkv_all = _dot_tn(ckvn, dkv_all)
        for h in range(HEADS):
            dwh_ref[h, :Q_LORA, :] += dwq_all[:, h * HEAD_PAD:(h + 1) * HEAD_PAD]
            dwh_ref[h, Q_LORA:, :] += dwkv_all[:, h * HEAD_PAD:(h + 1) * HEAD_PAD]
        dcqn = _dot_nt(dq_all, jnp.concatenate([wh_ref[h, :Q_LORA, :] for h in range(HEADS)], axis=1))
        dckvn = _dot_nt(dkv_all, jnp.concatenate([wh_ref[h, Q_LORA:, :] for h in range(HEADS)], axis=1))
        rot_lanes = (lane >= KR_LO) & (lane < KR_LO + ROPE)
        dkr_raw = jnp.where(rot_lanes, _rope(dkr_rot, cos, s1, s2, -1.0), 0.0)
        dqg_ref[...] += jnp.sum(dcqn * qhat, axis=0, keepdims=True)
        dkvg_ref[...] += jnp.sum(dckvn * kvhat, axis=0, keepdims=True)
        dqh = dcqn * qg_ref[...]
        dkvh = dckvn * kvg_ref[...]
        dc_q = rstd_q * (dqh - qhat * jnp.mean(dqh * qhat, axis=-1, keepdims=True))
        dc_kv = rstd_kv * (dkvh - kvhat * jnp.mean(dkvh * kvhat, axis=-1, keepdims=True))
        dh_b = jnp.concatenate([dc_q, dc_kv, dkr_raw], axis=-1).astype(BF16)
        dwin_ref[:, :n_head] += _dot_tn(xb, dh_b)
        gx_ref[...] = gx_rest + _dot_nt(dh_b, win_ref[:, :n_head])

    full = lambda a: pl.BlockSpec(a.shape, lambda i: (0,) * a.ndim)
    tile = lambda w: pl.BlockSpec((tm, w), lambda i: (i, 0))
    heads = pl.BlockSpec((HEADS, tm, HEAD_PAD), lambda i: (0, i, 0))
    acc = lambda shape: (pl.BlockSpec(shape, lambda i: (0,) * len(shape)), jax.ShapeDtypeStruct(shape, F32))
    accs = [acc(wp_in.shape), acc(w_heads.shape), acc((1, Q_LORA)), acc((1, KV_LORA))]
    return pl.pallas_call(
        body, name="bwd_tail", grid=(t // tm,),
        in_specs=[pl.BlockSpec((HEADS, HEAD_PAD, tm), lambda i: (0, 0, i)), heads, heads, tile(n_head),
                  pl.BlockSpec((tm, 1), lambda i: (i, 0)), full(invf_row), full(w_heads), full(q_g), full(kv_g),
                  tile(D_MODEL), tile(D_MODEL), tile(drest.shape[1]), full(wp_in)],
        out_specs=[tile(D_MODEL)] + [a[0] for a in accs],
        out_shape=[jax.ShapeDtypeStruct((t, D_MODEL), F32)] + [a[1] for a in accs],
        compiler_params=_cparams(("arbitrary",), vmem_limit=BWD_TAIL_VMEM_LIMIT),
    )(dq, dk, dv, proj, pos_col, invf_row, w_heads, q_g, kv_g, x, dr, drest, wp_in)


def _adam_update(g, w, m, v):
    m_new = ADAM_B1 * m + (1.0 - ADAM_B1) * g
    v_new = ADAM_B2 * v + (1.0 - ADAM_B2) * (g * g)
    m_hat = m_new / (1.0 - ADAM_B1 ** ADAM_STEP)
    v_hat = v_new / (1.0 - ADAM_B2 ** ADAM_STEP)
    return -ADAM_LR * (m_hat / (jnp.sqrt(v_hat) + ADAM_EPS) + ADAM_WD * w), m_new, v_new


def _adam(parts, w, m, v, *, name, tile_rows, transposed=False):
    n, rows, cols = parts.shape
    lane_pad = -(-cols // LANES) * LANES
    own_rows = rows if transposed else w.shape[0]
    assert own_rows == rows or tile_rows == rows

    def body(p_ref, w_ref, m_ref, v_ref, g_ref, d_ref, nm_ref, nv_ref, *scratch):
        g = p_ref[0].astype(F32)
        for s in range(1, n):
            g = g + p_ref[s].astype(F32)
        if transposed:
            wide_ref, = scratch
            wide_ref[:, lane_pad - LANES:] = jnp.zeros((tile_rows, LANES), F32)
            wide_ref[:, :cols] = g
            g = jnp.transpose(wide_ref[...])[:cols]
        g_ref[...] = g
        d_ref[...], nm_ref[...], nv_ref[...] = _adam_update(g[:w_ref.shape[0]], w_ref[...], m_ref[...], v_ref[...])

    if transposed:
        flat = grad = pl.BlockSpec((cols, tile_rows), lambda i: (0, i))
        shape = grad_shape = jax.ShapeDtypeStruct((cols, rows), F32)
        scratch = [pltpu.VMEM((tile_rows, lane_pad), F32)]
    else:
        own_tile = min(tile_rows, own_rows)
        flat = pl.BlockSpec((own_tile, cols), lambda i: (i, 0))
        grad = pl.BlockSpec((tile_rows, cols), lambda i: (i, 0))
        shape, grad_shape = jax.ShapeDtypeStruct((own_rows, cols), F32), jax.ShapeDtypeStruct((rows, cols), F32)
        scratch = []
    return pl.pallas_call(
        body, name=name, grid=(rows // tile_rows,),
        in_specs=[pl.BlockSpec((n, tile_rows, cols), lambda i: (0, i, 0)), flat, flat, flat],
        out_specs=[grad, flat, flat, flat], out_shape=[grad_shape, shape, shape, shape], scratch_shapes=scratch,
        compiler_params=_cparams(("arbitrary",)),
    )(parts, w, m, v)


def _adam_replicated(rep_g, ws, ms, vs):
    count = len(ws)
    small_rows = REP_ROWS - CHUNK

    def body(g_ref, *refs):
        w_refs, m_refs, v_refs = refs[:count], refs[count:2 * count], refs[2 * count:3 * count]
        outs, last_ref, slab_ref = refs[3 * count:7 * count], refs[7 * count], refs[7 * count + 1]
        for d in range(N_DEV):
            slab_ref[d * small_rows:(d + 1) * small_rows, :] = g_ref[d, CHUNK:, :]
        last_ref[...] = slab_ref[N_DEV * small_rows - 1:, LANES - 1:]
        at = 0
        for k, w_ref in enumerate(w_refs):
            if w_ref.ndim == 3:
                g = g_ref[:, :CHUNK, :]
            else:
                n_rows = w_ref.size // LANES
                g = slab_ref[at:at + n_rows, :].reshape(w_ref.shape)
                at += n_rows
            delta, m_new, v_new = _adam_update(g, w_ref[...], m_refs[k][...], v_refs[k][...])
            for which, val in enumerate((g, delta, m_new, v_new)):
                outs[which * count + k][...] = val

    shapes = [jax.ShapeDtypeStruct(w.shape, F32) for w in ws]
    res = pl.pallas_call(
        body, name="adam_rep", out_shape=shapes * 4 + [jax.ShapeDtypeStruct((1, 1), F32)],
        scratch_shapes=[pltpu.VMEM((N_DEV * small_rows, LANES), F32)],
        compiler_params=_cparams(),
    )(rep_g, *ws, *ms, *vs)
    return [res[which * count:(which + 1) * count] for which in range(4)], res[4 * count]


def _pack_small(vals, last):
    flat = jnp.concatenate([v.reshape(-1) for v in vals])
    pad = SMALL_LEN - flat.shape[0]
    return jnp.concatenate([flat, jnp.zeros((pad - 1,), F32), last.reshape(1)])


UQ_SHARD = HEADS * (NOPE + ROPE) // N_DEV
HEAD_ROWS = Q_LORA + KV_LORA
MIXED_ROWS = HEAD_ROWS + CHUNK + SMALL_LEN // N_DEV // LANES


def _head_slab(w_uq_shard, w_ukv_shard):
    return jnp.concatenate([jnp.pad(w_uq_shard, ((0, 0), (0, LANES - UQ_SHARD))), w_ukv_shard])


IN_SHARD = D_IN // N_DEV


def _w_in_pieces():
    split = Q_LORA + KV_LORA
    moves = ((0, split, 0), (split, split + ROPE, KR_LO), (split + ROPE, D_IN, LANES - ROPE))
    pieces = []
    for s in range(N_DEV):
        lo, hi = s * IN_SHARD, (s + 1) * IN_SHARD
        for a, b, shift in moves:
            a, b = max(a, lo), min(b, hi)
            if a < b:
                pieces.append((s, a - lo, a + shift, b - a))
    return pieces


def _w_in_shards(dwp_in):
    tr = TOKEN_TILE
    by_shard = [[p for p in _w_in_pieces() if p[0] == s] for s in range(N_DEV)]

    def body(w_ref, o_ref):
        for s, pieces in enumerate(by_shard):
            parts = [w_ref[:, dst:dst + width] for _, _, dst, width in pieces]
            o_ref[s] = parts[0] if len(parts) == 1 else jnp.concatenate(parts, axis=1)

    return pl.pallas_call(
        body, name="w_in_split", grid=(D_MODEL // tr,),
        in_specs=[pl.BlockSpec((tr, D_IN_PAD), lambda i: (i, 0))],
        out_specs=pl.BlockSpec((N_DEV, tr, IN_SHARD), lambda i: (0, i, 0)),
        out_shape=jax.ShapeDtypeStruct((N_DEV, D_MODEL, IN_SHARD), dwp_in.dtype),
        compiler_params=_cparams(("arbitrary",)),
    )(dwp_in)


def kernel(x, positions, w_in, q_norm_g, w_uq, kv_norm_g, w_ukv, sgu_norm_g, sgu_norm_b, w_spatial, b_spatial, w_out, ln_g, ln_b, loss_target, m_w_in, m_q_norm_g, m_w_uq, m_kv_norm_g, m_w_ukv, m_sgu_norm_g, m_sgu_norm_b, m_w_spatial, m_b_spatial, m_w_out, m_ln_g, m_ln_b, v_w_in, v_q_norm_g, v_w_uq, v_kv_norm_g, v_w_ukv, v_sgu_norm_g, v_sgu_norm_b, v_w_spatial, v_b_spatial, v_w_out, v_ln_g, v_ln_b):
    seq = x.shape[1]
    x2 = x.reshape(seq, D_MODEL)
    tgt2 = loss_target.reshape(seq, D_MODEL)
    pos_col = positions.reshape(seq, 1)

    w_in_shards, w_out_shards, w_heads = _gather_two_level(
        [w_in.astype(BF16), w_out.astype(BF16), _head_slab(w_uq, w_ukv).astype(BF16)],
        name="wgather")
    (loss_part, grad_x, d_in, d_heads, d_out, d_ws, d_bs_t, d_lng, d_lnb, d_sgug, d_sgub, d_qg, d_kvg) = _local_step(
        x2, tgt2, pos_col, w_in_shards, w_heads, w_out_shards.reshape(D_MODEL, D_MODEL), q_norm_g, kv_norm_g,
        sgu_norm_g, sgu_norm_b, w_spatial, b_spatial, ln_g, ln_b)

    small_part = _pack_small([d_qg, d_kvg, d_sgug, d_sgub, d_bs_t[:, :HEADS].T, d_lng, d_lnb], last=loss_part[0, :1])
    mixed = jnp.concatenate([d_heads, d_ws, small_part.reshape(N_DEV, -1, LANES)], axis=1)
    by_chip = [g.reshape((N_CHIPS, 2) + g.shape[1:])
               for g in (d_in, d_out.reshape(N_DEV, D_MODEL // N_DEV, D_MODEL), mixed)]
    from_sibling = _sibling_swap(by_chip, name="gswap")
    core = lax.axis_index("c").astype(jnp.int32).reshape(1)
    pair_sums = [_pair_sum(a, b, core, name=nm, out_dtype=dt) for a, b, nm, dt in zip(
        by_chip, from_sibling, ("gsum_in", "gsum_out", "gsum_mixed"), (BF16, BF16, F32))]
    recv_in, recv_out, recv_mixed = _chip_exchange(pair_sums, name="gexch")

    res_in = [a.T for a in _adam(recv_in, w_in.T, m_w_in.T, v_w_in.T, name="adam_in", tile_rows=PROJ_TILE,
                                 transposed=True)]
    res_out = _adam(recv_out, w_out, m_w_out, v_w_out, name="adam_out", tile_rows=D_MODEL // N_DEV)
    res_mixed = _adam(recv_mixed, _head_slab(w_uq, w_ukv), _head_slab(m_w_uq, m_w_ukv), _head_slab(v_w_uq, v_w_ukv),
                      name="adam_mixed", tile_rows=MIXED_ROWS)

    rep_g, = _gather_direct([res_mixed[0]], name="sgather", first_row=HEAD_ROWS)
    res_rep, loss = _adam_replicated(
        rep_g,
        [q_norm_g, kv_norm_g, sgu_norm_g, sgu_norm_b, w_spatial, b_spatial, ln_g, ln_b],
        [m_q_norm_g, m_kv_norm_g, m_sgu_norm_g, m_sgu_norm_b, m_w_spatial, m_b_spatial, m_ln_g, m_ln_b],
        [v_q_norm_g, v_kv_norm_g, v_sgu_norm_g, v_sgu_norm_b, v_w_spatial, v_b_spatial, v_ln_g, v_ln_b])

    def ordered(which):
        r_qg, r_kvg, r_sg, r_sb, r_ws, r_bs, r_lg, r_lb = res_rep[which]
        heads = res_mixed[which]
        return [res_in[which], r_qg, heads[:Q_LORA, :UQ_SHARD], r_kvg, heads[Q_LORA:HEAD_ROWS], r_sg, r_sb, r_ws, r_bs,
                res_out[which], r_lg, r_lb]

    outs = [loss.reshape(()), grad_x.reshape(x.shape)]
    for which in range(4):
        outs += ordered(which)
    return tuple(outs)


def _local_step(x2, tgt2, pos_col, w_in_shards, w_heads, w_out_full, q_norm_g, kv_norm_g, sgu_norm_g, sgu_norm_b,
                w_spatial, b_spatial, ln_g, ln_b):
    half = jnp.arange(HALF, dtype=F32)
    inv_freq = 1.0 / (ROPE_THETA ** (half / HALF))
    invf_row = jnp.concatenate([jnp.zeros((KR_LO,), F32), inv_freq, inv_freq,
                                jnp.zeros((LANES - KR_LO - ROPE,), F32)]).reshape(1, LANES)
    tri = jnp.tril(jnp.ones((CHUNK, CHUNK), dtype=bool))
    ws_low = jnp.where(tri[None], w_spatial, 0.0).astype(BF16)
    ws_low_t = ws_low.transpose(0, 2, 1)
    bsp = jnp.repeat(b_spatial.T, G_HEAD_DIM, axis=1)
    row = lambda a: a.reshape(1, -1)

    proj, q, k, v, vt, wp_in = _fwd_proj(x2, pos_col, invf_row, w_in_shards, w_heads, row(q_norm_g), row(kv_norm_g))
    o, lse_row = _attn_fwd(q, k, vt)
    (dr, do, d_row, drest, d_out, d_ws, d_bs_t, d_lng, d_lnb, d_sgug, d_sgub, loss_part) = _mid(
        x2, tgt2, proj, o, w_out_full, ws_low, ws_low_t, bsp, row(sgu_norm_g), row(sgu_norm_b), row(ln_g), row(ln_b))
    dqt, dk, dv = _attn_bwd(q, k, v, do, lse_row, d_row)
    grad_x, dwp_in, d_heads, d_qg, d_kvg = _bwd_tail(dqt, dk, dv, proj, pos_col, invf_row, w_heads, row(q_norm_g),
                                                      row(kv_norm_g), x2, dr, drest, wp_in)
    return (loss_part, grad_x, _w_in_shards(dwp_in), d_heads, d_out, d_ws, d_bs_t, d_lng, d_lnb, d_sgug, d_sgub,
            d_qg, d_kvg)
```

```python
import math

import jax
import jax.numpy as jnp
from jax import lax
from jax.experimental import pallas as pl
from jax.experimental.pallas import tpu as pltpu

F32 = jnp.float32
BF16 = jnp.bfloat16

N_DEV = 8
D_MODEL = 1024
HEADS = 8
NOPE = 64
ROPE = 32
HALF = ROPE // 2
VDIM = 64
Q_LORA = 256
KV_LORA = 128
G_WIDTH = 512
G_HEAD_DIM = 64
CHUNK = 128
HEAD_PAD = 128
D_IN = 2464
D_IN_PAD = 2560
KR_LO = NOPE
SUM_ROW = NOPE - 1
LIVE_ROWS = slice(NOPE - 16, HEAD_PAD)
ROPE_THETA = 10000.0
DN_ALPHA = 2.0 ** 0.25
EPS = 1e-5
ATTN_SCALE = 1.0 / math.sqrt(NOPE + ROPE)
ADAM_LR, ADAM_B1, ADAM_B2, ADAM_EPS, ADAM_WD, ADAM_STEP = 0.001, 0.9, 0.999, 1e-08, 0.01, 10

LANES = 128
REP_ROWS = 136
SMALL_LEN = 8192
VMEM_LIMIT = 56 * 1024 * 1024
ATTN_BWD_VMEM_LIMIT = 61 * 1024 * 1024
BWD_TAIL_VMEM_LIMIT = 61 * 1024 * 1024

TOKEN_TILE = 256
PROJ_TILE = 512
ATTN_FWD_WIDE = 2048
ATTN_BWD_WIDE = 2048
ATTN_NARROW = 512
SOFTMAX_ROWS = 512
LOG2E = 1.4426950408889634
LN2 = 0.6931471805599453
Q_PRESCALE = ATTN_SCALE * LOG2E


def _cparams(sem=None, vmem_limit=VMEM_LIMIT):
    return pltpu.CompilerParams(dimension_semantics=sem, vmem_limit_bytes=vmem_limit)


def _dot(a, b):
    return jnp.dot(a, b, preferred_element_type=F32)


def _dot_nt(a, b):
    return lax.dot_general(a, b, (((1,), (1,)), ((), ())), preferred_element_type=F32)


def _dot_tn(a, b):
    return lax.dot_general(a, b, (((0,), (0,)), ((), ())), preferred_element_type=F32)


def _sigmoid(z):
    return 1.0 / (1.0 + jnp.exp(-z))


def _gelu(x):
    return 0.5 * x * (1.0 + lax.erf(x * 0.7071067811865476))


def _gelu_grad(x):
    cdf = 0.5 * (1.0 + lax.erf(x * 0.7071067811865476))
    return cdf + x * jnp.exp(-0.5 * x * x) * 0.3989422804014327


def _gather_direct(srcs, *, name, first_row=0):
    n = len(srcs)
    shapes = [(s.shape[0] - first_row,) + s.shape[1:] for s in srcs]

    def body(*refs):
        src_refs, out_refs = [r.at[pl.ds(first_row, shape[0])] for r, shape in zip(refs[:n], shapes)], refs[n:2 * n]
        send_sems, recv_sems, local_sems = refs[2 * n:]
        x, y, c = lax.axis_index("x"), lax.axis_index("y"), lax.axis_index("c")
        me = 4 * x + 2 * y + c
        mine = [pltpu.make_async_copy(src_refs[t], out_refs[t].at[me], local_sems.at[t]) for t in range(n)]
        for cp in mine:
            cp.start()
        sends, arrivals = [], []
        for k in (6, 7, 4, 5, 2, 3, 1):
            px = 1 - x if k & 4 else x
            py = 1 - y if k & 2 else y
            pc = 1 - c if k & 1 else c
            peer = 4 * px + 2 * py + pc
            for t in range(n):
                sem = (k - 1) * n + t
                cp = pltpu.make_async_remote_copy(
                    src_ref=src_refs[t], dst_ref=out_refs[t].at[me],
                    send_sem=send_sems.at[sem], recv_sem=recv_sems.at[sem],
                    device_id=(px, py, pc), device_id_type=pl.DeviceIdType.MESH)
                cp.start()
                sends.append(cp)
                arrivals.append(pltpu.make_async_remote_copy(
                    src_ref=src_refs[t], dst_ref=out_refs[t].at[peer],
                    send_sem=send_sems.at[sem], recv_sem=recv_sems.at[sem],
                    device_id=(x, y, c), device_id_type=pl.DeviceIdType.MESH))
        for cp in arrivals:
            cp.wait_recv()
        for cp in sends:
            cp.wait_send()
        for cp in mine:
            cp.wait()

    hbm = pl.BlockSpec(memory_space=pl.ANY)
    return pl.pallas_call(
        body, name=name,
        out_shape=[jax.ShapeDtypeStruct((N_DEV,) + shape, s.dtype) for shape, s in zip(shapes, srcs)],
        in_specs=[hbm] * n, out_specs=[hbm] * n,
        scratch_shapes=[pltpu.SemaphoreType.DMA(((N_DEV - 1) * n,)), pltpu.SemaphoreType.DMA(((N_DEV - 1) * n,)),
                        pltpu.SemaphoreType.DMA((n,))],
    )(*srcs)


def _gather_two_level(srcs, *, name):
    n = len(srcs)

    def body(*refs):
        src_refs, out_refs = refs[:n], refs[n:2 * n]
        send_sems, recv_sems, local_sems = refs[2 * n:]
        x, y, c = lax.axis_index("x"), lax.axis_index("y"), lax.axis_index("c")
        me, sibling = (x, y, c), (x, y, 1 - c)
        chips = [(1 - x, 1 - y), (1 - x, y), (x, 1 - y)]
        index = lambda px, py, pc: 4 * px + 2 * py + pc

        def copy(k, t, block, to, src=None):
            place = out_refs[t].at[index(*block)]
            return pltpu.make_async_remote_copy(
                src_ref=place if src is None else src, dst_ref=place,
                send_sem=send_sems.at[k * n + t], recv_sem=recv_sems.at[k * n + t],
                device_id=to, device_id_type=pl.DeviceIdType.MESH)

        mine = [pltpu.make_async_copy(src_refs[t], out_refs[t].at[index(*me)], local_sems.at[t]) for t in range(n)]
        for cp in mine:
            cp.start()
        first = [copy(1 + j, t, me, (*chip, c), src=src_refs[t]) for j, chip in enumerate(chips) for t in range(n)]
        first += [copy(0, t, me, sibling, src=src_refs[t]) for t in range(n)]
        for cp in first:
            cp.start()
        passed = []
        for j, chip in enumerate(chips):
            for t in range(n):
                copy(1 + j, t, (*chip, c), me).wait_recv()
                cp = copy(4 + j, t, (*chip, c), sibling)
                cp.start()
                passed.append(cp)
        for t in range(n):
            copy(0, t, sibling, me).wait_recv()
        for j, chip in enumerate(chips):
            for t in range(n):
                copy(4 + j, t, (*chip, 1 - c), me).wait_recv()
        for cp in first + passed:
            cp.wait_send()
        for cp in mine:
            cp.wait()

    hbm = pl.BlockSpec(memory_space=pl.ANY)
    return pl.pallas_call(
        body, name=name,
        out_shape=[jax.ShapeDtypeStruct((N_DEV,) + s.shape, s.dtype) for s in srcs],
        in_specs=[hbm] * n, out_specs=[hbm] * n,
        scratch_shapes=[pltpu.SemaphoreType.DMA((7 * n,)), pltpu.SemaphoreType.DMA((7 * n,)),
                        pltpu.SemaphoreType.DMA((n,))],
    )(*srcs)


N_CHIPS = N_DEV // 2


def _sibling_swap(srcs, *, name):
    n = len(srcs)

    def body(*refs):
        src_refs, out_refs = refs[:n], refs[n:2 * n]
        send_sems, recv_sems = refs[2 * n:]
        x, y, c = lax.axis_index("x"), lax.axis_index("y"), lax.axis_index("c")
        sends = []
        for chip in range(N_CHIPS):
            for t in range(n):
                cp = pltpu.make_async_remote_copy(
                    src_ref=src_refs[t].at[chip, 1 - c], dst_ref=out_refs[t].at[chip],
                    send_sem=send_sems.at[chip * n + t], recv_sem=recv_sems.at[chip * n + t],
                    device_id=(x, y, 1 - c), device_id_type=pl.DeviceIdType.MESH)
                cp.start()
                sends.append(cp)
        for cp in sends:
            cp.wait_recv()
        for cp in sends:
            cp.wait_send()

    hbm = pl.BlockSpec(memory_space=pl.ANY)
    return pl.pallas_call(
        body, name=name,
        out_shape=[jax.ShapeDtypeStruct((N_CHIPS,) + s.shape[2:], s.dtype) for s in srcs],
        in_specs=[hbm] * n, out_specs=[hbm] * n,
        scratch_shapes=[pltpu.SemaphoreType.DMA((N_CHIPS * n,)), pltpu.SemaphoreType.DMA((N_CHIPS * n,))],
    )(*srcs)


def _pair_sum(mine, theirs, core, *, name, out_dtype):
    _, _, rows, cols = mine.shape

    def body(core_ref, a_ref, b_ref, o_ref):
        o_ref[...] = (a_ref[0] + b_ref[...]).astype(out_dtype)

    return pl.pallas_call(
        body, name=name,
        grid_spec=pltpu.PrefetchScalarGridSpec(
            num_scalar_prefetch=1, grid=(N_CHIPS,),
            in_specs=[pl.BlockSpec((1, 1, rows, cols), lambda q, core_ref: (q, core_ref[0], 0, 0)),
                      pl.BlockSpec((1, rows, cols), lambda q, core_ref: (q, 0, 0))],
            out_specs=pl.BlockSpec((1, rows, cols), lambda q, core_ref: (q, 0, 0))),
        out_shape=jax.ShapeDtypeStruct((N_CHIPS, rows, cols), out_dtype),
        compiler_params=_cparams(("arbitrary",)),
    )(core, mine, theirs)


def _chip_exchange(srcs, *, name):
    n = len(srcs)

    def body(*refs):
        src_refs, out_refs = refs[:n], refs[n:2 * n]
        send_sems, recv_sems, local_sems = refs[2 * n:]
        x, y, c = lax.axis_index("x"), lax.axis_index("y"), lax.axis_index("c")
        my_chip = 2 * x + y
        mine = [pltpu.make_async_copy(src_refs[t].at[my_chip], out_refs[t].at[my_chip], local_sems.at[t])
                for t in range(n)]
        for cp in mine:
            cp.start()
        sends, arrivals = [], []
        for k in (3, 2, 1):
            px = 1 - x if k & 2 else x
            py = 1 - y if k & 1 else y
            peer_chip = 2 * px + py
            for t in range(n):
                sem = (k - 1) * n + t
                cp = pltpu.make_async_remote_copy(
                    src_ref=src_refs[t].at[peer_chip], dst_ref=out_refs[t].at[my_chip],
                    send_sem=send_sems.at[sem], recv_sem=recv_sems.at[sem],
                    device_id=(px, py, c), device_id_type=pl.DeviceIdType.MESH)
                cp.start()
                sends.append(cp)
                arrivals.append(pltpu.make_async_remote_copy(
                    src_ref=src_refs[t].at[peer_chip], dst_ref=out_refs[t].at[peer_chip],
                    send_sem=send_sems.at[sem], recv_sem=recv_sems.at[sem],
                    device_id=(x, y, c), device_id_type=pl.DeviceIdType.MESH))
        for cp in arrivals:
            cp.wait_recv()
        for cp in sends:
            cp.wait_send()
        for cp in mine:
            cp.wait()

    hbm = pl.BlockSpec(memory_space=pl.ANY)
    return pl.pallas_call(
        body, name=name,
        out_shape=[jax.ShapeDtypeStruct(s.shape, s.dtype) for s in srcs],
        in_specs=[hbm] * n, out_specs=[hbm] * n,
        scratch_shapes=[pltpu.SemaphoreType.DMA((3 * n,)), pltpu.SemaphoreType.DMA((3 * n,)),
                        pltpu.SemaphoreType.DMA((n,))],
    )(*srcs)


def _rope_tables(pos_row, invf_col):
    tm = pos_row.shape[1]
    ang = pos_row.astype(F32) * invf_col
    cos, sin = jnp.cos(ang), jnp.sin(ang)
    ones = lambda n: jnp.ones((n, tm), F32)
    zeros = lambda n: jnp.zeros((n, tm), F32)
    cos_t = jnp.concatenate([ones(KR_LO), cos, cos, ones(LANES - KR_LO - ROPE)], axis=0)
    first_t = jnp.concatenate([zeros(KR_LO), sin, zeros(LANES - KR_LO - HALF)], axis=0)
    second_t = jnp.concatenate([zeros(KR_LO + HALF), sin, zeros(LANES - KR_LO - ROPE)], axis=0)
    return jnp.transpose(cos_t), jnp.transpose(first_t), jnp.transpose(second_t)


def _rope(t, cos, sin_first, sin_second, sign):
    up = pltpu.roll(t, LANES - HALF, 1)
    down = pltpu.roll(t, HALF, 1)
    return t * cos - sign * (up * sin_first) + sign * (down * sin_second)


def _fwd_proj(x, pos_row, invf_col, w_in_shards, w_heads, q_g, kv_g):
    t = x.shape[0]
    tm = PROJ_TILE

    def body(x_ref, pos_ref, invf_ref, sh_ref, wh_ref, qg_ref, kvg_ref,
             proj_ref, q_ref, k_ref, v_ref, vt_ref, win_ref):
        @pl.when(pl.program_id(0) == 0)
        def _():
            win_ref[...] = jnp.zeros_like(win_ref)
            for s, src, dst, width in _w_in_pieces():
                win_ref[:, dst:dst + width] = sh_ref[s, :, src:src + width]

        proj = _dot(x_ref[...].astype(BF16), win_ref[...])
        proj_ref[...] = proj
        c_q = proj[:, :Q_LORA]
        c_kv = proj[:, Q_LORA:Q_LORA + KV_LORA]
        kr_raw = proj[:, Q_LORA + KV_LORA:Q_LORA + KV_LORA + LANES]
        cqn = (c_q * lax.rsqrt(jnp.mean(c_q * c_q, axis=-1, keepdims=True) + EPS) * qg_ref[...]).astype(BF16)
        ckvn = (c_kv * lax.rsqrt(jnp.mean(c_kv * c_kv, axis=-1, keepdims=True) + EPS) * kvg_ref[...]).astype(BF16)
        cos, s1, s2 = _rope_tables(pos_ref[...], invf_ref[...])
        kr = _rope(kr_raw, cos, s1, s2, 1.0)
        lane = lax.broadcasted_iota(jnp.int32, (tm, HEAD_PAD), 1)
        q_all = _dot(cqn, jnp.concatenate([wh_ref[h, :Q_LORA, :] for h in range(HEADS)], axis=1))
        kv_all = _dot(ckvn, jnp.concatenate([wh_ref[h, Q_LORA:, :] for h in range(HEADS)], axis=1))
        for h in range(HEADS):
            q_h = q_all[:, h * HEAD_PAD:(h + 1) * HEAD_PAD]
            kv_h = kv_all[:, h * HEAD_PAD:(h + 1) * HEAD_PAD]
            q_ref[h] = (_rope(q_h, cos, s1, s2, 1.0) * Q_PRESCALE).astype(BF16)
            k_ref[h] = jnp.where(lane < NOPE, kv_h, kr).astype(BF16)
            v_ref[h] = kv_h.astype(BF16)
            vt_ref[h] = jnp.transpose(jnp.where(lane == SUM_ROW, 1.0, kv_h)).astype(BF16)

    full = lambda a: pl.BlockSpec(a.shape, lambda i: (0,) * a.ndim)
    head_spec = pl.BlockSpec((HEADS, tm, HEAD_PAD), lambda i: (0, i, 0))
    head_shape = jax.ShapeDtypeStruct((HEADS, t, HEAD_PAD), BF16)
    return pl.pallas_call(
        body, name="fwd_proj", grid=(t // tm,),
        in_specs=[pl.BlockSpec((tm, D_MODEL), lambda i: (i, 0)), pl.BlockSpec((1, tm), lambda i: (0, i)),
                  full(invf_col), full(w_in_shards), full(w_heads), full(q_g), full(kv_g)],
        out_specs=[pl.BlockSpec((tm, D_IN_PAD), lambda i: (i, 0)), head_spec, head_spec, head_spec,
                   pl.BlockSpec((HEADS, HEAD_PAD, tm), lambda i: (0, 0, i)),
                   pl.BlockSpec((D_MODEL, D_IN_PAD), lambda i: (0, 0))],
        out_shape=[jax.ShapeDtypeStruct((t, D_IN_PAD), F32), head_shape, head_shape, head_shape,
                   jax.ShapeDtypeStruct((HEADS, HEAD_PAD, t), BF16),
                   jax.ShapeDtypeStruct((D_MODEL, D_IN_PAD), w_in_shards.dtype)],
        compiler_params=_cparams(("arbitrary",)),
    )(x, pos_row, invf_col, w_in_shards, w_heads, q_g, kv_g)


def _attn_fwd(q, k, vt):
    t = q.shape[1]
    bq, bk = ATTN_FWD_WIDE, ATTN_NARROW
    n_diag = bq // bk
    chunk = SOFTMAX_ROWS

    def body(q_ref, k_ref, vt_ref, o_ref, lse_ref, s0, s1, p0, p1, x0, x1, m_scr, a_scr, acc_scr):
        i = pl.program_id(1)
        at = lambda j: pl.ds(pl.multiple_of(j * bk, bk), bk)

        def exp_pass(s_in, block_max, p_out, diagonal=False, cols=slice(None)):
            width = bq if cols == slice(None) else cols.stop - cols.start

            def load(r):
                s = s_in[r:r + chunk, cols]
                if diagonal:
                    key = lax.broadcasted_iota(jnp.int32, (chunk, width), 0) + r
                    qry = lax.broadcasted_iota(jnp.int32, (chunk, width), 1)
                    s = jnp.where(qry >= key, s, -jnp.inf)
                return s

            if diagonal:
                block_max = jnp.max(load(0), axis=0, keepdims=True)
                for r in range(chunk, bk, chunk):
                    block_max = jnp.maximum(block_max, jnp.max(load(r), axis=0, keepdims=True))
            m_old = m_scr[:, cols]
            m_new = jnp.maximum(m_old, block_max)
            alpha = jnp.exp2(m_old - m_new)
            for r in range(0, bk, chunk):
                p_out[r:r + chunk, cols] = jnp.exp2(load(r) - m_new).astype(BF16)
            m_scr[:, cols] = m_new
            return alpha

        def scores(j, s_out, x_out):
            s = _dot_nt(k_ref[0, at(j), :], q_ref[0])
            s_out[...] = s
            x_out[...] = jnp.max(s, axis=0, keepdims=True)

        def value_product(j, p_in):
            return _dot(vt_ref[0, LIVE_ROWS, at(j)], p_in[...])

        def one_pass(j, s_in, x_in, s_out, x_out, p_prev, p_cur):
            scores(j + 1, s_out, x_out)
            acc_scr[...] = a_scr[...] * acc_scr[...] + value_product(jnp.maximum(j - 1, 0), p_prev)
            a_scr[...] = exp_pass(s_in, x_in[...], p_cur)

        scores(0, s0, x0)
        p1[...] = jnp.zeros_like(p1)
        a_scr[...] = jnp.ones_like(a_scr)
        m_scr[...] = jnp.full(m_scr.shape, -jnp.inf, F32)
        acc_scr[...] = jnp.zeros_like(acc_scr)

        def two_passes(n, _):
            one_pass(2 * n, s0, x0, s1, x1, p1, p0)
            one_pass(2 * n + 1, s1, x1, s0, x0, p0, p1)
            return 0

        lax.fori_loop(0, (n_diag // 2) * i, two_passes, 0)
        d = n_diag * i
        alpha, p_prev, cols = a_scr[...], p1, slice(0, bq)
        for u in range(n_diag + 1):
            s_in, s_next, p_cur = (s0, s1, p0) if u % 2 == 0 else (s1, s0, p1)
            if u + 1 < n_diag:
                ahead = slice((u + 1) * bk, bq)
                s_next[:, ahead] = _dot_nt(k_ref[0, at(d + u + 1), :], q_ref[0, ahead, :])
            acc_scr[:, cols] = alpha * acc_scr[:, cols] + _dot(vt_ref[0, LIVE_ROWS, at(jnp.maximum(d + u - 1, 0))],
                                                               p_prev[:, cols])
            if u < n_diag:
                cols = slice(u * bk, bq)
                alpha = exp_pass(s_in, None, p_cur, diagonal=True, cols=cols)
                p_prev = p_cur
        denom = acc_scr[SUM_ROW - LIVE_ROWS.start:NOPE - LIVE_ROWS.start, :]
        o = jnp.transpose(acc_scr[NOPE - LIVE_ROWS.start:, :] / denom)
        o_ref[0] = jnp.concatenate([jnp.zeros_like(o), o], axis=1)
        lse_ref[0] = m_scr[...] + jnp.log2(denom)

    tile = lambda dtype: pltpu.VMEM((bk, bq), dtype)
    stat = pltpu.VMEM((1, bq), F32)
    return pl.pallas_call(
        body, name="attn_fwd", grid=(HEADS, t // bq),
        in_specs=[pl.BlockSpec((1, bq, HEAD_PAD), lambda h, i: (h, i, 0)),
                  pl.BlockSpec((1, t, HEAD_PAD), lambda h, i: (h, 0, 0)),
                  pl.BlockSpec((1, HEAD_PAD, t), lambda h, i: (h, 0, 0))],
        out_specs=[pl.BlockSpec((1, bq, HEAD_PAD), lambda h, i: (h, i, 0)),
                   pl.BlockSpec((1, 1, bq), lambda h, i: (h, 0, i))],
        out_shape=[jax.ShapeDtypeStruct((HEADS, t, HEAD_PAD), F32), jax.ShapeDtypeStruct((HEADS, 1, t), F32)],
        scratch_shapes=[tile(F32), tile(F32), tile(BF16), tile(BF16), stat, stat, stat, stat,
                        pltpu.VMEM((HEAD_PAD - LIVE_ROWS.start, bq), F32)],
        compiler_params=_cparams(("arbitrary", "arbitrary")),
    )(q, k, vt)


def _mid(x, target, proj, ol, w_out, ws_low, ws_low_t, bsp, sgu_g, sgu_b, ln_g, ln_b):
    t = x.shape[0]
    tm = TOKEN_TILE
    n_steps = t // tm

    def body(x_ref, tgt_ref, za_ref, u_ref, v_ref, zb_ref, ol_ref, wout_ref, ws_ref, wst_ref, bsp_ref,
             sg_ref, sb_ref, lg_ref, lb_ref,
             dr_ref, do_ref, drow_ref, drest_ref, dwout_ref, dws_ref, dbs_ref, dlg_ref, dlb_ref, dsg_ref, dsb_ref,
             loss_ref, dbsp_acc):
        step = pl.program_id(0)

        @pl.when(step == 0)
        def _():
            dwout_ref[...] = jnp.zeros_like(dwout_ref)
            dws_ref[...] = jnp.zeros_like(dws_ref)
            dbs_ref[...] = jnp.zeros_like(dbs_ref)
            dlg_ref[...] = jnp.zeros_like(dlg_ref)
            dlb_ref[...] = jnp.zeros_like(dlb_ref)
            dsg_ref[...] = jnp.zeros_like(dsg_ref)
            dsb_ref[...] = jnp.zeros_like(dsb_ref)
            loss_ref[...] = jnp.zeros_like(loss_ref)
            dbsp_acc[...] = jnp.zeros_like(dbsp_acc)

        n_chunks = tm // CHUNK
        groups = G_WIDTH // LANES

        def side_by_side(a):
            return [jnp.concatenate([a[c * CHUNK:(c + 1) * CHUNK, g * LANES:(g + 1) * LANES] for c in range(n_chunks)],
                                    axis=1) for g in range(groups)]

        def by_chunk(wide):
            return jnp.concatenate([jnp.concatenate([wide[g][:, c * LANES:(c + 1) * LANES] for g in range(groups)], axis=1)
                                    for c in range(n_chunks)], axis=0)

        def own_lanes(h):
            lane = lax.broadcasted_iota(jnp.int32, (CHUNK, n_chunks * LANES), 1)
            return (lane % LANES) // G_HEAD_DIM == h % 2

        def spatial(w_ref, wide):
            return [sum(jnp.where(own_lanes(h), _dot(w_ref[h], wide[g]), 0.0) for h in (2 * g, 2 * g + 1))
                    for g in range(groups)]

        attn = jnp.concatenate([ol_ref[h][:, NOPE:] for h in range(HEADS)], axis=-1)
        za = za_ref[...]
        sig_a = _sigmoid(za)
        silu_a = za * sig_a
        out_a = attn * silu_a
        u = u_ref[...]
        ug = _gelu(u)
        vpre = v_ref[...]
        gv = _gelu(vpre)
        mu_v = jnp.mean(gv, axis=-1, keepdims=True)
        cen_v = gv - mu_v
        rstd_v = lax.rsqrt(jnp.mean(cen_v * cen_v, axis=-1, keepdims=True) + EPS)
        vhat = cen_v * rstd_v
        vg = vhat * sg_ref[...] + sb_ref[...]
        vg_b = vg.astype(BF16)
        sv = by_chunk(spatial(ws_ref, side_by_side(vg_b))) + jnp.tile(bsp_ref[...], (n_chunks, 1))
        sgu = ug * sv
        zb = zb_ref[...]
        sig_b = _sigmoid(zb)
        silu_b = zb * sig_b
        out_b = sgu * silu_b
        merged = jnp.concatenate([out_a, out_b], axis=-1).astype(BF16)
        r = DN_ALPHA * x_ref[...] + _dot(merged, wout_ref[...])
        mu = jnp.mean(r, axis=-1, keepdims=True)
        cen = r - mu
        rstd = lax.rsqrt(jnp.mean(cen * cen, axis=-1, keepdims=True) + EPS)
        xhat = cen * rstd
        hout = xhat * lg_ref[...] + lb_ref[...]
        err = hout - tgt_ref[...]
        row_loss = jnp.mean(err * err, axis=-1, keepdims=True)
        loss_ref[...] += jnp.broadcast_to(0.5 * jnp.sum(row_loss, axis=0, keepdims=True), loss_ref.shape)

        dh = err * (1.0 / D_MODEL)
        dlg_ref[...] += jnp.sum(dh * xhat, axis=0, keepdims=True)
        dlb_ref[...] += jnp.sum(dh, axis=0, keepdims=True)
        dxhat = dh * lg_ref[...]
        dr = rstd * (dxhat - jnp.mean(dxhat, axis=-1, keepdims=True)
                     - xhat * jnp.mean(dxhat * xhat, axis=-1, keepdims=True))
        dr_ref[...] = dr
        dr_b = dr.astype(BF16)
        dwout_ref[...] += _dot_tn(merged, dr_b)
        dmerged = _dot_nt(dr_b, wout_ref[...])
        d_out_a = dmerged[:, :G_WIDTH]
        d_out_b = dmerged[:, G_WIDTH:]
        dattn = d_out_a * silu_a
        for h in range(HEADS):
            do_h = dattn[:, h * VDIM:(h + 1) * VDIM]
            do_ref[h] = jnp.concatenate([jnp.zeros((tm, NOPE), F32), do_h], axis=-1).astype(BF16)
        feature = lax.broadcasted_iota(jnp.int32, (G_WIDTH, LANES), 0) // VDIM
        column = lax.broadcasted_iota(jnp.int32, (G_WIDTH, LANES), 1)
        head_sums = jnp.dot(dattn * attn, jnp.where(feature == column, 1.0, 0.0).astype(F32),
                            preferred_element_type=F32, precision=lax.Precision.HIGH)
        dsums_t = jnp.transpose(head_sums)
        for h in range(HEADS):
            drow_ref[h] = dsums_t[h:h + 1, :]
        dza = d_out_a * attn * (sig_a * (1.0 + za * (1.0 - sig_a)))
        dsgu = d_out_b * silu_b
        dzb = d_out_b * sgu * (sig_b * (1.0 + zb * (1.0 - sig_b)))
        du = dsgu * sv * _gelu_grad(u)
        dsv = dsgu * ug
        dsv_b = dsv.astype(BF16)
        for cix in range(n_chunks):
            dbsp_acc[...] += dsv[cix * CHUNK:(cix + 1) * CHUNK, :]
        dsv_wide, vg_wide = side_by_side(dsv_b), side_by_side(vg_b)
        dvg = by_chunk(spatial(wst_ref, dsv_wide))
        for h in range(HEADS):
            mine = jnp.where(own_lanes(h), dsv_wide[h // 2], jnp.zeros_like(dsv_wide[h // 2]))
            dws_ref[h] += _dot_nt(mine, vg_wide[h // 2])
        dsg_ref[...] += jnp.sum(dvg * vhat, axis=0, keepdims=True)
        dsb_ref[...] += jnp.sum(dvg, axis=0, keepdims=True)
        dvhat = dvg * sg_ref[...]
        dgv = rstd_v * (dvhat - jnp.mean(dvhat, axis=-1, keepdims=True)
                        - vhat * jnp.mean(dvhat * vhat, axis=-1, keepdims=True))
        dv = dgv * _gelu_grad(vpre)
        drest_ref[...] = jnp.concatenate([dza, du, dv, dzb], axis=-1).astype(BF16)

        @pl.when(step == n_steps - 1)
        def _():
            tri = (lax.broadcasted_iota(jnp.int32, (CHUNK, CHUNK), 0)
                   >= lax.broadcasted_iota(jnp.int32, (CHUNK, CHUNK), 1))
            for h in range(HEADS):
                dws_ref[h] = jnp.where(tri, dws_ref[h], 0.0)
            tot = dbsp_acc[...]
            lane = lax.broadcasted_iota(jnp.int32, (CHUNK, LANES), 1)
            dbs = jnp.zeros((CHUNK, LANES), F32)
            for h in range(HEADS):
                head_sum = jnp.sum(tot[:, h * G_HEAD_DIM:(h + 1) * G_HEAD_DIM], axis=-1, keepdims=True)
                dbs = jnp.where(lane == h, head_sum, dbs)
            dbs_ref[...] = dbs

    full = lambda a: pl.BlockSpec(a.shape, lambda i: (0,) * a.ndim)
    tile = lambda w, j=0: pl.BlockSpec((tm, w), lambda i, j=j: (i, j))
    heads = pl.BlockSpec((HEADS, tm, HEAD_PAD), lambda i: (0, i, 0))
    acc = lambda shape: (pl.BlockSpec(shape, lambda i: (0,) * len(shape)), jax.ShapeDtypeStruct(shape, F32))
    accs = [acc((D_MODEL, D_MODEL)), acc((HEADS, CHUNK, CHUNK)), acc((CHUNK, LANES)), acc((1, D_MODEL)),
            acc((1, D_MODEL)), acc((1, G_WIDTH)), acc((1, G_WIDTH)), acc((1, LANES))]
    return pl.pallas_call(
        body, name="mid", grid=(n_steps,),
        in_specs=[tile(D_MODEL), tile(D_MODEL), tile(G_WIDTH, 1), tile(G_WIDTH, 2), tile(G_WIDTH, 3), tile(G_WIDTH, 4),
                  heads, full(w_out), full(ws_low), full(ws_low_t), full(bsp), full(sgu_g), full(sgu_b),
                  full(ln_g), full(ln_b)],
        out_specs=[tile(D_MODEL), heads, pl.BlockSpec((HEADS, 1, tm), lambda i: (0, 0, i)), tile(4 * G_WIDTH)]
        + [a[0] for a in accs],
        out_shape=[jax.ShapeDtypeStruct((t, D_MODEL), F32), jax.ShapeDtypeStruct((HEADS, t, HEAD_PAD), BF16),
                   jax.ShapeDtypeStruct((HEADS, 1, t), F32), jax.ShapeDtypeStruct((t, 4 * G_WIDTH), BF16)]
        + [a[1] for a in accs],
        scratch_shapes=[pltpu.VMEM((CHUNK, G_WIDTH), F32)],
        compiler_params=_cparams(("arbitrary",)),
    )(x, target, proj, proj, proj, proj, ol, w_out, ws_low, ws_low_t, bsp, sgu_g, sgu_b, ln_g, ln_b)


def _attn_bwd(q, k, v, do, lse_row, d_row):
    t = q.shape[1]
    bk, bq = ATTN_BWD_WIDE, ATTN_NARROW
    n_diag = bk // bq
    half = bq // 2
    last = t // bq - 1
    chunk = SOFTMAX_ROWS

    def body(q_ref, k_ref, v_ref, do_ref, lse_ref, drow_ref, dqt_ref, dk_ref, dv_ref,
             s0, s1, e0, e1, p0, p1, g0, g1, kt_scr):
        j = pl.program_id(1)
        at = lambda i: pl.ds(pl.multiple_of(i * bq, bq), bq)

        @pl.when(j == 0)
        def _():
            dqt_ref[...] = jnp.zeros_like(dqt_ref)

        kt_scr[...] = jnp.transpose(k_ref[0].astype(F32)).astype(BF16)
        dk_ref[...] = jnp.zeros_like(dk_ref)
        dv_ref[...] = jnp.zeros_like(dv_ref)

        whole_tile = ((slice(0, bk), slice(0, bq)),)

        def queries(i, lanes):
            return pl.ds(pl.multiple_of(i * bq + lanes.start, half), lanes.stop - lanes.start)

        def products(i, s_out, e_out, areas=whole_tile):
            i = jnp.minimum(i, last)
            for keys, lanes in areas:
                s_out[keys, lanes] = _dot_nt(k_ref[0, keys, :], q_ref[0, queries(i, lanes), :])
                e_out[keys, lanes] = _dot_nt(v_ref[0, keys, :], do_ref[0, queries(i, lanes), :])

        def gradients(i, p_in, g_in, areas=whole_tile):
            for keys, lanes in areas:
                dv_ref[0, keys, :] += _dot(p_in[keys, lanes], do_ref[0, queries(i, lanes), :])
                dk_ref[0, keys, :] += _dot(g_in[keys, lanes], q_ref[0, queries(i, lanes), :])
                dqt_ref[0, :, queries(i, lanes)] += _dot(kt_scr[:, keys], g_in[keys, lanes])

        def elementwise(i, s_in, e_in, p_out, g_out, qry0=None, areas=whole_tile):
            for keys, lanes in areas:
                width = lanes.stop - lanes.start
                step = chunk if qry0 is None else half
                lse = lse_ref[0, :, queries(i, lanes)]
                dsum = drow_ref[0, :, queries(i, lanes)]
                for r in range(keys.start, keys.stop, step):
                    p = jnp.exp2(s_in[r:r + step, lanes] - lse)
                    if qry0 is not None:
                        key = lax.broadcasted_iota(jnp.int32, (step, width), 0) + r
                        qry = lax.broadcasted_iota(jnp.int32, (step, width), 1) + (qry0 + lanes.start)
                        p = jnp.where(qry >= key, p, 0.0)
                    p_out[r:r + step, lanes] = p.astype(BF16)
                    g_out[r:r + step, lanes] = (p * (e_in[r:r + step, lanes] - dsum)).astype(BF16)

        def one_pass(i, s_in, e_in, s_out, e_out, p_prev, g_prev, p_cur, g_cur):
            products(i + 1, s_out, e_out)
            gradients(i - 1, p_prev, g_prev)
            elementwise(i, s_in, e_in, p_cur, g_cur)

        first = n_diag * j

        def areas_of(u):
            if u >= n_diag:
                return whole_tile
            return ((slice(0, u * bq + half), slice(0, bq)), (slice(u * bq + half, (u + 1) * bq), slice(half, bq)))

        even, odd = (s0, e0, p0, g0), (s1, e1, p1, g1)
        products(first, s0, e0, areas_of(0))
        products(first + 1, s1, e1, areas_of(1))
        elementwise(first, s0, e0, p0, g0, qry0=0, areas=areas_of(0))
        for u in range(1, n_diag):
            (s_in, e_in, p_cur, g_cur), (s_out, e_out, p_prev, g_prev) = (odd, even) if u % 2 else (even, odd)
            products(first + u + 1, s_out, e_out, areas_of(u + 1))
            gradients(first + u - 1, p_prev, g_prev, areas_of(u - 1))
            elementwise(first + u, s_in, e_in, p_cur, g_cur, qry0=u * bq, areas=areas_of(u))
        corner = (slice(bk - half, bk), slice(0, half))
        p1[corner] = jnp.zeros((half, half), BF16)
        g1[corner] = jnp.zeros((half, half), BF16)

        def two_passes(n, _):
            i = first + n_diag + 2 * n
            one_pass(i, s0, e0, s1, e1, p1, g1, p0, g0)
            one_pass(i + 1, s1, e1, s0, e0, p0, g0, p1, g1)
            return 0

        lax.fori_loop(0, (last - first - n_diag + 1) // 2, two_passes, 0)
        gradients(last, p1, g1)
        dk_ref[0] = dk_ref[0] * LN2

    whole = pl.BlockSpec((1, t, HEAD_PAD), lambda h, j: (h, 0, 0))
    block = pl.BlockSpec((1, bk, HEAD_PAD), lambda h, j: (h, j, 0))
    rows = pl.BlockSpec((1, 1, t), lambda h, j: (h, 0, 0), pipeline_mode=pl.Buffered(1))
    shape = jax.ShapeDtypeStruct((HEADS, t, HEAD_PAD), F32)
    tile = lambda dtype: pltpu.VMEM((bk, bq), dtype)
    return pl.pallas_call(
        body, name="attn_bwd", grid=(HEADS, t // bk),
        in_specs=[whole, block, block, whole, rows, rows],
        out_specs=[pl.BlockSpec((1, HEAD_PAD, t), lambda h, j: (h, 0, 0)), block, block],
        out_shape=[jax.ShapeDtypeStruct((HEADS, HEAD_PAD, t), F32), shape, shape],
        scratch_shapes=[tile(F32), tile(F32), tile(F32), tile(F32), tile(BF16), tile(BF16),
                        tile(BF16), tile(BF16), pltpu.VMEM((HEAD_PAD, bk), BF16)],
        compiler_params=_cparams(("arbitrary", "arbitrary"), vmem_limit=ATTN_BWD_VMEM_LIMIT),
    )(q, k, v, do, lse_row, d_row)


def _bwd_tail(dq, dk, dv, proj, pos_row, invf_col, w_heads, q_g, kv_g, x, dr, drest, wp_in):
    t = proj.shape[0]
    tm = PROJ_TILE
    n_head = 4 * LANES

    def body(dq_ref, dk_ref, dv_ref, ph_ref, pos_ref, invf_ref, wh_ref, qg_ref, kvg_ref,
             x_ref, dr_ref, drest_ref, win_ref,
             gx_ref, dwin_ref, dwh_ref, dqg_ref, dkvg_ref):
        @pl.when(pl.program_id(0) == 0)
        def _():
            dwin_ref[...] = jnp.zeros_like(dwin_ref)
            dwh_ref[...] = jnp.zeros_like(dwh_ref)
            dqg_ref[...] = jnp.zeros_like(dqg_ref)
            dkvg_ref[...] = jnp.zeros_like(dkvg_ref)

        xb = x_ref[...].astype(BF16)
        dr_b = drest_ref[...]
        dwin_ref[:, n_head:] += _dot_tn(xb, dr_b)
        gx_rest = DN_ALPHA * dr_ref[...] + _dot_nt(dr_b, win_ref[:, n_head:])

        cos, s1, s2 = _rope_tables(pos_ref[...], invf_ref[...])
        lane = lax.broadcasted_iota(jnp.int32, (tm, LANES), 1)
        c_q = ph_ref[:, :Q_LORA]
        c_kv = ph_ref[:, Q_LORA:Q_LORA + KV_LORA]
        rstd_q = lax.rsqrt(jnp.mean(c_q * c_q, axis=-1, keepdims=True) + EPS)
        rstd_kv = lax.rsqrt(jnp.mean(c_kv * c_kv, axis=-1, keepdims=True) + EPS)
        qhat = c_q * rstd_q
        kvhat = c_kv * rstd_kv
        cqn = (qhat * qg_ref[...]).astype(BF16)
        ckvn = (kvhat * kvg_ref[...]).astype(BF16)
        dkr_rot = jnp.zeros((tm, LANES), F32)
        dq_heads, dkv_heads = [], []
        for h in range(HEADS):
            dq_heads.append(_rope(jnp.transpose(dq_ref[h]) * ATTN_SCALE, cos, s1, s2, -1.0).astype(BF16))
            dk_h = dk_ref[h]
            dkv_heads.append(jnp.where(lane < NOPE, dk_h, dv_ref[h]).astype(BF16))
            dkr_rot = dkr_rot + dk_h
        dq_all = jnp.concatenate(dq_heads, axis=1)
        dkv_all = jnp.concatenate(dkv_heads, axis=1)
        dwq_all = _dot_tn(cqn, dq_all)
        dwkv_all = _dot_tn(ckvn, dkv_all)
        for h in range(HEADS):
            dwh_ref[h, :Q_LORA, :] += dwq_all[:, h * HEAD_PAD:(h + 1) * HEAD_PAD]
            dwh_ref[h, Q_LORA:, :] += dwkv_all[:, h * HEAD_PAD:(h + 1) * HEAD_PAD]
        dcqn = _dot_nt(dq_all, jnp.concatenate([wh_ref[h, :Q_LORA, :] for h in range(HEADS)], axis=1))
        dckvn = _dot_nt(dkv_all, jnp.concatenate([wh_ref[h, Q_LORA:, :] for h in range(HEADS)], axis=1))
        rot_lanes = (lane >= KR_LO) & (lane < KR_LO + ROPE)
        dkr_raw = jnp.where(rot_lanes, _rope(dkr_rot, cos, s1, s2, -1.0), 0.0)
        dqg_ref[...] += jnp.sum(dcqn * qhat, axis=0, keepdims=True)
        dkvg_ref[...] += jnp.sum(dckvn * kvhat, axis=0, keepdims=True)
        dqh = dcqn * qg_ref[...]
        dkvh = dckvn * kvg_ref[...]
        dc_q = rstd_q * (dqh - qhat * jnp.mean(dqh * qhat, axis=-1, keepdims=True))
        dc_kv = rstd_kv * (dkvh - kvhat * jnp.mean(dkvh * kvhat, axis=-1, keepdims=True))
        dh_b = jnp.concatenate([dc_q, dc_kv, dkr_raw], axis=-1).astype(BF16)
        dwin_ref[:, :n_head] += _dot_tn(xb, dh_b)
        gx_ref[...] = gx_rest + _dot_nt(dh_b, win_ref[:, :n_head])

    full = lambda a: pl.BlockSpec(a.shape, lambda i: (0,) * a.ndim)
    tile = lambda w: pl.BlockSpec((tm, w), lambda i: (i, 0))
    heads = pl.BlockSpec((HEADS, tm, HEAD_PAD), lambda i: (0, i, 0))
    acc = lambda shape: (pl.BlockSpec(shape, lambda i: (0,) * len(shape)), jax.ShapeDtypeStruct(shape, F32))
    accs = [acc(wp_in.shape), acc(w_heads.shape), acc((1, Q_LORA)), acc((1, KV_LORA))]
    return pl.pallas_call(
        body, name="bwd_tail", grid=(t // tm,),
        in_specs=[pl.BlockSpec((HEADS, HEAD_PAD, tm), lambda i: (0, 0, i)), heads, heads, tile(n_head),
                  pl.BlockSpec((1, tm), lambda i: (0, i)), full(invf_col), full(w_heads), full(q_g), full(kv_g),
                  tile(D_MODEL), tile(D_MODEL), tile(drest.shape[1]), full(wp_in)],
        out_specs=[tile(D_MODEL)] + [a[0] for a in accs],
        out_shape=[jax.ShapeDtypeStruct((t, D_MODEL), F32)] + [a[1] for a in accs],
        compiler_params=_cparams(("arbitrary",), vmem_limit=BWD_TAIL_VMEM_LIMIT),
    )(dq, dk, dv, proj, pos_row, invf_col, w_heads, q_g, kv_g, x, dr, drest, wp_in)


def _adam_update(g, w, m, v):
    m_new = ADAM_B1 * m + (1.0 - ADAM_B1) * g
    v_new = ADAM_B2 * v + (1.0 - ADAM_B2) * (g * g)
    m_hat = m_new / (1.0 - ADAM_B1 ** ADAM_STEP)
    v_hat = v_new / (1.0 - ADAM_B2 ** ADAM_STEP)
    return -ADAM_LR * (m_hat / (jnp.sqrt(v_hat) + ADAM_EPS) + ADAM_WD * w), m_new, v_new


def _adam(parts, w, m, v, *, name, tile_rows, transposed=False):
    n, rows, cols = parts.shape
    lane_pad = -(-cols // LANES) * LANES
    own_rows = rows if transposed else w.shape[0]
    assert own_rows == rows or tile_rows == rows

    def body(p_ref, w_ref, m_ref, v_ref, g_ref, d_ref, nm_ref, nv_ref, *scratch):
        g = p_ref[0].astype(F32)
        for s in range(1, n):
            g = g + p_ref[s].astype(F32)
        if transposed:
            wide_ref, = scratch
            wide_ref[:, lane_pad - LANES:] = jnp.zeros((tile_rows, LANES), F32)
            wide_ref[:, :cols] = g
            g = jnp.transpose(wide_ref[...])[:cols]
        g_ref[...] = g
        d_ref[...], nm_ref[...], nv_ref[...] = _adam_update(g[:w_ref.shape[0]], w_ref[...], m_ref[...], v_ref[...])

    if transposed:
        flat = grad = pl.BlockSpec((cols, tile_rows), lambda i: (0, i))
        shape = grad_shape = jax.ShapeDtypeStruct((cols, rows), F32)
        scratch = [pltpu.VMEM((tile_rows, lane_pad), F32)]
    else:
        own_tile = min(tile_rows, own_rows)
        flat = pl.BlockSpec((own_tile, cols), lambda i: (i, 0))
        grad = pl.BlockSpec((tile_rows, cols), lambda i: (i, 0))
        shape, grad_shape = jax.ShapeDtypeStruct((own_rows, cols), F32), jax.ShapeDtypeStruct((rows, cols), F32)
        scratch = []
    return pl.pallas_call(
        body, name=name, grid=(rows // tile_rows,),
        in_specs=[pl.BlockSpec((n, tile_rows, cols), lambda i: (0, i, 0)), flat, flat, flat],
        out_specs=[grad, flat, flat, flat], out_shape=[grad_shape, shape, shape, shape], scratch_shapes=scratch,
        compiler_params=_cparams(("arbitrary",)),
    )(parts, w, m, v)


def _adam_replicated(rep_g, ws, ms, vs):
    count = len(ws)
    small_rows = REP_ROWS - CHUNK

    def body(g_ref, *refs):
        w_refs, m_refs, v_refs = refs[:count], refs[count:2 * count], refs[2 * count:3 * count]
        outs, last_ref, slab_ref = refs[3 * count:7 * count], refs[7 * count], refs[7 * count + 1]
        for d in range(N_DEV):
            slab_ref[d * small_rows:(d + 1) * small_rows, :] = g_ref[d, CHUNK:, :]
        last_ref[...] = slab_ref[N_DEV * small_rows - 1:, LANES - 1:]
        at = 0
        for k, w_ref in enumerate(w_refs):
            if w_ref.ndim == 3:
                g = g_ref[:, :CHUNK, :]
            else:
                n_rows = w_ref.size // LANES
                g = slab_ref[at:at + n_rows, :].reshape(w_ref.shape)
                at += n_rows
            delta, m_new, v_new = _adam_update(g, w_ref[...], m_refs[k][...], v_refs[k][...])
            for which, val in enumerate((g, delta, m_new, v_new)):
                outs[which * count + k][...] = val

    shapes = [jax.ShapeDtypeStruct(w.shape, F32) for w in ws]
    res = pl.pallas_call(
        body, name="adam_rep", out_shape=shapes * 4 + [jax.ShapeDtypeStruct((1, 1), F32)],
        scratch_shapes=[pltpu.VMEM((N_DEV * small_rows, LANES), F32)],
        compiler_params=_cparams(),
    )(rep_g, *ws, *ms, *vs)
    return [res[which * count:(which + 1) * count] for which in range(4)], res[4 * count]


def _pack_small(vals, last):
    flat = jnp.concatenate([v.reshape(-1) for v in vals])
    pad = SMALL_LEN - flat.shape[0]
    return jnp.concatenate([flat, jnp.zeros((pad - 1,), F32), last.reshape(1)])


UQ_SHARD = HEADS * (NOPE + ROPE) // N_DEV
HEAD_ROWS = Q_LORA + KV_LORA
MIXED_ROWS = HEAD_ROWS + CHUNK + SMALL_LEN // N_DEV // LANES


def _head_slab(w_uq_shard, w_ukv_shard):
    return jnp.concatenate([jnp.pad(w_uq_shard, ((0, 0), (0, LANES - UQ_SHARD))), w_ukv_shard])


IN_SHARD = D_IN // N_DEV


def _w_in_pieces():
    split = Q_LORA + KV_LORA
    moves = ((0, split, 0), (split, split + ROPE, KR_LO), (split + ROPE, D_IN, LANES - ROPE))
    pieces = []
    for s in range(N_DEV):
        lo, hi = s * IN_SHARD, (s + 1) * IN_SHARD
        for a, b, shift in moves:
            a, b = max(a, lo), min(b, hi)
            if a < b:
                pieces.append((s, a - lo, a + shift, b - a))
    return pieces


def _w_in_shards(dwp_in):
    tr = TOKEN_TILE
    by_shard = [[p for p in _w_in_pieces() if p[0] == s] for s in range(N_DEV)]

    def body(w_ref, o_ref):
        for s, pieces in enumerate(by_shard):
            parts = [w_ref[:, dst:dst + width] for _, _, dst, width in pieces]
            o_ref[s] = parts[0] if len(parts) == 1 else jnp.concatenate(parts, axis=1)

    return pl.pallas_call(
        body, name="w_in_split", grid=(D_MODEL // tr,),
        in_specs=[pl.BlockSpec((tr, D_IN_PAD), lambda i: (i, 0))],
        out_specs=pl.BlockSpec((N_DEV, tr, IN_SHARD), lambda i: (0, i, 0)),
        out_shape=jax.ShapeDtypeStruct((N_DEV, D_MODEL, IN_SHARD), dwp_in.dtype),
        compiler_params=_cparams(("arbitrary",)),
    )(dwp_in)


def kernel(x, positions, w_in, q_norm_g, w_uq, kv_norm_g, w_ukv, sgu_norm_g, sgu_norm_b, w_spatial, b_spatial, w_out, ln_g, ln_b, loss_target, m_w_in, m_q_norm_g, m_w_uq, m_kv_norm_g, m_w_ukv, m_sgu_norm_g, m_sgu_norm_b, m_w_spatial, m_b_spatial, m_w_out, m_ln_g, m_ln_b, v_w_in, v_q_norm_g, v_w_uq, v_kv_norm_g, v_w_ukv, v_sgu_norm_g, v_sgu_norm_b, v_w_spatial, v_b_spatial, v_w_out, v_ln_g, v_ln_b):
    seq = x.shape[1]
    x2 = x.reshape(seq, D_MODEL)
    tgt2 = loss_target.reshape(seq, D_MODEL)
    pos_row = positions.reshape(1, seq)

    w_in_shards, w_out_shards, w_heads = _gather_two_level(
        [w_in.astype(BF16), w_out.astype(BF16), _head_slab(w_uq, w_ukv).astype(BF16)],
        name="wgather")
    (loss_part, grad_x, d_in, d_heads, d_out, d_ws, d_bs_t, d_lng, d_lnb, d_sgug, d_sgub, d_qg, d_kvg) = _local_step(
        x2, tgt2, pos_row, w_in_shards, w_heads, w_out_shards.reshape(D_MODEL, D_MODEL), q_norm_g, kv_norm_g,
        sgu_norm_g, sgu_norm_b, w_spatial, b_spatial, ln_g, ln_b)

    small_part = _pack_small([d_qg, d_kvg, d_sgug, d_sgub, d_bs_t[:, :HEADS].T, d_lng, d_lnb], last=loss_part[0, :1])
    mixed = jnp.concatenate([d_heads, d_ws, small_part.reshape(N_DEV, -1, LANES)], axis=1)
    by_chip = [g.reshape((N_CHIPS, 2) + g.shape[1:])
               for g in (d_in, d_out.reshape(N_DEV, D_MODEL // N_DEV, D_MODEL), mixed)]
    from_sibling = _sibling_swap(by_chip, name="gswap")
    core = lax.axis_index("c").astype(jnp.int32).reshape(1)
    pair_sums = [_pair_sum(a, b, core, name=nm, out_dtype=dt) for a, b, nm, dt in zip(
        by_chip, from_sibling, ("gsum_in", "gsum_out", "gsum_mixed"), (BF16, BF16, F32))]
    recv_in, recv_out, recv_mixed = _chip_exchange(pair_sums, name="gexch")

    res_in = [a.T for a in _adam(recv_in, w_in.T, m_w_in.T, v_w_in.T, name="adam_in", tile_rows=PROJ_TILE,
                                 transposed=True)]
    res_out = _adam(recv_out, w_out, m_w_out, v_w_out, name="adam_out", tile_rows=D_MODEL // N_DEV)
    res_mixed = _adam(recv_mixed, _head_slab(w_uq, w_ukv), _head_slab(m_w_uq, m_w_ukv), _head_slab(v_w_uq, v_w_ukv),
                      name="adam_mixed", tile_rows=MIXED_ROWS)

    rep_g, = _gather_direct([res_mixed[0]], name="sgather", first_row=HEAD_ROWS)
    res_rep, loss = _adam_replicated(
        rep_g,
        [q_norm_g, kv_norm_g, sgu_norm_g, sgu_norm_b, w_spatial, b_spatial, ln_g, ln_b],
        [m_q_norm_g, m_kv_norm_g, m_sgu_norm_g, m_sgu_norm_b, m_w_spatial, m_b_spatial, m_ln_g, m_ln_b],
        [v_q_norm_g, v_kv_norm_g, v_sgu_norm_g, v_sgu_norm_b, v_w_spatial, v_b_spatial, v_ln_g, v_ln_b])

    def ordered(which):
        r_qg, r_kvg, r_sg, r_sb, r_ws, r_bs, r_lg, r_lb = res_rep[which]
        heads = res_mixed[which]
        return [res_in[which], r_qg, heads[:Q_LORA, :UQ_SHARD], r_kvg, heads[Q_LORA:HEAD_ROWS], r_sg, r_sb, r_ws, r_bs,
                res_out[which], r_lg, r_lb]

    outs = [loss.reshape(()), grad_x.reshape(x.shape)]
    for which in range(4):
        outs += ordered(which)
    return tuple(outs)


def _local_step(x2, tgt2, pos_row, w_in_shards, w_heads, w_out_full, q_norm_g, kv_norm_g, sgu_norm_g, sgu_norm_b,
                w_spatial, b_spatial, ln_g, ln_b):
    half = jnp.arange(HALF, dtype=F32)
    invf_col = (1.0 / (ROPE_THETA ** (half / HALF))).reshape(HALF, 1)
    tri = jnp.tril(jnp.ones((CHUNK, CHUNK), dtype=bool))
    ws_low = jnp.where(tri[None], w_spatial, 0.0).astype(BF16)
    ws_low_t = ws_low.transpose(0, 2, 1)
    bsp = jnp.repeat(b_spatial.T, G_HEAD_DIM, axis=1)
    row = lambda a: a.reshape(1, -1)

    proj, q, k, v, vt, wp_in = _fwd_proj(x2, pos_row, invf_col, w_in_shards, w_heads, row(q_norm_g), row(kv_norm_g))
    o, lse_row = _attn_fwd(q, k, vt)
    (dr, do, d_row, drest, d_out, d_ws, d_bs_t, d_lng, d_lnb, d_sgug, d_sgub, loss_part) = _mid(
        x2, tgt2, proj, o, w_out_full, ws_low, ws_low_t, bsp, row(sgu_norm_g), row(sgu_norm_b), row(ln_g), row(ln_b))
    dqt, dk, dv = _attn_bwd(q, k, v, do, lse_row, d_row)
    grad_x, dwp_in, d_heads, d_qg, d_kvg = _bwd_tail(dqt, dk, dv, proj, pos_row, invf_col, w_heads, row(q_norm_g),
                                                      row(kv_norm_g), x2, dr, drest, wp_in)
    return (loss_part, grad_x, _w_in_shards(dwp_in), d_heads, d_out, d_ws, d_bs_t, d_lng, d_lnb, d_sgug, d_sgub,
            d_qg, d_kvg)
```

```python
import math

import jax
import jax.numpy as jnp
from jax import lax
from jax.experimental import pallas as pl
from jax.experimental.pallas import tpu as pltpu

F32 = jnp.float32
BF16 = jnp.bfloat16

N_DEV = 8
D_MODEL = 1024
HEADS = 8
NOPE = 64
ROPE = 32
HALF = ROPE // 2
VDIM = 64
Q_LORA = 256
KV_LORA = 128
G_WIDTH = 512
G_HEAD_DIM = 64
CHUNK = 128
HEAD_PAD = 128
D_IN = 2464
D_IN_PAD = 2560
KR_LO = NOPE
SUM_ROW = NOPE - 1
LIVE_ROWS = slice(NOPE - 16, HEAD_PAD)
ROPE_THETA = 10000.0
DN_ALPHA = 2.0 ** 0.25
EPS = 1e-5
ATTN_SCALE = 1.0 / math.sqrt(NOPE + ROPE)
ADAM_LR, ADAM_B1, ADAM_B2, ADAM_EPS, ADAM_WD, ADAM_STEP = 0.001, 0.9, 0.999, 1e-08, 0.01, 10

LANES = 128
REP_ROWS = 136
SMALL_LEN = 8192
VMEM_LIMIT = 56 * 1024 * 1024
ATTN_BWD_VMEM_LIMIT = 61 * 1024 * 1024
BWD_TAIL_VMEM_LIMIT = 61 * 1024 * 1024

TOKEN_TILE = 256
PROJ_TILE = 512
ATTN_FWD_WIDE = 2048
ATTN_BWD_WIDE = 2048
ATTN_NARROW = 512
SOFTMAX_ROWS = 512
LOG2E = 1.4426950408889634
LN2 = 0.6931471805599453
Q_PRESCALE = ATTN_SCALE * LOG2E


def _cparams(sem=None, vmem_limit=VMEM_LIMIT):
    return pltpu.CompilerParams(dimension_semantics=sem, vmem_limit_bytes=vmem_limit)


def _dot(a, b):
    return jnp.dot(a, b, preferred_element_type=F32)


def _dot_nt(a, b):
    return lax.dot_general(a, b, (((1,), (1,)), ((), ())), preferred_element_type=F32)


def _dot_tn(a, b):
    return lax.dot_general(a, b, (((0,), (0,)), ((), ())), preferred_element_type=F32)


def _sigmoid(z):
    return 1.0 / (1.0 + jnp.exp(-z))


def _normal_cdf(x):
    return 0.5 * (1.0 + lax.erf(x * 0.7071067811865476))


def _gelu_grad(x, cdf):
    return cdf + x * jnp.exp(-0.5 * x * x) * 0.3989422804014327


def _gather_direct(srcs, *, name, first_row=0):
    n = len(srcs)
    shapes = [(s.shape[0] - first_row,) + s.shape[1:] for s in srcs]

    def body(*refs):
        src_refs, out_refs = [r.at[pl.ds(first_row, shape[0])] for r, shape in zip(refs[:n], shapes)], refs[n:2 * n]
        send_sems, recv_sems, local_sems = refs[2 * n:]
        x, y, c = lax.axis_index("x"), lax.axis_index("y"), lax.axis_index("c")
        me = 4 * x + 2 * y + c
        mine = [pltpu.make_async_copy(src_refs[t], out_refs[t].at[me], local_sems.at[t]) for t in range(n)]
        for cp in mine:
            cp.start()
        sends, arrivals = [], []
        for k in (6, 7, 4, 5, 2, 3, 1):
            px = 1 - x if k & 4 else x
            py = 1 - y if k & 2 else y
            pc = 1 - c if k & 1 else c
            peer = 4 * px + 2 * py + pc
            for t in range(n):
                sem = (k - 1) * n + t
                cp = pltpu.make_async_remote_copy(
                    src_ref=src_refs[t], dst_ref=out_refs[t].at[me],
                    send_sem=send_sems.at[sem], recv_sem=recv_sems.at[sem],
                    device_id=(px, py, pc), device_id_type=pl.DeviceIdType.MESH)
                cp.start()
                sends.append(cp)
                arrivals.append(pltpu.make_async_remote_copy(
                    src_ref=src_refs[t], dst_ref=out_refs[t].at[peer],
                    send_sem=send_sems.at[sem], recv_sem=recv_sems.at[sem],
                    device_id=(x, y, c), device_id_type=pl.DeviceIdType.MESH))
        for cp in arrivals:
            cp.wait_recv()
        for cp in sends:
            cp.wait_send()
        for cp in mine:
            cp.wait()

    hbm = pl.BlockSpec(memory_space=pl.ANY)
    return pl.pallas_call(
        body, name=name,
        out_shape=[jax.ShapeDtypeStruct((N_DEV,) + shape, s.dtype) for shape, s in zip(shapes, srcs)],
        in_specs=[hbm] * n, out_specs=[hbm] * n,
        scratch_shapes=[pltpu.SemaphoreType.DMA(((N_DEV - 1) * n,)), pltpu.SemaphoreType.DMA(((N_DEV - 1) * n,)),
                        pltpu.SemaphoreType.DMA((n,))],
    )(*srcs)


def _gather_two_level(srcs, *, name):
    n = len(srcs)

    def body(*refs):
        src_refs, out_refs = refs[:n], refs[n:2 * n]
        send_sems, recv_sems, local_sems = refs[2 * n:]
        x, y, c = lax.axis_index("x"), lax.axis_index("y"), lax.axis_index("c")
        me, sibling = (x, y, c), (x, y, 1 - c)
        chips = [(1 - x, 1 - y), (1 - x, y), (x, 1 - y)]
        index = lambda px, py, pc: 4 * px + 2 * py + pc

        def copy(k, t, block, to, src=None):
            place = out_refs[t].at[index(*block)]
            return pltpu.make_async_remote_copy(
                src_ref=place if src is None else src, dst_ref=place,
                send_sem=send_sems.at[k * n + t], recv_sem=recv_sems.at[k * n + t],
                device_id=to, device_id_type=pl.DeviceIdType.MESH)

        mine = [pltpu.make_async_copy(src_refs[t], out_refs[t].at[index(*me)], local_sems.at[t]) for t in range(n)]
        for cp in mine:
            cp.start()
        first = [copy(1 + j, t, me, (*chip, c), src=src_refs[t]) for j, chip in enumerate(chips) for t in range(n)]
        first += [copy(0, t, me, sibling, src=src_refs[t]) for t in range(n)]
        for cp in first:
            cp.start()
        passed = []
        for j, chip in enumerate(chips):
            for t in range(n):
                copy(1 + j, t, (*chip, c), me).wait_recv()
                cp = copy(4 + j, t, (*chip, c), sibling)
                cp.start()
                passed.append(cp)
        for t in range(n):
            copy(0, t, sibling, me).wait_recv()
        for j, chip in enumerate(chips):
            for t in range(n):
                copy(4 + j, t, (*chip, 1 - c), me).wait_recv()
        for cp in first + passed:
            cp.wait_send()
        for cp in mine:
            cp.wait()

    hbm = pl.BlockSpec(memory_space=pl.ANY)
    return pl.pallas_call(
        body, name=name,
        out_shape=[jax.ShapeDtypeStruct((N_DEV,) + s.shape, s.dtype) for s in srcs],
        in_specs=[hbm] * n, out_specs=[hbm] * n,
        scratch_shapes=[pltpu.SemaphoreType.DMA((7 * n,)), pltpu.SemaphoreType.DMA((7 * n,)),
                        pltpu.SemaphoreType.DMA((n,))],
    )(*srcs)


N_CHIPS = N_DEV // 2


def _sibling_swap(srcs, *, name):
    n = len(srcs)

    def body(*refs):
        src_refs, out_refs = refs[:n], refs[n:2 * n]
        send_sems, recv_sems = refs[2 * n:]
        x, y, c = lax.axis_index("x"), lax.axis_index("y"), lax.axis_index("c")
        sends = []
        for chip in range(N_CHIPS):
            for t in range(n):
                cp = pltpu.make_async_remote_copy(
                    src_ref=src_refs[t].at[chip, 1 - c], dst_ref=out_refs[t].at[chip],
                    send_sem=send_sems.at[chip * n + t], recv_sem=recv_sems.at[chip * n + t],
                    device_id=(x, y, 1 - c), device_id_type=pl.DeviceIdType.MESH)
                cp.start()
                sends.append(cp)
        for cp in sends:
            cp.wait_recv()
        for cp in sends:
            cp.wait_send()

    hbm = pl.BlockSpec(memory_space=pl.ANY)
    return pl.pallas_call(
        body, name=name,
        out_shape=[jax.ShapeDtypeStruct((N_CHIPS,) + s.shape[2:], s.dtype) for s in srcs],
        in_specs=[hbm] * n, out_specs=[hbm] * n,
        scratch_shapes=[pltpu.SemaphoreType.DMA((N_CHIPS * n,)), pltpu.SemaphoreType.DMA((N_CHIPS * n,))],
    )(*srcs)


def _pair_sum(mine, theirs, core, *, name, out_dtype):
    _, _, rows, cols = mine.shape

    def body(core_ref, a_ref, b_ref, o_ref):
        o_ref[...] = (a_ref[0] + b_ref[...]).astype(out_dtype)

    return pl.pallas_call(
        body, name=name,
        grid_spec=pltpu.PrefetchScalarGridSpec(
            num_scalar_prefetch=1, grid=(N_CHIPS,),
            in_specs=[pl.BlockSpec((1, 1, rows, cols), lambda q, core_ref: (q, core_ref[0], 0, 0)),
                      pl.BlockSpec((1, rows, cols), lambda q, core_ref: (q, 0, 0))],
            out_specs=pl.BlockSpec((1, rows, cols), lambda q, core_ref: (q, 0, 0))),
        out_shape=jax.ShapeDtypeStruct((N_CHIPS, rows, cols), out_dtype),
        compiler_params=_cparams(("arbitrary",)),
    )(core, mine, theirs)


def _chip_exchange(srcs, *, name):
    n = len(srcs)

    def body(*refs):
        src_refs, out_refs = refs[:n], refs[n:2 * n]
        send_sems, recv_sems, local_sems = refs[2 * n:]
        x, y, c = lax.axis_index("x"), lax.axis_index("y"), lax.axis_index("c")
        my_chip = 2 * x + y
        mine = [pltpu.make_async_copy(src_refs[t].at[my_chip], out_refs[t].at[my_chip], local_sems.at[t])
                for t in range(n)]
        for cp in mine:
            cp.start()
        sends, arrivals = [], []
        for k in (3, 2, 1):
            px = 1 - x if k & 2 else x
            py = 1 - y if k & 1 else y
            peer_chip = 2 * px + py
            for t in range(n):
                sem = (k - 1) * n + t
                cp = pltpu.make_async_remote_copy(
                    src_ref=src_refs[t].at[peer_chip], dst_ref=out_refs[t].at[my_chip],
                    send_sem=send_sems.at[sem], recv_sem=recv_sems.at[sem],
                    device_id=(px, py, c), device_id_type=pl.DeviceIdType.MESH)
                cp.start()
                sends.append(cp)
                arrivals.append(pltpu.make_async_remote_copy(
                    src_ref=src_refs[t].at[peer_chip], dst_ref=out_refs[t].at[peer_chip],
                    send_sem=send_sems.at[sem], recv_sem=recv_sems.at[sem],
                    device_id=(x, y, c), device_id_type=pl.DeviceIdType.MESH))
        for cp in arrivals:
            cp.wait_recv()
        for cp in sends:
            cp.wait_send()
        for cp in mine:
            cp.wait()

    hbm = pl.BlockSpec(memory_space=pl.ANY)
    return pl.pallas_call(
        body, name=name,
        out_shape=[jax.ShapeDtypeStruct(s.shape, s.dtype) for s in srcs],
        in_specs=[hbm] * n, out_specs=[hbm] * n,
        scratch_shapes=[pltpu.SemaphoreType.DMA((3 * n,)), pltpu.SemaphoreType.DMA((3 * n,)),
                        pltpu.SemaphoreType.DMA((n,))],
    )(*srcs)


def _rope_tables(pos_row, invf_col):
    tm = pos_row.shape[1]
    ang = pos_row.astype(F32) * invf_col
    cos, sin = jnp.cos(ang), jnp.sin(ang)
    ones = lambda n: jnp.ones((n, tm), F32)
    zeros = lambda n: jnp.zeros((n, tm), F32)
    cos_t = jnp.concatenate([ones(KR_LO), cos, cos, ones(LANES - KR_LO - ROPE)], axis=0)
    first_t = jnp.concatenate([zeros(KR_LO), sin, zeros(LANES - KR_LO - HALF)], axis=0)
    second_t = jnp.concatenate([zeros(KR_LO + HALF), sin, zeros(LANES - KR_LO - ROPE)], axis=0)
    return jnp.transpose(cos_t), jnp.transpose(first_t), jnp.transpose(second_t)


def _rope(t, cos, sin_first, sin_second, sign):
    up = pltpu.roll(t, LANES - HALF, 1)
    down = pltpu.roll(t, HALF, 1)
    return t * cos - sign * (up * sin_first) + sign * (down * sin_second)


def _fwd_proj(x, pos_row, invf_col, w_in_shards, w_heads, q_g, kv_g):
    t = x.shape[0]
    tm = PROJ_TILE

    def body(x_ref, pos_ref, invf_ref, sh_ref, wh_ref, qg_ref, kvg_ref,
             proj_ref, q_ref, k_ref, v_ref, vt_ref, win_ref):
        @pl.when(pl.program_id(0) == 0)
        def _():
            win_ref[...] = jnp.zeros_like(win_ref)
            for s, src, dst, width in _w_in_pieces():
                win_ref[:, dst:dst + width] = sh_ref[s, :, src:src + width]

        proj = _dot(x_ref[...].astype(BF16), win_ref[...])
        proj_ref[...] = proj
        c_q = proj[:, :Q_LORA]
        c_kv = proj[:, Q_LORA:Q_LORA + KV_LORA]
        kr_raw = proj[:, Q_LORA + KV_LORA:Q_LORA + KV_LORA + LANES]
        cqn = (c_q * lax.rsqrt(jnp.mean(c_q * c_q, axis=-1, keepdims=True) + EPS) * qg_ref[...]).astype(BF16)
        ckvn = (c_kv * lax.rsqrt(jnp.mean(c_kv * c_kv, axis=-1, keepdims=True) + EPS) * kvg_ref[...]).astype(BF16)
        cos, s1, s2 = _rope_tables(pos_ref[...], invf_ref[...])
        kr = _rope(kr_raw, cos, s1, s2, 1.0)
        lane = lax.broadcasted_iota(jnp.int32, (tm, HEAD_PAD), 1)
        q_all = _dot(cqn, jnp.concatenate([wh_ref[h, :Q_LORA, :] for h in range(HEADS)], axis=1))
        kv_all = _dot(ckvn, jnp.concatenate([wh_ref[h, Q_LORA:, :] for h in range(HEADS)], axis=1))
        for h in range(HEADS):
            q_h = q_all[:, h * HEAD_PAD:(h + 1) * HEAD_PAD]
            kv_h = kv_all[:, h * HEAD_PAD:(h + 1) * HEAD_PAD]
            q_ref[h] = (_rope(q_h, cos, s1, s2, 1.0) * Q_PRESCALE).astype(BF16)
            k_ref[h] = jnp.where(lane < NOPE, kv_h, kr).astype(BF16)
            v_ref[h] = kv_h.astype(BF16)
            vt_ref[h] = jnp.transpose(jnp.where(lane == SUM_ROW, 1.0, kv_h)).astype(BF16)

    full = lambda a: pl.BlockSpec(a.shape, lambda i: (0,) * a.ndim)
    head_spec = pl.BlockSpec((HEADS, tm, HEAD_PAD), lambda i: (0, i, 0))
    head_shape = jax.ShapeDtypeStruct((HEADS, t, HEAD_PAD), BF16)
    return pl.pallas_call(
        body, name="fwd_proj", grid=(t // tm,),
        in_specs=[pl.BlockSpec((tm, D_MODEL), lambda i: (i, 0)), pl.BlockSpec((1, tm), lambda i: (0, i)),
                  full(invf_col), full(w_in_shards), full(w_heads), full(q_g), full(kv_g)],
        out_specs=[pl.BlockSpec((tm, D_IN_PAD), lambda i: (i, 0)), head_spec, head_spec, head_spec,
                   pl.BlockSpec((HEADS, HEAD_PAD, tm), lambda i: (0, 0, i)),
                   pl.BlockSpec((D_MODEL, D_IN_PAD), lambda i: (0, 0))],
        out_shape=[jax.ShapeDtypeStruct((t, D_IN_PAD), F32), head_shape, head_shape, head_shape,
                   jax.ShapeDtypeStruct((HEADS, HEAD_PAD, t), BF16),
                   jax.ShapeDtypeStruct((D_MODEL, D_IN_PAD), w_in_shards.dtype)],
        compiler_params=_cparams(("arbitrary",)),
    )(x, pos_row, invf_col, w_in_shards, w_heads, q_g, kv_g)


def _attn_fwd(q, k, vt):
    t = q.shape[1]
    bq, bk = ATTN_FWD_WIDE, ATTN_NARROW
    n_diag = bq // bk
    chunk = SOFTMAX_ROWS

    def body(q_ref, k_ref, vt_ref, o_ref, lse_ref, s0, s1, p0, p1, x0, x1, m_scr, a_scr, acc_scr):
        i = pl.program_id(1)
        at = lambda j: pl.ds(pl.multiple_of(j * bk, bk), bk)

        def exp_pass(s_in, block_max, p_out, diagonal=False, cols=slice(None)):
            width = bq if cols == slice(None) else cols.stop - cols.start

            def load(r):
                s = s_in[r:r + chunk, cols]
                if diagonal:
                    key = lax.broadcasted_iota(jnp.int32, (chunk, width), 0) + r
                    qry = lax.broadcasted_iota(jnp.int32, (chunk, width), 1)
                    s = jnp.where(qry >= key, s, -jnp.inf)
                return s

            if diagonal:
                block_max = jnp.max(load(0), axis=0, keepdims=True)
                for r in range(chunk, bk, chunk):
                    block_max = jnp.maximum(block_max, jnp.max(load(r), axis=0, keepdims=True))
            m_old = m_scr[:, cols]
            m_new = jnp.maximum(m_old, block_max)
            alpha = jnp.exp2(m_old - m_new)
            for r in range(0, bk, chunk):
                p_out[r:r + chunk, cols] = jnp.exp2(load(r) - m_new).astype(BF16)
            m_scr[:, cols] = m_new
            return alpha

        def scores(j, s_out, x_out):
            s = _dot_nt(k_ref[0, at(j), :], q_ref[0])
            s_out[...] = s
            x_out[...] = jnp.max(s, axis=0, keepdims=True)

        def value_product(j, p_in):
            return _dot(vt_ref[0, LIVE_ROWS, at(j)], p_in[...])

        def one_pass(j, s_in, x_in, s_out, x_out, p_prev, p_cur):
            scores(j + 1, s_out, x_out)
            acc_scr[...] = a_scr[...] * acc_scr[...] + value_product(jnp.maximum(j - 1, 0), p_prev)
            a_scr[...] = exp_pass(s_in, x_in[...], p_cur)

        scores(0, s0, x0)
        p1[...] = jnp.zeros_like(p1)
        a_scr[...] = jnp.ones_like(a_scr)
        m_scr[...] = jnp.full(m_scr.shape, -jnp.inf, F32)
        acc_scr[...] = jnp.zeros_like(acc_scr)

        def two_passes(n, _):
            one_pass(2 * n, s0, x0, s1, x1, p1, p0)
            one_pass(2 * n + 1, s1, x1, s0, x0, p0, p1)
            return 0

        lax.fori_loop(0, (n_diag // 2) * i, two_passes, 0)
        d = n_diag * i
        alpha, p_prev, cols = a_scr[...], p1, slice(0, bq)
        for u in range(n_diag + 1):
            s_in, s_next, p_cur = (s0, s1, p0) if u % 2 == 0 else (s1, s0, p1)
            if u + 1 < n_diag:
                ahead = slice((u + 1) * bk, bq)
                s_next[:, ahead] = _dot_nt(k_ref[0, at(d + u + 1), :], q_ref[0, ahead, :])
            acc_scr[:, cols] = alpha * acc_scr[:, cols] + _dot(vt_ref[0, LIVE_ROWS, at(jnp.maximum(d + u - 1, 0))],
                                                               p_prev[:, cols])
            if u < n_diag:
                cols = slice(u * bk, bq)
                alpha = exp_pass(s_in, None, p_cur, diagonal=True, cols=cols)
                p_prev = p_cur
        denom = acc_scr[SUM_ROW - LIVE_ROWS.start:NOPE - LIVE_ROWS.start, :]
        o = jnp.transpose(acc_scr[NOPE - LIVE_ROWS.start:, :] / denom)
        o_ref[0] = jnp.concatenate([jnp.zeros_like(o), o], axis=1)
        lse_ref[0] = m_scr[...] + jnp.log2(denom)

    tile = lambda dtype: pltpu.VMEM((bk, bq), dtype)
    stat = pltpu.VMEM((1, bq), F32)
    return pl.pallas_call(
        body, name="attn_fwd", grid=(HEADS, t // bq),
        in_specs=[pl.BlockSpec((1, bq, HEAD_PAD), lambda h, i: (h, i, 0)),
                  pl.BlockSpec((1, t, HEAD_PAD), lambda h, i: (h, 0, 0)),
                  pl.BlockSpec((1, HEAD_PAD, t), lambda h, i: (h, 0, 0))],
        out_specs=[pl.BlockSpec((1, bq, HEAD_PAD), lambda h, i: (h, i, 0)),
                   pl.BlockSpec((1, 1, bq), lambda h, i: (h, 0, i))],
        out_shape=[jax.ShapeDtypeStruct((HEADS, t, HEAD_PAD), F32), jax.ShapeDtypeStruct((HEADS, 1, t), F32)],
        scratch_shapes=[tile(F32), tile(F32), tile(BF16), tile(BF16), stat, stat, stat, stat,
                        pltpu.VMEM((HEAD_PAD - LIVE_ROWS.start, bq), F32)],
        compiler_params=_cparams(("arbitrary", "arbitrary")),
    )(q, k, vt)


def _mid(x, target, proj, ol, w_out, ws_low, ws_low_t, bsp, sgu_g, sgu_b, ln_g, ln_b):
    t = x.shape[0]
    tm = TOKEN_TILE
    n_steps = t // tm

    def body(x_ref, tgt_ref, za_ref, u_ref, v_ref, zb_ref, ol_ref, prev_za_ref, prev_u_ref, prev_v_ref, prev_zb_ref,
             prev_ol_ref, wout_ref, ws_ref, wst_ref, bsp_ref, sg_ref, sb_ref, lg_ref, lb_ref,
             dr_ref, do_ref, drow_ref, drest_ref, dwout_ref, dws_ref, dbs_ref, dlg_ref, dlb_ref, dsg_ref, dsb_ref,
             loss_ref, dbsp_acc, *kept_refs):
        step = pl.program_id(0)
        kept_sets = (kept_refs[:len(kept_refs) // 2], kept_refs[len(kept_refs) // 2:])

        @pl.when(step == 0)
        def _():
            dwout_ref[...] = jnp.zeros_like(dwout_ref)
            dws_ref[...] = jnp.zeros_like(dws_ref)
            dbs_ref[...] = jnp.zeros_like(dbs_ref)
            dlg_ref[...] = jnp.zeros_like(dlg_ref)
            dlb_ref[...] = jnp.zeros_like(dlb_ref)
            dsg_ref[...] = jnp.zeros_like(dsg_ref)
            dsb_ref[...] = jnp.zeros_like(dsb_ref)
            loss_ref[...] = jnp.zeros_like(loss_ref)
            dbsp_acc[...] = jnp.zeros_like(dbsp_acc)

        n_chunks = tm // CHUNK
        groups = G_WIDTH // LANES

        def side_by_side(a):
            return [jnp.concatenate([a[c * CHUNK:(c + 1) * CHUNK, g * LANES:(g + 1) * LANES] for c in range(n_chunks)],
                                    axis=1) for g in range(groups)]

        def by_chunk(wide):
            return jnp.concatenate([jnp.concatenate([wide[g][:, c * LANES:(c + 1) * LANES] for g in range(groups)], axis=1)
                                    for c in range(n_chunks)], axis=0)

        def own_lanes(h):
            lane = lax.broadcasted_iota(jnp.int32, (CHUNK, n_chunks * LANES), 1)
            return (lane % LANES) // G_HEAD_DIM == h % 2

        def spatial(w_ref, wide):
            return [sum(jnp.where(own_lanes(h), _dot(w_ref[h], wide[g]), 0.0) for h in (2 * g, 2 * g + 1))
                    for g in range(groups)]

        def value_lanes(o_ref):
            return jnp.concatenate([o_ref[h][:, NOPE:] for h in range(HEADS)], axis=-1)

        def forward(kept):
            (sig_a_ref, cdf_u_ref, cdf_v_ref, vhat_ref, sv_ref, sig_b_ref, xhat_ref, dh_ref, merged_ref, vg_ref,
             rstd_ref, rstd_v_ref) = kept
            attn = value_lanes(ol_ref)
            za = za_ref[...]
            sig_a = _sigmoid(za)
            out_a = attn * (za * sig_a)
            u = u_ref[...]
            cdf_u = _normal_cdf(u)
            vpre = v_ref[...]
            cdf_v = _normal_cdf(vpre)
            gv = vpre * cdf_v
            mu_v = jnp.mean(gv, axis=-1, keepdims=True)
            cen_v = gv - mu_v
            rstd_v = lax.rsqrt(jnp.mean(cen_v * cen_v, axis=-1, keepdims=True) + EPS)
            vhat = cen_v * rstd_v
            vg = vhat * sg_ref[...] + sb_ref[...]
            vg_b = vg.astype(BF16)
            sv = by_chunk(spatial(ws_ref, side_by_side(vg_b))) + jnp.tile(bsp_ref[...], (n_chunks, 1))
            zb = zb_ref[...]
            sig_b = _sigmoid(zb)
            out_b = ((u * cdf_u) * sv) * (zb * sig_b)
            merged = jnp.concatenate([out_a, out_b], axis=-1).astype(BF16)
            r = DN_ALPHA * x_ref[...] + _dot(merged, wout_ref[...])
            mu = jnp.mean(r, axis=-1, keepdims=True)
            cen = r - mu
            rstd = lax.rsqrt(jnp.mean(cen * cen, axis=-1, keepdims=True) + EPS)
            xhat = cen * rstd
            hout = xhat * lg_ref[...] + lb_ref[...]
            err = hout - tgt_ref[...]
            row_loss = jnp.mean(err * err, axis=-1, keepdims=True)
            loss_ref[...] += jnp.broadcast_to(0.5 * jnp.sum(row_loss, axis=0, keepdims=True), loss_ref.shape)
            for ref, val in zip(kept, (sig_a, cdf_u, cdf_v, vhat, sv, sig_b, xhat, err * (1.0 / D_MODEL), merged, vg_b,
                                       jnp.broadcast_to(rstd, (tm, LANES)), jnp.broadcast_to(rstd_v, (tm, LANES)))):
                ref[...] = val

        def backward(kept):
            (sig_a_ref, cdf_u_ref, cdf_v_ref, vhat_ref, sv_ref, sig_b_ref, xhat_ref, dh_ref, merged_ref, vg_ref,
             rstd_ref, rstd_v_ref) = kept
            attn = value_lanes(prev_ol_ref)
            za, u, vpre, zb = prev_za_ref[...], prev_u_ref[...], prev_v_ref[...], prev_zb_ref[...]
            sig_a, cdf_u, cdf_v, vhat, sv, sig_b = (sig_a_ref[...], cdf_u_ref[...], cdf_v_ref[...], vhat_ref[...],
                                                    sv_ref[...], sig_b_ref[...])
            xhat, dh, merged, vg_b = xhat_ref[...], dh_ref[...], merged_ref[...], vg_ref[...]
            rstd, rstd_v = rstd_ref[:, :1], rstd_v_ref[:, :1]
            silu_a, silu_b, ug = za * sig_a, zb * sig_b, u * cdf_u
            sgu = ug * sv
            dlg_ref[...] += jnp.sum(dh * xhat, axis=0, keepdims=True)
            dlb_ref[...] += jnp.sum(dh, axis=0, keepdims=True)
            dxhat = dh * lg_ref[...]
            dr = rstd * (dxhat - jnp.mean(dxhat, axis=-1, keepdims=True)
                         - xhat * jnp.mean(dxhat * xhat, axis=-1, keepdims=True))
            dr_ref[...] = dr
            dr_b = dr.astype(BF16)
            dwout_ref[...] += _dot_tn(merged, dr_b)
            dmerged = _dot_nt(dr_b, wout_ref[...])
            d_out_a = dmerged[:, :G_WIDTH]
            d_out_b = dmerged[:, G_WIDTH:]
            dattn = d_out_a * silu_a
            for h in range(HEADS):
                do_h = dattn[:, h * VDIM:(h + 1) * VDIM]
                do_ref[h] = jnp.concatenate([jnp.zeros((tm, NOPE), F32), do_h], axis=-1).astype(BF16)
            feature = lax.broadcasted_iota(jnp.int32, (G_WIDTH, LANES), 0) // VDIM
            column = lax.broadcasted_iota(jnp.int32, (G_WIDTH, LANES), 1)
            head_sums = jnp.dot(dattn * attn, jnp.where(feature == column, 1.0, 0.0).astype(F32),
                                preferred_element_type=F32, precision=lax.Precision.HIGH)
            dsums_t = jnp.transpose(head_sums)
            for h in range(HEADS):
                drow_ref[h] = dsums_t[h:h + 1, :]
            dza = d_out_a * attn * (sig_a * (1.0 + za * (1.0 - sig_a)))
            dsgu = d_out_b * silu_b
            dzb = d_out_b * sgu * (sig_b * (1.0 + zb * (1.0 - sig_b)))
            du = dsgu * sv * _gelu_grad(u, cdf_u)
            dsv = dsgu * ug
            dsv_b = dsv.astype(BF16)
            for cix in range(n_chunks):
                dbsp_acc[...] += dsv[cix * CHUNK:(cix + 1) * CHUNK, :]
            dsv_wide, vg_wide = side_by_side(dsv_b), side_by_side(vg_b)
            dvg = by_chunk(spatial(wst_ref, dsv_wide))
            for h in range(HEADS):
                mine = jnp.where(own_lanes(h), dsv_wide[h // 2], jnp.zeros_like(dsv_wide[h // 2]))
                dws_ref[h] += _dot_nt(mine, vg_wide[h // 2])
            dsg_ref[...] += jnp.sum(dvg * vhat, axis=0, keepdims=True)
            dsb_ref[...] += jnp.sum(dvg, axis=0, keepdims=True)
            dvhat = dvg * sg_ref[...]
            dgv = rstd_v * (dvhat - jnp.mean(dvhat, axis=-1, keepdims=True)
                            - vhat * jnp.mean(dvhat * vhat, axis=-1, keepdims=True))
            dv = dgv * _gelu_grad(vpre, cdf_v)
            drest_ref[...] = jnp.concatenate([dza, du, dv, dzb], axis=-1).astype(BF16)

        @pl.when(step == 0)
        def _():
            forward(kept_sets[0])

        for parity in (0, 1):
            @pl.when((step > 0) & (step < n_steps) & (step % 2 == parity))
            def _():
                forward(kept_sets[parity])
                backward(kept_sets[1 - parity])

        @pl.when(step == n_steps)
        def _():
            backward(kept_sets[(n_steps - 1) % 2])
            tri = (lax.broadcasted_iota(jnp.int32, (CHUNK, CHUNK), 0)
                   >= lax.broadcasted_iota(jnp.int32, (CHUNK, CHUNK), 1))
            for h in range(HEADS):
                dws_ref[h] = jnp.where(tri, dws_ref[h], 0.0)
            tot = dbsp_acc[...]
            lane = lax.broadcasted_iota(jnp.int32, (CHUNK, LANES), 1)
            dbs = jnp.zeros((CHUNK, LANES), F32)
            for h in range(HEADS):
                head_sum = jnp.sum(tot[:, h * G_HEAD_DIM:(h + 1) * G_HEAD_DIM], axis=-1, keepdims=True)
                dbs = jnp.where(lane == h, head_sum, dbs)
            dbs_ref[...] = dbs

    full = lambda a: pl.BlockSpec(a.shape, lambda i: (0,) * a.ndim)
    this = lambda i: jnp.minimum(i, n_steps - 1)
    prev = lambda i: jnp.maximum(i - 1, 0)
    tile = lambda w, j=0, at=this: pl.BlockSpec((tm, w), lambda i, j=j: (at(i), j))
    heads = lambda at: pl.BlockSpec((HEADS, tm, HEAD_PAD), lambda i: (0, at(i), 0))
    acc = lambda shape: (pl.BlockSpec(shape, lambda i: (0,) * len(shape)), jax.ShapeDtypeStruct(shape, F32))
    accs = [acc((D_MODEL, D_MODEL)), acc((HEADS, CHUNK, CHUNK)), acc((CHUNK, LANES)), acc((1, D_MODEL)),
            acc((1, D_MODEL)), acc((1, G_WIDTH)), acc((1, G_WIDTH)), acc((1, LANES))]
    kept = ([pltpu.VMEM((tm, G_WIDTH), F32)] * 6 + [pltpu.VMEM((tm, D_MODEL), F32)] * 2
            + [pltpu.VMEM((tm, D_MODEL), BF16), pltpu.VMEM((tm, G_WIDTH), BF16)] + [pltpu.VMEM((tm, LANES), F32)] * 2)
    return pl.pallas_call(
        body, name="mid", grid=(n_steps + 1,),
        in_specs=[tile(D_MODEL), tile(D_MODEL), tile(G_WIDTH, 1), tile(G_WIDTH, 2), tile(G_WIDTH, 3), tile(G_WIDTH, 4),
                  heads(this), tile(G_WIDTH, 1, prev), tile(G_WIDTH, 2, prev), tile(G_WIDTH, 3, prev),
                  tile(G_WIDTH, 4, prev), heads(prev),
                  full(w_out), full(ws_low), full(ws_low_t), full(bsp), full(sgu_g), full(sgu_b),
                  full(ln_g), full(ln_b)],
        out_specs=[tile(D_MODEL, 0, prev), heads(prev), pl.BlockSpec((HEADS, 1, tm), lambda i: (0, 0, prev(i))),
                   tile(4 * G_WIDTH, 0, prev)]
        + [a[0] for a in accs],
        out_shape=[jax.ShapeDtypeStruct((t, D_MODEL), F32), jax.ShapeDtypeStruct((HEADS, t, HEAD_PAD), BF16),
                   jax.ShapeDtypeStruct((HEADS, 1, t), F32), jax.ShapeDtypeStruct((t, 4 * G_WIDTH), BF16)]
        + [a[1] for a in accs],
        scratch_shapes=[pltpu.VMEM((CHUNK, G_WIDTH), F32)] + kept + kept,
        compiler_params=_cparams(("arbitrary",)),
    )(x, target, proj, proj, proj, proj, ol, proj, proj, proj, proj, ol,
      w_out, ws_low, ws_low_t, bsp, sgu_g, sgu_b, ln_g, ln_b)


def _attn_bwd(q, k, v, do, lse_row, d_row):
    t = q.shape[1]
    bk, bq = ATTN_BWD_WIDE, ATTN_NARROW
    n_diag = bk // bq
    half = bq // 2
    last = t // bq - 1
    chunk = SOFTMAX_ROWS

    def body(q_ref, k_ref, v_ref, do_ref, lse_ref, drow_ref, dqt_ref, dk_ref, dv_ref,
             s0, s1, e0, e1, p0, p1, g0, g1, kt_scr):
        j = pl.program_id(1)
        at = lambda i: pl.ds(pl.multiple_of(i * bq, bq), bq)

        @pl.when(j == 0)
        def _():
            dqt_ref[...] = jnp.zeros_like(dqt_ref)

        kt_scr[...] = jnp.transpose(k_ref[0].astype(F32)).astype(BF16)
        dk_ref[...] = jnp.zeros_like(dk_ref)
        dv_ref[...] = jnp.zeros_like(dv_ref)

        whole_tile = ((slice(0, bk), slice(0, bq)),)

        def queries(i, lanes):
            return pl.ds(pl.multiple_of(i * bq + lanes.start, half), lanes.stop - lanes.start)

        def products(i, s_out, e_out, areas=whole_tile):
            i = jnp.minimum(i, last)
            for keys, lanes in areas:
                s_out[keys, lanes] = _dot_nt(k_ref[0, keys, :], q_ref[0, queries(i, lanes), :])
                e_out[keys, lanes] = _dot_nt(v_ref[0, keys, :], do_ref[0, queries(i, lanes), :])

        def gradients(i, p_in, g_in, areas=whole_tile):
            for keys, lanes in areas:
                dv_ref[0, keys, :] += _dot(p_in[keys, lanes], do_ref[0, queries(i, lanes), :])
                dk_ref[0, keys, :] += _dot(g_in[keys, lanes], q_ref[0, queries(i, lanes), :])
                dqt_ref[0, :, queries(i, lanes)] += _dot(kt_scr[:, keys], g_in[keys, lanes])

        def elementwise(i, s_in, e_in, p_out, g_out, qry0=None, areas=whole_tile):
            for keys, lanes in areas:
                width = lanes.stop - lanes.start
                step = chunk if qry0 is None else half
                lse = lse_ref[0, :, queries(i, lanes)]
                dsum = drow_ref[0, :, queries(i, lanes)]
                for r in range(keys.start, keys.stop, step):
                    p = jnp.exp2(s_in[r:r + step, lanes] - lse)
                    if qry0 is not None:
                        key = lax.broadcasted_iota(jnp.int32, (step, width), 0) + r
                        qry = lax.broadcasted_iota(jnp.int32, (step, width), 1) + (qry0 + lanes.start)
                        p = jnp.where(qry >= key, p, 0.0)
                    p_out[r:r + step, lanes] = p.astype(BF16)
                    g_out[r:r + step, lanes] = (p * (e_in[r:r + step, lanes] - dsum)).astype(BF16)

        def one_pass(i, s_in, e_in, s_out, e_out, p_prev, g_prev, p_cur, g_cur):
            products(i + 1, s_out, e_out)
            gradients(i - 1, p_prev, g_prev)
            elementwise(i, s_in, e_in, p_cur, g_cur)

        first = n_diag * j

        def areas_of(u):
            if u >= n_diag:
                return whole_tile
            return ((slice(0, u * bq + half), slice(0, bq)), (slice(u * bq + half, (u + 1) * bq), slice(half, bq)))

        even, odd = (s0, e0, p0, g0), (s1, e1, p1, g1)
        products(first, s0, e0, areas_of(0))
        products(first + 1, s1, e1, areas_of(1))
        elementwise(first, s0, e0, p0, g0, qry0=0, areas=areas_of(0))
        for u in range(1, n_diag):
            (s_in, e_in, p_cur, g_cur), (s_out, e_out, p_prev, g_prev) = (odd, even) if u % 2 else (even, odd)
            products(first + u + 1, s_out, e_out, areas_of(u + 1))
            gradients(first + u - 1, p_prev, g_prev, areas_of(u - 1))
            elementwise(first + u, s_in, e_in, p_cur, g_cur, qry0=u * bq, areas=areas_of(u))
        corner = (slice(bk - half, bk), slice(0, half))
        p1[corner] = jnp.zeros((half, half), BF16)
        g1[corner] = jnp.zeros((half, half), BF16)

        def two_passes(n, _):
            i = first + n_diag + 2 * n
            one_pass(i, s0, e0, s1, e1, p1, g1, p0, g0)
            one_pass(i + 1, s1, e1, s0, e0, p0, g0, p1, g1)
            return 0

        lax.fori_loop(0, (last - first - n_diag + 1) // 2, two_passes, 0)
        gradients(last, p1, g1)
        dk_ref[0] = dk_ref[0] * LN2

    whole = pl.BlockSpec((1, t, HEAD_PAD), lambda h, j: (h, 0, 0))
    block = pl.BlockSpec((1, bk, HEAD_PAD), lambda h, j: (h, j, 0))
    rows = pl.BlockSpec((1, 1, t), lambda h, j: (h, 0, 0), pipeline_mode=pl.Buffered(1))
    shape = jax.ShapeDtypeStruct((HEADS, t, HEAD_PAD), F32)
    tile = lambda dtype: pltpu.VMEM((bk, bq), dtype)
    return pl.pallas_call(
        body, name="attn_bwd", grid=(HEADS, t // bk),
        in_specs=[whole, block, block, whole, rows, rows],
        out_specs=[pl.BlockSpec((1, HEAD_PAD, t), lambda h, j: (h, 0, 0)), block, block],
        out_shape=[jax.ShapeDtypeStruct((HEADS, HEAD_PAD, t), F32), shape, shape],
        scratch_shapes=[tile(F32), tile(F32), tile(F32), tile(F32), tile(BF16), tile(BF16),
                        tile(BF16), tile(BF16), pltpu.VMEM((HEAD_PAD, bk), BF16)],
        compiler_params=_cparams(("arbitrary", "arbitrary"), vmem_limit=ATTN_BWD_VMEM_LIMIT),
    )(q, k, v, do, lse_row, d_row)


def _bwd_tail(dq, dk, dv, proj, pos_row, invf_col, w_heads, q_g, kv_g, x, dr, drest, wp_in):
    t = proj.shape[0]
    tm = PROJ_TILE
    n_head = 4 * LANES

    def body(dq_ref, dk_ref, dv_ref, ph_ref, pos_ref, invf_ref, wh_ref, qg_ref, kvg_ref,
             x_ref, dr_ref, drest_ref, win_ref,
             gx_ref, dwin_ref, dwh_ref, dqg_ref, dkvg_ref):
        @pl.when(pl.program_id(0) == 0)
        def _():
            dwin_ref[...] = jnp.zeros_like(dwin_ref)
            dwh_ref[...] = jnp.zeros_like(dwh_ref)
            dqg_ref[...] = jnp.zeros_like(dqg_ref)
            dkvg_ref[...] = jnp.zeros_like(dkvg_ref)

        xb = x_ref[...].astype(BF16)
        dr_b = drest_ref[...]
        dwin_ref[:, n_head:] += _dot_tn(xb, dr_b)
        gx_rest = DN_ALPHA * dr_ref[...] + _dot_nt(dr_b, win_ref[:, n_head:])

        cos, s1, s2 = _rope_tables(pos_ref[...], invf_ref[...])
        lane = lax.broadcasted_iota(jnp.int32, (tm, LANES), 1)
        c_q = ph_ref[:, :Q_LORA]
        c_kv = ph_ref[:, Q_LORA:Q_LORA + KV_LORA]
        rstd_q = lax.rsqrt(jnp.mean(c_q * c_q, axis=-1, keepdims=True) + EPS)
        rstd_kv = lax.rsqrt(jnp.mean(c_kv * c_kv, axis=-1, keepdims=True) + EPS)
        qhat = c_q * rstd_q
        kvhat = c_kv * rstd_kv
        cqn = (qhat * qg_ref[...]).astype(BF16)
        ckvn = (kvhat * kvg_ref[...]).astype(BF16)
        dkr_rot = jnp.zeros((tm, LANES), F32)
        dq_heads, dkv_heads = [], []
        for h in range(HEADS):
            dq_heads.append(_rope(jnp.transpose(dq_ref[h]) * ATTN_SCALE, cos, s1, s2, -1.0).astype(BF16))
            dk_h = dk_ref[h]
            dkv_heads.append(jnp.where(lane < NOPE, dk_h, dv_ref[h]).astype(BF16))
            dkr_rot = dkr_rot + dk_h
        dq_all = jnp.concatenate(dq_heads, axis=1)
        dkv_all = jnp.concatenate(dkv_heads, axis=1)
        dwq_all = _dot_tn(cqn, dq_all)
        dwkv_all = _dot_tn(ckvn, dkv_all)
        for h in range(HEADS):
            dwh_ref[h, :Q_LORA, :] += dwq_all[:, h * HEAD_PAD:(h + 1) * HEAD_PAD]
            dwh_ref[h, Q_LORA:, :] += dwkv_all[:, h * HEAD_PAD:(h + 1) * HEAD_PAD]
        dcqn = _dot_nt(dq_all, jnp.concatenate([wh_ref[h, :Q_LORA, :] for h in range(HEADS)], axis=1))
        dckvn = _dot_nt(dkv_all, jnp.concatenate([wh_ref[h, Q_LORA:, :] for h in range(HEADS)], axis=1))
        rot_lanes = (lane >= KR_LO) & (lane < KR_LO + ROPE)
        dkr_raw = jnp.where(rot_lanes, _rope(dkr_rot, cos, s1, s2, -1.0), 0.0)
        dqg_ref[...] += jnp.sum(dcqn * qhat, axis=0, keepdims=True)
        dkvg_ref[...] += jnp.sum(dckvn * kvhat, axis=0, keepdims=True)
        dqh = dcqn * qg_ref[...]
        dkvh = dckvn * kvg_ref[...]
        dc_q = rstd_q * (dqh - qhat * jnp.mean(dqh * qhat, axis=-1, keepdims=True))
        dc_kv = rstd_kv * (dkvh - kvhat * jnp.mean(dkvh * kvhat, axis=-1, keepdims=True))
        dh_b = jnp.concatenate([dc_q, dc_kv, dkr_raw], axis=-1).astype(BF16)
        dwin_ref[:, :n_head] += _dot_tn(xb, dh_b)
        gx_ref[...] = gx_rest + _dot_nt(dh_b, win_ref[:, :n_head])

    full = lambda a: pl.BlockSpec(a.shape, lambda i: (0,) * a.ndim)
    tile = lambda w: pl.BlockSpec((tm, w), lambda i: (i, 0))
    heads = pl.BlockSpec((HEADS, tm, HEAD_PAD), lambda i: (0, i, 0))
    acc = lambda shape: (pl.BlockSpec(shape, lambda i: (0,) * len(shape)), jax.ShapeDtypeStruct(shape, F32))
    accs = [acc(wp_in.shape), acc(w_heads.shape), acc((1, Q_LORA)), acc((1, KV_LORA))]
    return pl.pallas_call(
        body, name="bwd_tail", grid=(t // tm,),
        in_specs=[pl.BlockSpec((HEADS, HEAD_PAD, tm), lambda i: (0, 0, i)), heads, heads, tile(n_head),
                  pl.BlockSpec((1, tm), lambda i: (0, i)), full(invf_col), full(w_heads), full(q_g), full(kv_g),
                  tile(D_MODEL), tile(D_MODEL), tile(drest.shape[1]), full(wp_in)],
        out_specs=[tile(D_MODEL)] + [a[0] for a in accs],
        out_shape=[jax.ShapeDtypeStruct((t, D_MODEL), F32)] + [a[1] for a in accs],
        compiler_params=_cparams(("arbitrary",), vmem_limit=BWD_TAIL_VMEM_LIMIT),
    )(dq, dk, dv, proj, pos_row, invf_col, w_heads, q_g, kv_g, x, dr, drest, wp_in)


def _adam_update(g, w, m, v):
    m_new = ADAM_B1 * m + (1.0 - ADAM_B1) * g
    v_new = ADAM_B2 * v + (1.0 - ADAM_B2) * (g * g)
    m_hat = m_new / (1.0 - ADAM_B1 ** ADAM_STEP)
    v_hat = v_new / (1.0 - ADAM_B2 ** ADAM_STEP)
    return -ADAM_LR * (m_hat / (jnp.sqrt(v_hat) + ADAM_EPS) + ADAM_WD * w), m_new, v_new


def _adam(parts, w, m, v, *, name, tile_rows, transposed=False):
    n, rows, cols = parts.shape
    lane_pad = -(-cols // LANES) * LANES
    own_rows = rows if transposed else w.shape[0]
    assert own_rows == rows or tile_rows == rows

    def body(p_ref, w_ref, m_ref, v_ref, g_ref, d_ref, nm_ref, nv_ref, *scratch):
        g = p_ref[0].astype(F32)
        for s in range(1, n):
            g = g + p_ref[s].astype(F32)
        if transposed:
            wide_ref, = scratch
            wide_ref[:, lane_pad - LANES:] = jnp.zeros((tile_rows, LANES), F32)
            wide_ref[:, :cols] = g
            g = jnp.transpose(wide_ref[...])[:cols]
        g_ref[...] = g
        d_ref[...], nm_ref[...], nv_ref[...] = _adam_update(g[:w_ref.shape[0]], w_ref[...], m_ref[...], v_ref[...])

    if transposed:
        flat = grad = pl.BlockSpec((cols, tile_rows), lambda i: (0, i))
        shape = grad_shape = jax.ShapeDtypeStruct((cols, rows), F32)
        scratch = [pltpu.VMEM((tile_rows, lane_pad), F32)]
    else:
        own_tile = min(tile_rows, own_rows)
        flat = pl.BlockSpec((own_tile, cols), lambda i: (i, 0))
        grad = pl.BlockSpec((tile_rows, cols), lambda i: (i, 0))
        shape, grad_shape = jax.ShapeDtypeStruct((own_rows, cols), F32), jax.ShapeDtypeStruct((rows, cols), F32)
        scratch = []
    return pl.pallas_call(
        body, name=name, grid=(rows // tile_rows,),
        in_specs=[pl.BlockSpec((n, tile_rows, cols), lambda i: (0, i, 0)), flat, flat, flat],
        out_specs=[grad, flat, flat, flat], out_shape=[grad_shape, shape, shape, shape], scratch_shapes=scratch,
        compiler_params=_cparams(("arbitrary",)),
    )(parts, w, m, v)


def _adam_replicated(rep_g, ws, ms, vs):
    count = len(ws)
    small_rows = REP_ROWS - CHUNK

    def body(g_ref, *refs):
        w_refs, m_refs, v_refs = refs[:count], refs[count:2 * count], refs[2 * count:3 * count]
        outs, last_ref, slab_ref = refs[3 * count:7 * count], refs[7 * count], refs[7 * count + 1]
        for d in range(N_DEV):
            slab_ref[d * small_rows:(d + 1) * small_rows, :] = g_ref[d, CHUNK:, :]
        last_ref[...] = slab_ref[N_DEV * small_rows - 1:, LANES - 1:]
        at = 0
        for k, w_ref in enumerate(w_refs):
            if w_ref.ndim == 3:
                g = g_ref[:, :CHUNK, :]
            else:
                n_rows = w_ref.size // LANES
                g = slab_ref[at:at + n_rows, :].reshape(w_ref.shape)
                at += n_rows
            delta, m_new, v_new = _adam_update(g, w_ref[...], m_refs[k][...], v_refs[k][...])
            for which, val in enumerate((g, delta, m_new, v_new)):
                outs[which * count + k][...] = val

    shapes = [jax.ShapeDtypeStruct(w.shape, F32) for w in ws]
    res = pl.pallas_call(
        body, name="adam_rep", out_shape=shapes * 4 + [jax.ShapeDtypeStruct((1, 1), F32)],
        scratch_shapes=[pltpu.VMEM((N_DEV * small_rows, LANES), F32)],
        compiler_params=_cparams(),
    )(rep_g, *ws, *ms, *vs)
    return [res[which * count:(which + 1) * count] for which in range(4)], res[4 * count]


def _pack_small(vals, last):
    flat = jnp.concatenate([v.reshape(-1) for v in vals])
    pad = SMALL_LEN - flat.shape[0]
    return jnp.concatenate([flat, jnp.zeros((pad - 1,), F32), last.reshape(1)])


UQ_SHARD = HEADS * (NOPE + ROPE) // N_DEV
HEAD_ROWS = Q_LORA + KV_LORA
MIXED_ROWS = HEAD_ROWS + CHUNK + SMALL_LEN // N_DEV // LANES


def _head_slab(w_uq_shard, w_ukv_shard):
    return jnp.concatenate([jnp.pad(w_uq_shard, ((0, 0), (0, LANES - UQ_SHARD))), w_ukv_shard])


IN_SHARD = D_IN // N_DEV


def _w_in_pieces():
    split = Q_LORA + KV_LORA
    moves = ((0, split, 0), (split, split + ROPE, KR_LO), (split + ROPE, D_IN, LANES - ROPE))
    pieces = []
    for s in range(N_DEV):
        lo, hi = s * IN_SHARD, (s + 1) * IN_SHARD
        for a, b, shift in moves:
            a, b = max(a, lo), min(b, hi)
            if a < b:
                pieces.append((s, a - lo, a + shift, b - a))
    return pieces


def _w_in_shards(dwp_in):
    tr = TOKEN_TILE
    by_shard = [[p for p in _w_in_pieces() if p[0] == s] for s in range(N_DEV)]

    def body(w_ref, o_ref):
        for s, pieces in enumerate(by_shard):
            parts = [w_ref[:, dst:dst + width] for _, _, dst, width in pieces]
            o_ref[s] = parts[0] if len(parts) == 1 else jnp.concatenate(parts, axis=1)

    return pl.pallas_call(
        body, name="w_in_split", grid=(D_MODEL // tr,),
        in_specs=[pl.BlockSpec((tr, D_IN_PAD), lambda i: (i, 0))],
        out_specs=pl.BlockSpec((N_DEV, tr, IN_SHARD), lambda i: (0, i, 0)),
        out_shape=jax.ShapeDtypeStruct((N_DEV, D_MODEL, IN_SHARD), dwp_in.dtype),
        compiler_params=_cparams(("arbitrary",)),
    )(dwp_in)


def kernel(x, positions, w_in, q_norm_g, w_uq, kv_norm_g, w_ukv, sgu_norm_g, sgu_norm_b, w_spatial, b_spatial, w_out, ln_g, ln_b, loss_target, m_w_in, m_q_norm_g, m_w_uq, m_kv_norm_g, m_w_ukv, m_sgu_norm_g, m_sgu_norm_b, m_w_spatial, m_b_spatial, m_w_out, m_ln_g, m_ln_b, v_w_in, v_q_norm_g, v_w_uq, v_kv_norm_g, v_w_ukv, v_sgu_norm_g, v_sgu_norm_b, v_w_spatial, v_b_spatial, v_w_out, v_ln_g, v_ln_b):
    seq = x.shape[1]
    x2 = x.reshape(seq, D_MODEL)
    tgt2 = loss_target.reshape(seq, D_MODEL)
    pos_row = positions.reshape(1, seq)

    w_in_shards, w_out_shards, w_heads = _gather_two_level(
        [w_in.astype(BF16), w_out.astype(BF16), _head_slab(w_uq, w_ukv).astype(BF16)],
        name="wgather")
    (loss_part, grad_x, d_in, d_heads, d_out, d_ws, d_bs_t, d_lng, d_lnb, d_sgug, d_sgub, d_qg, d_kvg) = _local_step(
        x2, tgt2, pos_row, w_in_shards, w_heads, w_out_shards.reshape(D_MODEL, D_MODEL), q_norm_g, kv_norm_g,
        sgu_norm_g, sgu_norm_b, w_spatial, b_spatial, ln_g, ln_b)

    small_part = _pack_small([d_qg, d_kvg, d_sgug, d_sgub, d_bs_t[:, :HEADS].T, d_lng, d_lnb], last=loss_part[0, :1])
    mixed = jnp.concatenate([d_heads, d_ws, small_part.reshape(N_DEV, -1, LANES)], axis=1)
    by_chip = [g.reshape((N_CHIPS, 2) + g.shape[1:])
               for g in (d_in, d_out.reshape(N_DEV, D_MODEL // N_DEV, D_MODEL), mixed)]
    from_sibling = _sibling_swap(by_chip, name="gswap")
    core = lax.axis_index("c").astype(jnp.int32).reshape(1)
    pair_sums = [_pair_sum(a, b, core, name=nm, out_dtype=dt) for a, b, nm, dt in zip(
        by_chip, from_sibling, ("gsum_in", "gsum_out", "gsum_mixed"), (BF16, BF16, F32))]
    recv_in, recv_out, recv_mixed = _chip_exchange(pair_sums, name="gexch")

    res_in = [a.T for a in _adam(recv_in, w_in.T, m_w_in.T, v_w_in.T, name="adam_in", tile_rows=PROJ_TILE,
                                 transposed=True)]
    res_out = _adam(recv_out, w_out, m_w_out, v_w_out, name="adam_out", tile_rows=D_MODEL // N_DEV)
    res_mixed = _adam(recv_mixed, _head_slab(w_uq, w_ukv), _head_slab(m_w_uq, m_w_ukv), _head_slab(v_w_uq, v_w_ukv),
                      name="adam_mixed", tile_rows=MIXED_ROWS)

    rep_g, = _gather_direct([res_mixed[0]], name="sgather", first_row=HEAD_ROWS)
    res_rep, loss = _adam_replicated(
        rep_g,
        [q_norm_g, kv_norm_g, sgu_norm_g, sgu_norm_b, w_spatial, b_spatial, ln_g, ln_b],
        [m_q_norm_g, m_kv_norm_g, m_sgu_norm_g, m_sgu_norm_b, m_w_spatial, m_b_spatial, m_ln_g, m_ln_b],
        [v_q_norm_g, v_kv_norm_g, v_sgu_norm_g, v_sgu_norm_b, v_w_spatial, v_b_spatial, v_ln_g, v_ln_b])

    def ordered(which):
        r_qg, r_kvg, r_sg, r_sb, r_ws, r_bs, r_lg, r_lb = res_rep[which]
        heads = res_mixed[which]
        return [res_in[which], r_qg, heads[:Q_LORA, :UQ_SHARD], r_kvg, heads[Q_LORA:HEAD_ROWS], r_sg, r_sb, r_ws, r_bs,
                res_out[which], r_lg, r_lb]

    outs = [loss.reshape(()), grad_x.reshape(x.shape)]
    for which in range(4):
        outs += ordered(which)
    return tuple(outs)


def _local_step(x2, tgt2, pos_row, w_in_shards, w_heads, w_out_full, q_norm_g, kv_norm_g, sgu_norm_g, sgu_norm_b,
                w_spatial, b_spatial, ln_g, ln_b):
    half = jnp.arange(HALF, dtype=F32)
    invf_col = (1.0 / (ROPE_THETA ** (half / HALF))).reshape(HALF, 1)
    tri = jnp.tril(jnp.ones((CHUNK, CHUNK), dtype=bool))
    ws_low = jnp.where(tri[None], w_spatial, 0.0).astype(BF16)
    ws_low_t = ws_low.transpose(0, 2, 1)
    bsp = jnp.repeat(b_spatial.T, G_HEAD_DIM, axis=1)
    row = lambda a: a.reshape(1, -1)

    proj, q, k, v, vt, wp_in = _fwd_proj(x2, pos_row, invf_col, w_in_shards, w_heads, row(q_norm_g), row(kv_norm_g))
    o, lse_row = _attn_fwd(q, k, vt)
    (dr, do, d_row, drest, d_out, d_ws, d_bs_t, d_lng, d_lnb, d_sgug, d_sgub, loss_part) = _mid(
        x2, tgt2, proj, o, w_out_full, ws_low, ws_low_t, bsp, row(sgu_norm_g), row(sgu_norm_b), row(ln_g), row(ln_b))
    dqt, dk, dv = _attn_bwd(q, k, v, do, lse_row, d_row)
    grad_x, dwp_in, d_heads, d_qg, d_kvg = _bwd_tail(dqt, dk, dv, proj, pos_row, invf_col, w_heads, row(q_norm_g),
                                                      row(kv_norm_g), x2, dr, drest, wp_in)
    return (loss_part, grad_x, _w_in_shards(dwp_in), d_heads, d_out, d_ws, d_bs_t, d_lng, d_lnb, d_sgug, d_sgub,
            d_qg, d_kvg)
```

```python
import math

import jax
import jax.numpy as jnp
from jax import lax
from jax.experimental import pallas as pl
from jax.experimental.pallas import tpu as pltpu

F32 = jnp.float32
BF16 = jnp.bfloat16

N_DEV = 8
D_MODEL = 1024
HEADS = 8
NOPE = 64
ROPE = 32
HALF = ROPE // 2
VDIM = 64
Q_LORA = 256
KV_LORA = 128
G_WIDTH = 512
G_HEAD_DIM = 64
CHUNK = 128
HEAD_PAD = 128
D_IN = 2464
D_IN_PAD = 2560
KR_LO = NOPE
SUM_ROW = NOPE - 1
LIVE_ROWS = slice(NOPE - 16, HEAD_PAD)
ROPE_THETA = 10000.0
DN_ALPHA = 2.0 ** 0.25
EPS = 1e-5
ATTN_SCALE = 1.0 / math.sqrt(NOPE + ROPE)
ADAM_LR, ADAM_B1, ADAM_B2, ADAM_EPS, ADAM_WD, ADAM_STEP = 0.001, 0.9, 0.999, 1e-08, 0.01, 10

LANES = 128
REP_ROWS = 136
SMALL_LEN = 8192
VMEM_LIMIT = 56 * 1024 * 1024
ATTN_BWD_VMEM_LIMIT = 61 * 1024 * 1024
BWD_TAIL_VMEM_LIMIT = 61 * 1024 * 1024

TOKEN_TILE = 256
PROJ_TILE = 512
ATTN_FWD_WIDE = 2048
ATTN_BWD_WIDE = 2048
ATTN_NARROW = 512
SOFTMAX_ROWS = 512
LOG2E = 1.4426950408889634
LN2 = 0.6931471805599453
Q_PRESCALE = ATTN_SCALE * LOG2E


def _cparams(sem=None, vmem_limit=VMEM_LIMIT):
    return pltpu.CompilerParams(dimension_semantics=sem, vmem_limit_bytes=vmem_limit)


def _dot(a, b):
    return jnp.dot(a, b, preferred_element_type=F32)


def _dot_nt(a, b):
    return lax.dot_general(a, b, (((1,), (1,)), ((), ())), preferred_element_type=F32)


def _dot_tn(a, b):
    return lax.dot_general(a, b, (((0,), (0,)), ((), ())), preferred_element_type=F32)


def _sigmoid(z):
    return 1.0 / (1.0 + jnp.exp(-z))


def _normal_cdf(x):
    return 0.5 * (1.0 + lax.erf(x * 0.7071067811865476))


def _gelu_grad(x, cdf):
    return cdf + x * jnp.exp(-0.5 * x * x) * 0.3989422804014327


def _gather_direct(srcs, *, name, first_row=0):
    n = len(srcs)
    shapes = [(s.shape[0] - first_row,) + s.shape[1:] for s in srcs]

    def body(*refs):
        src_refs, out_refs = [r.at[pl.ds(first_row, shape[0])] for r, shape in zip(refs[:n], shapes)], refs[n:2 * n]
        send_sems, recv_sems, local_sems = refs[2 * n:]
        x, y, c = lax.axis_index("x"), lax.axis_index("y"), lax.axis_index("c")
        me = 4 * x + 2 * y + c
        mine = [pltpu.make_async_copy(src_refs[t], out_refs[t].at[me], local_sems.at[t]) for t in range(n)]
        for cp in mine:
            cp.start()
        sends, arrivals = [], []
        for k in (6, 7, 4, 5, 2, 3, 1):
            px = 1 - x if k & 4 else x
            py = 1 - y if k & 2 else y
            pc = 1 - c if k & 1 else c
            peer = 4 * px + 2 * py + pc
            for t in range(n):
                sem = (k - 1) * n + t
                cp = pltpu.make_async_remote_copy(
                    src_ref=src_refs[t], dst_ref=out_refs[t].at[me],
                    send_sem=send_sems.at[sem], recv_sem=recv_sems.at[sem],
                    device_id=(px, py, pc), device_id_type=pl.DeviceIdType.MESH)
                cp.start()
                sends.append(cp)
                arrivals.append(pltpu.make_async_remote_copy(
                    src_ref=src_refs[t], dst_ref=out_refs[t].at[peer],
                    send_sem=send_sems.at[sem], recv_sem=recv_sems.at[sem],
                    device_id=(x, y, c), device_id_type=pl.DeviceIdType.MESH))
        for cp in arrivals:
            cp.wait_recv()
        for cp in sends:
            cp.wait_send()
        for cp in mine:
            cp.wait()

    hbm = pl.BlockSpec(memory_space=pl.ANY)
    return pl.pallas_call(
        body, name=name,
        out_shape=[jax.ShapeDtypeStruct((N_DEV,) + shape, s.dtype) for shape, s in zip(shapes, srcs)],
        in_specs=[hbm] * n, out_specs=[hbm] * n,
        scratch_shapes=[pltpu.SemaphoreType.DMA(((N_DEV - 1) * n,)), pltpu.SemaphoreType.DMA(((N_DEV - 1) * n,)),
                        pltpu.SemaphoreType.DMA((n,))],
    )(*srcs)


def _gather_two_level(srcs, *, name):
    n = len(srcs)

    def body(*refs):
        src_refs, out_refs = refs[:n], refs[n:2 * n]
        send_sems, recv_sems, local_sems = refs[2 * n:]
        x, y, c = lax.axis_index("x"), lax.axis_index("y"), lax.axis_index("c")
        me, sibling = (x, y, c), (x, y, 1 - c)
        chips = [(1 - x, 1 - y), (1 - x, y), (x, 1 - y)]
        index = lambda px, py, pc: 4 * px + 2 * py + pc

        def copy(k, t, block, to, src=None):
            place = out_refs[t].at[index(*block)]
            return pltpu.make_async_remote_copy(
                src_ref=place if src is None else src, dst_ref=place,
                send_sem=send_sems.at[k * n + t], recv_sem=recv_sems.at[k * n + t],
                device_id=to, device_id_type=pl.DeviceIdType.MESH)

        mine = [pltpu.make_async_copy(src_refs[t], out_refs[t].at[index(*me)], local_sems.at[t]) for t in range(n)]
        for cp in mine:
            cp.start()
        first = [copy(1 + j, t, me, (*chip, c), src=src_refs[t]) for j, chip in enumerate(chips) for t in range(n)]
        first += [copy(0, t, me, sibling, src=src_refs[t]) for t in range(n)]
        for cp in first:
            cp.start()
        passed = []
        for j, chip in enumerate(chips):
            for t in range(n):
                copy(1 + j, t, (*chip, c), me).wait_recv()
                cp = copy(4 + j, t, (*chip, c), sibling)
                cp.start()
                passed.append(cp)
        for t in range(n):
            copy(0, t, sibling, me).wait_recv()
        for j, chip in enumerate(chips):
            for t in range(n):
                copy(4 + j, t, (*chip, 1 - c), me).wait_recv()
        for cp in first + passed:
            cp.wait_send()
        for cp in mine:
            cp.wait()

    hbm = pl.BlockSpec(memory_space=pl.ANY)
    return pl.pallas_call(
        body, name=name,
        out_shape=[jax.ShapeDtypeStruct((N_DEV,) + s.shape, s.dtype) for s in srcs],
        in_specs=[hbm] * n, out_specs=[hbm] * n,
        scratch_shapes=[pltpu.SemaphoreType.DMA((7 * n,)), pltpu.SemaphoreType.DMA((7 * n,)),
                        pltpu.SemaphoreType.DMA((n,))],
    )(*srcs)


N_CHIPS = N_DEV // 2


def _sibling_swap(srcs, *, name):
    n = len(srcs)

    def body(*refs):
        src_refs, out_refs = refs[:n], refs[n:2 * n]
        send_sems, recv_sems = refs[2 * n:]
        x, y, c = lax.axis_index("x"), lax.axis_index("y"), lax.axis_index("c")
        sends = []
        for chip in range(N_CHIPS):
            for t in range(n):
                cp = pltpu.make_async_remote_copy(
                    src_ref=src_refs[t].at[chip, 1 - c], dst_ref=out_refs[t].at[chip],
                    send_sem=send_sems.at[chip * n + t], recv_sem=recv_sems.at[chip * n + t],
                    device_id=(x, y, 1 - c), device_id_type=pl.DeviceIdType.MESH)
                cp.start()
                sends.append(cp)
        for cp in sends:
            cp.wait_recv()
        for cp in sends:
            cp.wait_send()

    hbm = pl.BlockSpec(memory_space=pl.ANY)
    return pl.pallas_call(
        body, name=name,
        out_shape=[jax.ShapeDtypeStruct((N_CHIPS,) + s.shape[2:], s.dtype) for s in srcs],
        in_specs=[hbm] * n, out_specs=[hbm] * n,
        scratch_shapes=[pltpu.SemaphoreType.DMA((N_CHIPS * n,)), pltpu.SemaphoreType.DMA((N_CHIPS * n,))],
    )(*srcs)


def _pair_sum(mine, theirs, core, *, name, out_dtype):
    _, _, rows, cols = mine.shape

    def body(core_ref, a_ref, b_ref, o_ref):
        o_ref[...] = (a_ref[0] + b_ref[...]).astype(out_dtype)

    return pl.pallas_call(
        body, name=name,
        grid_spec=pltpu.PrefetchScalarGridSpec(
            num_scalar_prefetch=1, grid=(N_CHIPS,),
            in_specs=[pl.BlockSpec((1, 1, rows, cols), lambda q, core_ref: (q, core_ref[0], 0, 0)),
                      pl.BlockSpec((1, rows, cols), lambda q, core_ref: (q, 0, 0))],
            out_specs=pl.BlockSpec((1, rows, cols), lambda q, core_ref: (q, 0, 0))),
        out_shape=jax.ShapeDtypeStruct((N_CHIPS, rows, cols), out_dtype),
        compiler_params=_cparams(("arbitrary",)),
    )(core, mine, theirs)


def _chip_exchange(srcs, *, name):
    n = len(srcs)

    def body(*refs):
        src_refs, out_refs = refs[:n], refs[n:2 * n]
        send_sems, recv_sems, local_sems = refs[2 * n:]
        x, y, c = lax.axis_index("x"), lax.axis_index("y"), lax.axis_index("c")
        my_chip = 2 * x + y
        mine = [pltpu.make_async_copy(src_refs[t].at[my_chip], out_refs[t].at[my_chip], local_sems.at[t])
                for t in range(n)]
        for cp in mine:
            cp.start()
        sends, arrivals = [], []
        for k in (3, 2, 1):
            px = 1 - x if k & 2 else x
            py = 1 - y if k & 1 else y
            peer_chip = 2 * px + py
            for t in range(n):
                sem = (k - 1) * n + t
                cp = pltpu.make_async_remote_copy(
                    src_ref=src_refs[t].at[peer_chip], dst_ref=out_refs[t].at[my_chip],
                    send_sem=send_sems.at[sem], recv_sem=recv_sems.at[sem],
                    device_id=(px, py, c), device_id_type=pl.DeviceIdType.MESH)
                cp.start()
                sends.append(cp)
                arrivals.append(pltpu.make_async_remote_copy(
                    src_ref=src_refs[t].at[peer_chip], dst_ref=out_refs[t].at[peer_chip],
                    send_sem=send_sems.at[sem], recv_sem=recv_sems.at[sem],
                    device_id=(x, y, c), device_id_type=pl.DeviceIdType.MESH))
        for cp in arrivals:
            cp.wait_recv()
        for cp in sends:
            cp.wait_send()
        for cp in mine:
            cp.wait()

    hbm = pl.BlockSpec(memory_space=pl.ANY)
    return pl.pallas_call(
        body, name=name,
        out_shape=[jax.ShapeDtypeStruct(s.shape, s.dtype) for s in srcs],
        in_specs=[hbm] * n, out_specs=[hbm] * n,
        scratch_shapes=[pltpu.SemaphoreType.DMA((3 * n,)), pltpu.SemaphoreType.DMA((3 * n,)),
                        pltpu.SemaphoreType.DMA((n,))],
    )(*srcs)


def _rope_tables(pos_row, invf_col):
    tm = pos_row.shape[1]
    ang = pos_row.astype(F32) * invf_col
    cos, sin = jnp.cos(ang), jnp.sin(ang)
    ones = lambda n: jnp.ones((n, tm), F32)
    zeros = lambda n: jnp.zeros((n, tm), F32)
    cos_t = jnp.concatenate([ones(KR_LO), cos, cos, ones(LANES - KR_LO - ROPE)], axis=0)
    first_t = jnp.concatenate([zeros(KR_LO), sin, zeros(LANES - KR_LO - HALF)], axis=0)
    second_t = jnp.concatenate([zeros(KR_LO + HALF), sin, zeros(LANES - KR_LO - ROPE)], axis=0)
    return jnp.transpose(cos_t), jnp.transpose(first_t), jnp.transpose(second_t)


def _rope(t, cos, sin_first, sin_second, sign):
    up = pltpu.roll(t, LANES - HALF, 1)
    down = pltpu.roll(t, HALF, 1)
    return t * cos - sign * (up * sin_first) + sign * (down * sin_second)


def _fwd_proj(x, pos_row, invf_col, w_in_shards, w_heads, q_g, kv_g):
    t = x.shape[0]
    tm = PROJ_TILE

    def body(x_ref, pos_ref, invf_ref, sh_ref, wh_ref, qg_ref, kvg_ref,
             proj_ref, q_ref, k_ref, v_ref, vt_ref, win_ref):
        @pl.when(pl.program_id(0) == 0)
        def _():
            win_ref[...] = jnp.zeros_like(win_ref)
            for s, src, dst, width in _w_in_pieces():
                win_ref[:, dst:dst + width] = sh_ref[s, :, src:src + width]

        proj = _dot(x_ref[...].astype(BF16), win_ref[...])
        proj_ref[...] = proj
        c_q = proj[:, :Q_LORA]
        c_kv = proj[:, Q_LORA:Q_LORA + KV_LORA]
        kr_raw = proj[:, Q_LORA + KV_LORA:Q_LORA + KV_LORA + LANES]
        cqn = (c_q * lax.rsqrt(jnp.mean(c_q * c_q, axis=-1, keepdims=True) + EPS) * qg_ref[...]).astype(BF16)
        ckvn = (c_kv * lax.rsqrt(jnp.mean(c_kv * c_kv, axis=-1, keepdims=True) + EPS) * kvg_ref[...]).astype(BF16)
        cos, s1, s2 = _rope_tables(pos_ref[...], invf_ref[...])
        kr = _rope(kr_raw, cos, s1, s2, 1.0)
        lane = lax.broadcasted_iota(jnp.int32, (tm, HEAD_PAD), 1)
        q_all = _dot(cqn, jnp.concatenate([wh_ref[h, :Q_LORA, :] for h in range(HEADS)], axis=1))
        kv_all = _dot(ckvn, jnp.concatenate([wh_ref[h, Q_LORA:, :] for h in range(HEADS)], axis=1))
        for h in range(HEADS):
            q_h = q_all[:, h * HEAD_PAD:(h + 1) * HEAD_PAD]
            kv_h = kv_all[:, h * HEAD_PAD:(h + 1) * HEAD_PAD]
            q_ref[h] = (_rope(q_h, cos, s1, s2, 1.0) * Q_PRESCALE).astype(BF16)
            k_ref[h] = jnp.where(lane < NOPE, kv_h, kr).astype(BF16)
            v_ref[h] = kv_h.astype(BF16)
            vt_ref[h] = jnp.transpose(jnp.where(lane == SUM_ROW, 1.0, kv_h)).astype(BF16)

    full = lambda a: pl.BlockSpec(a.shape, lambda i: (0,) * a.ndim)
    head_spec = pl.BlockSpec((HEADS, tm, HEAD_PAD), lambda i: (0, i, 0))
    head_shape = jax.ShapeDtypeStruct((HEADS, t, HEAD_PAD), BF16)
    return pl.pallas_call(
        body, name="fwd_proj", grid=(t // tm,),
        in_specs=[pl.BlockSpec((tm, D_MODEL), lambda i: (i, 0)), pl.BlockSpec((1, tm), lambda i: (0, i)),
                  full(invf_col), full(w_in_shards), full(w_heads), full(q_g), full(kv_g)],
        out_specs=[pl.BlockSpec((tm, D_IN_PAD), lambda i: (i, 0)), head_spec, head_spec, head_spec,
                   pl.BlockSpec((HEADS, HEAD_PAD, tm), lambda i: (0, 0, i)),
                   pl.BlockSpec((D_MODEL, D_IN_PAD), lambda i: (0, 0))],
        out_shape=[jax.ShapeDtypeStruct((t, D_IN_PAD), F32), head_shape, head_shape, head_shape,
                   jax.ShapeDtypeStruct((HEADS, HEAD_PAD, t), BF16),
                   jax.ShapeDtypeStruct((D_MODEL, D_IN_PAD), w_in_shards.dtype)],
        compiler_params=_cparams(("arbitrary",)),
    )(x, pos_row, invf_col, w_in_shards, w_heads, q_g, kv_g)


def _attn_fwd(q, k, vt):
    t = q.shape[1]
    bq, bk = ATTN_FWD_WIDE, ATTN_NARROW
    n_diag = bq // bk
    chunk = SOFTMAX_ROWS

    def body(q_ref, k_ref, vt_ref, o_ref, lse_ref, s0, s1, p0, p1, x0, x1, m_scr, a_scr, acc_scr):
        i = pl.program_id(1)
        at = lambda j: pl.ds(pl.multiple_of(j * bk, bk), bk)

        def exp_pass(s_in, block_max, p_out, diagonal=False, cols=slice(None)):
            width = bq if cols == slice(None) else cols.stop - cols.start

            def load(r):
                s = s_in[r:r + chunk, cols]
                if diagonal:
                    key = lax.broadcasted_iota(jnp.int32, (chunk, width), 0) + r
                    qry = lax.broadcasted_iota(jnp.int32, (chunk, width), 1)
                    s = jnp.where(qry >= key, s, -jnp.inf)
                return s

            if diagonal:
                block_max = jnp.max(load(0), axis=0, keepdims=True)
                for r in range(chunk, bk, chunk):
                    block_max = jnp.maximum(block_max, jnp.max(load(r), axis=0, keepdims=True))
            m_old = m_scr[:, cols]
            m_new = jnp.maximum(m_old, block_max)
            alpha = jnp.exp2(m_old - m_new)
            for r in range(0, bk, chunk):
                p_out[r:r + chunk, cols] = jnp.exp2(load(r) - m_new).astype(BF16)
            m_scr[:, cols] = m_new
            return alpha

        def scores(j, s_out, x_out):
            s = _dot_nt(k_ref[0, at(j), :], q_ref[0])
            s_out[...] = s
            x_out[...] = jnp.max(s, axis=0, keepdims=True)

        def value_product(j, p_in):
            return _dot(vt_ref[0, LIVE_ROWS, at(j)], p_in[...])

        def one_pass(j, s_in, x_in, s_out, x_out, p_prev, p_cur):
            scores(j + 1, s_out, x_out)
            acc_scr[...] = a_scr[...] * acc_scr[...] + value_product(jnp.maximum(j - 1, 0), p_prev)
            a_scr[...] = exp_pass(s_in, x_in[...], p_cur)

        scores(0, s0, x0)
        p1[...] = jnp.zeros_like(p1)
        a_scr[...] = jnp.ones_like(a_scr)
        m_scr[...] = jnp.full(m_scr.shape, -jnp.inf, F32)
        acc_scr[...] = jnp.zeros_like(acc_scr)

        def two_passes(n, _):
            one_pass(2 * n, s0, x0, s1, x1, p1, p0)
            one_pass(2 * n + 1, s1, x1, s0, x0, p0, p1)
            return 0

        lax.fori_loop(0, (n_diag // 2) * i, two_passes, 0)
        d = n_diag * i
        alpha, p_prev, cols = a_scr[...], p1, slice(0, bq)
        for u in range(n_diag + 1):
            s_in, s_next, p_cur = (s0, s1, p0) if u % 2 == 0 else (s1, s0, p1)
            if u + 1 < n_diag:
                ahead = slice((u + 1) * bk, bq)
                s_next[:, ahead] = _dot_nt(k_ref[0, at(d + u + 1), :], q_ref[0, ahead, :])
            acc_scr[:, cols] = alpha * acc_scr[:, cols] + _dot(vt_ref[0, LIVE_ROWS, at(jnp.maximum(d + u - 1, 0))],
                                                               p_prev[:, cols])
            if u < n_diag:
                cols = slice(u * bk, bq)
                alpha = exp_pass(s_in, None, p_cur, diagonal=True, cols=cols)
                p_prev = p_cur
        denom = acc_scr[SUM_ROW - LIVE_ROWS.start:NOPE - LIVE_ROWS.start, :]
        o = jnp.transpose(acc_scr[NOPE - LIVE_ROWS.start:, :] / denom)
        o_ref[0] = jnp.concatenate([jnp.zeros_like(o), o], axis=1)
        lse_ref[0] = m_scr[...] + jnp.log2(denom)

    tile = lambda dtype: pltpu.VMEM((bk, bq), dtype)
    stat = pltpu.VMEM((1, bq), F32)
    return pl.pallas_call(
        body, name="attn_fwd", grid=(HEADS, t // bq),
        in_specs=[pl.BlockSpec((1, bq, HEAD_PAD), lambda h, i: (h, i, 0)),
                  pl.BlockSpec((1, t, HEAD_PAD), lambda h, i: (h, 0, 0)),
                  pl.BlockSpec((1, HEAD_PAD, t), lambda h, i: (h, 0, 0))],
        out_specs=[pl.BlockSpec((1, bq, HEAD_PAD), lambda h, i: (h, i, 0)),
                   pl.BlockSpec((1, 1, bq), lambda h, i: (h, 0, i))],
        out_shape=[jax.ShapeDtypeStruct((HEADS, t, HEAD_PAD), F32), jax.ShapeDtypeStruct((HEADS, 1, t), F32)],
        scratch_shapes=[tile(F32), tile(F32), tile(BF16), tile(BF16), stat, stat, stat, stat,
                        pltpu.VMEM((HEAD_PAD - LIVE_ROWS.start, bq), F32)],
        compiler_params=_cparams(("arbitrary", "arbitrary")),
    )(q, k, vt)


def _mid(x, target, proj, ol, w_out, ws_low, ws_low_t, bsp, sgu_g, sgu_b, ln_g, ln_b):
    t = x.shape[0]
    tm = TOKEN_TILE
    n_steps = t // tm

    def body(x_ref, tgt_ref, za_ref, u_ref, v_ref, zb_ref, ol_ref, prev_za_ref, prev_u_ref, prev_v_ref, prev_zb_ref,
             prev_ol_ref, wout_ref, ws_ref, wst_ref, bsp_ref, sg_ref, sb_ref, lg_ref, lb_ref,
             dr_ref, do_ref, drow_ref, drest_ref, dwout_ref, dws_ref, dbs_ref, dlg_ref, dlb_ref, dsg_ref, dsb_ref,
             loss_ref, dbsp_acc, *kept_refs):
        step = pl.program_id(0)
        kept_sets = (kept_refs[:len(kept_refs) // 2], kept_refs[len(kept_refs) // 2:])

        @pl.when(step == 0)
        def _():
            dwout_ref[...] = jnp.zeros_like(dwout_ref)
            dws_ref[...] = jnp.zeros_like(dws_ref)
            dbs_ref[...] = jnp.zeros_like(dbs_ref)
            dlg_ref[...] = jnp.zeros_like(dlg_ref)
            dlb_ref[...] = jnp.zeros_like(dlb_ref)
            dsg_ref[...] = jnp.zeros_like(dsg_ref)
            dsb_ref[...] = jnp.zeros_like(dsb_ref)
            loss_ref[...] = jnp.zeros_like(loss_ref)
            dbsp_acc[...] = jnp.zeros_like(dbsp_acc)

        n_chunks = tm // CHUNK
        groups = G_WIDTH // LANES

        def side_by_side(a):
            return [jnp.concatenate([a[c * CHUNK:(c + 1) * CHUNK, g * LANES:(g + 1) * LANES] for c in range(n_chunks)],
                                    axis=1) for g in range(groups)]

        def by_chunk(wide):
            return jnp.concatenate([jnp.concatenate([wide[g][:, c * LANES:(c + 1) * LANES] for g in range(groups)], axis=1)
                                    for c in range(n_chunks)], axis=0)

        def own_lanes(h):
            lane = lax.broadcasted_iota(jnp.int32, (CHUNK, n_chunks * LANES), 1)
            return (lane % LANES) // G_HEAD_DIM == h % 2

        def spatial(w_ref, wide):
            return [sum(jnp.where(own_lanes(h), _dot(w_ref[h], wide[g]), 0.0) for h in (2 * g, 2 * g + 1))
                    for g in range(groups)]

        def value_lanes(o_ref):
            return jnp.concatenate([o_ref[h][:, NOPE:] for h in range(HEADS)], axis=-1)

        def forward(kept):
            attn = value_lanes(ol_ref)
            za = za_ref[...]
            sig_a = _sigmoid(za)
            out_a = attn * (za * sig_a)
            u = u_ref[...]
            cdf_u = _normal_cdf(u)
            vpre = v_ref[...]
            cdf_v = _normal_cdf(vpre)
            gv = vpre * cdf_v
            mu_v = jnp.mean(gv, axis=-1, keepdims=True)
            cen_v = gv - mu_v
            rstd_v = lax.rsqrt(jnp.mean(cen_v * cen_v, axis=-1, keepdims=True) + EPS)
            vhat = cen_v * rstd_v
            vg = vhat * sg_ref[...] + sb_ref[...]
            vg_b = vg.astype(BF16)
            yield
            sv = by_chunk(spatial(ws_ref, side_by_side(vg_b))) + jnp.tile(bsp_ref[...], (n_chunks, 1))
            zb = zb_ref[...]
            sig_b = _sigmoid(zb)
            out_b = ((u * cdf_u) * sv) * (zb * sig_b)
            merged = jnp.concatenate([out_a, out_b], axis=-1).astype(BF16)
            yield
            r = DN_ALPHA * x_ref[...] + _dot(merged, wout_ref[...])
            mu = jnp.mean(r, axis=-1, keepdims=True)
            cen = r - mu
            rstd = lax.rsqrt(jnp.mean(cen * cen, axis=-1, keepdims=True) + EPS)
            xhat = cen * rstd
            hout = xhat * lg_ref[...] + lb_ref[...]
            err = hout - tgt_ref[...]
            row_loss = jnp.mean(err * err, axis=-1, keepdims=True)
            loss_ref[...] += jnp.broadcast_to(0.5 * jnp.sum(row_loss, axis=0, keepdims=True), loss_ref.shape)
            for ref, val in zip(kept, (sig_a, cdf_u, cdf_v, vhat, sv, sig_b, xhat, err * (1.0 / D_MODEL), merged, vg_b,
                                       jnp.broadcast_to(rstd, (tm, LANES)), jnp.broadcast_to(rstd_v, (tm, LANES)))):
                ref[...] = val

        def backward(kept):
            (sig_a_ref, cdf_u_ref, cdf_v_ref, vhat_ref, sv_ref, sig_b_ref, xhat_ref, dh_ref, merged_ref, vg_ref,
             rstd_ref, rstd_v_ref) = kept
            attn = value_lanes(prev_ol_ref)
            za, u, vpre, zb = prev_za_ref[...], prev_u_ref[...], prev_v_ref[...], prev_zb_ref[...]
            sig_a, cdf_u, cdf_v, vhat, sv, sig_b = (sig_a_ref[...], cdf_u_ref[...], cdf_v_ref[...], vhat_ref[...],
                                                    sv_ref[...], sig_b_ref[...])
            xhat, dh, merged, vg_b = xhat_ref[...], dh_ref[...], merged_ref[...], vg_ref[...]
            rstd, rstd_v = rstd_ref[:, :1], rstd_v_ref[:, :1]
            silu_a, silu_b, ug = za * sig_a, zb * sig_b, u * cdf_u
            sgu = ug * sv
            dlg_ref[...] += jnp.sum(dh * xhat, axis=0, keepdims=True)
            dlb_ref[...] += jnp.sum(dh, axis=0, keepdims=True)
            dxhat = dh * lg_ref[...]
            dr = rstd * (dxhat - jnp.mean(dxhat, axis=-1, keepdims=True)
                         - xhat * jnp.mean(dxhat * xhat, axis=-1, keepdims=True))
            dr_ref[...] = dr
            dr_b = dr.astype(BF16)
            yield
            dwout_ref[...] += _dot_tn(merged, dr_b)
            dmerged = _dot_nt(dr_b, wout_ref[...])
            yield
            d_out_a = dmerged[:, :G_WIDTH]
            d_out_b = dmerged[:, G_WIDTH:]
            dattn = d_out_a * silu_a
            for h in range(HEADS):
                do_h = dattn[:, h * VDIM:(h + 1) * VDIM]
                do_ref[h] = jnp.concatenate([jnp.zeros((tm, NOPE), F32), do_h], axis=-1).astype(BF16)
            feature = lax.broadcasted_iota(jnp.int32, (G_WIDTH, LANES), 0) // VDIM
            column = lax.broadcasted_iota(jnp.int32, (G_WIDTH, LANES), 1)
            head_sums = jnp.dot(dattn * attn, jnp.where(feature == column, 1.0, 0.0).astype(F32),
                                preferred_element_type=F32, precision=lax.Precision.HIGH)
            dsums_t = jnp.transpose(head_sums)
            for h in range(HEADS):
                drow_ref[h] = dsums_t[h:h + 1, :]
            dza = d_out_a * attn * (sig_a * (1.0 + za * (1.0 - sig_a)))
            dsgu = d_out_b * silu_b
            dzb = d_out_b * sgu * (sig_b * (1.0 + zb * (1.0 - sig_b)))
            du = dsgu * sv * _gelu_grad(u, cdf_u)
            dsv = dsgu * ug
            dsv_b = dsv.astype(BF16)
            for cix in range(n_chunks):
                dbsp_acc[...] += dsv[cix * CHUNK:(cix + 1) * CHUNK, :]
            yield
            dsv_wide, vg_wide = side_by_side(dsv_b), side_by_side(vg_b)
            dvg = by_chunk(spatial(wst_ref, dsv_wide))
            for h in range(HEADS):
                mine = jnp.where(own_lanes(h), dsv_wide[h // 2], jnp.zeros_like(dsv_wide[h // 2]))
                dws_ref[h] += _dot_nt(mine, vg_wide[h // 2])
            dsg_ref[...] += jnp.sum(dvg * vhat, axis=0, keepdims=True)
            dsb_ref[...] += jnp.sum(dvg, axis=0, keepdims=True)
            dvhat = dvg * sg_ref[...]
            dgv = rstd_v * (dvhat - jnp.mean(dvhat, axis=-1, keepdims=True)
                            - vhat * jnp.mean(dvhat * vhat, axis=-1, keepdims=True))
            dv = dgv * _gelu_grad(vpre, cdf_v)
            drest_ref[...] = jnp.concatenate([dza, du, dv, dzb], axis=-1).astype(BF16)

        def emit(order, **stages):
            for who in order:
                next(stages[who], None)

        @pl.when(step == 0)
        def _():
            emit("fff", f=forward(kept_sets[0]))

        for parity in (0, 1):
            @pl.when((step > 0) & (step < n_steps) & (step % 2 == parity))
            def _():
                emit("ffbbfbb", f=forward(kept_sets[parity]), b=backward(kept_sets[1 - parity]))

        @pl.when(step == n_steps)
        def _():
            emit("bbbb", b=backward(kept_sets[(n_steps - 1) % 2]))
            tri = (lax.broadcasted_iota(jnp.int32, (CHUNK, CHUNK), 0)
                   >= lax.broadcasted_iota(jnp.int32, (CHUNK, CHUNK), 1))
            for h in range(HEADS):
                dws_ref[h] = jnp.where(tri, dws_ref[h], 0.0)
            tot = dbsp_acc[...]
            lane = lax.broadcasted_iota(jnp.int32, (CHUNK, LANES), 1)
            dbs = jnp.zeros((CHUNK, LANES), F32)
            for h in range(HEADS):
                head_sum = jnp.sum(tot[:, h * G_HEAD_DIM:(h + 1) * G_HEAD_DIM], axis=-1, keepdims=True)
                dbs = jnp.where(lane == h, head_sum, dbs)
            dbs_ref[...] = dbs

    full = lambda a: pl.BlockSpec(a.shape, lambda i: (0,) * a.ndim)
    this = lambda i: jnp.minimum(i, n_steps - 1)
    prev = lambda i: jnp.maximum(i - 1, 0)
    tile = lambda w, j=0, at=this: pl.BlockSpec((tm, w), lambda i, j=j: (at(i), j))
    heads = lambda at: pl.BlockSpec((HEADS, tm, HEAD_PAD), lambda i: (0, at(i), 0))
    acc = lambda shape: (pl.BlockSpec(shape, lambda i: (0,) * len(shape)), jax.ShapeDtypeStruct(shape, F32))
    accs = [acc((D_MODEL, D_MODEL)), acc((HEADS, CHUNK, CHUNK)), acc((CHUNK, LANES)), acc((1, D_MODEL)),
            acc((1, D_MODEL)), acc((1, G_WIDTH)), acc((1, G_WIDTH)), acc((1, LANES))]
    kept = ([pltpu.VMEM((tm, G_WIDTH), F32)] * 6 + [pltpu.VMEM((tm, D_MODEL), F32)] * 2
            + [pltpu.VMEM((tm, D_MODEL), BF16), pltpu.VMEM((tm, G_WIDTH), BF16)] + [pltpu.VMEM((tm, LANES), F32)] * 2)
    return pl.pallas_call(
        body, name="mid", grid=(n_steps + 1,),
        in_specs=[tile(D_MODEL), tile(D_MODEL), tile(G_WIDTH, 1), tile(G_WIDTH, 2), tile(G_WIDTH, 3), tile(G_WIDTH, 4),
                  heads(this), tile(G_WIDTH, 1, prev), tile(G_WIDTH, 2, prev), tile(G_WIDTH, 3, prev),
                  tile(G_WIDTH, 4, prev), heads(prev),
                  full(w_out), full(ws_low), full(ws_low_t), full(bsp), full(sgu_g), full(sgu_b),
                  full(ln_g), full(ln_b)],
        out_specs=[tile(D_MODEL, 0, prev), heads(prev), pl.BlockSpec((HEADS, 1, tm), lambda i: (0, 0, prev(i))),
                   tile(4 * G_WIDTH, 0, prev)]
        + [a[0] for a in accs],
        out_shape=[jax.ShapeDtypeStruct((t, D_MODEL), F32), jax.ShapeDtypeStruct((HEADS, t, HEAD_PAD), BF16),
                   jax.ShapeDtypeStruct((HEADS, 1, t), F32), jax.ShapeDtypeStruct((t, 4 * G_WIDTH), BF16)]
        + [a[1] for a in accs],
        scratch_shapes=[pltpu.VMEM((CHUNK, G_WIDTH), F32)] + kept + kept,
        compiler_params=_cparams(("arbitrary",)),
    )(x, target, proj, proj, proj, proj, ol, proj, proj, proj, proj, ol,
      w_out, ws_low, ws_low_t, bsp, sgu_g, sgu_b, ln_g, ln_b)


def _attn_bwd(q, k, v, do, lse_row, d_row):
    t = q.shape[1]
    bk, bq = ATTN_BWD_WIDE, ATTN_NARROW
    n_diag = bk // bq
    half = bq // 2
    last = t // bq - 1
    chunk = SOFTMAX_ROWS

    def body(q_ref, k_ref, v_ref, do_ref, lse_ref, drow_ref, dqt_ref, dk_ref, dv_ref,
             s0, s1, e0, e1, p0, p1, g0, g1, kt_scr):
        j = pl.program_id(1)
        at = lambda i: pl.ds(pl.multiple_of(i * bq, bq), bq)

        @pl.when(j == 0)
        def _():
            dqt_ref[...] = jnp.zeros_like(dqt_ref)

        kt_scr[...] = jnp.transpose(k_ref[0].astype(F32)).astype(BF16)
        dk_ref[...] = jnp.zeros_like(dk_ref)
        dv_ref[...] = jnp.zeros_like(dv_ref)

        whole_tile = ((slice(0, bk), slice(0, bq)),)

        def queries(i, lanes):
            return pl.ds(pl.multiple_of(i * bq + lanes.start, half), lanes.stop - lanes.start)

        def products(i, s_out, e_out, areas=whole_tile):
            i = jnp.minimum(i, last)
            for keys, lanes in areas:
                s_out[keys, lanes] = _dot_nt(k_ref[0, keys, :], q_ref[0, queries(i, lanes), :])
                e_out[keys, lanes] = _dot_nt(v_ref[0, keys, :], do_ref[0, queries(i, lanes), :])

        def gradients(i, p_in, g_in, areas=whole_tile):
            for keys, lanes in areas:
                dv_ref[0, keys, :] += _dot(p_in[keys, lanes], do_ref[0, queries(i, lanes), :])
                dk_ref[0, keys, :] += _dot(g_in[keys, lanes], q_ref[0, queries(i, lanes), :])
                dqt_ref[0, :, queries(i, lanes)] += _dot(kt_scr[:, keys], g_in[keys, lanes])

        def elementwise(i, s_in, e_in, p_out, g_out, qry0=None, areas=whole_tile):
            for keys, lanes in areas:
                width = lanes.stop - lanes.start
                step = chunk if qry0 is None else half
                lse = lse_ref[0, :, queries(i, lanes)]
                dsum = drow_ref[0, :, queries(i, lanes)]
                for r in range(keys.start, keys.stop, step):
                    p = jnp.exp2(s_in[r:r + step, lanes] - lse)
                    if qry0 is not None:
                        key = lax.broadcasted_iota(jnp.int32, (step, width), 0) + r
                        qry = lax.broadcasted_iota(jnp.int32, (step, width), 1) + (qry0 + lanes.start)
                        p = jnp.where(qry >= key, p, 0.0)
                    p_out[r:r + step, lanes] = p.astype(BF16)
                    g_out[r:r + step, lanes] = (p * (e_in[r:r + step, lanes] - dsum)).astype(BF16)

        def one_pass(i, s_in, e_in, s_out, e_out, p_prev, g_prev, p_cur, g_cur):
            products(i + 1, s_out, e_out)
            gradients(i - 1, p_prev, g_prev)
            elementwise(i, s_in, e_in, p_cur, g_cur)

        first = n_diag * j

        def areas_of(u):
            if u >= n_diag:
                return whole_tile
            return ((slice(0, u * bq + half), slice(0, bq)), (slice(u * bq + half, (u + 1) * bq), slice(half, bq)))

        even, odd = (s0, e0, p0, g0), (s1, e1, p1, g1)
        products(first, s0, e0, areas_of(0))
        products(first + 1, s1, e1, areas_of(1))
        elementwise(first, s0, e0, p0, g0, qry0=0, areas=areas_of(0))
        for u in range(1, n_diag):
            (s_in, e_in, p_cur, g_cur), (s_out, e_out, p_prev, g_prev) = (odd, even) if u % 2 else (even, odd)
            products(first + u + 1, s_out, e_out, areas_of(u + 1))
            gradients(first + u - 1, p_prev, g_prev, areas_of(u - 1))
            elementwise(first + u, s_in, e_in, p_cur, g_cur, qry0=u * bq, areas=areas_of(u))
        corner = (slice(bk - half, bk), slice(0, half))
        p1[corner] = jnp.zeros((half, half), BF16)
        g1[corner] = jnp.zeros((half, half), BF16)

        def two_passes(n, _):
            i = first + n_diag + 2 * n
            one_pass(i, s0, e0, s1, e1, p1, g1, p0, g0)
            one_pass(i + 1, s1, e1, s0, e0, p0, g0, p1, g1)
            return 0

        lax.fori_loop(0, (last - first - n_diag + 1) // 2, two_passes, 0)
        gradients(last, p1, g1)
        dk_ref[0] = dk_ref[0] * LN2

    whole = pl.BlockSpec((1, t, HEAD_PAD), lambda h, j: (h, 0, 0))
    block = pl.BlockSpec((1, bk, HEAD_PAD), lambda h, j: (h, j, 0))
    rows = pl.BlockSpec((1, 1, t), lambda h, j: (h, 0, 0), pipeline_mode=pl.Buffered(1))
    shape = jax.ShapeDtypeStruct((HEADS, t, HEAD_PAD), F32)
    tile = lambda dtype: pltpu.VMEM((bk, bq), dtype)
    return pl.pallas_call(
        body, name="attn_bwd", grid=(HEADS, t // bk),
        in_specs=[whole, block, block, whole, rows, rows],
        out_specs=[pl.BlockSpec((1, HEAD_PAD, t), lambda h, j: (h, 0, 0)), block, block],
        out_shape=[jax.ShapeDtypeStruct((HEADS, HEAD_PAD, t), F32), shape, shape],
        scratch_shapes=[tile(F32), tile(F32), tile(F32), tile(F32), tile(BF16), tile(BF16),
                        tile(BF16), tile(BF16), pltpu.VMEM((HEAD_PAD, bk), BF16)],
        compiler_params=_cparams(("arbitrary", "arbitrary"), vmem_limit=ATTN_BWD_VMEM_LIMIT),
    )(q, k, v, do, lse_row, d_row)


def _bwd_tail(dq, dk, dv, proj, pos_row, invf_col, w_heads, q_g, kv_g, x, dr, drest, wp_in):
    t = proj.shape[0]
    tm = PROJ_TILE
    n_head = 4 * LANES

    def body(dq_ref, dk_ref, dv_ref, ph_ref, pos_ref, invf_ref, wh_ref, qg_ref, kvg_ref,
             x_ref, dr_ref, drest_ref, win_ref,
             gx_ref, dwin_ref, dwh_ref, dqg_ref, dkvg_ref):
        @pl.when(pl.program_id(0) == 0)
        def _():
            dwin_ref[...] = jnp.zeros_like(dwin_ref)
            dwh_ref[...] = jnp.zeros_like(dwh_ref)
            dqg_ref[...] = jnp.zeros_like(dqg_ref)
            dkvg_ref[...] = jnp.zeros_like(dkvg_ref)

        xb = x_ref[...].astype(BF16)
        dr_b = drest_ref[...]
        dwin_ref[:, n_head:] += _dot_tn(xb, dr_b)
        gx_rest = DN_ALPHA * dr_ref[...] + _dot_nt(dr_b, win_ref[:, n_head:])

        cos, s1, s2 = _rope_tables(pos_ref[...], invf_ref[...])
        lane = lax.broadcasted_iota(jnp.int32, (tm, LANES), 1)
        c_q = ph_ref[:, :Q_LORA]
        c_kv = ph_ref[:, Q_LORA:Q_LORA + KV_LORA]
        rstd_q = lax.rsqrt(jnp.mean(c_q * c_q, axis=-1, keepdims=True) + EPS)
        rstd_kv = lax.rsqrt(jnp.mean(c_kv * c_kv, axis=-1, keepdims=True) + EPS)
        qhat = c_q * rstd_q
        kvhat = c_kv * rstd_kv
        cqn = (qhat * qg_ref[...]).astype(BF16)
        ckvn = (kvhat * kvg_ref[...]).astype(BF16)
        dkr_rot = jnp.zeros((tm, LANES), F32)
        dq_heads, dkv_heads = [], []
        for h in range(HEADS):
            dq_heads.append(_rope(jnp.transpose(dq_ref[h]) * ATTN_SCALE, cos, s1, s2, -1.0).astype(BF16))
            dk_h = dk_ref[h]
            dkv_heads.append(jnp.where(lane < NOPE, dk_h, dv_ref[h]).astype(BF16))
            dkr_rot = dkr_rot + dk_h
        dq_all = jnp.concatenate(dq_heads, axis=1)
        dkv_all = jnp.concatenate(dkv_heads, axis=1)
        dwq_all = _dot_tn(cqn, dq_all)
        dwkv_all = _dot_tn(ckvn, dkv_all)
        for h in range(HEADS):
            dwh_ref[h, :Q_LORA, :] += dwq_all[:, h * HEAD_PAD:(h + 1) * HEAD_PAD]
            dwh_ref[h, Q_LORA:, :] += dwkv_all[:, h * HEAD_PAD:(h + 1) * HEAD_PAD]
        dcqn = _dot_nt(dq_all, jnp.concatenate([wh_ref[h, :Q_LORA, :] for h in range(HEADS)], axis=1))
        dckvn = _dot_nt(dkv_all, jnp.concatenate([wh_ref[h, Q_LORA:, :] for h in range(HEADS)], axis=1))
        rot_lanes = (lane >= KR_LO) & (lane < KR_LO + ROPE)
        dkr_raw = jnp.where(rot_lanes, _rope(dkr_rot, cos, s1, s2, -1.0), 0.0)
        dqg_ref[...] += jnp.sum(dcqn * qhat, axis=0, keepdims=True)
        dkvg_ref[...] += jnp.sum(dckvn * kvhat, axis=0, keepdims=True)
        dqh = dcqn * qg_ref[...]
        dkvh = dckvn * kvg_ref[...]
        dc_q = rstd_q * (dqh - qhat * jnp.mean(dqh * qhat, axis=-1, keepdims=True))
        dc_kv = rstd_kv * (dkvh - kvhat * jnp.mean(dkvh * kvhat, axis=-1, keepdims=True))
        dh_b = jnp.concatenate([dc_q, dc_kv, dkr_raw], axis=-1).astype(BF16)
        dwin_ref[:, :n_head] += _dot_tn(xb, dh_b)
        gx_ref[...] = gx_rest + _dot_nt(dh_b, win_ref[:, :n_head])

    full = lambda a: pl.BlockSpec(a.shape, lambda i: (0,) * a.ndim)
    tile = lambda w: pl.BlockSpec((tm, w), lambda i: (i, 0))
    heads = pl.BlockSpec((HEADS, tm, HEAD_PAD), lambda i: (0, i, 0))
    acc = lambda shape: (pl.BlockSpec(shape, lambda i: (0,) * len(shape)), jax.ShapeDtypeStruct(shape, F32))
    accs = [acc(wp_in.shape), acc(w_heads.shape), acc((1, Q_LORA)), acc((1, KV_LORA))]
    return pl.pallas_call(
        body, name="bwd_tail", grid=(t // tm,),
        in_specs=[pl.BlockSpec((HEADS, HEAD_PAD, tm), lambda i: (0, 0, i)), heads, heads, tile(n_head),
                  pl.BlockSpec((1, tm), lambda i: (0, i)), full(invf_col), full(w_heads), full(q_g), full(kv_g),
                  tile(D_MODEL), tile(D_MODEL), tile(drest.shape[1]), full(wp_in)],
        out_specs=[tile(D_MODEL)] + [a[0] for a in accs],
        out_shape=[jax.ShapeDtypeStruct((t, D_MODEL), F32)] + [a[1] for a in accs],
        compiler_params=_cparams(("arbitrary",), vmem_limit=BWD_TAIL_VMEM_LIMIT),
    )(dq, dk, dv, proj, pos_row, invf_col, w_heads, q_g, kv_g, x, dr, drest, wp_in)


def _adam_update(g, w, m, v):
    m_new = ADAM_B1 * m + (1.0 - ADAM_B1) * g
    v_new = ADAM_B2 * v + (1.0 - ADAM_B2) * (g * g)
    m_hat = m_new / (1.0 - ADAM_B1 ** ADAM_STEP)
    v_hat = v_new / (1.0 - ADAM_B2 ** ADAM_STEP)
    return -ADAM_LR * (m_hat / (jnp.sqrt(v_hat) + ADAM_EPS) + ADAM_WD * w), m_new, v_new


def _adam(parts, w, m, v, *, name, tile_rows, transposed=False):
    n, rows, cols = parts.shape
    lane_pad = -(-cols // LANES) * LANES
    own_rows = rows if transposed else w.shape[0]
    assert own_rows == rows or tile_rows == rows

    def body(p_ref, w_ref, m_ref, v_ref, g_ref, d_ref, nm_ref, nv_ref, *scratch):
        g = p_ref[0].astype(F32)
        for s in range(1, n):
            g = g + p_ref[s].astype(F32)
        if transposed:
            wide_ref, = scratch
            wide_ref[:, lane_pad - LANES:] = jnp.zeros((tile_rows, LANES), F32)
            wide_ref[:, :cols] = g
            g = jnp.transpose(wide_ref[...])[:cols]
        g_ref[...] = g
        d_ref[...], nm_ref[...], nv_ref[...] = _adam_update(g[:w_ref.shape[0]], w_ref[...], m_ref[...], v_ref[...])

    if transposed:
        flat = grad = pl.BlockSpec((cols, tile_rows), lambda i: (0, i))
        shape = grad_shape = jax.ShapeDtypeStruct((cols, rows), F32)
        scratch = [pltpu.VMEM((tile_rows, lane_pad), F32)]
    else:
        own_tile = min(tile_rows, own_rows)
        flat = pl.BlockSpec((own_tile, cols), lambda i: (i, 0))
        grad = pl.BlockSpec((tile_rows, cols), lambda i: (i, 0))
        shape, grad_shape = jax.ShapeDtypeStruct((own_rows, cols), F32), jax.ShapeDtypeStruct((rows, cols), F32)
        scratch = []
    return pl.pallas_call(
        body, name=name, grid=(rows // tile_rows,),
        in_specs=[pl.BlockSpec((n, tile_rows, cols), lambda i: (0, i, 0)), flat, flat, flat],
        out_specs=[grad, flat, flat, flat], out_shape=[grad_shape, shape, shape, shape], scratch_shapes=scratch,
        compiler_params=_cparams(("arbitrary",)),
    )(parts, w, m, v)


def _adam_replicated(rep_g, ws, ms, vs):
    count = len(ws)
    small_rows = REP_ROWS - CHUNK

    def body(g_ref, *refs):
        w_refs, m_refs, v_refs = refs[:count], refs[count:2 * count], refs[2 * count:3 * count]
        outs, last_ref, slab_ref = refs[3 * count:7 * count], refs[7 * count], refs[7 * count + 1]
        for d in range(N_DEV):
            slab_ref[d * small_rows:(d + 1) * small_rows, :] = g_ref[d, CHUNK:, :]
        last_ref[...] = slab_ref[N_DEV * small_rows - 1:, LANES - 1:]
        at = 0
        for k, w_ref in enumerate(w_refs):
            if w_ref.ndim == 3:
                g = g_ref[:, :CHUNK, :]
            else:
                n_rows = w_ref.size // LANES
                g = slab_ref[at:at + n_rows, :].reshape(w_ref.shape)
                at += n_rows
            delta, m_new, v_new = _adam_update(g, w_ref[...], m_refs[k][...], v_refs[k][...])
            for which, val in enumerate((g, delta, m_new, v_new)):
                outs[which * count + k][...] = val

    shapes = [jax.ShapeDtypeStruct(w.shape, F32) for w in ws]
    res = pl.pallas_call(
        body, name="adam_rep", out_shape=shapes * 4 + [jax.ShapeDtypeStruct((1, 1), F32)],
        scratch_shapes=[pltpu.VMEM((N_DEV * small_rows, LANES), F32)],
        compiler_params=_cparams(),
    )(rep_g, *ws, *ms, *vs)
    return [res[which * count:(which + 1) * count] for which in range(4)], res[4 * count]


def _pack_small(vals, last):
    flat = jnp.concatenate([v.reshape(-1) for v in vals])
    pad = SMALL_LEN - flat.shape[0]
    return jnp.concatenate([flat, jnp.zeros((pad - 1,), F32), last.reshape(1)])


UQ_SHARD = HEADS * (NOPE + ROPE) // N_DEV
HEAD_ROWS = Q_LORA + KV_LORA
MIXED_ROWS = HEAD_ROWS + CHUNK + SMALL_LEN // N_DEV // LANES


def _head_slab(w_uq_shard, w_ukv_shard):
    return jnp.concatenate([jnp.pad(w_uq_shard, ((0, 0), (0, LANES - UQ_SHARD))), w_ukv_shard])


IN_SHARD = D_IN // N_DEV


def _w_in_pieces():
    split = Q_LORA + KV_LORA
    moves = ((0, split, 0), (split, split + ROPE, KR_LO), (split + ROPE, D_IN, LANES - ROPE))
    pieces = []
    for s in range(N_DEV):
        lo, hi = s * IN_SHARD, (s + 1) * IN_SHARD
        for a, b, shift in moves:
            a, b = max(a, lo), min(b, hi)
            if a < b:
                pieces.append((s, a - lo, a + shift, b - a))
    return pieces


def _w_in_shards(dwp_in):
    tr = TOKEN_TILE
    by_shard = [[p for p in _w_in_pieces() if p[0] == s] for s in range(N_DEV)]

    def body(w_ref, o_ref):
        for s, pieces in enumerate(by_shard):
            parts = [w_ref[:, dst:dst + width] for _, _, dst, width in pieces]
            o_ref[s] = parts[0] if len(parts) == 1 else jnp.concatenate(parts, axis=1)

    return pl.pallas_call(
        body, name="w_in_split", grid=(D_MODEL // tr,),
        in_specs=[pl.BlockSpec((tr, D_IN_PAD), lambda i: (i, 0))],
        out_specs=pl.BlockSpec((N_DEV, tr, IN_SHARD), lambda i: (0, i, 0)),
        out_shape=jax.ShapeDtypeStruct((N_DEV, D_MODEL, IN_SHARD), dwp_in.dtype),
        compiler_params=_cparams(("arbitrary",)),
    )(dwp_in)


def kernel(x, positions, w_in, q_norm_g, w_uq, kv_norm_g, w_ukv, sgu_norm_g, sgu_norm_b, w_spatial, b_spatial, w_out, ln_g, ln_b, loss_target, m_w_in, m_q_norm_g, m_w_uq, m_kv_norm_g, m_w_ukv, m_sgu_norm_g, m_sgu_norm_b, m_w_spatial, m_b_spatial, m_w_out, m_ln_g, m_ln_b, v_w_in, v_q_norm_g, v_w_uq, v_kv_norm_g, v_w_ukv, v_sgu_norm_g, v_sgu_norm_b, v_w_spatial, v_b_spatial, v_w_out, v_ln_g, v_ln_b):
    seq = x.shape[1]
    x2 = x.reshape(seq, D_MODEL)
    tgt2 = loss_target.reshape(seq, D_MODEL)
    pos_row = positions.reshape(1, seq)

    w_in_shards, w_out_shards, w_heads = _gather_two_level(
        [w_in.astype(BF16), w_out.astype(BF16), _head_slab(w_uq, w_ukv).astype(BF16)],
        name="wgather")
    (loss_part, grad_x, d_in, d_heads, d_out, d_ws, d_bs_t, d_lng, d_lnb, d_sgug, d_sgub, d_qg, d_kvg) = _local_step(
        x2, tgt2, pos_row, w_in_shards, w_heads, w_out_shards.reshape(D_MODEL, D_MODEL), q_norm_g, kv_norm_g,
        sgu_norm_g, sgu_norm_b, w_spatial, b_spatial, ln_g, ln_b)

    small_part = _pack_small([d_qg, d_kvg, d_sgug, d_sgub, d_bs_t[:, :HEADS].T, d_lng, d_lnb], last=loss_part[0, :1])
    mixed = jnp.concatenate([d_heads, d_ws, small_part.reshape(N_DEV, -1, LANES)], axis=1)
    by_chip = [g.reshape((N_CHIPS, 2) + g.shape[1:])
               for g in (d_in, d_out.reshape(N_DEV, D_MODEL // N_DEV, D_MODEL), mixed)]
    from_sibling = _sibling_swap(by_chip, name="gswap")
    core = lax.axis_index("c").astype(jnp.int32).reshape(1)
    pair_sums = [_pair_sum(a, b, core, name=nm, out_dtype=dt) for a, b, nm, dt in zip(
        by_chip, from_sibling, ("gsum_in", "gsum_out", "gsum_mixed"), (BF16, BF16, F32))]
    recv_in, recv_out, recv_mixed = _chip_exchange(pair_sums, name="gexch")

    res_in = [a.T for a in _adam(recv_in, w_in.T, m_w_in.T, v_w_in.T, name="adam_in", tile_rows=PROJ_TILE,
                                 transposed=True)]
    res_out = _adam(recv_out, w_out, m_w_out, v_w_out, name="adam_out", tile_rows=D_MODEL // N_DEV)
    res_mixed = _adam(recv_mixed, _head_slab(w_uq, w_ukv), _head_slab(m_w_uq, m_w_ukv), _head_slab(v_w_uq, v_w_ukv),
                      name="adam_mixed", tile_rows=MIXED_ROWS)

    rep_g, = _gather_direct([res_mixed[0]], name="sgather", first_row=HEAD_ROWS)
    res_rep, loss = _adam_replicated(
        rep_g,
        [q_norm_g, kv_norm_g, sgu_norm_g, sgu_norm_b, w_spatial, b_spatial, ln_g, ln_b],
        [m_q_norm_g, m_kv_norm_g, m_sgu_norm_g, m_sgu_norm_b, m_w_spatial, m_b_spatial, m_ln_g, m_ln_b],
        [v_q_norm_g, v_kv_norm_g, v_sgu_norm_g, v_sgu_norm_b, v_w_spatial, v_b_spatial, v_ln_g, v_ln_b])

    def ordered(which):
        r_qg, r_kvg, r_sg, r_sb, r_ws, r_bs, r_lg, r_lb = res_rep[which]
        heads = res_mixed[which]
        return [res_in[which], r_qg, heads[:Q_LORA, :UQ_SHARD], r_kvg, heads[Q_LORA:HEAD_ROWS], r_sg, r_sb, r_ws, r_bs,
                res_out[which], r_lg, r_lb]

    outs = [loss.reshape(()), grad_x.reshape(x.shape)]
    for which in range(4):
        outs += ordered(which)
    return tuple(outs)


def _local_step(x2, tgt2, pos_row, w_in_shards, w_heads, w_out_full, q_norm_g, kv_norm_g, sgu_norm_g, sgu_norm_b,
                w_spatial, b_spatial, ln_g, ln_b):
    half = jnp.arange(HALF, dtype=F32)
    invf_col = (1.0 / (ROPE_THETA ** (half / HALF))).reshape(HALF, 1)
    tri = jnp.tril(jnp.ones((CHUNK, CHUNK), dtype=bool))
    ws_low = jnp.where(tri[None], w_spatial, 0.0).astype(BF16)
    ws_low_t = ws_low.transpose(0, 2, 1)
    bsp = jnp.repeat(b_spatial.T, G_HEAD_DIM, axis=1)
    row = lambda a: a.reshape(1, -1)

    proj, q, k, v, vt, wp_in = _fwd_proj(x2, pos_row, invf_col, w_in_shards, w_heads, row(q_norm_g), row(kv_norm_g))
    o, lse_row = _attn_fwd(q, k, vt)
    (dr, do, d_row, drest, d_out, d_ws, d_bs_t, d_lng, d_lnb, d_sgug, d_sgub, loss_part) = _mid(
        x2, tgt2, proj, o, w_out_full, ws_low, ws_low_t, bsp, row(sgu_norm_g), row(sgu_norm_b), row(ln_g), row(ln_b))
    dqt, dk, dv = _attn_bwd(q, k, v, do, lse_row, d_row)
    grad_x, dwp_in, d_heads, d_qg, d_kvg = _bwd_tail(dqt, dk, dv, proj, pos_row, invf_col, w_heads, row(q_norm_g),
                                                      row(kv_norm_g), x2, dr, drest, wp_in)
    return (loss_part, grad_x, _w_in_shards(dwp_in), d_heads, d_out, d_ws, d_bs_t, d_lng, d_lnb, d_sgug, d_sgub,
            d_qg, d_kvg)
```

```python
import math

import jax
import jax.numpy as jnp
from jax import lax
from jax.experimental import pallas as pl
from jax.experimental.pallas import tpu as pltpu

F32 = jnp.float32
BF16 = jnp.bfloat16

N_DEV = 8
D_MODEL = 1024
HEADS = 8
NOPE = 64
ROPE = 32
HALF = ROPE // 2
VDIM = 64
Q_LORA = 256
KV_LORA = 128
G_WIDTH = 512
G_HEAD_DIM = 64
CHUNK = 128
HEAD_PAD = 128
D_IN = 2464
D_IN_PAD = 2560
KR_LO = NOPE
SUM_ROW = NOPE - 1
LIVE_ROWS = slice(NOPE - 16, HEAD_PAD)
ROPE_THETA = 10000.0
DN_ALPHA = 2.0 ** 0.25
EPS = 1e-5
ATTN_SCALE = 1.0 / math.sqrt(NOPE + ROPE)
ADAM_LR, ADAM_B1, ADAM_B2, ADAM_EPS, ADAM_WD, ADAM_STEP = 0.001, 0.9, 0.999, 1e-08, 0.01, 10

LANES = 128
REP_ROWS = 136
SMALL_LEN = 8192
VMEM_LIMIT = 56 * 1024 * 1024
ATTN_BWD_VMEM_LIMIT = 61 * 1024 * 1024
BWD_TAIL_VMEM_LIMIT = 61 * 1024 * 1024

TOKEN_TILE = 256
PROJ_TILE = 512
ATTN_FWD_WIDE = 2048
ATTN_BWD_WIDE = 2048
ATTN_NARROW = 512
SOFTMAX_ROWS = 512
LOG2E = 1.4426950408889634
LN2 = 0.6931471805599453
Q_PRESCALE = ATTN_SCALE * LOG2E


def _cparams(sem=None, vmem_limit=VMEM_LIMIT):
    return pltpu.CompilerParams(dimension_semantics=sem, vmem_limit_bytes=vmem_limit)


def _dot(a, b):
    return jnp.dot(a, b, preferred_element_type=F32)


def _dot_nt(a, b):
    return lax.dot_general(a, b, (((1,), (1,)), ((), ())), preferred_element_type=F32)


def _dot_tn(a, b):
    return lax.dot_general(a, b, (((0,), (0,)), ((), ())), preferred_element_type=F32)


def _sigmoid(z):
    return 1.0 / (1.0 + jnp.exp(-z))


def _normal_cdf(x):
    return 0.5 * (1.0 + lax.erf(x * 0.7071067811865476))


def _gelu_grad(x, cdf):
    return cdf + x * jnp.exp(-0.5 * x * x) * 0.3989422804014327


def _gather_direct(srcs, *, name, first_row=0):
    n = len(srcs)
    shapes = [(s.shape[0] - first_row,) + s.shape[1:] for s in srcs]

    def body(*refs):
        src_refs, out_refs = [r.at[pl.ds(first_row, shape[0])] for r, shape in zip(refs[:n], shapes)], refs[n:2 * n]
        send_sems, recv_sems, local_sems = refs[2 * n:]
        x, y, c = lax.axis_index("x"), lax.axis_index("y"), lax.axis_index("c")
        me = 4 * x + 2 * y + c
        mine = [pltpu.make_async_copy(src_refs[t], out_refs[t].at[me], local_sems.at[t]) for t in range(n)]
        for cp in mine:
            cp.start()
        sends, arrivals = [], []
        for k in (6, 7, 4, 5, 2, 3, 1):
            px = 1 - x if k & 4 else x
            py = 1 - y if k & 2 else y
            pc = 1 - c if k & 1 else c
            peer = 4 * px + 2 * py + pc
            for t in range(n):
                sem = (k - 1) * n + t
                cp = pltpu.make_async_remote_copy(
                    src_ref=src_refs[t], dst_ref=out_refs[t].at[me],
                    send_sem=send_sems.at[sem], recv_sem=recv_sems.at[sem],
                    device_id=(px, py, pc), device_id_type=pl.DeviceIdType.MESH)
                cp.start()
                sends.append(cp)
                arrivals.append(pltpu.make_async_remote_copy(
                    src_ref=src_refs[t], dst_ref=out_refs[t].at[peer],
                    send_sem=send_sems.at[sem], recv_sem=recv_sems.at[sem],
                    device_id=(x, y, c), device_id_type=pl.DeviceIdType.MESH))
        for cp in arrivals:
            cp.wait_recv()
        for cp in sends:
            cp.wait_send()
        for cp in mine:
            cp.wait()

    hbm = pl.BlockSpec(memory_space=pl.ANY)
    return pl.pallas_call(
        body, name=name,
        out_shape=[jax.ShapeDtypeStruct((N_DEV,) + shape, s.dtype) for shape, s in zip(shapes, srcs)],
        in_specs=[hbm] * n, out_specs=[hbm] * n,
        scratch_shapes=[pltpu.SemaphoreType.DMA(((N_DEV - 1) * n,)), pltpu.SemaphoreType.DMA(((N_DEV - 1) * n,)),
                        pltpu.SemaphoreType.DMA((n,))],
    )(*srcs)


def _gather_two_level(srcs, *, name):
    n = len(srcs)

    def body(*refs):
        src_refs, out_refs = refs[:n], refs[n:2 * n]
        send_sems, recv_sems, local_sems = refs[2 * n:]
        x, y, c = lax.axis_index("x"), lax.axis_index("y"), lax.axis_index("c")
        me, sibling = (x, y, c), (x, y, 1 - c)
        chips = [(1 - x, 1 - y), (1 - x, y), (x, 1 - y)]
        index = lambda px, py, pc: 4 * px + 2 * py + pc

        def copy(k, t, block, to, src=None):
            place = out_refs[t].at[index(*block)]
            return pltpu.make_async_remote_copy(
                src_ref=place if src is None else src, dst_ref=place,
                send_sem=send_sems.at[k * n + t], recv_sem=recv_sems.at[k * n + t],
                device_id=to, device_id_type=pl.DeviceIdType.MESH)

        mine = [pltpu.make_async_copy(src_refs[t], out_refs[t].at[index(*me)], local_sems.at[t]) for t in range(n)]
        for cp in mine:
            cp.start()
        first = [copy(1 + j, t, me, (*chip, c), src=src_refs[t]) for j, chip in enumerate(chips) for t in range(n)]
        first += [copy(0, t, me, sibling, src=src_refs[t]) for t in range(n)]
        for cp in first:
            cp.start()
        passed = []
        for j, chip in enumerate(chips):
            for t in range(n):
                copy(1 + j, t, (*chip, c), me).wait_recv()
                cp = copy(4 + j, t, (*chip, c), sibling)
                cp.start()
                passed.append(cp)
        for t in range(n):
            copy(0, t, sibling, me).wait_recv()
        for j, chip in enumerate(chips):
            for t in range(n):
                copy(4 + j, t, (*chip, 1 - c), me).wait_recv()
        for cp in first + passed:
            cp.wait_send()
        for cp in mine:
            cp.wait()

    hbm = pl.BlockSpec(memory_space=pl.ANY)
    return pl.pallas_call(
        body, name=name,
        out_shape=[jax.ShapeDtypeStruct((N_DEV,) + s.shape, s.dtype) for s in srcs],
        in_specs=[hbm] * n, out_specs=[hbm] * n,
        scratch_shapes=[pltpu.SemaphoreType.DMA((7 * n,)), pltpu.SemaphoreType.DMA((7 * n,)),
                        pltpu.SemaphoreType.DMA((n,))],
    )(*srcs)


N_CHIPS = N_DEV // 2


def _sibling_swap(srcs, *, name):
    n = len(srcs)

    def body(*refs):
        src_refs, out_refs = refs[:n], refs[n:2 * n]
        send_sems, recv_sems = refs[2 * n:]
        x, y, c = lax.axis_index("x"), lax.axis_index("y"), lax.axis_index("c")
        sends = []
        for chip in range(N_CHIPS):
            for t in range(n):
                cp = pltpu.make_async_remote_copy(
                    src_ref=src_refs[t].at[chip, 1 - c], dst_ref=out_refs[t].at[chip],
                    send_sem=send_sems.at[chip * n + t], recv_sem=recv_sems.at[chip * n + t],
                    device_id=(x, y, 1 - c), device_id_type=pl.DeviceIdType.MESH)
                cp.start()
                sends.append(cp)
        for cp in sends:
            cp.wait_recv()
        for cp in sends:
            cp.wait_send()

    hbm = pl.BlockSpec(memory_space=pl.ANY)
    return pl.pallas_call(
        body, name=name,
        out_shape=[jax.ShapeDtypeStruct((N_CHIPS,) + s.shape[2:], s.dtype) for s in srcs],
        in_specs=[hbm] * n, out_specs=[hbm] * n,
        scratch_shapes=[pltpu.SemaphoreType.DMA((N_CHIPS * n,)), pltpu.SemaphoreType.DMA((N_CHIPS * n,))],
    )(*srcs)


def _pair_sum(mine, theirs, core, *, name, out_dtype):
    _, _, rows, cols = mine.shape

    def body(core_ref, a_ref, b_ref, o_ref):
        o_ref[...] = (a_ref[0] + b_ref[...]).astype(out_dtype)

    return pl.pallas_call(
        body, name=name,
        grid_spec=pltpu.PrefetchScalarGridSpec(
            num_scalar_prefetch=1, grid=(N_CHIPS,),
            in_specs=[pl.BlockSpec((1, 1, rows, cols), lambda q, core_ref: (q, core_ref[0], 0, 0)),
                      pl.BlockSpec((1, rows, cols), lambda q, core_ref: (q, 0, 0))],
            out_specs=pl.BlockSpec((1, rows, cols), lambda q, core_ref: (q, 0, 0))),
        out_shape=jax.ShapeDtypeStruct((N_CHIPS, rows, cols), out_dtype),
        compiler_params=_cparams(("arbitrary",)),
    )(core, mine, theirs)


def _chip_exchange(srcs, *, name):
    n = len(srcs)

    def body(*refs):
        src_refs, out_refs = refs[:n], refs[n:2 * n]
        send_sems, recv_sems, local_sems = refs[2 * n:]
        x, y, c = lax.axis_index("x"), lax.axis_index("y"), lax.axis_index("c")
        my_chip = 2 * x + y
        mine = [pltpu.make_async_copy(src_refs[t].at[my_chip], out_refs[t].at[my_chip], local_sems.at[t])
                for t in range(n)]
        for cp in mine:
            cp.start()
        sends, arrivals = [], []
        for k in (3, 2, 1):
            px = 1 - x if k & 2 else x
            py = 1 - y if k & 1 else y
            peer_chip = 2 * px + py
            for t in range(n):
                sem = (k - 1) * n + t
                cp = pltpu.make_async_remote_copy(
                    src_ref=src_refs[t].at[peer_chip], dst_ref=out_refs[t].at[my_chip],
                    send_sem=send_sems.at[sem], recv_sem=recv_sems.at[sem],
                    device_id=(px, py, c), device_id_type=pl.DeviceIdType.MESH)
                cp.start()
                sends.append(cp)
                arrivals.append(pltpu.make_async_remote_copy(
                    src_ref=src_refs[t].at[peer_chip], dst_ref=out_refs[t].at[peer_chip],
                    send_sem=send_sems.at[sem], recv_sem=recv_sems.at[sem],
                    device_id=(x, y, c), device_id_type=pl.DeviceIdType.MESH))
        for cp in arrivals:
            cp.wait_recv()
        for cp in sends:
            cp.wait_send()
        for cp in mine:
            cp.wait()

    hbm = pl.BlockSpec(memory_space=pl.ANY)
    return pl.pallas_call(
        body, name=name,
        out_shape=[jax.ShapeDtypeStruct(s.shape, s.dtype) for s in srcs],
        in_specs=[hbm] * n, out_specs=[hbm] * n,
        scratch_shapes=[pltpu.SemaphoreType.DMA((3 * n,)), pltpu.SemaphoreType.DMA((3 * n,)),
                        pltpu.SemaphoreType.DMA((n,))],
    )(*srcs)


def _rope_tables(pos_row, invf_col):
    tm = pos_row.shape[1]
    ang = pos_row.astype(F32) * invf_col
    cos, sin = jnp.cos(ang), jnp.sin(ang)
    ones = lambda n: jnp.ones((n, tm), F32)
    zeros = lambda n: jnp.zeros((n, tm), F32)
    cos_t = jnp.concatenate([ones(KR_LO), cos, cos, ones(LANES - KR_LO - ROPE)], axis=0)
    first_t = jnp.concatenate([zeros(KR_LO), sin, zeros(LANES - KR_LO - HALF)], axis=0)
    second_t = jnp.concatenate([zeros(KR_LO + HALF), sin, zeros(LANES - KR_LO - ROPE)], axis=0)
    return jnp.transpose(cos_t), jnp.transpose(first_t), jnp.transpose(second_t)


def _rope(t, cos, sin_first, sin_second, sign):
    up = pltpu.roll(t, LANES - HALF, 1)
    down = pltpu.roll(t, HALF, 1)
    return t * cos - sign * (up * sin_first) + sign * (down * sin_second)


def _fwd_proj(x, pos_row, invf_col, w_in_shards, w_heads, q_g, kv_g):
    t = x.shape[0]
    tm = PROJ_TILE
    n_steps = t // tm
    n_latent = Q_LORA + KV_LORA + LANES

    def body(x_ref, pos_ref, invf_ref, sh_ref, wh_ref, qg_ref, kvg_ref,
             proj_ref, q_ref, k_ref, v_ref, vt_ref, win_ref, latent_even, latent_odd):
        step = pl.program_id(0)
        latents = (latent_even, latent_odd)

        def project(latent_ref):
            proj = _dot(x_ref[...].astype(BF16), win_ref[...])
            proj_ref[...] = proj
            latent_ref[...] = proj[:, :n_latent]

        def heads(latent_ref):
            c_q = latent_ref[:, :Q_LORA]
            c_kv = latent_ref[:, Q_LORA:Q_LORA + KV_LORA]
            kr_raw = latent_ref[:, Q_LORA + KV_LORA:]
            cqn = (c_q * lax.rsqrt(jnp.mean(c_q * c_q, axis=-1, keepdims=True) + EPS) * qg_ref[...]).astype(BF16)
            ckvn = (c_kv * lax.rsqrt(jnp.mean(c_kv * c_kv, axis=-1, keepdims=True) + EPS) * kvg_ref[...]).astype(BF16)
            cos, s1, s2 = _rope_tables(pos_ref[...], invf_ref[...])
            kr = _rope(kr_raw, cos, s1, s2, 1.0)
            lane = lax.broadcasted_iota(jnp.int32, (tm, HEAD_PAD), 1)
            q_all = _dot(cqn, jnp.concatenate([wh_ref[h, :Q_LORA, :] for h in range(HEADS)], axis=1))
            kv_all = _dot(ckvn, jnp.concatenate([wh_ref[h, Q_LORA:, :] for h in range(HEADS)], axis=1))
            for h in range(HEADS):
                q_h = q_all[:, h * HEAD_PAD:(h + 1) * HEAD_PAD]
                kv_h = kv_all[:, h * HEAD_PAD:(h + 1) * HEAD_PAD]
                q_ref[h] = (_rope(q_h, cos, s1, s2, 1.0) * Q_PRESCALE).astype(BF16)
                k_ref[h] = jnp.where(lane < NOPE, kv_h, kr).astype(BF16)
                v_ref[h] = kv_h.astype(BF16)
                vt_ref[h] = jnp.transpose(jnp.where(lane == SUM_ROW, 1.0, kv_h)).astype(BF16)

        @pl.when(step == 0)
        def _():
            win_ref[...] = jnp.zeros_like(win_ref)
            for s, src, dst, width in _w_in_pieces():
                win_ref[:, dst:dst + width] = sh_ref[s, :, src:src + width]
            project(latents[0])

        for parity in (0, 1):
            @pl.when((step > 0) & (step < n_steps) & (step % 2 == parity))
            def _():
                heads(latents[1 - parity])
                project(latents[parity])

        @pl.when(step == n_steps)
        def _():
            heads(latents[(n_steps - 1) % 2])

    full = lambda a: pl.BlockSpec(a.shape, lambda i: (0,) * a.ndim)
    this = lambda i: jnp.minimum(i, n_steps - 1)
    prev = lambda i: jnp.maximum(i - 1, 0)
    head_spec = pl.BlockSpec((HEADS, tm, HEAD_PAD), lambda i: (0, prev(i), 0))
    head_shape = jax.ShapeDtypeStruct((HEADS, t, HEAD_PAD), BF16)
    return pl.pallas_call(
        body, name="fwd_proj", grid=(n_steps + 1,),
        in_specs=[pl.BlockSpec((tm, D_MODEL), lambda i: (this(i), 0)), pl.BlockSpec((1, tm), lambda i: (0, prev(i))),
                  full(invf_col), full(w_in_shards), full(w_heads), full(q_g), full(kv_g)],
        out_specs=[pl.BlockSpec((tm, D_IN_PAD), lambda i: (this(i), 0)), head_spec, head_spec, head_spec,
                   pl.BlockSpec((HEADS, HEAD_PAD, tm), lambda i: (0, 0, prev(i))),
                   pl.BlockSpec((D_MODEL, D_IN_PAD), lambda i: (0, 0))],
        out_shape=[jax.ShapeDtypeStruct((t, D_IN_PAD), F32), head_shape, head_shape, head_shape,
                   jax.ShapeDtypeStruct((HEADS, HEAD_PAD, t), BF16),
                   jax.ShapeDtypeStruct((D_MODEL, D_IN_PAD), w_in_shards.dtype)],
        scratch_shapes=[pltpu.VMEM((tm, n_latent), F32)] * 2,
        compiler_params=_cparams(("arbitrary",)),
    )(x, pos_row, invf_col, w_in_shards, w_heads, q_g, kv_g)


def _attn_fwd(q, k, vt):
    t = q.shape[1]
    bq, bk = ATTN_FWD_WIDE, ATTN_NARROW
    n_diag = bq // bk
    chunk = SOFTMAX_ROWS

    def body(q_ref, k_ref, vt_ref, o_ref, lse_ref, s0, s1, p0, p1, x0, x1, m_scr, a_scr, acc_scr):
        i = pl.program_id(1)
        at = lambda j: pl.ds(pl.multiple_of(j * bk, bk), bk)

        def exp_pass(s_in, block_max, p_out, diagonal=False, cols=slice(None)):
            width = bq if cols == slice(None) else cols.stop - cols.start

            def load(r):
                s = s_in[r:r + chunk, cols]
                if diagonal:
                    key = lax.broadcasted_iota(jnp.int32, (chunk, width), 0) + r
                    qry = lax.broadcasted_iota(jnp.int32, (chunk, width), 1)
                    s = jnp.where(qry >= key, s, -jnp.inf)
                return s

            if diagonal:
                block_max = jnp.max(load(0), axis=0, keepdims=True)
                for r in range(chunk, bk, chunk):
                    block_max = jnp.maximum(block_max, jnp.max(load(r), axis=0, keepdims=True))
            m_old = m_scr[:, cols]
            m_new = jnp.maximum(m_old, block_max)
            alpha = jnp.exp2(m_old - m_new)
            for r in range(0, bk, chunk):
                p_out[r:r + chunk, cols] = jnp.exp2(load(r) - m_new).astype(BF16)
            m_scr[:, cols] = m_new
            return alpha

        def scores(j, s_out, x_out):
            s = _dot_nt(k_ref[0, at(j), :], q_ref[0])
            s_out[...] = s
            x_out[...] = jnp.max(s, axis=0, keepdims=True)

        def value_product(j, p_in):
            return _dot(vt_ref[0, LIVE_ROWS, at(j)], p_in[...])

        def one_pass(j, s_in, x_in, s_out, x_out, p_prev, p_cur):
            scores(j + 1, s_out, x_out)
            acc_scr[...] = a_scr[...] * acc_scr[...] + value_product(jnp.maximum(j - 1, 0), p_prev)
            a_scr[...] = exp_pass(s_in, x_in[...], p_cur)

        scores(0, s0, x0)
        p1[...] = jnp.zeros_like(p1)
        a_scr[...] = jnp.ones_like(a_scr)
        m_scr[...] = jnp.full(m_scr.shape, -jnp.inf, F32)
        acc_scr[...] = jnp.zeros_like(acc_scr)

        def two_passes(n, _):
            one_pass(2 * n, s0, x0, s1, x1, p1, p0)
            one_pass(2 * n + 1, s1, x1, s0, x0, p0, p1)
            return 0

        lax.fori_loop(0, (n_diag // 2) * i, two_passes, 0)
        d = n_diag * i
        alpha, p_prev, cols = a_scr[...], p1, slice(0, bq)
        for u in range(n_diag + 1):
            s_in, s_next, p_cur = (s0, s1, p0) if u % 2 == 0 else (s1, s0, p1)
            if u + 1 < n_diag:
                ahead = slice((u + 1) * bk, bq)
                s_next[:, ahead] = _dot_nt(k_ref[0, at(d + u + 1), :], q_ref[0, ahead, :])
            acc_scr[:, cols] = alpha * acc_scr[:, cols] + _dot(vt_ref[0, LIVE_ROWS, at(jnp.maximum(d + u - 1, 0))],
                                                               p_prev[:, cols])
            if u < n_diag:
                cols = slice(u * bk, bq)
                alpha = exp_pass(s_in, None, p_cur, diagonal=True, cols=cols)
                p_prev = p_cur
        denom = acc_scr[SUM_ROW - LIVE_ROWS.start:NOPE - LIVE_ROWS.start, :]
        o = jnp.transpose(acc_scr[NOPE - LIVE_ROWS.start:, :] / denom)
        o_ref[0] = jnp.concatenate([jnp.zeros_like(o), o], axis=1)
        lse_ref[0] = m_scr[...] + jnp.log2(denom)

    tile = lambda dtype: pltpu.VMEM((bk, bq), dtype)
    stat = pltpu.VMEM((1, bq), F32)
    return pl.pallas_call(
        body, name="attn_fwd", grid=(HEADS, t // bq),
        in_specs=[pl.BlockSpec((1, bq, HEAD_PAD), lambda h, i: (h, i, 0)),
                  pl.BlockSpec((1, t, HEAD_PAD), lambda h, i: (h, 0, 0)),
                  pl.BlockSpec((1, HEAD_PAD, t), lambda h, i: (h, 0, 0))],
        out_specs=[pl.BlockSpec((1, bq, HEAD_PAD), lambda h, i: (h, i, 0)),
                   pl.BlockSpec((1, 1, bq), lambda h, i: (h, 0, i))],
        out_shape=[jax.ShapeDtypeStruct((HEADS, t, HEAD_PAD), F32), jax.ShapeDtypeStruct((HEADS, 1, t), F32)],
        scratch_shapes=[tile(F32), tile(F32), tile(BF16), tile(BF16), stat, stat, stat, stat,
                        pltpu.VMEM((HEAD_PAD - LIVE_ROWS.start, bq), F32)],
        compiler_params=_cparams(("arbitrary", "arbitrary")),
    )(q, k, vt)


def _mid(x, target, proj, ol, w_out, ws_low, ws_low_t, bsp, sgu_g, sgu_b, ln_g, ln_b):
    t = x.shape[0]
    tm = TOKEN_TILE
    n_steps = t // tm

    def body(x_ref, tgt_ref, za_ref, u_ref, v_ref, zb_ref, ol_ref, prev_za_ref, prev_u_ref, prev_v_ref, prev_zb_ref,
             prev_ol_ref, wout_ref, ws_ref, wst_ref, bsp_ref, sg_ref, sb_ref, lg_ref, lb_ref,
             dr_ref, do_ref, drow_ref, drest_ref, dwout_ref, dws_ref, dbs_ref, dlg_ref, dlb_ref, dsg_ref, dsb_ref,
             loss_ref, dbsp_acc, *kept_refs):
        step = pl.program_id(0)
        kept_sets = (kept_refs[:len(kept_refs) // 2], kept_refs[len(kept_refs) // 2:])

        @pl.when(step == 0)
        def _():
            dwout_ref[...] = jnp.zeros_like(dwout_ref)
            dws_ref[...] = jnp.zeros_like(dws_ref)
            dbs_ref[...] = jnp.zeros_like(dbs_ref)
            dlg_ref[...] = jnp.zeros_like(dlg_ref)
            dlb_ref[...] = jnp.zeros_like(dlb_ref)
            dsg_ref[...] = jnp.zeros_like(dsg_ref)
            dsb_ref[...] = jnp.zeros_like(dsb_ref)
            loss_ref[...] = jnp.zeros_like(loss_ref)
            dbsp_acc[...] = jnp.zeros_like(dbsp_acc)

        n_chunks = tm // CHUNK
        groups = G_WIDTH // LANES

        def side_by_side(a):
            return [jnp.concatenate([a[c * CHUNK:(c + 1) * CHUNK, g * LANES:(g + 1) * LANES] for c in range(n_chunks)],
                                    axis=1) for g in range(groups)]

        def by_chunk(wide):
            return jnp.concatenate([jnp.concatenate([wide[g][:, c * LANES:(c + 1) * LANES] for g in range(groups)], axis=1)
                                    for c in range(n_chunks)], axis=0)

        def own_lanes(h):
            lane = lax.broadcasted_iota(jnp.int32, (CHUNK, n_chunks * LANES), 1)
            return (lane % LANES) // G_HEAD_DIM == h % 2

        def spatial(w_ref, wide):
            return [sum(jnp.where(own_lanes(h), _dot(w_ref[h], wide[g]), 0.0) for h in (2 * g, 2 * g + 1))
                    for g in range(groups)]

        def value_lanes(o_ref):
            return jnp.concatenate([o_ref[h][:, NOPE:] for h in range(HEADS)], axis=-1)

        def forward(kept):
            attn = value_lanes(ol_ref)
            za = za_ref[...]
            sig_a = _sigmoid(za)
            out_a = attn * (za * sig_a)
            u = u_ref[...]
            cdf_u = _normal_cdf(u)
            vpre = v_ref[...]
            cdf_v = _normal_cdf(vpre)
            gv = vpre * cdf_v
            mu_v = jnp.mean(gv, axis=-1, keepdims=True)
            cen_v = gv - mu_v
            rstd_v = lax.rsqrt(jnp.mean(cen_v * cen_v, axis=-1, keepdims=True) + EPS)
            vhat = cen_v * rstd_v
            vg = vhat * sg_ref[...] + sb_ref[...]
            vg_b = vg.astype(BF16)
            yield
            sv = by_chunk(spatial(ws_ref, side_by_side(vg_b))) + jnp.tile(bsp_ref[...], (n_chunks, 1))
            zb = zb_ref[...]
            sig_b = _sigmoid(zb)
            out_b = ((u * cdf_u) * sv) * (zb * sig_b)
            merged = jnp.concatenate([out_a, out_b], axis=-1).astype(BF16)
            yield
            r = DN_ALPHA * x_ref[...] + _dot(merged, wout_ref[...])
            mu = jnp.mean(r, axis=-1, keepdims=True)
            cen = r - mu
            rstd = lax.rsqrt(jnp.mean(cen * cen, axis=-1, keepdims=True) + EPS)
            xhat = cen * rstd
            hout = xhat * lg_ref[...] + lb_ref[...]
            err = hout - tgt_ref[...]
            row_loss = jnp.mean(err * err, axis=-1, keepdims=True)
            loss_ref[...] += jnp.broadcast_to(0.5 * jnp.sum(row_loss, axis=0, keepdims=True), loss_ref.shape)
            for ref, val in zip(kept, (sig_a, cdf_u, cdf_v, vhat, sv, sig_b, xhat, err * (1.0 / D_MODEL), merged, vg_b,
                                       jnp.broadcast_to(rstd, (tm, LANES)), jnp.broadcast_to(rstd_v, (tm, LANES)))):
                ref[...] = val

        def backward(kept):
            (sig_a_ref, cdf_u_ref, cdf_v_ref, vhat_ref, sv_ref, sig_b_ref, xhat_ref, dh_ref, merged_ref, vg_ref,
             rstd_ref, rstd_v_ref) = kept
            attn = value_lanes(prev_ol_ref)
            za, u, vpre, zb = prev_za_ref[...], prev_u_ref[...], prev_v_ref[...], prev_zb_ref[...]
            sig_a, cdf_u, cdf_v, vhat, sv, sig_b = (sig_a_ref[...], cdf_u_ref[...], cdf_v_ref[...], vhat_ref[...],
                                                    sv_ref[...], sig_b_ref[...])
            xhat, dh, merged, vg_b = xhat_ref[...], dh_ref[...], merged_ref[...], vg_ref[...]
            rstd, rstd_v = rstd_ref[:, :1], rstd_v_ref[:, :1]
            silu_a, silu_b, ug = za * sig_a, zb * sig_b, u * cdf_u
            sgu = ug * sv
            dlg_ref[...] += jnp.sum(dh * xhat, axis=0, keepdims=True)
            dlb_ref[...] += jnp.sum(dh, axis=0, keepdims=True)
            dxhat = dh * lg_ref[...]
            dr = rstd * (dxhat - jnp.mean(dxhat, axis=-1, keepdims=True)
                         - xhat * jnp.mean(dxhat * xhat, axis=-1, keepdims=True))
            dr_ref[...] = dr
            dr_b = dr.astype(BF16)
            yield
            dwout_ref[...] += _dot_tn(merged, dr_b)
            dmerged = _dot_nt(dr_b, wout_ref[...])
            yield
            d_out_a = dmerged[:, :G_WIDTH]
            d_out_b = dmerged[:, G_WIDTH:]
            dattn = d_out_a * silu_a
            for h in range(HEADS):
                do_h = dattn[:, h * VDIM:(h + 1) * VDIM]
                do_ref[h] = jnp.concatenate([jnp.zeros((tm, NOPE), F32), do_h], axis=-1).astype(BF16)
            feature = lax.broadcasted_iota(jnp.int32, (G_WIDTH, LANES), 0) // VDIM
            column = lax.broadcasted_iota(jnp.int32, (G_WIDTH, LANES), 1)
            head_sums = jnp.dot(dattn * attn, jnp.where(feature == column, 1.0, 0.0).astype(F32),
                                preferred_element_type=F32, precision=lax.Precision.HIGH)
            dsums_t = jnp.transpose(head_sums)
            for h in range(HEADS):
                drow_ref[h] = dsums_t[h:h + 1, :]
            dza = d_out_a * attn * (sig_a * (1.0 + za * (1.0 - sig_a)))
            dsgu = d_out_b * silu_b
            dzb = d_out_b * sgu * (sig_b * (1.0 + zb * (1.0 - sig_b)))
            du = dsgu * sv * _gelu_grad(u, cdf_u)
            dsv = dsgu * ug
            dsv_b = dsv.astype(BF16)
            for cix in range(n_chunks):
                dbsp_acc[...] += dsv[cix * CHUNK:(cix + 1) * CHUNK, :]
            yield
            dsv_wide, vg_wide = side_by_side(dsv_b), side_by_side(vg_b)
            dvg = by_chunk(spatial(wst_ref, dsv_wide))
            for h in range(HEADS):
                mine = jnp.where(own_lanes(h), dsv_wide[h // 2], jnp.zeros_like(dsv_wide[h // 2]))
                dws_ref[h] += _dot_nt(mine, vg_wide[h // 2])
            dsg_ref[...] += jnp.sum(dvg * vhat, axis=0, keepdims=True)
            dsb_ref[...] += jnp.sum(dvg, axis=0, keepdims=True)
            dvhat = dvg * sg_ref[...]
            dgv = rstd_v * (dvhat - jnp.mean(dvhat, axis=-1, keepdims=True)
                            - vhat * jnp.mean(dvhat * vhat, axis=-1, keepdims=True))
            dv = dgv * _gelu_grad(vpre, cdf_v)
            drest_ref[...] = jnp.concatenate([dza, du, dv, dzb], axis=-1).astype(BF16)

        def emit(order, **stages):
            for who in order:
                next(stages[who], None)

        @pl.when(step == 0)
        def _():
            emit("fff", f=forward(kept_sets[0]))

        for parity in (0, 1):
            @pl.when((step > 0) & (step < n_steps) & (step % 2 == parity))
            def _():
                emit("ffbbfbb", f=forward(kept_sets[parity]), b=backward(kept_sets[1 - parity]))

        @pl.when(step == n_steps)
        def _():
            emit("bbbb", b=backward(kept_sets[(n_steps - 1) % 2]))
            tri = (lax.broadcasted_iota(jnp.int32, (CHUNK, CHUNK), 0)
                   >= lax.broadcasted_iota(jnp.int32, (CHUNK, CHUNK), 1))
            for h in range(HEADS):
                dws_ref[h] = jnp.where(tri, dws_ref[h], 0.0)
            tot = dbsp_acc[...]
            lane = lax.broadcasted_iota(jnp.int32, (CHUNK, LANES), 1)
            dbs = jnp.zeros((CHUNK, LANES), F32)
            for h in range(HEADS):
                head_sum = jnp.sum(tot[:, h * G_HEAD_DIM:(h + 1) * G_HEAD_DIM], axis=-1, keepdims=True)
                dbs = jnp.where(lane == h, head_sum, dbs)
            dbs_ref[...] = dbs

    full = lambda a: pl.BlockSpec(a.shape, lambda i: (0,) * a.ndim)
    this = lambda i: jnp.minimum(i, n_steps - 1)
    prev = lambda i: jnp.maximum(i - 1, 0)
    tile = lambda w, j=0, at=this: pl.BlockSpec((tm, w), lambda i, j=j: (at(i), j))
    heads = lambda at: pl.BlockSpec((HEADS, tm, HEAD_PAD), lambda i: (0, at(i), 0))
    acc = lambda shape: (pl.BlockSpec(shape, lambda i: (0,) * len(shape)), jax.ShapeDtypeStruct(shape, F32))
    accs = [acc((D_MODEL, D_MODEL)), acc((HEADS, CHUNK, CHUNK)), acc((CHUNK, LANES)), acc((1, D_MODEL)),
            acc((1, D_MODEL)), acc((1, G_WIDTH)), acc((1, G_WIDTH)), acc((1, LANES))]
    kept = ([pltpu.VMEM((tm, G_WIDTH), F32)] * 6 + [pltpu.VMEM((tm, D_MODEL), F32)] * 2
            + [pltpu.VMEM((tm, D_MODEL), BF16), pltpu.VMEM((tm, G_WIDTH), BF16)] + [pltpu.VMEM((tm, LANES), F32)] * 2)
    return pl.pallas_call(
        body, name="mid", grid=(n_steps + 1,),
        in_specs=[tile(D_MODEL), tile(D_MODEL), tile(G_WIDTH, 1), tile(G_WIDTH, 2), tile(G_WIDTH, 3), tile(G_WIDTH, 4),
                  heads(this), tile(G_WIDTH, 1, prev), tile(G_WIDTH, 2, prev), tile(G_WIDTH, 3, prev),
                  tile(G_WIDTH, 4, prev), heads(prev),
                  full(w_out), full(ws_low), full(ws_low_t), full(bsp), full(sgu_g), full(sgu_b),
                  full(ln_g), full(ln_b)],
        out_specs=[tile(D_MODEL, 0, prev), heads(prev), pl.BlockSpec((HEADS, 1, tm), lambda i: (0, 0, prev(i))),
                   tile(4 * G_WIDTH, 0, prev)]
        + [a[0] for a in accs],
        out_shape=[jax.ShapeDtypeStruct((t, D_MODEL), F32), jax.ShapeDtypeStruct((HEADS, t, HEAD_PAD), BF16),
                   jax.ShapeDtypeStruct((HEADS, 1, t), F32), jax.ShapeDtypeStruct((t, 4 * G_WIDTH), BF16)]
        + [a[1] for a in accs],
        scratch_shapes=[pltpu.VMEM((CHUNK, G_WIDTH), F32)] + kept + kept,
        compiler_params=_cparams(("arbitrary",)),
    )(x, target, proj, proj, proj, proj, ol, proj, proj, proj, proj, ol,
      w_out, ws_low, ws_low_t, bsp, sgu_g, sgu_b, ln_g, ln_b)


def _attn_bwd(q, k, v, do, lse_row, d_row):
    t = q.shape[1]
    bk, bq = ATTN_BWD_WIDE, ATTN_NARROW
    n_diag = bk // bq
    half = bq // 2
    last = t // bq - 1
    chunk = SOFTMAX_ROWS

    def body(q_ref, k_ref, v_ref, do_ref, lse_ref, drow_ref, dqt_ref, dk_ref, dv_ref,
             s0, s1, e0, e1, p0, p1, g0, g1, kt_scr):
        j = pl.program_id(1)
        at = lambda i: pl.ds(pl.multiple_of(i * bq, bq), bq)

        @pl.when(j == 0)
        def _():
            dqt_ref[...] = jnp.zeros_like(dqt_ref)

        kt_scr[...] = jnp.transpose(k_ref[0].astype(F32)).astype(BF16)
        dk_ref[...] = jnp.zeros_like(dk_ref)
        dv_ref[...] = jnp.zeros_like(dv_ref)

        whole_tile = ((slice(0, bk), slice(0, bq)),)

        def queries(i, lanes):
            return pl.ds(pl.multiple_of(i * bq + lanes.start, half), lanes.stop - lanes.start)

        def products(i, s_out, e_out, areas=whole_tile):
            i = jnp.minimum(i, last)
            for keys, lanes in areas:
                s_out[keys, lanes] = _dot_nt(k_ref[0, keys, :], q_ref[0, queries(i, lanes), :])
                e_out[keys, lanes] = _dot_nt(v_ref[0, keys, :], do_ref[0, queries(i, lanes), :])

        def gradients(i, p_in, g_in, areas=whole_tile):
            for keys, lanes in areas:
                dv_ref[0, keys, :] += _dot(p_in[keys, lanes], do_ref[0, queries(i, lanes), :])
                dk_ref[0, keys, :] += _dot(g_in[keys, lanes], q_ref[0, queries(i, lanes), :])
                dqt_ref[0, :, queries(i, lanes)] += _dot(kt_scr[:, keys], g_in[keys, lanes])

        def elementwise(i, s_in, e_in, p_out, g_out, qry0=None, areas=whole_tile):
            for keys, lanes in areas:
                width = lanes.stop - lanes.start
                step = chunk if qry0 is None else half
                lse = lse_ref[0, :, queries(i, lanes)]
                dsum = drow_ref[0, :, queries(i, lanes)]
                for r in range(keys.start, keys.stop, step):
                    p = jnp.exp2(s_in[r:r + step, lanes] - lse)
                    if qry0 is not None:
                        key = lax.broadcasted_iota(jnp.int32, (step, width), 0) + r
                        qry = lax.broadcasted_iota(jnp.int32, (step, width), 1) + (qry0 + lanes.start)
                        p = jnp.where(qry >= key, p, 0.0)
                    p_out[r:r + step, lanes] = p.astype(BF16)
                    g_out[r:r + step, lanes] = (p * (e_in[r:r + step, lanes] - dsum)).astype(BF16)

        def one_pass(i, s_in, e_in, s_out, e_out, p_prev, g_prev, p_cur, g_cur):
            products(i + 1, s_out, e_out)
            gradients(i - 1, p_prev, g_prev)
            elementwise(i, s_in, e_in, p_cur, g_cur)

        first = n_diag * j

        def areas_of(u):
            if u >= n_diag:
                return whole_tile
            return ((slice(0, u * bq + half), slice(0, bq)), (slice(u * bq + half, (u + 1) * bq), slice(half, bq)))

        even, odd = (s0, e0, p0, g0), (s1, e1, p1, g1)
        products(first, s0, e0, areas_of(0))
        products(first + 1, s1, e1, areas_of(1))
        elementwise(first, s0, e0, p0, g0, qry0=0, areas=areas_of(0))
        for u in range(1, n_diag):
            (s_in, e_in, p_cur, g_cur), (s_out, e_out, p_prev, g_prev) = (odd, even) if u % 2 else (even, odd)
            products(first + u + 1, s_out, e_out, areas_of(u + 1))
            gradients(first + u - 1, p_prev, g_prev, areas_of(u - 1))
            elementwise(first + u, s_in, e_in, p_cur, g_cur, qry0=u * bq, areas=areas_of(u))
        corner = (slice(bk - half, bk), slice(0, half))
        p1[corner] = jnp.zeros((half, half), BF16)
        g1[corner] = jnp.zeros((half, half), BF16)

        def two_passes(n, _):
            i = first + n_diag + 2 * n
            one_pass(i, s0, e0, s1, e1, p1, g1, p0, g0)
            one_pass(i + 1, s1, e1, s0, e0, p0, g0, p1, g1)
            return 0

        lax.fori_loop(0, (last - first - n_diag + 1) // 2, two_passes, 0)
        gradients(last, p1, g1)
        dk_ref[0] = dk_ref[0] * LN2

    whole = pl.BlockSpec((1, t, HEAD_PAD), lambda h, j: (h, 0, 0))
    block = pl.BlockSpec((1, bk, HEAD_PAD), lambda h, j: (h, j, 0))
    rows = pl.BlockSpec((1, 1, t), lambda h, j: (h, 0, 0), pipeline_mode=pl.Buffered(1))
    shape = jax.ShapeDtypeStruct((HEADS, t, HEAD_PAD), F32)
    tile = lambda dtype: pltpu.VMEM((bk, bq), dtype)
    return pl.pallas_call(
        body, name="attn_bwd", grid=(HEADS, t // bk),
        in_specs=[whole, block, block, whole, rows, rows],
        out_specs=[pl.BlockSpec((1, HEAD_PAD, t), lambda h, j: (h, 0, 0)), block, block],
        out_shape=[jax.ShapeDtypeStruct((HEADS, HEAD_PAD, t), F32), shape, shape],
        scratch_shapes=[tile(F32), tile(F32), tile(F32), tile(F32), tile(BF16), tile(BF16),
                        tile(BF16), tile(BF16), pltpu.VMEM((HEAD_PAD, bk), BF16)],
        compiler_params=_cparams(("arbitrary", "arbitrary"), vmem_limit=ATTN_BWD_VMEM_LIMIT),
    )(q, k, v, do, lse_row, d_row)


def _bwd_tail(dq, dk, dv, proj, pos_row, invf_col, w_heads, q_g, kv_g, x, dr, drest, wp_in):
    t = proj.shape[0]
    tm = PROJ_TILE
    n_head = 4 * LANES

    def body(dq_ref, dk_ref, dv_ref, ph_ref, pos_ref, invf_ref, wh_ref, qg_ref, kvg_ref,
             x_ref, dr_ref, drest_ref, win_ref,
             gx_ref, dwin_ref, dwh_ref, dqg_ref, dkvg_ref):
        @pl.when(pl.program_id(0) == 0)
        def _():
            dwin_ref[...] = jnp.zeros_like(dwin_ref)
            dwh_ref[...] = jnp.zeros_like(dwh_ref)
            dqg_ref[...] = jnp.zeros_like(dqg_ref)
            dkvg_ref[...] = jnp.zeros_like(dkvg_ref)

        xb = x_ref[...].astype(BF16)
        dr_b = drest_ref[...]
        dwin_ref[:, n_head:] += _dot_tn(xb, dr_b)
        gx_rest = DN_ALPHA * dr_ref[...] + _dot_nt(dr_b, win_ref[:, n_head:])

        cos, s1, s2 = _rope_tables(pos_ref[...], invf_ref[...])
        lane = lax.broadcasted_iota(jnp.int32, (tm, LANES), 1)
        c_q = ph_ref[:, :Q_LORA]
        c_kv = ph_ref[:, Q_LORA:Q_LORA + KV_LORA]
        rstd_q = lax.rsqrt(jnp.mean(c_q * c_q, axis=-1, keepdims=True) + EPS)
        rstd_kv = lax.rsqrt(jnp.mean(c_kv * c_kv, axis=-1, keepdims=True) + EPS)
        qhat = c_q * rstd_q
        kvhat = c_kv * rstd_kv
        cqn = (qhat * qg_ref[...]).astype(BF16)
        ckvn = (kvhat * kvg_ref[...]).astype(BF16)
        dkr_rot = jnp.zeros((tm, LANES), F32)
        dq_heads, dkv_heads = [], []
        for h in range(HEADS):
            dq_heads.append(_rope(jnp.transpose(dq_ref[h]) * ATTN_SCALE, cos, s1, s2, -1.0).astype(BF16))
            dk_h = dk_ref[h]
            dkv_heads.append(jnp.where(lane < NOPE, dk_h, dv_ref[h]).astype(BF16))
            dkr_rot = dkr_rot + dk_h
        dq_all = jnp.concatenate(dq_heads, axis=1)
        dkv_all = jnp.concatenate(dkv_heads, axis=1)
        dwq_all = _dot_tn(cqn, dq_all)
        dwkv_all = _dot_tn(ckvn, dkv_all)
        for h in range(HEADS):
            dwh_ref[h, :Q_LORA, :] += dwq_all[:, h * HEAD_PAD:(h + 1) * HEAD_PAD]
            dwh_ref[h, Q_LORA:, :] += dwkv_all[:, h * HEAD_PAD:(h + 1) * HEAD_PAD]
        dcqn = _dot_nt(dq_all, jnp.concatenate([wh_ref[h, :Q_LORA, :] for h in range(HEADS)], axis=1))
        dckvn = _dot_nt(dkv_all, jnp.concatenate([wh_ref[h, Q_LORA:, :] for h in range(HEADS)], axis=1))
        rot_lanes = (lane >= KR_LO) & (lane < KR_LO + ROPE)
        dkr_raw = jnp.where(rot_lanes, _rope(dkr_rot, cos, s1, s2, -1.0), 0.0)
        dqg_ref[...] += jnp.sum(dcqn * qhat, axis=0, keepdims=True)
        dkvg_ref[...] += jnp.sum(dckvn * kvhat, axis=0, keepdims=True)
        dqh = dcqn * qg_ref[...]
        dkvh = dckvn * kvg_ref[...]
        dc_q = rstd_q * (dqh - qhat * jnp.mean(dqh * qhat, axis=-1, keepdims=True))
        dc_kv = rstd_kv * (dkvh - kvhat * jnp.mean(dkvh * kvhat, axis=-1, keepdims=True))
        dh_b = jnp.concatenate([dc_q, dc_kv, dkr_raw], axis=-1).astype(BF16)
        dwin_ref[:, :n_head] += _dot_tn(xb, dh_b)
        gx_ref[...] = gx_rest + _dot_nt(dh_b, win_ref[:, :n_head])

    full = lambda a: pl.BlockSpec(a.shape, lambda i: (0,) * a.ndim)
    tile = lambda w: pl.BlockSpec((tm, w), lambda i: (i, 0))
    heads = pl.BlockSpec((HEADS, tm, HEAD_PAD), lambda i: (0, i, 0))
    acc = lambda shape: (pl.BlockSpec(shape, lambda i: (0,) * len(shape)), jax.ShapeDtypeStruct(shape, F32))
    accs = [acc(wp_in.shape), acc(w_heads.shape), acc((1, Q_LORA)), acc((1, KV_LORA))]
    return pl.pallas_call(
        body, name="bwd_tail", grid=(t // tm,),
        in_specs=[pl.BlockSpec((HEADS, HEAD_PAD, tm), lambda i: (0, 0, i)), heads, heads, tile(n_head),
                  pl.BlockSpec((1, tm), lambda i: (0, i)), full(invf_col), full(w_heads), full(q_g), full(kv_g),
                  tile(D_MODEL), tile(D_MODEL), tile(drest.shape[1]), full(wp_in)],
        out_specs=[tile(D_MODEL)] + [a[0] for a in accs],
        out_shape=[jax.ShapeDtypeStruct((t, D_MODEL), F32)] + [a[1] for a in accs],
        compiler_params=_cparams(("arbitrary",), vmem_limit=BWD_TAIL_VMEM_LIMIT),
    )(dq, dk, dv, proj, pos_row, invf_col, w_heads, q_g, kv_g, x, dr, drest, wp_in)


def _adam_update(g, w, m, v):
    m_new = ADAM_B1 * m + (1.0 - ADAM_B1) * g
    v_new = ADAM_B2 * v + (1.0 - ADAM_B2) * (g * g)
    m_hat = m_new / (1.0 - ADAM_B1 ** ADAM_STEP)
    v_hat = v_new / (1.0 - ADAM_B2 ** ADAM_STEP)
    return -ADAM_LR * (m_hat / (jnp.sqrt(v_hat) + ADAM_EPS) + ADAM_WD * w), m_new, v_new


def _adam(parts, w, m, v, *, name, tile_rows, transposed=False):
    n, rows, cols = parts.shape
    lane_pad = -(-cols // LANES) * LANES
    own_rows = rows if transposed else w.shape[0]
    assert own_rows == rows or tile_rows == rows

    def body(p_ref, w_ref, m_ref, v_ref, g_ref, d_ref, nm_ref, nv_ref, *scratch):
        g = p_ref[0].astype(F32)
        for s in range(1, n):
            g = g + p_ref[s].astype(F32)
        if transposed:
            wide_ref, = scratch
            wide_ref[:, lane_pad - LANES:] = jnp.zeros((tile_rows, LANES), F32)
            wide_ref[:, :cols] = g
            g = jnp.transpose(wide_ref[...])[:cols]
        g_ref[...] = g
        d_ref[...], nm_ref[...], nv_ref[...] = _adam_update(g[:w_ref.shape[0]], w_ref[...], m_ref[...], v_ref[...])

    if transposed:
        flat = grad = pl.BlockSpec((cols, tile_rows), lambda i: (0, i))
        shape = grad_shape = jax.ShapeDtypeStruct((cols, rows), F32)
        scratch = [pltpu.VMEM((tile_rows, lane_pad), F32)]
    else:
        own_tile = min(tile_rows, own_rows)
        flat = pl.BlockSpec((own_tile, cols), lambda i: (i, 0))
        grad = pl.BlockSpec((tile_rows, cols), lambda i: (i, 0))
        shape, grad_shape = jax.ShapeDtypeStruct((own_rows, cols), F32), jax.ShapeDtypeStruct((rows, cols), F32)
        scratch = []
    return pl.pallas_call(
        body, name=name, grid=(rows // tile_rows,),
        in_specs=[pl.BlockSpec((n, tile_rows, cols), lambda i: (0, i, 0)), flat, flat, flat],
        out_specs=[grad, flat, flat, flat], out_shape=[grad_shape, shape, shape, shape], scratch_shapes=scratch,
        compiler_params=_cparams(("arbitrary",)),
    )(parts, w, m, v)


def _adam_replicated(rep_g, ws, ms, vs):
    count = len(ws)
    small_rows = REP_ROWS - CHUNK

    def body(g_ref, *refs):
        w_refs, m_refs, v_refs = refs[:count], refs[count:2 * count], refs[2 * count:3 * count]
        outs, last_ref, slab_ref = refs[3 * count:7 * count], refs[7 * count], refs[7 * count + 1]
        for d in range(N_DEV):
            slab_ref[d * small_rows:(d + 1) * small_rows, :] = g_ref[d, CHUNK:, :]
        last_ref[...] = slab_ref[N_DEV * small_rows - 1:, LANES - 1:]
        at = 0
        for k, w_ref in enumerate(w_refs):
            if w_ref.ndim == 3:
                g = g_ref[:, :CHUNK, :]
            else:
                n_rows = w_ref.size // LANES
                g = slab_ref[at:at + n_rows, :].reshape(w_ref.shape)
                at += n_rows
            delta, m_new, v_new = _adam_update(g, w_ref[...], m_refs[k][...], v_refs[k][...])
            for which, val in enumerate((g, delta, m_new, v_new)):
                outs[which * count + k][...] = val

    shapes = [jax.ShapeDtypeStruct(w.shape, F32) for w in ws]
    res = pl.pallas_call(
        body, name="adam_rep", out_shape=shapes * 4 + [jax.ShapeDtypeStruct((1, 1), F32)],
        scratch_shapes=[pltpu.VMEM((N_DEV * small_rows, LANES), F32)],
        compiler_params=_cparams(),
    )(rep_g, *ws, *ms, *vs)
    return [res[which * count:(which + 1) * count] for which in range(4)], res[4 * count]


def _pack_small(vals, last):
    flat = jnp.concatenate([v.reshape(-1) for v in vals])
    pad = SMALL_LEN - flat.shape[0]
    return jnp.concatenate([flat, jnp.zeros((pad - 1,), F32), last.reshape(1)])


UQ_SHARD = HEADS * (NOPE + ROPE) // N_DEV
HEAD_ROWS = Q_LORA + KV_LORA
MIXED_ROWS = HEAD_ROWS + CHUNK + SMALL_LEN // N_DEV // LANES


def _head_slab(w_uq_shard, w_ukv_shard):
    return jnp.concatenate([jnp.pad(w_uq_shard, ((0, 0), (0, LANES - UQ_SHARD))), w_ukv_shard])


IN_SHARD = D_IN // N_DEV


def _w_in_pieces():
    split = Q_LORA + KV_LORA
    moves = ((0, split, 0), (split, split + ROPE, KR_LO), (split + ROPE, D_IN, LANES - ROPE))
    pieces = []
    for s in range(N_DEV):
        lo, hi = s * IN_SHARD, (s + 1) * IN_SHARD
        for a, b, shift in moves:
            a, b = max(a, lo), min(b, hi)
            if a < b:
                pieces.append((s, a - lo, a + shift, b - a))
    return pieces


def _w_in_shards(dwp_in):
    tr = TOKEN_TILE
    by_shard = [[p for p in _w_in_pieces() if p[0] == s] for s in range(N_DEV)]

    def body(w_ref, o_ref):
        for s, pieces in enumerate(by_shard):
            parts = [w_ref[:, dst:dst + width] for _, _, dst, width in pieces]
            o_ref[s] = parts[0] if len(parts) == 1 else jnp.concatenate(parts, axis=1)

    return pl.pallas_call(
        body, name="w_in_split", grid=(D_MODEL // tr,),
        in_specs=[pl.BlockSpec((tr, D_IN_PAD), lambda i: (i, 0))],
        out_specs=pl.BlockSpec((N_DEV, tr, IN_SHARD), lambda i: (0, i, 0)),
        out_shape=jax.ShapeDtypeStruct((N_DEV, D_MODEL, IN_SHARD), dwp_in.dtype),
        compiler_params=_cparams(("arbitrary",)),
    )(dwp_in)


def kernel(x, positions, w_in, q_norm_g, w_uq, kv_norm_g, w_ukv, sgu_norm_g, sgu_norm_b, w_spatial, b_spatial, w_out, ln_g, ln_b, loss_target, m_w_in, m_q_norm_g, m_w_uq, m_kv_norm_g, m_w_ukv, m_sgu_norm_g, m_sgu_norm_b, m_w_spatial, m_b_spatial, m_w_out, m_ln_g, m_ln_b, v_w_in, v_q_norm_g, v_w_uq, v_kv_norm_g, v_w_ukv, v_sgu_norm_g, v_sgu_norm_b, v_w_spatial, v_b_spatial, v_w_out, v_ln_g, v_ln_b):
    seq = x.shape[1]
    x2 = x.reshape(seq, D_MODEL)
    tgt2 = loss_target.reshape(seq, D_MODEL)
    pos_row = positions.reshape(1, seq)

    w_in_shards, w_out_shards, w_heads = _gather_two_level(
        [w_in.astype(BF16), w_out.astype(BF16), _head_slab(w_uq, w_ukv).astype(BF16)],
        name="wgather")
    (loss_part, grad_x, d_in, d_heads, d_out, d_ws, d_bs_t, d_lng, d_lnb, d_sgug, d_sgub, d_qg, d_kvg) = _local_step(
        x2, tgt2, pos_row, w_in_shards, w_heads, w_out_shards.reshape(D_MODEL, D_MODEL), q_norm_g, kv_norm_g,
        sgu_norm_g, sgu_norm_b, w_spatial, b_spatial, ln_g, ln_b)

    small_part = _pack_small([d_qg, d_kvg, d_sgug, d_sgub, d_bs_t[:, :HEADS].T, d_lng, d_lnb], last=loss_part[0, :1])
    mixed = jnp.concatenate([d_heads, d_ws, small_part.reshape(N_DEV, -1, LANES)], axis=1)
    by_chip = [g.reshape((N_CHIPS, 2) + g.shape[1:])
               for g in (d_in, d_out.reshape(N_DEV, D_MODEL // N_DEV, D_MODEL), mixed)]
    from_sibling = _sibling_swap(by_chip, name="gswap")
    core = lax.axis_index("c").astype(jnp.int32).reshape(1)
    pair_sums = [_pair_sum(a, b, core, name=nm, out_dtype=dt) for a, b, nm, dt in zip(
        by_chip, from_sibling, ("gsum_in", "gsum_out", "gsum_mixed"), (BF16, BF16, F32))]
    recv_in, recv_out, recv_mixed = _chip_exchange(pair_sums, name="gexch")

    res_in = [a.T for a in _adam(recv_in, w_in.T, m_w_in.T, v_w_in.T, name="adam_in", tile_rows=PROJ_TILE,
                                 transposed=True)]
    res_out = _adam(recv_out, w_out, m_w_out, v_w_out, name="adam_out", tile_rows=D_MODEL // N_DEV)
    res_mixed = _adam(recv_mixed, _head_slab(w_uq, w_ukv), _head_slab(m_w_uq, m_w_ukv), _head_slab(v_w_uq, v_w_ukv),
                      name="adam_mixed", tile_rows=MIXED_ROWS)

    rep_g, = _gather_direct([res_mixed[0]], name="sgather", first_row=HEAD_ROWS)
    res_rep, loss = _adam_replicated(
        rep_g,
        [q_norm_g, kv_norm_g, sgu_norm_g, sgu_norm_b, w_spatial, b_spatial, ln_g, ln_b],
        [m_q_norm_g, m_kv_norm_g, m_sgu_norm_g, m_sgu_norm_b, m_w_spatial, m_b_spatial, m_ln_g, m_ln_b],
        [v_q_norm_g, v_kv_norm_g, v_sgu_norm_g, v_sgu_norm_b, v_w_spatial, v_b_spatial, v_ln_g, v_ln_b])

    def ordered(which):
        r_qg, r_kvg, r_sg, r_sb, r_ws, r_bs, r_lg, r_lb = res_rep[which]
        heads = res_mixed[which]
        return [res_in[which], r_qg, heads[:Q_LORA, :UQ_SHARD], r_kvg, heads[Q_LORA:HEAD_ROWS], r_sg, r_sb, r_ws, r_bs,
                res_out[which], r_lg, r_lb]

    outs = [loss.reshape(()), grad_x.reshape(x.shape)]
    for which in range(4):
        outs += ordered(which)
    return tuple(outs)


def _local_step(x2, tgt2, pos_row, w_in_shards, w_heads, w_out_full, q_norm_g, kv_norm_g, sgu_norm_g, sgu_norm_b,
                w_spatial, b_spatial, ln_g, ln_b):
    half = jnp.arange(HALF, dtype=F32)
    invf_col = (1.0 / (ROPE_THETA ** (half / HALF))).reshape(HALF, 1)
    tri = jnp.tril(jnp.ones((CHUNK, CHUNK), dtype=bool))
    ws_low = jnp.where(tri[None], w_spatial, 0.0).astype(BF16)
    ws_low_t = ws_low.transpose(0, 2, 1)
    bsp = jnp.repeat(b_spatial.T, G_HEAD_DIM, axis=1)
    row = lambda a: a.reshape(1, -1)

    proj, q, k, v, vt, wp_in = _fwd_proj(x2, pos_row, invf_col, w_in_shards, w_heads, row(q_norm_g), row(kv_norm_g))
    o, lse_row = _attn_fwd(q, k, vt)
    (dr, do, d_row, drest, d_out, d_ws, d_bs_t, d_lng, d_lnb, d_sgug, d_sgub, loss_part) = _mid(
        x2, tgt2, proj, o, w_out_full, ws_low, ws_low_t, bsp, row(sgu_norm_g), row(sgu_norm_b), row(ln_g), row(ln_b))
    dqt, dk, dv = _attn_bwd(q, k, v, do, lse_row, d_row)
    grad_x, dwp_in, d_heads, d_qg, d_kvg = _bwd_tail(dqt, dk, dv, proj, pos_row, invf_col, w_heads, row(q_norm_g),
                                                      row(kv_norm_g), x2, dr, drest, wp_in)
    return (loss_part, grad_x, _w_in_shards(dwp_in), d_heads, d_out, d_ws, d_bs_t, d_lng, d_lnb, d_sgug, d_sgub,
            d_qg, d_kvg)
```

```python
import math

import jax
import jax.numpy as jnp
from jax import lax
from jax.experimental import pallas as pl
from jax.experimental.pallas import tpu as pltpu

F32 = jnp.float32
BF16 = jnp.bfloat16

N_DEV = 8
D_MODEL = 1024
HEADS = 8
NOPE = 64
ROPE = 32
HALF = ROPE // 2
VDIM = 64
Q_LORA = 256
KV_LORA = 128
G_WIDTH = 512
G_HEAD_DIM = 64
CHUNK = 128
HEAD_PAD = 128
D_IN = 2464
D_IN_PAD = 2560
KR_LO = NOPE
SUM_ROW = NOPE - 1
LIVE_ROWS = slice(NOPE - 16, HEAD_PAD)
ROPE_THETA = 10000.0
DN_ALPHA = 2.0 ** 0.25
EPS = 1e-5
ATTN_SCALE = 1.0 / math.sqrt(NOPE + ROPE)
ADAM_LR, ADAM_B1, ADAM_B2, ADAM_EPS, ADAM_WD, ADAM_STEP = 0.001, 0.9, 0.999, 1e-08, 0.01, 10

LANES = 128
REP_ROWS = 136
SMALL_LEN = 8192
VMEM_LIMIT = 56 * 1024 * 1024
ATTN_BWD_VMEM_LIMIT = 61 * 1024 * 1024
BWD_TAIL_VMEM_LIMIT = 61 * 1024 * 1024

TOKEN_TILE = 256
PROJ_TILE = 512
X_RING = 3
ATTN_FWD_WIDE = 2048
ATTN_BWD_WIDE = 2048
ATTN_NARROW = 512
SOFTMAX_ROWS = 512
LOG2E = 1.4426950408889634
LN2 = 0.6931471805599453
Q_PRESCALE = ATTN_SCALE * LOG2E


def _cparams(sem=None, vmem_limit=VMEM_LIMIT):
    return pltpu.CompilerParams(dimension_semantics=sem, vmem_limit_bytes=vmem_limit)


def _dot(a, b):
    return jnp.dot(a, b, preferred_element_type=F32)


def _dot_nt(a, b):
    return lax.dot_general(a, b, (((1,), (1,)), ((), ())), preferred_element_type=F32)


def _dot_tn(a, b):
    return lax.dot_general(a, b, (((0,), (0,)), ((), ())), preferred_element_type=F32)


def _sigmoid(z):
    return 1.0 / (1.0 + jnp.exp(-z))


def _normal_cdf(x):
    return 0.5 * (1.0 + lax.erf(x * 0.7071067811865476))


def _gelu_grad(x, cdf):
    return cdf + x * jnp.exp(-0.5 * x * x) * 0.3989422804014327


def _gather_direct(srcs, *, name, first_row=0):
    n = len(srcs)
    shapes = [(s.shape[0] - first_row,) + s.shape[1:] for s in srcs]

    def body(*refs):
        src_refs, out_refs = [r.at[pl.ds(first_row, shape[0])] for r, shape in zip(refs[:n], shapes)], refs[n:2 * n]
        send_sems, recv_sems, local_sems = refs[2 * n:]
        x, y, c = lax.axis_index("x"), lax.axis_index("y"), lax.axis_index("c")
        me = 4 * x + 2 * y + c
        mine = [pltpu.make_async_copy(src_refs[t], out_refs[t].at[me], local_sems.at[t]) for t in range(n)]
        for cp in mine:
            cp.start()
        sends, arrivals = [], []
        for k in (6, 7, 4, 5, 2, 3, 1):
            px = 1 - x if k & 4 else x
            py = 1 - y if k & 2 else y
            pc = 1 - c if k & 1 else c
            peer = 4 * px + 2 * py + pc
            for t in range(n):
                sem = (k - 1) * n + t
                cp = pltpu.make_async_remote_copy(
                    src_ref=src_refs[t], dst_ref=out_refs[t].at[me],
                    send_sem=send_sems.at[sem], recv_sem=recv_sems.at[sem],
                    device_id=(px, py, pc), device_id_type=pl.DeviceIdType.MESH)
                cp.start()
                sends.append(cp)
                arrivals.append(pltpu.make_async_remote_copy(
                    src_ref=src_refs[t], dst_ref=out_refs[t].at[peer],
                    send_sem=send_sems.at[sem], recv_sem=recv_sems.at[sem],
                    device_id=(x, y, c), device_id_type=pl.DeviceIdType.MESH))
        for cp in arrivals:
            cp.wait_recv()
        for cp in sends:
            cp.wait_send()
        for cp in mine:
            cp.wait()

    hbm = pl.BlockSpec(memory_space=pl.ANY)
    return pl.pallas_call(
        body, name=name,
        out_shape=[jax.ShapeDtypeStruct((N_DEV,) + shape, s.dtype) for shape, s in zip(shapes, srcs)],
        in_specs=[hbm] * n, out_specs=[hbm] * n,
        scratch_shapes=[pltpu.SemaphoreType.DMA(((N_DEV - 1) * n,)), pltpu.SemaphoreType.DMA(((N_DEV - 1) * n,)),
                        pltpu.SemaphoreType.DMA((n,))],
    )(*srcs)


def _gather_two_level(srcs, *, name):
    n = len(srcs)

    def body(*refs):
        src_refs, out_refs = refs[:n], refs[n:2 * n]
        send_sems, recv_sems, local_sems = refs[2 * n:]
        x, y, c = lax.axis_index("x"), lax.axis_index("y"), lax.axis_index("c")
        me, sibling = (x, y, c), (x, y, 1 - c)
        chips = [(1 - x, 1 - y), (1 - x, y), (x, 1 - y)]
        index = lambda px, py, pc: 4 * px + 2 * py + pc

        def copy(k, t, block, to, src=None):
            place = out_refs[t].at[index(*block)]
            return pltpu.make_async_remote_copy(
                src_ref=place if src is None else src, dst_ref=place,
                send_sem=send_sems.at[k * n + t], recv_sem=recv_sems.at[k * n + t],
                device_id=to, device_id_type=pl.DeviceIdType.MESH)

        mine = [pltpu.make_async_copy(src_refs[t], out_refs[t].at[index(*me)], local_sems.at[t]) for t in range(n)]
        for cp in mine:
            cp.start()
        first = [copy(1 + j, t, me, (*chip, c), src=src_refs[t]) for j, chip in enumerate(chips) for t in range(n)]
        first += [copy(0, t, me, sibling, src=src_refs[t]) for t in range(n)]
        for cp in first:
            cp.start()
        passed = []
        for j, chip in enumerate(chips):
            for t in range(n):
                copy(1 + j, t, (*chip, c), me).wait_recv()
                cp = copy(4 + j, t, (*chip, c), sibling)
                cp.start()
                passed.append(cp)
        for t in range(n):
            copy(0, t, sibling, me).wait_recv()
        for j, chip in enumerate(chips):
            for t in range(n):
                copy(4 + j, t, (*chip, 1 - c), me).wait_recv()
        for cp in first + passed:
            cp.wait_send()
        for cp in mine:
            cp.wait()

    hbm = pl.BlockSpec(memory_space=pl.ANY)
    return pl.pallas_call(
        body, name=name,
        out_shape=[jax.ShapeDtypeStruct((N_DEV,) + s.shape, s.dtype) for s in srcs],
        in_specs=[hbm] * n, out_specs=[hbm] * n,
        scratch_shapes=[pltpu.SemaphoreType.DMA((7 * n,)), pltpu.SemaphoreType.DMA((7 * n,)),
                        pltpu.SemaphoreType.DMA((n,))],
    )(*srcs)


N_CHIPS = N_DEV // 2


def _sibling_swap(srcs, *, name):
    n = len(srcs)

    def body(*refs):
        src_refs, out_refs = refs[:n], refs[n:2 * n]
        send_sems, recv_sems = refs[2 * n:]
        x, y, c = lax.axis_index("x"), lax.axis_index("y"), lax.axis_index("c")
        sends = []
        for chip in range(N_CHIPS):
            for t in range(n):
                cp = pltpu.make_async_remote_copy(
                    src_ref=src_refs[t].at[chip, 1 - c], dst_ref=out_refs[t].at[chip],
                    send_sem=send_sems.at[chip * n + t], recv_sem=recv_sems.at[chip * n + t],
                    device_id=(x, y, 1 - c), device_id_type=pl.DeviceIdType.MESH)
                cp.start()
                sends.append(cp)
        for cp in sends:
            cp.wait_recv()
        for cp in sends:
            cp.wait_send()

    hbm = pl.BlockSpec(memory_space=pl.ANY)
    return pl.pallas_call(
        body, name=name,
        out_shape=[jax.ShapeDtypeStruct((N_CHIPS,) + s.shape[2:], s.dtype) for s in srcs],
        in_specs=[hbm] * n, out_specs=[hbm] * n,
        scratch_shapes=[pltpu.SemaphoreType.DMA((N_CHIPS * n,)), pltpu.SemaphoreType.DMA((N_CHIPS * n,))],
    )(*srcs)


def _pair_sum(mine, theirs, core, *, name, out_dtype):
    _, _, rows, cols = mine.shape

    def body(core_ref, a_ref, b_ref, o_ref):
        o_ref[...] = (a_ref[0] + b_ref[...]).astype(out_dtype)

    return pl.pallas_call(
        body, name=name,
        grid_spec=pltpu.PrefetchScalarGridSpec(
            num_scalar_prefetch=1, grid=(N_CHIPS,),
            in_specs=[pl.BlockSpec((1, 1, rows, cols), lambda q, core_ref: (q, core_ref[0], 0, 0)),
                      pl.BlockSpec((1, rows, cols), lambda q, core_ref: (q, 0, 0))],
            out_specs=pl.BlockSpec((1, rows, cols), lambda q, core_ref: (q, 0, 0))),
        out_shape=jax.ShapeDtypeStruct((N_CHIPS, rows, cols), out_dtype),
        compiler_params=_cparams(("arbitrary",)),
    )(core, mine, theirs)


def _chip_exchange(srcs, *, name):
    n = len(srcs)

    def body(*refs):
        src_refs, out_refs = refs[:n], refs[n:2 * n]
        send_sems, recv_sems, local_sems = refs[2 * n:]
        x, y, c = lax.axis_index("x"), lax.axis_index("y"), lax.axis_index("c")
        my_chip = 2 * x + y
        mine = [pltpu.make_async_copy(src_refs[t].at[my_chip], out_refs[t].at[my_chip], local_sems.at[t])
                for t in range(n)]
        for cp in mine:
            cp.start()
        sends, arrivals = [], []
        for k in (3, 2, 1):
            px = 1 - x if k & 2 else x
            py = 1 - y if k & 1 else y
            peer_chip = 2 * px + py
            for t in range(n):
                sem = (k - 1) * n + t
                cp = pltpu.make_async_remote_copy(
                    src_ref=src_refs[t].at[peer_chip], dst_ref=out_refs[t].at[my_chip],
                    send_sem=send_sems.at[sem], recv_sem=recv_sems.at[sem],
                    device_id=(px, py, c), device_id_type=pl.DeviceIdType.MESH)
                cp.start()
                sends.append(cp)
                arrivals.append(pltpu.make_async_remote_copy(
                    src_ref=src_refs[t].at[peer_chip], dst_ref=out_refs[t].at[peer_chip],
                    send_sem=send_sems.at[sem], recv_sem=recv_sems.at[sem],
                    device_id=(x, y, c), device_id_type=pl.DeviceIdType.MESH))
        for cp in arrivals:
            cp.wait_recv()
        for cp in sends:
            cp.wait_send()
        for cp in mine:
            cp.wait()

    hbm = pl.BlockSpec(memory_space=pl.ANY)
    return pl.pallas_call(
        body, name=name,
        out_shape=[jax.ShapeDtypeStruct(s.shape, s.dtype) for s in srcs],
        in_specs=[hbm] * n, out_specs=[hbm] * n,
        scratch_shapes=[pltpu.SemaphoreType.DMA((3 * n,)), pltpu.SemaphoreType.DMA((3 * n,)),
                        pltpu.SemaphoreType.DMA((n,))],
    )(*srcs)


def _rope_tables(pos_row, invf_col):
    tm = pos_row.shape[1]
    ang = pos_row.astype(F32) * invf_col
    cos, sin = jnp.cos(ang), jnp.sin(ang)
    ones = lambda n: jnp.ones((n, tm), F32)
    zeros = lambda n: jnp.zeros((n, tm), F32)
    cos_t = jnp.concatenate([ones(KR_LO), cos, cos, ones(LANES - KR_LO - ROPE)], axis=0)
    first_t = jnp.concatenate([zeros(KR_LO), sin, zeros(LANES - KR_LO - HALF)], axis=0)
    second_t = jnp.concatenate([zeros(KR_LO + HALF), sin, zeros(LANES - KR_LO - ROPE)], axis=0)
    return jnp.transpose(cos_t), jnp.transpose(first_t), jnp.transpose(second_t)


def _rope(t, cos, sin_first, sin_second, sign):
    up = pltpu.roll(t, LANES - HALF, 1)
    down = pltpu.roll(t, HALF, 1)
    return t * cos - sign * (up * sin_first) + sign * (down * sin_second)


def _fwd_proj(x, pos_row, invf_col, w_in_shards, w_heads, q_g, kv_g):
    t = x.shape[0]
    tm = PROJ_TILE
    n_steps = t // tm
    n_latent = Q_LORA + KV_LORA + LANES

    def body(x_ref, pos_ref, invf_ref, sh_ref, wh_ref, qg_ref, kvg_ref,
             proj_ref, q_ref, k_ref, v_ref, vt_ref, win_ref, latent_even, latent_odd, x_ring, x_sems):
        step = pl.program_id(0)
        latents = (latent_even, latent_odd)

        def x_copy(tile):
            slot = tile % X_RING
            return pltpu.make_async_copy(x_ref.at[pl.ds(pl.multiple_of(tile * tm, tm), tm), :], x_ring.at[slot],
                                         x_sems.at[slot])

        @pl.when(step == 0)
        def _():
            x_copy(0).start()
            x_copy(1).start()

        @pl.when(step + 2 < n_steps)
        def _():
            x_copy(step + 2).start()

        @pl.when(step < n_steps)
        def _():
            x_copy(step).wait()

        def project(latent_ref):
            proj = _dot(x_ring[step % X_RING].astype(BF16), win_ref[...])
            proj_ref[...] = proj
            latent_ref[...] = proj[:, :n_latent]

        def heads(latent_ref):
            c_q = latent_ref[:, :Q_LORA]
            c_kv = latent_ref[:, Q_LORA:Q_LORA + KV_LORA]
            kr_raw = latent_ref[:, Q_LORA + KV_LORA:]
            cqn = (c_q * lax.rsqrt(jnp.mean(c_q * c_q, axis=-1, keepdims=True) + EPS) * qg_ref[...]).astype(BF16)
            ckvn = (c_kv * lax.rsqrt(jnp.mean(c_kv * c_kv, axis=-1, keepdims=True) + EPS) * kvg_ref[...]).astype(BF16)
            cos, s1, s2 = _rope_tables(pos_ref[...], invf_ref[...])
            kr = _rope(kr_raw, cos, s1, s2, 1.0)
            lane = lax.broadcasted_iota(jnp.int32, (tm, HEAD_PAD), 1)
            q_all = _dot(cqn, jnp.concatenate([wh_ref[h, :Q_LORA, :] for h in range(HEADS)], axis=1))
            kv_all = _dot(ckvn, jnp.concatenate([wh_ref[h, Q_LORA:, :] for h in range(HEADS)], axis=1))
            for h in range(HEADS):
                q_h = q_all[:, h * HEAD_PAD:(h + 1) * HEAD_PAD]
                kv_h = kv_all[:, h * HEAD_PAD:(h + 1) * HEAD_PAD]
                q_ref[h] = (_rope(q_h, cos, s1, s2, 1.0) * Q_PRESCALE).astype(BF16)
                k_ref[h] = jnp.where(lane < NOPE, kv_h, kr).astype(BF16)
                v_ref[h] = kv_h.astype(BF16)
                vt_ref[h] = jnp.transpose(jnp.where(lane == SUM_ROW, 1.0, kv_h)).astype(BF16)

        @pl.when(step == 0)
        def _():
            win_ref[...] = jnp.zeros_like(win_ref)
            for s, src, dst, width in _w_in_pieces():
                win_ref[:, dst:dst + width] = sh_ref[s, :, src:src + width]
            project(latents[0])

        for parity in (0, 1):
            @pl.when((step > 0) & (step < n_steps) & (step % 2 == parity))
            def _():
                heads(latents[1 - parity])
                project(latents[parity])

        @pl.when(step == n_steps)
        def _():
            heads(latents[(n_steps - 1) % 2])

    full = lambda a: pl.BlockSpec(a.shape, lambda i: (0,) * a.ndim)
    this = lambda i: jnp.minimum(i, n_steps - 1)
    prev = lambda i: jnp.maximum(i - 1, 0)
    head_spec = pl.BlockSpec((HEADS, tm, HEAD_PAD), lambda i: (0, prev(i), 0))
    head_shape = jax.ShapeDtypeStruct((HEADS, t, HEAD_PAD), BF16)
    return pl.pallas_call(
        body, name="fwd_proj", grid=(n_steps + 1,),
        in_specs=[pl.BlockSpec(memory_space=pl.ANY), pl.BlockSpec((1, tm), lambda i: (0, prev(i))),
                  full(invf_col), full(w_in_shards), full(w_heads), full(q_g), full(kv_g)],
        out_specs=[pl.BlockSpec((tm, D_IN_PAD), lambda i: (this(i), 0)), head_spec, head_spec, head_spec,
                   pl.BlockSpec((HEADS, HEAD_PAD, tm), lambda i: (0, 0, prev(i))),
                   pl.BlockSpec((D_MODEL, D_IN_PAD), lambda i: (0, 0))],
        out_shape=[jax.ShapeDtypeStruct((t, D_IN_PAD), F32), head_shape, head_shape, head_shape,
                   jax.ShapeDtypeStruct((HEADS, HEAD_PAD, t), BF16),
                   jax.ShapeDtypeStruct((D_MODEL, D_IN_PAD), w_in_shards.dtype)],
        scratch_shapes=[pltpu.VMEM((tm, n_latent), F32)] * 2
        + [pltpu.VMEM((X_RING, tm, D_MODEL), F32), pltpu.SemaphoreType.DMA((X_RING,))],
        compiler_params=_cparams(("arbitrary",)),
    )(x, pos_row, invf_col, w_in_shards, w_heads, q_g, kv_g)


def _attn_fwd(q, k, vt):
    t = q.shape[1]
    bq, bk = ATTN_FWD_WIDE, ATTN_NARROW
    n_diag = bq // bk
    chunk = SOFTMAX_ROWS

    def body(q_ref, k_ref, vt_ref, o_ref, lse_ref, s0, s1, p0, p1, x0, x1, m_scr, a_scr, acc_scr):
        i = pl.program_id(1)
        at = lambda j: pl.ds(pl.multiple_of(j * bk, bk), bk)

        def exp_pass(s_in, block_max, p_out, diagonal=False, cols=slice(None)):
            width = bq if cols == slice(None) else cols.stop - cols.start

            def load(r):
                s = s_in[r:r + chunk, cols]
                if diagonal:
                    key = lax.broadcasted_iota(jnp.int32, (chunk, width), 0) + r
                    qry = lax.broadcasted_iota(jnp.int32, (chunk, width), 1)
                    s = jnp.where(qry >= key, s, -jnp.inf)
                return s

            if diagonal:
                block_max = jnp.max(load(0), axis=0, keepdims=True)
                for r in range(chunk, bk, chunk):
                    block_max = jnp.maximum(block_max, jnp.max(load(r), axis=0, keepdims=True))
            m_old = m_scr[:, cols]
            m_new = jnp.maximum(m_old, block_max)
            alpha = jnp.exp2(m_old - m_new)
            for r in range(0, bk, chunk):
                p_out[r:r + chunk, cols] = jnp.exp2(load(r) - m_new).astype(BF16)
            m_scr[:, cols] = m_new
            return alpha

        def scores(j, s_out, x_out):
            s = _dot_nt(k_ref[0, at(j), :], q_ref[0])
            s_out[...] = s
            x_out[...] = jnp.max(s, axis=0, keepdims=True)

        def value_product(j, p_in):
            return _dot(vt_ref[0, LIVE_ROWS, at(j)], p_in[...])

        def one_pass(j, s_in, x_in, s_out, x_out, p_prev, p_cur):
            scores(j + 1, s_out, x_out)
            acc_scr[...] = a_scr[...] * acc_scr[...] + value_product(jnp.maximum(j - 1, 0), p_prev)
            a_scr[...] = exp_pass(s_in, x_in[...], p_cur)

        scores(0, s0, x0)
        p1[...] = jnp.zeros_like(p1)
        a_scr[...] = jnp.ones_like(a_scr)
        m_scr[...] = jnp.full(m_scr.shape, -jnp.inf, F32)
        acc_scr[...] = jnp.zeros_like(acc_scr)

        def two_passes(n, _):
            one_pass(2 * n, s0, x0, s1, x1, p1, p0)
            one_pass(2 * n + 1, s1, x1, s0, x0, p0, p1)
            return 0

        lax.fori_loop(0, (n_diag // 2) * i, two_passes, 0)
        d = n_diag * i
        alpha, p_prev, cols = a_scr[...], p1, slice(0, bq)
        for u in range(n_diag + 1):
            s_in, s_next, p_cur = (s0, s1, p0) if u % 2 == 0 else (s1, s0, p1)
            if u + 1 < n_diag:
                ahead = slice((u + 1) * bk, bq)
                s_next[:, ahead] = _dot_nt(k_ref[0, at(d + u + 1), :], q_ref[0, ahead, :])
            acc_scr[:, cols] = alpha * acc_scr[:, cols] + _dot(vt_ref[0, LIVE_ROWS, at(jnp.maximum(d + u - 1, 0))],
                                                               p_prev[:, cols])
            if u < n_diag:
                cols = slice(u * bk, bq)
                alpha = exp_pass(s_in, None, p_cur, diagonal=True, cols=cols)
                p_prev = p_cur
        denom = acc_scr[SUM_ROW - LIVE_ROWS.start:NOPE - LIVE_ROWS.start, :]
        o = jnp.transpose(acc_scr[NOPE - LIVE_ROWS.start:, :] / denom)
        o_ref[0] = jnp.concatenate([jnp.zeros_like(o), o], axis=1)
        lse_ref[0] = m_scr[...] + jnp.log2(denom)

    tile = lambda dtype: pltpu.VMEM((bk, bq), dtype)
    stat = pltpu.VMEM((1, bq), F32)
    return pl.pallas_call(
        body, name="attn_fwd", grid=(HEADS, t // bq),
        in_specs=[pl.BlockSpec((1, bq, HEAD_PAD), lambda h, i: (h, i, 0)),
                  pl.BlockSpec((1, t, HEAD_PAD), lambda h, i: (h, 0, 0)),
                  pl.BlockSpec((1, HEAD_PAD, t), lambda h, i: (h, 0, 0))],
        out_specs=[pl.BlockSpec((1, bq, HEAD_PAD), lambda h, i: (h, i, 0)),
                   pl.BlockSpec((1, 1, bq), lambda h, i: (h, 0, i))],
        out_shape=[jax.ShapeDtypeStruct((HEADS, t, HEAD_PAD), F32), jax.ShapeDtypeStruct((HEADS, 1, t), F32)],
        scratch_shapes=[tile(F32), tile(F32), tile(BF16), tile(BF16), stat, stat, stat, stat,
                        pltpu.VMEM((HEAD_PAD - LIVE_ROWS.start, bq), F32)],
        compiler_params=_cparams(("arbitrary", "arbitrary")),
    )(q, k, vt)


def _mid(x, target, proj, ol, w_out, ws_low, ws_low_t, bsp, sgu_g, sgu_b, ln_g, ln_b):
    t = x.shape[0]
    tm = TOKEN_TILE
    n_steps = t // tm

    def body(x_ref, tgt_ref, za_ref, u_ref, v_ref, zb_ref, ol_ref, prev_za_ref, prev_u_ref, prev_v_ref, prev_zb_ref,
             prev_ol_ref, wout_ref, ws_ref, wst_ref, bsp_ref, sg_ref, sb_ref, lg_ref, lb_ref,
             dr_ref, do_ref, drow_ref, drest_ref, dwout_ref, dws_ref, dbs_ref, dlg_ref, dlb_ref, dsg_ref, dsb_ref,
             loss_ref, dbsp_acc, *kept_refs):
        step = pl.program_id(0)
        kept_sets = (kept_refs[:len(kept_refs) // 2], kept_refs[len(kept_refs) // 2:])

        @pl.when(step == 0)
        def _():
            dwout_ref[...] = jnp.zeros_like(dwout_ref)
            dws_ref[...] = jnp.zeros_like(dws_ref)
            dbs_ref[...] = jnp.zeros_like(dbs_ref)
            dlg_ref[...] = jnp.zeros_like(dlg_ref)
            dlb_ref[...] = jnp.zeros_like(dlb_ref)
            dsg_ref[...] = jnp.zeros_like(dsg_ref)
            dsb_ref[...] = jnp.zeros_like(dsb_ref)
            loss_ref[...] = jnp.zeros_like(loss_ref)
            dbsp_acc[...] = jnp.zeros_like(dbsp_acc)

        n_chunks = tm // CHUNK
        groups = G_WIDTH // LANES

        def side_by_side(a):
            return [jnp.concatenate([a[c * CHUNK:(c + 1) * CHUNK, g * LANES:(g + 1) * LANES] for c in range(n_chunks)],
                                    axis=1) for g in range(groups)]

        def by_chunk(wide):
            return jnp.concatenate([jnp.concatenate([wide[g][:, c * LANES:(c + 1) * LANES] for g in range(groups)], axis=1)
                                    for c in range(n_chunks)], axis=0)

        def own_lanes(h):
            lane = lax.broadcasted_iota(jnp.int32, (CHUNK, n_chunks * LANES), 1)
            return (lane % LANES) // G_HEAD_DIM == h % 2

        def spatial(w_ref, wide):
            return [sum(jnp.where(own_lanes(h), _dot(w_ref[h], wide[g]), 0.0) for h in (2 * g, 2 * g + 1))
                    for g in range(groups)]

        def value_lanes(o_ref):
            return jnp.concatenate([o_ref[h][:, NOPE:] for h in range(HEADS)], axis=-1)

        def forward(kept):
            attn = value_lanes(ol_ref)
            za = za_ref[...]
            sig_a = _sigmoid(za)
            out_a = attn * (za * sig_a)
            u = u_ref[...]
            cdf_u = _normal_cdf(u)
            vpre = v_ref[...]
            cdf_v = _normal_cdf(vpre)
            gv = vpre * cdf_v
            mu_v = jnp.mean(gv, axis=-1, keepdims=True)
            cen_v = gv - mu_v
            rstd_v = lax.rsqrt(jnp.mean(cen_v * cen_v, axis=-1, keepdims=True) + EPS)
            vhat = cen_v * rstd_v
            vg = vhat * sg_ref[...] + sb_ref[...]
            vg_b = vg.astype(BF16)
            yield
            sv = by_chunk(spatial(ws_ref, side_by_side(vg_b))) + jnp.tile(bsp_ref[...], (n_chunks, 1))
            zb = zb_ref[...]
            sig_b = _sigmoid(zb)
            out_b = ((u * cdf_u) * sv) * (zb * sig_b)
            merged = jnp.concatenate([out_a, out_b], axis=-1).astype(BF16)
            yield
            r = DN_ALPHA * x_ref[...] + _dot(merged, wout_ref[...])
            mu = jnp.mean(r, axis=-1, keepdims=True)
            cen = r - mu
            rstd = lax.rsqrt(jnp.mean(cen * cen, axis=-1, keepdims=True) + EPS)
            xhat = cen * rstd
            hout = xhat * lg_ref[...] + lb_ref[...]
            err = hout - tgt_ref[...]
            row_loss = jnp.mean(err * err, axis=-1, keepdims=True)
            loss_ref[...] += jnp.broadcast_to(0.5 * jnp.sum(row_loss, axis=0, keepdims=True), loss_ref.shape)
            for ref, val in zip(kept, (sig_a, cdf_u, cdf_v, vhat, sv, sig_b, xhat, err * (1.0 / D_MODEL), merged, vg_b,
                                       jnp.broadcast_to(rstd, (tm, LANES)), jnp.broadcast_to(rstd_v, (tm, LANES)))):
                ref[...] = val

        def backward(kept):
            (sig_a_ref, cdf_u_ref, cdf_v_ref, vhat_ref, sv_ref, sig_b_ref, xhat_ref, dh_ref, merged_ref, vg_ref,
             rstd_ref, rstd_v_ref) = kept
            attn = value_lanes(prev_ol_ref)
            za, u, vpre, zb = prev_za_ref[...], prev_u_ref[...], prev_v_ref[...], prev_zb_ref[...]
            sig_a, cdf_u, cdf_v, vhat, sv, sig_b = (sig_a_ref[...], cdf_u_ref[...], cdf_v_ref[...], vhat_ref[...],
                                                    sv_ref[...], sig_b_ref[...])
            xhat, dh, merged, vg_b = xhat_ref[...], dh_ref[...], merged_ref[...], vg_ref[...]
            rstd, rstd_v = rstd_ref[:, :1], rstd_v_ref[:, :1]
            silu_a, silu_b, ug = za * sig_a, zb * sig_b, u * cdf_u
            sgu = ug * sv
            dlg_ref[...] += jnp.sum(dh * xhat, axis=0, keepdims=True)
            dlb_ref[...] += jnp.sum(dh, axis=0, keepdims=True)
            dxhat = dh * lg_ref[...]
            dr = rstd * (dxhat - jnp.mean(dxhat, axis=-1, keepdims=True)
                         - xhat * jnp.mean(dxhat * xhat, axis=-1, keepdims=True))
            dr_ref[...] = dr
            dr_b = dr.astype(BF16)
            yield
            dwout_ref[...] += _dot_tn(merged, dr_b)
            dmerged = _dot_nt(dr_b, wout_ref[...])
            yield
            d_out_a = dmerged[:, :G_WIDTH]
            d_out_b = dmerged[:, G_WIDTH:]
            dattn = d_out_a * silu_a
            for h in range(HEADS):
                do_h = dattn[:, h * VDIM:(h + 1) * VDIM]
                do_ref[h] = jnp.concatenate([jnp.zeros((tm, NOPE), F32), do_h], axis=-1).astype(BF16)
            feature = lax.broadcasted_iota(jnp.int32, (G_WIDTH, LANES), 0) // VDIM
            column = lax.broadcasted_iota(jnp.int32, (G_WIDTH, LANES), 1)
            head_sums = jnp.dot(dattn * attn, jnp.where(feature == column, 1.0, 0.0).astype(F32),
                                preferred_element_type=F32, precision=lax.Precision.HIGH)
            dsums_t = jnp.transpose(head_sums)
            for h in range(HEADS):
                drow_ref[h] = dsums_t[h:h + 1, :]
            dza = d_out_a * attn * (sig_a * (1.0 + za * (1.0 - sig_a)))
            dsgu = d_out_b * silu_b
            dzb = d_out_b * sgu * (sig_b * (1.0 + zb * (1.0 - sig_b)))
            du = dsgu * sv * _gelu_grad(u, cdf_u)
            dsv = dsgu * ug
            dsv_b = dsv.astype(BF16)
            for cix in range(n_chunks):
                dbsp_acc[...] += dsv[cix * CHUNK:(cix + 1) * CHUNK, :]
            yield
            dsv_wide, vg_wide = side_by_side(dsv_b), side_by_side(vg_b)
            dvg = by_chunk(spatial(wst_ref, dsv_wide))
            for h in range(HEADS):
                mine = jnp.where(own_lanes(h), dsv_wide[h // 2], jnp.zeros_like(dsv_wide[h // 2]))
                dws_ref[h] += _dot_nt(mine, vg_wide[h // 2])
            dsg_ref[...] += jnp.sum(dvg * vhat, axis=0, keepdims=True)
            dsb_ref[...] += jnp.sum(dvg, axis=0, keepdims=True)
            dvhat = dvg * sg_ref[...]
            dgv = rstd_v * (dvhat - jnp.mean(dvhat, axis=-1, keepdims=True)
                            - vhat * jnp.mean(dvhat * vhat, axis=-1, keepdims=True))
            dv = dgv * _gelu_grad(vpre, cdf_v)
            drest_ref[...] = jnp.concatenate([dza, du, dv, dzb], axis=-1).astype(BF16)

        def emit(order, **stages):
            for who in order:
                next(stages[who], None)

        @pl.when(step == 0)
        def _():
            emit("fff", f=forward(kept_sets[0]))

        for parity in (0, 1):
            @pl.when((step > 0) & (step < n_steps) & (step % 2 == parity))
            def _():
                emit("ffbbfbb", f=forward(kept_sets[parity]), b=backward(kept_sets[1 - parity]))

        @pl.when(step == n_steps)
        def _():
            emit("bbbb", b=backward(kept_sets[(n_steps - 1) % 2]))
            tri = (lax.broadcasted_iota(jnp.int32, (CHUNK, CHUNK), 0)
                   >= lax.broadcasted_iota(jnp.int32, (CHUNK, CHUNK), 1))
            for h in range(HEADS):
                dws_ref[h] = jnp.where(tri, dws_ref[h], 0.0)
            tot = dbsp_acc[...]
            lane = lax.broadcasted_iota(jnp.int32, (CHUNK, LANES), 1)
            dbs = jnp.zeros((CHUNK, LANES), F32)
            for h in range(HEADS):
                head_sum = jnp.sum(tot[:, h * G_HEAD_DIM:(h + 1) * G_HEAD_DIM], axis=-1, keepdims=True)
                dbs = jnp.where(lane == h, head_sum, dbs)
            dbs_ref[...] = dbs

    full = lambda a: pl.BlockSpec(a.shape, lambda i: (0,) * a.ndim)
    this = lambda i: jnp.minimum(i, n_steps - 1)
    prev = lambda i: jnp.maximum(i - 1, 0)
    tile = lambda w, j=0, at=this: pl.BlockSpec((tm, w), lambda i, j=j: (at(i), j))
    heads = lambda at: pl.BlockSpec((HEADS, tm, HEAD_PAD), lambda i: (0, at(i), 0))
    acc = lambda shape: (pl.BlockSpec(shape, lambda i: (0,) * len(shape)), jax.ShapeDtypeStruct(shape, F32))
    accs = [acc((D_MODEL, D_MODEL)), acc((HEADS, CHUNK, CHUNK)), acc((CHUNK, LANES)), acc((1, D_MODEL)),
            acc((1, D_MODEL)), acc((1, G_WIDTH)), acc((1, G_WIDTH)), acc((1, LANES))]
    kept = ([pltpu.VMEM((tm, G_WIDTH), F32)] * 6 + [pltpu.VMEM((tm, D_MODEL), F32)] * 2
            + [pltpu.VMEM((tm, D_MODEL), BF16), pltpu.VMEM((tm, G_WIDTH), BF16)] + [pltpu.VMEM((tm, LANES), F32)] * 2)
    return pl.pallas_call(
        body, name="mid", grid=(n_steps + 1,),
        in_specs=[tile(D_MODEL), tile(D_MODEL), tile(G_WIDTH, 1), tile(G_WIDTH, 2), tile(G_WIDTH, 3), tile(G_WIDTH, 4),
                  heads(this), tile(G_WIDTH, 1, prev), tile(G_WIDTH, 2, prev), tile(G_WIDTH, 3, prev),
                  tile(G_WIDTH, 4, prev), heads(prev),
                  full(w_out), full(ws_low), full(ws_low_t), full(bsp), full(sgu_g), full(sgu_b),
                  full(ln_g), full(ln_b)],
        out_specs=[tile(D_MODEL, 0, prev), heads(prev), pl.BlockSpec((HEADS, 1, tm), lambda i: (0, 0, prev(i))),
                   tile(4 * G_WIDTH, 0, prev)]
        + [a[0] for a in accs],
        out_shape=[jax.ShapeDtypeStruct((t, D_MODEL), F32), jax.ShapeDtypeStruct((HEADS, t, HEAD_PAD), BF16),
                   jax.ShapeDtypeStruct((HEADS, 1, t), F32), jax.ShapeDtypeStruct((t, 4 * G_WIDTH), BF16)]
        + [a[1] for a in accs],
        scratch_shapes=[pltpu.VMEM((CHUNK, G_WIDTH), F32)] + kept + kept,
        compiler_params=_cparams(("arbitrary",)),
    )(x, target, proj, proj, proj, proj, ol, proj, proj, proj, proj, ol,
      w_out, ws_low, ws_low_t, bsp, sgu_g, sgu_b, ln_g, ln_b)


def _attn_bwd(q, k, v, do, lse_row, d_row):
    t = q.shape[1]
    bk, bq = ATTN_BWD_WIDE, ATTN_NARROW
    n_diag = bk // bq
    half = bq // 2
    last = t // bq - 1
    chunk = SOFTMAX_ROWS

    def body(q_ref, k_ref, v_ref, do_ref, lse_ref, drow_ref, dqt_ref, dk_ref, dv_ref,
             s0, s1, e0, e1, p0, p1, g0, g1, kt_scr):
        j = pl.program_id(1)
        at = lambda i: pl.ds(pl.multiple_of(i * bq, bq), bq)

        @pl.when(j == 0)
        def _():
            dqt_ref[...] = jnp.zeros_like(dqt_ref)

        kt_scr[...] = jnp.transpose(k_ref[0].astype(F32)).astype(BF16)
        dk_ref[...] = jnp.zeros_like(dk_ref)
        dv_ref[...] = jnp.zeros_like(dv_ref)

        whole_tile = ((slice(0, bk), slice(0, bq)),)

        def queries(i, lanes):
            return pl.ds(pl.multiple_of(i * bq + lanes.start, half), lanes.stop - lanes.start)

        def products(i, s_out, e_out, areas=whole_tile):
            i = jnp.minimum(i, last)
            for keys, lanes in areas:
                s_out[keys, lanes] = _dot_nt(k_ref[0, keys, :], q_ref[0, queries(i, lanes), :])
                e_out[keys, lanes] = _dot_nt(v_ref[0, keys, :], do_ref[0, queries(i, lanes), :])

        def gradients(i, p_in, g_in, areas=whole_tile):
            for keys, lanes in areas:
                dv_ref[0, keys, :] += _dot(p_in[keys, lanes], do_ref[0, queries(i, lanes), :])
                dk_ref[0, keys, :] += _dot(g_in[keys, lanes], q_ref[0, queries(i, lanes), :])
                dqt_ref[0, :, queries(i, lanes)] += _dot(kt_scr[:, keys], g_in[keys, lanes])

        def elementwise(i, s_in, e_in, p_out, g_out, qry0=None, areas=whole_tile):
            for keys, lanes in areas:
                width = lanes.stop - lanes.start
                step = chunk if qry0 is None else half
                lse = lse_ref[0, :, queries(i, lanes)]
                dsum = drow_ref[0, :, queries(i, lanes)]
                for r in range(keys.start, keys.stop, step):
                    p = jnp.exp2(s_in[r:r + step, lanes] - lse)
                    if qry0 is not None:
                        key = lax.broadcasted_iota(jnp.int32, (step, width), 0) + r
                        qry = lax.broadcasted_iota(jnp.int32, (step, width), 1) + (qry0 + lanes.start)
                        p = jnp.where(qry >= key, p, 0.0)
                    p_out[r:r + step, lanes] = p.astype(BF16)
                    g_out[r:r + step, lanes] = (p * (e_in[r:r + step, lanes] - dsum)).astype(BF16)

        def one_pass(i, s_in, e_in, s_out, e_out, p_prev, g_prev, p_cur, g_cur):
            products(i + 1, s_out, e_out)
            gradients(i - 1, p_prev, g_prev)
            elementwise(i, s_in, e_in, p_cur, g_cur)

        first = n_diag * j

        def areas_of(u):
            if u >= n_diag:
                return whole_tile
            return ((slice(0, u * bq + half), slice(0, bq)), (slice(u * bq + half, (u + 1) * bq), slice(half, bq)))

        even, odd = (s0, e0, p0, g0), (s1, e1, p1, g1)
        products(first, s0, e0, areas_of(0))
        products(first + 1, s1, e1, areas_of(1))
        elementwise(first, s0, e0, p0, g0, qry0=0, areas=areas_of(0))
        for u in range(1, n_diag):
            (s_in, e_in, p_cur, g_cur), (s_out, e_out, p_prev, g_prev) = (odd, even) if u % 2 else (even, odd)
            products(first + u + 1, s_out, e_out, areas_of(u + 1))
            gradients(first + u - 1, p_prev, g_prev, areas_of(u - 1))
            elementwise(first + u, s_in, e_in, p_cur, g_cur, qry0=u * bq, areas=areas_of(u))
        corner = (slice(bk - half, bk), slice(0, half))
        p1[corner] = jnp.zeros((half, half), BF16)
        g1[corner] = jnp.zeros((half, half), BF16)

        def two_passes(n, _):
            i = first + n_diag + 2 * n
            one_pass(i, s0, e0, s1, e1, p1, g1, p0, g0)
            one_pass(i + 1, s1, e1, s0, e0, p0, g0, p1, g1)
            return 0

        lax.fori_loop(0, (last - first - n_diag + 1) // 2, two_passes, 0)
        gradients(last, p1, g1)
        dk_ref[0] = dk_ref[0] * LN2

    whole = pl.BlockSpec((1, t, HEAD_PAD), lambda h, j: (h, 0, 0))
    block = pl.BlockSpec((1, bk, HEAD_PAD), lambda h, j: (h, j, 0))
    rows = pl.BlockSpec((1, 1, t), lambda h, j: (h, 0, 0), pipeline_mode=pl.Buffered(1))
    shape = jax.ShapeDtypeStruct((HEADS, t, HEAD_PAD), F32)
    tile = lambda dtype: pltpu.VMEM((bk, bq), dtype)
    return pl.pallas_call(
        body, name="attn_bwd", grid=(HEADS, t // bk),
        in_specs=[whole, block, block, whole, rows, rows],
        out_specs=[pl.BlockSpec((1, HEAD_PAD, t), lambda h, j: (h, 0, 0)), block, block],
        out_shape=[jax.ShapeDtypeStruct((HEADS, HEAD_PAD, t), F32), shape, shape],
        scratch_shapes=[tile(F32), tile(F32), tile(F32), tile(F32), tile(BF16), tile(BF16),
                        tile(BF16), tile(BF16), pltpu.VMEM((HEAD_PAD, bk), BF16)],
        compiler_params=_cparams(("arbitrary", "arbitrary"), vmem_limit=ATTN_BWD_VMEM_LIMIT),
    )(q, k, v, do, lse_row, d_row)


def _bwd_tail(dq, dk, dv, proj, pos_row, invf_col, w_heads, q_g, kv_g, x, dr, drest, wp_in):
    t = proj.shape[0]
    tm = PROJ_TILE
    n_head = 4 * LANES

    def body(dq_ref, dk_ref, dv_ref, ph_ref, pos_ref, invf_ref, wh_ref, qg_ref, kvg_ref,
             x_ref, dr_ref, drest_ref, win_ref,
             gx_ref, dwin_ref, dwh_ref, dqg_ref, dkvg_ref):
        @pl.when(pl.program_id(0) == 0)
        def _():
            dwin_ref[...] = jnp.zeros_like(dwin_ref)
            dwh_ref[...] = jnp.zeros_like(dwh_ref)
            dqg_ref[...] = jnp.zeros_like(dqg_ref)
            dkvg_ref[...] = jnp.zeros_like(dkvg_ref)

        xb = x_ref[...].astype(BF16)
        dr_b = drest_ref[...]
        dwin_ref[:, n_head:] += _dot_tn(xb, dr_b)
        gx_rest = DN_ALPHA * dr_ref[...] + _dot_nt(dr_b, win_ref[:, n_head:])

        cos, s1, s2 = _rope_tables(pos_ref[...], invf_ref[...])
        lane = lax.broadcasted_iota(jnp.int32, (tm, LANES), 1)
        c_q = ph_ref[:, :Q_LORA]
        c_kv = ph_ref[:, Q_LORA:Q_LORA + KV_LORA]
        rstd_q = lax.rsqrt(jnp.mean(c_q * c_q, axis=-1, keepdims=True) + EPS)
        rstd_kv = lax.rsqrt(jnp.mean(c_kv * c_kv, axis=-1, keepdims=True) + EPS)
        qhat = c_q * rstd_q
        kvhat = c_kv * rstd_kv
        cqn = (qhat * qg_ref[...]).astype(BF16)
        ckvn = (kvhat * kvg_ref[...]).astype(BF16)
        dkr_rot = jnp.zeros((tm, LANES), F32)
        dq_heads, dkv_heads = [], []
        for h in range(HEADS):
            dq_heads.append(_rope(jnp.transpose(dq_ref[h]) * ATTN_SCALE, cos, s1, s2, -1.0).astype(BF16))
            dk_h = dk_ref[h]
            dkv_heads.append(jnp.where(lane < NOPE, dk_h, dv_ref[h]).astype(BF16))
            dkr_rot = dkr_rot + dk_h
        dq_all = jnp.concatenate(dq_heads, axis=1)
        dkv_all = jnp.concatenate(dkv_heads, axis=1)
        dwq_all = _dot_tn(cqn, dq_all)
        dwkv_all = _dot_tn(ckvn, dkv_all)
        for h in range(HEADS):
            dwh_ref[h, :Q_LORA, :] += dwq_all[:, h * HEAD_PAD:(h + 1) * HEAD_PAD]
            dwh_ref[h, Q_LORA:, :] += dwkv_all[:, h * HEAD_PAD:(h + 1) * HEAD_PAD]
        dcqn = _dot_nt(dq_all, jnp.concatenate([wh_ref[h, :Q_LORA, :] for h in range(HEADS)], axis=1))
        dckvn = _dot_nt(dkv_all, jnp.concatenate([wh_ref[h, Q_LORA:, :] for h in range(HEADS)], axis=1))
        rot_lanes = (lane >= KR_LO) & (lane < KR_LO + ROPE)
        dkr_raw = jnp.where(rot_lanes, _rope(dkr_rot, cos, s1, s2, -1.0), 0.0)
        dqg_ref[...] += jnp.sum(dcqn * qhat, axis=0, keepdims=True)
        dkvg_ref[...] += jnp.sum(dckvn * kvhat, axis=0, keepdims=True)
        dqh = dcqn * qg_ref[...]
        dkvh = dckvn * kvg_ref[...]
        dc_q = rstd_q * (dqh - qhat * jnp.mean(dqh * qhat, axis=-1, keepdims=True))
        dc_kv = rstd_kv * (dkvh - kvhat * jnp.mean(dkvh * kvhat, axis=-1, keepdims=True))
        dh_b = jnp.concatenate([dc_q, dc_kv, dkr_raw], axis=-1).astype(BF16)
        dwin_ref[:, :n_head] += _dot_tn(xb, dh_b)
        gx_ref[...] = gx_rest + _dot_nt(dh_b, win_ref[:, :n_head])

    full = lambda a: pl.BlockSpec(a.shape, lambda i: (0,) * a.ndim)
    tile = lambda w: pl.BlockSpec((tm, w), lambda i: (i, 0))
    heads = pl.BlockSpec((HEADS, tm, HEAD_PAD), lambda i: (0, i, 0))
    acc = lambda shape: (pl.BlockSpec(shape, lambda i: (0,) * len(shape)), jax.ShapeDtypeStruct(shape, F32))
    accs = [acc(wp_in.shape), acc(w_heads.shape), acc((1, Q_LORA)), acc((1, KV_LORA))]
    return pl.pallas_call(
        body, name="bwd_tail", grid=(t // tm,),
        in_specs=[pl.BlockSpec((HEADS, HEAD_PAD, tm), lambda i: (0, 0, i)), heads, heads, tile(n_head),
                  pl.BlockSpec((1, tm), lambda i: (0, i)), full(invf_col), full(w_heads), full(q_g), full(kv_g),
                  tile(D_MODEL), tile(D_MODEL), tile(drest.shape[1]), full(wp_in)],
        out_specs=[tile(D_MODEL)] + [a[0] for a in accs],
        out_shape=[jax.ShapeDtypeStruct((t, D_MODEL), F32)] + [a[1] for a in accs],
        compiler_params=_cparams(("arbitrary",), vmem_limit=BWD_TAIL_VMEM_LIMIT),
    )(dq, dk, dv, proj, pos_row, invf_col, w_heads, q_g, kv_g, x, dr, drest, wp_in)


def _adam_update(g, w, m, v):
    m_new = ADAM_B1 * m + (1.0 - ADAM_B1) * g
    v_new = ADAM_B2 * v + (1.0 - ADAM_B2) * (g * g)
    m_hat = m_new / (1.0 - ADAM_B1 ** ADAM_STEP)
    v_hat = v_new / (1.0 - ADAM_B2 ** ADAM_STEP)
    return -ADAM_LR * (m_hat / (jnp.sqrt(v_hat) + ADAM_EPS) + ADAM_WD * w), m_new, v_new


def _adam(parts, w, m, v, *, name, tile_rows, transposed=False):
    n, rows, cols = parts.shape
    lane_pad = -(-cols // LANES) * LANES
    own_rows = rows if transposed else w.shape[0]
    assert own_rows == rows or tile_rows == rows

    def body(p_ref, w_ref, m_ref, v_ref, g_ref, d_ref, nm_ref, nv_ref, *scratch):
        g = p_ref[0].astype(F32)
        for s in range(1, n):
            g = g + p_ref[s].astype(F32)
        if transposed:
            wide_ref, = scratch
            wide_ref[:, lane_pad - LANES:] = jnp.zeros((tile_rows, LANES), F32)
            wide_ref[:, :cols] = g
            g = jnp.transpose(wide_ref[...])[:cols]
        g_ref[...] = g
        d_ref[...], nm_ref[...], nv_ref[...] = _adam_update(g[:w_ref.shape[0]], w_ref[...], m_ref[...], v_ref[...])

    if transposed:
        flat = grad = pl.BlockSpec((cols, tile_rows), lambda i: (0, i))
        shape = grad_shape = jax.ShapeDtypeStruct((cols, rows), F32)
        scratch = [pltpu.VMEM((tile_rows, lane_pad), F32)]
    else:
        own_tile = min(tile_rows, own_rows)
        flat = pl.BlockSpec((own_tile, cols), lambda i: (i, 0))
        grad = pl.BlockSpec((tile_rows, cols), lambda i: (i, 0))
        shape, grad_shape = jax.ShapeDtypeStruct((own_rows, cols), F32), jax.ShapeDtypeStruct((rows, cols), F32)
        scratch = []
    return pl.pallas_call(
        body, name=name, grid=(rows // tile_rows,),
        in_specs=[pl.BlockSpec((n, tile_rows, cols), lambda i: (0, i, 0)), flat, flat, flat],
        out_specs=[grad, flat, flat, flat], out_shape=[grad_shape, shape, shape, shape], scratch_shapes=scratch,
        compiler_params=_cparams(("arbitrary",)),
    )(parts, w, m, v)


def _adam_replicated(rep_g, ws, ms, vs):
    count = len(ws)
    small_rows = REP_ROWS - CHUNK

    def body(g_ref, *refs):
        w_refs, m_refs, v_refs = refs[:count], refs[count:2 * count], refs[2 * count:3 * count]
        outs, last_ref, slab_ref = refs[3 * count:7 * count], refs[7 * count], refs[7 * count + 1]
        for d in range(N_DEV):
            slab_ref[d * small_rows:(d + 1) * small_rows, :] = g_ref[d, CHUNK:, :]
        last_ref[...] = slab_ref[N_DEV * small_rows - 1:, LANES - 1:]
        at = 0
        for k, w_ref in enumerate(w_refs):
            if w_ref.ndim == 3:
                g = g_ref[:, :CHUNK, :]
            else:
                n_rows = w_ref.size // LANES
                g = slab_ref[at:at + n_rows, :].reshape(w_ref.shape)
                at += n_rows
            delta, m_new, v_new = _adam_update(g, w_ref[...], m_refs[k][...], v_refs[k][...])
            for which, val in enumerate((g, delta, m_new, v_new)):
                outs[which * count + k][...] = val

    shapes = [jax.ShapeDtypeStruct(w.shape, F32) for w in ws]
    res = pl.pallas_call(
        body, name="adam_rep", out_shape=shapes * 4 + [jax.ShapeDtypeStruct((1, 1), F32)],
        scratch_shapes=[pltpu.VMEM((N_DEV * small_rows, LANES), F32)],
        compiler_params=_cparams(),
    )(rep_g, *ws, *ms, *vs)
    return [res[which * count:(which + 1) * count] for which in range(4)], res[4 * count]


def _pack_small(vals, last):
    flat = jnp.concatenate([v.reshape(-1) for v in vals])
    pad = SMALL_LEN - flat.shape[0]
    return jnp.concatenate([flat, jnp.zeros((pad - 1,), F32), last.reshape(1)])


UQ_SHARD = HEADS * (NOPE + ROPE) // N_DEV
HEAD_ROWS = Q_LORA + KV_LORA
MIXED_ROWS = HEAD_ROWS + CHUNK + SMALL_LEN // N_DEV // LANES


def _head_slab(w_uq_shard, w_ukv_shard):
    return jnp.concatenate([jnp.pad(w_uq_shard, ((0, 0), (0, LANES - UQ_SHARD))), w_ukv_shard])


IN_SHARD = D_IN // N_DEV


def _w_in_pieces():
    split = Q_LORA + KV_LORA
    moves = ((0, split, 0), (split, split + ROPE, KR_LO), (split + ROPE, D_IN, LANES - ROPE))
    pieces = []
    for s in range(N_DEV):
        lo, hi = s * IN_SHARD, (s + 1) * IN_SHARD
        for a, b, shift in moves:
            a, b = max(a, lo), min(b, hi)
            if a < b:
                pieces.append((s, a - lo, a + shift, b - a))
    return pieces


def _w_in_shards(dwp_in):
    tr = TOKEN_TILE
    by_shard = [[p for p in _w_in_pieces() if p[0] == s] for s in range(N_DEV)]

    def body(w_ref, o_ref):
        for s, pieces in enumerate(by_shard):
            parts = [w_ref[:, dst:dst + width] for _, _, dst, width in pieces]
            o_ref[s] = parts[0] if len(parts) == 1 else jnp.concatenate(parts, axis=1)

    return pl.pallas_call(
        body, name="w_in_split", grid=(D_MODEL // tr,),
        in_specs=[pl.BlockSpec((tr, D_IN_PAD), lambda i: (i, 0))],
        out_specs=pl.BlockSpec((N_DEV, tr, IN_SHARD), lambda i: (0, i, 0)),
        out_shape=jax.ShapeDtypeStruct((N_DEV, D_MODEL, IN_SHARD), dwp_in.dtype),
        compiler_params=_cparams(("arbitrary",)),
    )(dwp_in)


def kernel(x, positions, w_in, q_norm_g, w_uq, kv_norm_g, w_ukv, sgu_norm_g, sgu_norm_b, w_spatial, b_spatial, w_out, ln_g, ln_b, loss_target, m_w_in, m_q_norm_g, m_w_uq, m_kv_norm_g, m_w_ukv, m_sgu_norm_g, m_sgu_norm_b, m_w_spatial, m_b_spatial, m_w_out, m_ln_g, m_ln_b, v_w_in, v_q_norm_g, v_w_uq, v_kv_norm_g, v_w_ukv, v_sgu_norm_g, v_sgu_norm_b, v_w_spatial, v_b_spatial, v_w_out, v_ln_g, v_ln_b):
    seq = x.shape[1]
    x2 = x.reshape(seq, D_MODEL)
    tgt2 = loss_target.reshape(seq, D_MODEL)
    pos_row = positions.reshape(1, seq)

    w_in_shards, w_out_shards, w_heads = _gather_two_level(
        [w_in.astype(BF16), w_out.astype(BF16), _head_slab(w_uq, w_ukv).astype(BF16)],
        name="wgather")
    (loss_part, grad_x, d_in, d_heads, d_out, d_ws, d_bs_t, d_lng, d_lnb, d_sgug, d_sgub, d_qg, d_kvg) = _local_step(
        x2, tgt2, pos_row, w_in_shards, w_heads, w_out_shards.reshape(D_MODEL, D_MODEL), q_norm_g, kv_norm_g,
        sgu_norm_g, sgu_norm_b, w_spatial, b_spatial, ln_g, ln_b)

    small_part = _pack_small([d_qg, d_kvg, d_sgug, d_sgub, d_bs_t[:, :HEADS].T, d_lng, d_lnb], last=loss_part[0, :1])
    mixed = jnp.concatenate([d_heads, d_ws, small_part.reshape(N_DEV, -1, LANES)], axis=1)
    by_chip = [g.reshape((N_CHIPS, 2) + g.shape[1:])
               for g in (d_in, d_out.reshape(N_DEV, D_MODEL // N_DEV, D_MODEL), mixed)]
    from_sibling = _sibling_swap(by_chip, name="gswap")
    core = lax.axis_index("c").astype(jnp.int32).reshape(1)
    pair_sums = [_pair_sum(a, b, core, name=nm, out_dtype=dt) for a, b, nm, dt in zip(
        by_chip, from_sibling, ("gsum_in", "gsum_out", "gsum_mixed"), (BF16, BF16, F32))]
    recv_in, recv_out, recv_mixed = _chip_exchange(pair_sums, name="gexch")

    res_in = [a.T for a in _adam(recv_in, w_in.T, m_w_in.T, v_w_in.T, name="adam_in", tile_rows=PROJ_TILE,
                                 transposed=True)]
    res_out = _adam(recv_out, w_out, m_w_out, v_w_out, name="adam_out", tile_rows=D_MODEL // N_DEV)
    res_mixed = _adam(recv_mixed, _head_slab(w_uq, w_ukv), _head_slab(m_w_uq, m_w_ukv), _head_slab(v_w_uq, v_w_ukv),
                      name="adam_mixed", tile_rows=MIXED_ROWS)

    rep_g, = _gather_direct([res_mixed[0]], name="sgather", first_row=HEAD_ROWS)
    res_rep, loss = _adam_replicated(
        rep_g,
        [q_norm_g, kv_norm_g, sgu_norm_g, sgu_norm_b, w_spatial, b_spatial, ln_g, ln_b],
        [m_q_norm_g, m_kv_norm_g, m_sgu_norm_g, m_sgu_norm_b, m_w_spatial, m_b_spatial, m_ln_g, m_ln_b],
        [v_q_norm_g, v_kv_norm_g, v_sgu_norm_g, v_sgu_norm_b, v_w_spatial, v_b_spatial, v_ln_g, v_ln_b])

    def ordered(which):
        r_qg, r_kvg, r_sg, r_sb, r_ws, r_bs, r_lg, r_lb = res_rep[which]
        heads = res_mixed[which]
        return [res_in[which], r_qg, heads[:Q_LORA, :UQ_SHARD], r_kvg, heads[Q_LORA:HEAD_ROWS], r_sg, r_sb, r_ws, r_bs,
                res_out[which], r_lg, r_lb]

    outs = [loss.reshape(()), grad_x.reshape(x.shape)]
    for which in range(4):
        outs += ordered(which)
    return tuple(outs)


def _local_step(x2, tgt2, pos_row, w_in_shards, w_heads, w_out_full, q_norm_g, kv_norm_g, sgu_norm_g, sgu_norm_b,
                w_spatial, b_spatial, ln_g, ln_b):
    half = jnp.arange(HALF, dtype=F32)
    invf_col = (1.0 / (ROPE_THETA ** (half / HALF))).reshape(HALF, 1)
    tri = jnp.tril(jnp.ones((CHUNK, CHUNK), dtype=bool))
    ws_low = jnp.where(tri[None], w_spatial, 0.0).astype(BF16)
    ws_low_t = ws_low.transpose(0, 2, 1)
    bsp = jnp.repeat(b_spatial.T, G_HEAD_DIM, axis=1)
    row = lambda a: a.reshape(1, -1)

    proj, q, k, v, vt, wp_in = _fwd_proj(x2, pos_row, invf_col, w_in_shards, w_heads, row(q_norm_g), row(kv_norm_g))
    o, lse_row = _attn_fwd(q, k, vt)
    (dr, do, d_row, drest, d_out, d_ws, d_bs_t, d_lng, d_lnb, d_sgug, d_sgub, loss_part) = _mid(
        x2, tgt2, proj, o, w_out_full, ws_low, ws_low_t, bsp, row(sgu_norm_g), row(sgu_norm_b), row(ln_g), row(ln_b))
    dqt, dk, dv = _attn_bwd(q, k, v, do, lse_row, d_row)
    grad_x, dwp_in, d_heads, d_qg, d_kvg = _bwd_tail(dqt, dk, dv, proj, pos_row, invf_col, w_heads, row(q_norm_g),
                                                      row(kv_norm_g), x2, dr, drest, wp_in)
    return (loss_part, grad_x, _w_in_shards(dwp_in), d_heads, d_out, d_ws, d_bs_t, d_lng, d_lnb, d_sgug, d_sgub,
            d_qg, d_kvg)
```

```python
import math

import jax
import jax.numpy as jnp
from jax import lax
from jax.experimental import pallas as pl
from jax.experimental.pallas import tpu as pltpu

F32 = jnp.float32
BF16 = jnp.bfloat16

N_DEV = 8
D_MODEL = 1024
HEADS = 8
NOPE = 64
ROPE = 32
HALF = ROPE // 2
VDIM = 64
Q_LORA = 256
KV_LORA = 128
G_WIDTH = 512
G_HEAD_DIM = 64
CHUNK = 128
HEAD_PAD = 128
D_IN = 2464
D_IN_PAD = 2560
KR_LO = NOPE
SUM_ROW = NOPE - 1
LIVE_ROWS = slice(NOPE - 16, HEAD_PAD)
ROPE_THETA = 10000.0
DN_ALPHA = 2.0 ** 0.25
EPS = 1e-5
ATTN_SCALE = 1.0 / math.sqrt(NOPE + ROPE)
ADAM_LR, ADAM_B1, ADAM_B2, ADAM_EPS, ADAM_WD, ADAM_STEP = 0.001, 0.9, 0.999, 1e-08, 0.01, 10

LANES = 128
REP_ROWS = 136
SMALL_LEN = 8192
VMEM_LIMIT = 56 * 1024 * 1024
ATTN_BWD_VMEM_LIMIT = 61 * 1024 * 1024
BWD_TAIL_VMEM_LIMIT = 61 * 1024 * 1024

TOKEN_TILE = 256
PROJ_TILE = 512
ATTN_FWD_WIDE = 2048
ATTN_BWD_WIDE = 2048
ATTN_NARROW = 512
SOFTMAX_ROWS = 512
LOG2E = 1.4426950408889634
LN2 = 0.6931471805599453
Q_PRESCALE = ATTN_SCALE * LOG2E


def _cparams(sem=None, vmem_limit=VMEM_LIMIT):
    return pltpu.CompilerParams(dimension_semantics=sem, vmem_limit_bytes=vmem_limit)


def _dot(a, b):
    return jnp.dot(a, b, preferred_element_type=F32)


def _dot_nt(a, b):
    return lax.dot_general(a, b, (((1,), (1,)), ((), ())), preferred_element_type=F32)


def _dot_tn(a, b):
    return lax.dot_general(a, b, (((0,), (0,)), ((), ())), preferred_element_type=F32)


def _sigmoid(z):
    return 1.0 / (1.0 + jnp.exp(-z))


def _normal_cdf(x):
    return 0.5 * (1.0 + lax.erf(x * 0.7071067811865476))


def _gelu_grad(x, cdf):
    return cdf + x * jnp.exp(-0.5 * x * x) * 0.3989422804014327


def _gather_direct(srcs, *, name, first_row=0):
    n = len(srcs)
    shapes = [(s.shape[0] - first_row,) + s.shape[1:] for s in srcs]

    def body(*refs):
        src_refs, out_refs = [r.at[pl.ds(first_row, shape[0])] for r, shape in zip(refs[:n], shapes)], refs[n:2 * n]
        send_sems, recv_sems, local_sems = refs[2 * n:]
        x, y, c = lax.axis_index("x"), lax.axis_index("y"), lax.axis_index("c")
        me = 4 * x + 2 * y + c
        mine = [pltpu.make_async_copy(src_refs[t], out_refs[t].at[me], local_sems.at[t]) for t in range(n)]
        for cp in mine:
            cp.start()
        sends, arrivals = [], []
        for k in (6, 7, 4, 5, 2, 3, 1):
            px = 1 - x if k & 4 else x
            py = 1 - y if k & 2 else y
            pc = 1 - c if k & 1 else c
            peer = 4 * px + 2 * py + pc
            for t in range(n):
                sem = (k - 1) * n + t
                cp = pltpu.make_async_remote_copy(
                    src_ref=src_refs[t], dst_ref=out_refs[t].at[me],
                    send_sem=send_sems.at[sem], recv_sem=recv_sems.at[sem],
                    device_id=(px, py, pc), device_id_type=pl.DeviceIdType.MESH)
                cp.start()
                sends.append(cp)
                arrivals.append(pltpu.make_async_remote_copy(
                    src_ref=src_refs[t], dst_ref=out_refs[t].at[peer],
                    send_sem=send_sems.at[sem], recv_sem=recv_sems.at[sem],
                    device_id=(x, y, c), device_id_type=pl.DeviceIdType.MESH))
        for cp in arrivals:
            cp.wait_recv()
        for cp in sends:
            cp.wait_send()
        for cp in mine:
            cp.wait()

    hbm = pl.BlockSpec(memory_space=pl.ANY)
    return pl.pallas_call(
        body, name=name,
        out_shape=[jax.ShapeDtypeStruct((N_DEV,) + shape, s.dtype) for shape, s in zip(shapes, srcs)],
        in_specs=[hbm] * n, out_specs=[hbm] * n,
        scratch_shapes=[pltpu.SemaphoreType.DMA(((N_DEV - 1) * n,)), pltpu.SemaphoreType.DMA(((N_DEV - 1) * n,)),
                        pltpu.SemaphoreType.DMA((n,))],
    )(*srcs)


def _gather_two_level(srcs, *, name):
    n = len(srcs)

    def body(*refs):
        src_refs, out_refs = refs[:n], refs[n:2 * n]
        send_sems, recv_sems, local_sems = refs[2 * n:]
        x, y, c = lax.axis_index("x"), lax.axis_index("y"), lax.axis_index("c")
        me, sibling = (x, y, c), (x, y, 1 - c)
        chips = [(1 - x, 1 - y), (1 - x, y), (x, 1 - y)]
        index = lambda px, py, pc: 4 * px + 2 * py + pc

        def copy(k, t, block, to, src=None):
            place = out_refs[t].at[index(*block)]
            return pltpu.make_async_remote_copy(
                src_ref=place if src is None else src, dst_ref=place,
                send_sem=send_sems.at[k * n + t], recv_sem=recv_sems.at[k * n + t],
                device_id=to, device_id_type=pl.DeviceIdType.MESH)

        mine = [pltpu.make_async_copy(src_refs[t], out_refs[t].at[index(*me)], local_sems.at[t]) for t in range(n)]
        for cp in mine:
            cp.start()
        first = [copy(1 + j, t, me, (*chip, c), src=src_refs[t]) for j, chip in enumerate(chips) for t in range(n)]
        first += [copy(0, t, me, sibling, src=src_refs[t]) for t in range(n)]
        for cp in first:
            cp.start()
        passed = []
        for j, chip in enumerate(chips):
            for t in range(n):
                copy(1 + j, t, (*chip, c), me).wait_recv()
                cp = copy(4 + j, t, (*chip, c), sibling)
                cp.start()
                passed.append(cp)
        for t in range(n):
            copy(0, t, sibling, me).wait_recv()
        for j, chip in enumerate(chips):
            for t in range(n):
                copy(4 + j, t, (*chip, 1 - c), me).wait_recv()
        for cp in first + passed:
            cp.wait_send()
        for cp in mine:
            cp.wait()

    hbm = pl.BlockSpec(memory_space=pl.ANY)
    return pl.pallas_call(
        body, name=name,
        out_shape=[jax.ShapeDtypeStruct((N_DEV,) + s.shape, s.dtype) for s in srcs],
        in_specs=[hbm] * n, out_specs=[hbm] * n,
        scratch_shapes=[pltpu.SemaphoreType.DMA((7 * n,)), pltpu.SemaphoreType.DMA((7 * n,)),
                        pltpu.SemaphoreType.DMA((n,))],
    )(*srcs)


N_CHIPS = N_DEV // 2


def _sibling_swap(srcs, *, name):
    n = len(srcs)

    def body(*refs):
        src_refs, out_refs = refs[:n], refs[n:2 * n]
        send_sems, recv_sems = refs[2 * n:]
        x, y, c = lax.axis_index("x"), lax.axis_index("y"), lax.axis_index("c")
        sends = []
        for chip in range(N_CHIPS):
            for t in range(n):
                cp = pltpu.make_async_remote_copy(
                    src_ref=src_refs[t].at[chip, 1 - c], dst_ref=out_refs[t].at[chip],
                    send_sem=send_sems.at[chip * n + t], recv_sem=recv_sems.at[chip * n + t],
                    device_id=(x, y, 1 - c), device_id_type=pl.DeviceIdType.MESH)
                cp.start()
                sends.append(cp)
        for cp in sends:
            cp.wait_recv()
        for cp in sends:
            cp.wait_send()

    hbm = pl.BlockSpec(memory_space=pl.ANY)
    return pl.pallas_call(
        body, name=name,
        out_shape=[jax.ShapeDtypeStruct((N_CHIPS,) + s.shape[2:], s.dtype) for s in srcs],
        in_specs=[hbm] * n, out_specs=[hbm] * n,
        scratch_shapes=[pltpu.SemaphoreType.DMA((N_CHIPS * n,)), pltpu.SemaphoreType.DMA((N_CHIPS * n,))],
    )(*srcs)


def _pair_sum(mine, theirs, core, *, name, out_dtype):
    _, _, rows, cols = mine.shape

    def body(core_ref, a_ref, b_ref, o_ref):
        o_ref[...] = (a_ref[0] + b_ref[...]).astype(out_dtype)

    return pl.pallas_call(
        body, name=name,
        grid_spec=pltpu.PrefetchScalarGridSpec(
            num_scalar_prefetch=1, grid=(N_CHIPS,),
            in_specs=[pl.BlockSpec((1, 1, rows, cols), lambda q, core_ref: (q, core_ref[0], 0, 0)),
                      pl.BlockSpec((1, rows, cols), lambda q, core_ref: (q, 0, 0))],
            out_specs=pl.BlockSpec((1, rows, cols), lambda q, core_ref: (q, 0, 0))),
        out_shape=jax.ShapeDtypeStruct((N_CHIPS, rows, cols), out_dtype),
        compiler_params=_cparams(("arbitrary",)),
    )(core, mine, theirs)


def _chip_exchange(srcs, *, name):
    n = len(srcs)

    def body(*refs):
        src_refs, out_refs = refs[:n], refs[n:2 * n]
        send_sems, recv_sems, local_sems = refs[2 * n:]
        x, y, c = lax.axis_index("x"), lax.axis_index("y"), lax.axis_index("c")
        my_chip = 2 * x + y
        mine = [pltpu.make_async_copy(src_refs[t].at[my_chip], out_refs[t].at[my_chip], local_sems.at[t])
                for t in range(n)]
        for cp in mine:
            cp.start()
        sends, arrivals = [], []
        for k in (3, 2, 1):
            px = 1 - x if k & 2 else x
            py = 1 - y if k & 1 else y
            peer_chip = 2 * px + py
            for t in range(n):
                sem = (k - 1) * n + t
                cp = pltpu.make_async_remote_copy(
                    src_ref=src_refs[t].at[peer_chip], dst_ref=out_refs[t].at[my_chip],
                    send_sem=send_sems.at[sem], recv_sem=recv_sems.at[sem],
                    device_id=(px, py, c), device_id_type=pl.DeviceIdType.MESH)
                cp.start()
                sends.append(cp)
                arrivals.append(pltpu.make_async_remote_copy(
                    src_ref=src_refs[t].at[peer_chip], dst_ref=out_refs[t].at[peer_chip],
                    send_sem=send_sems.at[sem], recv_sem=recv_sems.at[sem],
                    device_id=(x, y, c), device_id_type=pl.DeviceIdType.MESH))
        for cp in arrivals:
            cp.wait_recv()
        for cp in sends:
            cp.wait_send()
        for cp in mine:
            cp.wait()

    hbm = pl.BlockSpec(memory_space=pl.ANY)
    return pl.pallas_call(
        body, name=name,
        out_shape=[jax.ShapeDtypeStruct(s.shape, s.dtype) for s in srcs],
        in_specs=[hbm] * n, out_specs=[hbm] * n,
        scratch_shapes=[pltpu.SemaphoreType.DMA((3 * n,)), pltpu.SemaphoreType.DMA((3 * n,)),
                        pltpu.SemaphoreType.DMA((n,))],
    )(*srcs)


def _rope_tables(pos_row, invf_col):
    tm = pos_row.shape[1]
    ang = pos_row.astype(F32) * invf_col
    cos, sin = jnp.cos(ang), jnp.sin(ang)
    ones = lambda n: jnp.ones((n, tm), F32)
    zeros = lambda n: jnp.zeros((n, tm), F32)
    cos_t = jnp.concatenate([ones(KR_LO), cos, cos, ones(LANES - KR_LO - ROPE)], axis=0)
    first_t = jnp.concatenate([zeros(KR_LO), sin, zeros(LANES - KR_LO - HALF)], axis=0)
    second_t = jnp.concatenate([zeros(KR_LO + HALF), sin, zeros(LANES - KR_LO - ROPE)], axis=0)
    return jnp.transpose(cos_t), jnp.transpose(first_t), jnp.transpose(second_t)


def _rope(t, cos, sin_first, sin_second, sign):
    up = pltpu.roll(t, LANES - HALF, 1)
    down = pltpu.roll(t, HALF, 1)
    return t * cos - sign * (up * sin_first) + sign * (down * sin_second)


def _fwd_proj(x, pos_row, invf_col, w_in_shards, w_heads, q_g, kv_g):
    t = x.shape[0]
    tm = PROJ_TILE
    n_steps = t // tm
    n_latent = Q_LORA + KV_LORA + LANES

    def body(x_ref, pos_ref, invf_ref, sh_ref, wh_ref, qg_ref, kvg_ref,
             proj_ref, q_ref, k_ref, v_ref, vt_ref, win_ref, latent_even, latent_odd):
        step = pl.program_id(0)
        latents = (latent_even, latent_odd)

        def project(latent_ref):
            proj = _dot(x_ref[...].astype(BF16), win_ref[...])
            proj_ref[...] = proj
            latent_ref[...] = proj[:, :n_latent]

        def heads(latent_ref):
            c_q = latent_ref[:, :Q_LORA]
            c_kv = latent_ref[:, Q_LORA:Q_LORA + KV_LORA]
            kr_raw = latent_ref[:, Q_LORA + KV_LORA:]
            cqn = (c_q * lax.rsqrt(jnp.mean(c_q * c_q, axis=-1, keepdims=True) + EPS) * qg_ref[...]).astype(BF16)
            ckvn = (c_kv * lax.rsqrt(jnp.mean(c_kv * c_kv, axis=-1, keepdims=True) + EPS) * kvg_ref[...]).astype(BF16)
            cos, s1, s2 = _rope_tables(pos_ref[...], invf_ref[...])
            kr = _rope(kr_raw, cos, s1, s2, 1.0)
            lane = lax.broadcasted_iota(jnp.int32, (tm, HEAD_PAD), 1)
            q_all = _dot(cqn, jnp.concatenate([wh_ref[h, :Q_LORA, :] for h in range(HEADS)], axis=1))
            kv_all = _dot(ckvn, jnp.concatenate([wh_ref[h, Q_LORA:, :] for h in range(HEADS)], axis=1))
            for h in range(HEADS):
                q_h = q_all[:, h * HEAD_PAD:(h + 1) * HEAD_PAD]
                kv_h = kv_all[:, h * HEAD_PAD:(h + 1) * HEAD_PAD]
                q_ref[h] = (_rope(q_h, cos, s1, s2, 1.0) * Q_PRESCALE).astype(BF16)
                k_ref[h] = jnp.where(lane < NOPE, kv_h, kr).astype(BF16)
                v_ref[h] = kv_h.astype(BF16)
                vt_ref[h] = jnp.transpose(jnp.where(lane == SUM_ROW, 1.0, kv_h)).astype(BF16)

        @pl.when(step == 0)
        def _():
            win_ref[...] = jnp.zeros_like(win_ref)
            for s, src, dst, width in _w_in_pieces():
                win_ref[:, dst:dst + width] = sh_ref[s, :, src:src + width]
            project(latents[0])

        for parity in (0, 1):
            @pl.when((step > 0) & (step < n_steps) & (step % 2 == parity))
            def _():
                heads(latents[1 - parity])
                project(latents[parity])

        @pl.when(step == n_steps)
        def _():
            heads(latents[(n_steps - 1) % 2])

    full = lambda a: pl.BlockSpec(a.shape, lambda i: (0,) * a.ndim)
    this = lambda i: jnp.minimum(i, n_steps - 1)
    prev = lambda i: jnp.maximum(i - 1, 0)
    head_spec = pl.BlockSpec((HEADS, tm, HEAD_PAD), lambda i: (0, prev(i), 0))
    head_shape = jax.ShapeDtypeStruct((HEADS, t, HEAD_PAD), BF16)
    return pl.pallas_call(
        body, name="fwd_proj", grid=(n_steps + 1,),
        in_specs=[pl.BlockSpec((tm, D_MODEL), lambda i: (this(i), 0)), pl.BlockSpec((1, tm), lambda i: (0, prev(i))),
                  full(invf_col), full(w_in_shards), full(w_heads), full(q_g), full(kv_g)],
        out_specs=[pl.BlockSpec((tm, D_IN_PAD), lambda i: (this(i), 0)), head_spec, head_spec, head_spec,
                   pl.BlockSpec((HEADS, HEAD_PAD, tm), lambda i: (0, 0, prev(i))),
                   pl.BlockSpec((D_MODEL, D_IN_PAD), lambda i: (0, 0))],
        out_shape=[jax.ShapeDtypeStruct((t, D_IN_PAD), F32), head_shape, head_shape, head_shape,
                   jax.ShapeDtypeStruct((HEADS, HEAD_PAD, t), BF16),
                   jax.ShapeDtypeStruct((D_MODEL, D_IN_PAD), w_in_shards.dtype)],
        scratch_shapes=[pltpu.VMEM((tm, n_latent), F32)] * 2,
        compiler_params=_cparams(("arbitrary",)),
    )(x, pos_row, invf_col, w_in_shards, w_heads, q_g, kv_g)


def _attn_fwd(q, k, vt):
    t = q.shape[1]
    bq, bk = ATTN_FWD_WIDE, ATTN_NARROW
    n_diag = bq // bk
    chunk = SOFTMAX_ROWS

    def body(q_ref, k_ref, vt_ref, o_ref, lse_ref, s0, s1, p0, p1, x0, x1, m_scr, a_scr, acc_scr):
        i = pl.program_id(1)
        at = lambda j: pl.ds(pl.multiple_of(j * bk, bk), bk)

        def exp_pass(s_in, block_max, p_out, diagonal=False, cols=slice(None)):
            width = bq if cols == slice(None) else cols.stop - cols.start

            def load(r):
                s = s_in[r:r + chunk, cols]
                if diagonal:
                    key = lax.broadcasted_iota(jnp.int32, (chunk, width), 0) + r
                    qry = lax.broadcasted_iota(jnp.int32, (chunk, width), 1)
                    s = jnp.where(qry >= key, s, -jnp.inf)
                return s

            if diagonal:
                block_max = jnp.max(load(0), axis=0, keepdims=True)
                for r in range(chunk, bk, chunk):
                    block_max = jnp.maximum(block_max, jnp.max(load(r), axis=0, keepdims=True))
            m_old = m_scr[:, cols]
            m_new = jnp.maximum(m_old, block_max)
            alpha = jnp.exp2(m_old - m_new)
            for r in range(0, bk, chunk):
                p_out[r:r + chunk, cols] = jnp.exp2(load(r) - m_new).astype(BF16)
            m_scr[:, cols] = m_new
            return alpha

        def scores(j, s_out, x_out):
            s = _dot_nt(k_ref[0, at(j), :], q_ref[0])
            s_out[...] = s
            x_out[...] = jnp.max(s, axis=0, keepdims=True)

        def value_product(j, p_in):
            return _dot(vt_ref[0, LIVE_ROWS, at(j)], p_in[...])

        def one_pass(j, s_in, x_in, s_out, x_out, p_prev, p_cur):
            scores(j + 1, s_out, x_out)
            acc_scr[...] = a_scr[...] * acc_scr[...] + value_product(jnp.maximum(j - 1, 0), p_prev)
            a_scr[...] = exp_pass(s_in, x_in[...], p_cur)

        scores(0, s0, x0)
        p1[...] = jnp.zeros_like(p1)
        a_scr[...] = jnp.ones_like(a_scr)
        m_scr[...] = jnp.full(m_scr.shape, -jnp.inf, F32)
        acc_scr[...] = jnp.zeros_like(acc_scr)

        def two_passes(n, _):
            one_pass(2 * n, s0, x0, s1, x1, p1, p0)
            one_pass(2 * n + 1, s1, x1, s0, x0, p0, p1)
            return 0

        lax.fori_loop(0, (n_diag // 2) * i, two_passes, 0)
        d = n_diag * i
        alpha, p_prev, cols = a_scr[...], p1, slice(0, bq)
        for u in range(n_diag + 1):
            s_in, s_next, p_cur = (s0, s1, p0) if u % 2 == 0 else (s1, s0, p1)
            if u + 1 < n_diag:
                ahead = slice((u + 1) * bk, bq)
                s_next[:, ahead] = _dot_nt(k_ref[0, at(d + u + 1), :], q_ref[0, ahead, :])
            acc_scr[:, cols] = alpha * acc_scr[:, cols] + _dot(vt_ref[0, LIVE_ROWS, at(jnp.maximum(d + u - 1, 0))],
                                                               p_prev[:, cols])
            if u < n_diag:
                cols = slice(u * bk, bq)
                alpha = exp_pass(s_in, None, p_cur, diagonal=True, cols=cols)
                p_prev = p_cur
        denom = acc_scr[SUM_ROW - LIVE_ROWS.start:NOPE - LIVE_ROWS.start, :]
        o = jnp.transpose(acc_scr[NOPE - LIVE_ROWS.start:, :] / denom)
        o_ref[0] = jnp.concatenate([jnp.zeros_like(o), o], axis=1)
        lse_ref[0] = m_scr[...] + jnp.log2(denom)

    tile = lambda dtype: pltpu.VMEM((bk, bq), dtype)
    stat = pltpu.VMEM((1, bq), F32)
    return pl.pallas_call(
        body, name="attn_fwd", grid=(HEADS, t // bq),
        in_specs=[pl.BlockSpec((1, bq, HEAD_PAD), lambda h, i: (h, i, 0)),
                  pl.BlockSpec((1, t, HEAD_PAD), lambda h, i: (h, 0, 0)),
                  pl.BlockSpec((1, HEAD_PAD, t), lambda h, i: (h, 0, 0))],
        out_specs=[pl.BlockSpec((1, bq, HEAD_PAD), lambda h, i: (h, i, 0)),
                   pl.BlockSpec((1, 1, bq), lambda h, i: (h, 0, i))],
        out_shape=[jax.ShapeDtypeStruct((HEADS, t, HEAD_PAD), F32), jax.ShapeDtypeStruct((HEADS, 1, t), F32)],
        scratch_shapes=[tile(F32), tile(F32), tile(BF16), tile(BF16), stat, stat, stat, stat,
                        pltpu.VMEM((HEAD_PAD - LIVE_ROWS.start, bq), F32)],
        compiler_params=_cparams(("arbitrary", "arbitrary")),
    )(q, k, vt)


def _mid(x, target, proj, ol, w_out, ws_low, ws_low_t, bsp, sgu_g, sgu_b, ln_g, ln_b):
    t = x.shape[0]
    tm = TOKEN_TILE
    n_steps = t // tm

    def body(x_ref, tgt_ref, za_ref, u_ref, v_ref, zb_ref, ol_ref, prev_za_ref, prev_u_ref, prev_v_ref, prev_zb_ref,
             prev_ol_ref, wout_ref, ws_ref, wst_ref, bsp_ref, sg_ref, sb_ref, lg_ref, lb_ref,
             dr_ref, do_ref, drow_ref, drest_ref, dwout_ref, dws_ref, dbs_ref, dlg_ref, dlb_ref, dsg_ref, dsb_ref,
             loss_ref, dbsp_acc, *kept_refs):
        step = pl.program_id(0)
        kept_sets = (kept_refs[:len(kept_refs) // 2], kept_refs[len(kept_refs) // 2:])

        @pl.when(step == 0)
        def _():
            dwout_ref[...] = jnp.zeros_like(dwout_ref)
            dws_ref[...] = jnp.zeros_like(dws_ref)
            dbs_ref[...] = jnp.zeros_like(dbs_ref)
            dlg_ref[...] = jnp.zeros_like(dlg_ref)
            dlb_ref[...] = jnp.zeros_like(dlb_ref)
            dsg_ref[...] = jnp.zeros_like(dsg_ref)
            dsb_ref[...] = jnp.zeros_like(dsb_ref)
            loss_ref[...] = jnp.zeros_like(loss_ref)
            dbsp_acc[...] = jnp.zeros_like(dbsp_acc)

        n_chunks = tm // CHUNK
        groups = G_WIDTH // LANES

        def side_by_side(a):
            return [jnp.concatenate([a[c * CHUNK:(c + 1) * CHUNK, g * LANES:(g + 1) * LANES] for c in range(n_chunks)],
                                    axis=1) for g in range(groups)]

        def by_chunk(wide):
            return jnp.concatenate([jnp.concatenate([wide[g][:, c * LANES:(c + 1) * LANES] for g in range(groups)], axis=1)
                                    for c in range(n_chunks)], axis=0)

        def own_lanes(h):
            lane = lax.broadcasted_iota(jnp.int32, (CHUNK, n_chunks * LANES), 1)
            return (lane % LANES) // G_HEAD_DIM == h % 2

        def spatial(w_ref, wide):
            return [sum(jnp.where(own_lanes(h), _dot(w_ref[h], wide[g]), 0.0) for h in (2 * g, 2 * g + 1))
                    for g in range(groups)]

        def value_lanes(o_ref):
            return jnp.concatenate([o_ref[h][:, NOPE:] for h in range(HEADS)], axis=-1)

        def forward(kept):
            attn = value_lanes(ol_ref)
            za = za_ref[...]
            sig_a = _sigmoid(za)
            out_a = attn * (za * sig_a)
            u = u_ref[...]
            cdf_u = _normal_cdf(u)
            vpre = v_ref[...]
            cdf_v = _normal_cdf(vpre)
            gv = vpre * cdf_v
            mu_v = jnp.mean(gv, axis=-1, keepdims=True)
            cen_v = gv - mu_v
            rstd_v = lax.rsqrt(jnp.mean(cen_v * cen_v, axis=-1, keepdims=True) + EPS)
            vhat = cen_v * rstd_v
            vg = vhat * sg_ref[...] + sb_ref[...]
            vg_b = vg.astype(BF16)
            yield
            sv = by_chunk(spatial(ws_ref, side_by_side(vg_b))) + jnp.tile(bsp_ref[...], (n_chunks, 1))
            zb = zb_ref[...]
            sig_b = _sigmoid(zb)
            out_b = ((u * cdf_u) * sv) * (zb * sig_b)
            merged = jnp.concatenate([out_a, out_b], axis=-1).astype(BF16)
            yield
            r = DN_ALPHA * x_ref[...] + _dot(merged, wout_ref[...])
            mu = jnp.mean(r, axis=-1, keepdims=True)
            cen = r - mu
            rstd = lax.rsqrt(jnp.mean(cen * cen, axis=-1, keepdims=True) + EPS)
            xhat = cen * rstd
            hout = xhat * lg_ref[...] + lb_ref[...]
            err = hout - tgt_ref[...]
            row_loss = jnp.mean(err * err, axis=-1, keepdims=True)
            loss_ref[...] += jnp.broadcast_to(0.5 * jnp.sum(row_loss, axis=0, keepdims=True), loss_ref.shape)
            for ref, val in zip(kept, (sig_a, cdf_u, cdf_v, vhat, sv, sig_b, xhat, err * (1.0 / D_MODEL), merged, vg_b,
                                       jnp.broadcast_to(rstd, (tm, LANES)), jnp.broadcast_to(rstd_v, (tm, LANES)))):
                ref[...] = val

        def backward(kept):
            (sig_a_ref, cdf_u_ref, cdf_v_ref, vhat_ref, sv_ref, sig_b_ref, xhat_ref, dh_ref, merged_ref, vg_ref,
             rstd_ref, rstd_v_ref) = kept
            xhat, dh, rstd = xhat_ref[...], dh_ref[...], rstd_ref[:, :1]
            dlg_ref[...] += jnp.sum(dh * xhat, axis=0, keepdims=True)
            dlb_ref[...] += jnp.sum(dh, axis=0, keepdims=True)
            dxhat = dh * lg_ref[...]
            dr = rstd * (dxhat - jnp.mean(dxhat, axis=-1, keepdims=True)
                         - xhat * jnp.mean(dxhat * xhat, axis=-1, keepdims=True))
            dr_ref[...] = dr
            dr_b = dr.astype(BF16)
            yield
            dwout_ref[...] += _dot_tn(merged_ref[...], dr_b)
            dmerged = _dot_nt(dr_b, wout_ref[...])
            yield
            attn = value_lanes(prev_ol_ref)
            za, u, zb = prev_za_ref[...], prev_u_ref[...], prev_zb_ref[...]
            sig_a, cdf_u, sv, sig_b = sig_a_ref[...], cdf_u_ref[...], sv_ref[...], sig_b_ref[...]
            silu_a, silu_b, ug = za * sig_a, zb * sig_b, u * cdf_u
            sgu = ug * sv
            d_out_a = dmerged[:, :G_WIDTH]
            d_out_b = dmerged[:, G_WIDTH:]
            dattn = d_out_a * silu_a
            for h in range(HEADS):
                do_h = dattn[:, h * VDIM:(h + 1) * VDIM]
                do_ref[h] = jnp.concatenate([jnp.zeros((tm, NOPE), F32), do_h], axis=-1).astype(BF16)
            feature = lax.broadcasted_iota(jnp.int32, (G_WIDTH, LANES), 0) // VDIM
            column = lax.broadcasted_iota(jnp.int32, (G_WIDTH, LANES), 1)
            head_sums = jnp.dot(dattn * attn, jnp.where(feature == column, 1.0, 0.0).astype(F32),
                                preferred_element_type=F32, precision=lax.Precision.HIGH)
            dsums_t = jnp.transpose(head_sums)
            for h in range(HEADS):
                drow_ref[h] = dsums_t[h:h + 1, :]
            dza = d_out_a * attn * (sig_a * (1.0 + za * (1.0 - sig_a)))
            dsgu = d_out_b * silu_b
            dzb = d_out_b * sgu * (sig_b * (1.0 + zb * (1.0 - sig_b)))
            du = dsgu * sv * _gelu_grad(u, cdf_u)
            dsv = dsgu * ug
            dsv_b = dsv.astype(BF16)
            for cix in range(n_chunks):
                dbsp_acc[...] += dsv[cix * CHUNK:(cix + 1) * CHUNK, :]
            yield
            vpre, cdf_v, vhat, vg_b, rstd_v = (prev_v_ref[...], cdf_v_ref[...], vhat_ref[...], vg_ref[...],
                                               rstd_v_ref[:, :1])
            dsv_wide, vg_wide = side_by_side(dsv_b), side_by_side(vg_b)
            dvg = by_chunk(spatial(wst_ref, dsv_wide))
            for h in range(HEADS):
                mine = jnp.where(own_lanes(h), dsv_wide[h // 2], jnp.zeros_like(dsv_wide[h // 2]))
                dws_ref[h] += _dot_nt(mine, vg_wide[h // 2])
            dsg_ref[...] += jnp.sum(dvg * vhat, axis=0, keepdims=True)
            dsb_ref[...] += jnp.sum(dvg, axis=0, keepdims=True)
            dvhat = dvg * sg_ref[...]
            dgv = rstd_v * (dvhat - jnp.mean(dvhat, axis=-1, keepdims=True)
                            - vhat * jnp.mean(dvhat * vhat, axis=-1, keepdims=True))
            dv = dgv * _gelu_grad(vpre, cdf_v)
            drest_ref[...] = jnp.concatenate([dza, du, dv, dzb], axis=-1).astype(BF16)

        def emit(order, **stages):
            for who in order:
                next(stages[who], None)

        @pl.when(step == 0)
        def _():
            emit("fff", f=forward(kept_sets[0]))

        for parity in (0, 1):
            @pl.when((step > 0) & (step < n_steps) & (step % 2 == parity))
            def _():
                emit("ffbbfbb", f=forward(kept_sets[parity]), b=backward(kept_sets[1 - parity]))

        @pl.when(step == n_steps)
        def _():
            emit("bbbb", b=backward(kept_sets[(n_steps - 1) % 2]))
            tri = (lax.broadcasted_iota(jnp.int32, (CHUNK, CHUNK), 0)
                   >= lax.broadcasted_iota(jnp.int32, (CHUNK, CHUNK), 1))
            for h in range(HEADS):
                dws_ref[h] = jnp.where(tri, dws_ref[h], 0.0)
            tot = dbsp_acc[...]
            lane = lax.broadcasted_iota(jnp.int32, (CHUNK, LANES), 1)
            dbs = jnp.zeros((CHUNK, LANES), F32)
            for h in range(HEADS):
                head_sum = jnp.sum(tot[:, h * G_HEAD_DIM:(h + 1) * G_HEAD_DIM], axis=-1, keepdims=True)
                dbs = jnp.where(lane == h, head_sum, dbs)
            dbs_ref[...] = dbs

    full = lambda a: pl.BlockSpec(a.shape, lambda i: (0,) * a.ndim)
    this = lambda i: jnp.minimum(i, n_steps - 1)
    prev = lambda i: jnp.maximum(i - 1, 0)
    tile = lambda w, j=0, at=this: pl.BlockSpec((tm, w), lambda i, j=j: (at(i), j))
    heads = lambda at: pl.BlockSpec((HEADS, tm, HEAD_PAD), lambda i: (0, at(i), 0))
    acc = lambda shape: (pl.BlockSpec(shape, lambda i: (0,) * len(shape)), jax.ShapeDtypeStruct(shape, F32))
    accs = [acc((D_MODEL, D_MODEL)), acc((HEADS, CHUNK, CHUNK)), acc((CHUNK, LANES)), acc((1, D_MODEL)),
            acc((1, D_MODEL)), acc((1, G_WIDTH)), acc((1, G_WIDTH)), acc((1, LANES))]
    kept = ([pltpu.VMEM((tm, G_WIDTH), F32)] * 6 + [pltpu.VMEM((tm, D_MODEL), F32)] * 2
            + [pltpu.VMEM((tm, D_MODEL), BF16), pltpu.VMEM((tm, G_WIDTH), BF16)] + [pltpu.VMEM((tm, LANES), F32)] * 2)
    return pl.pallas_call(
        body, name="mid", grid=(n_steps + 1,),
        in_specs=[tile(D_MODEL), tile(D_MODEL), tile(G_WIDTH, 1), tile(G_WIDTH, 2), tile(G_WIDTH, 3), tile(G_WIDTH, 4),
                  heads(this), tile(G_WIDTH, 1, prev), tile(G_WIDTH, 2, prev), tile(G_WIDTH, 3, prev),
                  tile(G_WIDTH, 4, prev), heads(prev),
                  full(w_out), full(ws_low), full(ws_low_t), full(bsp), full(sgu_g), full(sgu_b),
                  full(ln_g), full(ln_b)],
        out_specs=[tile(D_MODEL, 0, prev), heads(prev), pl.BlockSpec((HEADS, 1, tm), lambda i: (0, 0, prev(i))),
                   tile(4 * G_WIDTH, 0, prev)]
        + [a[0] for a in accs],
        out_shape=[jax.ShapeDtypeStruct((t, D_MODEL), F32), jax.ShapeDtypeStruct((HEADS, t, HEAD_PAD), BF16),
                   jax.ShapeDtypeStruct((HEADS, 1, t), F32), jax.ShapeDtypeStruct((t, 4 * G_WIDTH), BF16)]
        + [a[1] for a in accs],
        scratch_shapes=[pltpu.VMEM((CHUNK, G_WIDTH), F32)] + kept + kept,
        compiler_params=_cparams(("arbitrary",)),
    )(x, target, proj, proj, proj, proj, ol, proj, proj, proj, proj, ol,
      w_out, ws_low, ws_low_t, bsp, sgu_g, sgu_b, ln_g, ln_b)


def _attn_bwd(q, k, v, do, lse_row, d_row):
    t = q.shape[1]
    bk, bq = ATTN_BWD_WIDE, ATTN_NARROW
    n_diag = bk // bq
    half = bq // 2
    last = t // bq - 1
    chunk = SOFTMAX_ROWS

    def body(q_ref, k_ref, v_ref, do_ref, lse_ref, drow_ref, dqt_ref, dk_ref, dv_ref,
             s0, s1, e0, e1, p0, p1, g0, g1, kt_scr):
        j = pl.program_id(1)
        at = lambda i: pl.ds(pl.multiple_of(i * bq, bq), bq)

        @pl.when(j == 0)
        def _():
            dqt_ref[...] = jnp.zeros_like(dqt_ref)

        kt_scr[...] = jnp.transpose(k_ref[0].astype(F32)).astype(BF16)
        dk_ref[...] = jnp.zeros_like(dk_ref)
        dv_ref[...] = jnp.zeros_like(dv_ref)

        whole_tile = ((slice(0, bk), slice(0, bq)),)

        def queries(i, lanes):
            return pl.ds(pl.multiple_of(i * bq + lanes.start, half), lanes.stop - lanes.start)

        def products(i, s_out, e_out, areas=whole_tile):
            i = jnp.minimum(i, last)
            for keys, lanes in areas:
                s_out[keys, lanes] = _dot_nt(k_ref[0, keys, :], q_ref[0, queries(i, lanes), :])
                e_out[keys, lanes] = _dot_nt(v_ref[0, keys, :], do_ref[0, queries(i, lanes), :])

        def gradients(i, p_in, g_in, areas=whole_tile):
            for keys, lanes in areas:
                dv_ref[0, keys, :] += _dot(p_in[keys, lanes], do_ref[0, queries(i, lanes), :])
                dk_ref[0, keys, :] += _dot(g_in[keys, lanes], q_ref[0, queries(i, lanes), :])
                dqt_ref[0, :, queries(i, lanes)] += _dot(kt_scr[:, keys], g_in[keys, lanes])

        def elementwise(i, s_in, e_in, p_out, g_out, qry0=None, areas=whole_tile):
            for keys, lanes in areas:
                width = lanes.stop - lanes.start
                step = chunk if qry0 is None else half
                lse = lse_ref[0, :, queries(i, lanes)]
                dsum = drow_ref[0, :, queries(i, lanes)]
                for r in range(keys.start, keys.stop, step):
                    p = jnp.exp2(s_in[r:r + step, lanes] - lse)
                    if qry0 is not None:
                        key = lax.broadcasted_iota(jnp.int32, (step, width), 0) + r
                        qry = lax.broadcasted_iota(jnp.int32, (step, width), 1) + (qry0 + lanes.start)
                        p = jnp.where(qry >= key, p, 0.0)
                    p_out[r:r + step, lanes] = p.astype(BF16)
                    g_out[r:r + step, lanes] = (p * (e_in[r:r + step, lanes] - dsum)).astype(BF16)

        def one_pass(i, s_in, e_in, s_out, e_out, p_prev, g_prev, p_cur, g_cur):
            products(i + 1, s_out, e_out)
            gradients(i - 1, p_prev, g_prev)
            elementwise(i, s_in, e_in, p_cur, g_cur)

        first = n_diag * j

        def areas_of(u):
            if u >= n_diag:
                return whole_tile
            return ((slice(0, u * bq + half), slice(0, bq)), (slice(u * bq + half, (u + 1) * bq), slice(half, bq)))

        even, odd = (s0, e0, p0, g0), (s1, e1, p1, g1)
        products(first, s0, e0, areas_of(0))
        products(first + 1, s1, e1, areas_of(1))
        elementwise(first, s0, e0, p0, g0, qry0=0, areas=areas_of(0))
        for u in range(1, n_diag):
            (s_in, e_in, p_cur, g_cur), (s_out, e_out, p_prev, g_prev) = (odd, even) if u % 2 else (even, odd)
            products(first + u + 1, s_out, e_out, areas_of(u + 1))
            gradients(first + u - 1, p_prev, g_prev, areas_of(u - 1))
            elementwise(first + u, s_in, e_in, p_cur, g_cur, qry0=u * bq, areas=areas_of(u))
        corner = (slice(bk - half, bk), slice(0, half))
        p1[corner] = jnp.zeros((half, half), BF16)
        g1[corner] = jnp.zeros((half, half), BF16)

        def two_passes(n, _):
            i = first + n_diag + 2 * n
            one_pass(i, s0, e0, s1, e1, p1, g1, p0, g0)
            one_pass(i + 1, s1, e1, s0, e0, p0, g0, p1, g1)
            return 0

        lax.fori_loop(0, (last - first - n_diag + 1) // 2, two_passes, 0)
        gradients(last, p1, g1)
        dk_ref[0] = dk_ref[0] * LN2

    whole = pl.BlockSpec((1, t, HEAD_PAD), lambda h, j: (h, 0, 0))
    block = pl.BlockSpec((1, bk, HEAD_PAD), lambda h, j: (h, j, 0))
    rows = pl.BlockSpec((1, 1, t), lambda h, j: (h, 0, 0), pipeline_mode=pl.Buffered(1))
    shape = jax.ShapeDtypeStruct((HEADS, t, HEAD_PAD), F32)
    tile = lambda dtype: pltpu.VMEM((bk, bq), dtype)
    return pl.pallas_call(
        body, name="attn_bwd", grid=(HEADS, t // bk),
        in_specs=[whole, block, block, whole, rows, rows],
        out_specs=[pl.BlockSpec((1, HEAD_PAD, t), lambda h, j: (h, 0, 0)), block, block],
        out_shape=[jax.ShapeDtypeStruct((HEADS, HEAD_PAD, t), F32), shape, shape],
        scratch_shapes=[tile(F32), tile(F32), tile(F32), tile(F32), tile(BF16), tile(BF16),
                        tile(BF16), tile(BF16), pltpu.VMEM((HEAD_PAD, bk), BF16)],
        compiler_params=_cparams(("arbitrary", "arbitrary"), vmem_limit=ATTN_BWD_VMEM_LIMIT),
    )(q, k, v, do, lse_row, d_row)


def _bwd_tail(dq, dk, dv, proj, pos_row, invf_col, w_heads, q_g, kv_g, x, dr, drest, wp_in):
    t = proj.shape[0]
    tm = PROJ_TILE
    n_head = 4 * LANES

    def body(dq_ref, dk_ref, dv_ref, ph_ref, pos_ref, invf_ref, wh_ref, qg_ref, kvg_ref,
             x_ref, dr_ref, drest_ref, win_ref,
             gx_ref, dwin_ref, dwh_ref, dqg_ref, dkvg_ref):
        @pl.when(pl.program_id(0) == 0)
        def _():
            dwin_ref[...] = jnp.zeros_like(dwin_ref)
            dwh_ref[...] = jnp.zeros_like(dwh_ref)
            dqg_ref[...] = jnp.zeros_like(dqg_ref)
            dkvg_ref[...] = jnp.zeros_like(dkvg_ref)

        xb = x_ref[...].astype(BF16)
        dr_b = drest_ref[...]
        dwin_ref[:, n_head:] += _dot_tn(xb, dr_b)
        gx_rest = DN_ALPHA * dr_ref[...] + _dot_nt(dr_b, win_ref[:, n_head:])

        cos, s1, s2 = _rope_tables(pos_ref[...], invf_ref[...])
        lane = lax.broadcasted_iota(jnp.int32, (tm, LANES), 1)
        c_q = ph_ref[:, :Q_LORA]
        c_kv = ph_ref[:, Q_LORA:Q_LORA + KV_LORA]
        rstd_q = lax.rsqrt(jnp.mean(c_q * c_q, axis=-1, keepdims=True) + EPS)
        rstd_kv = lax.rsqrt(jnp.mean(c_kv * c_kv, axis=-1, keepdims=True) + EPS)
        qhat = c_q * rstd_q
        kvhat = c_kv * rstd_kv
        cqn = (qhat * qg_ref[...]).astype(BF16)
        ckvn = (kvhat * kvg_ref[...]).astype(BF16)
        dkr_rot = jnp.zeros((tm, LANES), F32)
        dq_heads, dkv_heads = [], []
        for h in range(HEADS):
            dq_heads.append(_rope(jnp.transpose(dq_ref[h]) * ATTN_SCALE, cos, s1, s2, -1.0).astype(BF16))
            dk_h = dk_ref[h]
            dkv_heads.append(jnp.where(lane < NOPE, dk_h, dv_ref[h]).astype(BF16))
            dkr_rot = dkr_rot + dk_h
        dq_all = jnp.concatenate(dq_heads, axis=1)
        dkv_all = jnp.concatenate(dkv_heads, axis=1)
        dwq_all = _dot_tn(cqn, dq_all)
        dwkv_all = _dot_tn(ckvn, dkv_all)
        for h in range(HEADS):
            dwh_ref[h, :Q_LORA, :] += dwq_all[:, h * HEAD_PAD:(h + 1) * HEAD_PAD]
            dwh_ref[h, Q_LORA:, :] += dwkv_all[:, h * HEAD_PAD:(h + 1) * HEAD_PAD]
        dcqn = _dot_nt(dq_all, jnp.concatenate([wh_ref[h, :Q_LORA, :] for h in range(HEADS)], axis=1))
        dckvn = _dot_nt(dkv_all, jnp.concatenate([wh_ref[h, Q_LORA:, :] for h in range(HEADS)], axis=1))
        rot_lanes = (lane >= KR_LO) & (lane < KR_LO + ROPE)
        dkr_raw = jnp.where(rot_lanes, _rope(dkr_rot, cos, s1, s2, -1.0), 0.0)
        dqg_ref[...] += jnp.sum(dcqn * qhat, axis=0, keepdims=True)
        dkvg_ref[...] += jnp.sum(dckvn * kvhat, axis=0, keepdims=True)
        dqh = dcqn * qg_ref[...]
        dkvh = dckvn * kvg_ref[...]
        dc_q = rstd_q * (dqh - qhat * jnp.mean(dqh * qhat, axis=-1, keepdims=True))
        dc_kv = rstd_kv * (dkvh - kvhat * jnp.mean(dkvh * kvhat, axis=-1, keepdims=True))
        dh_b = jnp.concatenate([dc_q, dc_kv, dkr_raw], axis=-1).astype(BF16)
        dwin_ref[:, :n_head] += _dot_tn(xb, dh_b)
        gx_ref[...] = gx_rest + _dot_nt(dh_b, win_ref[:, :n_head])

    full = lambda a: pl.BlockSpec(a.shape, lambda i: (0,) * a.ndim)
    tile = lambda w: pl.BlockSpec((tm, w), lambda i: (i, 0))
    heads = pl.BlockSpec((HEADS, tm, HEAD_PAD), lambda i: (0, i, 0))
    acc = lambda shape: (pl.BlockSpec(shape, lambda i: (0,) * len(shape)), jax.ShapeDtypeStruct(shape, F32))
    accs = [acc(wp_in.shape), acc(w_heads.shape), acc((1, Q_LORA)), acc((1, KV_LORA))]
    return pl.pallas_call(
        body, name="bwd_tail", grid=(t // tm,),
        in_specs=[pl.BlockSpec((HEADS, HEAD_PAD, tm), lambda i: (0, 0, i)), heads, heads, tile(n_head),
                  pl.BlockSpec((1, tm), lambda i: (0, i)), full(invf_col), full(w_heads), full(q_g), full(kv_g),
                  tile(D_MODEL), tile(D_MODEL), tile(drest.shape[1]), full(wp_in)],
        out_specs=[tile(D_MODEL)] + [a[0] for a in accs],
        out_shape=[jax.ShapeDtypeStruct((t, D_MODEL), F32)] + [a[1] for a in accs],
        compiler_params=_cparams(("arbitrary",), vmem_limit=BWD_TAIL_VMEM_LIMIT),
    )(dq, dk, dv, proj, pos_row, invf_col, w_heads, q_g, kv_g, x, dr, drest, wp_in)


def _adam_update(g, w, m, v):
    m_new = ADAM_B1 * m + (1.0 - ADAM_B1) * g
    v_new = ADAM_B2 * v + (1.0 - ADAM_B2) * (g * g)
    m_hat = m_new / (1.0 - ADAM_B1 ** ADAM_STEP)
    v_hat = v_new / (1.0 - ADAM_B2 ** ADAM_STEP)
    return -ADAM_LR * (m_hat / (jnp.sqrt(v_hat) + ADAM_EPS) + ADAM_WD * w), m_new, v_new


def _adam(parts, w, m, v, *, name, tile_rows, transposed=False):
    n, rows, cols = parts.shape
    lane_pad = -(-cols // LANES) * LANES
    own_rows = rows if transposed else w.shape[0]
    assert own_rows == rows or tile_rows == rows

    def body(p_ref, w_ref, m_ref, v_ref, g_ref, d_ref, nm_ref, nv_ref, *scratch):
        g = p_ref[0].astype(F32)
        for s in range(1, n):
            g = g + p_ref[s].astype(F32)
        if transposed:
            wide_ref, = scratch
            wide_ref[:, lane_pad - LANES:] = jnp.zeros((tile_rows, LANES), F32)
            wide_ref[:, :cols] = g
            g = jnp.transpose(wide_ref[...])[:cols]
        g_ref[...] = g
        d_ref[...], nm_ref[...], nv_ref[...] = _adam_update(g[:w_ref.shape[0]], w_ref[...], m_ref[...], v_ref[...])

    if transposed:
        flat = grad = pl.BlockSpec((cols, tile_rows), lambda i: (0, i))
        shape = grad_shape = jax.ShapeDtypeStruct((cols, rows), F32)
        scratch = [pltpu.VMEM((tile_rows, lane_pad), F32)]
    else:
        own_tile = min(tile_rows, own_rows)
        flat = pl.BlockSpec((own_tile, cols), lambda i: (i, 0))
        grad = pl.BlockSpec((tile_rows, cols), lambda i: (i, 0))
        shape, grad_shape = jax.ShapeDtypeStruct((own_rows, cols), F32), jax.ShapeDtypeStruct((rows, cols), F32)
        scratch = []
    return pl.pallas_call(
        body, name=name, grid=(rows // tile_rows,),
        in_specs=[pl.BlockSpec((n, tile_rows, cols), lambda i: (0, i, 0)), flat, flat, flat],
        out_specs=[grad, flat, flat, flat], out_shape=[grad_shape, shape, shape, shape], scratch_shapes=scratch,
        compiler_params=_cparams(("arbitrary",)),
    )(parts, w, m, v)


def _adam_replicated(rep_g, ws, ms, vs):
    count = len(ws)
    small_rows = REP_ROWS - CHUNK

    def body(g_ref, *refs):
        w_refs, m_refs, v_refs = refs[:count], refs[count:2 * count], refs[2 * count:3 * count]
        outs, last_ref, slab_ref = refs[3 * count:7 * count], refs[7 * count], refs[7 * count + 1]
        for d in range(N_DEV):
            slab_ref[d * small_rows:(d + 1) * small_rows, :] = g_ref[d, CHUNK:, :]
        last_ref[...] = slab_ref[N_DEV * small_rows - 1:, LANES - 1:]
        at = 0
        for k, w_ref in enumerate(w_refs):
            if w_ref.ndim == 3:
                g = g_ref[:, :CHUNK, :]
            else:
                n_rows = w_ref.size // LANES
                g = slab_ref[at:at + n_rows, :].reshape(w_ref.shape)
                at += n_rows
            delta, m_new, v_new = _adam_update(g, w_ref[...], m_refs[k][...], v_refs[k][...])
            for which, val in enumerate((g, delta, m_new, v_new)):
                outs[which * count + k][...] = val

    shapes = [jax.ShapeDtypeStruct(w.shape, F32) for w in ws]
    res = pl.pallas_call(
        body, name="adam_rep", out_shape=shapes * 4 + [jax.ShapeDtypeStruct((1, 1), F32)],
        scratch_shapes=[pltpu.VMEM((N_DEV * small_rows, LANES), F32)],
        compiler_params=_cparams(),
    )(rep_g, *ws, *ms, *vs)
    return [res[which * count:(which + 1) * count] for which in range(4)], res[4 * count]


def _pack_small(vals, last):
    flat = jnp.concatenate([v.reshape(-1) for v in vals])
    pad = SMALL_LEN - flat.shape[0]
    return jnp.concatenate([flat, jnp.zeros((pad - 1,), F32), last.reshape(1)])


UQ_SHARD = HEADS * (NOPE + ROPE) // N_DEV
HEAD_ROWS = Q_LORA + KV_LORA
MIXED_ROWS = HEAD_ROWS + CHUNK + SMALL_LEN // N_DEV // LANES


def _head_slab(w_uq_shard, w_ukv_shard):
    return jnp.concatenate([jnp.pad(w_uq_shard, ((0, 0), (0, LANES - UQ_SHARD))), w_ukv_shard])


IN_SHARD = D_IN // N_DEV


def _w_in_pieces():
    split = Q_LORA + KV_LORA
    moves = ((0, split, 0), (split, split + ROPE, KR_LO), (split + ROPE, D_IN, LANES - ROPE))
    pieces = []
    for s in range(N_DEV):
        lo, hi = s * IN_SHARD, (s + 1) * IN_SHARD
        for a, b, shift in moves:
            a, b = max(a, lo), min(b, hi)
            if a < b:
                pieces.append((s, a - lo, a + shift, b - a))
    return pieces


def _w_in_shards(dwp_in):
    tr = TOKEN_TILE
    by_shard = [[p for p in _w_in_pieces() if p[0] == s] for s in range(N_DEV)]

    def body(w_ref, o_ref):
        for s, pieces in enumerate(by_shard):
            parts = [w_ref[:, dst:dst + width] for _, _, dst, width in pieces]
            o_ref[s] = parts[0] if len(parts) == 1 else jnp.concatenate(parts, axis=1)

    return pl.pallas_call(
        body, name="w_in_split", grid=(D_MODEL // tr,),
        in_specs=[pl.BlockSpec((tr, D_IN_PAD), lambda i: (i, 0))],
        out_specs=pl.BlockSpec((N_DEV, tr, IN_SHARD), lambda i: (0, i, 0)),
        out_shape=jax.ShapeDtypeStruct((N_DEV, D_MODEL, IN_SHARD), dwp_in.dtype),
        compiler_params=_cparams(("arbitrary",)),
    )(dwp_in)


def kernel(x, positions, w_in, q_norm_g, w_uq, kv_norm_g, w_ukv, sgu_norm_g, sgu_norm_b, w_spatial, b_spatial, w_out, ln_g, ln_b, loss_target, m_w_in, m_q_norm_g, m_w_uq, m_kv_norm_g, m_w_ukv, m_sgu_norm_g, m_sgu_norm_b, m_w_spatial, m_b_spatial, m_w_out, m_ln_g, m_ln_b, v_w_in, v_q_norm_g, v_w_uq, v_kv_norm_g, v_w_ukv, v_sgu_norm_g, v_sgu_norm_b, v_w_spatial, v_b_spatial, v_w_out, v_ln_g, v_ln_b):
    seq = x.shape[1]
    x2 = x.reshape(seq, D_MODEL)
    tgt2 = loss_target.reshape(seq, D_MODEL)
    pos_row = positions.reshape(1, seq)

    w_in_shards, w_out_shards, w_heads = _gather_two_level(
        [w_in.astype(BF16), w_out.astype(BF16), _head_slab(w_uq, w_ukv).astype(BF16)],
        name="wgather")
    (loss_part, grad_x, d_in, d_heads, d_out, d_ws, d_bs_t, d_lng, d_lnb, d_sgug, d_sgub, d_qg, d_kvg) = _local_step(
        x2, tgt2, pos_row, w_in_shards, w_heads, w_out_shards.reshape(D_MODEL, D_MODEL), q_norm_g, kv_norm_g,
        sgu_norm_g, sgu_norm_b, w_spatial, b_spatial, ln_g, ln_b)

    small_part = _pack_small([d_qg, d_kvg, d_sgug, d_sgub, d_bs_t[:, :HEADS].T, d_lng, d_lnb], last=loss_part[0, :1])
    mixed = jnp.concatenate([d_heads, d_ws, small_part.reshape(N_DEV, -1, LANES)], axis=1)
    by_chip = [g.reshape((N_CHIPS, 2) + g.shape[1:])
               for g in (d_in, d_out.reshape(N_DEV, D_MODEL // N_DEV, D_MODEL), mixed)]
    from_sibling = _sibling_swap(by_chip, name="gswap")
    core = lax.axis_index("c").astype(jnp.int32).reshape(1)
    pair_sums = [_pair_sum(a, b, core, name=nm, out_dtype=dt) for a, b, nm, dt in zip(
        by_chip, from_sibling, ("gsum_in", "gsum_out", "gsum_mixed"), (BF16, BF16, F32))]
    recv_in, recv_out, recv_mixed = _chip_exchange(pair_sums, name="gexch")

    res_in = [a.T for a in _adam(recv_in, w_in.T, m_w_in.T, v_w_in.T, name="adam_in", tile_rows=PROJ_TILE,
                                 transposed=True)]
    res_out = _adam(recv_out, w_out, m_w_out, v_w_out, name="adam_out", tile_rows=D_MODEL // N_DEV)
    res_mixed = _adam(recv_mixed, _head_slab(w_uq, w_ukv), _head_slab(m_w_uq, m_w_ukv), _head_slab(v_w_uq, v_w_ukv),
                      name="adam_mixed", tile_rows=MIXED_ROWS)

    rep_g, = _gather_direct([res_mixed[0]], name="sgather", first_row=HEAD_ROWS)
    res_rep, loss = _adam_replicated(
        rep_g,
        [q_norm_g, kv_norm_g, sgu_norm_g, sgu_norm_b, w_spatial, b_spatial, ln_g, ln_b],
        [m_q_norm_g, m_kv_norm_g, m_sgu_norm_g, m_sgu_norm_b, m_w_spatial, m_b_spatial, m_ln_g, m_ln_b],
        [v_q_norm_g, v_kv_norm_g, v_sgu_norm_g, v_sgu_norm_b, v_w_spatial, v_b_spatial, v_ln_g, v_ln_b])

    def ordered(which):
        r_qg, r_kvg, r_sg, r_sb, r_ws, r_bs, r_lg, r_lb = res_rep[which]
        heads = res_mixed[which]
        return [res_in[which], r_qg, heads[:Q_LORA, :UQ_SHARD], r_kvg, heads[Q_LORA:HEAD_ROWS], r_sg, r_sb, r_ws, r_bs,
                res_out[which], r_lg, r_lb]

    outs = [loss.reshape(()), grad_x.reshape(x.shape)]
    for which in range(4):
        outs += ordered(which)
    return tuple(outs)


def _local_step(x2, tgt2, pos_row, w_in_shards, w_heads, w_out_full, q_norm_g, kv_norm_g, sgu_norm_g, sgu_norm_b,
                w_spatial, b_spatial, ln_g, ln_b):
    half = jnp.arange(HALF, dtype=F32)
    invf_col = (1.0 / (ROPE_THETA ** (half / HALF))).reshape(HALF, 1)
    tri = jnp.tril(jnp.ones((CHUNK, CHUNK), dtype=bool))
    ws_low = jnp.where(tri[None], w_spatial, 0.0).astype(BF16)
    ws_low_t = ws_low.transpose(0, 2, 1)
    bsp = jnp.repeat(b_spatial.T, G_HEAD_DIM, axis=1)
    row = lambda a: a.reshape(1, -1)

    proj, q, k, v, vt, wp_in = _fwd_proj(x2, pos_row, invf_col, w_in_shards, w_heads, row(q_norm_g), row(kv_norm_g))
    o, lse_row = _attn_fwd(q, k, vt)
    (dr, do, d_row, drest, d_out, d_ws, d_bs_t, d_lng, d_lnb, d_sgug, d_sgub, loss_part) = _mid(
        x2, tgt2, proj, o, w_out_full, ws_low, ws_low_t, bsp, row(sgu_norm_g), row(sgu_norm_b), row(ln_g), row(ln_b))
    dqt, dk, dv = _attn_bwd(q, k, v, do, lse_row, d_row)
    grad_x, dwp_in, d_heads, d_qg, d_kvg = _bwd_tail(dqt, dk, dv, proj, pos_row, invf_col, w_heads, row(q_norm_g),
                                                      row(kv_norm_g), x2, dr, drest, wp_in)
    return (loss_part, grad_x, _w_in_shards(dwp_in), d_heads, d_out, d_ws, d_bs_t, d_lng, d_lnb, d_sgug, d_sgub,
            d_qg, d_kvg)
```
